```python
import math
import jax, jax.numpy as jnp
from jax import lax
import numpy as np

D_MODEL = 1024
BATCH = 8
SEQ = 8192
DEPTH = 4

CHUNK = 64
EPS = 1e-6
GDN_HEAD_DIM = 128
GDN_HEADS = D_MODEL // GDN_HEAD_DIM
GDN_W = GDN_HEADS * GDN_HEAD_DIM
GDN_CONV = 4
ATT_HEAD_DIM = 128
ATT_HEADS = D_MODEL // ATT_HEAD_DIM
ATT_W = ATT_HEADS * ATT_HEAD_DIM
PAST_CHUNKS = 8
BAND = (PAST_CHUNKS + 1) * CHUNK
MAX_REL_DIST = 256
D_FF = 4 * D_MODEL
IN_SPLITS = (3 * GDN_W, 4 * GDN_W, 4 * GDN_W + GDN_HEADS, 4 * GDN_W + 2 * GDN_HEADS,
             4 * GDN_W + 2 * GDN_HEADS + 3 * ATT_W)
IN_WIDTH = 4 * GDN_W + 2 * GDN_HEADS + 3 * ATT_W + 2 * D_MODEL

kernel_name = "hybrid_gdn_chunkattn_adaln_encoder"


def rms_norm(x, gain):
    xf = x.astype(jnp.float32)
    y = xf * lax.rsqrt(jnp.mean(xf * xf, axis=-1, keepdims=True) + EPS)
    return (y * gain.astype(jnp.float32)).astype(x.dtype)


def l2_normalize(x):
    xf = x.astype(jnp.float32)
    return xf * lax.rsqrt(jnp.sum(xf * xf, axis=-1, keepdims=True) + EPS)


def causal_depthwise_conv(x, w):
    k_len, ch = w.shape
    return lax.conv_general_dilated(
        x, w[:, None, :], window_strides=(1,), padding=[(k_len - 1, 0)],
        dimension_numbers=('NWC', 'WIO', 'NWC'), feature_group_count=ch)


def chunk_gated_delta_rule(q, k, v, g, beta):
    B, S, H, DK = q.shape
    DV = v.shape[-1]
    N = S // CHUNK
    f32 = jnp.float32

    def to_chunks(t):
        t = t.astype(f32).reshape((B, N, CHUNK, H) + t.shape[3:])
        return jnp.moveaxis(t, 3, 1)

    q = to_chunks(q) * (DK ** -0.5)
    k = to_chunks(k)
    v = to_chunks(v)
    g = to_chunks(g)
    beta = to_chunks(beta)
    G = jnp.cumsum(g, axis=-1)
    pos = jnp.arange(CHUNK)
    incl = pos[:, None] >= pos[None, :]
    strict = pos[:, None] > pos[None, :]
    decay = jnp.exp(jnp.where(incl, G[..., :, None] - G[..., None, :], -jnp.inf))
    kk = jnp.einsum('bhnid,bhnjd->bhnij', k * beta[..., None], k)
    a_mat = jnp.where(strict, kk * decay, 0.0) + jnp.eye(CHUNK, dtype=f32)
    rhs = jnp.concatenate([v * beta[..., None], k * (beta * jnp.exp(G))[..., None]], axis=-1)
    sol = lax.linalg.triangular_solve(a_mat, rhs, left_side=True, lower=True, unit_diagonal=True)
    u, w = sol[..., :DV], sol[..., DV:]
    attn = jnp.einsum('bhnid,bhnjd->bhnij', q, k) * decay
    q_dec = q * jnp.exp(G)[..., None]
    k_dec = k * jnp.exp(G[..., -1:] - G)[..., None]
    chunk_decay = jnp.exp(G[..., -1])
    xs = tuple(jnp.moveaxis(t, 2, 0) for t in (q_dec, k_dec, u, w, attn, chunk_decay))

    def step(state, inp):
        qd, kd, uu, ww, at, cd = inp
        v_new = uu - jnp.einsum('bhck,bhkv->bhcv', ww, state)
        o = jnp.einsum('bhck,bhkv->bhcv', qd, state) + jnp.einsum('bhij,bhjv->bhiv', at, v_new)
        state = state * cd[..., None, None] + jnp.einsum('bhck,bhcv->bhkv', kd, v_new)
        return state, o

    s0 = jnp.zeros((B, H, DK, DV), f32)
    _, o = lax.scan(step, s0, xs)
    return jnp.transpose(o, (1, 0, 3, 2, 4)).reshape(B, S, H, DV)


def gated_deltanet_branch(qkv, gate, a, b, conv_w, a_log, dt_bias, o_norm):
    B, S, _ = qkv.shape
    qkv = jax.nn.silu(causal_depthwise_conv(qkv, conv_w))
    q, k, v = jnp.split(qkv, 3, axis=-1)
    q = l2_normalize(q.reshape(B, S, GDN_HEADS, GDN_HEAD_DIM))
    k = l2_normalize(k.reshape(B, S, GDN_HEADS, GDN_HEAD_DIM))
    v = v.reshape(B, S, GDN_HEADS, GDN_HEAD_DIM)
    beta = jax.nn.sigmoid(b.astype(jnp.float32))
    g = -jnp.exp(a_log.astype(jnp.float32)) * jax.nn.softplus(
        a.astype(jnp.float32) + dt_bias.astype(jnp.float32))
    o = chunk_gated_delta_rule(q, k, v, g, beta)
    o = rms_norm(o, o_norm) * jax.nn.silu(
        gate.reshape(B, S, GDN_HEADS, GDN_HEAD_DIM).astype(jnp.float32))
    return o.reshape(B, S, GDN_W).astype(qkv.dtype)


def chunk_band_attention(q, k, v, rel_bias):
    B, S, _ = q.shape
    N = S // CHUNK
    dtype = q.dtype
    q = q.reshape(B, N, CHUNK, ATT_HEADS, ATT_HEAD_DIM).transpose(1, 0, 2, 3, 4) * (ATT_HEAD_DIM ** -0.5)
    pad = ((0, 0), (PAST_CHUNKS * CHUNK, 0), (0, 0), (0, 0))
    k = jnp.pad(k.reshape(B, S, ATT_HEADS, ATT_HEAD_DIM), pad)
    v = jnp.pad(v.reshape(B, S, ATT_HEADS, ATT_HEAD_DIM), pad)
    qi = jnp.arange(CHUNK)[:, None]
    kj = jnp.arange(BAND)[None, :]
    rel = jnp.clip(qi - kj + PAST_CHUNKS * CHUNK, -MAX_REL_DIST, MAX_REL_DIST) + MAX_REL_DIST
    bias = rel_bias[:, rel].astype(jnp.float32)

    def one_chunk(args):
        n, qn = args
        kb = lax.dynamic_slice_in_dim(k, n * CHUNK, BAND, axis=1)
        vb = lax.dynamic_slice_in_dim(v, n * CHUNK, BAND, axis=1)
        s = jnp.einsum('bqhd,bkhd->bhqk', qn, kb).astype(jnp.float32) + bias
        valid = kj >= (PAST_CHUNKS - n) * CHUNK
        s = jnp.where(valid, s, -jnp.inf)
        p = jax.nn.softmax(s, axis=-1).astype(dtype)
        return jnp.einsum('bhqk,bkhd->bqhd', p, vb)

    o = lax.map(one_chunk, (jnp.arange(N), q))
    return o.transpose(1, 0, 2, 3, 4).reshape(B, S, ATT_W)


def _fwd_setup_inputs(seed: int = 0) -> dict:
    key = jax.random.key(seed)
    ks = jax.random.split(key, 16)
    f32 = jnp.float32
    L = DEPTH

    def nrm(k, shape, s):
        return jax.random.normal(k, shape, f32) * s

    x = nrm(ks[0], (BATCH, SEQ, D_MODEL), 1.0)
    c = nrm(ks[1], (BATCH, D_MODEL), 1.0)
    w_ada = nrm(ks[2], (L, D_MODEL, 6 * D_MODEL), 0.5 * D_MODEL ** -0.5)
    b_ada = nrm(ks[3], (L, 6 * D_MODEL), 0.02)
    norm_mix = 1.0 + nrm(ks[4], (L, D_MODEL), 0.02)
    norm_mlp = 1.0 + nrm(ks[5], (L, D_MODEL), 0.02)
    w_in = nrm(ks[6], (L, D_MODEL, IN_WIDTH), D_MODEL ** -0.5)
    conv_w = nrm(ks[7], (L, GDN_CONV, 3 * GDN_W), GDN_CONV ** -0.5)
    a_log = jnp.log(jax.random.uniform(ks[8], (L, GDN_HEADS), f32, 1.0, 16.0))
    dt = jnp.exp(jax.random.uniform(ks[9], (L, GDN_HEADS), f32, math.log(1e-3), math.log(1e-1)))
    dt_bias = dt + jnp.log(-jnp.expm1(-dt))
    gdn_norm = 1.0 + nrm(ks[10], (L, GDN_HEAD_DIM), 0.02)
    rel_bias = nrm(ks[11], (L, ATT_HEADS, 2 * MAX_REL_DIST + 1), 0.2)
    w_out = nrm(ks[12], (L, D_MODEL, D_MODEL), D_MODEL ** -0.5)
    w_ff_in = nrm(ks[13], (L, D_MODEL, D_FF), D_MODEL ** -0.5)
    w_ff_out = nrm(ks[14], (L, D_FF, D_MODEL), D_FF ** -0.5)
    final_norm = 1.0 + nrm(ks[15], (D_MODEL,), 0.02)
    return {"x": x, "c": c, "w_ada": w_ada, "b_ada": b_ada, "norm_mix": norm_mix,
            "norm_mlp": norm_mlp, "w_in": w_in, "conv_w": conv_w, "a_log": a_log,
            "dt_bias": dt_bias, "gdn_norm": gdn_norm, "rel_bias": rel_bias, "w_out": w_out,
            "w_ff_in": w_ff_in, "w_ff_out": w_ff_out, "final_norm": final_norm}


def _fwd_reference(x, c, w_ada, b_ada, norm_mix, norm_mlp, w_in, conv_w, a_log, dt_bias,
              gdn_norm, rel_bias, w_out, w_ff_in, w_ff_out, final_norm):
    c_act = jax.nn.silu(c)
    for l in range(DEPTH):
        mod = (c_act @ w_ada[l] + b_ada[l])[:, None, :]
        sh1, sc1, gt1, sh2, sc2, gt2 = jnp.split(mod, 6, axis=-1)
        h = rms_norm(x, norm_mix[l]) * (1.0 + sc1) + sh1
        z = h @ w_in[l]
        gdn_qkv, gdn_gate, gdn_a, gdn_b, att_qkv, br_gates = jnp.split(z, IN_SPLITS, axis=-1)
        o_a = gated_deltanet_branch(gdn_qkv, gdn_gate, gdn_a, gdn_b, conv_w[l],
                                    a_log[l], dt_bias[l], gdn_norm[l])
        att_q, att_k, att_v = jnp.split(att_qkv, 3, axis=-1)
        o_b = chunk_band_attention(att_q, att_k, att_v, rel_bias[l])
        g_a, g_b = jnp.split(jax.nn.sigmoid(br_gates), 2, axis=-1)
        x = x + gt1 * ((g_a * o_a + g_b * o_b) @ w_out[l])
        h = rms_norm(x, norm_mlp[l]) * (1.0 + sc2) + sh2
        x = x + gt2 * (jnp.square(jax.nn.relu(h @ w_ff_in[l])) @ w_ff_out[l])
    return rms_norm(x, final_norm)


import jax as _jax
import jax.numpy as _jnp

TWIN_FORMAT = 'train_step'
FWD_PARAMS = ['x', 'c', 'w_ada', 'b_ada', 'norm_mix', 'norm_mlp', 'w_in', 'conv_w', 'a_log', 'dt_bias', 'gdn_norm', 'rel_bias', 'w_out', 'w_ff_in', 'w_ff_out', 'final_norm']
TWIN_WEIGHTS = ['w_ada', 'b_ada', 'norm_mix', 'norm_mlp', 'w_in', 'conv_w', 'a_log', 'dt_bias', 'gdn_norm', 'rel_bias', 'w_out', 'w_ff_in', 'w_ff_out', 'final_norm']
TWIN_DIFF_INPUT = 'x'
TWIN_INPUTS = ['x', 'c', 'w_ada', 'b_ada', 'norm_mix', 'norm_mlp', 'w_in', 'conv_w', 'a_log', 'dt_bias', 'gdn_norm', 'rel_bias', 'w_out', 'w_ff_in', 'w_ff_out', 'final_norm', 'loss_target', 'm_w_ada', 'm_b_ada', 'm_norm_mix', 'm_norm_mlp', 'm_w_in', 'm_conv_w', 'm_a_log', 'm_dt_bias', 'm_gdn_norm', 'm_rel_bias', 'm_w_out', 'm_w_ff_in', 'm_w_ff_out', 'm_final_norm', 'v_w_ada', 'v_b_ada', 'v_norm_mix', 'v_norm_mlp', 'v_w_in', 'v_conv_w', 'v_a_log', 'v_dt_bias', 'v_gdn_norm', 'v_rel_bias', 'v_w_out', 'v_w_ff_in', 'v_w_ff_out', 'v_final_norm']
TWIN_OUTPUTS = ['loss', 'grad_x', 'grad_w_ada', 'grad_b_ada', 'grad_norm_mix', 'grad_norm_mlp', 'grad_w_in', 'grad_conv_w', 'grad_a_log', 'grad_dt_bias', 'grad_gdn_norm', 'grad_rel_bias', 'grad_w_out', 'grad_w_ff_in', 'grad_w_ff_out', 'grad_final_norm', 'delta_w_ada', 'delta_b_ada', 'delta_norm_mix', 'delta_norm_mlp', 'delta_w_in', 'delta_conv_w', 'delta_a_log', 'delta_dt_bias', 'delta_gdn_norm', 'delta_rel_bias', 'delta_w_out', 'delta_w_ff_in', 'delta_w_ff_out', 'delta_final_norm', 'new_m_w_ada', 'new_m_b_ada', 'new_m_norm_mix', 'new_m_norm_mlp', 'new_m_w_in', 'new_m_conv_w', 'new_m_a_log', 'new_m_dt_bias', 'new_m_gdn_norm', 'new_m_rel_bias', 'new_m_w_out', 'new_m_w_ff_in', 'new_m_w_ff_out', 'new_m_final_norm', 'new_v_w_ada', 'new_v_b_ada', 'new_v_norm_mix', 'new_v_norm_mlp', 'new_v_w_in', 'new_v_conv_w', 'new_v_a_log', 'new_v_dt_bias', 'new_v_gdn_norm', 'new_v_rel_bias', 'new_v_w_out', 'new_v_w_ff_in', 'new_v_w_ff_out', 'new_v_final_norm']
TWIN_LEAF_KINDS = {'loss': 'loss', 'grad_x': 'grad_x', 'grad_w_ada': 'grad_w', 'grad_b_ada': 'grad_w', 'grad_norm_mix': 'grad_w', 'grad_norm_mlp': 'grad_w', 'grad_w_in': 'grad_w', 'grad_conv_w': 'grad_w', 'grad_a_log': 'grad_w', 'grad_dt_bias': 'grad_w', 'grad_gdn_norm': 'grad_w', 'grad_rel_bias': 'grad_w', 'grad_w_out': 'grad_w', 'grad_w_ff_in': 'grad_w', 'grad_w_ff_out': 'grad_w', 'grad_final_norm': 'grad_w', 'delta_w_ada': 'delta_w', 'delta_b_ada': 'delta_w', 'delta_norm_mix': 'delta_w', 'delta_norm_mlp': 'delta_w', 'delta_w_in': 'delta_w', 'delta_conv_w': 'delta_w', 'delta_a_log': 'delta_w', 'delta_dt_bias': 'delta_w', 'delta_gdn_norm': 'delta_w', 'delta_rel_bias': 'delta_w', 'delta_w_out': 'delta_w', 'delta_w_ff_in': 'delta_w', 'delta_w_ff_out': 'delta_w', 'delta_final_norm': 'delta_w', 'new_m_w_ada': 'new_m', 'new_m_b_ada': 'new_m', 'new_m_norm_mix': 'new_m', 'new_m_norm_mlp': 'new_m', 'new_m_w_in': 'new_m', 'new_m_conv_w': 'new_m', 'new_m_a_log': 'new_m', 'new_m_dt_bias': 'new_m', 'new_m_gdn_norm': 'new_m', 'new_m_rel_bias': 'new_m', 'new_m_w_out': 'new_m', 'new_m_w_ff_in': 'new_m', 'new_m_w_ff_out': 'new_m', 'new_m_final_norm': 'new_m', 'new_v_w_ada': 'new_v', 'new_v_b_ada': 'new_v', 'new_v_norm_mix': 'new_v', 'new_v_norm_mlp': 'new_v', 'new_v_w_in': 'new_v', 'new_v_conv_w': 'new_v', 'new_v_a_log': 'new_v', 'new_v_dt_bias': 'new_v', 'new_v_gdn_norm': 'new_v', 'new_v_rel_bias': 'new_v', 'new_v_w_out': 'new_v', 'new_v_w_ff_in': 'new_v', 'new_v_w_ff_out': 'new_v', 'new_v_final_norm': 'new_v'}


def _forward(args):
    return _fwd_reference(*[args[k] for k in FWD_PARAMS])


def _output_shape():
    def fwd():
        inp = _fwd_setup_inputs(0)
        return _fwd_reference(*[inp[k] for k in FWD_PARAMS])
    out = _jax.eval_shape(fwd)
    return out.shape, out.dtype

N_MICROBATCH = 1
ADAM_LR = 0.001
ADAM_B1 = 0.9
ADAM_B2 = 0.999
ADAM_EPS = 1e-08
ADAM_WD = 0.01
ADAM_STEP = 10
PER_EXAMPLE_BATCH_AXIS = {'x': 0, 'c': 0, 'loss_target': 0}
SHARED_INPUTS = []
_WEIGHT_DTYPES = {'w_ada': _jnp.float32, 'b_ada': _jnp.float32, 'norm_mix': _jnp.float32, 'norm_mlp': _jnp.float32, 'w_in': _jnp.float32, 'conv_w': _jnp.float32, 'a_log': _jnp.float32, 'dt_bias': _jnp.float32, 'gdn_norm': _jnp.float32, 'rel_bias': _jnp.float32, 'w_out': _jnp.float32, 'w_ff_in': _jnp.float32, 'w_ff_out': _jnp.float32, 'final_norm': _jnp.float32}
MOMENT_SCALE = {'w_ada': 1.059752e-01, 'b_ada': 1.954568e-01, 'norm_mix': 4.421918e-02, 'norm_mlp': 1.038166e-01, 'w_in': 1.581082e-02, 'conv_w': 1.945348e-02, 'a_log': 8.743543e-02, 'dt_bias': 8.701448e-02, 'gdn_norm': 6.874461e-02, 'rel_bias': 3.018882e-03, 'w_out': 2.959921e-02, 'w_ff_in': 5.495850e-02, 'w_ff_out': 1.038863e-01, 'final_norm': 6.459628e+01}


def _to_microbatches(a, axis):
    t = _jnp.moveaxis(a, axis, 0)
    t = t.reshape((N_MICROBATCH, t.shape[0] // N_MICROBATCH) + t.shape[1:])
    return _jnp.moveaxis(t, 1, axis + 1)


def setup_inputs(seed: int = 0) -> dict:
    inp = _fwd_setup_inputs(seed)
    key = _jax.random.fold_in(_jax.random.key(seed), 7919)
    shape, _ = _output_shape()
    out = dict(inp)
    out["loss_target"] = _jax.random.normal(_jax.random.fold_in(key, 0), shape, _jnp.float32)
    for i, name in enumerate(TWIN_WEIGHTS):
        w = inp[name].astype(_jnp.float32)
        if MOMENT_SCALE is None:
            s = _jnp.sqrt(_jnp.mean(_jnp.square(w)) + 1e-30)
        else:
            s = MOMENT_SCALE[name]
        km, kv = _jax.random.split(_jax.random.fold_in(key, i + 1))
        out[name] = w
        out["m_" + name] = s * _jax.random.normal(km, w.shape, _jnp.float32)
        out["v_" + name] = (s * s) * _jax.random.uniform(kv, w.shape, _jnp.float32, 0.5, 1.5)
    if N_MICROBATCH > 1:
        for name, axis in PER_EXAMPLE_BATCH_AXIS.items():
            out[name] = _to_microbatches(out[name], axis)
    return {'x': out['x'], 'c': out['c'], 'w_ada': out['w_ada'], 'b_ada': out['b_ada'], 'norm_mix': out['norm_mix'], 'norm_mlp': out['norm_mlp'], 'w_in': out['w_in'], 'conv_w': out['conv_w'], 'a_log': out['a_log'], 'dt_bias': out['dt_bias'], 'gdn_norm': out['gdn_norm'], 'rel_bias': out['rel_bias'], 'w_out': out['w_out'], 'w_ff_in': out['w_ff_in'], 'w_ff_out': out['w_ff_out'], 'final_norm': out['final_norm'], 'loss_target': out['loss_target'], 'm_w_ada': out['m_w_ada'], 'm_b_ada': out['m_b_ada'], 'm_norm_mix': out['m_norm_mix'], 'm_norm_mlp': out['m_norm_mlp'], 'm_w_in': out['m_w_in'], 'm_conv_w': out['m_conv_w'], 'm_a_log': out['m_a_log'], 'm_dt_bias': out['m_dt_bias'], 'm_gdn_norm': out['m_gdn_norm'], 'm_rel_bias': out['m_rel_bias'], 'm_w_out': out['m_w_out'], 'm_w_ff_in': out['m_w_ff_in'], 'm_w_ff_out': out['m_w_ff_out'], 'm_final_norm': out['m_final_norm'], 'v_w_ada': out['v_w_ada'], 'v_b_ada': out['v_b_ada'], 'v_norm_mix': out['v_norm_mix'], 'v_norm_mlp': out['v_norm_mlp'], 'v_w_in': out['v_w_in'], 'v_conv_w': out['v_conv_w'], 'v_a_log': out['v_a_log'], 'v_dt_bias': out['v_dt_bias'], 'v_gdn_norm': out['v_gdn_norm'], 'v_rel_bias': out['v_rel_bias'], 'v_w_out': out['v_w_out'], 'v_w_ff_in': out['v_w_ff_in'], 'v_w_ff_out': out['v_w_ff_out'], 'v_final_norm': out['v_final_norm']}


def _loss(weights, diff, rest, loss_target):
    with _jax.named_scope("forward"):
        args = {**rest, TWIN_DIFF_INPUT: diff, **{k: w.astype(_WEIGHT_DTYPES[k]) for k, w in weights.items()}}
        y = _forward(args)
    with _jax.named_scope("loss_head"):
        err = _jnp.square(y.astype(_jnp.float32) - loss_target)
        return 0.5 * _jnp.sum(_jnp.mean(err, axis=-1)) if err.ndim else 0.5 * err


def _adamw(w, g, m, v):
    m = ADAM_B1 * m + (1.0 - ADAM_B1) * g
    v = ADAM_B2 * v + (1.0 - ADAM_B2) * _jnp.square(g)
    m_hat = m / (1.0 - ADAM_B1 ** ADAM_STEP)
    v_hat = v / (1.0 - ADAM_B2 ** ADAM_STEP)
    delta = -ADAM_LR * (m_hat / (_jnp.sqrt(v_hat) + ADAM_EPS) + ADAM_WD * w)
    return delta, m, v


def reference(x, c, w_ada, b_ada, norm_mix, norm_mlp, w_in, conv_w, a_log, dt_bias, gdn_norm, rel_bias, w_out, w_ff_in, w_ff_out, final_norm, loss_target, m_w_ada, m_b_ada, m_norm_mix, m_norm_mlp, m_w_in, m_conv_w, m_a_log, m_dt_bias, m_gdn_norm, m_rel_bias, m_w_out, m_w_ff_in, m_w_ff_out, m_final_norm, v_w_ada, v_b_ada, v_norm_mix, v_norm_mlp, v_w_in, v_conv_w, v_a_log, v_dt_bias, v_gdn_norm, v_rel_bias, v_w_out, v_w_ff_in, v_w_ff_out, v_final_norm):
    given = dict(x=x, c=c, w_ada=w_ada, b_ada=b_ada, norm_mix=norm_mix, norm_mlp=norm_mlp, w_in=w_in, conv_w=conv_w, a_log=a_log, dt_bias=dt_bias, gdn_norm=gdn_norm, rel_bias=rel_bias, w_out=w_out, w_ff_in=w_ff_in, w_ff_out=w_ff_out, final_norm=final_norm, loss_target=loss_target, m_w_ada=m_w_ada, m_b_ada=m_b_ada, m_norm_mix=m_norm_mix, m_norm_mlp=m_norm_mlp, m_w_in=m_w_in, m_conv_w=m_conv_w, m_a_log=m_a_log, m_dt_bias=m_dt_bias, m_gdn_norm=m_gdn_norm, m_rel_bias=m_rel_bias, m_w_out=m_w_out, m_w_ff_in=m_w_ff_in, m_w_ff_out=m_w_ff_out, m_final_norm=m_final_norm, v_w_ada=v_w_ada, v_b_ada=v_b_ada, v_norm_mix=v_norm_mix, v_norm_mlp=v_norm_mlp, v_w_in=v_w_in, v_conv_w=v_conv_w, v_a_log=v_a_log, v_dt_bias=v_dt_bias, v_gdn_norm=v_gdn_norm, v_rel_bias=v_rel_bias, v_w_out=v_w_out, v_w_ff_in=v_w_ff_in, v_w_ff_out=v_w_ff_out, v_final_norm=v_final_norm)
    weights = {n: given[n] for n in TWIN_WEIGHTS}
    shared = {n: given[n] for n in SHARED_INPUTS}
    per_example = {n: given[n] for n in ['x', 'c']}
    grad_fn = _jax.value_and_grad(_loss, argnums=(0, 1))

    def one_microbatch(ex, loss_target):
        ex = dict(ex)
        diff = ex.pop(TWIN_DIFF_INPUT)
        return grad_fn(weights, diff, {**shared, **ex}, loss_target)

    if N_MICROBATCH == 1:
        loss, (grad_w, grad_x) = one_microbatch(per_example, given["loss_target"])
    else:
        def body(carry, xs):
            loss_sum, grad_sum = carry
            l_k, (gw_k, gx_k) = one_microbatch(xs[0], xs[1])
            with _jax.named_scope("update"):
                return (loss_sum + l_k, _jax.tree.map(_jnp.add, grad_sum, gw_k)), gx_k

        init = (_jnp.zeros((), _jnp.float32), _jax.tree.map(_jnp.zeros_like, weights))
        (loss, grad_w), grad_x = _jax.lax.scan(body, init, (per_example, given["loss_target"]))
    with _jax.named_scope("update"):
        delta_w, new_m, new_v = {}, {}, {}
        for n in TWIN_WEIGHTS:
            delta_w[n], new_m[n], new_v[n] = _adamw(weights[n], grad_w[n], given["m_" + n], given["v_" + n])
    return (loss, grad_x, *[grad_w[n] for n in TWIN_WEIGHTS], *[delta_w[n] for n in TWIN_WEIGHTS],
            *[new_m[n] for n in TWIN_WEIGHTS], *[new_v[n] for n in TWIN_WEIGHTS])
```

```python
import functools
import math

import numpy as np
import jax
import jax.numpy as jnp
from jax import lax
from jax.experimental import pallas as pl
from jax.experimental.pallas import tpu as pltpu

F32 = jnp.float32
MXU = jnp.bfloat16
HI = lax.Precision.HIGHEST
MESH_ID = pl.DeviceIdType.MESH

D = 1024
NH = 8
HD = 128
CH = 64
PAST = 8
DFF = 4096
EPS = 1e-6
NDEV = 8
DEPTH = 4
IN_W = 9232
ZW = 9728
Z_GATE, Z_ATT, Z_BR, Z_AB = 3072, 4096, 7168, 9216
QB = 256
KWIN = 768
FW = 1024
ADAM_LR, ADAM_B1, ADAM_B2, ADAM_EPS, ADAM_WD, ADAM_STEP = 0.001, 0.9, 0.999, 1e-08, 0.01, 10


def _dot(a, b, prec=None):
    return jnp.dot(a, b, preferred_element_type=F32, precision=prec)


def _dot_nt(a, b, prec=None):
    return lax.dot_general(a, b, (((1,), (1,)), ((), ())), preferred_element_type=F32, precision=prec)


def _dot_tn(a, b, prec=None):
    return lax.dot_general(a, b, (((0,), (0,)), ((), ())), preferred_element_type=F32, precision=prec)


def _mx(a):
    return a.astype(MXU)


def _sigmoid(x):
    return 1.0 / (1.0 + jnp.exp(-x))


def _softplus(x):
    return jnp.maximum(x, 0.0) + jnp.log(1.0 + jnp.exp(-jnp.abs(x)))


def _rowsum(x):
    return jnp.sum(x, axis=1, keepdims=True)


def _colsum(x):
    return jnp.sum(x, axis=0, keepdims=True)


def _call(body, name, grid, in_specs, out_specs, out_shape, scratch=(), **params):
    cp = pltpu.CompilerParams(**params) if params else None
    kw = dict(compiler_params=cp) if cp is not None else {}
    return pl.pallas_call(body, name=name, grid=grid, in_specs=in_specs, out_specs=out_specs,
                          out_shape=out_shape, scratch_shapes=list(scratch), **kw)


def _full(shape):
    n = len(shape)
    return pl.BlockSpec(shape, lambda *_: (0,) * n)


def _mesh_pos():
    return lax.axis_index("x"), lax.axis_index("y"), lax.axis_index("c")


def _peer(pos, k):
    x, y, c = pos
    return (x ^ ((k >> 2) & 1), y ^ ((k >> 1) & 1), c ^ (k & 1))


def _flat(pos):
    return 4 * pos[0] + 2 * pos[1] + pos[2]


def _all_gather(x, name):
    def body(x_ref, out_ref, send_sems, recv_sems, local_sem):
        pos = _mesh_pos()
        me = _flat(pos)
        mine = pltpu.make_async_copy(x_ref, out_ref.at[me], local_sem)
        mine.start()
        sends = []
        for k in range(1, NDEV):
            cp = pltpu.make_async_remote_copy(
                src_ref=x_ref, dst_ref=out_ref.at[me], send_sem=send_sems.at[k - 1],
                recv_sem=recv_sems.at[k - 1], device_id=_peer(pos, k), device_id_type=MESH_ID)
            cp.start()
            sends.append(cp)
        for k in range(1, NDEV):
            src = _flat(_peer(pos, k))
            pltpu.make_async_remote_copy(
                src_ref=x_ref, dst_ref=out_ref.at[src], send_sem=send_sems.at[k - 1],
                recv_sem=recv_sems.at[k - 1], device_id=_peer(pos, k), device_id_type=MESH_ID).wait_recv()
        for cp in sends:
            cp.wait_send()
        mine.wait()

    return pl.pallas_call(
        body, name=name,
        out_shape=jax.ShapeDtypeStruct((NDEV,) + x.shape, x.dtype),
        in_specs=[pl.BlockSpec(memory_space=pl.ANY)],
        out_specs=pl.BlockSpec(memory_space=pl.ANY),
        scratch_shapes=[pltpu.SemaphoreType.DMA((NDEV - 1,)), pltpu.SemaphoreType.DMA((NDEV - 1,)),
                        pltpu.SemaphoreType.DMA],
    )(x)


def _all_to_all(x, name):
    def body(x_ref, out_ref, send_sems, recv_sems, local_sem):
        pos = _mesh_pos()
        me = _flat(pos)
        mine = pltpu.make_async_copy(x_ref.at[me], out_ref.at[me], local_sem)
        mine.start()
        sends = []
        for k in range(1, NDEV):
            dst = _flat(_peer(pos, k))
            cp = pltpu.make_async_remote_copy(
                src_ref=x_ref.at[dst], dst_ref=out_ref.at[me], send_sem=send_sems.at[k - 1],
                recv_sem=recv_sems.at[k - 1], device_id=_peer(pos, k), device_id_type=MESH_ID)
            cp.start()
            sends.append(cp)
        for k in range(1, NDEV):
            src = _flat(_peer(pos, k))
            pltpu.make_async_remote_copy(
                src_ref=x_ref.at[src], dst_ref=out_ref.at[src], send_sem=send_sems.at[k - 1],
                recv_sem=recv_sems.at[k - 1], device_id=_peer(pos, k), device_id_type=MESH_ID).wait_recv()
        for cp in sends:
            cp.wait_send()
        mine.wait()

    return pl.pallas_call(
        body, name=name,
        out_shape=jax.ShapeDtypeStruct(x.shape, x.dtype),
        in_specs=[pl.BlockSpec(memory_space=pl.ANY)],
        out_specs=pl.BlockSpec(memory_space=pl.ANY),
        scratch_shapes=[pltpu.SemaphoreType.DMA((NDEV - 1,)), pltpu.SemaphoreType.DMA((NDEV - 1,)),
                        pltpu.SemaphoreType.DMA],
    )(x)


def _mm_nn(a, w, name, *, mode="plain", res=None, gate=None, tm=512, tn=512, tk=1024):
    M, K = a.shape
    N = w.shape[1]
    tk = min(tk, K)
    tn = min(tn, N)
    nk = K // tk

    def body(*refs):
        if mode == "resid":
            a_ref, w_ref, res_ref, gate_ref, o_ref, acc = refs
        elif mode == "relu2":
            a_ref, w_ref, o_ref, r_ref, acc = refs
        else:
            a_ref, w_ref, o_ref, acc = refs
        k = pl.program_id(2)

        @pl.when(k == 0)
        def _():
            acc[...] = jnp.zeros_like(acc)

        acc[...] += _dot(_mx(a_ref[...]), w_ref[...])

        @pl.when(k == nk - 1)
        def _():
            r = acc[...]
            if mode == "resid":
                o_ref[...] = res_ref[...] + gate_ref[...] * r
            elif mode == "relu2":
                o_ref[...] = r
                r_ref[...] = jnp.square(jnp.maximum(r, 0.0)).astype(r_ref.dtype)
            else:
                o_ref[...] = r

    in_specs = [pl.BlockSpec((tm, tk), lambda i, j, k: (i, k)),
                pl.BlockSpec((tk, tn), lambda i, j, k: (k, j))]
    args = [a, w]
    o_spec = pl.BlockSpec((tm, tn), lambda i, j, k: (i, j))
    out_specs, out_shape = o_spec, jax.ShapeDtypeStruct((M, N), F32)
    if mode == "resid":
        in_specs += [o_spec, pl.BlockSpec((1, tn), lambda i, j, k: (0, j))]
        args += [res, gate]
    elif mode == "relu2":
        out_specs = [o_spec, o_spec]
        out_shape = [jax.ShapeDtypeStruct((M, N), F32), jax.ShapeDtypeStruct((M, N), MXU)]
    return _call(body, name, (M // tm, N // tn, nk), in_specs, out_specs, out_shape,
                 [pltpu.VMEM((tm, tn), F32)])(*args)


def _mm_nt(a, w, name, *, gate=None, drelu=None, tm=512, tko=512, tn=1024):
    M, N = a.shape
    K = w.shape[0]
    tn = min(tn, N)
    tko = min(tko, K)
    nn = N // tn

    def body(*refs):
        refs = list(refs)
        a_ref, w_ref = refs[:2]
        rest = refs[2:]
        gate_ref = rest.pop(0) if gate is not None else None
        pre_ref = rest.pop(0) if drelu is not None else None
        o_ref, acc = rest
        n = pl.program_id(2)

        @pl.when(n == 0)
        def _():
            acc[...] = jnp.zeros_like(acc)

        av = a_ref[...]
        if gate_ref is not None:
            av = av * gate_ref[...]
        acc[...] += _dot_nt(_mx(av), w_ref[...])

        @pl.when(n == nn - 1)
        def _():
            r = acc[...]
            if pre_ref is not None:
                r = r * (2.0 * jnp.maximum(pre_ref[...], 0.0))
            o_ref[...] = r.astype(o_ref.dtype)

    in_specs = [pl.BlockSpec((tm, tn), lambda i, j, n: (i, n)),
                pl.BlockSpec((tko, tn), lambda i, j, n: (j, n))]
    args = [a, w]
    if gate is not None:
        in_specs.append(pl.BlockSpec((1, tn), lambda i, j, n: (0, n)))
        args.append(gate)
    o_spec = pl.BlockSpec((tm, tko), lambda i, j, n: (i, j))
    if drelu is not None:
        in_specs.append(o_spec)
        args.append(drelu)
    out_dtype = MXU if drelu is not None else F32
    return _call(body, name, (M // tm, K // tko, nn), in_specs, o_spec,
                 jax.ShapeDtypeStruct((M, K), out_dtype), [pltpu.VMEM((tm, tko), F32)])(*args)


def _mm_tn(a, b, name, *, gate=None, w=None, tk=512, tn=512, tm=512):
    M, K = a.shape
    N = b.shape[1]
    tk = min(tk, K)
    tn = min(tn, N)
    nm = M // tm
    gated = gate is not None

    def body(*refs):
        if gated:
            a_ref, b_ref, gate_ref, w_ref, o_ref, dg_ref, acc = refs
        else:
            a_ref, b_ref, o_ref, acc = refs
        kk = pl.program_id(1)
        m = pl.program_id(2)

        @pl.when(m == 0)
        def _():
            acc[...] = jnp.zeros_like(acc)

        acc[...] += _dot_tn(_mx(a_ref[...]), _mx(b_ref[...]))

        if gated:
            @pl.when((m == 0) & (kk == 0))
            def _():
                dg_ref[...] = jnp.zeros_like(dg_ref)

        @pl.when(m == nm - 1)
        def _():
            r = acc[...]
            if gated:
                o_ref[...] = r * gate_ref[...]
                dg_ref[...] += _colsum(r * w_ref[...].astype(F32))
            else:
                o_ref[...] = r

    in_specs = [pl.BlockSpec((tm, tk), lambda j, k, m: (m, k)),
                pl.BlockSpec((tm, tn), lambda j, k, m: (m, j))]
    args = [a, b]
    o_spec = pl.BlockSpec((tk, tn), lambda j, k, m: (k, j))
    out_specs, out_shape = o_spec, jax.ShapeDtypeStruct((K, N), F32)
    if gated:
        in_specs += [pl.BlockSpec((1, tn), lambda j, k, m: (0, j)), o_spec]
        args += [gate, w]
        out_specs = [o_spec, pl.BlockSpec((1, tn), lambda j, k, m: (0, j))]
        out_shape = [out_shape, jax.ShapeDtypeStruct((1, N), F32)]
    return _call(body, name, (N // tn, K // tk, nm), in_specs, out_specs, out_shape,
                 [pltpu.VMEM((tk, tn), F32)])(*args)


def _modnorm_fwd(x, gain, sc, sh, name, ts=512):
    S = x.shape[0]

    def body(x_ref, g_ref, sc_ref, sh_ref, h_ref):
        xv = x_ref[...]
        r = lax.rsqrt(jnp.mean(xv * xv, axis=1, keepdims=True) + EPS)
        h_ref[...] = ((xv * r * g_ref[...]) * (1.0 + sc_ref[...]) + sh_ref[...]).astype(h_ref.dtype)

    row = pl.BlockSpec((ts, D), lambda i: (i, 0))
    vec = pl.BlockSpec((1, D), lambda i: (0, 0))
    return _call(body, name, (S // ts,), [row, vec, vec, vec], row,
                 jax.ShapeDtypeStruct((S, D), MXU))(x, gain, sc, sh)


def _modnorm_bwd(dh, x, gain, sc, sh, dx_in, name, ts=512):
    S = x.shape[0]

    def body(dh_ref, x_ref, g_ref, sc_ref, sh_ref, dxin_ref, dx_ref, dsc_ref, dsh_ref, dg_ref):
        i = pl.program_id(0)
        xv = x_ref[...]
        dhv = dh_ref[...]
        g = g_ref[...]
        r = lax.rsqrt(jnp.mean(xv * xv, axis=1, keepdims=True) + EPS)
        xr = xv * r
        dn = dhv * (1.0 + sc_ref[...])
        u = dn * g
        dx_ref[...] = dxin_ref[...] + r * (u - xr * jnp.mean(xr * u, axis=1, keepdims=True))

        @pl.when(i == 0)
        def _():
            dsc_ref[...] = jnp.zeros_like(dsc_ref)
            dsh_ref[...] = jnp.zeros_like(dsh_ref)
            dg_ref[...] = jnp.zeros_like(dg_ref)

        dsc_ref[...] += _colsum(dhv * (xr * g))
        dsh_ref[...] += _colsum(dhv)
        dg_ref[...] += _colsum(dn * xr)

    row = pl.BlockSpec((ts, D), lambda i: (i, 0))
    vec = pl.BlockSpec((1, D), lambda i: (0, 0))
    vshape = jax.ShapeDtypeStruct((1, D), F32)
    return _call(body, name, (S // ts,), [row, row, vec, vec, vec, row], [row, vec, vec, vec],
                 [jax.ShapeDtypeStruct((S, D), F32), vshape, vshape, vshape])(dh, x, gain, sc, sh, dx_in)


def _loss_head(x, target, gain, name, ts=512):
    S = x.shape[0]

    def body(x_ref, t_ref, g_ref, loss_ref, dx_ref, dg_ref):
        i = pl.program_id(0)
        xv = x_ref[...]
        g = g_ref[...]
        r = lax.rsqrt(jnp.mean(xv * xv, axis=1, keepdims=True) + EPS)
        xr = xv * r
        e = xr * g - t_ref[...]
        dy = e * (1.0 / D)
        u = dy * g
        dx_ref[...] = r * (u - xr * jnp.mean(xr * u, axis=1, keepdims=True))

        @pl.when(i == 0)
        def _():
            loss_ref[...] = jnp.zeros_like(loss_ref)
            dg_ref[...] = jnp.zeros_like(dg_ref)

        part = 0.5 * jnp.sum(jnp.mean(e * e, axis=1, keepdims=True), axis=0, keepdims=True)
        loss_ref[...] += jnp.broadcast_to(part, loss_ref.shape)
        dg_ref[...] += _colsum(dy * xr)

    row = pl.BlockSpec((ts, D), lambda i: (i, 0))
    vec = pl.BlockSpec((1, D), lambda i: (0, 0))
    return _call(body, name, (S // ts,), [row, row, vec],
                 [pl.BlockSpec((1, 128), lambda i: (0, 0)), row, vec],
                 [jax.ShapeDtypeStruct((1, 128), F32), jax.ShapeDtypeStruct((S, D), F32),
                  jax.ShapeDtypeStruct((1, D), F32)])(x, target, gain)


def _merge_specs(ts):
    o_spec = pl.BlockSpec((NH, ts, HD), lambda i: (0, i, 0))
    zg = pl.BlockSpec((ts, D), lambda i: (i, Z_GATE // D))
    za = pl.BlockSpec((ts, D), lambda i: (i, Z_BR // D))
    zb = pl.BlockSpec((ts, D), lambda i: (i, Z_BR // D + 1))
    row = pl.BlockSpec((ts, D), lambda i: (i, 0))
    gn = pl.BlockSpec((1, HD), lambda i: (0, 0))
    return o_spec, zg, za, zb, row, gn


def _merge_fwd(o, z, ob, gn, name, ts=256):
    S = ob.shape[0]

    def body(o_ref, zg_ref, za_ref, zb_ref, ob_ref, gn_ref, m_ref):
        for h in range(NH):
            sl = slice(h * HD, (h + 1) * HD)
            oh = o_ref[h]
            r = lax.rsqrt(jnp.mean(oh * oh, axis=1, keepdims=True) + EPS)
            gate = zg_ref[:, sl]
            oa = (oh * r * gn_ref[...]) * (gate * _sigmoid(gate))
            m = _sigmoid(za_ref[:, sl]) * oa + _sigmoid(zb_ref[:, sl]) * ob_ref[:, sl]
            m_ref[:, sl] = m.astype(m_ref.dtype)

    o_spec, zg, za, zb, row, gns = _merge_specs(ts)
    return _call(body, name, (S // ts,), [o_spec, zg, za, zb, row, gns], row,
                 jax.ShapeDtypeStruct((S, D), MXU))(o, z, z, z, ob, gn)


def _merge_bwd(dm, o, z, ob, gn, name, ts=256):
    S = ob.shape[0]

    def body(dm_ref, o_ref, zg_ref, za_ref, zb_ref, ob_ref, gn_ref,
             do_ref, dzg_ref, dob_ref, dza_ref, dzb_ref, dgn_ref):
        i = pl.program_id(0)
        gn_v = gn_ref[...]
        dgn = jnp.zeros((1, HD), F32)
        for h in range(NH):
            sl = slice(h * HD, (h + 1) * HD)
            dmh = dm_ref[:, sl]
            oh = o_ref[h]
            r = lax.rsqrt(jnp.mean(oh * oh, axis=1, keepdims=True) + EPS)
            ohr = oh * r
            on = ohr * gn_v
            gate = zg_ref[:, sl]
            sg = _sigmoid(gate)
            silu = gate * sg
            oa = on * silu
            ga = _sigmoid(za_ref[:, sl])
            gb = _sigmoid(zb_ref[:, sl])
            obh = ob_ref[:, sl]
            doa = dmh * ga
            dob_ref[:, sl] = dmh * gb
            dza_ref[:, sl] = (dmh * oa * ga * (1.0 - ga)).astype(dza_ref.dtype)
            dzb_ref[:, sl] = (dmh * obh * gb * (1.0 - gb)).astype(dzb_ref.dtype)
            don = doa * silu
            dzg_ref[:, sl] = (doa * on * (sg * (1.0 + gate * (1.0 - sg)))).astype(dzg_ref.dtype)
            dgn = dgn + _colsum(don * ohr)
            u = don * gn_v
            do_ref[h] = r * (u - ohr * jnp.mean(ohr * u, axis=1, keepdims=True))

        @pl.when(i == 0)
        def _():
            dgn_ref[...] = jnp.zeros_like(dgn_ref)

        dgn_ref[...] += dgn

    o_spec, zg, za, zb, row, gns = _merge_specs(ts)
    return _call(
        body, name, (S // ts,), [row, o_spec, zg, za, zb, row, gns],
        [o_spec, row, row, row, row, gns],
        [jax.ShapeDtypeStruct((NH, S, HD), F32), jax.ShapeDtypeStruct((S, D), MXU),
         jax.ShapeDtypeStruct((S, D), F32), jax.ShapeDtypeStruct((S, D), MXU),
         jax.ShapeDtypeStruct((S, D), MXU), jax.ShapeDtypeStruct((1, HD), F32)],
    )(dm, o, z, z, z, ob, gn)


GROWS = 256
GCH = GROWS // CH


def _gdn_prep_fwd(z, conv_w, alog_row, dtb_row, name):
    S = z.shape[0]
    ts = GROWS
    scale = HD ** -0.5

    def body(z_ref, halo_ref, zab_ref, w_ref, al_ref, dt_ref, q_ref, k_ref, v_ref, gb_ref, buf):
        i = pl.program_id(0)
        buf[0:8, :] = jnp.where(i == 0, 0.0, halo_ref[...])
        buf[8:8 + ts, :] = z_ref[...]
        outs = (q_ref, k_ref, v_ref)
        for seg in range(3):
            cs = slice(seg * D, (seg + 1) * D)
            c = jnp.zeros((ts, D), F32)
            for j in range(4):
                c = c + w_ref[j:j + 1, cs] * buf[pl.ds(5 + j, ts), cs]
            s = c * _sigmoid(c)
            if seg == 2:
                outs[seg][...] = s
            else:
                mul = scale if seg == 0 else 1.0
                for h in range(NH):
                    sl = slice(h * HD, (h + 1) * HD)
                    sh = s[:, sl]
                    r = lax.rsqrt(_rowsum(sh * sh) + EPS)
                    outs[seg][:, sl] = sh * (r * mul)
        zab = zab_ref[...]
        lane = lax.broadcasted_iota(jnp.int32, zab.shape, 1)
        g = -jnp.exp(al_ref[...]) * _softplus(zab + dt_ref[...])
        gb_ref[...] = jnp.where(lane < NH, g, jnp.where(lane < 2 * NH, _sigmoid(zab), 0.0))

    row = pl.BlockSpec((ts, D), lambda i: (i, 0))
    vec = pl.BlockSpec((1, 128), lambda i: (0, 0))
    return _call(
        body, name, (S // ts,),
        [pl.BlockSpec((ts, 3 * D), lambda i: (i, 0)),
         pl.BlockSpec((8, 3 * D), lambda i: (jnp.maximum(i * (ts // 8) - 1, 0), 0)),
         pl.BlockSpec((ts, 128), lambda i: (i, Z_AB // 128)),
         _full((4, 3 * D)), vec, vec],
        [row, row, row, pl.BlockSpec((ts, 128), lambda i: (i, 0))],
        [jax.ShapeDtypeStruct((S, D), F32)] * 3 + [jax.ShapeDtypeStruct((S, 128), F32)],
        [pltpu.VMEM((ts + 8, 3 * D), F32)],
    )(z, z, z, conv_w, alog_row, dtb_row)


def _chunk_common(gbk, h, k):
    lane = lax.broadcasted_iota(jnp.int32, gbk.shape, 1)
    g_col = _rowsum(jnp.where(lane == h, gbk, 0.0))
    b_col = _rowsum(jnp.where(lane == h + NH, gbk, 0.0))
    ri = lax.broadcasted_iota(jnp.int32, (CH, CH), 0)
    ci = lax.broadcasted_iota(jnp.int32, (CH, CH), 1)
    incl = ri >= ci
    gc = _dot(incl.astype(F32), jnp.broadcast_to(g_col, (CH, CH)), HI)
    decay = jnp.where(incl, jnp.exp(jnp.where(incl, gc - gc.T, 0.0)), 0.0)
    G = gc[:, 0:1]
    Gl = G[CH - 1:CH, :]
    eG = jnp.exp(G)
    e2 = jnp.exp(Gl - G)
    cd = jnp.exp(Gl)
    kb = k * b_col
    kk = _dot_nt(_mx(kb), _mx(k))
    return dict(g=g_col, b=b_col, ri=ri, ci=ci, incl=incl, strict=ri > ci, decay=decay,
                eG=eG, e2=e2, cd=cd, kb=kb, kk=kk)


def _gdn_intra_fwd(qn, kn, v, gb, name):
    S = qn.shape[0]

    def body(q_ref, k_ref, v_ref, gb_ref, u_ref, w_ref, qd_ref, kd_ref, at_ref, t_ref, cd_ref):
        h = pl.program_id(1)
        for c in range(GCH):
            rows = slice(c * CH, (c + 1) * CH)
            q, k, vv = q_ref[rows, :], k_ref[rows, :], v_ref[rows, :]
            cm = _chunk_common(gb_ref[rows, :], h, k)
            lm = jnp.where(cm["strict"], cm["kk"] * cm["decay"], 0.0)
            eye = (cm["ri"] == cm["ci"]).astype(F32)
            t = eye - lm
            p = lm
            for _ in range(5):
                p = _dot(p, p, HI)
                t = t + _dot(t, p, HI)
            rhs = jnp.concatenate([vv * cm["b"], k * (cm["b"] * cm["eG"])], axis=1)
            sol = _dot(t, rhs, HI)
            u_ref[0, rows, :] = sol[:, :HD]
            w_ref[0, rows, :] = sol[:, HD:]
            qk = _dot_nt(_mx(q), _mx(k))
            at_ref[0, rows, :] = jnp.where(cm["incl"], qk * cm["decay"], 0.0)
            t_ref[0, rows, :] = t
            qd_ref[0, rows, :] = q * cm["eG"]
            kd_ref[0, rows, :] = k * cm["e2"]
            cd_ref[0, c] = jnp.broadcast_to(cm["cd"], (8, 128))

    tok = pl.BlockSpec((GROWS, HD), lambda i, h: (i, h))
    hm = pl.BlockSpec((1, GROWS, HD), lambda i, h: (h, i, 0))
    hm64 = pl.BlockSpec((1, GROWS, CH), lambda i, h: (h, i, 0))
    big = jax.ShapeDtypeStruct((NH, S, HD), F32)
    sm = jax.ShapeDtypeStruct((NH, S, CH), F32)
    return _call(
        body, name, (S // GROWS, NH),
        [tok, tok, tok, pl.BlockSpec((GROWS, 128), lambda i, h: (i, 0))],
        [hm, hm, hm, hm, hm64, hm64, pl.BlockSpec((1, GCH, 8, 128), lambda i, h: (h, i, 0, 0))],
        [big, big, big, big, sm, sm, jax.ShapeDtypeStruct((NH, S // CH, 8, 128), F32)],
    )(qn, kn, v, gb)


def _scale_state(s, cd_tile):
    return (s.reshape(HD // 8, 8, HD) * cd_tile[None]).reshape(HD, HD)


def _gdn_scan_fwd(u, w, qd, kd, attn, cdt, name):
    S = u.shape[1]
    nblk = S // GROWS

    def body(u_ref, w_ref, qd_ref, kd_ref, at_ref, cd_ref, o_ref, vn_ref, st_ref, s_ref):
        i = pl.program_id(0)

        @pl.when(i == 0)
        def _():
            s_ref[...] = jnp.zeros_like(s_ref)

        def chunk(c, carry):
            r0 = pl.multiple_of(c * CH, CH)
            rows = pl.ds(r0, CH)
            for h in range(NH):
                sh = s_ref[h]
                st_ref[h, c] = sh
                sb = _mx(sh)
                vn = u_ref[h, rows, :] - _dot(_mx(w_ref[h, rows, :]), sb)
                vb = _mx(vn)
                vn_ref[h, rows, :] = vn
                o_ref[h, rows, :] = _dot(_mx(qd_ref[h, rows, :]), sb) + _dot(_mx(at_ref[h, rows, :]), vb)
                s_ref[h] = _scale_state(sh, cd_ref[h, c]) + _dot_tn(_mx(kd_ref[h, rows, :]), vb)
            return carry

        lax.fori_loop(0, GCH, chunk, 0)

    hm = pl.BlockSpec((NH, GROWS, HD), lambda i: (0, i, 0))
    hm64 = pl.BlockSpec((NH, GROWS, CH), lambda i: (0, i, 0))
    big = jax.ShapeDtypeStruct((NH, S, HD), F32)
    return _call(
        body, name, (nblk,),
        [hm, hm, hm, hm, hm64, pl.BlockSpec((NH, GCH, 8, 128), lambda i: (0, i, 0, 0))],
        [hm, hm, pl.BlockSpec((NH, GCH, HD, HD), lambda i: (0, i, 0, 0))],
        [big, big, jax.ShapeDtypeStruct((NH, S // CH, HD, HD), F32)],
        [pltpu.VMEM((NH, HD, HD), F32)],
    )(u, w, qd, kd, attn, cdt)


def _gdn_scan_bwd(do, w, qd, kd, attn, cdt, vn, st, name):
    S = do.shape[1]
    nblk = S // GROWS

    def body(do_ref, w_ref, qd_ref, kd_ref, at_ref, cd_ref, vn_ref, st_ref,
             dqd_ref, dkd_ref, dvn_ref, dw_ref, dat_ref, dcd_ref, ds_ref):
        i = pl.program_id(0)

        @pl.when(i == 0)
        def _():
            ds_ref[...] = jnp.zeros_like(ds_ref)

        def chunk(cc, carry):
            c = GCH - 1 - cc
            r0 = pl.multiple_of(c * CH, CH)
            rows = pl.ds(r0, CH)
            for h in range(NH):
                dsp = ds_ref[h]
                sh = st_ref[h, c]
                dsb, sb = _mx(dsp), _mx(sh)
                dob = _mx(do_ref[h, rows, :])
                vb = _mx(vn_ref[h, rows, :])
                dvn = _dot(_mx(kd_ref[h, rows, :]), dsb) + _dot_tn(_mx(at_ref[h, rows, :]), dob)
                dvb = _mx(dvn)
                dvn_ref[h, rows, :] = dvn
                dqd_ref[h, rows, :] = _dot_nt(dob, sb)
                dat_ref[h, rows, :] = _dot_nt(dob, vb)
                dkd_ref[h, rows, :] = _dot_nt(vb, dsb)
                dw_ref[h, rows, :] = -_dot_nt(dvb, sb)
                dcd = jnp.sum(_rowsum(dsp * sh), axis=0, keepdims=True)
                dcd_ref[h, c] = jnp.broadcast_to(dcd, (8, 128))
                ds_ref[h] = (_scale_state(dsp, cd_ref[h, c]) + _dot_tn(_mx(qd_ref[h, rows, :]), dob)
                             - _dot_tn(_mx(w_ref[h, rows, :]), dvb))
            return carry

        lax.fori_loop(0, GCH, chunk, 0)

    hm = pl.BlockSpec((NH, GROWS, HD), lambda i: (0, nblk - 1 - i, 0))
    hm64 = pl.BlockSpec((NH, GROWS, CH), lambda i: (0, nblk - 1 - i, 0))
    tile = pl.BlockSpec((NH, GCH, 8, 128), lambda i: (0, nblk - 1 - i, 0, 0))
    big = jax.ShapeDtypeStruct((NH, S, HD), F32)
    return _call(
        body, name, (nblk,),
        [hm, hm, hm, hm, hm64, tile, hm, pl.BlockSpec((NH, GCH, HD, HD), lambda i: (0, nblk - 1 - i, 0, 0))],
        [hm, hm, hm, hm, hm64, tile],
        [big, big, big, big, jax.ShapeDtypeStruct((NH, S, CH), F32),
         jax.ShapeDtypeStruct((NH, S // CH, 8, 128), F32)],
        [pltpu.VMEM((NH, HD, HD), F32)],
    )(do, w, qd, kd, attn, cdt, vn, st)


def _gdn_intra_bwd(qn, kn, v, gb, u, w, tmat, dqd, dkd, du, dw, dattn, dcdt, name):
    S = qn.shape[0]

    def body(q_ref, k_ref, v_ref, gb_ref, u_ref, w_ref, t_ref, dqd_ref, dkd_ref, du_ref, dw_ref,
             dat_ref, dcd_ref, dq_ref, dk_ref, dv_ref, dgb_ref):
        h = pl.program_id(1)
        for c in range(GCH):
            rows = slice(c * CH, (c + 1) * CH)
            q, k, vv = q_ref[rows, :], k_ref[rows, :], v_ref[rows, :]
            cm = _chunk_common(gb_ref[rows, :], h, k)
            decay, eG, e2, b = cm["decay"], cm["eG"], cm["e2"], cm["b"]
            t = t_ref[0, rows, :]
            sol = jnp.concatenate([u_ref[0, rows, :], w_ref[0, rows, :]], axis=1)
            dsol = jnp.concatenate([du_ref[0, rows, :], dw_ref[0, rows, :]], axis=1)
            drhs = _dot_tn(t, dsol, HI)
            da = -_dot_nt(drhs, sol, HI)
            dru, drw = drhs[:, :HD], drhs[:, HD:]
            dv_ref[rows, :] = dru * b
            s_w = _rowsum(drw * k)
            dbeta = _rowsum(dru * vv) + s_w * eG
            deg = s_w * b
            dk = drw * (b * eG)
            dkk = jnp.where(cm["strict"], da * decay, 0.0)
            ddec = jnp.where(cm["strict"], da * cm["kk"], 0.0)
            dkkb = _mx(dkk)
            dkb = _dot(dkkb, _mx(k))
            dk = dk + _dot_tn(dkkb, _mx(cm["kb"])) + dkb * b
            dbeta = dbeta + _rowsum(dkb * k)
            dat = jnp.where(cm["incl"], dat_ref[0, rows, :], 0.0)
            qk = _dot_nt(_mx(q), _mx(k))
            dqk = _mx(dat * decay)
            ddec = ddec + dat * qk
            dqd = dqd_ref[0, rows, :]
            dkd = dkd_ref[0, rows, :]
            dq_ref[rows, :] = _dot(dqk, _mx(k)) + dqd * eG
            dk = dk + _dot_tn(dqk, _mx(q)) + dkd * e2
            dk_ref[rows, :] = dk
            deg = deg + _rowsum(dqd * q)
            t2 = _rowsum(dkd * k) * e2
            dcd = dcd_ref[0, c][0:1, 0:1]
            dgl = jnp.sum(t2, axis=0, keepdims=True) + dcd * cm["cd"]
            dd = ddec * decay
            dG = deg * eG - t2 + _rowsum(dd) - _rowsum(dd.T)
            rowi = lax.broadcasted_iota(jnp.int32, (CH, 1), 0)
            dG = dG + jnp.where(rowi == CH - 1, dgl, 0.0)
            lane = lax.broadcasted_iota(jnp.int32, (CH, 128), 1)
            rev = (cm["ci"] >= cm["ri"]).astype(F32)
            dg_blk = _dot(rev, jnp.where(lane == h, dG, 0.0), HI)
            dgb_ref[0, rows, :] = dg_blk + jnp.where(lane == h + NH, dbeta, 0.0)

    tok = pl.BlockSpec((GROWS, HD), lambda i, h: (i, h))
    hm = pl.BlockSpec((1, GROWS, HD), lambda i, h: (h, i, 0))
    hm64 = pl.BlockSpec((1, GROWS, CH), lambda i, h: (h, i, 0))
    tile = pl.BlockSpec((1, GCH, 8, 128), lambda i, h: (h, i, 0, 0))
    tokout = jax.ShapeDtypeStruct((S, D), F32)
    return _call(
        body, name, (S // GROWS, NH),
        [tok, tok, tok, pl.BlockSpec((GROWS, 128), lambda i, h: (i, 0)), hm, hm, hm64,
         hm, hm, hm, hm, hm64, tile],
        [tok, tok, tok, pl.BlockSpec((1, GROWS, 128), lambda i, h: (h, i, 0))],
        [tokout, tokout, tokout, jax.ShapeDtypeStruct((NH, S, 128), F32)],
    )(qn, kn, v, gb, u, w, tmat, dqd, dkd, du, dw, dattn, dcdt)


def _gdn_prep_bwd(z, dqn, dkn, dv, dgb, conv_w, alog_row, dtb_row, name):
    S = z.shape[0]
    ts = GROWS
    nblk = S // ts
    scale = HD ** -0.5
    tb = ts // 8

    def body(z_ref, hp_ref, hn_ref, zab_ref, dq_ref, dqn_ref, dk_ref, dkn_ref, dv_ref, dvn_ref,
             dgb_ref, w_ref, al_ref, dt_ref, dz_ref, dzab_ref, dcw_ref, dvec_ref, buf, dybuf, dcbuf):
        i = pl.program_id(0)
        last = i == nblk - 1

        @pl.when(i == 0)
        def _():
            dcw_ref[...] = jnp.zeros_like(dcw_ref)
            dvec_ref[...] = jnp.zeros_like(dvec_ref)

        buf[0:8, :] = jnp.where(i == 0, 0.0, hp_ref[...])
        buf[8:8 + ts, :] = z_ref[...]
        buf[8 + ts:16 + ts, :] = hn_ref[...]
        rowi = lax.broadcasted_iota(jnp.int32, (ts + 8, 1), 0)
        live = jnp.logical_or(rowi < ts, jnp.logical_not(last))
        dys = ((dq_ref, dqn_ref), (dk_ref, dkn_ref), (dv_ref, dvn_ref))
        for seg in range(3):
            cs = slice(seg * D, (seg + 1) * D)
            dybuf[0:ts, :] = dys[seg][0][...]
            dybuf[ts:ts + 8, :] = dys[seg][1][...]
            c = jnp.zeros((ts + 8, D), F32)
            for j in range(4):
                c = c + w_ref[j:j + 1, cs] * buf[pl.ds(5 + j, ts + 8), cs]
            sg = _sigmoid(c)
            s = c * sg
            dsilu = sg * (1.0 + c * (1.0 - sg))
            if seg == 2:
                dcbuf[...] = jnp.where(live, dybuf[...] * dsilu, 0.0)
            else:
                mul = scale if seg == 0 else 1.0
                for h in range(NH):
                    sl = slice(h * HD, (h + 1) * HD)
                    sh = s[:, sl]
                    dy = dybuf[:, sl]
                    r = lax.rsqrt(_rowsum(sh * sh) + EPS)
                    shr = sh * r
                    ds = (mul * r) * (dy - shr * _rowsum(shr * dy))
                    dcbuf[:, sl] = jnp.where(live, ds * dsilu[:, sl], 0.0)
            dx = jnp.zeros((ts, D), F32)
            for j in range(4):
                dcw_ref[j:j + 1, cs] += _colsum(dcbuf[0:ts, :] * buf[pl.ds(5 + j, ts), cs])
                dx = dx + w_ref[j:j + 1, cs] * dcbuf[pl.ds(3 - j, ts), :]
            dz_ref[:, cs] = dx.astype(dz_ref.dtype)
        dgbs = dgb_ref[0]
        for h in range(1, NH):
            dgbs = dgbs + dgb_ref[h]
        zab = zab_ref[...]
        lane = lax.broadcasted_iota(jnp.int32, zab.shape, 1)
        xx = zab + dt_ref[...]
        ea = jnp.exp(al_ref[...])
        g = -ea * _softplus(xx)
        da = dgbs * (-ea) * _sigmoid(xx)
        beta = _sigmoid(zab)
        db = dgbs * beta * (1.0 - beta)
        is_a = lane < NH
        dzab = jnp.where(is_a, da, jnp.where(lane < 2 * NH, db, 0.0))
        dzab_ref[:, 0:128] = dzab.astype(dzab_ref.dtype)
        dzab_ref[:, 128:512] = jnp.zeros((ts, 384), dzab_ref.dtype)
        dvec_ref[0:1, :] += _colsum(jnp.where(is_a, dgbs * g, 0.0))
        dvec_ref[1:2, :] += _colsum(jnp.where(is_a, da, 0.0))

    z3 = pl.BlockSpec((ts, 3 * D), lambda i: (i, 0))
    row = pl.BlockSpec((ts, D), lambda i: (i, 0))
    nxt = pl.BlockSpec((8, D), lambda i: (jnp.minimum((i + 1) * tb, S // 8 - 1), 0))
    vec = pl.BlockSpec((1, 128), lambda i: (0, 0))
    return _call(
        body, name, (nblk,),
        [z3,
         pl.BlockSpec((8, 3 * D), lambda i: (jnp.maximum(i * tb - 1, 0), 0)),
         pl.BlockSpec((8, 3 * D), lambda i: (jnp.minimum((i + 1) * tb, S // 8 - 1), 0)),
         pl.BlockSpec((ts, 128), lambda i: (i, Z_AB // 128)),
         row, nxt, row, nxt, row, nxt,
         pl.BlockSpec((NH, ts, 128), lambda i: (0, i, 0)),
         _full((4, 3 * D)), vec, vec],
        [z3, pl.BlockSpec((ts, 512), lambda i: (i, 0)), _full((8, 3 * D)), _full((8, 128))],
        [jax.ShapeDtypeStruct((S, 3 * D), MXU), jax.ShapeDtypeStruct((S, 512), MXU),
         jax.ShapeDtypeStruct((8, 3 * D), F32), jax.ShapeDtypeStruct((8, 128), F32)],
        [pltpu.VMEM((ts + 16, 3 * D), F32), pltpu.VMEM((ts + 8, D), F32), pltpu.VMEM((ts + 8, D), F32)],
    )(z, z, z, z, dqn, dqn, dkn, dkn, dv, dv, dgb, conv_w, alog_row, dtb_row)


def _bias_index():
    u = np.arange(FW)[None, :]
    s = np.arange(3)[:, None]
    return np.clip(KWIN - 1 - u - QB * s, -256, 256) + 256


def _att_window(i):
    return pl.multiple_of(jnp.maximum(i * QB - PAST * CH, 0), QB)


def _att_probs(q_ref, k_ref, f_ref, i):
    ws = _att_window(i)
    q = _mx(q_ref[...] * (HD ** -0.5))
    kw = _mx(k_ref[pl.ds(ws, KWIN), :])
    fb = jnp.broadcast_to(f_ref[0], (QB, FW))
    bias = pltpu.roll(fb, FW - 255, 1, stride=1, stride_axis=0)[:, :KWIN]
    s = _dot_nt(q, kw) + bias
    qc = (i * QB + lax.broadcasted_iota(jnp.int32, (QB, KWIN), 0)) // CH
    kc = (ws + lax.broadcasted_iota(jnp.int32, (QB, KWIN), 1)) // CH
    valid = (kc <= qc) & (kc >= qc - PAST)
    s = jnp.where(valid, s, -jnp.inf)
    p = jnp.exp(s - jnp.max(s, axis=1, keepdims=True))
    p = p / _rowsum(p)
    return q, kw, ws, p


def _att_specs(S):
    c0 = Z_ATT // HD
    q = pl.BlockSpec((QB, HD), lambda h, i: (i, c0 + h))
    k = pl.BlockSpec((S, HD), lambda h, i: (0, c0 + NH + h))
    v = pl.BlockSpec((S, HD), lambda h, i: (0, c0 + 2 * NH + h))
    f = pl.BlockSpec((1, 1, FW), lambda h, i: (jnp.maximum(2 - i, 0) * NH + h, 0, 0))
    tok = pl.BlockSpec((QB, HD), lambda h, i: (i, h))
    return q, k, v, f, tok


def _att_fwd(z, fvec, name):
    S = z.shape[0]

    def body(q_ref, k_ref, v_ref, f_ref, o_ref):
        i = pl.program_id(1)
        _, _, ws, p = _att_probs(q_ref, k_ref, f_ref, i)
        o_ref[...] = _dot(_mx(p), _mx(v_ref[pl.ds(ws, KWIN), :]))

    q, k, v, f, tok = _att_specs(S)
    return _call(body, name, (NH, S // QB), [q, k, v, f], tok,
                 jax.ShapeDtypeStruct((S, D), F32))(z, z, z, fvec)


def _att_bwd(z, fvec, ob, dob, name):
    S = z.shape[0]

    def body(q_ref, k_ref, v_ref, f_ref, o_ref, do_ref, dq_ref, dk_ref, dv_ref, db_ref):
        i = pl.program_id(1)
        q, kw, ws, p = _att_probs(q_ref, k_ref, f_ref, i)
        do = do_ref[...]
        dob16 = _mx(do)
        dp = _dot_nt(dob16, _mx(v_ref[pl.ds(ws, KWIN), :]))
        ds = p * (dp - _rowsum(do * o_ref[...]))
        dsb = _mx(ds)
        dq_ref[...] = (_dot(dsb, kw) * (HD ** -0.5)).astype(dq_ref.dtype)

        @pl.when(i == 0)
        def _():
            dk_ref[...] = jnp.zeros_like(dk_ref)
            dv_ref[...] = jnp.zeros_like(dv_ref)

        dk_ref[pl.ds(ws, KWIN), :] += _dot_tn(dsb, q)
        dv_ref[pl.ds(ws, KWIN), :] += _dot_tn(_mx(p), dob16)

        @pl.when(i <= 2)
        def _():
            db_ref[0] = ds

        @pl.when(i > 2)
        def _():
            db_ref[0] += ds

    q, k, v, f, tok = _att_specs(S)
    acc = pl.BlockSpec((S, HD), lambda h, i: (0, h))
    return _call(
        body, name, (NH, S // QB), [q, k, v, f, tok, tok],
        [tok, acc, acc, pl.BlockSpec((1, QB, KWIN), lambda h, i: (jnp.maximum(2 - i, 0) * NH + h, 0, 0))],
        [jax.ShapeDtypeStruct((S, D), MXU), jax.ShapeDtypeStruct((S, D), F32),
         jax.ShapeDtypeStruct((S, D), F32), jax.ShapeDtypeStruct((3 * NH, QB, KWIN), F32)],
    )(z, z, z, fvec, ob, dob)


def _bias_fold(dbias, onehot, name):
    def body(db_ref, e_ref, o_ref):
        j = pl.program_id(0)
        h = j % NH
        x = jnp.concatenate([db_ref[0], jnp.zeros((QB, FW - KWIN), F32)], axis=1)
        half = QB // 2
        while half >= 8:
            x = x[:half] + pltpu.roll(x[half:2 * half], FW - half, 1)
            half //= 2
        df = jnp.zeros((1, FW), F32)
        for r in range(8):
            df = df + pltpu.roll(x[r:r + 1], 255 - r, 1)
        contrib = _dot(df, e_ref[0], HI)
        rowh = lax.broadcasted_iota(jnp.int32, (NH, 640), 0)

        @pl.when(j == 0)
        def _():
            o_ref[...] = jnp.zeros_like(o_ref)

        o_ref[...] += jnp.where(rowh == h, contrib, 0.0)

    return _call(
        body, name, (3 * NH,),
        [pl.BlockSpec((1, QB, KWIN), lambda j: (j, 0, 0)),
         pl.BlockSpec((1, FW, 640), lambda j: (j // NH, 0, 0))],
        _full((NH, 640)), jax.ShapeDtypeStruct((NH, 640), F32),
    )(dbias, onehot)


ADA_SHARD = 6 * D // NDEV


def _ada_mod(c_all, w_ada, b_shard, name):
    def body(c_ref, w_ref, b_ref, o_ref):
        cv = c_ref[...]
        ca = cv * _sigmoid(cv)
        o_ref[0] = _dot(_mx(ca), _mx(w_ref[0])) + b_ref[0]

    return _call(
        body, name, (DEPTH,),
        [_full((NDEV, D)), pl.BlockSpec((1, D, ADA_SHARD), lambda l: (l, 0, 0)),
         pl.BlockSpec((1, 1, ADA_SHARD), lambda l: (l, 0, 0))],
        pl.BlockSpec((1, NDEV, ADA_SHARD), lambda l: (l, 0, 0)),
        jax.ShapeDtypeStruct((DEPTH, NDEV, ADA_SHARD), F32),
    )(c_all, w_ada, b_shard.reshape(DEPTH, 1, ADA_SHARD))


def _adam(g, w, m, v):
    m = ADAM_B1 * m + (1.0 - ADAM_B1) * g
    v = ADAM_B2 * v + (1.0 - ADAM_B2) * jnp.square(g)
    m_hat = m / (1.0 - ADAM_B1 ** ADAM_STEP)
    v_hat = v / (1.0 - ADAM_B2 ** ADAM_STEP)
    delta = -ADAM_LR * (m_hat / (jnp.sqrt(v_hat) + ADAM_EPS) + ADAM_WD * w)
    return delta, m, v


def _wada_adamw(c_all_t, dmod, w, m, v, name):
    def body(c_ref, d_ref, w_ref, m_ref, v_ref, g_ref, dl_ref, mo_ref, vo_ref):
        cv = c_ref[...]
        ca = cv * _sigmoid(cv)
        g = _dot(ca, d_ref[0], HI)
        g_ref[0] = g
        dl_ref[0], mo_ref[0], vo_ref[0] = _adam(g, w_ref[0], m_ref[0], v_ref[0])

    blk = pl.BlockSpec((1, D, ADA_SHARD), lambda l: (l, 0, 0))
    shp = jax.ShapeDtypeStruct((DEPTH, D, ADA_SHARD), F32)
    return _call(
        body, name, (DEPTH,),
        [_full((D, NDEV)), pl.BlockSpec((1, NDEV, ADA_SHARD), lambda l: (l, 0, 0)), blk, blk, blk],
        [blk] * 4, [shp] * 4,
    )(c_all_t, dmod, w, m, v)


def _adamw_reduce(parts, w, m, v, name, tr):
    P, R, C = parts.shape

    def body(p_ref, w_ref, m_ref, v_ref, g_ref, dl_ref, mo_ref, vo_ref):
        g = p_ref[0]
        for k in range(1, P):
            g = g + p_ref[k]
        g_ref[...] = g
        dl_ref[...], mo_ref[...], vo_ref[...] = _adam(g, w_ref[...], m_ref[...], v_ref[...])

    blk = pl.BlockSpec((tr, C), lambda i: (i, 0))
    shp = jax.ShapeDtypeStruct((R, C), F32)
    return _call(body, name, (R // tr,), [pl.BlockSpec((P, tr, C), lambda i: (0, i, 0)), blk, blk, blk],
                 [blk] * 4, [shp] * 4)(parts, w, m, v)


def _sum_parts(parts, name):
    P, R, C = parts.shape

    def body(p_ref, o_ref):
        g = p_ref[0]
        for k in range(1, P):
            g = g + p_ref[k]
        o_ref[...] = g

    return _call(body, name, (1,), [_full((P, R, C))], _full((R, C)),
                 jax.ShapeDtypeStruct((R, C), F32))(parts)


def _pack_rows(vecs, width=1024):
    flat = jnp.concatenate([a.reshape(-1) for a in vecs])
    n = flat.shape[0]
    rows = -(-n // width)
    rows = -(-rows // 8) * 8
    return jnp.pad(flat, (0, rows * width - n)).reshape(rows, width)


def _unpack_rows(packed, shapes):
    flat = packed.reshape(-1)
    out, off = [], 0
    for s in shapes:
        n = int(np.prod(s)) if len(s) else 1
        out.append(flat[off:off + n].reshape(s))
        off += n
    return out


def _forward_layer(x, mod_l, p):
    sh1, sc1, gt1, sh2, sc2, gt2 = [mod_l[k][None] for k in range(6)]
    h = _modnorm_fwd(x, p["norm_mix"], sc1, sh1, "norm_mix_fwd")
    z = _mm_nn(h, p["wz"], "in_proj")
    qn, kn, v, gb = _gdn_prep_fwd(z, p["conv_w"], p["alog"], p["dtb"], "gdn_prep_fwd")
    u, w, qd, kd, attn, tmat, cdt = _gdn_intra_fwd(qn, kn, v, gb, "gdn_intra_fwd")
    o, vn, st = _gdn_scan_fwd(u, w, qd, kd, attn, cdt, "gdn_scan_fwd")
    ob = _att_fwd(z, p["fvec"], "att_fwd")
    m = _merge_fwd(o, z, ob, p["gdn_norm"], "merge_fwd")
    x1 = _mm_nn(m, p["wout"], "out_proj", mode="resid", res=x, gate=gt1)
    h2 = _modnorm_fwd(x1, p["norm_mlp"], sc2, sh2, "norm_mlp_fwd")
    a, r = _mm_nn(h2, p["w1"], "ff_in", mode="relu2")
    x2 = _mm_nn(r, p["w2"], "ff_out", mode="resid", res=x1, gate=gt2)
    saved = dict(x=x, h=h, z=z, qn=qn, kn=kn, v=v, gb=gb, u=u, w=w, qd=qd, kd=kd, attn=attn,
                 tmat=tmat, cdt=cdt, o=o, vn=vn, st=st, ob=ob, m=m, x1=x1, h2=h2, a=a, r=r)
    return x2, saved


def _backward_layer(dx2, mod_l, p, s, onehot):
    sh1, sc1, gt1, sh2, sc2, gt2 = [mod_l[k][None] for k in range(6)]
    dw2, dgt2 = _mm_tn(s["r"], dx2, "ff_out_dw", gate=gt2, w=p["w2"])
    da = _mm_nt(dx2, p["w2"], "ff_out_dx", gate=gt2, drelu=s["a"])
    dw1 = _mm_tn(s["h2"], da, "ff_in_dw")
    dh2 = _mm_nt(da, p["w1"], "ff_in_dx")
    dx1, dsc2, dsh2, dnmlp = _modnorm_bwd(dh2, s["x1"], p["norm_mlp"], sc2, sh2, dx2, "norm_mlp_bwd")
    dwout, dgt1 = _mm_tn(s["m"], dx1, "out_proj_dw", gate=gt1, w=p["wout"])
    dm = _mm_nt(dx1, p["wout"], "out_proj_dx", gate=gt1)
    do, dzg, dob, dza, dzb, dgn = _merge_bwd(dm, s["o"], s["z"], s["ob"], p["gdn_norm"], "merge_bwd")
    dq_att, dk_att, dv_att, dbias = _att_bwd(s["z"], p["fvec"], s["ob"], dob, "att_bwd")
    drb = _bias_fold(dbias, onehot, "rel_bias_fold")[:, :513]
    dqd, dkd, dvn, dw, dattn, dcdt = _gdn_scan_bwd(do, s["w"], s["qd"], s["kd"], s["attn"], s["cdt"],
                                                   s["vn"], s["st"], "gdn_scan_bwd")
    dqn, dkn, dv, dgb = _gdn_intra_bwd(s["qn"], s["kn"], s["v"], s["gb"], s["u"], s["w"], s["tmat"],
                                       dqd, dkd, dvn, dw, dattn, dcdt, "gdn_intra_bwd")
    dzq, dzab, dcw, dvec = _gdn_prep_bwd(s["z"], dqn, dkn, dv, dgb, p["conv_w"], p["alog"], p["dtb"],
                                         "gdn_prep_bwd")
    dz = jnp.concatenate([dzq, dzg, dq_att, dk_att.astype(MXU), dv_att.astype(MXU), dza, dzb, dzab], axis=1)
    dwz = _mm_tn(s["h"], dz, "in_proj_dw")
    dh = _mm_nt(dz, p["wz"], "in_proj_dx", tn=512)
    dx, dsc1, dsh1, dnmix = _modnorm_bwd(dh, s["x"], p["norm_mix"], sc1, sh1, dx1, "norm_mix_bwd")
    dw_in = jnp.concatenate([dwz[:, :Z_ATT], dwz[:, Z_AB:Z_AB + 2 * NH], dwz[:, Z_ATT:Z_AB]], axis=1)
    grads = dict(w_in=dw_in, w_out=dwout, w_ff_in=dw1, w_ff_out=dw2, norm_mix=dnmix[0], norm_mlp=dnmlp[0],
                 conv_w=dcw[:4], a_log=dvec[0, :NH], dt_bias=dvec[1, :NH], gdn_norm=dgn[0], rel_bias=drb,
                 mod=jnp.concatenate([dsh1, dsc1, dgt1, dsh2, dsc2, dgt2], axis=1)[0])
    return dx, grads


def _layer_params(l, wz, wout, w1, w2, conv_full, norm_mix, norm_mlp, a_log, dt_bias, gdn_norm, rel_bias, idx):
    pad = lambda a: jnp.pad(a, (0, 128 - NH))[None]
    fvec = jnp.transpose(rel_bias[l][:, idx], (1, 0, 2)).reshape(3 * NH, 1, FW)
    return dict(wz=wz[l], wout=wout[l], w1=w1[l], w2=w2[l], conv_w=conv_full[l], norm_mix=norm_mix[l][None],
                norm_mlp=norm_mlp[l][None], alog=pad(a_log[l]), dtb=pad(dt_bias[l]), gdn_norm=gdn_norm[l][None],
                fvec=fvec)


def _local_step(x, target, mod, params, final_norm):
    idx = _bias_index()
    onehot = (jnp.asarray(idx)[:, :, None] == jnp.arange(640)[None, None, :]).astype(F32)
    saved = []
    for l in range(len(params)):
        x, s = _forward_layer(x, mod[l].reshape(6, D), params[l])
        saved.append(s)
    loss, dx, dfn = _loss_head(x, target, final_norm[None], "loss_head")
    grads = [None] * len(params)
    for l in reversed(range(len(params))):
        dx, grads[l] = _backward_layer(dx, mod[l].reshape(6, D), params[l], saved[l], onehot)
    return loss, dx, grads, dfn[0]


SMALL = ("b_ada", "norm_mix", "norm_mlp", "a_log", "dt_bias", "gdn_norm", "rel_bias", "final_norm")


def kernel(x, c, w_ada, b_ada, norm_mix, norm_mlp, w_in, conv_w, a_log, dt_bias, gdn_norm, rel_bias, w_out, w_ff_in, w_ff_out, final_norm, loss_target, m_w_ada, m_b_ada, m_norm_mix, m_norm_mlp, m_w_in, m_conv_w, m_a_log, m_dt_bias, m_gdn_norm, m_rel_bias, m_w_out, m_w_ff_in, m_w_ff_out, m_final_norm, v_w_ada, v_b_ada, v_norm_mix, v_norm_mlp, v_w_in, v_conv_w, v_a_log, v_dt_bias, v_gdn_norm, v_rel_bias, v_w_out, v_w_ff_in, v_w_ff_out, v_final_norm):
    W = dict(w_ada=w_ada, b_ada=b_ada, norm_mix=norm_mix, norm_mlp=norm_mlp, w_in=w_in, conv_w=conv_w,
             a_log=a_log, dt_bias=dt_bias, gdn_norm=gdn_norm, rel_bias=rel_bias, w_out=w_out,
             w_ff_in=w_ff_in, w_ff_out=w_ff_out, final_norm=final_norm)
    Mo = dict(w_ada=m_w_ada, b_ada=m_b_ada, norm_mix=m_norm_mix, norm_mlp=m_norm_mlp, w_in=m_w_in,
              conv_w=m_conv_w, a_log=m_a_log, dt_bias=m_dt_bias, gdn_norm=m_gdn_norm, rel_bias=m_rel_bias,
              w_out=m_w_out, w_ff_in=m_w_ff_in, w_ff_out=m_w_ff_out, final_norm=m_final_norm)
    Vo = dict(w_ada=v_w_ada, b_ada=v_b_ada, norm_mix=v_norm_mix, norm_mlp=v_norm_mlp, w_in=v_w_in,
              conv_w=v_conv_w, a_log=v_a_log, dt_bias=v_dt_bias, gdn_norm=v_gdn_norm, rel_bias=v_rel_bias,
              w_out=v_w_out, w_ff_in=v_w_ff_in, w_ff_out=v_w_ff_out, final_norm=v_final_norm)
    L = w_in.shape[0]
    me = _flat(_mesh_pos())
    cshard = conv_w.shape[2]

    small_in = _all_gather(_pack_rows([c, conv_w]), "gather_c_conv")
    c_all = small_in[:, 0, :]
    conv_full = small_in.reshape(NDEV, -1)[:, D:D + L * 4 * cshard].reshape(NDEV, L, 4, cshard)
    conv_full = jnp.transpose(conv_full, (1, 2, 0, 3)).reshape(L, 4, NDEV * cshard)

    b_shard = lax.dynamic_slice_in_dim(b_ada, me * ADA_SHARD, ADA_SHARD, axis=1)
    mod_all = _all_gather(_ada_mod(c_all, w_ada, b_shard, "ada_mod"), "gather_mod")
    mod = lax.dynamic_index_in_dim(mod_all, me, axis=2, keepdims=False)
    mod = jnp.transpose(mod, (1, 0, 2)).reshape(L, 6 * D)

    g_in = _all_gather(w_in.astype(MXU), "gather_w_in")
    w_in_full = jnp.transpose(g_in, (1, 2, 0, 3)).reshape(L, D, IN_W)
    wz = jnp.concatenate([w_in_full[..., :Z_ATT], w_in_full[..., Z_ATT + 2 * NH:],
                          w_in_full[..., Z_ATT:Z_ATT + 2 * NH],
                          jnp.zeros((L, D, ZW - IN_W), MXU)], axis=-1)
    wout = jnp.transpose(_all_gather(w_out.astype(MXU), "gather_w_out"), (1, 0, 2, 3)).reshape(L, D, D)
    w1 = jnp.transpose(_all_gather(w_ff_in.astype(MXU), "gather_w_ff_in"), (1, 2, 0, 3)).reshape(L, D, DFF)
    w2 = jnp.transpose(_all_gather(w_ff_out.astype(MXU), "gather_w_ff_out"), (1, 0, 2, 3)).reshape(L, DFF, D)

    idx = _bias_index()
    params = [_layer_params(l, wz, wout, w1, w2, conv_full, norm_mix, norm_mlp, a_log, dt_bias, gdn_norm,
                            rel_bias, idx) for l in range(L)]
    loss, dx, grads, dfn = _local_step(x[0], loss_target[0], mod, params, final_norm)

    def stack(name):
        return jnp.stack([g[name] for g in grads])

    small_names = ("mod", "norm_mix", "norm_mlp", "a_log", "dt_bias", "gdn_norm", "rel_bias")
    small_parts = [stack(n) for n in small_names] + [dfn, stack("conv_w"), loss[0, 0:1]]
    small_shapes = [a.shape for a in small_parts]
    gathered = _all_gather(_pack_rows(small_parts), "gather_small_grads")
    total = _unpack_rows(_sum_parts(gathered, "sum_small_grads"), small_shapes)
    tot = dict(zip(small_names + ("final_norm", "conv_w", "loss"), total))
    tot["b_ada"] = tot.pop("mod")
    tot["conv_w"] = lax.dynamic_slice_in_dim(tot["conv_w"], me * cshard, cshard, axis=2)

    out_g, out_d, out_m, out_v = {}, {}, {}, {}
    names = SMALL + ("conv_w",)
    shapes = [W[n].shape for n in names]
    packed = [_pack_rows([src[n] for n in names])[None] if src is tot else _pack_rows([src[n] for n in names])
              for src in (tot, W, Mo, Vo)]
    res = _adamw_reduce(*packed, "adamw_small", tr=8)
    for dst, arr in zip((out_g, out_d, out_m, out_v), res):
        dst.update(zip(names, _unpack_rows(arr, shapes)))

    dmod_all = gathered.reshape(NDEV, -1)[:, :L * 6 * D].reshape(NDEV, L, 6 * D)
    dmod_mine = jnp.transpose(lax.dynamic_slice_in_dim(dmod_all, me * ADA_SHARD, ADA_SHARD, axis=2), (1, 0, 2))
    res = _wada_adamw(jnp.transpose(c_all), dmod_mine, w_ada, m_w_ada, v_w_ada, "adamw_w_ada")
    for dst, arr in zip((out_g, out_d, out_m, out_v), res):
        dst["w_ada"] = arr

    def exchange(name, per_dev, tr):
        recv = _all_to_all(per_dev, "exchange_" + name)
        sh = W[name].shape
        rows = int(np.prod(sh[:-1]))
        flat = lambda a: a.reshape(rows, sh[-1])
        res = _adamw_reduce(recv.reshape(NDEV, rows, sh[-1]), flat(W[name]), flat(Mo[name]), flat(Vo[name]),
                            "adamw_" + name, tr=tr)
        for dst, arr in zip((out_g, out_d, out_m, out_v), res):
            dst[name] = arr.reshape(sh)

    col = lambda a: jnp.transpose(a.reshape(a.shape[0], a.shape[1], NDEV, -1), (2, 0, 1, 3))
    row = lambda a: jnp.transpose(a.reshape(a.shape[0], NDEV, -1, a.shape[2]), (1, 0, 2, 3))
    exchange("w_in", col(stack("w_in")), 256)
    exchange("w_out", row(stack("w_out")), 128)
    exchange("w_ff_in", col(stack("w_ff_in")), 256)
    exchange("w_ff_out", row(stack("w_ff_out")), 256)

    order = ("w_ada", "b_ada", "norm_mix", "norm_mlp", "w_in", "conv_w", "a_log", "dt_bias", "gdn_norm",
             "rel_bias", "w_out", "w_ff_in", "w_ff_out", "final_norm")
    return (tot["loss"].reshape(()), dx[None], *[out_g[n] for n in order], *[out_d[n] for n in order],
            *[out_m[n] for n in order], *[out_v[n] for n in order])
```

```python
import functools
import math

import numpy as np
import jax
import jax.numpy as jnp
from jax import lax
from jax.experimental import pallas as pl
from jax.experimental.pallas import tpu as pltpu

F32 = jnp.float32
MXU = jnp.bfloat16
HI = lax.Precision.HIGHEST
MESH_ID = pl.DeviceIdType.MESH

D = 1024
NH = 8
HD = 128
CH = 64
PAST = 8
DFF = 4096
EPS = 1e-6
NDEV = 8
DEPTH = 4
IN_W = 9232
ZW = 9728
Z_GATE, Z_ATT, Z_BR, Z_AB = 3072, 4096, 7168, 9216
QB = 256
KWIN = 768
FW = 1024
ADAM_LR, ADAM_B1, ADAM_B2, ADAM_EPS, ADAM_WD, ADAM_STEP = 0.001, 0.9, 0.999, 1e-08, 0.01, 10


def _dot(a, b, prec=None):
    return jnp.dot(a, b, preferred_element_type=F32, precision=prec)


def _dot_nt(a, b, prec=None):
    return lax.dot_general(a, b, (((1,), (1,)), ((), ())), preferred_element_type=F32, precision=prec)


def _dot_tn(a, b, prec=None):
    return lax.dot_general(a, b, (((0,), (0,)), ((), ())), preferred_element_type=F32, precision=prec)


def _mx(a):
    return a.astype(MXU)


def _sigmoid(x):
    return 1.0 / (1.0 + jnp.exp(-x))


def _softplus(x):
    return jnp.maximum(x, 0.0) + jnp.log(1.0 + jnp.exp(-jnp.abs(x)))


def _rowsum(x):
    return jnp.sum(x, axis=1, keepdims=True)


def _colsum(x):
    return jnp.sum(x, axis=0, keepdims=True)


def _call(body, name, grid, in_specs, out_specs, out_shape, scratch=(), **params):
    cp = pltpu.CompilerParams(**params) if params else None
    kw = dict(compiler_params=cp) if cp is not None else {}
    return pl.pallas_call(body, name=name, grid=grid, in_specs=in_specs, out_specs=out_specs,
                          out_shape=out_shape, scratch_shapes=list(scratch), **kw)


def _full(shape):
    n = len(shape)
    return pl.BlockSpec(shape, lambda *_: (0,) * n)


def _mesh_pos():
    return lax.axis_index("x"), lax.axis_index("y"), lax.axis_index("c")


def _peer(pos, k):
    x, y, c = pos
    return (x ^ ((k >> 2) & 1), y ^ ((k >> 1) & 1), c ^ (k & 1))


def _flat(pos):
    return 4 * pos[0] + 2 * pos[1] + pos[2]


def _all_gather(x, name):
    def body(x_ref, out_ref, send_sems, recv_sems, local_sem):
        pos = _mesh_pos()
        me = _flat(pos)
        mine = pltpu.make_async_copy(x_ref, out_ref.at[me], local_sem)
        mine.start()
        sends = []
        for k in range(1, NDEV):
            cp = pltpu.make_async_remote_copy(
                src_ref=x_ref, dst_ref=out_ref.at[me], send_sem=send_sems.at[k - 1],
                recv_sem=recv_sems.at[k - 1], device_id=_peer(pos, k), device_id_type=MESH_ID)
            cp.start()
            sends.append(cp)
        for k in range(1, NDEV):
            src = _flat(_peer(pos, k))
            pltpu.make_async_remote_copy(
                src_ref=x_ref, dst_ref=out_ref.at[src], send_sem=send_sems.at[k - 1],
                recv_sem=recv_sems.at[k - 1], device_id=_peer(pos, k), device_id_type=MESH_ID).wait_recv()
        for cp in sends:
            cp.wait_send()
        mine.wait()

    return pl.pallas_call(
        body, name=name,
        out_shape=jax.ShapeDtypeStruct((NDEV,) + x.shape, x.dtype),
        in_specs=[pl.BlockSpec(memory_space=pl.ANY)],
        out_specs=pl.BlockSpec(memory_space=pl.ANY),
        scratch_shapes=[pltpu.SemaphoreType.DMA((NDEV - 1,)), pltpu.SemaphoreType.DMA((NDEV - 1,)),
                        pltpu.SemaphoreType.DMA],
    )(x)


def _all_to_all(x, name):
    def body(x_ref, out_ref, send_sems, recv_sems, local_sem):
        pos = _mesh_pos()
        me = _flat(pos)
        mine = pltpu.make_async_copy(x_ref.at[me], out_ref.at[me], local_sem)
        mine.start()
        sends = []
        for k in range(1, NDEV):
            dst = _flat(_peer(pos, k))
            cp = pltpu.make_async_remote_copy(
                src_ref=x_ref.at[dst], dst_ref=out_ref.at[me], send_sem=send_sems.at[k - 1],
                recv_sem=recv_sems.at[k - 1], device_id=_peer(pos, k), device_id_type=MESH_ID)
            cp.start()
            sends.append(cp)
        for k in range(1, NDEV):
            src = _flat(_peer(pos, k))
            pltpu.make_async_remote_copy(
                src_ref=x_ref.at[src], dst_ref=out_ref.at[src], send_sem=send_sems.at[k - 1],
                recv_sem=recv_sems.at[k - 1], device_id=_peer(pos, k), device_id_type=MESH_ID).wait_recv()
        for cp in sends:
            cp.wait_send()
        mine.wait()

    return pl.pallas_call(
        body, name=name,
        out_shape=jax.ShapeDtypeStruct(x.shape, x.dtype),
        in_specs=[pl.BlockSpec(memory_space=pl.ANY)],
        out_specs=pl.BlockSpec(memory_space=pl.ANY),
        scratch_shapes=[pltpu.SemaphoreType.DMA((NDEV - 1,)), pltpu.SemaphoreType.DMA((NDEV - 1,)),
                        pltpu.SemaphoreType.DMA],
    )(x)


def _mm_nn(a, w, name, *, mode="plain", res=None, gate=None, tm=1024, tn=1024, tk=1024):
    M, K = a.shape
    N = w.shape[1]
    tm, tk, tn = min(tm, M), min(tk, K), min(tn, N)
    nk = K // tk

    def body(*refs):
        refs = list(refs)
        acc = refs.pop() if nk > 1 else None
        if mode == "resid":
            a_ref, w_ref, res_ref, gate_ref, o_ref = refs
        elif mode == "relu2":
            a_ref, w_ref, o_ref, r_ref = refs
        else:
            a_ref, w_ref, o_ref = refs
        k = pl.program_id(2)
        part = _dot(_mx(a_ref[...]), w_ref[...])

        def finish(r):
            if mode == "resid":
                o_ref[...] = res_ref[...] + gate_ref[...] * r
            elif mode == "relu2":
                o_ref[...] = r
                r_ref[...] = jnp.square(jnp.maximum(r, 0.0)).astype(r_ref.dtype)
            else:
                o_ref[...] = r

        if nk == 1:
            finish(part)
        else:
            @pl.when(k == 0)
            def _():
                acc[...] = part

            @pl.when((k > 0) & (k < nk - 1))
            def _():
                acc[...] += part

            @pl.when(k == nk - 1)
            def _():
                finish(acc[...] + part)

    in_specs = [pl.BlockSpec((tm, tk), lambda i, j, k: (i, k)),
                pl.BlockSpec((tk, tn), lambda i, j, k: (k, j))]
    args = [a, w]
    o_spec = pl.BlockSpec((tm, tn), lambda i, j, k: (i, j))
    out_specs, out_shape = o_spec, jax.ShapeDtypeStruct((M, N), F32)
    if mode == "resid":
        in_specs += [o_spec, pl.BlockSpec((1, tn), lambda i, j, k: (0, j))]
        args += [res, gate]
    elif mode == "relu2":
        out_specs = [o_spec, o_spec]
        out_shape = [jax.ShapeDtypeStruct((M, N), F32), jax.ShapeDtypeStruct((M, N), MXU)]
    return _call(body, name, (M // tm, N // tn, nk), in_specs, out_specs, out_shape,
                 [pltpu.VMEM((tm, tn), F32)] if nk > 1 else [])(*args)


def _mm_nt(a, w, name, *, gate=None, drelu=None, tm=1024, tko=1024, tn=1024):
    M, N = a.shape
    K = w.shape[0]
    tm, tn, tko = min(tm, M), min(tn, N), min(tko, K)
    nn = N // tn

    def body(*refs):
        refs = list(refs)
        acc = refs.pop() if nn > 1 else None
        a_ref, w_ref = refs[:2]
        rest = refs[2:]
        gate_ref = rest.pop(0) if gate is not None else None
        pre_ref = rest.pop(0) if drelu is not None else None
        (o_ref,) = rest
        n = pl.program_id(2)
        av = a_ref[...]
        if gate_ref is not None:
            av = av * gate_ref[...]
        part = _dot_nt(_mx(av), w_ref[...])

        def finish(r):
            if pre_ref is not None:
                r = r * (2.0 * jnp.maximum(pre_ref[...], 0.0))
            o_ref[...] = r.astype(o_ref.dtype)

        if nn == 1:
            finish(part)
        else:
            @pl.when(n == 0)
            def _():
                acc[...] = part

            @pl.when((n > 0) & (n < nn - 1))
            def _():
                acc[...] += part

            @pl.when(n == nn - 1)
            def _():
                finish(acc[...] + part)

    in_specs = [pl.BlockSpec((tm, tn), lambda i, j, n: (i, n)),
                pl.BlockSpec((tko, tn), lambda i, j, n: (j, n))]
    args = [a, w]
    if gate is not None:
        in_specs.append(pl.BlockSpec((1, tn), lambda i, j, n: (0, n)))
        args.append(gate)
    o_spec = pl.BlockSpec((tm, tko), lambda i, j, n: (i, j))
    if drelu is not None:
        in_specs.append(o_spec)
        args.append(drelu)
    out_dtype = MXU if drelu is not None else F32
    return _call(body, name, (M // tm, K // tko, nn), in_specs, o_spec,
                 jax.ShapeDtypeStruct((M, K), out_dtype), [pltpu.VMEM((tm, tko), F32)] if nn > 1 else [])(*args)


def _mm_tn(a, b, name, *, gate=None, w=None, tk=1024, tn=1024, tm=1024):
    M, K = a.shape
    N = b.shape[1]
    tm, tk, tn = min(tm, M), min(tk, K), min(tn, N)
    nm = M // tm
    gated = gate is not None

    def body(*refs):
        if gated:
            a_ref, b_ref, gate_ref, w_ref, o_ref, dg_ref, acc = refs
        else:
            a_ref, b_ref, o_ref, acc = refs
        kk = pl.program_id(1)
        m = pl.program_id(2)
        part = _dot_tn(_mx(a_ref[...]), _mx(b_ref[...]))

        @pl.when((m == 0) & (nm > 1))
        def _():
            acc[...] = part

        @pl.when((m > 0) & (m < nm - 1))
        def _():
            acc[...] += part

        if gated:
            @pl.when((m == 0) & (kk == 0))
            def _():
                dg_ref[...] = jnp.zeros_like(dg_ref)

        @pl.when(m == nm - 1)
        def _():
            r = acc[...] + part if nm > 1 else part
            if gated:
                o_ref[...] = r * gate_ref[...]
                dg_ref[...] += _colsum(r * w_ref[...].astype(F32))
            else:
                o_ref[...] = r

    in_specs = [pl.BlockSpec((tm, tk), lambda j, k, m: (m, k)),
                pl.BlockSpec((tm, tn), lambda j, k, m: (m, j))]
    args = [a, b]
    o_spec = pl.BlockSpec((tk, tn), lambda j, k, m: (k, j))
    out_specs, out_shape = o_spec, jax.ShapeDtypeStruct((K, N), F32)
    if gated:
        in_specs += [pl.BlockSpec((1, tn), lambda j, k, m: (0, j)), o_spec]
        args += [gate, w]
        out_specs = [o_spec, pl.BlockSpec((1, tn), lambda j, k, m: (0, j))]
        out_shape = [out_shape, jax.ShapeDtypeStruct((1, N), F32)]
    return _call(body, name, (N // tn, K // tk, nm), in_specs, out_specs, out_shape,
                 [pltpu.VMEM((tk, tn), F32)])(*args)


def _modnorm_fwd(x, gain, sc, sh, name, ts=512):
    S = x.shape[0]

    def body(x_ref, g_ref, sc_ref, sh_ref, h_ref):
        xv = x_ref[...]
        r = lax.rsqrt(jnp.mean(xv * xv, axis=1, keepdims=True) + EPS)
        h_ref[...] = ((xv * r * g_ref[...]) * (1.0 + sc_ref[...]) + sh_ref[...]).astype(h_ref.dtype)

    row = pl.BlockSpec((ts, D), lambda i: (i, 0))
    vec = pl.BlockSpec((1, D), lambda i: (0, 0))
    return _call(body, name, (S // ts,), [row, vec, vec, vec], row,
                 jax.ShapeDtypeStruct((S, D), MXU))(x, gain, sc, sh)


def _modnorm_bwd(dh, x, gain, sc, sh, dx_in, name, ts=512):
    S = x.shape[0]

    def body(dh_ref, x_ref, g_ref, sc_ref, sh_ref, dxin_ref, dx_ref, dsc_ref, dsh_ref, dg_ref):
        i = pl.program_id(0)
        xv = x_ref[...]
        dhv = dh_ref[...]
        g = g_ref[...]
        r = lax.rsqrt(jnp.mean(xv * xv, axis=1, keepdims=True) + EPS)
        xr = xv * r
        dn = dhv * (1.0 + sc_ref[...])
        u = dn * g
        dx_ref[...] = dxin_ref[...] + r * (u - xr * jnp.mean(xr * u, axis=1, keepdims=True))

        @pl.when(i == 0)
        def _():
            dsc_ref[...] = jnp.zeros_like(dsc_ref)
            dsh_ref[...] = jnp.zeros_like(dsh_ref)
            dg_ref[...] = jnp.zeros_like(dg_ref)

        dsc_ref[...] += _colsum(dhv * (xr * g))
        dsh_ref[...] += _colsum(dhv)
        dg_ref[...] += _colsum(dn * xr)

    row = pl.BlockSpec((ts, D), lambda i: (i, 0))
    vec = pl.BlockSpec((1, D), lambda i: (0, 0))
    vshape = jax.ShapeDtypeStruct((1, D), F32)
    return _call(body, name, (S // ts,), [row, row, vec, vec, vec, row], [row, vec, vec, vec],
                 [jax.ShapeDtypeStruct((S, D), F32), vshape, vshape, vshape])(dh, x, gain, sc, sh, dx_in)


def _loss_head(x, target, gain, name, ts=512):
    S = x.shape[0]

    def body(x_ref, t_ref, g_ref, loss_ref, dx_ref, dg_ref):
        i = pl.program_id(0)
        xv = x_ref[...]
        g = g_ref[...]
        r = lax.rsqrt(jnp.mean(xv * xv, axis=1, keepdims=True) + EPS)
        xr = xv * r
        e = xr * g - t_ref[...]
        dy = e * (1.0 / D)
        u = dy * g
        dx_ref[...] = r * (u - xr * jnp.mean(xr * u, axis=1, keepdims=True))

        @pl.when(i == 0)
        def _():
            loss_ref[...] = jnp.zeros_like(loss_ref)
            dg_ref[...] = jnp.zeros_like(dg_ref)

        part = 0.5 * jnp.sum(jnp.mean(e * e, axis=1, keepdims=True), axis=0, keepdims=True)
        loss_ref[...] += jnp.broadcast_to(part, loss_ref.shape)
        dg_ref[...] += _colsum(dy * xr)

    row = pl.BlockSpec((ts, D), lambda i: (i, 0))
    vec = pl.BlockSpec((1, D), lambda i: (0, 0))
    return _call(body, name, (S // ts,), [row, row, vec],
                 [pl.BlockSpec((1, 128), lambda i: (0, 0)), row, vec],
                 [jax.ShapeDtypeStruct((1, 128), F32), jax.ShapeDtypeStruct((S, D), F32),
                  jax.ShapeDtypeStruct((1, D), F32)])(x, target, gain)


def _merge_specs(ts):
    o_spec = pl.BlockSpec((NH, ts, HD), lambda i: (0, i, 0))
    zg = pl.BlockSpec((ts, D), lambda i: (i, Z_GATE // D))
    za = pl.BlockSpec((ts, D), lambda i: (i, Z_BR // D))
    zb = pl.BlockSpec((ts, D), lambda i: (i, Z_BR // D + 1))
    row = pl.BlockSpec((ts, D), lambda i: (i, 0))
    gn = pl.BlockSpec((1, HD), lambda i: (0, 0))
    return o_spec, zg, za, zb, row, gn


def _merge_fwd(o, z, ob, gn, name, ts=256):
    S = ob.shape[0]

    def body(o_ref, zg_ref, za_ref, zb_ref, ob_ref, gn_ref, m_ref):
        for h in range(NH):
            sl = slice(h * HD, (h + 1) * HD)
            oh = o_ref[h]
            r = lax.rsqrt(jnp.mean(oh * oh, axis=1, keepdims=True) + EPS)
            gate = zg_ref[:, sl]
            oa = (oh * r * gn_ref[...]) * (gate * _sigmoid(gate))
            m = _sigmoid(za_ref[:, sl]) * oa + _sigmoid(zb_ref[:, sl]) * ob_ref[:, sl]
            m_ref[:, sl] = m.astype(m_ref.dtype)

    o_spec, zg, za, zb, row, gns = _merge_specs(ts)
    return _call(body, name, (S // ts,), [o_spec, zg, za, zb, row, gns], row,
                 jax.ShapeDtypeStruct((S, D), MXU))(o, z, z, z, ob, gn)


def _merge_bwd(dm, o, z, ob, gn, name, ts=256):
    S = ob.shape[0]

    def body(dm_ref, o_ref, zg_ref, za_ref, zb_ref, ob_ref, gn_ref,
             do_ref, dzg_ref, dob_ref, dza_ref, dzb_ref, dgn_ref):
        i = pl.program_id(0)
        gn_v = gn_ref[...]
        dgn = jnp.zeros((1, HD), F32)
        for h in range(NH):
            sl = slice(h * HD, (h + 1) * HD)
            dmh = dm_ref[:, sl]
            oh = o_ref[h]
            r = lax.rsqrt(jnp.mean(oh * oh, axis=1, keepdims=True) + EPS)
            ohr = oh * r
            on = ohr * gn_v
            gate = zg_ref[:, sl]
            sg = _sigmoid(gate)
            silu = gate * sg
            oa = on * silu
            ga = _sigmoid(za_ref[:, sl])
            gb = _sigmoid(zb_ref[:, sl])
            obh = ob_ref[:, sl]
            doa = dmh * ga
            dob_ref[:, sl] = dmh * gb
            dza_ref[:, sl] = (dmh * oa * ga * (1.0 - ga)).astype(dza_ref.dtype)
            dzb_ref[:, sl] = (dmh * obh * gb * (1.0 - gb)).astype(dzb_ref.dtype)
            don = doa * silu
            dzg_ref[:, sl] = (doa * on * (sg * (1.0 + gate * (1.0 - sg)))).astype(dzg_ref.dtype)
            dgn = dgn + _colsum(don * ohr)
            u = don * gn_v
            do_ref[h] = r * (u - ohr * jnp.mean(ohr * u, axis=1, keepdims=True))

        @pl.when(i == 0)
        def _():
            dgn_ref[...] = jnp.zeros_like(dgn_ref)

        dgn_ref[...] += dgn

    o_spec, zg, za, zb, row, gns = _merge_specs(ts)
    return _call(
        body, name, (S // ts,), [row, o_spec, zg, za, zb, row, gns],
        [o_spec, row, row, row, row, gns],
        [jax.ShapeDtypeStruct((NH, S, HD), F32), jax.ShapeDtypeStruct((S, D), MXU),
         jax.ShapeDtypeStruct((S, D), F32), jax.ShapeDtypeStruct((S, D), MXU),
         jax.ShapeDtypeStruct((S, D), MXU), jax.ShapeDtypeStruct((1, HD), F32)],
    )(dm, o, z, z, z, ob, gn)


GROWS = 256
GCH = GROWS // CH


def _gdn_prep_fwd(z, conv_w, alog_row, dtb_row, name):
    S = z.shape[0]
    ts = GROWS
    scale = HD ** -0.5

    def body(z_ref, halo_ref, zab_ref, w_ref, al_ref, dt_ref, q_ref, k_ref, v_ref, gb_ref, buf):
        i = pl.program_id(0)
        buf[0:8, :] = jnp.where(i == 0, 0.0, halo_ref[...])
        buf[8:8 + ts, :] = z_ref[...]
        outs = (q_ref, k_ref, v_ref)
        for seg in range(3):
            cs = slice(seg * D, (seg + 1) * D)
            c = jnp.zeros((ts, D), F32)
            for j in range(4):
                c = c + w_ref[j:j + 1, cs] * buf[pl.ds(5 + j, ts), cs]
            s = c * _sigmoid(c)
            if seg == 2:
                outs[seg][...] = s
            else:
                mul = scale if seg == 0 else 1.0
                for h in range(NH):
                    sl = slice(h * HD, (h + 1) * HD)
                    sh = s[:, sl]
                    r = lax.rsqrt(_rowsum(sh * sh) + EPS)
                    outs[seg][:, sl] = sh * (r * mul)
        zab = zab_ref[...]
        lane = lax.broadcasted_iota(jnp.int32, zab.shape, 1)
        g = -jnp.exp(al_ref[...]) * _softplus(zab + dt_ref[...])
        ri = lax.broadcasted_iota(jnp.int32, (CH, CH), 0)
        ci = lax.broadcasted_iota(jnp.int32, (CH, CH), 1)
        incl = (ri >= ci).astype(F32)
        gcum = jnp.concatenate([_dot(incl, g[c * CH:(c + 1) * CH], HI) for c in range(ts // CH)], axis=0)
        gb_ref[...] = jnp.where(lane < NH, gcum, jnp.where(lane < 2 * NH, _sigmoid(zab), 0.0))

    row = pl.BlockSpec((ts, D), lambda i: (i, 0))
    vec = pl.BlockSpec((1, 128), lambda i: (0, 0))
    return _call(
        body, name, (S // ts,),
        [pl.BlockSpec((ts, 3 * D), lambda i: (i, 0)),
         pl.BlockSpec((8, 3 * D), lambda i: (jnp.maximum(i * (ts // 8) - 1, 0), 0)),
         pl.BlockSpec((ts, 128), lambda i: (i, Z_AB // 128)),
         _full((4, 3 * D)), vec, vec],
        [row, row, row, pl.BlockSpec((ts, 128), lambda i: (i, 0))],
        [jax.ShapeDtypeStruct((S, D), F32)] * 3 + [jax.ShapeDtypeStruct((S, 128), F32)],
        [pltpu.VMEM((ts + 8, 3 * D), F32)],
    )(z, z, z, conv_w, alog_row, dtb_row)


def _split(a):
    hi = a.astype(MXU)
    return hi, (a - hi.astype(F32)).astype(MXU)


def _dot3(a, b, dot=_dot):
    ah, al = _split(a)
    bh, bl = _split(b)
    return dot(ah, bh) + (dot(ah, bl) + dot(al, bh))


IROWS = 512
ICH = IROWS // CH


def _chunk_common(gbk, h, k):
    lane = lax.broadcasted_iota(jnp.int32, gbk.shape, 1)
    G = _rowsum(jnp.where(lane == h, gbk, 0.0))
    b_col = _rowsum(jnp.where(lane == h + NH, gbk, 0.0))
    ri = lax.broadcasted_iota(jnp.int32, (CH, CH), 0)
    ci = lax.broadcasted_iota(jnp.int32, (CH, CH), 1)
    incl = ri >= ci
    gc = jnp.broadcast_to(G, (CH, CH))
    decay = jnp.where(incl, jnp.exp(jnp.where(incl, gc - gc.T, 0.0)), 0.0)
    Gl = G[CH - 1:CH, :]
    kb = k * b_col
    return dict(b=b_col, ri=ri, ci=ci, incl=incl, strict=ri > ci, decay=decay, eG=jnp.exp(G),
                e2=jnp.exp(Gl - G), cd=jnp.exp(Gl), kb=kb, kk=_dot_nt(_mx(kb), _mx(k)))


def _gdn_intra_fwd(qn, kn, v, gb, name):
    S = qn.shape[0]

    def body(q_ref, k_ref, v_ref, gb_ref, u_ref, w_ref, qd_ref, kd_ref, at_ref, t_ref, cd_ref):
        h = pl.program_id(1)
        rows = [slice(c * CH, (c + 1) * CH) for c in range(ICH)]
        ks = [k_ref[r, :] for r in rows]
        cms = [_chunk_common(gb_ref[r, :], h, k) for r, k in zip(rows, ks)]
        ps = [jnp.where(cm["strict"], cm["kk"] * cm["decay"], 0.0) for cm in cms]
        ts = [(cm["ri"] == cm["ci"]).astype(F32) - p for cm, p in zip(cms, ps)]
        for _ in range(5):
            ps = [_dot3(p, p) for p in ps]
            ts = [t + _dot3(t, p) for t, p in zip(ts, ps)]
        for c, (r, k, cm, t) in enumerate(zip(rows, ks, cms, ts)):
            rhs = jnp.concatenate([v_ref[r, :] * cm["b"], k * (cm["b"] * cm["eG"])], axis=1)
            sol = _dot3(t, rhs)
            u_ref[0, r, :] = sol[:, :HD]
            w_ref[0, r, :] = sol[:, HD:]
            t_ref[0, r, :] = t
        for c, (r, k, cm) in enumerate(zip(rows, ks, cms)):
            q = q_ref[r, :]
            qk = _dot_nt(_mx(q), _mx(k))
            at_ref[0, r, :] = jnp.where(cm["incl"], qk * cm["decay"], 0.0)
            qd_ref[0, r, :] = q * cm["eG"]
            kd_ref[0, r, :] = k * cm["e2"]
            cd_ref[0, c] = jnp.broadcast_to(cm["cd"], (8, 128))

    tok = pl.BlockSpec((IROWS, HD), lambda i, h: (i, h))
    hm = pl.BlockSpec((1, IROWS, HD), lambda i, h: (h, i, 0))
    hm64 = pl.BlockSpec((1, IROWS, CH), lambda i, h: (h, i, 0))
    big = jax.ShapeDtypeStruct((NH, S, HD), F32)
    sm = jax.ShapeDtypeStruct((NH, S, CH), F32)
    return _call(
        body, name, (S // IROWS, NH),
        [tok, tok, tok, pl.BlockSpec((IROWS, 128), lambda i, h: (i, 0))],
        [hm, hm, hm, hm, hm64, hm64, pl.BlockSpec((1, ICH, 8, 128), lambda i, h: (h, i, 0, 0))],
        [big, big, big, big, sm, sm, jax.ShapeDtypeStruct((NH, S // CH, 8, 128), F32)],
    )(qn, kn, v, gb)


def _scale_state(s, cd_tile):
    return (s.reshape(HD // 8, 8, HD) * cd_tile[None]).reshape(HD, HD)


def _gdn_scan_fwd(u, w, qd, kd, attn, cdt, name):
    S = u.shape[1]
    nblk = S // GROWS

    def body(u_ref, w_ref, qd_ref, kd_ref, at_ref, cd_ref, o_ref, vn_ref, st_ref, s_ref):
        i = pl.program_id(0)

        @pl.when(i == 0)
        def _():
            s_ref[...] = jnp.zeros_like(s_ref)

        def chunk(c, carry):
            r0 = pl.multiple_of(c * CH, CH)
            rows = pl.ds(r0, CH)
            for h in range(NH):
                sh = s_ref[h]
                st_ref[h, c] = sh
                sb = _mx(sh)
                vn = u_ref[h, rows, :] - _dot(_mx(w_ref[h, rows, :]), sb)
                vb = _mx(vn)
                vn_ref[h, rows, :] = vn
                o_ref[h, rows, :] = _dot(_mx(qd_ref[h, rows, :]), sb) + _dot(_mx(at_ref[h, rows, :]), vb)
                s_ref[h] = _scale_state(sh, cd_ref[h, c]) + _dot_tn(_mx(kd_ref[h, rows, :]), vb)
            return carry

        lax.fori_loop(0, GCH, chunk, 0)

    hm = pl.BlockSpec((NH, GROWS, HD), lambda i: (0, i, 0))
    hm64 = pl.BlockSpec((NH, GROWS, CH), lambda i: (0, i, 0))
    big = jax.ShapeDtypeStruct((NH, S, HD), F32)
    return _call(
        body, name, (nblk,),
        [hm, hm, hm, hm, hm64, pl.BlockSpec((NH, GCH, 8, 128), lambda i: (0, i, 0, 0))],
        [hm, hm, pl.BlockSpec((NH, GCH, HD, HD), lambda i: (0, i, 0, 0))],
        [big, big, jax.ShapeDtypeStruct((NH, S // CH, HD, HD), F32)],
        [pltpu.VMEM((NH, HD, HD), F32)],
    )(u, w, qd, kd, attn, cdt)


def _gdn_scan_bwd(do, w, qd, kd, attn, cdt, vn, st, name):
    S = do.shape[1]
    nblk = S // GROWS

    def body(do_ref, w_ref, qd_ref, kd_ref, at_ref, cd_ref, vn_ref, st_ref,
             dqd_ref, dkd_ref, dvn_ref, dw_ref, dat_ref, dcd_ref, ds_ref):
        i = pl.program_id(0)

        @pl.when(i == 0)
        def _():
            ds_ref[...] = jnp.zeros_like(ds_ref)

        def chunk(cc, carry):
            c = GCH - 1 - cc
            r0 = pl.multiple_of(c * CH, CH)
            rows = pl.ds(r0, CH)
            for h in range(NH):
                dsp = ds_ref[h]
                sh = st_ref[h, c]
                dsb, sb = _mx(dsp), _mx(sh)
                dob = _mx(do_ref[h, rows, :])
                vb = _mx(vn_ref[h, rows, :])
                dvn = _dot(_mx(kd_ref[h, rows, :]), dsb) + _dot_tn(_mx(at_ref[h, rows, :]), dob)
                dvb = _mx(dvn)
                dvn_ref[h, rows, :] = dvn
                dqd_ref[h, rows, :] = _dot_nt(dob, sb)
                dat_ref[h, rows, :] = _dot_nt(dob, vb)
                dkd_ref[h, rows, :] = _dot_nt(vb, dsb)
                dw_ref[h, rows, :] = -_dot_nt(dvb, sb)
                dcd = jnp.sum(_rowsum(dsp * sh), axis=0, keepdims=True)
                dcd_ref[h, c] = jnp.broadcast_to(dcd, (8, 128))
                ds_ref[h] = (_scale_state(dsp, cd_ref[h, c]) + _dot_tn(_mx(qd_ref[h, rows, :]), dob)
                             - _dot_tn(_mx(w_ref[h, rows, :]), dvb))
            return carry

        lax.fori_loop(0, GCH, chunk, 0)

    hm = pl.BlockSpec((NH, GROWS, HD), lambda i: (0, nblk - 1 - i, 0))
    hm64 = pl.BlockSpec((NH, GROWS, CH), lambda i: (0, nblk - 1 - i, 0))
    tile = pl.BlockSpec((NH, GCH, 8, 128), lambda i: (0, nblk - 1 - i, 0, 0))
    big = jax.ShapeDtypeStruct((NH, S, HD), F32)
    return _call(
        body, name, (nblk,),
        [hm, hm, hm, hm, hm64, tile, hm, pl.BlockSpec((NH, GCH, HD, HD), lambda i: (0, nblk - 1 - i, 0, 0))],
        [hm, hm, hm, hm, hm64, tile],
        [big, big, big, big, jax.ShapeDtypeStruct((NH, S, CH), F32),
         jax.ShapeDtypeStruct((NH, S // CH, 8, 128), F32)],
        [pltpu.VMEM((NH, HD, HD), F32)],
    )(do, w, qd, kd, attn, cdt, vn, st)


def _gdn_intra_bwd(qn, kn, v, gb, u, w, tmat, dqd, dkd, du, dw, dattn, dcdt, name):
    S = qn.shape[0]

    def body(q_ref, k_ref, v_ref, gb_ref, u_ref, w_ref, t_ref, dqd_ref, dkd_ref, du_ref, dw_ref,
             dat_ref, dcd_ref, dq_ref, dk_ref, dv_ref, dgb_ref):
        h = pl.program_id(1)
        rows = [slice(c * CH, (c + 1) * CH) for c in range(ICH)]
        ks = [k_ref[r, :] for r in rows]
        cms = [_chunk_common(gb_ref[r, :], h, k) for r, k in zip(rows, ks)]
        sols = [jnp.concatenate([u_ref[0, r, :], w_ref[0, r, :]], axis=1) for r in rows]
        drhss = [_dot3(t_ref[0, r, :], jnp.concatenate([du_ref[0, r, :], dw_ref[0, r, :]], axis=1), _dot_tn)
                 for r in rows]
        das = [-_dot3(drhs, sol, _dot_nt) for drhs, sol in zip(drhss, sols)]
        for c, (r, k, cm, drhs, da) in enumerate(zip(rows, ks, cms, drhss, das)):
            q, vv = q_ref[r, :], v_ref[r, :]
            decay, eG, e2, b = cm["decay"], cm["eG"], cm["e2"], cm["b"]
            dru, drw = drhs[:, :HD], drhs[:, HD:]
            dv_ref[r, :] = dru * b
            s_w = _rowsum(drw * k)
            dbeta = _rowsum(dru * vv) + s_w * eG
            deg = s_w * b
            dk = drw * (b * eG)
            dkk = jnp.where(cm["strict"], da * decay, 0.0)
            ddec = jnp.where(cm["strict"], da * cm["kk"], 0.0)
            dkkb = _mx(dkk)
            dkb = _dot(dkkb, _mx(k))
            dk = dk + _dot_tn(dkkb, _mx(cm["kb"])) + dkb * b
            dbeta = dbeta + _rowsum(dkb * k)
            dat = jnp.where(cm["incl"], dat_ref[0, r, :], 0.0)
            qk = _dot_nt(_mx(q), _mx(k))
            dqk = _mx(dat * decay)
            ddec = ddec + dat * qk
            dqd = dqd_ref[0, r, :]
            dkd = dkd_ref[0, r, :]
            dq_ref[r, :] = _dot(dqk, _mx(k)) + dqd * eG
            dk_ref[r, :] = dk + _dot_tn(dqk, _mx(q)) + dkd * e2
            deg = deg + _rowsum(dqd * q)
            t2 = _rowsum(dkd * k) * e2
            dgl = jnp.sum(t2, axis=0, keepdims=True) + dcd_ref[0, c][0:1, 0:1] * cm["cd"]
            dd = ddec * decay
            dG = deg * eG - t2 + _rowsum(dd) - _rowsum(dd.T)
            rowi = lax.broadcasted_iota(jnp.int32, (CH, 1), 0)
            dG = dG + jnp.where(rowi == CH - 1, dgl, 0.0)
            lane = lax.broadcasted_iota(jnp.int32, (CH, 128), 1)
            dgb_ref[0, r, :] = jnp.where(lane == h, dG, 0.0) + jnp.where(lane == h + NH, dbeta, 0.0)

    tok = pl.BlockSpec((IROWS, HD), lambda i, h: (i, h))
    hm = pl.BlockSpec((1, IROWS, HD), lambda i, h: (h, i, 0))
    hm64 = pl.BlockSpec((1, IROWS, CH), lambda i, h: (h, i, 0))
    tile = pl.BlockSpec((1, ICH, 8, 128), lambda i, h: (h, i, 0, 0))
    tokout = jax.ShapeDtypeStruct((S, D), F32)
    return _call(
        body, name, (S // IROWS, NH),
        [tok, tok, tok, pl.BlockSpec((IROWS, 128), lambda i, h: (i, 0)), hm, hm, hm64,
         hm, hm, hm, hm, hm64, tile],
        [tok, tok, tok, pl.BlockSpec((1, IROWS, 128), lambda i, h: (h, i, 0))],
        [tokout, tokout, tokout, jax.ShapeDtypeStruct((NH, S, 128), F32)],
    )(qn, kn, v, gb, u, w, tmat, dqd, dkd, du, dw, dattn, dcdt)


def _gdn_prep_bwd(z, dqn, dkn, dv, dgb, conv_w, alog_row, dtb_row, name):
    S = z.shape[0]
    ts = GROWS
    nblk = S // ts
    scale = HD ** -0.5
    tb = ts // 8

    def body(z_ref, hp_ref, hn_ref, zab_ref, dq_ref, dqn_ref, dk_ref, dkn_ref, dv_ref, dvn_ref,
             dgb_ref, w_ref, al_ref, dt_ref, dz_ref, dzab_ref, dcw_ref, dvec_ref, buf, dybuf, dcbuf):
        i = pl.program_id(0)
        last = i == nblk - 1

        @pl.when(i == 0)
        def _():
            dcw_ref[...] = jnp.zeros_like(dcw_ref)
            dvec_ref[...] = jnp.zeros_like(dvec_ref)

        buf[0:8, :] = jnp.where(i == 0, 0.0, hp_ref[...])
        buf[8:8 + ts, :] = z_ref[...]
        buf[8 + ts:16 + ts, :] = hn_ref[...]
        rowi = lax.broadcasted_iota(jnp.int32, (ts + 8, 1), 0)
        live = jnp.logical_or(rowi < ts, jnp.logical_not(last))
        dys = ((dq_ref, dqn_ref), (dk_ref, dkn_ref), (dv_ref, dvn_ref))
        for seg in range(3):
            cs = slice(seg * D, (seg + 1) * D)
            dybuf[0:ts, :] = dys[seg][0][...]
            dybuf[ts:ts + 8, :] = dys[seg][1][...]
            c = jnp.zeros((ts + 8, D), F32)
            for j in range(4):
                c = c + w_ref[j:j + 1, cs] * buf[pl.ds(5 + j, ts + 8), cs]
            sg = _sigmoid(c)
            s = c * sg
            dsilu = sg * (1.0 + c * (1.0 - sg))
            if seg == 2:
                dcbuf[...] = jnp.where(live, dybuf[...] * dsilu, 0.0)
            else:
                mul = scale if seg == 0 else 1.0
                for h in range(NH):
                    sl = slice(h * HD, (h + 1) * HD)
                    sh = s[:, sl]
                    dy = dybuf[:, sl]
                    r = lax.rsqrt(_rowsum(sh * sh) + EPS)
                    shr = sh * r
                    ds = (mul * r) * (dy - shr * _rowsum(shr * dy))
                    dcbuf[:, sl] = jnp.where(live, ds * dsilu[:, sl], 0.0)
            dx = jnp.zeros((ts, D), F32)
            for j in range(4):
                dcw_ref[j:j + 1, cs] += _colsum(dcbuf[0:ts, :] * buf[pl.ds(5 + j, ts), cs])
                dx = dx + w_ref[j:j + 1, cs] * dcbuf[pl.ds(3 - j, ts), :]
            dz_ref[:, cs] = dx.astype(dz_ref.dtype)
        dgbs = dgb_ref[0]
        for h in range(1, NH):
            dgbs = dgbs + dgb_ref[h]
        ri = lax.broadcasted_iota(jnp.int32, (CH, CH), 0)
        ci = lax.broadcasted_iota(jnp.int32, (CH, CH), 1)
        rev = (ci >= ri).astype(F32)
        dgrev = jnp.concatenate([_dot(rev, dgbs[c * CH:(c + 1) * CH], HI) for c in range(ts // CH)], axis=0)
        lane0 = lax.broadcasted_iota(jnp.int32, dgbs.shape, 1)
        dgbs = jnp.where(lane0 < NH, dgrev, dgbs)
        zab = zab_ref[...]
        lane = lax.broadcasted_iota(jnp.int32, zab.shape, 1)
        xx = zab + dt_ref[...]
        ea = jnp.exp(al_ref[...])
        g = -ea * _softplus(xx)
        da = dgbs * (-ea) * _sigmoid(xx)
        beta = _sigmoid(zab)
        db = dgbs * beta * (1.0 - beta)
        is_a = lane < NH
        dzab = jnp.where(is_a, da, jnp.where(lane < 2 * NH, db, 0.0))
        dzab_ref[:, 0:128] = dzab.astype(dzab_ref.dtype)
        dzab_ref[:, 128:512] = jnp.zeros((ts, 384), dzab_ref.dtype)
        dvec_ref[0:1, :] += _colsum(jnp.where(is_a, dgbs * g, 0.0))
        dvec_ref[1:2, :] += _colsum(jnp.where(is_a, da, 0.0))

    z3 = pl.BlockSpec((ts, 3 * D), lambda i: (i, 0))
    row = pl.BlockSpec((ts, D), lambda i: (i, 0))
    nxt = pl.BlockSpec((8, D), lambda i: (jnp.minimum((i + 1) * tb, S // 8 - 1), 0))
    vec = pl.BlockSpec((1, 128), lambda i: (0, 0))
    return _call(
        body, name, (nblk,),
        [z3,
         pl.BlockSpec((8, 3 * D), lambda i: (jnp.maximum(i * tb - 1, 0), 0)),
         pl.BlockSpec((8, 3 * D), lambda i: (jnp.minimum((i + 1) * tb, S // 8 - 1), 0)),
         pl.BlockSpec((ts, 128), lambda i: (i, Z_AB // 128)),
         row, nxt, row, nxt, row, nxt,
         pl.BlockSpec((NH, ts, 128), lambda i: (0, i, 0)),
         _full((4, 3 * D)), vec, vec],
        [z3, pl.BlockSpec((ts, 512), lambda i: (i, 0)), _full((8, 3 * D)), _full((8, 128))],
        [jax.ShapeDtypeStruct((S, 3 * D), MXU), jax.ShapeDtypeStruct((S, 512), MXU),
         jax.ShapeDtypeStruct((8, 3 * D), F32), jax.ShapeDtypeStruct((8, 128), F32)],
        [pltpu.VMEM((ts + 16, 3 * D), F32), pltpu.VMEM((ts + 8, D), F32), pltpu.VMEM((ts + 8, D), F32)],
    )(z, z, z, z, dqn, dqn, dkn, dkn, dv, dv, dgb, conv_w, alog_row, dtb_row)


def _bias_index():
    u = np.arange(FW)[None, :]
    s = np.arange(3)[:, None]
    return np.clip(KWIN - 1 - u - QB * s, -256, 256) + 256


def _bias_vec(rel_bias_pad, onehot, name):
    def body(rb_ref, e_ref, o_ref):
        o_ref[:, 0, :] = _dot_nt(rb_ref[...], e_ref[0], HI)

    return _call(body, name, (3,),
                 [_full((NH, 640)), pl.BlockSpec((1, FW, 640), lambda s: (s, 0, 0))],
                 pl.BlockSpec((NH, 1, FW), lambda s: (s, 0, 0)),
                 jax.ShapeDtypeStruct((3 * NH, 1, FW), F32))(rel_bias_pad, onehot)


def _att_window(i):
    return pl.multiple_of(jnp.maximum(i * QB - PAST * CH, 0), QB)


def _att_probs(q_ref, k_ref, f_ref, i):
    ws = _att_window(i)
    q = _mx(q_ref[...] * (HD ** -0.5))
    kw = _mx(k_ref[pl.ds(ws, KWIN), :])
    fb = jnp.broadcast_to(f_ref[0], (QB, FW))
    bias = pltpu.roll(fb, FW - 255, 1, stride=1, stride_axis=0)[:, :KWIN]
    s = _dot_nt(q, kw) + bias
    qc = (i * QB + lax.broadcasted_iota(jnp.int32, (QB, KWIN), 0)) // CH
    kc = (ws + lax.broadcasted_iota(jnp.int32, (QB, KWIN), 1)) // CH
    valid = (kc <= qc) & (kc >= qc - PAST)
    s = jnp.where(valid, s, -jnp.inf)
    p = jnp.exp(s - jnp.max(s, axis=1, keepdims=True))
    p = p / _rowsum(p)
    return q, kw, ws, p


def _att_specs(S):
    c0 = Z_ATT // HD
    q = pl.BlockSpec((QB, HD), lambda h, i: (i, c0 + h))
    k = pl.BlockSpec((S, HD), lambda h, i: (0, c0 + NH + h))
    v = pl.BlockSpec((S, HD), lambda h, i: (0, c0 + 2 * NH + h))
    f = pl.BlockSpec((1, 1, FW), lambda h, i: (jnp.maximum(2 - i, 0) * NH + h, 0, 0))
    tok = pl.BlockSpec((QB, HD), lambda h, i: (i, h))
    return q, k, v, f, tok


def _att_fwd(z, fvec, name):
    S = z.shape[0]

    def body(q_ref, k_ref, v_ref, f_ref, o_ref):
        i = pl.program_id(1)
        _, _, ws, p = _att_probs(q_ref, k_ref, f_ref, i)
        o_ref[...] = _dot(_mx(p), _mx(v_ref[pl.ds(ws, KWIN), :]))

    q, k, v, f, tok = _att_specs(S)
    return _call(body, name, (NH, S // QB), [q, k, v, f], tok,
                 jax.ShapeDtypeStruct((S, D), F32))(z, z, z, fvec)


def _att_bwd(z, fvec, ob, dob, name):
    S = z.shape[0]

    def body(q_ref, k_ref, v_ref, f_ref, o_ref, do_ref, dq_ref, dk_ref, dv_ref, db_ref):
        i = pl.program_id(1)
        q, kw, ws, p = _att_probs(q_ref, k_ref, f_ref, i)
        do = do_ref[...]
        dob16 = _mx(do)
        dp = _dot_nt(dob16, _mx(v_ref[pl.ds(ws, KWIN), :]))
        ds = p * (dp - _rowsum(do * o_ref[...]))
        dsb = _mx(ds)
        dq_ref[...] = (_dot(dsb, kw) * (HD ** -0.5)).astype(dq_ref.dtype)

        @pl.when(i == 0)
        def _():
            dk_ref[...] = jnp.zeros_like(dk_ref)
            dv_ref[...] = jnp.zeros_like(dv_ref)

        dk_ref[pl.ds(ws, KWIN), :] += _dot_tn(dsb, q)
        dv_ref[pl.ds(ws, KWIN), :] += _dot_tn(_mx(p), dob16)

        @pl.when(i <= 2)
        def _():
            db_ref[0] = ds

        @pl.when(i > 2)
        def _():
            db_ref[0] += ds

    q, k, v, f, tok = _att_specs(S)
    acc = pl.BlockSpec((S, HD), lambda h, i: (0, h))
    return _call(
        body, name, (NH, S // QB), [q, k, v, f, tok, tok],
        [tok, acc, acc, pl.BlockSpec((1, QB, KWIN), lambda h, i: (jnp.maximum(2 - i, 0) * NH + h, 0, 0))],
        [jax.ShapeDtypeStruct((S, D), MXU), jax.ShapeDtypeStruct((S, D), F32),
         jax.ShapeDtypeStruct((S, D), F32), jax.ShapeDtypeStruct((3 * NH, QB, KWIN), F32)],
    )(z, z, z, fvec, ob, dob)


def _bias_fold(dbias, onehot, name):
    def body(db_ref, e_ref, o_ref):
        j = pl.program_id(0)
        h = j % NH
        x = jnp.concatenate([db_ref[0], jnp.zeros((QB, FW - KWIN), F32)], axis=1)
        half = QB // 2
        while half >= 8:
            x = x[:half] + pltpu.roll(x[half:2 * half], FW - half, 1)
            half //= 2
        df = jnp.zeros((1, FW), F32)
        for r in range(8):
            df = df + pltpu.roll(x[r:r + 1], 255 - r, 1)
        contrib = _dot(df, e_ref[0], HI)
        rowh = lax.broadcasted_iota(jnp.int32, (NH, 640), 0)

        @pl.when(j == 0)
        def _():
            o_ref[...] = jnp.zeros_like(o_ref)

        o_ref[...] += jnp.where(rowh == h, contrib, 0.0)

    return _call(
        body, name, (3 * NH,),
        [pl.BlockSpec((1, QB, KWIN), lambda j: (j, 0, 0)),
         pl.BlockSpec((1, FW, 640), lambda j: (j // NH, 0, 0))],
        _full((NH, 640)), jax.ShapeDtypeStruct((NH, 640), F32),
    )(dbias, onehot)


ADA_SHARD = 6 * D // NDEV


def _ada_mod(c_all, w_ada, b_shard, name):
    def body(c_ref, w_ref, b_ref, o_ref):
        cv = c_ref[...]
        ca = cv * _sigmoid(cv)
        o_ref[0] = _dot(_mx(ca), _mx(w_ref[0])) + b_ref[0]

    return _call(
        body, name, (DEPTH,),
        [_full((NDEV, D)), pl.BlockSpec((1, D, ADA_SHARD), lambda l: (l, 0, 0)),
         pl.BlockSpec((1, 1, ADA_SHARD), lambda l: (l, 0, 0))],
        pl.BlockSpec((1, NDEV, ADA_SHARD), lambda l: (l, 0, 0)),
        jax.ShapeDtypeStruct((DEPTH, NDEV, ADA_SHARD), F32),
    )(c_all, w_ada, b_shard.reshape(DEPTH, 1, ADA_SHARD))


def _adam(g, w, m, v):
    m = ADAM_B1 * m + (1.0 - ADAM_B1) * g
    v = ADAM_B2 * v + (1.0 - ADAM_B2) * jnp.square(g)
    m_hat = m / (1.0 - ADAM_B1 ** ADAM_STEP)
    v_hat = v / (1.0 - ADAM_B2 ** ADAM_STEP)
    delta = -ADAM_LR * (m_hat / (jnp.sqrt(v_hat) + ADAM_EPS) + ADAM_WD * w)
    return delta, m, v


def _wada_adamw(c_all_t, dmod, w, m, v, name):
    def body(c_ref, d_ref, w_ref, m_ref, v_ref, g_ref, dl_ref, mo_ref, vo_ref):
        cv = c_ref[...]
        ca = cv * _sigmoid(cv)
        g = _dot(ca, d_ref[0], HI)
        g_ref[0] = g
        dl_ref[0], mo_ref[0], vo_ref[0] = _adam(g, w_ref[0], m_ref[0], v_ref[0])

    blk = pl.BlockSpec((1, D, ADA_SHARD), lambda l: (l, 0, 0))
    shp = jax.ShapeDtypeStruct((DEPTH, D, ADA_SHARD), F32)
    return _call(
        body, name, (DEPTH,),
        [_full((D, NDEV)), pl.BlockSpec((1, NDEV, ADA_SHARD), lambda l: (l, 0, 0)), blk, blk, blk],
        [blk] * 4, [shp] * 4,
    )(c_all_t, dmod, w, m, v)


def _adamw_reduce(parts, w, m, v, name, tr):
    P, R, C = parts.shape

    def body(p_ref, w_ref, m_ref, v_ref, g_ref, dl_ref, mo_ref, vo_ref):
        g = p_ref[0].astype(F32)
        for k in range(1, P):
            g = g + p_ref[k].astype(F32)
        g_ref[...] = g
        dl_ref[...], mo_ref[...], vo_ref[...] = _adam(g, w_ref[...], m_ref[...], v_ref[...])

    blk = pl.BlockSpec((tr, C), lambda i: (i, 0))
    shp = jax.ShapeDtypeStruct((R, C), F32)
    return _call(body, name, (R // tr,), [pl.BlockSpec((P, tr, C), lambda i: (0, i, 0)), blk, blk, blk],
                 [blk] * 4, [shp] * 4)(parts, w, m, v)


def _sum_parts(parts, name):
    P, R, C = parts.shape

    def body(p_ref, o_ref):
        g = p_ref[0]
        for k in range(1, P):
            g = g + p_ref[k]
        o_ref[...] = g

    return _call(body, name, (1,), [_full((P, R, C))], _full((R, C)),
                 jax.ShapeDtypeStruct((R, C), F32))(parts)


def _pack_rows(vecs, width=1024):
    flat = jnp.concatenate([a.reshape(-1) for a in vecs])
    n = flat.shape[0]
    rows = -(-n // width)
    rows = -(-rows // 8) * 8
    return jnp.pad(flat, (0, rows * width - n)).reshape(rows, width)


def _unpack_rows(packed, shapes):
    flat = packed.reshape(-1)
    out, off = [], 0
    for s in shapes:
        n = int(np.prod(s)) if len(s) else 1
        out.append(flat[off:off + n].reshape(s))
        off += n
    return out


def _forward_layer(x, mod_l, p):
    sh1, sc1, gt1, sh2, sc2, gt2 = [mod_l[k][None] for k in range(6)]
    h = _modnorm_fwd(x, p["norm_mix"], sc1, sh1, "norm_mix_fwd")
    z = _mm_nn(h, p["wz"], "in_proj", tn=512)
    qn, kn, v, gb = _gdn_prep_fwd(z, p["conv_w"], p["alog"], p["dtb"], "gdn_prep_fwd")
    u, w, qd, kd, attn, tmat, cdt = _gdn_intra_fwd(qn, kn, v, gb, "gdn_intra_fwd")
    o, vn, st = _gdn_scan_fwd(u, w, qd, kd, attn, cdt, "gdn_scan_fwd")
    ob = _att_fwd(z, p["fvec"], "att_fwd")
    m = _merge_fwd(o, z, ob, p["gdn_norm"], "merge_fwd")
    x1 = _mm_nn(m, p["wout"], "out_proj", mode="resid", res=x, gate=gt1)
    h2 = _modnorm_fwd(x1, p["norm_mlp"], sc2, sh2, "norm_mlp_fwd")
    a, r = _mm_nn(h2, p["w1"], "ff_in", mode="relu2")
    x2 = _mm_nn(r, p["w2"], "ff_out", mode="resid", res=x1, gate=gt2)
    saved = dict(x=x, h=h, z=z, qn=qn, kn=kn, v=v, gb=gb, u=u, w=w, qd=qd, kd=kd, attn=attn,
                 tmat=tmat, cdt=cdt, o=o, vn=vn, st=st, ob=ob, m=m, x1=x1, h2=h2, a=a, r=r)
    return x2, saved


def _backward_layer(dx2, mod_l, p, s, onehot):
    sh1, sc1, gt1, sh2, sc2, gt2 = [mod_l[k][None] for k in range(6)]
    dw2, dgt2 = _mm_tn(s["r"], dx2, "ff_out_dw", gate=gt2, w=p["w2"])
    da = _mm_nt(dx2, p["w2"], "ff_out_dx", gate=gt2, drelu=s["a"])
    dw1 = _mm_tn(s["h2"], da, "ff_in_dw")
    dh2 = _mm_nt(da, p["w1"], "ff_in_dx")
    dx1, dsc2, dsh2, dnmlp = _modnorm_bwd(dh2, s["x1"], p["norm_mlp"], sc2, sh2, dx2, "norm_mlp_bwd")
    dwout, dgt1 = _mm_tn(s["m"], dx1, "out_proj_dw", gate=gt1, w=p["wout"])
    dm = _mm_nt(dx1, p["wout"], "out_proj_dx", gate=gt1)
    do, dzg, dob, dza, dzb, dgn = _merge_bwd(dm, s["o"], s["z"], s["ob"], p["gdn_norm"], "merge_bwd")
    dq_att, dk_att, dv_att, dbias = _att_bwd(s["z"], p["fvec"], s["ob"], dob, "att_bwd")
    drb = _bias_fold(dbias, onehot, "rel_bias_fold")[:, :513]
    dqd, dkd, dvn, dw, dattn, dcdt = _gdn_scan_bwd(do, s["w"], s["qd"], s["kd"], s["attn"], s["cdt"],
                                                   s["vn"], s["st"], "gdn_scan_bwd")
    dqn, dkn, dv, dgb = _gdn_intra_bwd(s["qn"], s["kn"], s["v"], s["gb"], s["u"], s["w"], s["tmat"],
                                       dqd, dkd, dvn, dw, dattn, dcdt, "gdn_intra_bwd")
    dzq, dzab, dcw, dvec = _gdn_prep_bwd(s["z"], dqn, dkn, dv, dgb, p["conv_w"], p["alog"], p["dtb"],
                                         "gdn_prep_bwd")
    dz = jnp.concatenate([dzq, dzg, dq_att, dk_att.astype(MXU), dv_att.astype(MXU), dza, dzb, dzab], axis=1)
    dwz = _mm_tn(s["h"], dz, "in_proj_dw", tn=512)
    dh = _mm_nt(dz, p["wz"], "in_proj_dx", tn=512)
    dx, dsc1, dsh1, dnmix = _modnorm_bwd(dh, s["x"], p["norm_mix"], sc1, sh1, dx1, "norm_mix_bwd")
    dw_in = jnp.concatenate([dwz[:, :Z_ATT], dwz[:, Z_AB:Z_AB + 2 * NH], dwz[:, Z_ATT:Z_AB]], axis=1)
    grads = dict(w_in=dw_in, w_out=dwout, w_ff_in=dw1, w_ff_out=dw2, norm_mix=dnmix[0], norm_mlp=dnmlp[0],
                 conv_w=dcw[:4], a_log=dvec[0, :NH], dt_bias=dvec[1, :NH], gdn_norm=dgn[0], rel_bias=drb,
                 mod=jnp.concatenate([dsh1, dsc1, dgt1, dsh2, dsc2, dgt2], axis=1)[0])
    return dx, grads


def _bias_onehot():
    return (jnp.asarray(_bias_index())[:, :, None] == jnp.arange(640)[None, None, :]).astype(F32)


def _layer_params(l, wz, wout, w1, w2, conv_full, norm_mix, norm_mlp, a_log, dt_bias, gdn_norm, rel_bias, onehot):
    pad = lambda a: jnp.pad(a, (0, 128 - NH))[None]
    fvec = _bias_vec(jnp.pad(rel_bias[l], ((0, 0), (0, 640 - rel_bias.shape[2]))), onehot, "rel_bias_vec")
    return dict(wz=wz[l], wout=wout[l], w1=w1[l], w2=w2[l], conv_w=conv_full[l], norm_mix=norm_mix[l][None],
                norm_mlp=norm_mlp[l][None], alog=pad(a_log[l]), dtb=pad(dt_bias[l]), gdn_norm=gdn_norm[l][None],
                fvec=fvec)


def _local_step(x, target, mod, params, final_norm, onehot):
    saved = []
    for l in range(len(params)):
        x, s = _forward_layer(x, mod[l].reshape(6, D), params[l])
        saved.append(s)
    loss, dx, dfn = _loss_head(x, target, final_norm[None], "loss_head")
    grads = [None] * len(params)
    for l in reversed(range(len(params))):
        dx, grads[l] = _backward_layer(dx, mod[l].reshape(6, D), params[l], saved[l], onehot)
    return loss, dx, grads, dfn[0]


SMALL = ("b_ada", "norm_mix", "norm_mlp", "a_log", "dt_bias", "gdn_norm", "rel_bias", "final_norm")


def kernel(x, c, w_ada, b_ada, norm_mix, norm_mlp, w_in, conv_w, a_log, dt_bias, gdn_norm, rel_bias, w_out, w_ff_in, w_ff_out, final_norm, loss_target, m_w_ada, m_b_ada, m_norm_mix, m_norm_mlp, m_w_in, m_conv_w, m_a_log, m_dt_bias, m_gdn_norm, m_rel_bias, m_w_out, m_w_ff_in, m_w_ff_out, m_final_norm, v_w_ada, v_b_ada, v_norm_mix, v_norm_mlp, v_w_in, v_conv_w, v_a_log, v_dt_bias, v_gdn_norm, v_rel_bias, v_w_out, v_w_ff_in, v_w_ff_out, v_final_norm):
    W = dict(w_ada=w_ada, b_ada=b_ada, norm_mix=norm_mix, norm_mlp=norm_mlp, w_in=w_in, conv_w=conv_w,
             a_log=a_log, dt_bias=dt_bias, gdn_norm=gdn_norm, rel_bias=rel_bias, w_out=w_out,
             w_ff_in=w_ff_in, w_ff_out=w_ff_out, final_norm=final_norm)
    Mo = dict(w_ada=m_w_ada, b_ada=m_b_ada, norm_mix=m_norm_mix, norm_mlp=m_norm_mlp, w_in=m_w_in,
              conv_w=m_conv_w, a_log=m_a_log, dt_bias=m_dt_bias, gdn_norm=m_gdn_norm, rel_bias=m_rel_bias,
              w_out=m_w_out, w_ff_in=m_w_ff_in, w_ff_out=m_w_ff_out, final_norm=m_final_norm)
    Vo = dict(w_ada=v_w_ada, b_ada=v_b_ada, norm_mix=v_norm_mix, norm_mlp=v_norm_mlp, w_in=v_w_in,
              conv_w=v_conv_w, a_log=v_a_log, dt_bias=v_dt_bias, gdn_norm=v_gdn_norm, rel_bias=v_rel_bias,
              w_out=v_w_out, w_ff_in=v_w_ff_in, w_ff_out=v_w_ff_out, final_norm=v_final_norm)
    L = w_in.shape[0]
    me = _flat(_mesh_pos())
    cshard = conv_w.shape[2]

    small_in = _all_gather(_pack_rows([c, conv_w]), "gather_c_conv")
    c_all = small_in[:, 0, :]
    conv_full = small_in.reshape(NDEV, -1)[:, D:D + L * 4 * cshard].reshape(NDEV, L, 4, cshard)
    conv_full = jnp.transpose(conv_full, (1, 2, 0, 3)).reshape(L, 4, NDEV * cshard)

    b_shard = lax.dynamic_slice_in_dim(b_ada, me * ADA_SHARD, ADA_SHARD, axis=1)
    mod_all = _all_gather(_ada_mod(c_all, w_ada, b_shard, "ada_mod"), "gather_mod")
    mod = lax.dynamic_index_in_dim(mod_all, me, axis=2, keepdims=False)
    mod = jnp.transpose(mod, (1, 0, 2)).reshape(L, 6 * D)

    g_in = _all_gather(w_in.astype(MXU), "gather_w_in")
    w_in_full = jnp.transpose(g_in, (1, 2, 0, 3)).reshape(L, D, IN_W)
    wz = jnp.concatenate([w_in_full[..., :Z_ATT], w_in_full[..., Z_ATT + 2 * NH:],
                          w_in_full[..., Z_ATT:Z_ATT + 2 * NH],
                          jnp.zeros((L, D, ZW - IN_W), MXU)], axis=-1)
    wout = jnp.transpose(_all_gather(w_out.astype(MXU), "gather_w_out"), (1, 0, 2, 3)).reshape(L, D, D)
    w1 = jnp.transpose(_all_gather(w_ff_in.astype(MXU), "gather_w_ff_in"), (1, 2, 0, 3)).reshape(L, D, DFF)
    w2 = jnp.transpose(_all_gather(w_ff_out.astype(MXU), "gather_w_ff_out"), (1, 0, 2, 3)).reshape(L, DFF, D)

    onehot = _bias_onehot()
    params = [_layer_params(l, wz, wout, w1, w2, conv_full, norm_mix, norm_mlp, a_log, dt_bias, gdn_norm,
                            rel_bias, onehot) for l in range(L)]
    loss, dx, grads, dfn = _local_step(x[0], loss_target[0], mod, params, final_norm, onehot)

    def stack(name):
        return jnp.stack([g[name] for g in grads])

    small_names = ("mod", "norm_mix", "norm_mlp", "a_log", "dt_bias", "gdn_norm", "rel_bias")
    small_parts = [stack(n) for n in small_names] + [dfn, stack("conv_w"), loss[0, 0:1]]
    small_shapes = [a.shape for a in small_parts]
    gathered = _all_gather(_pack_rows(small_parts), "gather_small_grads")
    total = _unpack_rows(_sum_parts(gathered, "sum_small_grads"), small_shapes)
    tot = dict(zip(small_names + ("final_norm", "conv_w", "loss"), total))
    tot["b_ada"] = tot.pop("mod")
    tot["conv_w"] = lax.dynamic_slice_in_dim(tot["conv_w"], me * cshard, cshard, axis=2)

    out_g, out_d, out_m, out_v = {}, {}, {}, {}
    names = SMALL + ("conv_w",)
    shapes = [W[n].shape for n in names]
    packed = [_pack_rows([src[n] for n in names])[None] if src is tot else _pack_rows([src[n] for n in names])
              for src in (tot, W, Mo, Vo)]
    res = _adamw_reduce(*packed, "adamw_small", tr=8)
    for dst, arr in zip((out_g, out_d, out_m, out_v), res):
        dst.update(zip(names, _unpack_rows(arr, shapes)))

    dmod_all = gathered.reshape(NDEV, -1)[:, :L * 6 * D].reshape(NDEV, L, 6 * D)
    dmod_mine = jnp.transpose(lax.dynamic_slice_in_dim(dmod_all, me * ADA_SHARD, ADA_SHARD, axis=2), (1, 0, 2))
    res = _wada_adamw(jnp.transpose(c_all), dmod_mine, w_ada, m_w_ada, v_w_ada, "adamw_w_ada")
    for dst, arr in zip((out_g, out_d, out_m, out_v), res):
        dst["w_ada"] = arr

    def exchange(name, per_dev, tr):
        recv = _all_to_all(per_dev, "exchange_" + name)
        sh = W[name].shape
        rows = int(np.prod(sh[:-1]))
        flat = lambda a: a.reshape(rows, sh[-1])
        res = _adamw_reduce(recv.reshape(NDEV, rows, sh[-1]), flat(W[name]), flat(Mo[name]), flat(Vo[name]),
                            "adamw_" + name, tr=tr)
        for dst, arr in zip((out_g, out_d, out_m, out_v), res):
            dst[name] = arr.reshape(sh)

    col = lambda a: jnp.transpose(a.reshape(a.shape[0], a.shape[1], NDEV, -1), (2, 0, 1, 3))
    row = lambda a: jnp.transpose(a.reshape(a.shape[0], NDEV, -1, a.shape[2]), (1, 0, 2, 3))
    exchange("w_in", col(stack("w_in").astype(MXU)), 256)
    exchange("w_out", row(stack("w_out").astype(MXU)), 128)
    exchange("w_ff_in", col(stack("w_ff_in").astype(MXU)), 256)
    exchange("w_ff_out", row(stack("w_ff_out").astype(MXU)), 256)

    order = ("w_ada", "b_ada", "norm_mix", "norm_mlp", "w_in", "conv_w", "a_log", "dt_bias", "gdn_norm",
             "rel_bias", "w_out", "w_ff_in", "w_ff_out", "final_norm")
    return (tot["loss"].reshape(()), dx[None], *[out_g[n] for n in order], *[out_d[n] for n in order],
            *[out_m[n] for n in order], *[out_v[n] for n in order])
```

```python
import functools
import math

import numpy as np
import jax
import jax.numpy as jnp
from jax import lax
from jax.experimental import pallas as pl
from jax.experimental.pallas import tpu as pltpu

F32 = jnp.float32
MXU = jnp.bfloat16
HI = lax.Precision.HIGHEST
MESH_ID = pl.DeviceIdType.MESH

D = 1024
NH = 8
HD = 128
CH = 64
PAST = 8
DFF = 4096
EPS = 1e-6
NDEV = 8
DEPTH = 4
IN_W = 9232
ZW = 9728
Z_GATE, Z_ATT, Z_BR, Z_AB = 3072, 4096, 7168, 9216
QB = 256
KWIN = 768
FW = 1024
ADAM_LR, ADAM_B1, ADAM_B2, ADAM_EPS, ADAM_WD, ADAM_STEP = 0.001, 0.9, 0.999, 1e-08, 0.01, 10


def _dot(a, b, prec=None):
    return jnp.dot(a, b, preferred_element_type=F32, precision=prec)


def _dot_nt(a, b, prec=None):
    return lax.dot_general(a, b, (((1,), (1,)), ((), ())), preferred_element_type=F32, precision=prec)


def _dot_tn(a, b, prec=None):
    return lax.dot_general(a, b, (((0,), (0,)), ((), ())), preferred_element_type=F32, precision=prec)


def _mx(a):
    return a.astype(MXU)


def _sigmoid(x):
    return 1.0 / (1.0 + jnp.exp(-x))


def _softplus(x):
    return jnp.maximum(x, 0.0) + jnp.log(1.0 + jnp.exp(-jnp.abs(x)))


def _rowsum(x):
    return jnp.sum(x, axis=1, keepdims=True)


def _colsum(x):
    return jnp.sum(x, axis=0, keepdims=True)


def _call(body, name, grid, in_specs, out_specs, out_shape, scratch=(), comm=None):
    if comm is None:
        return pl.pallas_call(body, name=name, grid=grid, in_specs=in_specs, out_specs=out_specs,
                              out_shape=out_shape, scratch_shapes=list(scratch))
    kind, x = comm
    single = not isinstance(out_specs, (list, tuple))
    o_specs = [out_specs] if single else list(out_specs)
    o_shape = [out_shape] if single else list(out_shape)
    n_in, n_out, n_scr = len(in_specs), len(o_specs), len(scratch)
    c_shape = (NDEV,) + x.shape if kind == "gather" else x.shape

    def wrapped(*refs):
        ins, x_ref = refs[:n_in], refs[n_in]
        outs, c_ref = refs[n_in + 1:n_in + 1 + n_out], refs[n_in + 1 + n_out]
        scr = refs[n_in + 2 + n_out:n_in + 2 + n_out + n_scr]
        sems = refs[n_in + 2 + n_out + n_scr:]
        first = functools.reduce(jnp.logical_and, [pl.program_id(a) == 0 for a in range(len(grid))])
        last = functools.reduce(jnp.logical_and, [pl.program_id(a) == grid[a] - 1 for a in range(len(grid))])

        @pl.when(first)
        def _():
            _comm_start(*_comm_copies(kind, x_ref, c_ref, *sems))

        body(*ins, *outs, *scr)

        @pl.when(last)
        def _():
            _comm_wait(*_comm_copies(kind, x_ref, c_ref, *sems))

    any_spec = pl.BlockSpec(memory_space=pl.ANY)
    call = pl.pallas_call(
        wrapped, name=name, grid=grid, in_specs=list(in_specs) + [any_spec], out_specs=o_specs + [any_spec],
        out_shape=o_shape + [jax.ShapeDtypeStruct(c_shape, x.dtype)],
        scratch_shapes=list(scratch) + _comm_sems())

    def run(*args):
        res = call(*args, x)
        return (res[0] if single else list(res[:-1])), res[-1]

    return run


def _hosted(fn, comm):
    return (fn(), None) if comm is None else fn(comm=comm)


def _full(shape):
    n = len(shape)
    return pl.BlockSpec(shape, lambda *_: (0,) * n)


def _mesh_pos():
    return lax.axis_index("x"), lax.axis_index("y"), lax.axis_index("c")


def _peer(pos, k):
    x, y, c = pos
    return (x ^ ((k >> 2) & 1), y ^ ((k >> 1) & 1), c ^ (k & 1))


def _flat(pos):
    return 4 * pos[0] + 2 * pos[1] + pos[2]


def _comm_sems():
    return [pltpu.SemaphoreType.DMA((NDEV - 1,)), pltpu.SemaphoreType.DMA((NDEV - 1,)), pltpu.SemaphoreType.DMA]


def _comm_copies(kind, x_ref, out_ref, send_sems, recv_sems, local_sem):
    pos = _mesh_pos()
    me = _flat(pos)
    src = (lambda d: x_ref) if kind == "gather" else (lambda d: x_ref.at[d])
    mine = pltpu.make_async_copy(src(me), out_ref.at[me], local_sem)
    sends, recvs = [], []
    for k in range(1, NDEV):
        peer = _peer(pos, k)
        pid = _flat(peer)
        sems = dict(send_sem=send_sems.at[k - 1], recv_sem=recv_sems.at[k - 1], device_id=peer,
                    device_id_type=MESH_ID)
        sends.append(pltpu.make_async_remote_copy(src_ref=src(pid), dst_ref=out_ref.at[me], **sems))
        recvs.append(pltpu.make_async_remote_copy(src_ref=src(pid), dst_ref=out_ref.at[pid], **sems))
    return mine, sends, recvs


def _comm_start(mine, sends, recvs):
    mine.start()
    for cp in sends:
        cp.start()


def _comm_wait(mine, sends, recvs):
    for cp in recvs:
        cp.wait_recv()
    for cp in sends:
        cp.wait_send()
    mine.wait()


def _collective(kind, x, name):
    def body(x_ref, out_ref, *sems):
        copies = _comm_copies(kind, x_ref, out_ref, *sems)
        _comm_start(*copies)
        _comm_wait(*copies)

    shape = (NDEV,) + x.shape if kind == "gather" else x.shape
    return pl.pallas_call(
        body, name=name, out_shape=jax.ShapeDtypeStruct(shape, x.dtype),
        in_specs=[pl.BlockSpec(memory_space=pl.ANY)], out_specs=pl.BlockSpec(memory_space=pl.ANY),
        scratch_shapes=_comm_sems())(x)


def _all_gather(x, name):
    return _collective("gather", x, name)


def _all_to_all(x, name):
    return _collective("a2a", x, name)


def _mm_nn(a, w, name, *, mode="plain", res=None, gate=None, tm=1024, tn=1024, tk=1024, comm=None):
    M, K = a.shape
    N = w.shape[1]
    tm, tk, tn = min(tm, M), min(tk, K), min(tn, N)
    nk = K // tk

    def body(*refs):
        refs = list(refs)
        acc = refs.pop() if nk > 1 else None
        if mode == "resid":
            a_ref, w_ref, res_ref, gate_ref, o_ref = refs
        elif mode == "relu2":
            a_ref, w_ref, o_ref, r_ref = refs
        else:
            a_ref, w_ref, o_ref = refs
        k = pl.program_id(2)
        part = _dot(_mx(a_ref[...]), w_ref[...])

        def finish(r):
            if mode == "resid":
                o_ref[...] = res_ref[...] + gate_ref[...] * r
            elif mode == "relu2":
                o_ref[...] = r
                r_ref[...] = jnp.square(jnp.maximum(r, 0.0)).astype(r_ref.dtype)
            else:
                o_ref[...] = r

        if nk == 1:
            finish(part)
        else:
            @pl.when(k == 0)
            def _():
                acc[...] = part

            @pl.when((k > 0) & (k < nk - 1))
            def _():
                acc[...] += part

            @pl.when(k == nk - 1)
            def _():
                finish(acc[...] + part)

    in_specs = [pl.BlockSpec((tm, tk), lambda i, j, k: (i, k)),
                pl.BlockSpec((tk, tn), lambda i, j, k: (k, j))]
    args = [a, w]
    o_spec = pl.BlockSpec((tm, tn), lambda i, j, k: (i, j))
    out_specs, out_shape = o_spec, jax.ShapeDtypeStruct((M, N), F32)
    if mode == "resid":
        in_specs += [o_spec, pl.BlockSpec((1, tn), lambda i, j, k: (0, j))]
        args += [res, gate]
    elif mode == "relu2":
        out_specs = [o_spec, o_spec]
        out_shape = [jax.ShapeDtypeStruct((M, N), F32), jax.ShapeDtypeStruct((M, N), MXU)]
    return _call(body, name, (M // tm, N // tn, nk), in_specs, out_specs, out_shape,
                 [pltpu.VMEM((tm, tn), F32)] if nk > 1 else [], comm=comm)(*args)


def _mm_nt(a, w, name, *, gate=None, drelu=None, tm=1024, tko=1024, tn=1024):
    M, N = a.shape
    K = w.shape[0]
    tm, tn, tko = min(tm, M), min(tn, N), min(tko, K)
    nn = N // tn

    def body(*refs):
        refs = list(refs)
        acc = refs.pop() if nn > 1 else None
        a_ref, w_ref = refs[:2]
        rest = refs[2:]
        gate_ref = rest.pop(0) if gate is not None else None
        pre_ref = rest.pop(0) if drelu is not None else None
        (o_ref,) = rest
        n = pl.program_id(2)
        av = a_ref[...]
        if gate_ref is not None:
            av = av * gate_ref[...]
        part = _dot_nt(_mx(av), w_ref[...])

        def finish(r):
            if pre_ref is not None:
                r = r * (2.0 * jnp.maximum(pre_ref[...], 0.0))
            o_ref[...] = r.astype(o_ref.dtype)

        if nn == 1:
            finish(part)
        else:
            @pl.when(n == 0)
            def _():
                acc[...] = part

            @pl.when((n > 0) & (n < nn - 1))
            def _():
                acc[...] += part

            @pl.when(n == nn - 1)
            def _():
                finish(acc[...] + part)

    in_specs = [pl.BlockSpec((tm, tn), lambda i, j, n: (i, n)),
                pl.BlockSpec((tko, tn), lambda i, j, n: (j, n))]
    args = [a, w]
    if gate is not None:
        in_specs.append(pl.BlockSpec((1, tn), lambda i, j, n: (0, n)))
        args.append(gate)
    o_spec = pl.BlockSpec((tm, tko), lambda i, j, n: (i, j))
    if drelu is not None:
        in_specs.append(o_spec)
        args.append(drelu)
    out_dtype = MXU if drelu is not None else F32
    return _call(body, name, (M // tm, K // tko, nn), in_specs, o_spec,
                 jax.ShapeDtypeStruct((M, K), out_dtype), [pltpu.VMEM((tm, tko), F32)] if nn > 1 else [])(*args)


def _mm_tn(a, b, name, *, gate=None, w=None, tk=1024, tn=1024, tm=1024, comm=None):
    M, K = a.shape
    N = b.shape[1]
    tm, tk, tn = min(tm, M), min(tk, K), min(tn, N)
    nm = M // tm
    gated = gate is not None

    def body(*refs):
        if gated:
            a_ref, b_ref, gate_ref, w_ref, o_ref, dg_ref, acc = refs
        else:
            a_ref, b_ref, o_ref, acc = refs
        kk = pl.program_id(1)
        m = pl.program_id(2)
        part = _dot_tn(_mx(a_ref[...]), _mx(b_ref[...]))

        @pl.when((m == 0) & (nm > 1))
        def _():
            acc[...] = part

        @pl.when((m > 0) & (m < nm - 1))
        def _():
            acc[...] += part

        if gated:
            @pl.when((m == 0) & (kk == 0))
            def _():
                dg_ref[...] = jnp.zeros_like(dg_ref)

        @pl.when(m == nm - 1)
        def _():
            r = acc[...] + part if nm > 1 else part
            if gated:
                o_ref[...] = r * gate_ref[...]
                dg_ref[...] += _colsum(r * w_ref[...].astype(F32))
            else:
                o_ref[...] = r

    in_specs = [pl.BlockSpec((tm, tk), lambda j, k, m: (m, k)),
                pl.BlockSpec((tm, tn), lambda j, k, m: (m, j))]
    args = [a, b]
    o_spec = pl.BlockSpec((tk, tn), lambda j, k, m: (k, j))
    out_specs, out_shape = o_spec, jax.ShapeDtypeStruct((K, N), F32)
    if gated:
        in_specs += [pl.BlockSpec((1, tn), lambda j, k, m: (0, j)), o_spec]
        args += [gate, w]
        out_specs = [o_spec, pl.BlockSpec((1, tn), lambda j, k, m: (0, j))]
        out_shape = [out_shape, jax.ShapeDtypeStruct((1, N), F32)]
    return _call(body, name, (N // tn, K // tk, nm), in_specs, out_specs, out_shape,
                 [pltpu.VMEM((tk, tn), F32)], comm=comm)(*args)


def _modnorm_fwd(x, gain, sc, sh, name, ts=512):
    S = x.shape[0]

    def body(x_ref, g_ref, sc_ref, sh_ref, h_ref):
        xv = x_ref[...]
        r = lax.rsqrt(jnp.mean(xv * xv, axis=1, keepdims=True) + EPS)
        h_ref[...] = ((xv * r * g_ref[...]) * (1.0 + sc_ref[...]) + sh_ref[...]).astype(h_ref.dtype)

    row = pl.BlockSpec((ts, D), lambda i: (i, 0))
    vec = pl.BlockSpec((1, D), lambda i: (0, 0))
    return _call(body, name, (S // ts,), [row, vec, vec, vec], row,
                 jax.ShapeDtypeStruct((S, D), MXU))(x, gain, sc, sh)


def _modnorm_bwd(dh, x, gain, sc, sh, dx_in, name, ts=512):
    S = x.shape[0]

    def body(dh_ref, x_ref, g_ref, sc_ref, sh_ref, dxin_ref, dx_ref, dsc_ref, dsh_ref, dg_ref):
        i = pl.program_id(0)
        xv = x_ref[...]
        dhv = dh_ref[...]
        g = g_ref[...]
        r = lax.rsqrt(jnp.mean(xv * xv, axis=1, keepdims=True) + EPS)
        xr = xv * r
        dn = dhv * (1.0 + sc_ref[...])
        u = dn * g
        dx_ref[...] = dxin_ref[...] + r * (u - xr * jnp.mean(xr * u, axis=1, keepdims=True))

        @pl.when(i == 0)
        def _():
            dsc_ref[...] = jnp.zeros_like(dsc_ref)
            dsh_ref[...] = jnp.zeros_like(dsh_ref)
            dg_ref[...] = jnp.zeros_like(dg_ref)

        dsc_ref[...] += _colsum(dhv * (xr * g))
        dsh_ref[...] += _colsum(dhv)
        dg_ref[...] += _colsum(dn * xr)

    row = pl.BlockSpec((ts, D), lambda i: (i, 0))
    vec = pl.BlockSpec((1, D), lambda i: (0, 0))
    vshape = jax.ShapeDtypeStruct((1, D), F32)
    return _call(body, name, (S // ts,), [row, row, vec, vec, vec, row], [row, vec, vec, vec],
                 [jax.ShapeDtypeStruct((S, D), F32), vshape, vshape, vshape])(dh, x, gain, sc, sh, dx_in)


def _loss_head(x, target, gain, name, ts=512):
    S = x.shape[0]

    def body(x_ref, t_ref, g_ref, loss_ref, dx_ref, dg_ref):
        i = pl.program_id(0)
        xv = x_ref[...]
        g = g_ref[...]
        r = lax.rsqrt(jnp.mean(xv * xv, axis=1, keepdims=True) + EPS)
        xr = xv * r
        e = xr * g - t_ref[...]
        dy = e * (1.0 / D)
        u = dy * g
        dx_ref[...] = r * (u - xr * jnp.mean(xr * u, axis=1, keepdims=True))

        @pl.when(i == 0)
        def _():
            loss_ref[...] = jnp.zeros_like(loss_ref)
            dg_ref[...] = jnp.zeros_like(dg_ref)

        part = 0.5 * jnp.sum(jnp.mean(e * e, axis=1, keepdims=True), axis=0, keepdims=True)
        loss_ref[...] += jnp.broadcast_to(part, loss_ref.shape)
        dg_ref[...] += _colsum(dy * xr)

    row = pl.BlockSpec((ts, D), lambda i: (i, 0))
    vec = pl.BlockSpec((1, D), lambda i: (0, 0))
    return _call(body, name, (S // ts,), [row, row, vec],
                 [pl.BlockSpec((1, 128), lambda i: (0, 0)), row, vec],
                 [jax.ShapeDtypeStruct((1, 128), F32), jax.ShapeDtypeStruct((S, D), F32),
                  jax.ShapeDtypeStruct((1, D), F32)])(x, target, gain)


def _merge_specs(ts):
    o_spec = pl.BlockSpec((NH, ts, HD), lambda i: (0, i, 0))
    zg = pl.BlockSpec((ts, D), lambda i: (i, Z_GATE // D))
    za = pl.BlockSpec((ts, D), lambda i: (i, Z_BR // D))
    zb = pl.BlockSpec((ts, D), lambda i: (i, Z_BR // D + 1))
    row = pl.BlockSpec((ts, D), lambda i: (i, 0))
    gn = pl.BlockSpec((1, HD), lambda i: (0, 0))
    return o_spec, zg, za, zb, row, gn


def _merge_fwd(o, z, ob, gn, name, ts=256):
    S = ob.shape[0]

    def body(o_ref, zg_ref, za_ref, zb_ref, ob_ref, gn_ref, m_ref):
        for h in range(NH):
            sl = slice(h * HD, (h + 1) * HD)
            oh = o_ref[h]
            r = lax.rsqrt(jnp.mean(oh * oh, axis=1, keepdims=True) + EPS)
            gate = zg_ref[:, sl]
            oa = (oh * r * gn_ref[...]) * (gate * _sigmoid(gate))
            m = _sigmoid(za_ref[:, sl]) * oa + _sigmoid(zb_ref[:, sl]) * ob_ref[:, sl]
            m_ref[:, sl] = m.astype(m_ref.dtype)

    o_spec, zg, za, zb, row, gns = _merge_specs(ts)
    return _call(body, name, (S // ts,), [o_spec, zg, za, zb, row, gns], row,
                 jax.ShapeDtypeStruct((S, D), MXU))(o, z, z, z, ob, gn)


def _merge_bwd(dm, o, z, ob, gn, name, ts=256):
    S = ob.shape[0]

    def body(dm_ref, o_ref, zg_ref, za_ref, zb_ref, ob_ref, gn_ref,
             do_ref, dzg_ref, dob_ref, dza_ref, dzb_ref, dgn_ref):
        i = pl.program_id(0)
        gn_v = gn_ref[...]
        dgn = jnp.zeros((1, HD), F32)
        for h in range(NH):
            sl = slice(h * HD, (h + 1) * HD)
            dmh = dm_ref[:, sl]
            oh = o_ref[h]
            r = lax.rsqrt(jnp.mean(oh * oh, axis=1, keepdims=True) + EPS)
            ohr = oh * r
            on = ohr * gn_v
            gate = zg_ref[:, sl]
            sg = _sigmoid(gate)
            silu = gate * sg
            oa = on * silu
            ga = _sigmoid(za_ref[:, sl])
            gb = _sigmoid(zb_ref[:, sl])
            obh = ob_ref[:, sl]
            doa = dmh * ga
            dob_ref[:, sl] = dmh * gb
            dza_ref[:, sl] = (dmh * oa * ga * (1.0 - ga)).astype(dza_ref.dtype)
            dzb_ref[:, sl] = (dmh * obh * gb * (1.0 - gb)).astype(dzb_ref.dtype)
            don = doa * silu
            dzg_ref[:, sl] = (doa * on * (sg * (1.0 + gate * (1.0 - sg)))).astype(dzg_ref.dtype)
            dgn = dgn + _colsum(don * ohr)
            u = don * gn_v
            do_ref[h] = r * (u - ohr * jnp.mean(ohr * u, axis=1, keepdims=True))

        @pl.when(i == 0)
        def _():
            dgn_ref[...] = jnp.zeros_like(dgn_ref)

        dgn_ref[...] += dgn

    o_spec, zg, za, zb, row, gns = _merge_specs(ts)
    return _call(
        body, name, (S // ts,), [row, o_spec, zg, za, zb, row, gns],
        [o_spec, row, row, row, row, gns],
        [jax.ShapeDtypeStruct((NH, S, HD), F32), jax.ShapeDtypeStruct((S, D), MXU),
         jax.ShapeDtypeStruct((S, D), F32), jax.ShapeDtypeStruct((S, D), MXU),
         jax.ShapeDtypeStruct((S, D), MXU), jax.ShapeDtypeStruct((1, HD), F32)],
    )(dm, o, z, z, z, ob, gn)


GROWS = 256
GCH = GROWS // CH


def _gdn_prep_fwd(z, conv_w, alog_row, dtb_row, name):
    S = z.shape[0]
    ts = GROWS
    scale = HD ** -0.5

    def body(z_ref, halo_ref, zab_ref, w_ref, al_ref, dt_ref, q_ref, k_ref, v_ref, gb_ref, buf):
        i = pl.program_id(0)
        buf[0:8, :] = jnp.where(i == 0, 0.0, halo_ref[...])
        buf[8:8 + ts, :] = z_ref[...]
        outs = (q_ref, k_ref, v_ref)
        for seg in range(3):
            cs = slice(seg * D, (seg + 1) * D)
            c = jnp.zeros((ts, D), F32)
            for j in range(4):
                c = c + w_ref[j:j + 1, cs] * buf[pl.ds(5 + j, ts), cs]
            s = c * _sigmoid(c)
            if seg == 2:
                outs[seg][...] = s
            else:
                mul = scale if seg == 0 else 1.0
                for h in range(NH):
                    sl = slice(h * HD, (h + 1) * HD)
                    sh = s[:, sl]
                    r = lax.rsqrt(_rowsum(sh * sh) + EPS)
                    outs[seg][:, sl] = sh * (r * mul)
        zab = zab_ref[...]
        lane = lax.broadcasted_iota(jnp.int32, zab.shape, 1)
        g = -jnp.exp(al_ref[...]) * _softplus(zab + dt_ref[...])
        ri = lax.broadcasted_iota(jnp.int32, (CH, CH), 0)
        ci = lax.broadcasted_iota(jnp.int32, (CH, CH), 1)
        incl = (ri >= ci).astype(F32)
        gcum = jnp.concatenate([_dot(incl, g[c * CH:(c + 1) * CH], HI) for c in range(ts // CH)], axis=0)
        gb_ref[...] = jnp.where(lane < NH, gcum, jnp.where(lane < 2 * NH, _sigmoid(zab), 0.0))

    row = pl.BlockSpec((ts, D), lambda i: (i, 0))
    vec = pl.BlockSpec((1, 128), lambda i: (0, 0))
    return _call(
        body, name, (S // ts,),
        [pl.BlockSpec((ts, 3 * D), lambda i: (i, 0)),
         pl.BlockSpec((8, 3 * D), lambda i: (jnp.maximum(i * (ts // 8) - 1, 0), 0)),
         pl.BlockSpec((ts, 128), lambda i: (i, Z_AB // 128)),
         _full((4, 3 * D)), vec, vec],
        [row, row, row, pl.BlockSpec((ts, 128), lambda i: (i, 0))],
        [jax.ShapeDtypeStruct((S, D), F32)] * 3 + [jax.ShapeDtypeStruct((S, 128), F32)],
        [pltpu.VMEM((ts + 8, 3 * D), F32)],
    )(z, z, z, conv_w, alog_row, dtb_row)


def _split(a):
    hi = a.astype(MXU)
    return hi, (a - hi.astype(F32)).astype(MXU)


def _dot3(a, b, dot=_dot):
    ah, al = _split(a)
    bh, bl = _split(b)
    return dot(ah, bh) + (dot(ah, bl) + dot(al, bh))


IROWS = 512
ICH = IROWS // CH


def _chunk_common(gbk, h, k):
    lane = lax.broadcasted_iota(jnp.int32, gbk.shape, 1)
    G = _rowsum(jnp.where(lane == h, gbk, 0.0))
    b_col = _rowsum(jnp.where(lane == h + NH, gbk, 0.0))
    ri = lax.broadcasted_iota(jnp.int32, (CH, CH), 0)
    ci = lax.broadcasted_iota(jnp.int32, (CH, CH), 1)
    incl = ri >= ci
    gc = jnp.broadcast_to(G, (CH, CH))
    decay = jnp.where(incl, jnp.exp(jnp.where(incl, gc - gc.T, 0.0)), 0.0)
    Gl = G[CH - 1:CH, :]
    kb = k * b_col
    return dict(b=b_col, ri=ri, ci=ci, incl=incl, strict=ri > ci, decay=decay, eG=jnp.exp(G),
                e2=jnp.exp(Gl - G), cd=jnp.exp(Gl), kb=kb, kk=_dot_nt(_mx(kb), _mx(k)))


def _gdn_intra_fwd(qn, kn, v, gb, name):
    S = qn.shape[0]

    def body(q_ref, k_ref, v_ref, gb_ref, u_ref, w_ref, qd_ref, kd_ref, at_ref, t_ref, cd_ref):
        h = pl.program_id(1)
        rows = [slice(c * CH, (c + 1) * CH) for c in range(ICH)]
        ks = [k_ref[r, :] for r in rows]
        cms = [_chunk_common(gb_ref[r, :], h, k) for r, k in zip(rows, ks)]
        ps = [jnp.where(cm["strict"], cm["kk"] * cm["decay"], 0.0) for cm in cms]
        ts = [(cm["ri"] == cm["ci"]).astype(F32) - p for cm, p in zip(cms, ps)]
        for _ in range(5):
            ps = [_dot3(p, p) for p in ps]
            ts = [t + _dot3(t, p) for t, p in zip(ts, ps)]
        for c, (r, k, cm, t) in enumerate(zip(rows, ks, cms, ts)):
            rhs = jnp.concatenate([v_ref[r, :] * cm["b"], k * (cm["b"] * cm["eG"])], axis=1)
            sol = _dot3(t, rhs)
            u_ref[0, r, :] = sol[:, :HD]
            w_ref[0, r, :] = sol[:, HD:]
            t_ref[0, r, :] = t
        for c, (r, k, cm) in enumerate(zip(rows, ks, cms)):
            q = q_ref[r, :]
            qk = _dot_nt(_mx(q), _mx(k))
            at_ref[0, r, :] = jnp.where(cm["incl"], qk * cm["decay"], 0.0)
            qd_ref[0, r, :] = q * cm["eG"]
            kd_ref[0, r, :] = k * cm["e2"]
            cd_ref[0, c] = jnp.broadcast_to(cm["cd"], (8, 128))

    tok = pl.BlockSpec((IROWS, HD), lambda i, h: (i, h))
    hm = pl.BlockSpec((1, IROWS, HD), lambda i, h: (h, i, 0))
    hm64 = pl.BlockSpec((1, IROWS, CH), lambda i, h: (h, i, 0))
    big = jax.ShapeDtypeStruct((NH, S, HD), F32)
    sm = jax.ShapeDtypeStruct((NH, S, CH), F32)
    return _call(
        body, name, (S // IROWS, NH),
        [tok, tok, tok, pl.BlockSpec((IROWS, 128), lambda i, h: (i, 0))],
        [hm, hm, hm, hm, hm64, hm64, pl.BlockSpec((1, ICH, 8, 128), lambda i, h: (h, i, 0, 0))],
        [big, big, big, big, sm, sm, jax.ShapeDtypeStruct((NH, S // CH, 8, 128), F32)],
    )(qn, kn, v, gb)


def _scale_state(s, cd_tile):
    return (s.reshape(HD // 8, 8, HD) * cd_tile[None]).reshape(HD, HD)


def _gdn_scan_fwd(u, w, qd, kd, attn, cdt, name):
    S = u.shape[1]
    nblk = S // GROWS

    def body(u_ref, w_ref, qd_ref, kd_ref, at_ref, cd_ref, o_ref, vn_ref, st_ref, s_ref):
        i = pl.program_id(0)

        @pl.when(i == 0)
        def _():
            s_ref[...] = jnp.zeros_like(s_ref)

        def chunk(c, carry):
            r0 = pl.multiple_of(c * CH, CH)
            rows = pl.ds(r0, CH)
            for h in range(NH):
                sh = s_ref[h]
                st_ref[h, c] = sh
                sb = _mx(sh)
                vn = u_ref[h, rows, :] - _dot(_mx(w_ref[h, rows, :]), sb)
                vb = _mx(vn)
                vn_ref[h, rows, :] = vn
                o_ref[h, rows, :] = _dot(_mx(qd_ref[h, rows, :]), sb) + _dot(_mx(at_ref[h, rows, :]), vb)
                s_ref[h] = _scale_state(sh, cd_ref[h, c]) + _dot_tn(_mx(kd_ref[h, rows, :]), vb)
            return carry

        lax.fori_loop(0, GCH, chunk, 0)

    hm = pl.BlockSpec((NH, GROWS, HD), lambda i: (0, i, 0))
    hm64 = pl.BlockSpec((NH, GROWS, CH), lambda i: (0, i, 0))
    big = jax.ShapeDtypeStruct((NH, S, HD), F32)
    return _call(
        body, name, (nblk,),
        [hm, hm, hm, hm, hm64, pl.BlockSpec((NH, GCH, 8, 128), lambda i: (0, i, 0, 0))],
        [hm, hm, pl.BlockSpec((NH, GCH, HD, HD), lambda i: (0, i, 0, 0))],
        [big, big, jax.ShapeDtypeStruct((NH, S // CH, HD, HD), F32)],
        [pltpu.VMEM((NH, HD, HD), F32)],
    )(u, w, qd, kd, attn, cdt)


def _gdn_scan_bwd(do, w, qd, kd, attn, cdt, vn, st, name):
    S = do.shape[1]
    nblk = S // GROWS

    def body(do_ref, w_ref, qd_ref, kd_ref, at_ref, cd_ref, vn_ref, st_ref,
             dqd_ref, dkd_ref, dvn_ref, dw_ref, dat_ref, dcd_ref, ds_ref):
        i = pl.program_id(0)

        @pl.when(i == 0)
        def _():
            ds_ref[...] = jnp.zeros_like(ds_ref)

        def chunk(cc, carry):
            c = GCH - 1 - cc
            r0 = pl.multiple_of(c * CH, CH)
            rows = pl.ds(r0, CH)
            for h in range(NH):
                dsp = ds_ref[h]
                sh = st_ref[h, c]
                dsb, sb = _mx(dsp), _mx(sh)
                dob = _mx(do_ref[h, rows, :])
                vb = _mx(vn_ref[h, rows, :])
                dvn = _dot(_mx(kd_ref[h, rows, :]), dsb) + _dot_tn(_mx(at_ref[h, rows, :]), dob)
                dvb = _mx(dvn)
                dvn_ref[h, rows, :] = dvn
                dqd_ref[h, rows, :] = _dot_nt(dob, sb)
                dat_ref[h, rows, :] = _dot_nt(dob, vb)
                dkd_ref[h, rows, :] = _dot_nt(vb, dsb)
                dw_ref[h, rows, :] = -_dot_nt(dvb, sb)
                dcd = jnp.sum(_rowsum(dsp * sh), axis=0, keepdims=True)
                dcd_ref[h, c] = jnp.broadcast_to(dcd, (8, 128))
                ds_ref[h] = (_scale_state(dsp, cd_ref[h, c]) + _dot_tn(_mx(qd_ref[h, rows, :]), dob)
                             - _dot_tn(_mx(w_ref[h, rows, :]), dvb))
            return carry

        lax.fori_loop(0, GCH, chunk, 0)

    hm = pl.BlockSpec((NH, GROWS, HD), lambda i: (0, nblk - 1 - i, 0))
    hm64 = pl.BlockSpec((NH, GROWS, CH), lambda i: (0, nblk - 1 - i, 0))
    tile = pl.BlockSpec((NH, GCH, 8, 128), lambda i: (0, nblk - 1 - i, 0, 0))
    big = jax.ShapeDtypeStruct((NH, S, HD), F32)
    return _call(
        body, name, (nblk,),
        [hm, hm, hm, hm, hm64, tile, hm, pl.BlockSpec((NH, GCH, HD, HD), lambda i: (0, nblk - 1 - i, 0, 0))],
        [hm, hm, hm, hm, hm64, tile],
        [big, big, big, big, jax.ShapeDtypeStruct((NH, S, CH), F32),
         jax.ShapeDtypeStruct((NH, S // CH, 8, 128), F32)],
        [pltpu.VMEM((NH, HD, HD), F32)],
    )(do, w, qd, kd, attn, cdt, vn, st)


def _gdn_intra_bwd(qn, kn, v, gb, u, w, tmat, dqd, dkd, du, dw, dattn, dcdt, name):
    S = qn.shape[0]

    def body(q_ref, k_ref, v_ref, gb_ref, u_ref, w_ref, t_ref, dqd_ref, dkd_ref, du_ref, dw_ref,
             dat_ref, dcd_ref, dq_ref, dk_ref, dv_ref, dgb_ref):
        h = pl.program_id(1)
        rows = [slice(c * CH, (c + 1) * CH) for c in range(ICH)]
        ks = [k_ref[r, :] for r in rows]
        cms = [_chunk_common(gb_ref[r, :], h, k) for r, k in zip(rows, ks)]
        sols = [jnp.concatenate([u_ref[0, r, :], w_ref[0, r, :]], axis=1) for r in rows]
        drhss = [_dot3(t_ref[0, r, :], jnp.concatenate([du_ref[0, r, :], dw_ref[0, r, :]], axis=1), _dot_tn)
                 for r in rows]
        das = [-_dot3(drhs, sol, _dot_nt) for drhs, sol in zip(drhss, sols)]
        for c, (r, k, cm, drhs, da) in enumerate(zip(rows, ks, cms, drhss, das)):
            q, vv = q_ref[r, :], v_ref[r, :]
            decay, eG, e2, b = cm["decay"], cm["eG"], cm["e2"], cm["b"]
            dru, drw = drhs[:, :HD], drhs[:, HD:]
            dv_ref[r, :] = dru * b
            s_w = _rowsum(drw * k)
            dbeta = _rowsum(dru * vv) + s_w * eG
            deg = s_w * b
            dk = drw * (b * eG)
            dkk = jnp.where(cm["strict"], da * decay, 0.0)
            ddec = jnp.where(cm["strict"], da * cm["kk"], 0.0)
            dkkb = _mx(dkk)
            dkb = _dot(dkkb, _mx(k))
            dk = dk + _dot_tn(dkkb, _mx(cm["kb"])) + dkb * b
            dbeta = dbeta + _rowsum(dkb * k)
            dat = jnp.where(cm["incl"], dat_ref[0, r, :], 0.0)
            qk = _dot_nt(_mx(q), _mx(k))
            dqk = _mx(dat * decay)
            ddec = ddec + dat * qk
            dqd = dqd_ref[0, r, :]
            dkd = dkd_ref[0, r, :]
            dq_ref[r, :] = _dot(dqk, _mx(k)) + dqd * eG
            dk_ref[r, :] = dk + _dot_tn(dqk, _mx(q)) + dkd * e2
            deg = deg + _rowsum(dqd * q)
            t2 = _rowsum(dkd * k) * e2
            dgl = jnp.sum(t2, axis=0, keepdims=True) + dcd_ref[0, c][0:1, 0:1] * cm["cd"]
            dd = ddec * decay
            dG = deg * eG - t2 + _rowsum(dd) - _rowsum(dd.T)
            rowi = lax.broadcasted_iota(jnp.int32, (CH, 1), 0)
            dG = dG + jnp.where(rowi == CH - 1, dgl, 0.0)
            lane = lax.broadcasted_iota(jnp.int32, (CH, 128), 1)
            dgb_ref[0, r, :] = jnp.where(lane == h, dG, 0.0) + jnp.where(lane == h + NH, dbeta, 0.0)

    tok = pl.BlockSpec((IROWS, HD), lambda i, h: (i, h))
    hm = pl.BlockSpec((1, IROWS, HD), lambda i, h: (h, i, 0))
    hm64 = pl.BlockSpec((1, IROWS, CH), lambda i, h: (h, i, 0))
    tile = pl.BlockSpec((1, ICH, 8, 128), lambda i, h: (h, i, 0, 0))
    tokout = jax.ShapeDtypeStruct((S, D), F32)
    return _call(
        body, name, (S // IROWS, NH),
        [tok, tok, tok, pl.BlockSpec((IROWS, 128), lambda i, h: (i, 0)), hm, hm, hm64,
         hm, hm, hm, hm, hm64, tile],
        [tok, tok, tok, pl.BlockSpec((1, IROWS, 128), lambda i, h: (h, i, 0))],
        [tokout, tokout, tokout, jax.ShapeDtypeStruct((NH, S, 128), F32)],
    )(qn, kn, v, gb, u, w, tmat, dqd, dkd, du, dw, dattn, dcdt)


def _gdn_prep_bwd(z, dqn, dkn, dv, dgb, conv_w, alog_row, dtb_row, name):
    S = z.shape[0]
    ts = GROWS
    nblk = S // ts
    scale = HD ** -0.5
    tb = ts // 8

    def body(z_ref, hp_ref, hn_ref, zab_ref, dq_ref, dqn_ref, dk_ref, dkn_ref, dv_ref, dvn_ref,
             dgb_ref, w_ref, al_ref, dt_ref, dz_ref, dzab_ref, dcw_ref, dvec_ref, buf, dybuf, dcbuf):
        i = pl.program_id(0)
        last = i == nblk - 1

        @pl.when(i == 0)
        def _():
            dcw_ref[...] = jnp.zeros_like(dcw_ref)
            dvec_ref[...] = jnp.zeros_like(dvec_ref)

        buf[0:8, :] = jnp.where(i == 0, 0.0, hp_ref[...])
        buf[8:8 + ts, :] = z_ref[...]
        buf[8 + ts:16 + ts, :] = hn_ref[...]
        rowi = lax.broadcasted_iota(jnp.int32, (ts + 8, 1), 0)
        live = jnp.logical_or(rowi < ts, jnp.logical_not(last))
        dys = ((dq_ref, dqn_ref), (dk_ref, dkn_ref), (dv_ref, dvn_ref))
        for seg in range(3):
            cs = slice(seg * D, (seg + 1) * D)
            dybuf[0:ts, :] = dys[seg][0][...]
            dybuf[ts:ts + 8, :] = dys[seg][1][...]
            c = jnp.zeros((ts + 8, D), F32)
            for j in range(4):
                c = c + w_ref[j:j + 1, cs] * buf[pl.ds(5 + j, ts + 8), cs]
            sg = _sigmoid(c)
            s = c * sg
            dsilu = sg * (1.0 + c * (1.0 - sg))
            if seg == 2:
                dcbuf[...] = jnp.where(live, dybuf[...] * dsilu, 0.0)
            else:
                mul = scale if seg == 0 else 1.0
                for h in range(NH):
                    sl = slice(h * HD, (h + 1) * HD)
                    sh = s[:, sl]
                    dy = dybuf[:, sl]
                    r = lax.rsqrt(_rowsum(sh * sh) + EPS)
                    shr = sh * r
                    ds = (mul * r) * (dy - shr * _rowsum(shr * dy))
                    dcbuf[:, sl] = jnp.where(live, ds * dsilu[:, sl], 0.0)
            dx = jnp.zeros((ts, D), F32)
            for j in range(4):
                dcw_ref[j:j + 1, cs] += _colsum(dcbuf[0:ts, :] * buf[pl.ds(5 + j, ts), cs])
                dx = dx + w_ref[j:j + 1, cs] * dcbuf[pl.ds(3 - j, ts), :]
            dz_ref[:, cs] = dx.astype(dz_ref.dtype)
        dgbs = dgb_ref[0]
        for h in range(1, NH):
            dgbs = dgbs + dgb_ref[h]
        ri = lax.broadcasted_iota(jnp.int32, (CH, CH), 0)
        ci = lax.broadcasted_iota(jnp.int32, (CH, CH), 1)
        rev = (ci >= ri).astype(F32)
        dgrev = jnp.concatenate([_dot(rev, dgbs[c * CH:(c + 1) * CH], HI) for c in range(ts // CH)], axis=0)
        lane0 = lax.broadcasted_iota(jnp.int32, dgbs.shape, 1)
        dgbs = jnp.where(lane0 < NH, dgrev, dgbs)
        zab = zab_ref[...]
        lane = lax.broadcasted_iota(jnp.int32, zab.shape, 1)
        xx = zab + dt_ref[...]
        ea = jnp.exp(al_ref[...])
        g = -ea * _softplus(xx)
        da = dgbs * (-ea) * _sigmoid(xx)
        beta = _sigmoid(zab)
        db = dgbs * beta * (1.0 - beta)
        is_a = lane < NH
        dzab = jnp.where(is_a, da, jnp.where(lane < 2 * NH, db, 0.0))
        dzab_ref[:, 0:128] = dzab.astype(dzab_ref.dtype)
        dzab_ref[:, 128:512] = jnp.zeros((ts, 384), dzab_ref.dtype)
        dvec_ref[0:1, :] += _colsum(jnp.where(is_a, dgbs * g, 0.0))
        dvec_ref[1:2, :] += _colsum(jnp.where(is_a, da, 0.0))

    z3 = pl.BlockSpec((ts, 3 * D), lambda i: (i, 0))
    row = pl.BlockSpec((ts, D), lambda i: (i, 0))
    nxt = pl.BlockSpec((8, D), lambda i: (jnp.minimum((i + 1) * tb, S // 8 - 1), 0))
    vec = pl.BlockSpec((1, 128), lambda i: (0, 0))
    return _call(
        body, name, (nblk,),
        [z3,
         pl.BlockSpec((8, 3 * D), lambda i: (jnp.maximum(i * tb - 1, 0), 0)),
         pl.BlockSpec((8, 3 * D), lambda i: (jnp.minimum((i + 1) * tb, S // 8 - 1), 0)),
         pl.BlockSpec((ts, 128), lambda i: (i, Z_AB // 128)),
         row, nxt, row, nxt, row, nxt,
         pl.BlockSpec((NH, ts, 128), lambda i: (0, i, 0)),
         _full((4, 3 * D)), vec, vec],
        [z3, pl.BlockSpec((ts, 512), lambda i: (i, 0)), _full((8, 3 * D)), _full((8, 128))],
        [jax.ShapeDtypeStruct((S, 3 * D), MXU), jax.ShapeDtypeStruct((S, 512), MXU),
         jax.ShapeDtypeStruct((8, 3 * D), F32), jax.ShapeDtypeStruct((8, 128), F32)],
        [pltpu.VMEM((ts + 16, 3 * D), F32), pltpu.VMEM((ts + 8, D), F32), pltpu.VMEM((ts + 8, D), F32)],
    )(z, z, z, z, dqn, dqn, dkn, dkn, dv, dv, dgb, conv_w, alog_row, dtb_row)


def _bias_index():
    u = np.arange(FW)[None, :]
    s = np.arange(3)[:, None]
    return np.clip(KWIN - 1 - u - QB * s, -256, 256) + 256


def _bias_vec(rel_bias_pad, onehot, name):
    def body(rb_ref, e_ref, o_ref):
        o_ref[:, 0, :] = _dot_nt(rb_ref[...], e_ref[0], HI)

    return _call(body, name, (3,),
                 [_full((NH, 640)), pl.BlockSpec((1, FW, 640), lambda s: (s, 0, 0))],
                 pl.BlockSpec((NH, 1, FW), lambda s: (s, 0, 0)),
                 jax.ShapeDtypeStruct((3 * NH, 1, FW), F32))(rel_bias_pad, onehot)


def _att_window(i):
    return pl.multiple_of(jnp.maximum(i * QB - PAST * CH, 0), QB)


def _bias_mask(fvec, name):
    def body(f_ref, o_ref):
        i = 2 - pl.program_id(0) // NH
        ws = jnp.maximum(i * QB - PAST * CH, 0)
        fb = jnp.broadcast_to(f_ref[0], (QB, FW))
        bias = pltpu.roll(fb, FW - 255, 1, stride=1, stride_axis=0)[:, :KWIN]
        qc = (i * QB + lax.broadcasted_iota(jnp.int32, (QB, KWIN), 0)) // CH
        kc = (ws + lax.broadcasted_iota(jnp.int32, (QB, KWIN), 1)) // CH
        o_ref[0] = jnp.where((kc <= qc) & (kc >= qc - PAST), bias, -jnp.inf)

    return _call(body, name, (3 * NH,), [pl.BlockSpec((1, 1, FW), lambda j: (j, 0, 0))],
                 pl.BlockSpec((1, QB, KWIN), lambda j: (j, 0, 0)),
                 jax.ShapeDtypeStruct((3 * NH, QB, KWIN), F32))(fvec)


def _att_probs(q_ref, k_ref, bm_ref, i):
    ws = _att_window(i)
    q = _mx(q_ref[...] * (HD ** -0.5))
    kw = _mx(k_ref[pl.ds(ws, KWIN), :])
    s = _dot_nt(q, kw) + bm_ref[0]
    p = jnp.exp(s - jnp.max(s, axis=1, keepdims=True))
    p = p * (1.0 / _rowsum(p))
    return q, kw, ws, p


def _att_specs(S):
    c0 = Z_ATT // HD
    q = pl.BlockSpec((QB, HD), lambda h, i: (i, c0 + h))
    k = pl.BlockSpec((S, HD), lambda h, i: (0, c0 + NH + h))
    v = pl.BlockSpec((S, HD), lambda h, i: (0, c0 + 2 * NH + h))
    bm = pl.BlockSpec((1, QB, KWIN), lambda h, i: (jnp.maximum(2 - i, 0) * NH + h, 0, 0))
    tok = pl.BlockSpec((QB, HD), lambda h, i: (i, h))
    return q, k, v, bm, tok


def _att_fwd(z, bmask, name):
    S = z.shape[0]

    def body(q_ref, k_ref, v_ref, bm_ref, o_ref):
        i = pl.program_id(1)
        _, _, ws, p = _att_probs(q_ref, k_ref, bm_ref, i)
        o_ref[...] = _dot(_mx(p), _mx(v_ref[pl.ds(ws, KWIN), :]))

    q, k, v, bm, tok = _att_specs(S)
    return _call(body, name, (NH, S // QB), [q, k, v, bm], tok,
                 jax.ShapeDtypeStruct((S, D), F32))(z, z, z, bmask)


def _att_bwd(z, bmask, ob, dob, name, comm=None):
    S = z.shape[0]

    def body(q_ref, k_ref, v_ref, bm_ref, o_ref, do_ref, dq_ref, dk_ref, dv_ref, db_ref):
        i = pl.program_id(1)
        q, kw, ws, p = _att_probs(q_ref, k_ref, bm_ref, i)
        do = do_ref[...]
        dob16 = _mx(do)
        dp = _dot_nt(dob16, _mx(v_ref[pl.ds(ws, KWIN), :]))
        ds = p * (dp - _rowsum(do * o_ref[...]))
        dsb = _mx(ds)
        dq_ref[...] = (_dot(dsb, kw) * (HD ** -0.5)).astype(dq_ref.dtype)

        @pl.when(i == 0)
        def _():
            dk_ref[...] = jnp.zeros_like(dk_ref)
            dv_ref[...] = jnp.zeros_like(dv_ref)

        dk_ref[pl.ds(ws, KWIN), :] += _dot_tn(dsb, q)
        dv_ref[pl.ds(ws, KWIN), :] += _dot_tn(_mx(p), dob16)

        @pl.when(i <= 2)
        def _():
            db_ref[0] = ds

        @pl.when(i > 2)
        def _():
            db_ref[0] += ds

    q, k, v, bm, tok = _att_specs(S)
    acc = pl.BlockSpec((S, HD), lambda h, i: (0, h))
    return _call(
        body, name, (NH, S // QB), [q, k, v, bm, tok, tok], [tok, acc, acc, bm],
        [jax.ShapeDtypeStruct((S, D), MXU), jax.ShapeDtypeStruct((S, D), F32),
         jax.ShapeDtypeStruct((S, D), F32), jax.ShapeDtypeStruct((3 * NH, QB, KWIN), F32)],
        comm=comm,
    )(z, z, z, bmask, ob, dob)


def _bias_fold(dbias, onehot, name):
    def body(db_ref, e_ref, o_ref):
        j = pl.program_id(0)
        h = j % NH
        x = jnp.concatenate([db_ref[0], jnp.zeros((QB, FW - KWIN), F32)], axis=1)
        half = QB // 2
        while half >= 8:
            x = x[:half] + pltpu.roll(x[half:2 * half], FW - half, 1)
            half //= 2
        df = jnp.zeros((1, FW), F32)
        for r in range(8):
            df = df + pltpu.roll(x[r:r + 1], 255 - r, 1)
        contrib = _dot(df, e_ref[0], HI)
        rowh = lax.broadcasted_iota(jnp.int32, (NH, 640), 0)

        @pl.when(j == 0)
        def _():
            o_ref[...] = jnp.zeros_like(o_ref)

        o_ref[...] += jnp.where(rowh == h, contrib, 0.0)

    return _call(
        body, name, (3 * NH,),
        [pl.BlockSpec((1, QB, KWIN), lambda j: (j, 0, 0)),
         pl.BlockSpec((1, FW, 640), lambda j: (j // NH, 0, 0))],
        _full((NH, 640)), jax.ShapeDtypeStruct((NH, 640), F32),
    )(dbias, onehot)


ADA_SHARD = 6 * D // NDEV


def _ada_mod(c_all, w_ada, b_shard, name):
    def body(c_ref, w_ref, b_ref, o_ref):
        cv = c_ref[...]
        ca = cv * _sigmoid(cv)
        o_ref[0] = _dot(_mx(ca), _mx(w_ref[0])) + b_ref[0]

    return _call(
        body, name, (DEPTH,),
        [_full((NDEV, D)), pl.BlockSpec((1, D, ADA_SHARD), lambda l: (l, 0, 0)),
         pl.BlockSpec((1, 1, ADA_SHARD), lambda l: (l, 0, 0))],
        pl.BlockSpec((1, NDEV, ADA_SHARD), lambda l: (l, 0, 0)),
        jax.ShapeDtypeStruct((DEPTH, NDEV, ADA_SHARD), F32),
    )(c_all, w_ada, b_shard.reshape(DEPTH, 1, ADA_SHARD))


def _adam(g, w, m, v):
    m = ADAM_B1 * m + (1.0 - ADAM_B1) * g
    v = ADAM_B2 * v + (1.0 - ADAM_B2) * jnp.square(g)
    m_hat = m / (1.0 - ADAM_B1 ** ADAM_STEP)
    v_hat = v / (1.0 - ADAM_B2 ** ADAM_STEP)
    delta = -ADAM_LR * (m_hat / (jnp.sqrt(v_hat) + ADAM_EPS) + ADAM_WD * w)
    return delta, m, v


def _wada_adamw(c_all_t, dmod, w, m, v, name):
    def body(c_ref, d_ref, w_ref, m_ref, v_ref, g_ref, dl_ref, mo_ref, vo_ref):
        cv = c_ref[...]
        ca = cv * _sigmoid(cv)
        g = _dot(ca, d_ref[0], HI)
        g_ref[0] = g
        dl_ref[0], mo_ref[0], vo_ref[0] = _adam(g, w_ref[0], m_ref[0], v_ref[0])

    blk = pl.BlockSpec((1, D, ADA_SHARD), lambda l: (l, 0, 0))
    shp = jax.ShapeDtypeStruct((DEPTH, D, ADA_SHARD), F32)
    return _call(
        body, name, (DEPTH,),
        [_full((D, NDEV)), pl.BlockSpec((1, NDEV, ADA_SHARD), lambda l: (l, 0, 0)), blk, blk, blk],
        [blk] * 4, [shp] * 4,
    )(c_all_t, dmod, w, m, v)


def _adamw_reduce(parts, w, m, v, name, tr):
    P, R, C = parts.shape

    def body(p_ref, w_ref, m_ref, v_ref, g_ref, dl_ref, mo_ref, vo_ref):
        g = p_ref[0].astype(F32)
        for k in range(1, P):
            g = g + p_ref[k].astype(F32)
        g_ref[...] = g
        dl_ref[...], mo_ref[...], vo_ref[...] = _adam(g, w_ref[...], m_ref[...], v_ref[...])

    blk = pl.BlockSpec((tr, C), lambda i: (i, 0))
    shp = jax.ShapeDtypeStruct((R, C), F32)
    return _call(body, name, (R // tr,), [pl.BlockSpec((P, tr, C), lambda i: (0, i, 0)), blk, blk, blk],
                 [blk] * 4, [shp] * 4)(parts, w, m, v)


def _sum_parts(parts, name):
    P, R, C = parts.shape

    def body(p_ref, o_ref):
        g = p_ref[0]
        for k in range(1, P):
            g = g + p_ref[k]
        o_ref[...] = g

    return _call(body, name, (1,), [_full((P, R, C))], _full((R, C)),
                 jax.ShapeDtypeStruct((R, C), F32))(parts)


def _pack_rows(vecs, width=1024):
    flat = jnp.concatenate([a.reshape(-1) for a in vecs])
    n = flat.shape[0]
    rows = -(-n // width)
    rows = -(-rows // 8) * 8
    return jnp.pad(flat, (0, rows * width - n)).reshape(rows, width)


def _unpack_rows(packed, shapes):
    flat = packed.reshape(-1)
    out, off = [], 0
    for s in shapes:
        n = int(np.prod(s)) if len(s) else 1
        out.append(flat[off:off + n].reshape(s))
        off += n
    return out


BIG = ("w_in", "w_out", "w_ff_in", "w_ff_out")


def _pick(kind, d, n):
    return None if d is None else (kind, d[n])


def _forward_layer(x, mod_l, p, nxt=None):
    sh1, sc1, gt1, sh2, sc2, gt2 = [mod_l[k][None] for k in range(6)]
    P = functools.partial
    h = _modnorm_fwd(x, p["norm_mix"], sc1, sh1, "norm_mix_fwd")
    z, g_in = _hosted(P(_mm_nn, h, p["wz"], "in_proj", tn=512), _pick("gather", nxt, "w_in"))
    qn, kn, v, gb = _gdn_prep_fwd(z, p["conv_w"], p["alog"], p["dtb"], "gdn_prep_fwd")
    u, w, qd, kd, attn, tmat, cdt = _gdn_intra_fwd(qn, kn, v, gb, "gdn_intra_fwd")
    o, vn, st = _gdn_scan_fwd(u, w, qd, kd, attn, cdt, "gdn_scan_fwd")
    ob = _att_fwd(z, p["bmask"], "att_fwd")
    m = _merge_fwd(o, z, ob, p["gdn_norm"], "merge_fwd")
    x1, g_out = _hosted(P(_mm_nn, m, p["wout"], "out_proj", mode="resid", res=x, gate=gt1),
                        _pick("gather", nxt, "w_out"))
    h2 = _modnorm_fwd(x1, p["norm_mlp"], sc2, sh2, "norm_mlp_fwd")
    (a, r), g_w1 = _hosted(P(_mm_nn, h2, p["w1"], "ff_in", mode="relu2"), _pick("gather", nxt, "w_ff_in"))
    x2, g_w2 = _hosted(P(_mm_nn, r, p["w2"], "ff_out", mode="resid", res=x1, gate=gt2),
                       _pick("gather", nxt, "w_ff_out"))
    saved = dict(x=x, h=h, z=z, qn=qn, kn=kn, v=v, gb=gb, u=u, w=w, qd=qd, kd=kd, attn=attn,
                 tmat=tmat, cdt=cdt, o=o, vn=vn, st=st, ob=ob, m=m, x1=x1, h2=h2, a=a, r=r)
    gathered = None if nxt is None else dict(w_in=g_in, w_out=g_out, w_ff_in=g_w1, w_ff_out=g_w2)
    return x2, saved, gathered


def _backward_layer(dx2, mod_l, p, s, onehot, send=None):
    sh1, sc1, gt1, sh2, sc2, gt2 = [mod_l[k][None] for k in range(6)]
    P = functools.partial
    (dw2, dgt2), r_w2 = _hosted(P(_mm_tn, s["r"], dx2, "ff_out_dw", gate=gt2, w=p["w2"]),
                                _pick("a2a", send, "w_ff_out"))
    da = _mm_nt(dx2, p["w2"], "ff_out_dx", gate=gt2, drelu=s["a"])
    dw1, r_w1 = _hosted(P(_mm_tn, s["h2"], da, "ff_in_dw"), _pick("a2a", send, "w_ff_in"))
    dh2 = _mm_nt(da, p["w1"], "ff_in_dx")
    dx1, dsc2, dsh2, dnmlp = _modnorm_bwd(dh2, s["x1"], p["norm_mlp"], sc2, sh2, dx2, "norm_mlp_bwd")
    (dwout, dgt1), r_out = _hosted(P(_mm_tn, s["m"], dx1, "out_proj_dw", gate=gt1, w=p["wout"]),
                                   _pick("a2a", send, "w_out"))
    dm = _mm_nt(dx1, p["wout"], "out_proj_dx", gate=gt1)
    do, dzg, dob, dza, dzb, dgn = _merge_bwd(dm, s["o"], s["z"], s["ob"], p["gdn_norm"], "merge_bwd")
    (dq_att, dk_att, dv_att, dbias), r_in = _hosted(P(_att_bwd, s["z"], p["bmask"], s["ob"], dob, "att_bwd"),
                                                    _pick("a2a", send, "w_in"))
    drb = _bias_fold(dbias, onehot, "rel_bias_fold")[:, :513]
    dqd, dkd, dvn, dw, dattn, dcdt = _gdn_scan_bwd(do, s["w"], s["qd"], s["kd"], s["attn"], s["cdt"],
                                                   s["vn"], s["st"], "gdn_scan_bwd")
    dqn, dkn, dv, dgb = _gdn_intra_bwd(s["qn"], s["kn"], s["v"], s["gb"], s["u"], s["w"], s["tmat"],
                                       dqd, dkd, dvn, dw, dattn, dcdt, "gdn_intra_bwd")
    dzq, dzab, dcw, dvec = _gdn_prep_bwd(s["z"], dqn, dkn, dv, dgb, p["conv_w"], p["alog"], p["dtb"],
                                         "gdn_prep_bwd")
    dz = jnp.concatenate([dzq, dzg, dq_att, dk_att.astype(MXU), dv_att.astype(MXU), dza, dzb, dzab], axis=1)
    dwz = _mm_tn(s["h"], dz, "in_proj_dw", tn=512)
    dh = _mm_nt(dz, p["wz"], "in_proj_dx", tn=512)
    dx, dsc1, dsh1, dnmix = _modnorm_bwd(dh, s["x"], p["norm_mix"], sc1, sh1, dx1, "norm_mix_bwd")
    dw_in = jnp.concatenate([dwz[:, :Z_ATT], dwz[:, Z_AB:Z_AB + 2 * NH], dwz[:, Z_ATT:Z_AB]], axis=1)
    grads = dict(w_in=dw_in, w_out=dwout, w_ff_in=dw1, w_ff_out=dw2, norm_mix=dnmix[0], norm_mlp=dnmlp[0],
                 conv_w=dcw[:4], a_log=dvec[0, :NH], dt_bias=dvec[1, :NH], gdn_norm=dgn[0], rel_bias=drb,
                 mod=jnp.concatenate([dsh1, dsc1, dgt1, dsh2, dsc2, dgt2], axis=1)[0])
    recv = None if send is None else dict(w_in=r_in, w_out=r_out, w_ff_in=r_w1, w_ff_out=r_w2)
    return dx, grads, recv


def _relayout_weights(g):
    w_in_full = jnp.transpose(g["w_in"], (1, 0, 2)).reshape(D, IN_W)
    wz = jnp.concatenate([w_in_full[:, :Z_ATT], w_in_full[:, Z_ATT + 2 * NH:], w_in_full[:, Z_ATT:Z_ATT + 2 * NH],
                          jnp.zeros((D, ZW - IN_W), MXU)], axis=-1)
    return dict(wz=wz, wout=g["w_out"].reshape(D, D),
                w1=jnp.transpose(g["w_ff_in"], (1, 0, 2)).reshape(D, DFF), w2=g["w_ff_out"].reshape(DFF, D))


def _to_owners(g):
    col = lambda a: jnp.transpose(a.astype(MXU).reshape(a.shape[0], NDEV, -1), (1, 0, 2))
    row = lambda a: a.astype(MXU).reshape(NDEV, -1, a.shape[1])
    return dict(w_in=col(g["w_in"]), w_out=row(g["w_out"]), w_ff_in=col(g["w_ff_in"]), w_ff_out=row(g["w_ff_out"]))


def _bias_onehot():
    return (jnp.asarray(_bias_index())[:, :, None] == jnp.arange(640)[None, None, :]).astype(F32)


def _layer_params(l, conv_full, norm_mix, norm_mlp, a_log, dt_bias, gdn_norm, rel_bias, onehot):
    pad = lambda a: jnp.pad(a, (0, 128 - NH))[None]
    fvec = _bias_vec(jnp.pad(rel_bias[l], ((0, 0), (0, 640 - rel_bias.shape[2]))), onehot, "rel_bias_vec")
    return dict(conv_w=conv_full[l], norm_mix=norm_mix[l][None], norm_mlp=norm_mlp[l][None], alog=pad(a_log[l]),
                dtb=pad(dt_bias[l]), gdn_norm=gdn_norm[l][None], bmask=_bias_mask(fvec, "rel_bias_mask"))


def _local_step(x, target, mod, small, shards, final_norm, onehot):
    L = len(small)
    gathered = {n: _all_gather(shards[0][n], "gather_" + n) for n in BIG}
    saved, params = [], []
    for l in range(L):
        params.append({**small[l], **_relayout_weights(gathered)})
        x, sv, gathered = _forward_layer(x, mod[l].reshape(6, D), params[l], shards[l + 1] if l + 1 < L else None)
        saved.append(sv)
    loss, dx, dfn = _loss_head(x, target, final_norm[None], "loss_head")
    grads, recv, send = [None] * L, [None] * L, None
    for l in reversed(range(L)):
        dx, grads[l], got = _backward_layer(dx, mod[l].reshape(6, D), params[l], saved[l], onehot, send)
        if send is not None:
            recv[l + 1] = got
        send = _to_owners(grads[l])
    recv[0] = {n: _all_to_all(send[n], "exchange_" + n) for n in BIG}
    return loss, dx, grads, dfn[0], recv


SMALL = ("b_ada", "norm_mix", "norm_mlp", "a_log", "dt_bias", "gdn_norm", "rel_bias", "final_norm")


def kernel(x, c, w_ada, b_ada, norm_mix, norm_mlp, w_in, conv_w, a_log, dt_bias, gdn_norm, rel_bias, w_out, w_ff_in, w_ff_out, final_norm, loss_target, m_w_ada, m_b_ada, m_norm_mix, m_norm_mlp, m_w_in, m_conv_w, m_a_log, m_dt_bias, m_gdn_norm, m_rel_bias, m_w_out, m_w_ff_in, m_w_ff_out, m_final_norm, v_w_ada, v_b_ada, v_norm_mix, v_norm_mlp, v_w_in, v_conv_w, v_a_log, v_dt_bias, v_gdn_norm, v_rel_bias, v_w_out, v_w_ff_in, v_w_ff_out, v_final_norm):
    W = dict(w_ada=w_ada, b_ada=b_ada, norm_mix=norm_mix, norm_mlp=norm_mlp, w_in=w_in, conv_w=conv_w,
             a_log=a_log, dt_bias=dt_bias, gdn_norm=gdn_norm, rel_bias=rel_bias, w_out=w_out,
             w_ff_in=w_ff_in, w_ff_out=w_ff_out, final_norm=final_norm)
    Mo = dict(w_ada=m_w_ada, b_ada=m_b_ada, norm_mix=m_norm_mix, norm_mlp=m_norm_mlp, w_in=m_w_in,
              conv_w=m_conv_w, a_log=m_a_log, dt_bias=m_dt_bias, gdn_norm=m_gdn_norm, rel_bias=m_rel_bias,
              w_out=m_w_out, w_ff_in=m_w_ff_in, w_ff_out=m_w_ff_out, final_norm=m_final_norm)
    Vo = dict(w_ada=v_w_ada, b_ada=v_b_ada, norm_mix=v_norm_mix, norm_mlp=v_norm_mlp, w_in=v_w_in,
              conv_w=v_conv_w, a_log=v_a_log, dt_bias=v_dt_bias, gdn_norm=v_gdn_norm, rel_bias=v_rel_bias,
              w_out=v_w_out, w_ff_in=v_w_ff_in, w_ff_out=v_w_ff_out, final_norm=v_final_norm)
    L = w_in.shape[0]
    me = _flat(_mesh_pos())
    cshard = conv_w.shape[2]

    small_in = _all_gather(_pack_rows([c, conv_w]), "gather_c_conv")
    c_all = small_in[:, 0, :]
    conv_full = small_in.reshape(NDEV, -1)[:, D:D + L * 4 * cshard].reshape(NDEV, L, 4, cshard)
    conv_full = jnp.transpose(conv_full, (1, 2, 0, 3)).reshape(L, 4, NDEV * cshard)

    b_shard = lax.dynamic_slice_in_dim(b_ada, me * ADA_SHARD, ADA_SHARD, axis=1)
    mod_all = _all_gather(_ada_mod(c_all, w_ada, b_shard, "ada_mod"), "gather_mod")
    mod = lax.dynamic_index_in_dim(mod_all, me, axis=2, keepdims=False)
    mod = jnp.transpose(mod, (1, 0, 2)).reshape(L, 6 * D)

    onehot = _bias_onehot()
    small = [_layer_params(l, conv_full, norm_mix, norm_mlp, a_log, dt_bias, gdn_norm, rel_bias, onehot)
             for l in range(L)]
    shards = [{n: W[n][l].astype(MXU) for n in BIG} for l in range(L)]
    loss, dx, grads, dfn, recv = _local_step(x[0], loss_target[0], mod, small, shards, final_norm, onehot)

    def stack(name):
        return jnp.stack([g[name] for g in grads])

    small_names = ("mod", "norm_mix", "norm_mlp", "a_log", "dt_bias", "gdn_norm", "rel_bias")
    small_parts = [stack(n) for n in small_names] + [dfn, stack("conv_w"), loss[0, 0:1]]
    small_shapes = [a.shape for a in small_parts]
    gathered = _all_gather(_pack_rows(small_parts), "gather_small_grads")
    total = _unpack_rows(_sum_parts(gathered, "sum_small_grads"), small_shapes)
    tot = dict(zip(small_names + ("final_norm", "conv_w", "loss"), total))
    tot["b_ada"] = tot.pop("mod")
    tot["conv_w"] = lax.dynamic_slice_in_dim(tot["conv_w"], me * cshard, cshard, axis=2)

    out_g, out_d, out_m, out_v = {}, {}, {}, {}
    names = SMALL + ("conv_w",)
    shapes = [W[n].shape for n in names]
    packed = [_pack_rows([src[n] for n in names])[None] if src is tot else _pack_rows([src[n] for n in names])
              for src in (tot, W, Mo, Vo)]
    res = _adamw_reduce(*packed, "adamw_small", tr=8)
    for dst, arr in zip((out_g, out_d, out_m, out_v), res):
        dst.update(zip(names, _unpack_rows(arr, shapes)))

    dmod_all = gathered.reshape(NDEV, -1)[:, :L * 6 * D].reshape(NDEV, L, 6 * D)
    dmod_mine = jnp.transpose(lax.dynamic_slice_in_dim(dmod_all, me * ADA_SHARD, ADA_SHARD, axis=2), (1, 0, 2))
    res = _wada_adamw(jnp.transpose(c_all), dmod_mine, w_ada, m_w_ada, v_w_ada, "adamw_w_ada")
    for dst, arr in zip((out_g, out_d, out_m, out_v), res):
        dst["w_ada"] = arr

    for name, tr in (("w_in", 256), ("w_out", 128), ("w_ff_in", 256), ("w_ff_out", 256)):
        sh = W[name].shape
        rows = int(np.prod(sh[:-1]))
        flat = lambda a: a.reshape(rows, sh[-1])
        parts = jnp.stack([recv[l][name] for l in range(L)], axis=1).reshape(NDEV, rows, sh[-1])
        res = _adamw_reduce(parts, flat(W[name]), flat(Mo[name]), flat(Vo[name]), "adamw_" + name, tr=tr)
        for dst, arr in zip((out_g, out_d, out_m, out_v), res):
            dst[name] = arr.reshape(sh)

    order = ("w_ada", "b_ada", "norm_mix", "norm_mlp", "w_in", "conv_w", "a_log", "dt_bias", "gdn_norm",
             "rel_bias", "w_out", "w_ff_in", "w_ff_out", "final_norm")
    return (tot["loss"].reshape(()), dx[None], *[out_g[n] for n in order], *[out_d[n] for n in order],
            *[out_m[n] for n in order], *[out_v[n] for n in order])
```

```python
import functools
import math

import numpy as np
import jax
import jax.numpy as jnp
from jax import lax
from jax.experimental import pallas as pl
from jax.experimental.pallas import tpu as pltpu

F32 = jnp.float32
MXU = jnp.bfloat16
HI = lax.Precision.HIGHEST
MESH_ID = pl.DeviceIdType.MESH

D = 1024
NH = 8
HD = 128
CH = 64
PAST = 8
DFF = 4096
EPS = 1e-6
NDEV = 8
DEPTH = 4
IN_W = 9232
ZW = 9728
Z_GATE, Z_ATT, Z_BR, Z_AB = 3072, 4096, 7168, 9216
QB = 256
KC = 128
MASKED = -1e30
KWIN = 768
FW = 1024
ADAM_LR, ADAM_B1, ADAM_B2, ADAM_EPS, ADAM_WD, ADAM_STEP = 0.001, 0.9, 0.999, 1e-08, 0.01, 10


def _dot(a, b, prec=None):
    return jnp.dot(a, b, preferred_element_type=F32, precision=prec)


def _dot_nt(a, b, prec=None):
    return lax.dot_general(a, b, (((1,), (1,)), ((), ())), preferred_element_type=F32, precision=prec)


def _dot_tn(a, b, prec=None):
    return lax.dot_general(a, b, (((0,), (0,)), ((), ())), preferred_element_type=F32, precision=prec)


def _mx(a):
    return a.astype(MXU)


def _sigmoid(x):
    return 1.0 / (1.0 + jnp.exp(-x))


def _softplus(x):
    return jnp.maximum(x, 0.0) + jnp.log(1.0 + jnp.exp(-jnp.abs(x)))


def _rowsum(x):
    return jnp.sum(x, axis=1, keepdims=True)


def _colsum(x):
    return jnp.sum(x, axis=0, keepdims=True)


def _call(body, name, grid, in_specs, out_specs, out_shape, scratch=(), comm=None):
    if comm is None:
        return pl.pallas_call(body, name=name, grid=grid, in_specs=in_specs, out_specs=out_specs,
                              out_shape=out_shape, scratch_shapes=list(scratch))
    kind, x = comm
    single = not isinstance(out_specs, (list, tuple))
    o_specs = [out_specs] if single else list(out_specs)
    o_shape = [out_shape] if single else list(out_shape)
    n_in, n_out, n_scr = len(in_specs), len(o_specs), len(scratch)
    c_shape = (NDEV,) + x.shape if kind == "gather" else x.shape

    def wrapped(*refs):
        ins, x_ref = refs[:n_in], refs[n_in]
        outs, c_ref = refs[n_in + 1:n_in + 1 + n_out], refs[n_in + 1 + n_out]
        scr = refs[n_in + 2 + n_out:n_in + 2 + n_out + n_scr]
        sems = refs[n_in + 2 + n_out + n_scr:]
        first = functools.reduce(jnp.logical_and, [pl.program_id(a) == 0 for a in range(len(grid))])
        last = functools.reduce(jnp.logical_and, [pl.program_id(a) == grid[a] - 1 for a in range(len(grid))])

        @pl.when(first)
        def _():
            _comm_start(*_comm_copies(kind, x_ref, c_ref, *sems))

        body(*ins, *outs, *scr)

        @pl.when(last)
        def _():
            _comm_wait(*_comm_copies(kind, x_ref, c_ref, *sems))

    any_spec = pl.BlockSpec(memory_space=pl.ANY)
    call = pl.pallas_call(
        wrapped, name=name, grid=grid, in_specs=list(in_specs) + [any_spec], out_specs=o_specs + [any_spec],
        out_shape=o_shape + [jax.ShapeDtypeStruct(c_shape, x.dtype)],
        scratch_shapes=list(scratch) + _comm_sems())

    def run(*args):
        res = call(*args, x)
        return (res[0] if single else list(res[:-1])), res[-1]

    return run


def _hosted(fn, comm):
    return (fn(), None) if comm is None else fn(comm=comm)


def _full(shape):
    n = len(shape)
    return pl.BlockSpec(shape, lambda *_: (0,) * n)


def _mesh_pos():
    return lax.axis_index("x"), lax.axis_index("y"), lax.axis_index("c")


def _peer(pos, k):
    x, y, c = pos
    return (x ^ ((k >> 2) & 1), y ^ ((k >> 1) & 1), c ^ (k & 1))


def _flat(pos):
    return 4 * pos[0] + 2 * pos[1] + pos[2]


def _comm_sems():
    return [pltpu.SemaphoreType.DMA((NDEV - 1,)), pltpu.SemaphoreType.DMA((NDEV - 1,)), pltpu.SemaphoreType.DMA]


def _comm_copies(kind, x_ref, out_ref, send_sems, recv_sems, local_sem):
    pos = _mesh_pos()
    me = _flat(pos)
    src = (lambda d: x_ref) if kind == "gather" else (lambda d: x_ref.at[d])
    mine = pltpu.make_async_copy(src(me), out_ref.at[me], local_sem)
    sends, recvs = [], []
    for k in range(1, NDEV):
        peer = _peer(pos, k)
        pid = _flat(peer)
        sems = dict(send_sem=send_sems.at[k - 1], recv_sem=recv_sems.at[k - 1], device_id=peer,
                    device_id_type=MESH_ID)
        sends.append(pltpu.make_async_remote_copy(src_ref=src(pid), dst_ref=out_ref.at[me], **sems))
        recvs.append(pltpu.make_async_remote_copy(src_ref=src(pid), dst_ref=out_ref.at[pid], **sems))
    return mine, sends, recvs


def _comm_start(mine, sends, recvs):
    mine.start()
    for cp in sends:
        cp.start()


def _comm_wait(mine, sends, recvs):
    for cp in recvs:
        cp.wait_recv()
    for cp in sends:
        cp.wait_send()
    mine.wait()


def _collective(kind, x, name):
    def body(x_ref, out_ref, *sems):
        copies = _comm_copies(kind, x_ref, out_ref, *sems)
        _comm_start(*copies)
        _comm_wait(*copies)

    shape = (NDEV,) + x.shape if kind == "gather" else x.shape
    return pl.pallas_call(
        body, name=name, out_shape=jax.ShapeDtypeStruct(shape, x.dtype),
        in_specs=[pl.BlockSpec(memory_space=pl.ANY)], out_specs=pl.BlockSpec(memory_space=pl.ANY),
        scratch_shapes=_comm_sems())(x)


def _all_gather(x, name):
    return _collective("gather", x, name)


def _all_to_all(x, name):
    return _collective("a2a", x, name)


def _mm_nn(a, w, name, *, mode="plain", res=None, gate=None, tm=1024, tn=1024, tk=1024, comm=None):
    M, K = a.shape
    N = w.shape[1]
    tm, tk, tn = min(tm, M), min(tk, K), min(tn, N)
    nk = K // tk

    def body(*refs):
        refs = list(refs)
        acc = refs.pop() if nk > 1 else None
        if mode == "resid":
            a_ref, w_ref, res_ref, gate_ref, o_ref = refs
        elif mode == "relu2":
            a_ref, w_ref, o_ref, r_ref = refs
        else:
            a_ref, w_ref, o_ref = refs
        k = pl.program_id(2)
        part = _dot(_mx(a_ref[...]), w_ref[...])

        def finish(r):
            if mode == "resid":
                o_ref[...] = res_ref[...] + gate_ref[...] * r
            elif mode == "relu2":
                o_ref[...] = r
                r_ref[...] = jnp.square(jnp.maximum(r, 0.0)).astype(r_ref.dtype)
            else:
                o_ref[...] = r

        if nk == 1:
            finish(part)
        else:
            @pl.when(k == 0)
            def _():
                acc[...] = part

            @pl.when((k > 0) & (k < nk - 1))
            def _():
                acc[...] += part

            @pl.when(k == nk - 1)
            def _():
                finish(acc[...] + part)

    in_specs = [pl.BlockSpec((tm, tk), lambda i, j, k: (i, k)),
                pl.BlockSpec((tk, tn), lambda i, j, k: (k, j))]
    args = [a, w]
    o_spec = pl.BlockSpec((tm, tn), lambda i, j, k: (i, j))
    out_specs, out_shape = o_spec, jax.ShapeDtypeStruct((M, N), F32)
    if mode == "resid":
        in_specs += [o_spec, pl.BlockSpec((1, tn), lambda i, j, k: (0, j))]
        args += [res, gate]
    elif mode == "relu2":
        out_specs = [o_spec, o_spec]
        out_shape = [jax.ShapeDtypeStruct((M, N), F32), jax.ShapeDtypeStruct((M, N), MXU)]
    return _call(body, name, (M // tm, N // tn, nk), in_specs, out_specs, out_shape,
                 [pltpu.VMEM((tm, tn), F32)] if nk > 1 else [], comm=comm)(*args)


def _mm_nt(a, w, name, *, gate=None, drelu=None, tm=1024, tko=1024, tn=1024):
    M, N = a.shape
    K = w.shape[0]
    tm, tn, tko = min(tm, M), min(tn, N), min(tko, K)
    nn = N // tn

    def body(*refs):
        refs = list(refs)
        acc = refs.pop() if nn > 1 else None
        a_ref, w_ref = refs[:2]
        rest = refs[2:]
        gate_ref = rest.pop(0) if gate is not None else None
        pre_ref = rest.pop(0) if drelu is not None else None
        (o_ref,) = rest
        n = pl.program_id(2)
        av = a_ref[...]
        if gate_ref is not None:
            av = av * gate_ref[...]
        part = _dot_nt(_mx(av), w_ref[...])

        def finish(r):
            if pre_ref is not None:
                r = r * (2.0 * jnp.maximum(pre_ref[...], 0.0))
            o_ref[...] = r.astype(o_ref.dtype)

        if nn == 1:
            finish(part)
        else:
            @pl.when(n == 0)
            def _():
                acc[...] = part

            @pl.when((n > 0) & (n < nn - 1))
            def _():
                acc[...] += part

            @pl.when(n == nn - 1)
            def _():
                finish(acc[...] + part)

    in_specs = [pl.BlockSpec((tm, tn), lambda i, j, n: (i, n)),
                pl.BlockSpec((tko, tn), lambda i, j, n: (j, n))]
    args = [a, w]
    if gate is not None:
        in_specs.append(pl.BlockSpec((1, tn), lambda i, j, n: (0, n)))
        args.append(gate)
    o_spec = pl.BlockSpec((tm, tko), lambda i, j, n: (i, j))
    if drelu is not None:
        in_specs.append(o_spec)
        args.append(drelu)
    out_dtype = MXU if drelu is not None else F32
    return _call(body, name, (M // tm, K // tko, nn), in_specs, o_spec,
                 jax.ShapeDtypeStruct((M, K), out_dtype), [pltpu.VMEM((tm, tko), F32)] if nn > 1 else [])(*args)


def _mm_tn(a, b, name, *, gate=None, w=None, tk=1024, tn=1024, tm=1024, comm=None):
    M, K = a.shape
    N = b.shape[1]
    tm, tk, tn = min(tm, M), min(tk, K), min(tn, N)
    nm = M // tm
    gated = gate is not None

    def body(*refs):
        if gated:
            a_ref, b_ref, gate_ref, w_ref, o_ref, dg_ref, acc = refs
        else:
            a_ref, b_ref, o_ref, acc = refs
        kk = pl.program_id(1)
        m = pl.program_id(2)
        part = _dot_tn(_mx(a_ref[...]), _mx(b_ref[...]))

        @pl.when((m == 0) & (nm > 1))
        def _():
            acc[...] = part

        @pl.when((m > 0) & (m < nm - 1))
        def _():
            acc[...] += part

        if gated:
            @pl.when((m == 0) & (kk == 0))
            def _():
                dg_ref[...] = jnp.zeros_like(dg_ref)

        @pl.when(m == nm - 1)
        def _():
            r = acc[...] + part if nm > 1 else part
            if gated:
                o_ref[...] = r * gate_ref[...]
                dg_ref[...] += _colsum(r * w_ref[...].astype(F32))
            else:
                o_ref[...] = r

    in_specs = [pl.BlockSpec((tm, tk), lambda j, k, m: (m, k)),
                pl.BlockSpec((tm, tn), lambda j, k, m: (m, j))]
    args = [a, b]
    o_spec = pl.BlockSpec((tk, tn), lambda j, k, m: (k, j))
    out_specs, out_shape = o_spec, jax.ShapeDtypeStruct((K, N), F32)
    if gated:
        in_specs += [pl.BlockSpec((1, tn), lambda j, k, m: (0, j)), o_spec]
        args += [gate, w]
        out_specs = [o_spec, pl.BlockSpec((1, tn), lambda j, k, m: (0, j))]
        out_shape = [out_shape, jax.ShapeDtypeStruct((1, N), F32)]
    return _call(body, name, (N // tn, K // tk, nm), in_specs, out_specs, out_shape,
                 [pltpu.VMEM((tk, tn), F32)], comm=comm)(*args)


SEG_T = 512


def _seg_layout(segs):
    starts, t = [], 0
    for a in segs:
        starts.append(t)
        t += a.shape[1] // SEG_T
    return starts, t


def _seg_spec(tm, lo, hi, row_axis, col_axis):
    def index(*ids):
        col = ids[col_axis]
        act = (col >= lo) & (col < hi)
        return jnp.where(act, ids[row_axis], 0), jnp.where(act, col - lo, 0)

    return pl.BlockSpec((tm, SEG_T), index)


def _in_proj_dw(h, segs, name, tm=1024):
    S = h.shape[0]
    tm = min(tm, S)
    nm = S // tm
    starts, ntile = _seg_layout(segs)
    bounds = [(lo, lo + a.shape[1] // SEG_T) for lo, a in zip(starts, segs)]

    def body(*refs):
        h_ref, seg_refs, o_ref, acc = refs[0], refs[1:1 + len(segs)], refs[-2], refs[-1]
        j = pl.program_id(0)
        m = pl.program_id(1)
        for (lo, hi), b_ref in zip(bounds, seg_refs):
            @pl.when((j >= lo) & (j < hi))
            def _():
                part = _dot_tn(h_ref[...], _mx(b_ref[...]))
                if nm == 1:
                    o_ref[...] = part
                else:
                    @pl.when(m == 0)
                    def _():
                        acc[...] = part

                    @pl.when((m > 0) & (m < nm - 1))
                    def _():
                        acc[...] += part

                    @pl.when(m == nm - 1)
                    def _():
                        o_ref[...] = acc[...] + part

    return _call(
        body, name, (ntile, nm),
        [pl.BlockSpec((tm, D), lambda j, m: (m, 0))] + [_seg_spec(tm, lo, hi, 1, 0) for lo, hi in bounds],
        pl.BlockSpec((D, SEG_T), lambda j, m: (0, j)), jax.ShapeDtypeStruct((D, ntile * SEG_T), F32),
        [pltpu.VMEM((D, SEG_T), F32)])(h, *segs)


def _in_proj_dx(segs, w, name, tm=1024):
    S = segs[0].shape[0]
    tm = min(tm, S)
    starts, ntile = _seg_layout(segs)
    bounds = [(lo, lo + a.shape[1] // SEG_T) for lo, a in zip(starts, segs)]

    def body(*refs):
        seg_refs, w_ref, o_ref, acc = refs[:len(segs)], refs[-3], refs[-2], refs[-1]
        n = pl.program_id(1)
        for (lo, hi), a_ref in zip(bounds, seg_refs):
            @pl.when((n >= lo) & (n < hi))
            def _():
                part = _dot_nt(_mx(a_ref[...]), w_ref[...])

                @pl.when(n == 0)
                def _():
                    acc[...] = part

                @pl.when((n > 0) & (n < ntile - 1))
                def _():
                    acc[...] += part

                @pl.when(n == ntile - 1)
                def _():
                    o_ref[...] = acc[...] + part

    return _call(
        body, name, (S // tm, ntile),
        [_seg_spec(tm, lo, hi, 0, 1) for lo, hi in bounds] + [pl.BlockSpec((D, SEG_T), lambda i, n: (0, n))],
        pl.BlockSpec((tm, D), lambda i, n: (i, 0)), jax.ShapeDtypeStruct((S, D), F32),
        [pltpu.VMEM((tm, D), F32)])(*segs, w)


def _modnorm_fwd(x, gain, sc, sh, name, ts=512):
    S = x.shape[0]

    def body(x_ref, g_ref, sc_ref, sh_ref, h_ref):
        xv = x_ref[...]
        r = lax.rsqrt(jnp.mean(xv * xv, axis=1, keepdims=True) + EPS)
        h_ref[...] = ((xv * r * g_ref[...]) * (1.0 + sc_ref[...]) + sh_ref[...]).astype(h_ref.dtype)

    row = pl.BlockSpec((ts, D), lambda i: (i, 0))
    vec = pl.BlockSpec((1, D), lambda i: (0, 0))
    return _call(body, name, (S // ts,), [row, vec, vec, vec], row,
                 jax.ShapeDtypeStruct((S, D), MXU))(x, gain, sc, sh)


def _modnorm_bwd(dh, x, gain, sc, sh, dx_in, name, ts=512):
    S = x.shape[0]

    def body(dh_ref, x_ref, g_ref, sc_ref, sh_ref, dxin_ref, dx_ref, dsc_ref, dsh_ref, dg_ref):
        i = pl.program_id(0)
        xv = x_ref[...]
        dhv = dh_ref[...]
        g = g_ref[...]
        r = lax.rsqrt(jnp.mean(xv * xv, axis=1, keepdims=True) + EPS)
        xr = xv * r
        dn = dhv * (1.0 + sc_ref[...])
        u = dn * g
        dx_ref[...] = dxin_ref[...] + r * (u - xr * jnp.mean(xr * u, axis=1, keepdims=True))

        @pl.when(i == 0)
        def _():
            dsc_ref[...] = jnp.zeros_like(dsc_ref)
            dsh_ref[...] = jnp.zeros_like(dsh_ref)
            dg_ref[...] = jnp.zeros_like(dg_ref)

        dsc_ref[...] += _colsum(dhv * (xr * g))
        dsh_ref[...] += _colsum(dhv)
        dg_ref[...] += _colsum(dn * xr)

    row = pl.BlockSpec((ts, D), lambda i: (i, 0))
    vec = pl.BlockSpec((1, D), lambda i: (0, 0))
    vshape = jax.ShapeDtypeStruct((1, D), F32)
    return _call(body, name, (S // ts,), [row, row, vec, vec, vec, row], [row, vec, vec, vec],
                 [jax.ShapeDtypeStruct((S, D), F32), vshape, vshape, vshape])(dh, x, gain, sc, sh, dx_in)


def _loss_head(x, target, gain, name, ts=512):
    S = x.shape[0]

    def body(x_ref, t_ref, g_ref, loss_ref, dx_ref, dg_ref):
        i = pl.program_id(0)
        xv = x_ref[...]
        g = g_ref[...]
        r = lax.rsqrt(jnp.mean(xv * xv, axis=1, keepdims=True) + EPS)
        xr = xv * r
        e = xr * g - t_ref[...]
        dy = e * (1.0 / D)
        u = dy * g
        dx_ref[...] = r * (u - xr * jnp.mean(xr * u, axis=1, keepdims=True))

        @pl.when(i == 0)
        def _():
            loss_ref[...] = jnp.zeros_like(loss_ref)
            dg_ref[...] = jnp.zeros_like(dg_ref)

        part = 0.5 * jnp.sum(jnp.mean(e * e, axis=1, keepdims=True), axis=0, keepdims=True)
        loss_ref[...] += jnp.broadcast_to(part, loss_ref.shape)
        dg_ref[...] += _colsum(dy * xr)

    row = pl.BlockSpec((ts, D), lambda i: (i, 0))
    vec = pl.BlockSpec((1, D), lambda i: (0, 0))
    return _call(body, name, (S // ts,), [row, row, vec],
                 [pl.BlockSpec((1, 128), lambda i: (0, 0)), row, vec],
                 [jax.ShapeDtypeStruct((1, 128), F32), jax.ShapeDtypeStruct((S, D), F32),
                  jax.ShapeDtypeStruct((1, D), F32)])(x, target, gain)


def _merge_specs(ts):
    o_spec = pl.BlockSpec((NH, ts, HD), lambda i: (0, i, 0))
    zg = pl.BlockSpec((ts, D), lambda i: (i, Z_GATE // D))
    za = pl.BlockSpec((ts, D), lambda i: (i, Z_BR // D))
    zb = pl.BlockSpec((ts, D), lambda i: (i, Z_BR // D + 1))
    row = pl.BlockSpec((ts, D), lambda i: (i, 0))
    gn = pl.BlockSpec((1, HD), lambda i: (0, 0))
    return o_spec, zg, za, zb, row, gn


def _merge_fwd(o, z, ob, gn, name, ts=256):
    S = ob.shape[0]

    def body(o_ref, zg_ref, za_ref, zb_ref, ob_ref, gn_ref, m_ref):
        for h in range(NH):
            sl = slice(h * HD, (h + 1) * HD)
            oh = o_ref[h]
            r = lax.rsqrt(jnp.mean(oh * oh, axis=1, keepdims=True) + EPS)
            gate = zg_ref[:, sl]
            oa = (oh * r * gn_ref[...]) * (gate * _sigmoid(gate))
            m = _sigmoid(za_ref[:, sl]) * oa + _sigmoid(zb_ref[:, sl]) * ob_ref[:, sl]
            m_ref[:, sl] = m.astype(m_ref.dtype)

    o_spec, zg, za, zb, row, gns = _merge_specs(ts)
    return _call(body, name, (S // ts,), [o_spec, zg, za, zb, row, gns], row,
                 jax.ShapeDtypeStruct((S, D), MXU))(o, z, z, z, ob, gn)


def _merge_bwd(dm, o, z, ob, gn, name, ts=256):
    S = ob.shape[0]

    def body(dm_ref, o_ref, zg_ref, za_ref, zb_ref, ob_ref, gn_ref,
             do_ref, dzg_ref, dob_ref, dza_ref, dzb_ref, dgn_ref):
        i = pl.program_id(0)
        gn_v = gn_ref[...]
        dgn = jnp.zeros((1, HD), F32)
        for h in range(NH):
            sl = slice(h * HD, (h + 1) * HD)
            dmh = dm_ref[:, sl]
            oh = o_ref[h]
            r = lax.rsqrt(jnp.mean(oh * oh, axis=1, keepdims=True) + EPS)
            ohr = oh * r
            on = ohr * gn_v
            gate = zg_ref[:, sl]
            sg = _sigmoid(gate)
            silu = gate * sg
            oa = on * silu
            ga = _sigmoid(za_ref[:, sl])
            gb = _sigmoid(zb_ref[:, sl])
            obh = ob_ref[:, sl]
            doa = dmh * ga
            dob_ref[:, sl] = dmh * gb
            dza_ref[:, sl] = (dmh * oa * ga * (1.0 - ga)).astype(dza_ref.dtype)
            dzb_ref[:, sl] = (dmh * obh * gb * (1.0 - gb)).astype(dzb_ref.dtype)
            don = doa * silu
            dzg_ref[:, sl] = (doa * on * (sg * (1.0 + gate * (1.0 - sg)))).astype(dzg_ref.dtype)
            dgn = dgn + _colsum(don * ohr)
            u = don * gn_v
            do_ref[h] = r * (u - ohr * jnp.mean(ohr * u, axis=1, keepdims=True))

        @pl.when(i == 0)
        def _():
            dgn_ref[...] = jnp.zeros_like(dgn_ref)

        dgn_ref[...] += dgn

    o_spec, zg, za, zb, row, gns = _merge_specs(ts)
    return _call(
        body, name, (S // ts,), [row, o_spec, zg, za, zb, row, gns],
        [o_spec, row, row, row, row, gns],
        [jax.ShapeDtypeStruct((NH, S, HD), F32), jax.ShapeDtypeStruct((S, D), MXU),
         jax.ShapeDtypeStruct((S, D), F32), jax.ShapeDtypeStruct((S, D), MXU),
         jax.ShapeDtypeStruct((S, D), MXU), jax.ShapeDtypeStruct((1, HD), F32)],
    )(dm, o, z, z, z, ob, gn)


GROWS = 256
GCH = GROWS // CH


def _gdn_prep_fwd(z, conv_w, alog_row, dtb_row, name):
    S = z.shape[0]
    ts = GROWS
    scale = HD ** -0.5

    def body(z_ref, halo_ref, zab_ref, w_ref, al_ref, dt_ref, q_ref, k_ref, v_ref, gb_ref, buf):
        i = pl.program_id(0)
        buf[0:8, :] = jnp.where(i == 0, 0.0, halo_ref[...])
        buf[8:8 + ts, :] = z_ref[...]
        outs = (q_ref, k_ref, v_ref)
        for seg in range(3):
            cs = slice(seg * D, (seg + 1) * D)
            c = jnp.zeros((ts, D), F32)
            for j in range(4):
                c = c + w_ref[j:j + 1, cs] * buf[pl.ds(5 + j, ts), cs]
            s = c * _sigmoid(c)
            if seg == 2:
                outs[seg][...] = s
            else:
                mul = scale if seg == 0 else 1.0
                for h in range(NH):
                    sl = slice(h * HD, (h + 1) * HD)
                    sh = s[:, sl]
                    r = lax.rsqrt(_rowsum(sh * sh) + EPS)
                    outs[seg][:, sl] = sh * (r * mul)
        zab = zab_ref[...]
        lane = lax.broadcasted_iota(jnp.int32, zab.shape, 1)
        g = -jnp.exp(al_ref[...]) * _softplus(zab + dt_ref[...])
        ri = lax.broadcasted_iota(jnp.int32, (CH, CH), 0)
        ci = lax.broadcasted_iota(jnp.int32, (CH, CH), 1)
        incl = (ri >= ci).astype(F32)
        gcum = jnp.concatenate([_dot(incl, g[c * CH:(c + 1) * CH], HI) for c in range(ts // CH)], axis=0)
        gb_ref[...] = jnp.where(lane < NH, gcum, jnp.where(lane < 2 * NH, _sigmoid(zab), 0.0))

    row = pl.BlockSpec((ts, D), lambda i: (i, 0))
    vec = pl.BlockSpec((1, 128), lambda i: (0, 0))
    return _call(
        body, name, (S // ts,),
        [pl.BlockSpec((ts, 3 * D), lambda i: (i, 0)),
         pl.BlockSpec((8, 3 * D), lambda i: (jnp.maximum(i * (ts // 8) - 1, 0), 0)),
         pl.BlockSpec((ts, 128), lambda i: (i, Z_AB // 128)),
         _full((4, 3 * D)), vec, vec],
        [row, row, row, pl.BlockSpec((ts, 128), lambda i: (i, 0))],
        [jax.ShapeDtypeStruct((S, D), F32)] * 3 + [jax.ShapeDtypeStruct((S, 128), F32)],
        [pltpu.VMEM((ts + 8, 3 * D), F32)],
    )(z, z, z, conv_w, alog_row, dtb_row)


def _split(a):
    hi = a.astype(MXU)
    return hi, (a - hi.astype(F32)).astype(MXU)


def _dot3(a, b, dot=_dot):
    ah, al = _split(a)
    bh, bl = _split(b)
    return dot(ah, bh) + (dot(ah, bl) + dot(al, bh))


IROWS = 512
ICH = IROWS // CH


def _chunk_common(gbk, h, k):
    lane = lax.broadcasted_iota(jnp.int32, gbk.shape, 1)
    G = _rowsum(jnp.where(lane == h, gbk, 0.0))
    b_col = _rowsum(jnp.where(lane == h + NH, gbk, 0.0))
    ri = lax.broadcasted_iota(jnp.int32, (CH, CH), 0)
    ci = lax.broadcasted_iota(jnp.int32, (CH, CH), 1)
    incl = ri >= ci
    gc = jnp.broadcast_to(G, (CH, CH))
    decay = jnp.where(incl, jnp.exp(jnp.where(incl, gc - gc.T, 0.0)), 0.0)
    Gl = G[CH - 1:CH, :]
    kb = k * b_col
    return dict(b=b_col, ri=ri, ci=ci, incl=incl, strict=ri > ci, decay=decay, eG=jnp.exp(G),
                e2=jnp.exp(Gl - G), cd=jnp.exp(Gl), kb=kb, kk=_dot_nt(_mx(kb), _mx(k)))


def _gdn_intra_fwd(qn, kn, v, gb, name):
    S = qn.shape[0]

    def body(q_ref, k_ref, v_ref, gb_ref, u_ref, w_ref, qd_ref, kd_ref, at_ref, t_ref, cd_ref):
        h = pl.program_id(1)
        rows = [slice(c * CH, (c + 1) * CH) for c in range(ICH)]
        ks = [k_ref[r, :] for r in rows]
        cms = [_chunk_common(gb_ref[r, :], h, k) for r, k in zip(rows, ks)]
        ps = [jnp.where(cm["strict"], cm["kk"] * cm["decay"], 0.0) for cm in cms]
        ts = [(cm["ri"] == cm["ci"]).astype(F32) - p for cm, p in zip(cms, ps)]
        for _ in range(5):
            ps = [_dot3(p, p) for p in ps]
            ts = [t + _dot3(t, p) for t, p in zip(ts, ps)]
        for c, (r, k, cm, t) in enumerate(zip(rows, ks, cms, ts)):
            rhs = jnp.concatenate([v_ref[r, :] * cm["b"], k * (cm["b"] * cm["eG"])], axis=1)
            sol = _dot3(t, rhs)
            u_ref[0, r, :] = sol[:, :HD]
            w_ref[0, r, :] = sol[:, HD:]
            t_ref[0, r, :] = t
        for c, (r, k, cm) in enumerate(zip(rows, ks, cms)):
            q = q_ref[r, :]
            qk = _dot_nt(_mx(q), _mx(k))
            at_ref[0, r, :] = jnp.where(cm["incl"], qk * cm["decay"], 0.0)
            qd_ref[0, r, :] = q * cm["eG"]
            kd_ref[0, r, :] = k * cm["e2"]
            cd_ref[0, c] = jnp.broadcast_to(cm["cd"], (8, 128))

    tok = pl.BlockSpec((IROWS, HD), lambda i, h: (i, h))
    hm = pl.BlockSpec((1, IROWS, HD), lambda i, h: (h, i, 0))
    hm64 = pl.BlockSpec((1, IROWS, CH), lambda i, h: (h, i, 0))
    big = jax.ShapeDtypeStruct((NH, S, HD), F32)
    sm = jax.ShapeDtypeStruct((NH, S, CH), F32)
    return _call(
        body, name, (S // IROWS, NH),
        [tok, tok, tok, pl.BlockSpec((IROWS, 128), lambda i, h: (i, 0))],
        [hm, hm, hm, hm, hm64, hm64, pl.BlockSpec((1, ICH, 8, 128), lambda i, h: (h, i, 0, 0))],
        [big, big, big, big, sm, sm, jax.ShapeDtypeStruct((NH, S // CH, 8, 128), F32)],
    )(qn, kn, v, gb)


def _scale_state(s, cd_tile):
    return (s.reshape(HD // 8, 8, HD) * cd_tile[None]).reshape(HD, HD)


def _gdn_scan_fwd(u, w, qd, kd, attn, cdt, name):
    S = u.shape[1]
    nblk = S // GROWS

    def body(u_ref, w_ref, qd_ref, kd_ref, at_ref, cd_ref, o_ref, vn_ref, st_ref, s_ref):
        i = pl.program_id(0)

        @pl.when(i == 0)
        def _():
            s_ref[...] = jnp.zeros_like(s_ref)

        def chunk(c, carry):
            r0 = pl.multiple_of(c * CH, CH)
            rows = pl.ds(r0, CH)
            for h in range(NH):
                sh = s_ref[h]
                st_ref[h, c] = sh
                sb = _mx(sh)
                vn = u_ref[h, rows, :] - _dot(_mx(w_ref[h, rows, :]), sb)
                vb = _mx(vn)
                vn_ref[h, rows, :] = vn
                o_ref[h, rows, :] = _dot(_mx(qd_ref[h, rows, :]), sb) + _dot(_mx(at_ref[h, rows, :]), vb)
                s_ref[h] = _scale_state(sh, cd_ref[h, c]) + _dot_tn(_mx(kd_ref[h, rows, :]), vb)
            return carry

        lax.fori_loop(0, GCH, chunk, 0)

    hm = pl.BlockSpec((NH, GROWS, HD), lambda i: (0, i, 0))
    hm64 = pl.BlockSpec((NH, GROWS, CH), lambda i: (0, i, 0))
    big = jax.ShapeDtypeStruct((NH, S, HD), F32)
    return _call(
        body, name, (nblk,),
        [hm, hm, hm, hm, hm64, pl.BlockSpec((NH, GCH, 8, 128), lambda i: (0, i, 0, 0))],
        [hm, hm, pl.BlockSpec((NH, GCH, HD, HD), lambda i: (0, i, 0, 0))],
        [big, big, jax.ShapeDtypeStruct((NH, S // CH, HD, HD), F32)],
        [pltpu.VMEM((NH, HD, HD), F32)],
    )(u, w, qd, kd, attn, cdt)


def _gdn_scan_bwd(do, w, qd, kd, attn, cdt, vn, st, name):
    S = do.shape[1]
    nblk = S // GROWS

    def body(do_ref, w_ref, qd_ref, kd_ref, at_ref, cd_ref, vn_ref, st_ref,
             dqd_ref, dkd_ref, dvn_ref, dw_ref, dat_ref, dcd_ref, ds_ref):
        i = pl.program_id(0)

        @pl.when(i == 0)
        def _():
            ds_ref[...] = jnp.zeros_like(ds_ref)

        def chunk(cc, carry):
            c = GCH - 1 - cc
            r0 = pl.multiple_of(c * CH, CH)
            rows = pl.ds(r0, CH)
            for h in range(NH):
                dsp = ds_ref[h]
                sh = st_ref[h, c]
                dsb, sb = _mx(dsp), _mx(sh)
                dob = _mx(do_ref[h, rows, :])
                vb = _mx(vn_ref[h, rows, :])
                dvn = _dot(_mx(kd_ref[h, rows, :]), dsb) + _dot_tn(_mx(at_ref[h, rows, :]), dob)
                dvb = _mx(dvn)
                dvn_ref[h, rows, :] = dvn
                dqd_ref[h, rows, :] = _dot_nt(dob, sb)
                dat_ref[h, rows, :] = _dot_nt(dob, vb)
                dkd_ref[h, rows, :] = _dot_nt(vb, dsb)
                dw_ref[h, rows, :] = -_dot_nt(dvb, sb)
                dcd = jnp.sum(_rowsum(dsp * sh), axis=0, keepdims=True)
                dcd_ref[h, c] = jnp.broadcast_to(dcd, (8, 128))
                ds_ref[h] = (_scale_state(dsp, cd_ref[h, c]) + _dot_tn(_mx(qd_ref[h, rows, :]), dob)
                             - _dot_tn(_mx(w_ref[h, rows, :]), dvb))
            return carry

        lax.fori_loop(0, GCH, chunk, 0)

    hm = pl.BlockSpec((NH, GROWS, HD), lambda i: (0, nblk - 1 - i, 0))
    hm64 = pl.BlockSpec((NH, GROWS, CH), lambda i: (0, nblk - 1 - i, 0))
    tile = pl.BlockSpec((NH, GCH, 8, 128), lambda i: (0, nblk - 1 - i, 0, 0))
    big = jax.ShapeDtypeStruct((NH, S, HD), F32)
    return _call(
        body, name, (nblk,),
        [hm, hm, hm, hm, hm64, tile, hm, pl.BlockSpec((NH, GCH, HD, HD), lambda i: (0, nblk - 1 - i, 0, 0))],
        [hm, hm, hm, hm, hm64, tile],
        [big, big, big, big, jax.ShapeDtypeStruct((NH, S, CH), F32),
         jax.ShapeDtypeStruct((NH, S // CH, 8, 128), F32)],
        [pltpu.VMEM((NH, HD, HD), F32)],
    )(do, w, qd, kd, attn, cdt, vn, st)


def _gdn_intra_bwd(qn, kn, v, gb, u, w, tmat, dqd, dkd, du, dw, dattn, dcdt, name):
    S = qn.shape[0]

    def body(q_ref, k_ref, v_ref, gb_ref, u_ref, w_ref, t_ref, dqd_ref, dkd_ref, du_ref, dw_ref,
             dat_ref, dcd_ref, dq_ref, dk_ref, dv_ref, dgb_ref):
        h = pl.program_id(1)
        rows = [slice(c * CH, (c + 1) * CH) for c in range(ICH)]
        ks = [k_ref[r, :] for r in rows]
        cms = [_chunk_common(gb_ref[r, :], h, k) for r, k in zip(rows, ks)]
        sols = [jnp.concatenate([u_ref[0, r, :], w_ref[0, r, :]], axis=1) for r in rows]
        drhss = [_dot3(t_ref[0, r, :], jnp.concatenate([du_ref[0, r, :], dw_ref[0, r, :]], axis=1), _dot_tn)
                 for r in rows]
        das = [-_dot3(drhs, sol, _dot_nt) for drhs, sol in zip(drhss, sols)]
        for c, (r, k, cm, drhs, da) in enumerate(zip(rows, ks, cms, drhss, das)):
            q, vv = q_ref[r, :], v_ref[r, :]
            decay, eG, e2, b = cm["decay"], cm["eG"], cm["e2"], cm["b"]
            dru, drw = drhs[:, :HD], drhs[:, HD:]
            dv_ref[r, :] = dru * b
            s_w = _rowsum(drw * k)
            dbeta = _rowsum(dru * vv) + s_w * eG
            deg = s_w * b
            dk = drw * (b * eG)
            dkk = jnp.where(cm["strict"], da * decay, 0.0)
            ddec = jnp.where(cm["strict"], da * cm["kk"], 0.0)
            dkkb = _mx(dkk)
            dkb = _dot(dkkb, _mx(k))
            dk = dk + _dot_tn(dkkb, _mx(cm["kb"])) + dkb * b
            dbeta = dbeta + _rowsum(dkb * k)
            dat = jnp.where(cm["incl"], dat_ref[0, r, :], 0.0)
            qk = _dot_nt(_mx(q), _mx(k))
            dqk = _mx(dat * decay)
            ddec = ddec + dat * qk
            dqd = dqd_ref[0, r, :]
            dkd = dkd_ref[0, r, :]
            dq_ref[r, :] = _dot(dqk, _mx(k)) + dqd * eG
            dk_ref[r, :] = dk + _dot_tn(dqk, _mx(q)) + dkd * e2
            deg = deg + _rowsum(dqd * q)
            t2 = _rowsum(dkd * k) * e2
            dgl = jnp.sum(t2, axis=0, keepdims=True) + dcd_ref[0, c][0:1, 0:1] * cm["cd"]
            dd = ddec * decay
            dG = deg * eG - t2 + _rowsum(dd) - _rowsum(dd.T)
            rowi = lax.broadcasted_iota(jnp.int32, (CH, 1), 0)
            dG = dG + jnp.where(rowi == CH - 1, dgl, 0.0)
            lane = lax.broadcasted_iota(jnp.int32, (CH, 128), 1)
            dgb_ref[0, r, :] = jnp.where(lane == h, dG, 0.0) + jnp.where(lane == h + NH, dbeta, 0.0)

    tok = pl.BlockSpec((IROWS, HD), lambda i, h: (i, h))
    hm = pl.BlockSpec((1, IROWS, HD), lambda i, h: (h, i, 0))
    hm64 = pl.BlockSpec((1, IROWS, CH), lambda i, h: (h, i, 0))
    tile = pl.BlockSpec((1, ICH, 8, 128), lambda i, h: (h, i, 0, 0))
    tokout = jax.ShapeDtypeStruct((S, D), F32)
    return _call(
        body, name, (S // IROWS, NH),
        [tok, tok, tok, pl.BlockSpec((IROWS, 128), lambda i, h: (i, 0)), hm, hm, hm64,
         hm, hm, hm, hm, hm64, tile],
        [tok, tok, tok, pl.BlockSpec((1, IROWS, 128), lambda i, h: (h, i, 0))],
        [tokout, tokout, tokout, jax.ShapeDtypeStruct((NH, S, 128), F32)],
    )(qn, kn, v, gb, u, w, tmat, dqd, dkd, du, dw, dattn, dcdt)


def _gdn_prep_bwd(z, dqn, dkn, dv, dgb, conv_w, alog_row, dtb_row, name):
    S = z.shape[0]
    ts = GROWS
    nblk = S // ts
    scale = HD ** -0.5
    tb = ts // 8

    def body(z_ref, hp_ref, hn_ref, zab_ref, dq_ref, dqn_ref, dk_ref, dkn_ref, dv_ref, dvn_ref,
             dgb_ref, w_ref, al_ref, dt_ref, dz_ref, dzab_ref, dcw_ref, dvec_ref, buf, dybuf, dcbuf):
        i = pl.program_id(0)
        last = i == nblk - 1

        @pl.when(i == 0)
        def _():
            dcw_ref[...] = jnp.zeros_like(dcw_ref)
            dvec_ref[...] = jnp.zeros_like(dvec_ref)

        buf[0:8, :] = jnp.where(i == 0, 0.0, hp_ref[...])
        buf[8:8 + ts, :] = z_ref[...]
        buf[8 + ts:16 + ts, :] = hn_ref[...]
        rowi = lax.broadcasted_iota(jnp.int32, (ts + 8, 1), 0)
        live = jnp.logical_or(rowi < ts, jnp.logical_not(last))
        dys = ((dq_ref, dqn_ref), (dk_ref, dkn_ref), (dv_ref, dvn_ref))
        for seg in range(3):
            cs = slice(seg * D, (seg + 1) * D)
            dybuf[0:ts, :] = dys[seg][0][...]
            dybuf[ts:ts + 8, :] = dys[seg][1][...]
            c = jnp.zeros((ts + 8, D), F32)
            for j in range(4):
                c = c + w_ref[j:j + 1, cs] * buf[pl.ds(5 + j, ts + 8), cs]
            sg = _sigmoid(c)
            s = c * sg
            dsilu = sg * (1.0 + c * (1.0 - sg))
            if seg == 2:
                dcbuf[...] = jnp.where(live, dybuf[...] * dsilu, 0.0)
            else:
                mul = scale if seg == 0 else 1.0
                for h in range(NH):
                    sl = slice(h * HD, (h + 1) * HD)
                    sh = s[:, sl]
                    dy = dybuf[:, sl]
                    r = lax.rsqrt(_rowsum(sh * sh) + EPS)
                    shr = sh * r
                    ds = (mul * r) * (dy - shr * _rowsum(shr * dy))
                    dcbuf[:, sl] = jnp.where(live, ds * dsilu[:, sl], 0.0)
            dx = jnp.zeros((ts, D), F32)
            for j in range(4):
                dcw_ref[j:j + 1, cs] += _colsum(dcbuf[0:ts, :] * buf[pl.ds(5 + j, ts), cs])
                dx = dx + w_ref[j:j + 1, cs] * dcbuf[pl.ds(3 - j, ts), :]
            dz_ref[:, cs] = dx.astype(dz_ref.dtype)
        dgbs = dgb_ref[0]
        for h in range(1, NH):
            dgbs = dgbs + dgb_ref[h]
        ri = lax.broadcasted_iota(jnp.int32, (CH, CH), 0)
        ci = lax.broadcasted_iota(jnp.int32, (CH, CH), 1)
        rev = (ci >= ri).astype(F32)
        dgrev = jnp.concatenate([_dot(rev, dgbs[c * CH:(c + 1) * CH], HI) for c in range(ts // CH)], axis=0)
        lane0 = lax.broadcasted_iota(jnp.int32, dgbs.shape, 1)
        dgbs = jnp.where(lane0 < NH, dgrev, dgbs)
        zab = zab_ref[...]
        lane = lax.broadcasted_iota(jnp.int32, zab.shape, 1)
        xx = zab + dt_ref[...]
        ea = jnp.exp(al_ref[...])
        g = -ea * _softplus(xx)
        da = dgbs * (-ea) * _sigmoid(xx)
        beta = _sigmoid(zab)
        db = dgbs * beta * (1.0 - beta)
        is_a = lane < NH
        dzab = jnp.where(is_a, da, jnp.where(lane < 2 * NH, db, 0.0))
        dzab_ref[:, 0:128] = dzab.astype(dzab_ref.dtype)
        dzab_ref[:, 128:512] = jnp.zeros((ts, 384), dzab_ref.dtype)
        dvec_ref[0:1, :] += _colsum(jnp.where(is_a, dgbs * g, 0.0))
        dvec_ref[1:2, :] += _colsum(jnp.where(is_a, da, 0.0))

    z3 = pl.BlockSpec((ts, 3 * D), lambda i: (i, 0))
    row = pl.BlockSpec((ts, D), lambda i: (i, 0))
    nxt = pl.BlockSpec((8, D), lambda i: (jnp.minimum((i + 1) * tb, S // 8 - 1), 0))
    vec = pl.BlockSpec((1, 128), lambda i: (0, 0))
    return _call(
        body, name, (nblk,),
        [z3,
         pl.BlockSpec((8, 3 * D), lambda i: (jnp.maximum(i * tb - 1, 0), 0)),
         pl.BlockSpec((8, 3 * D), lambda i: (jnp.minimum((i + 1) * tb, S // 8 - 1), 0)),
         pl.BlockSpec((ts, 128), lambda i: (i, Z_AB // 128)),
         row, nxt, row, nxt, row, nxt,
         pl.BlockSpec((NH, ts, 128), lambda i: (0, i, 0)),
         _full((4, 3 * D)), vec, vec],
        [z3, pl.BlockSpec((ts, 512), lambda i: (i, 0)), _full((8, 3 * D)), _full((8, 128))],
        [jax.ShapeDtypeStruct((S, 3 * D), MXU), jax.ShapeDtypeStruct((S, 512), MXU),
         jax.ShapeDtypeStruct((8, 3 * D), F32), jax.ShapeDtypeStruct((8, 128), F32)],
        [pltpu.VMEM((ts + 16, 3 * D), F32), pltpu.VMEM((ts + 8, D), F32), pltpu.VMEM((ts + 8, D), F32)],
    )(z, z, z, z, dqn, dqn, dkn, dkn, dv, dv, dgb, conv_w, alog_row, dtb_row)


def _bias_index():
    u = np.arange(FW)[None, :]
    s = np.arange(3)[:, None]
    return np.clip(KWIN - 1 - u - QB * s, -256, 256) + 256


def _bias_vec(rel_bias_pad, onehot, name):
    def body(rb_ref, e_ref, o_ref):
        o_ref[:, 0, :] = _dot_nt(rb_ref[...], e_ref[0], HI)

    return _call(body, name, (3,),
                 [_full((NH, 640)), pl.BlockSpec((1, FW, 640), lambda s: (s, 0, 0))],
                 pl.BlockSpec((NH, 1, FW), lambda s: (s, 0, 0)),
                 jax.ShapeDtypeStruct((3 * NH, 1, FW), F32))(rel_bias_pad, onehot)


def _att_window(i):
    return pl.multiple_of(jnp.maximum(i * QB - PAST * CH, 0), QB)


def _bias_mask(fvec, name):
    def body(f_ref, o_ref):
        i = 2 - pl.program_id(0) // NH
        ws = jnp.maximum(i * QB - PAST * CH, 0)
        fb = jnp.broadcast_to(f_ref[0], (QB, FW))
        bias = pltpu.roll(fb, FW - 255, 1, stride=1, stride_axis=0)[:, :KWIN]
        qc = (i * QB + lax.broadcasted_iota(jnp.int32, (QB, KWIN), 0)) // CH
        kc = (ws + lax.broadcasted_iota(jnp.int32, (QB, KWIN), 1)) // CH
        o_ref[0] = jnp.where((kc <= qc) & (kc >= qc - PAST), bias, MASKED)

    return _call(body, name, (3 * NH,), [pl.BlockSpec((1, 1, FW), lambda j: (j, 0, 0))],
                 pl.BlockSpec((1, QB, KWIN), lambda j: (j, 0, 0)),
                 jax.ShapeDtypeStruct((3 * NH, QB, KWIN), F32))(fvec)


def _att_specs(S):
    c0 = Z_ATT // HD
    q = pl.BlockSpec((QB, HD), lambda h, i: (i, c0 + h))
    k = pl.BlockSpec((S, HD), lambda h, i: (0, c0 + NH + h))
    v = pl.BlockSpec((S, HD), lambda h, i: (0, c0 + 2 * NH + h))
    bm = pl.BlockSpec((1, QB, KWIN), lambda h, i: (jnp.maximum(2 - i, 0) * NH + h, 0, 0))
    tok = pl.BlockSpec((QB, HD), lambda h, i: (i, h))
    return q, k, v, bm, tok


def _att_fwd(z, bmask, name):
    S = z.shape[0]

    def body(q_ref, k_ref, v_ref, bm_ref, o_ref, lse_ref):
        ws = _att_window(pl.program_id(1))
        q = _mx(q_ref[...] * (HD ** -0.5))
        m = jnp.full((QB, 1), MASKED, F32)
        l = jnp.zeros((QB, 1), F32)
        acc = jnp.zeros((QB, HD), F32)
        for c in range(KWIN // KC):
            rows = pl.ds(ws + c * KC, KC)
            s = _dot_nt(q, _mx(k_ref[rows, :])) + bm_ref[0, :, c * KC:(c + 1) * KC]
            m_new = jnp.maximum(m, jnp.max(s, axis=1, keepdims=True))
            alpha = jnp.exp(m - m_new)
            p = jnp.exp(s - m_new)
            l = alpha * l + _rowsum(p)
            acc = alpha * acc + _dot(_mx(p), _mx(v_ref[rows, :]))
            m = m_new
        o_ref[...] = acc * (1.0 / l)
        lse_ref[...] = jnp.broadcast_to(m + jnp.log(l), (QB, HD))

    q, k, v, bm, tok = _att_specs(S)
    shp = jax.ShapeDtypeStruct((S, D), F32)
    return _call(body, name, (NH, S // QB), [q, k, v, bm], [tok, tok], [shp, shp])(z, z, z, bmask)


def _att_bwd(z, bmask, ob, lse, dob, name, comm=None):
    S = z.shape[0]

    def body(q_ref, k_ref, v_ref, bm_ref, o_ref, lse_ref, do_ref, dq_ref, dk_ref, dv_ref, db_ref):
        i = pl.program_id(1)
        ws = _att_window(i)
        q = _mx(q_ref[...] * (HD ** -0.5))
        do = do_ref[...]
        dob16 = _mx(do)
        drow = _rowsum(do * o_ref[...])
        lse_col = lse_ref[:, 0:1]

        @pl.when(i == 0)
        def _():
            dk_ref[...] = jnp.zeros_like(dk_ref)
            dv_ref[...] = jnp.zeros_like(dv_ref)

        @pl.when(i <= 2)
        def _():
            db_ref[...] = jnp.zeros_like(db_ref)

        dq = jnp.zeros((QB, HD), F32)
        for c in range(KWIN // KC):
            rows = pl.ds(ws + c * KC, KC)
            cols = slice(c * KC, (c + 1) * KC)
            kc = _mx(k_ref[rows, :])
            p = jnp.exp(_dot_nt(q, kc) + bm_ref[0, :, cols] - lse_col)
            ds = p * (_dot_nt(dob16, _mx(v_ref[rows, :])) - drow)
            dsb = _mx(ds)
            dq = dq + _dot(dsb, kc)
            dk_ref[rows, :] += _dot_tn(dsb, q)
            dv_ref[rows, :] += _dot_tn(_mx(p), dob16)
            db_ref[0, :, cols] += ds
        dq_ref[...] = (dq * (HD ** -0.5)).astype(dq_ref.dtype)

    q, k, v, bm, tok = _att_specs(S)
    acc = pl.BlockSpec((S, HD), lambda h, i: (0, h))
    return _call(
        body, name, (NH, S // QB), [q, k, v, bm, tok, tok, tok], [tok, acc, acc, bm],
        [jax.ShapeDtypeStruct((S, D), MXU), jax.ShapeDtypeStruct((S, D), F32),
         jax.ShapeDtypeStruct((S, D), F32), jax.ShapeDtypeStruct((3 * NH, QB, KWIN), F32)],
        comm=comm,
    )(z, z, z, bmask, ob, lse, dob)


def _bias_fold(dbias, onehot, name):
    def body(db_ref, e_ref, o_ref):
        j = pl.program_id(0)
        h = j % NH
        x = jnp.concatenate([db_ref[0], jnp.zeros((QB, FW - KWIN), F32)], axis=1)
        half = QB // 2
        while half >= 8:
            x = x[:half] + pltpu.roll(x[half:2 * half], FW - half, 1)
            half //= 2
        df = jnp.zeros((1, FW), F32)
        for r in range(8):
            df = df + pltpu.roll(x[r:r + 1], 255 - r, 1)
        contrib = _dot(df, e_ref[0], HI)
        rowh = lax.broadcasted_iota(jnp.int32, (NH, 640), 0)

        @pl.when(j == 0)
        def _():
            o_ref[...] = jnp.zeros_like(o_ref)

        o_ref[...] += jnp.where(rowh == h, contrib, 0.0)

    return _call(
        body, name, (3 * NH,),
        [pl.BlockSpec((1, QB, KWIN), lambda j: (j, 0, 0)),
         pl.BlockSpec((1, FW, 640), lambda j: (j // NH, 0, 0))],
        _full((NH, 640)), jax.ShapeDtypeStruct((NH, 640), F32),
    )(dbias, onehot)


ADA_SHARD = 6 * D // NDEV


def _ada_mod(c_all, w_ada, b_shard, name):
    def body(c_ref, w_ref, b_ref, o_ref):
        cv = c_ref[...]
        ca = cv * _sigmoid(cv)
        o_ref[0] = _dot(_mx(ca), _mx(w_ref[0])) + b_ref[0]

    return _call(
        body, name, (DEPTH,),
        [_full((NDEV, D)), pl.BlockSpec((1, D, ADA_SHARD), lambda l: (l, 0, 0)),
         pl.BlockSpec((1, 1, ADA_SHARD), lambda l: (l, 0, 0))],
        pl.BlockSpec((1, NDEV, ADA_SHARD), lambda l: (l, 0, 0)),
        jax.ShapeDtypeStruct((DEPTH, NDEV, ADA_SHARD), F32),
    )(c_all, w_ada, b_shard.reshape(DEPTH, 1, ADA_SHARD))


def _adam(g, w, m, v):
    m = ADAM_B1 * m + (1.0 - ADAM_B1) * g
    v = ADAM_B2 * v + (1.0 - ADAM_B2) * jnp.square(g)
    m_hat = m / (1.0 - ADAM_B1 ** ADAM_STEP)
    v_hat = v / (1.0 - ADAM_B2 ** ADAM_STEP)
    delta = -ADAM_LR * (m_hat / (jnp.sqrt(v_hat) + ADAM_EPS) + ADAM_WD * w)
    return delta, m, v


def _wada_adamw(c_all_t, dmod, w, m, v, name):
    def body(c_ref, d_ref, w_ref, m_ref, v_ref, g_ref, dl_ref, mo_ref, vo_ref):
        cv = c_ref[...]
        ca = cv * _sigmoid(cv)
        g = _dot(ca, d_ref[0], HI)
        g_ref[0] = g
        dl_ref[0], mo_ref[0], vo_ref[0] = _adam(g, w_ref[0], m_ref[0], v_ref[0])

    blk = pl.BlockSpec((1, D, ADA_SHARD), lambda l: (l, 0, 0))
    shp = jax.ShapeDtypeStruct((DEPTH, D, ADA_SHARD), F32)
    return _call(
        body, name, (DEPTH,),
        [_full((D, NDEV)), pl.BlockSpec((1, NDEV, ADA_SHARD), lambda l: (l, 0, 0)), blk, blk, blk],
        [blk] * 4, [shp] * 4,
    )(c_all_t, dmod, w, m, v)


def _adamw_reduce(parts, w, m, v, name, tr):
    P, R, C = parts.shape

    def body(p_ref, w_ref, m_ref, v_ref, g_ref, dl_ref, mo_ref, vo_ref):
        g = p_ref[0].astype(F32)
        for k in range(1, P):
            g = g + p_ref[k].astype(F32)
        g_ref[...] = g
        dl_ref[...], mo_ref[...], vo_ref[...] = _adam(g, w_ref[...], m_ref[...], v_ref[...])

    blk = pl.BlockSpec((tr, C), lambda i: (i, 0))
    shp = jax.ShapeDtypeStruct((R, C), F32)
    return _call(body, name, (R // tr,), [pl.BlockSpec((P, tr, C), lambda i: (0, i, 0)), blk, blk, blk],
                 [blk] * 4, [shp] * 4)(parts, w, m, v)


def _sum_parts(parts, name):
    P, R, C = parts.shape

    def body(p_ref, o_ref):
        g = p_ref[0]
        for k in range(1, P):
            g = g + p_ref[k]
        o_ref[...] = g

    return _call(body, name, (1,), [_full((P, R, C))], _full((R, C)),
                 jax.ShapeDtypeStruct((R, C), F32))(parts)


def _pack_rows(vecs, width=1024):
    flat = jnp.concatenate([a.reshape(-1) for a in vecs])
    n = flat.shape[0]
    rows = -(-n // width)
    rows = -(-rows // 8) * 8
    return jnp.pad(flat, (0, rows * width - n)).reshape(rows, width)


def _unpack_rows(packed, shapes):
    flat = packed.reshape(-1)
    out, off = [], 0
    for s in shapes:
        n = int(np.prod(s)) if len(s) else 1
        out.append(flat[off:off + n].reshape(s))
        off += n
    return out


BIG = ("w_in", "w_out", "w_ff_in", "w_ff_out")


def _pick(kind, d, n):
    return None if d is None else (kind, d[n])


def _forward_layer(x, mod_l, p, nxt=None):
    sh1, sc1, gt1, sh2, sc2, gt2 = [mod_l[k][None] for k in range(6)]
    P = functools.partial
    h = _modnorm_fwd(x, p["norm_mix"], sc1, sh1, "norm_mix_fwd")
    z, g_in = _hosted(P(_mm_nn, h, p["wz"], "in_proj", tm=2048, tn=512), _pick("gather", nxt, "w_in"))
    qn, kn, v, gb = _gdn_prep_fwd(z, p["conv_w"], p["alog"], p["dtb"], "gdn_prep_fwd")
    u, w, qd, kd, attn, tmat, cdt = _gdn_intra_fwd(qn, kn, v, gb, "gdn_intra_fwd")
    o, vn, st = _gdn_scan_fwd(u, w, qd, kd, attn, cdt, "gdn_scan_fwd")
    ob, lse = _att_fwd(z, p["bmask"], "att_fwd")
    m = _merge_fwd(o, z, ob, p["gdn_norm"], "merge_fwd")
    x1, g_out = _hosted(P(_mm_nn, m, p["wout"], "out_proj", mode="resid", res=x, gate=gt1),
                        _pick("gather", nxt, "w_out"))
    h2 = _modnorm_fwd(x1, p["norm_mlp"], sc2, sh2, "norm_mlp_fwd")
    (a, r), g_w1 = _hosted(P(_mm_nn, h2, p["w1"], "ff_in", mode="relu2"), _pick("gather", nxt, "w_ff_in"))
    x2, g_w2 = _hosted(P(_mm_nn, r, p["w2"], "ff_out", mode="resid", res=x1, gate=gt2),
                       _pick("gather", nxt, "w_ff_out"))
    saved = dict(x=x, h=h, z=z, qn=qn, kn=kn, v=v, gb=gb, u=u, w=w, qd=qd, kd=kd, attn=attn,
                 tmat=tmat, cdt=cdt, o=o, vn=vn, st=st, ob=ob, lse=lse, m=m, x1=x1, h2=h2, a=a, r=r)
    gathered = None if nxt is None else dict(w_in=g_in, w_out=g_out, w_ff_in=g_w1, w_ff_out=g_w2)
    return x2, saved, gathered


def _backward_layer(dx2, mod_l, p, s, onehot, send=None):
    sh1, sc1, gt1, sh2, sc2, gt2 = [mod_l[k][None] for k in range(6)]
    P = functools.partial
    (dw2, dgt2), r_w2 = _hosted(P(_mm_tn, s["r"], dx2, "ff_out_dw", gate=gt2, w=p["w2"]),
                                _pick("a2a", send, "w_ff_out"))
    da = _mm_nt(dx2, p["w2"], "ff_out_dx", gate=gt2, drelu=s["a"])
    dw1, r_w1 = _hosted(P(_mm_tn, s["h2"], da, "ff_in_dw"), _pick("a2a", send, "w_ff_in"))
    dh2 = _mm_nt(da, p["w1"], "ff_in_dx")
    dx1, dsc2, dsh2, dnmlp = _modnorm_bwd(dh2, s["x1"], p["norm_mlp"], sc2, sh2, dx2, "norm_mlp_bwd")
    (dwout, dgt1), r_out = _hosted(P(_mm_tn, s["m"], dx1, "out_proj_dw", gate=gt1, w=p["wout"]),
                                   _pick("a2a", send, "w_out"))
    dm = _mm_nt(dx1, p["wout"], "out_proj_dx", gate=gt1)
    do, dzg, dob, dza, dzb, dgn = _merge_bwd(dm, s["o"], s["z"], s["ob"], p["gdn_norm"], "merge_bwd")
    (dq_att, dk_att, dv_att, dbias), r_in = _hosted(P(_att_bwd, s["z"], p["bmask"], s["ob"], s["lse"], dob, "att_bwd"),
                                                    _pick("a2a", send, "w_in"))
    drb = _bias_fold(dbias, onehot, "rel_bias_fold")[:, :513]
    dqd, dkd, dvn, dw, dattn, dcdt = _gdn_scan_bwd(do, s["w"], s["qd"], s["kd"], s["attn"], s["cdt"],
                                                   s["vn"], s["st"], "gdn_scan_bwd")
    dqn, dkn, dv, dgb = _gdn_intra_bwd(s["qn"], s["kn"], s["v"], s["gb"], s["u"], s["w"], s["tmat"],
                                       dqd, dkd, dvn, dw, dattn, dcdt, "gdn_intra_bwd")
    dzq, dzab, dcw, dvec = _gdn_prep_bwd(s["z"], dqn, dkn, dv, dgb, p["conv_w"], p["alog"], p["dtb"],
                                         "gdn_prep_bwd")
    dz = (dzq, dzg, dq_att, dk_att, dv_att, dza, dzb, dzab)
    dwz = _in_proj_dw(s["h"], dz, "in_proj_dw")
    dh = _in_proj_dx(dz, p["wz"], "in_proj_dx")
    dx, dsc1, dsh1, dnmix = _modnorm_bwd(dh, s["x"], p["norm_mix"], sc1, sh1, dx1, "norm_mix_bwd")
    dw_in = jnp.concatenate([dwz[:, :Z_ATT], dwz[:, Z_AB:Z_AB + 2 * NH], dwz[:, Z_ATT:Z_AB]], axis=1)
    grads = dict(w_in=dw_in, w_out=dwout, w_ff_in=dw1, w_ff_out=dw2, norm_mix=dnmix[0], norm_mlp=dnmlp[0],
                 conv_w=dcw[:4], a_log=dvec[0, :NH], dt_bias=dvec[1, :NH], gdn_norm=dgn[0], rel_bias=drb,
                 mod=jnp.concatenate([dsh1, dsc1, dgt1, dsh2, dsc2, dgt2], axis=1)[0])
    recv = None if send is None else dict(w_in=r_in, w_out=r_out, w_ff_in=r_w1, w_ff_out=r_w2)
    return dx, grads, recv


def _relayout_weights(g):
    w_in_full = jnp.transpose(g["w_in"], (1, 0, 2)).reshape(D, IN_W)
    wz = jnp.concatenate([w_in_full[:, :Z_ATT], w_in_full[:, Z_ATT + 2 * NH:], w_in_full[:, Z_ATT:Z_ATT + 2 * NH],
                          jnp.zeros((D, ZW - IN_W), MXU)], axis=-1)
    return dict(wz=wz, wout=g["w_out"].reshape(D, D),
                w1=jnp.transpose(g["w_ff_in"], (1, 0, 2)).reshape(D, DFF), w2=g["w_ff_out"].reshape(DFF, D))


def _to_owners(g):
    col = lambda a: jnp.transpose(a.astype(MXU).reshape(a.shape[0], NDEV, -1), (1, 0, 2))
    row = lambda a: a.astype(MXU).reshape(NDEV, -1, a.shape[1])
    return dict(w_in=col(g["w_in"]), w_out=row(g["w_out"]), w_ff_in=col(g["w_ff_in"]), w_ff_out=row(g["w_ff_out"]))


def _bias_onehot():
    return (jnp.asarray(_bias_index())[:, :, None] == jnp.arange(640)[None, None, :]).astype(F32)


def _layer_params(l, conv_full, norm_mix, norm_mlp, a_log, dt_bias, gdn_norm, rel_bias, onehot):
    pad = lambda a: jnp.pad(a, (0, 128 - NH))[None]
    fvec = _bias_vec(jnp.pad(rel_bias[l], ((0, 0), (0, 640 - rel_bias.shape[2]))), onehot, "rel_bias_vec")
    return dict(conv_w=conv_full[l], norm_mix=norm_mix[l][None], norm_mlp=norm_mlp[l][None], alog=pad(a_log[l]),
                dtb=pad(dt_bias[l]), gdn_norm=gdn_norm[l][None], bmask=_bias_mask(fvec, "rel_bias_mask"))


def _local_step(x, target, mod, small, shards, final_norm, onehot):
    L = len(small)
    gathered = {n: _all_gather(shards[0][n], "gather_" + n) for n in BIG}
    saved, params = [], []
    for l in range(L):
        params.append({**small[l], **_relayout_weights(gathered)})
        x, sv, gathered = _forward_layer(x, mod[l].reshape(6, D), params[l], shards[l + 1] if l + 1 < L else None)
        saved.append(sv)
    loss, dx, dfn = _loss_head(x, target, final_norm[None], "loss_head")
    grads, recv, send = [None] * L, [None] * L, None
    for l in reversed(range(L)):
        dx, grads[l], got = _backward_layer(dx, mod[l].reshape(6, D), params[l], saved[l], onehot, send)
        if send is not None:
            recv[l + 1] = got
        send = _to_owners(grads[l])
    recv[0] = {n: _all_to_all(send[n], "exchange_" + n) for n in BIG}
    return loss, dx, grads, dfn[0], recv


SMALL = ("b_ada", "norm_mix", "norm_mlp", "a_log", "dt_bias", "gdn_norm", "rel_bias", "final_norm")


def kernel(x, c, w_ada, b_ada, norm_mix, norm_mlp, w_in, conv_w, a_log, dt_bias, gdn_norm, rel_bias, w_out, w_ff_in, w_ff_out, final_norm, loss_target, m_w_ada, m_b_ada, m_norm_mix, m_norm_mlp, m_w_in, m_conv_w, m_a_log, m_dt_bias, m_gdn_norm, m_rel_bias, m_w_out, m_w_ff_in, m_w_ff_out, m_final_norm, v_w_ada, v_b_ada, v_norm_mix, v_norm_mlp, v_w_in, v_conv_w, v_a_log, v_dt_bias, v_gdn_norm, v_rel_bias, v_w_out, v_w_ff_in, v_w_ff_out, v_final_norm):
    W = dict(w_ada=w_ada, b_ada=b_ada, norm_mix=norm_mix, norm_mlp=norm_mlp, w_in=w_in, conv_w=conv_w,
             a_log=a_log, dt_bias=dt_bias, gdn_norm=gdn_norm, rel_bias=rel_bias, w_out=w_out,
             w_ff_in=w_ff_in, w_ff_out=w_ff_out, final_norm=final_norm)
    Mo = dict(w_ada=m_w_ada, b_ada=m_b_ada, norm_mix=m_norm_mix, norm_mlp=m_norm_mlp, w_in=m_w_in,
              conv_w=m_conv_w, a_log=m_a_log, dt_bias=m_dt_bias, gdn_norm=m_gdn_norm, rel_bias=m_rel_bias,
              w_out=m_w_out, w_ff_in=m_w_ff_in, w_ff_out=m_w_ff_out, final_norm=m_final_norm)
    Vo = dict(w_ada=v_w_ada, b_ada=v_b_ada, norm_mix=v_norm_mix, norm_mlp=v_norm_mlp, w_in=v_w_in,
              conv_w=v_conv_w, a_log=v_a_log, dt_bias=v_dt_bias, gdn_norm=v_gdn_norm, rel_bias=v_rel_bias,
              w_out=v_w_out, w_ff_in=v_w_ff_in, w_ff_out=v_w_ff_out, final_norm=v_final_norm)
    L = w_in.shape[0]
    me = _flat(_mesh_pos())
    cshard = conv_w.shape[2]

    small_in = _all_gather(_pack_rows([c, conv_w]), "gather_c_conv")
    c_all = small_in[:, 0, :]
    conv_full = small_in.reshape(NDEV, -1)[:, D:D + L * 4 * cshard].reshape(NDEV, L, 4, cshard)
    conv_full = jnp.transpose(conv_full, (1, 2, 0, 3)).reshape(L, 4, NDEV * cshard)

    b_shard = lax.dynamic_slice_in_dim(b_ada, me * ADA_SHARD, ADA_SHARD, axis=1)
    mod_all = _all_gather(_ada_mod(c_all, w_ada, b_shard, "ada_mod"), "gather_mod")
    mod = lax.dynamic_index_in_dim(mod_all, me, axis=2, keepdims=False)
    mod = jnp.transpose(mod, (1, 0, 2)).reshape(L, 6 * D)

    onehot = _bias_onehot()
    small = [_layer_params(l, conv_full, norm_mix, norm_mlp, a_log, dt_bias, gdn_norm, rel_bias, onehot)
             for l in range(L)]
    shards = [{n: W[n][l].astype(MXU) for n in BIG} for l in range(L)]
    loss, dx, grads, dfn, recv = _local_step(x[0], loss_target[0], mod, small, shards, final_norm, onehot)

    def stack(name):
        return jnp.stack([g[name] for g in grads])

    small_names = ("mod", "norm_mix", "norm_mlp", "a_log", "dt_bias", "gdn_norm", "rel_bias")
    small_parts = [stack(n) for n in small_names] + [dfn, stack("conv_w"), loss[0, 0:1]]
    small_shapes = [a.shape for a in small_parts]
    gathered = _all_gather(_pack_rows(small_parts), "gather_small_grads")
    total = _unpack_rows(_sum_parts(gathered, "sum_small_grads"), small_shapes)
    tot = dict(zip(small_names + ("final_norm", "conv_w", "loss"), total))
    tot["b_ada"] = tot.pop("mod")
    tot["conv_w"] = lax.dynamic_slice_in_dim(tot["conv_w"], me * cshard, cshard, axis=2)

    out_g, out_d, out_m, out_v = {}, {}, {}, {}
    names = SMALL + ("conv_w",)
    shapes = [W[n].shape for n in names]
    packed = [_pack_rows([src[n] for n in names])[None] if src is tot else _pack_rows([src[n] for n in names])
              for src in (tot, W, Mo, Vo)]
    res = _adamw_reduce(*packed, "adamw_small", tr=8)
    for dst, arr in zip((out_g, out_d, out_m, out_v), res):
        dst.update(zip(names, _unpack_rows(arr, shapes)))

    dmod_all = gathered.reshape(NDEV, -1)[:, :L * 6 * D].reshape(NDEV, L, 6 * D)
    dmod_mine = jnp.transpose(lax.dynamic_slice_in_dim(dmod_all, me * ADA_SHARD, ADA_SHARD, axis=2), (1, 0, 2))
    res = _wada_adamw(jnp.transpose(c_all), dmod_mine, w_ada, m_w_ada, v_w_ada, "adamw_w_ada")
    for dst, arr in zip((out_g, out_d, out_m, out_v), res):
        dst["w_ada"] = arr

    for name, tr in (("w_in", 256), ("w_out", 128), ("w_ff_in", 256), ("w_ff_out", 256)):
        sh = W[name].shape
        rows = int(np.prod(sh[:-1]))
        flat = lambda a: a.reshape(rows, sh[-1])
        parts = jnp.stack([recv[l][name] for l in range(L)], axis=1).reshape(NDEV, rows, sh[-1])
        res = _adamw_reduce(parts, flat(W[name]), flat(Mo[name]), flat(Vo[name]), "adamw_" + name, tr=tr)
        for dst, arr in zip((out_g, out_d, out_m, out_v), res):
            dst[name] = arr.reshape(sh)

    order = ("w_ada", "b_ada", "norm_mix", "norm_mlp", "w_in", "conv_w", "a_log", "dt_bias", "gdn_norm",
             "rel_bias", "w_out", "w_ff_in", "w_ff_out", "final_norm")
    return (tot["loss"].reshape(()), dx[None], *[out_g[n] for n in order], *[out_d[n] for n in order],
            *[out_m[n] for n in order], *[out_v[n] for n in order])
```

```python
import functools
import math

import numpy as np
import jax
import jax.numpy as jnp
from jax import lax
from jax.experimental import pallas as pl
from jax.experimental.pallas import tpu as pltpu

F32 = jnp.float32
MXU = jnp.bfloat16
HI = lax.Precision.HIGHEST
MESH_ID = pl.DeviceIdType.MESH

D = 1024
NH = 8
HD = 128
CH = 64
PAST = 8
DFF = 4096
EPS = 1e-6
NDEV = 8
DEPTH = 4
IN_W = 9232
ZW = 9728
Z_GATE, Z_ATT, Z_BR, Z_AB = 3072, 4096, 7168, 9216
QB = 256
MASKED = -1e30
KWIN = 768
FW = 1024
ADAM_LR, ADAM_B1, ADAM_B2, ADAM_EPS, ADAM_WD, ADAM_STEP = 0.001, 0.9, 0.999, 1e-08, 0.01, 10


def _dot(a, b, prec=None):
    return jnp.dot(a, b, preferred_element_type=F32, precision=prec)


def _dot_nt(a, b, prec=None):
    return lax.dot_general(a, b, (((1,), (1,)), ((), ())), preferred_element_type=F32, precision=prec)


def _dot_tn(a, b, prec=None):
    return lax.dot_general(a, b, (((0,), (0,)), ((), ())), preferred_element_type=F32, precision=prec)


def _mx(a):
    return a.astype(MXU)


def _sigmoid(x):
    return 1.0 / (1.0 + jnp.exp(-x))


def _softplus(x):
    return jnp.maximum(x, 0.0) + jnp.log(1.0 + jnp.exp(-jnp.abs(x)))


def _rowsum(x):
    return jnp.sum(x, axis=1, keepdims=True)


def _colsum(x):
    return jnp.sum(x, axis=0, keepdims=True)


def _call(body, name, grid, in_specs, out_specs, out_shape, scratch=(), comm=None):
    if comm is None:
        return pl.pallas_call(body, name=name, grid=grid, in_specs=in_specs, out_specs=out_specs,
                              out_shape=out_shape, scratch_shapes=list(scratch))
    kind, x = comm
    single = not isinstance(out_specs, (list, tuple))
    o_specs = [out_specs] if single else list(out_specs)
    o_shape = [out_shape] if single else list(out_shape)
    n_in, n_out, n_scr = len(in_specs), len(o_specs), len(scratch)
    c_shape = (NDEV,) + x.shape if kind == "gather" else x.shape

    def wrapped(*refs):
        ins, x_ref = refs[:n_in], refs[n_in]
        outs, c_ref = refs[n_in + 1:n_in + 1 + n_out], refs[n_in + 1 + n_out]
        scr = refs[n_in + 2 + n_out:n_in + 2 + n_out + n_scr]
        sems = refs[n_in + 2 + n_out + n_scr:]
        first = functools.reduce(jnp.logical_and, [pl.program_id(a) == 0 for a in range(len(grid))])
        last = functools.reduce(jnp.logical_and, [pl.program_id(a) == grid[a] - 1 for a in range(len(grid))])

        @pl.when(first)
        def _():
            _comm_start(*_comm_copies(kind, x_ref, c_ref, *sems))

        body(*ins, *outs, *scr)

        @pl.when(last)
        def _():
            _comm_wait(*_comm_copies(kind, x_ref, c_ref, *sems))

    any_spec = pl.BlockSpec(memory_space=pl.ANY)
    call = pl.pallas_call(
        wrapped, name=name, grid=grid, in_specs=list(in_specs) + [any_spec], out_specs=o_specs + [any_spec],
        out_shape=o_shape + [jax.ShapeDtypeStruct(c_shape, x.dtype)],
        scratch_shapes=list(scratch) + _comm_sems())

    def run(*args):
        res = call(*args, x)
        return (res[0] if single else list(res[:-1])), res[-1]

    return run


def _hosted(fn, comm):
    return (fn(), None) if comm is None else fn(comm=comm)


def _full(shape):
    n = len(shape)
    return pl.BlockSpec(shape, lambda *_: (0,) * n)


def _mesh_pos():
    return lax.axis_index("x"), lax.axis_index("y"), lax.axis_index("c")


def _peer(pos, k):
    x, y, c = pos
    return (x ^ ((k >> 2) & 1), y ^ ((k >> 1) & 1), c ^ (k & 1))


def _flat(pos):
    return 4 * pos[0] + 2 * pos[1] + pos[2]


def _comm_sems():
    return [pltpu.SemaphoreType.DMA((NDEV - 1,)), pltpu.SemaphoreType.DMA((NDEV - 1,)), pltpu.SemaphoreType.DMA]


def _comm_copies(kind, x_ref, out_ref, send_sems, recv_sems, local_sem):
    pos = _mesh_pos()
    me = _flat(pos)
    src = (lambda d: x_ref) if kind == "gather" else (lambda d: x_ref.at[d])
    mine = pltpu.make_async_copy(src(me), out_ref.at[me], local_sem)
    sends, recvs = [], []
    for k in range(1, NDEV):
        peer = _peer(pos, k)
        pid = _flat(peer)
        sems = dict(send_sem=send_sems.at[k - 1], recv_sem=recv_sems.at[k - 1], device_id=peer,
                    device_id_type=MESH_ID)
        sends.append(pltpu.make_async_remote_copy(src_ref=src(pid), dst_ref=out_ref.at[me], **sems))
        recvs.append(pltpu.make_async_remote_copy(src_ref=src(pid), dst_ref=out_ref.at[pid], **sems))
    return mine, sends, recvs


def _comm_start(mine, sends, recvs):
    mine.start()
    for cp in sends:
        cp.start()


def _comm_wait(mine, sends, recvs):
    for cp in recvs:
        cp.wait_recv()
    for cp in sends:
        cp.wait_send()
    mine.wait()


def _collective(kind, x, name):
    def body(x_ref, out_ref, *sems):
        copies = _comm_copies(kind, x_ref, out_ref, *sems)
        _comm_start(*copies)
        _comm_wait(*copies)

    shape = (NDEV,) + x.shape if kind == "gather" else x.shape
    return pl.pallas_call(
        body, name=name, out_shape=jax.ShapeDtypeStruct(shape, x.dtype),
        in_specs=[pl.BlockSpec(memory_space=pl.ANY)], out_specs=pl.BlockSpec(memory_space=pl.ANY),
        scratch_shapes=_comm_sems())(x)


def _all_gather(x, name):
    return _collective("gather", x, name)


def _mm_nn(a, w, name, *, mode="plain", res=None, gate=None, tm=1024, tn=1024, tk=1024, comm=None):
    M, K = a.shape
    N = w.shape[1]
    tm, tk, tn = min(tm, M), min(tk, K), min(tn, N)
    nk = K // tk

    def body(*refs):
        refs = list(refs)
        acc = refs.pop() if nk > 1 else None
        if mode == "resid":
            a_ref, w_ref, res_ref, gate_ref, o_ref = refs
        elif mode == "relu2":
            a_ref, w_ref, o_ref, r_ref = refs
        else:
            a_ref, w_ref, o_ref = refs
        k = pl.program_id(2)
        part = _dot(_mx(a_ref[...]), w_ref[...])

        def finish(r):
            if mode == "resid":
                o_ref[...] = res_ref[...] + gate_ref[...] * r
            elif mode == "relu2":
                o_ref[...] = r
                r_ref[...] = jnp.square(jnp.maximum(r, 0.0)).astype(r_ref.dtype)
            else:
                o_ref[...] = r

        if nk == 1:
            finish(part)
        else:
            @pl.when(k == 0)
            def _():
                acc[...] = part

            @pl.when((k > 0) & (k < nk - 1))
            def _():
                acc[...] += part

            @pl.when(k == nk - 1)
            def _():
                finish(acc[...] + part)

    in_specs = [pl.BlockSpec((tm, tk), lambda i, j, k: (i, k)),
                pl.BlockSpec((tk, tn), lambda i, j, k: (k, j))]
    args = [a, w]
    o_spec = pl.BlockSpec((tm, tn), lambda i, j, k: (i, j))
    out_specs, out_shape = o_spec, jax.ShapeDtypeStruct((M, N), F32)
    if mode == "resid":
        in_specs += [o_spec, pl.BlockSpec((1, tn), lambda i, j, k: (0, j))]
        args += [res, gate]
    elif mode == "relu2":
        out_specs = [o_spec, o_spec]
        out_shape = [jax.ShapeDtypeStruct((M, N), F32), jax.ShapeDtypeStruct((M, N), MXU)]
    return _call(body, name, (M // tm, N // tn, nk), in_specs, out_specs, out_shape,
                 [pltpu.VMEM((tm, tn), F32)] if nk > 1 else [], comm=comm)(*args)


def _mm_nt(a, w, name, *, gate=None, drelu=None, tm=1024, tko=1024, tn=1024, comm=None):
    M, N = a.shape
    K = w.shape[0]
    tm, tn, tko = min(tm, M), min(tn, N), min(tko, K)
    nn = N // tn

    def body(*refs):
        refs = list(refs)
        acc = refs.pop() if nn > 1 else None
        a_ref, w_ref = refs[:2]
        rest = refs[2:]
        gate_ref = rest.pop(0) if gate is not None else None
        pre_ref = rest.pop(0) if drelu is not None else None
        (o_ref,) = rest
        n = pl.program_id(2)
        av = a_ref[...]
        if gate_ref is not None:
            av = av * gate_ref[...]
        part = _dot_nt(_mx(av), w_ref[...])

        def finish(r):
            if pre_ref is not None:
                r = r * (2.0 * jnp.maximum(pre_ref[...], 0.0))
            o_ref[...] = r.astype(o_ref.dtype)

        if nn == 1:
            finish(part)
        else:
            @pl.when(n == 0)
            def _():
                acc[...] = part

            @pl.when((n > 0) & (n < nn - 1))
            def _():
                acc[...] += part

            @pl.when(n == nn - 1)
            def _():
                finish(acc[...] + part)

    in_specs = [pl.BlockSpec((tm, tn), lambda i, j, n: (i, n)),
                pl.BlockSpec((tko, tn), lambda i, j, n: (j, n))]
    args = [a, w]
    if gate is not None:
        in_specs.append(pl.BlockSpec((1, tn), lambda i, j, n: (0, n)))
        args.append(gate)
    o_spec = pl.BlockSpec((tm, tko), lambda i, j, n: (i, j))
    if drelu is not None:
        in_specs.append(o_spec)
        args.append(drelu)
    out_dtype = MXU if drelu is not None else F32
    return _call(body, name, (M // tm, K // tko, nn), in_specs, o_spec,
                 jax.ShapeDtypeStruct((M, K), out_dtype), [pltpu.VMEM((tm, tko), F32)] if nn > 1 else [],
                 comm=comm)(*args)


def _mm_tn(a, b, name, *, gate=None, w=None, tk=1024, tn=1024, tm=1024, comm=None):
    M, K = a.shape
    N = b.shape[1]
    tm, tk, tn = min(tm, M), min(tk, K), min(tn, N)
    nm = M // tm
    gated = gate is not None

    def body(*refs):
        if gated:
            a_ref, b_ref, gate_ref, w_ref, o_ref, dg_ref, acc = refs
        else:
            a_ref, b_ref, o_ref, acc = refs
        kk = pl.program_id(1)
        m = pl.program_id(2)
        part = _dot_tn(_mx(a_ref[...]), _mx(b_ref[...]))

        @pl.when((m == 0) & (nm > 1))
        def _():
            acc[...] = part

        @pl.when((m > 0) & (m < nm - 1))
        def _():
            acc[...] += part

        if gated:
            @pl.when((m == 0) & (kk == 0))
            def _():
                dg_ref[...] = jnp.zeros_like(dg_ref)

        @pl.when(m == nm - 1)
        def _():
            r = acc[...] + part if nm > 1 else part
            if gated:
                o_ref[...] = r * gate_ref[...]
                dg_ref[...] += _colsum(r * w_ref[...].astype(F32))
            else:
                o_ref[...] = r

    in_specs = [pl.BlockSpec((tm, tk), lambda j, k, m: (m, k)),
                pl.BlockSpec((tm, tn), lambda j, k, m: (m, j))]
    args = [a, b]
    o_spec = pl.BlockSpec((tk, tn), lambda j, k, m: (k, j))
    out_specs, out_shape = o_spec, jax.ShapeDtypeStruct((K, N), F32)
    if gated:
        in_specs += [pl.BlockSpec((1, tn), lambda j, k, m: (0, j)), o_spec]
        args += [gate, w]
        out_specs = [o_spec, pl.BlockSpec((1, tn), lambda j, k, m: (0, j))]
        out_shape = [out_shape, jax.ShapeDtypeStruct((1, N), F32)]
    return _call(body, name, (N // tn, K // tk, nm), in_specs, out_specs, out_shape,
                 [pltpu.VMEM((tk, tn), F32)], comm=comm)(*args)


SEG_T = 512


def _seg_layout(segs):
    starts, t = [], 0
    for a in segs:
        starts.append(t)
        t += a.shape[1] // SEG_T
    return starts, t


def _seg_spec(tm, lo, hi, row_axis, col_axis):
    def index(*ids):
        col = ids[col_axis]
        act = (col >= lo) & (col < hi)
        return jnp.where(act, ids[row_axis], 0), jnp.where(act, col - lo, 0)

    return pl.BlockSpec((tm, SEG_T), index)


def _in_proj_dw(h, segs, name, tm=1024):
    S = h.shape[0]
    tm = min(tm, S)
    nm = S // tm
    starts, ntile = _seg_layout(segs)
    bounds = [(lo, lo + a.shape[1] // SEG_T) for lo, a in zip(starts, segs)]

    def body(*refs):
        h_ref, seg_refs, o_ref, acc = refs[0], refs[1:1 + len(segs)], refs[-2], refs[-1]
        j = pl.program_id(0)
        m = pl.program_id(1)
        for (lo, hi), b_ref in zip(bounds, seg_refs):
            @pl.when((j >= lo) & (j < hi))
            def _():
                part = _dot_tn(h_ref[...], _mx(b_ref[...]))
                if nm == 1:
                    o_ref[...] = part
                else:
                    @pl.when(m == 0)
                    def _():
                        acc[...] = part

                    @pl.when((m > 0) & (m < nm - 1))
                    def _():
                        acc[...] += part

                    @pl.when(m == nm - 1)
                    def _():
                        o_ref[...] = acc[...] + part

    return _call(
        body, name, (ntile, nm),
        [pl.BlockSpec((tm, D), lambda j, m: (m, 0))] + [_seg_spec(tm, lo, hi, 1, 0) for lo, hi in bounds],
        pl.BlockSpec((D, SEG_T), lambda j, m: (0, j)), jax.ShapeDtypeStruct((D, ntile * SEG_T), F32),
        [pltpu.VMEM((D, SEG_T), F32)])(h, *segs)


def _in_proj_dx(segs, w, name, tm=1024, comm=None):
    S = segs[0].shape[0]
    tm = min(tm, S)
    starts, ntile = _seg_layout(segs)
    bounds = [(lo, lo + a.shape[1] // SEG_T) for lo, a in zip(starts, segs)]

    def body(*refs):
        seg_refs, w_ref, o_ref, acc = refs[:len(segs)], refs[-3], refs[-2], refs[-1]
        n = pl.program_id(1)
        for (lo, hi), a_ref in zip(bounds, seg_refs):
            @pl.when((n >= lo) & (n < hi))
            def _():
                part = _dot_nt(_mx(a_ref[...]), w_ref[...])

                @pl.when(n == 0)
                def _():
                    acc[...] = part

                @pl.when((n > 0) & (n < ntile - 1))
                def _():
                    acc[...] += part

                @pl.when(n == ntile - 1)
                def _():
                    o_ref[...] = acc[...] + part

    return _call(
        body, name, (S // tm, ntile),
        [_seg_spec(tm, lo, hi, 0, 1) for lo, hi in bounds] + [pl.BlockSpec((D, SEG_T), lambda i, n: (0, n))],
        pl.BlockSpec((tm, D), lambda i, n: (i, 0)), jax.ShapeDtypeStruct((S, D), F32),
        [pltpu.VMEM((tm, D), F32)], comm=comm)(*segs, w)


def _modnorm_fwd(x, gain, sc, sh, name, ts=512):
    S = x.shape[0]

    def body(x_ref, g_ref, sc_ref, sh_ref, h_ref):
        xv = x_ref[...]
        r = lax.rsqrt(jnp.mean(xv * xv, axis=1, keepdims=True) + EPS)
        h_ref[...] = ((xv * r * g_ref[...]) * (1.0 + sc_ref[...]) + sh_ref[...]).astype(h_ref.dtype)

    row = pl.BlockSpec((ts, D), lambda i: (i, 0))
    vec = pl.BlockSpec((1, D), lambda i: (0, 0))
    return _call(body, name, (S // ts,), [row, vec, vec, vec], row,
                 jax.ShapeDtypeStruct((S, D), MXU))(x, gain, sc, sh)


def _modnorm_bwd(dh, x, gain, sc, sh, dx_in, name, ts=512):
    S = x.shape[0]

    def body(dh_ref, x_ref, g_ref, sc_ref, sh_ref, dxin_ref, dx_ref, dsc_ref, dsh_ref, dg_ref):
        i = pl.program_id(0)
        xv = x_ref[...]
        dhv = dh_ref[...]
        g = g_ref[...]
        r = lax.rsqrt(jnp.mean(xv * xv, axis=1, keepdims=True) + EPS)
        xr = xv * r
        dn = dhv * (1.0 + sc_ref[...])
        u = dn * g
        dx_ref[...] = dxin_ref[...] + r * (u - xr * jnp.mean(xr * u, axis=1, keepdims=True))

        @pl.when(i == 0)
        def _():
            dsc_ref[...] = jnp.zeros_like(dsc_ref)
            dsh_ref[...] = jnp.zeros_like(dsh_ref)
            dg_ref[...] = jnp.zeros_like(dg_ref)

        dsc_ref[...] += _colsum(dhv * (xr * g))
        dsh_ref[...] += _colsum(dhv)
        dg_ref[...] += _colsum(dn * xr)

    row = pl.BlockSpec((ts, D), lambda i: (i, 0))
    vec = pl.BlockSpec((1, D), lambda i: (0, 0))
    vshape = jax.ShapeDtypeStruct((1, D), F32)
    return _call(body, name, (S // ts,), [row, row, vec, vec, vec, row], [row, vec, vec, vec],
                 [jax.ShapeDtypeStruct((S, D), F32), vshape, vshape, vshape])(dh, x, gain, sc, sh, dx_in)


def _loss_head(x, target, gain, name, ts=512):
    S = x.shape[0]

    def body(x_ref, t_ref, g_ref, loss_ref, dx_ref, dg_ref):
        i = pl.program_id(0)
        xv = x_ref[...]
        g = g_ref[...]
        r = lax.rsqrt(jnp.mean(xv * xv, axis=1, keepdims=True) + EPS)
        xr = xv * r
        e = xr * g - t_ref[...]
        dy = e * (1.0 / D)
        u = dy * g
        dx_ref[...] = r * (u - xr * jnp.mean(xr * u, axis=1, keepdims=True))

        @pl.when(i == 0)
        def _():
            loss_ref[...] = jnp.zeros_like(loss_ref)
            dg_ref[...] = jnp.zeros_like(dg_ref)

        part = 0.5 * jnp.sum(jnp.mean(e * e, axis=1, keepdims=True), axis=0, keepdims=True)
        loss_ref[...] += jnp.broadcast_to(part, loss_ref.shape)
        dg_ref[...] += _colsum(dy * xr)

    row = pl.BlockSpec((ts, D), lambda i: (i, 0))
    vec = pl.BlockSpec((1, D), lambda i: (0, 0))
    return _call(body, name, (S // ts,), [row, row, vec],
                 [pl.BlockSpec((1, 128), lambda i: (0, 0)), row, vec],
                 [jax.ShapeDtypeStruct((1, 128), F32), jax.ShapeDtypeStruct((S, D), F32),
                  jax.ShapeDtypeStruct((1, D), F32)])(x, target, gain)


def _merge_specs(ts):
    o_spec = pl.BlockSpec((NH, ts, HD), lambda i: (0, i, 0))
    zg = pl.BlockSpec((ts, D), lambda i: (i, Z_GATE // D))
    za = pl.BlockSpec((ts, D), lambda i: (i, Z_BR // D))
    zb = pl.BlockSpec((ts, D), lambda i: (i, Z_BR // D + 1))
    row = pl.BlockSpec((ts, D), lambda i: (i, 0))
    gn = pl.BlockSpec((1, HD), lambda i: (0, 0))
    return o_spec, zg, za, zb, row, gn


def _merge_fwd(o, z, ob, gn, name, ts=256):
    S = ob.shape[0]

    def body(o_ref, zg_ref, za_ref, zb_ref, ob_ref, gn_ref, m_ref):
        for h in range(NH):
            sl = slice(h * HD, (h + 1) * HD)
            oh = o_ref[h]
            r = lax.rsqrt(jnp.mean(oh * oh, axis=1, keepdims=True) + EPS)
            gate = zg_ref[:, sl]
            oa = (oh * r * gn_ref[...]) * (gate * _sigmoid(gate))
            m = _sigmoid(za_ref[:, sl]) * oa + _sigmoid(zb_ref[:, sl]) * ob_ref[:, sl]
            m_ref[:, sl] = m.astype(m_ref.dtype)

    o_spec, zg, za, zb, row, gns = _merge_specs(ts)
    return _call(body, name, (S // ts,), [o_spec, zg, za, zb, row, gns], row,
                 jax.ShapeDtypeStruct((S, D), MXU))(o, z, z, z, ob, gn)


def _merge_bwd(dm, o, z, ob, gn, name, ts=256):
    S = ob.shape[0]

    def body(dm_ref, o_ref, zg_ref, za_ref, zb_ref, ob_ref, gn_ref,
             do_ref, dzg_ref, dob_ref, dza_ref, dzb_ref, dgn_ref):
        i = pl.program_id(0)
        gn_v = gn_ref[...]
        dgn = jnp.zeros((1, HD), F32)
        for h in range(NH):
            sl = slice(h * HD, (h + 1) * HD)
            dmh = dm_ref[:, sl]
            oh = o_ref[h]
            r = lax.rsqrt(jnp.mean(oh * oh, axis=1, keepdims=True) + EPS)
            ohr = oh * r
            on = ohr * gn_v
            gate = zg_ref[:, sl]
            sg = _sigmoid(gate)
            silu = gate * sg
            oa = on * silu
            ga = _sigmoid(za_ref[:, sl])
            gb = _sigmoid(zb_ref[:, sl])
            obh = ob_ref[:, sl]
            doa = dmh * ga
            dob_ref[:, sl] = dmh * gb
            dza_ref[:, sl] = (dmh * oa * ga * (1.0 - ga)).astype(dza_ref.dtype)
            dzb_ref[:, sl] = (dmh * obh * gb * (1.0 - gb)).astype(dzb_ref.dtype)
            don = doa * silu
            dzg_ref[:, sl] = (doa * on * (sg * (1.0 + gate * (1.0 - sg)))).astype(dzg_ref.dtype)
            dgn = dgn + _colsum(don * ohr)
            u = don * gn_v
            do_ref[h] = r * (u - ohr * jnp.mean(ohr * u, axis=1, keepdims=True))

        @pl.when(i == 0)
        def _():
            dgn_ref[...] = jnp.zeros_like(dgn_ref)

        dgn_ref[...] += dgn

    o_spec, zg, za, zb, row, gns = _merge_specs(ts)
    return _call(
        body, name, (S // ts,), [row, o_spec, zg, za, zb, row, gns],
        [o_spec, row, row, row, row, gns],
        [jax.ShapeDtypeStruct((NH, S, HD), F32), jax.ShapeDtypeStruct((S, D), MXU),
         jax.ShapeDtypeStruct((S, D), F32), jax.ShapeDtypeStruct((S, D), MXU),
         jax.ShapeDtypeStruct((S, D), MXU), jax.ShapeDtypeStruct((1, HD), F32)],
    )(dm, o, z, z, z, ob, gn)


GROWS = 256
GCH = GROWS // CH


def _gdn_prep_fwd(z, conv_w, alog_row, dtb_row, name, comm=None):
    S = z.shape[0]
    ts = GROWS
    scale = HD ** -0.5

    def body(z_ref, halo_ref, zab_ref, w_ref, al_ref, dt_ref, q_ref, k_ref, v_ref, gb_ref, buf):
        i = pl.program_id(0)
        buf[0:8, :] = jnp.where(i == 0, 0.0, halo_ref[...])
        buf[8:8 + ts, :] = z_ref[...]
        outs = (q_ref, k_ref, v_ref)
        for seg in range(3):
            cs = slice(seg * D, (seg + 1) * D)
            c = jnp.zeros((ts, D), F32)
            for j in range(4):
                c = c + w_ref[j:j + 1, cs] * buf[pl.ds(5 + j, ts), cs]
            s = c * _sigmoid(c)
            if seg == 2:
                outs[seg][...] = s
            else:
                mul = scale if seg == 0 else 1.0
                for h in range(NH):
                    sl = slice(h * HD, (h + 1) * HD)
                    sh = s[:, sl]
                    r = lax.rsqrt(_rowsum(sh * sh) + EPS)
                    outs[seg][:, sl] = sh * (r * mul)
        zab = zab_ref[...]
        lane = lax.broadcasted_iota(jnp.int32, zab.shape, 1)
        g = -jnp.exp(al_ref[...]) * _softplus(zab + dt_ref[...])
        ri = lax.broadcasted_iota(jnp.int32, (CH, CH), 0)
        ci = lax.broadcasted_iota(jnp.int32, (CH, CH), 1)
        incl = (ri >= ci).astype(F32)
        gcum = jnp.concatenate([_dot(incl, g[c * CH:(c + 1) * CH], HI) for c in range(ts // CH)], axis=0)
        gb_ref[...] = jnp.where(lane < NH, gcum, jnp.where(lane < 2 * NH, _sigmoid(zab), 0.0))

    row = pl.BlockSpec((ts, D), lambda i: (i, 0))
    vec = pl.BlockSpec((1, 128), lambda i: (0, 0))
    return _call(
        body, name, (S // ts,),
        [pl.BlockSpec((ts, 3 * D), lambda i: (i, 0)),
         pl.BlockSpec((8, 3 * D), lambda i: (jnp.maximum(i * (ts // 8) - 1, 0), 0)),
         pl.BlockSpec((ts, 128), lambda i: (i, Z_AB // 128)),
         _full((4, 3 * D)), vec, vec],
        [row, row, row, pl.BlockSpec((ts, 128), lambda i: (i, 0))],
        [jax.ShapeDtypeStruct((S, D), F32)] * 3 + [jax.ShapeDtypeStruct((S, 128), F32)],
        [pltpu.VMEM((ts + 8, 3 * D), F32)], comm=comm,
    )(z, z, z, conv_w, alog_row, dtb_row)


def _split(a):
    hi = a.astype(MXU)
    return hi, (a - hi.astype(F32)).astype(MXU)


def _dot3(a, b, dot=_dot):
    ah, al = _split(a)
    bh, bl = _split(b)
    return dot(ah, bh) + (dot(ah, bl) + dot(al, bh))


IROWS = 512
ICH = IROWS // CH


def _chunk_common(gbk, h, k):
    lane = lax.broadcasted_iota(jnp.int32, gbk.shape, 1)
    G = _rowsum(jnp.where(lane == h, gbk, 0.0))
    b_col = _rowsum(jnp.where(lane == h + NH, gbk, 0.0))
    ri = lax.broadcasted_iota(jnp.int32, (CH, CH), 0)
    ci = lax.broadcasted_iota(jnp.int32, (CH, CH), 1)
    incl = ri >= ci
    gc = jnp.broadcast_to(G, (CH, CH))
    decay = jnp.where(incl, jnp.exp(jnp.where(incl, gc - gc.T, 0.0)), 0.0)
    Gl = G[CH - 1:CH, :]
    kb = k * b_col
    return dict(b=b_col, ri=ri, ci=ci, incl=incl, strict=ri > ci, decay=decay, eG=jnp.exp(G),
                e2=jnp.exp(Gl - G), cd=jnp.exp(Gl), kb=kb, kk=_dot_nt(_mx(kb), _mx(k)))


def _gdn_intra_fwd(qn, kn, v, gb, name, comm=None):
    S = qn.shape[0]

    def body(q_ref, k_ref, v_ref, gb_ref, u_ref, w_ref, qd_ref, kd_ref, at_ref, t_ref, cd_ref):
        h = pl.program_id(1)
        rows = [slice(c * CH, (c + 1) * CH) for c in range(ICH)]
        ks = [k_ref[r, :] for r in rows]
        cms = [_chunk_common(gb_ref[r, :], h, k) for r, k in zip(rows, ks)]
        ps = [jnp.where(cm["strict"], cm["kk"] * cm["decay"], 0.0) for cm in cms]
        ts = [(cm["ri"] == cm["ci"]).astype(F32) - p for cm, p in zip(cms, ps)]
        for _ in range(5):
            ps = [_dot3(p, p) for p in ps]
            ts = [t + _dot3(t, p) for t, p in zip(ts, ps)]
        for c, (r, k, cm, t) in enumerate(zip(rows, ks, cms, ts)):
            rhs = jnp.concatenate([v_ref[r, :] * cm["b"], k * (cm["b"] * cm["eG"])], axis=1)
            sol = _dot3(t, rhs)
            u_ref[0, r, :] = sol[:, :HD]
            w_ref[0, r, :] = sol[:, HD:]
            t_ref[0, r, :] = t
        for c, (r, k, cm) in enumerate(zip(rows, ks, cms)):
            q = q_ref[r, :]
            qk = _dot_nt(_mx(q), _mx(k))
            at_ref[0, r, :] = jnp.where(cm["incl"], qk * cm["decay"], 0.0)
            qd_ref[0, r, :] = q * cm["eG"]
            kd_ref[0, r, :] = k * cm["e2"]
            cd_ref[0, c] = jnp.broadcast_to(cm["cd"], (8, 128))

    tok = pl.BlockSpec((IROWS, HD), lambda i, h: (i, h))
    hm = pl.BlockSpec((1, IROWS, HD), lambda i, h: (h, i, 0))
    hm64 = pl.BlockSpec((1, IROWS, CH), lambda i, h: (h, i, 0))
    big = jax.ShapeDtypeStruct((NH, S, HD), F32)
    sm = jax.ShapeDtypeStruct((NH, S, CH), F32)
    return _call(
        body, name, (S // IROWS, NH),
        [tok, tok, tok, pl.BlockSpec((IROWS, 128), lambda i, h: (i, 0))],
        [hm, hm, hm, hm, hm64, hm64, pl.BlockSpec((1, ICH, 8, 128), lambda i, h: (h, i, 0, 0))],
        [big, big, big, big, sm, sm, jax.ShapeDtypeStruct((NH, S // CH, 8, 128), F32)], comm=comm,
    )(qn, kn, v, gb)


def _scale_state(s, cd_tile):
    return (s.reshape(HD // 8, 8, HD) * cd_tile[None]).reshape(HD, HD)


def _gdn_scan_fwd(u, w, qd, kd, attn, cdt, name):
    S = u.shape[1]
    nblk = S // GROWS

    def body(u_ref, w_ref, qd_ref, kd_ref, at_ref, cd_ref, o_ref, vn_ref, st_ref, s_ref):
        i = pl.program_id(0)

        @pl.when(i == 0)
        def _():
            s_ref[...] = jnp.zeros_like(s_ref)

        def chunk(c, carry):
            r0 = pl.multiple_of(c * CH, CH)
            rows = pl.ds(r0, CH)
            for h in range(NH):
                sh = s_ref[h]
                st_ref[h, c] = sh
                sb = _mx(sh)
                vn = u_ref[h, rows, :] - _dot(_mx(w_ref[h, rows, :]), sb)
                vb = _mx(vn)
                vn_ref[h, rows, :] = vn
                o_ref[h, rows, :] = _dot(_mx(qd_ref[h, rows, :]), sb) + _dot(_mx(at_ref[h, rows, :]), vb)
                s_ref[h] = _scale_state(sh, cd_ref[h, c]) + _dot_tn(_mx(kd_ref[h, rows, :]), vb)
            return carry

        lax.fori_loop(0, GCH, chunk, 0)

    hm = pl.BlockSpec((NH, GROWS, HD), lambda i: (0, i, 0))
    hm64 = pl.BlockSpec((NH, GROWS, CH), lambda i: (0, i, 0))
    big = jax.ShapeDtypeStruct((NH, S, HD), F32)
    return _call(
        body, name, (nblk,),
        [hm, hm, hm, hm, hm64, pl.BlockSpec((NH, GCH, 8, 128), lambda i: (0, i, 0, 0))],
        [hm, hm, pl.BlockSpec((NH, GCH, HD, HD), lambda i: (0, i, 0, 0))],
        [big, big, jax.ShapeDtypeStruct((NH, S // CH, HD, HD), F32)],
        [pltpu.VMEM((NH, HD, HD), F32)],
    )(u, w, qd, kd, attn, cdt)


def _gdn_scan_bwd(do, w, qd, kd, attn, cdt, vn, st, name):
    S = do.shape[1]
    nblk = S // GROWS

    def body(do_ref, w_ref, qd_ref, kd_ref, at_ref, cd_ref, vn_ref, st_ref,
             dqd_ref, dkd_ref, dvn_ref, dw_ref, dat_ref, dcd_ref, ds_ref):
        i = pl.program_id(0)

        @pl.when(i == 0)
        def _():
            ds_ref[...] = jnp.zeros_like(ds_ref)

        def chunk(cc, carry):
            c = GCH - 1 - cc
            r0 = pl.multiple_of(c * CH, CH)
            rows = pl.ds(r0, CH)
            for h in range(NH):
                dsp = ds_ref[h]
                sh = st_ref[h, c]
                dsb, sb = _mx(dsp), _mx(sh)
                dob = _mx(do_ref[h, rows, :])
                vb = _mx(vn_ref[h, rows, :])
                dvn = _dot(_mx(kd_ref[h, rows, :]), dsb) + _dot_tn(_mx(at_ref[h, rows, :]), dob)
                dvb = _mx(dvn)
                dvn_ref[h, rows, :] = dvn
                dqd_ref[h, rows, :] = _dot_nt(dob, sb)
                dat_ref[h, rows, :] = _dot_nt(dob, vb)
                dkd_ref[h, rows, :] = _dot_nt(vb, dsb)
                dw_ref[h, rows, :] = -_dot_nt(dvb, sb)
                dcd = jnp.sum(_rowsum(dsp * sh), axis=0, keepdims=True)
                dcd_ref[h, c] = jnp.broadcast_to(dcd, (8, 128))
                ds_ref[h] = (_scale_state(dsp, cd_ref[h, c]) + _dot_tn(_mx(qd_ref[h, rows, :]), dob)
                             - _dot_tn(_mx(w_ref[h, rows, :]), dvb))
            return carry

        lax.fori_loop(0, GCH, chunk, 0)

    hm = pl.BlockSpec((NH, GROWS, HD), lambda i: (0, nblk - 1 - i, 0))
    hm64 = pl.BlockSpec((NH, GROWS, CH), lambda i: (0, nblk - 1 - i, 0))
    tile = pl.BlockSpec((NH, GCH, 8, 128), lambda i: (0, nblk - 1 - i, 0, 0))
    big = jax.ShapeDtypeStruct((NH, S, HD), F32)
    return _call(
        body, name, (nblk,),
        [hm, hm, hm, hm, hm64, tile, hm, pl.BlockSpec((NH, GCH, HD, HD), lambda i: (0, nblk - 1 - i, 0, 0))],
        [hm, hm, hm, hm, hm64, tile],
        [big, big, big, big, jax.ShapeDtypeStruct((NH, S, CH), F32),
         jax.ShapeDtypeStruct((NH, S // CH, 8, 128), F32)],
        [pltpu.VMEM((NH, HD, HD), F32)],
    )(do, w, qd, kd, attn, cdt, vn, st)


def _gdn_intra_bwd(qn, kn, v, gb, u, w, tmat, dqd, dkd, du, dw, dattn, dcdt, name):
    S = qn.shape[0]

    def body(q_ref, k_ref, v_ref, gb_ref, u_ref, w_ref, t_ref, dqd_ref, dkd_ref, du_ref, dw_ref,
             dat_ref, dcd_ref, dq_ref, dk_ref, dv_ref, dgb_ref):
        h = pl.program_id(1)
        rows = [slice(c * CH, (c + 1) * CH) for c in range(ICH)]
        ks = [k_ref[r, :] for r in rows]
        cms = [_chunk_common(gb_ref[r, :], h, k) for r, k in zip(rows, ks)]
        sols = [jnp.concatenate([u_ref[0, r, :], w_ref[0, r, :]], axis=1) for r in rows]
        drhss = [_dot3(t_ref[0, r, :], jnp.concatenate([du_ref[0, r, :], dw_ref[0, r, :]], axis=1), _dot_tn)
                 for r in rows]
        das = [-_dot3(drhs, sol, _dot_nt) for drhs, sol in zip(drhss, sols)]
        for c, (r, k, cm, drhs, da) in enumerate(zip(rows, ks, cms, drhss, das)):
            q, vv = q_ref[r, :], v_ref[r, :]
            decay, eG, e2, b = cm["decay"], cm["eG"], cm["e2"], cm["b"]
            dru, drw = drhs[:, :HD], drhs[:, HD:]
            dv_ref[r, :] = dru * b
            s_w = _rowsum(drw * k)
            dbeta = _rowsum(dru * vv) + s_w * eG
            deg = s_w * b
            dk = drw * (b * eG)
            dkk = jnp.where(cm["strict"], da * decay, 0.0)
            ddec = jnp.where(cm["strict"], da * cm["kk"], 0.0)
            dkkb = _mx(dkk)
            dkb = _dot(dkkb, _mx(k))
            dk = dk + _dot_tn(dkkb, _mx(cm["kb"])) + dkb * b
            dbeta = dbeta + _rowsum(dkb * k)
            dat = jnp.where(cm["incl"], dat_ref[0, r, :], 0.0)
            qk = _dot_nt(_mx(q), _mx(k))
            dqk = _mx(dat * decay)
            ddec = ddec + dat * qk
            dqd = dqd_ref[0, r, :]
            dkd = dkd_ref[0, r, :]
            dq_ref[r, :] = _dot(dqk, _mx(k)) + dqd * eG
            dk_ref[r, :] = dk + _dot_tn(dqk, _mx(q)) + dkd * e2
            deg = deg + _rowsum(dqd * q)
            t2 = _rowsum(dkd * k) * e2
            dgl = jnp.sum(t2, axis=0, keepdims=True) + dcd_ref[0, c][0:1, 0:1] * cm["cd"]
            dd = ddec * decay
            dG = deg * eG - t2 + _rowsum(dd) - _rowsum(dd.T)
            rowi = lax.broadcasted_iota(jnp.int32, (CH, 1), 0)
            dG = dG + jnp.where(rowi == CH - 1, dgl, 0.0)
            lane = lax.broadcasted_iota(jnp.int32, (CH, 128), 1)
            dgb_ref[0, r, :] = jnp.where(lane == h, dG, 0.0) + jnp.where(lane == h + NH, dbeta, 0.0)

    tok = pl.BlockSpec((IROWS, HD), lambda i, h: (i, h))
    hm = pl.BlockSpec((1, IROWS, HD), lambda i, h: (h, i, 0))
    hm64 = pl.BlockSpec((1, IROWS, CH), lambda i, h: (h, i, 0))
    tile = pl.BlockSpec((1, ICH, 8, 128), lambda i, h: (h, i, 0, 0))
    tokout = jax.ShapeDtypeStruct((S, D), F32)
    return _call(
        body, name, (S // IROWS, NH),
        [tok, tok, tok, pl.BlockSpec((IROWS, 128), lambda i, h: (i, 0)), hm, hm, hm64,
         hm, hm, hm, hm, hm64, tile],
        [tok, tok, tok, pl.BlockSpec((1, IROWS, 128), lambda i, h: (h, i, 0))],
        [tokout, tokout, tokout, jax.ShapeDtypeStruct((NH, S, 128), F32)],
    )(qn, kn, v, gb, u, w, tmat, dqd, dkd, du, dw, dattn, dcdt)


def _gdn_prep_bwd(z, dqn, dkn, dv, dgb, conv_w, alog_row, dtb_row, name):
    S = z.shape[0]
    ts = GROWS
    nblk = S // ts
    scale = HD ** -0.5
    tb = ts // 8

    def body(z_ref, hp_ref, hn_ref, zab_ref, dq_ref, dqn_ref, dk_ref, dkn_ref, dv_ref, dvn_ref,
             dgb_ref, w_ref, al_ref, dt_ref, dz_ref, dzab_ref, dcw_ref, dvec_ref, buf, dybuf, dcbuf):
        i = pl.program_id(0)
        last = i == nblk - 1

        @pl.when(i == 0)
        def _():
            dcw_ref[...] = jnp.zeros_like(dcw_ref)
            dvec_ref[...] = jnp.zeros_like(dvec_ref)

        buf[0:8, :] = jnp.where(i == 0, 0.0, hp_ref[...])
        buf[8:8 + ts, :] = z_ref[...]
        buf[8 + ts:16 + ts, :] = hn_ref[...]
        rowi = lax.broadcasted_iota(jnp.int32, (ts + 8, 1), 0)
        live = jnp.logical_or(rowi < ts, jnp.logical_not(last))
        dys = ((dq_ref, dqn_ref), (dk_ref, dkn_ref), (dv_ref, dvn_ref))
        for seg in range(3):
            cs = slice(seg * D, (seg + 1) * D)
            dybuf[0:ts, :] = dys[seg][0][...]
            dybuf[ts:ts + 8, :] = dys[seg][1][...]
            c = jnp.zeros((ts + 8, D), F32)
            for j in range(4):
                c = c + w_ref[j:j + 1, cs] * buf[pl.ds(5 + j, ts + 8), cs]
            sg = _sigmoid(c)
            s = c * sg
            dsilu = sg * (1.0 + c * (1.0 - sg))
            if seg == 2:
                dcbuf[...] = jnp.where(live, dybuf[...] * dsilu, 0.0)
            else:
                mul = scale if seg == 0 else 1.0
                for h in range(NH):
                    sl = slice(h * HD, (h + 1) * HD)
                    sh = s[:, sl]
                    dy = dybuf[:, sl]
                    r = lax.rsqrt(_rowsum(sh * sh) + EPS)
                    shr = sh * r
                    ds = (mul * r) * (dy - shr * _rowsum(shr * dy))
                    dcbuf[:, sl] = jnp.where(live, ds * dsilu[:, sl], 0.0)
            dx = jnp.zeros((ts, D), F32)
            for j in range(4):
                dcw_ref[j:j + 1, cs] += _colsum(dcbuf[0:ts, :] * buf[pl.ds(5 + j, ts), cs])
                dx = dx + w_ref[j:j + 1, cs] * dcbuf[pl.ds(3 - j, ts), :]
            dz_ref[:, cs] = dx.astype(dz_ref.dtype)
        dgbs = dgb_ref[0]
        for h in range(1, NH):
            dgbs = dgbs + dgb_ref[h]
        ri = lax.broadcasted_iota(jnp.int32, (CH, CH), 0)
        ci = lax.broadcasted_iota(jnp.int32, (CH, CH), 1)
        rev = (ci >= ri).astype(F32)
        dgrev = jnp.concatenate([_dot(rev, dgbs[c * CH:(c + 1) * CH], HI) for c in range(ts // CH)], axis=0)
        lane0 = lax.broadcasted_iota(jnp.int32, dgbs.shape, 1)
        dgbs = jnp.where(lane0 < NH, dgrev, dgbs)
        zab = zab_ref[...]
        lane = lax.broadcasted_iota(jnp.int32, zab.shape, 1)
        xx = zab + dt_ref[...]
        ea = jnp.exp(al_ref[...])
        g = -ea * _softplus(xx)
        da = dgbs * (-ea) * _sigmoid(xx)
        beta = _sigmoid(zab)
        db = dgbs * beta * (1.0 - beta)
        is_a = lane < NH
        dzab = jnp.where(is_a, da, jnp.where(lane < 2 * NH, db, 0.0))
        dzab_ref[:, 0:128] = dzab.astype(dzab_ref.dtype)
        dzab_ref[:, 128:512] = jnp.zeros((ts, 384), dzab_ref.dtype)
        dvec_ref[0:1, :] += _colsum(jnp.where(is_a, dgbs * g, 0.0))
        dvec_ref[1:2, :] += _colsum(jnp.where(is_a, da, 0.0))

    z3 = pl.BlockSpec((ts, 3 * D), lambda i: (i, 0))
    row = pl.BlockSpec((ts, D), lambda i: (i, 0))
    nxt = pl.BlockSpec((8, D), lambda i: (jnp.minimum((i + 1) * tb, S // 8 - 1), 0))
    vec = pl.BlockSpec((1, 128), lambda i: (0, 0))
    return _call(
        body, name, (nblk,),
        [z3,
         pl.BlockSpec((8, 3 * D), lambda i: (jnp.maximum(i * tb - 1, 0), 0)),
         pl.BlockSpec((8, 3 * D), lambda i: (jnp.minimum((i + 1) * tb, S // 8 - 1), 0)),
         pl.BlockSpec((ts, 128), lambda i: (i, Z_AB // 128)),
         row, nxt, row, nxt, row, nxt,
         pl.BlockSpec((NH, ts, 128), lambda i: (0, i, 0)),
         _full((4, 3 * D)), vec, vec],
        [z3, pl.BlockSpec((ts, 512), lambda i: (i, 0)), _full((8, 3 * D)), _full((8, 128))],
        [jax.ShapeDtypeStruct((S, 3 * D), MXU), jax.ShapeDtypeStruct((S, 512), MXU),
         jax.ShapeDtypeStruct((8, 3 * D), F32), jax.ShapeDtypeStruct((8, 128), F32)],
        [pltpu.VMEM((ts + 16, 3 * D), F32), pltpu.VMEM((ts + 8, D), F32), pltpu.VMEM((ts + 8, D), F32)],
    )(z, z, z, z, dqn, dqn, dkn, dkn, dv, dv, dgb, conv_w, alog_row, dtb_row)


def _bias_index():
    u = np.arange(FW)[None, :]
    s = np.arange(3)[:, None]
    return np.clip(KWIN - 1 - u - QB * s, -256, 256) + 256


def _bias_vec(rel_bias_pad, onehot, name):
    def body(rb_ref, e_ref, o_ref):
        o_ref[:, 0, :] = _dot_nt(rb_ref[...], e_ref[0], HI)

    return _call(body, name, (3,),
                 [_full((NH, 640)), pl.BlockSpec((1, FW, 640), lambda s: (s, 0, 0))],
                 pl.BlockSpec((NH, 1, FW), lambda s: (s, 0, 0)),
                 jax.ShapeDtypeStruct((3 * NH, 1, FW), F32))(rel_bias_pad, onehot)


def _att_window(i):
    return pl.multiple_of(jnp.maximum(i * QB - PAST * CH, 0), QB)


def _bias_mask(fvec, name):
    def body(f_ref, o_ref):
        i = 2 - pl.program_id(0) // NH
        ws = jnp.maximum(i * QB - PAST * CH, 0)
        fb = jnp.broadcast_to(f_ref[0], (QB, FW))
        bias = pltpu.roll(fb, FW - 255, 1, stride=1, stride_axis=0)[:, :KWIN]
        qc = (i * QB + lax.broadcasted_iota(jnp.int32, (QB, KWIN), 0)) // CH
        kc = (ws + lax.broadcasted_iota(jnp.int32, (QB, KWIN), 1)) // CH
        o_ref[0] = jnp.where((kc <= qc) & (kc >= qc - PAST), bias, MASKED)

    return _call(body, name, (3 * NH,), [pl.BlockSpec((1, 1, FW), lambda j: (j, 0, 0))],
                 pl.BlockSpec((1, QB, KWIN), lambda j: (j, 0, 0)),
                 jax.ShapeDtypeStruct((3 * NH, QB, KWIN), F32))(fvec)


def _att_probs(q_ref, k_ref, bm_ref, i):
    ws = _att_window(i)
    q = _mx(q_ref[...] * (HD ** -0.5))
    kw = _mx(k_ref[pl.ds(ws, KWIN), :])
    s = _dot_nt(q, kw) + bm_ref[0]
    p = jnp.exp(s - jnp.max(s, axis=1, keepdims=True))
    p = p * (1.0 / _rowsum(p))
    return q, kw, ws, p


def _att_specs(S):
    c0 = Z_ATT // HD
    q = pl.BlockSpec((QB, HD), lambda h, i: (i, c0 + h))
    k = pl.BlockSpec((S, HD), lambda h, i: (0, c0 + NH + h))
    v = pl.BlockSpec((S, HD), lambda h, i: (0, c0 + 2 * NH + h))
    bm = pl.BlockSpec((1, QB, KWIN), lambda h, i: (jnp.maximum(2 - i, 0) * NH + h, 0, 0))
    tok = pl.BlockSpec((QB, HD), lambda h, i: (i, h))
    return q, k, v, bm, tok


def _att_fwd(z, bmask, name, comm=None):
    S = z.shape[0]

    def body(q_ref, k_ref, v_ref, bm_ref, o_ref):
        i = pl.program_id(1)
        _, _, ws, p = _att_probs(q_ref, k_ref, bm_ref, i)
        o_ref[...] = _dot(_mx(p), _mx(v_ref[pl.ds(ws, KWIN), :]))

    q, k, v, bm, tok = _att_specs(S)
    return _call(body, name, (NH, S // QB), [q, k, v, bm], tok,
                 jax.ShapeDtypeStruct((S, D), F32), comm=comm)(z, z, z, bmask)


def _att_bwd(z, bmask, ob, dob, name):
    S = z.shape[0]

    def body(q_ref, k_ref, v_ref, bm_ref, o_ref, do_ref, dq_ref, dk_ref, dv_ref, db_ref):
        i = pl.program_id(1)
        q, kw, ws, p = _att_probs(q_ref, k_ref, bm_ref, i)
        do = do_ref[...]
        dob16 = _mx(do)
        dp = _dot_nt(dob16, _mx(v_ref[pl.ds(ws, KWIN), :]))
        ds = p * (dp - _rowsum(do * o_ref[...]))
        dsb = _mx(ds)
        dq_ref[...] = (_dot(dsb, kw) * (HD ** -0.5)).astype(dq_ref.dtype)

        @pl.when(i == 0)
        def _():
            dk_ref[...] = jnp.zeros_like(dk_ref)
            dv_ref[...] = jnp.zeros_like(dv_ref)

        dk_ref[pl.ds(ws, KWIN), :] += _dot_tn(dsb, q)
        dv_ref[pl.ds(ws, KWIN), :] += _dot_tn(_mx(p), dob16)

        @pl.when(i <= 2)
        def _():
            db_ref[0] = ds

        @pl.when(i > 2)
        def _():
            db_ref[0] += ds

    q, k, v, bm, tok = _att_specs(S)
    acc = pl.BlockSpec((S, HD), lambda h, i: (0, h))
    return _call(
        body, name, (NH, S // QB), [q, k, v, bm, tok, tok], [tok, acc, acc, bm],
        [jax.ShapeDtypeStruct((S, D), MXU), jax.ShapeDtypeStruct((S, D), F32),
         jax.ShapeDtypeStruct((S, D), F32), jax.ShapeDtypeStruct((3 * NH, QB, KWIN), F32)],
    )(z, z, z, bmask, ob, dob)


def _bias_fold(dbias, onehot, name):
    def body(db_ref, e_ref, o_ref):
        j = pl.program_id(0)
        h = j % NH
        x = jnp.concatenate([db_ref[0], jnp.zeros((QB, FW - KWIN), F32)], axis=1)
        half = QB // 2
        while half >= 8:
            x = x[:half] + pltpu.roll(x[half:2 * half], FW - half, 1)
            half //= 2
        df = jnp.zeros((1, FW), F32)
        for r in range(8):
            df = df + pltpu.roll(x[r:r + 1], 255 - r, 1)
        contrib = _dot(df, e_ref[0], HI)
        rowh = lax.broadcasted_iota(jnp.int32, (NH, 640), 0)

        @pl.when(j == 0)
        def _():
            o_ref[...] = jnp.zeros_like(o_ref)

        o_ref[...] += jnp.where(rowh == h, contrib, 0.0)

    return _call(
        body, name, (3 * NH,),
        [pl.BlockSpec((1, QB, KWIN), lambda j: (j, 0, 0)),
         pl.BlockSpec((1, FW, 640), lambda j: (j // NH, 0, 0))],
        _full((NH, 640)), jax.ShapeDtypeStruct((NH, 640), F32),
    )(dbias, onehot)


ADA_SHARD = 6 * D // NDEV


def _ada_mod(c_all, w_ada, b_shard, name):
    def body(c_ref, w_ref, b_ref, o_ref):
        cv = c_ref[...]
        ca = cv * _sigmoid(cv)
        o_ref[0] = _dot(_mx(ca), _mx(w_ref[0])) + b_ref[0]

    return _call(
        body, name, (DEPTH,),
        [_full((NDEV, D)), pl.BlockSpec((1, D, ADA_SHARD), lambda l: (l, 0, 0)),
         pl.BlockSpec((1, 1, ADA_SHARD), lambda l: (l, 0, 0))],
        pl.BlockSpec((1, NDEV, ADA_SHARD), lambda l: (l, 0, 0)),
        jax.ShapeDtypeStruct((DEPTH, NDEV, ADA_SHARD), F32),
    )(c_all, w_ada, b_shard.reshape(DEPTH, 1, ADA_SHARD))


def _adam(g, w, m, v):
    m = ADAM_B1 * m + (1.0 - ADAM_B1) * g
    v = ADAM_B2 * v + (1.0 - ADAM_B2) * jnp.square(g)
    m_hat = m / (1.0 - ADAM_B1 ** ADAM_STEP)
    v_hat = v / (1.0 - ADAM_B2 ** ADAM_STEP)
    delta = -ADAM_LR * (m_hat / (jnp.sqrt(v_hat) + ADAM_EPS) + ADAM_WD * w)
    return delta, m, v


def _wada_adamw(c_all_t, dmod, w, m, v, name):
    def body(c_ref, d_ref, w_ref, m_ref, v_ref, g_ref, dl_ref, mo_ref, vo_ref):
        cv = c_ref[...]
        ca = cv * _sigmoid(cv)
        g = _dot(ca, d_ref[0], HI)
        g_ref[0] = g
        dl_ref[0], mo_ref[0], vo_ref[0] = _adam(g, w_ref[0], m_ref[0], v_ref[0])

    blk = pl.BlockSpec((1, D, ADA_SHARD), lambda l: (l, 0, 0))
    shp = jax.ShapeDtypeStruct((DEPTH, D, ADA_SHARD), F32)
    return _call(
        body, name, (DEPTH,),
        [_full((D, NDEV)), pl.BlockSpec((1, NDEV, ADA_SHARD), lambda l: (l, 0, 0)), blk, blk, blk],
        [blk] * 4, [shp] * 4,
    )(c_all_t, dmod, w, m, v)


def _adamw_reduce(parts, w, m, v, name, tr):
    P, R, C = parts.shape

    def body(p_ref, w_ref, m_ref, v_ref, g_ref, dl_ref, mo_ref, vo_ref):
        g = p_ref[0].astype(F32)
        for k in range(1, P):
            g = g + p_ref[k].astype(F32)
        g_ref[...] = g
        dl_ref[...], mo_ref[...], vo_ref[...] = _adam(g, w_ref[...], m_ref[...], v_ref[...])

    blk = pl.BlockSpec((tr, C), lambda i: (i, 0))
    shp = jax.ShapeDtypeStruct((R, C), F32)
    return _call(body, name, (R // tr,), [pl.BlockSpec((P, tr, C), lambda i: (0, i, 0)), blk, blk, blk],
                 [blk] * 4, [shp] * 4)(parts, w, m, v)


def _sum_parts(parts, name):
    P, R, C = parts.shape

    def body(p_ref, o_ref):
        g = p_ref[0]
        for k in range(1, P):
            g = g + p_ref[k]
        o_ref[...] = g

    return _call(body, name, (1,), [_full((P, R, C))], _full((R, C)),
                 jax.ShapeDtypeStruct((R, C), F32))(parts)


def _pack_rows(vecs, width=1024):
    flat = jnp.concatenate([a.reshape(-1) for a in vecs])
    n = flat.shape[0]
    rows = -(-n // width)
    rows = -(-rows // 8) * 8
    return jnp.pad(flat, (0, rows * width - n)).reshape(rows, width)


def _unpack_rows(packed, shapes):
    flat = packed.reshape(-1)
    out, off = [], 0
    for s in shapes:
        n = int(np.prod(s)) if len(s) else 1
        out.append(flat[off:off + n].reshape(s))
        off += n
    return out


BIG = ("w_in", "w_out", "w_ff_in", "w_ff_out")


def _z_weights(g_in):
    w = jnp.transpose(g_in, (1, 0, 2)).reshape(D, IN_W)
    return jnp.concatenate([w[:, :Z_ATT], w[:, Z_ATT + 2 * NH:], w[:, Z_ATT:Z_ATT + 2 * NH],
                            jnp.zeros((D, ZW - IN_W), MXU)], axis=-1)


def _cols_to_owners(a):
    return jnp.transpose(a.astype(MXU).reshape(a.shape[0], NDEV, -1), (1, 0, 2))


def _rows_to_owners(a):
    return a.astype(MXU).reshape(NDEV, -1, a.shape[1])


def _forward_layer(x, mod_l, p, shard, next_w_in):
    sh1, sc1, gt1, sh2, sc2, gt2 = [mod_l[k][None] for k in range(6)]
    P = functools.partial
    gather = lambda a: None if a is None else ("gather", a)
    h = _modnorm_fwd(x, p["norm_mix"], sc1, sh1, "norm_mix_fwd")
    z, g_in = _hosted(P(_mm_nn, h, p["wz"], "in_proj", tm=2048, tn=512), gather(next_w_in))
    (qn, kn, v, gb), g_w1 = _hosted(P(_gdn_prep_fwd, z, p["conv_w"], p["alog"], p["dtb"], "gdn_prep_fwd"),
                                    gather(shard["w_ff_in"]))
    (u, w, qd, kd, attn, tmat, cdt), g_w2 = _hosted(P(_gdn_intra_fwd, qn, kn, v, gb, "gdn_intra_fwd"),
                                                    gather(shard["w_ff_out"]))
    o, vn, st = _gdn_scan_fwd(u, w, qd, kd, attn, cdt, "gdn_scan_fwd")
    ob, g_out = _hosted(P(_att_fwd, z, p["bmask"], "att_fwd"), gather(shard["w_out"]))
    wout = g_out.reshape(D, D)
    w1 = jnp.transpose(g_w1, (1, 0, 2)).reshape(D, DFF)
    w2 = g_w2.reshape(DFF, D)
    m = _merge_fwd(o, z, ob, p["gdn_norm"], "merge_fwd")
    x1 = _mm_nn(m, wout, "out_proj", mode="resid", res=x, gate=gt1)
    h2 = _modnorm_fwd(x1, p["norm_mlp"], sc2, sh2, "norm_mlp_fwd")
    a, r = _mm_nn(h2, w1, "ff_in", mode="relu2")
    x2 = _mm_nn(r, w2, "ff_out", mode="resid", res=x1, gate=gt2)
    saved = dict(x=x, h=h, z=z, qn=qn, kn=kn, v=v, gb=gb, u=u, w=w, qd=qd, kd=kd, attn=attn,
                 tmat=tmat, cdt=cdt, o=o, vn=vn, st=st, ob=ob, m=m, x1=x1, h2=h2, a=a, r=r,
                 wout=wout, w1=w1, w2=w2)
    return x2, saved, g_in


def _backward_layer(dx2, mod_l, p, s, onehot):
    sh1, sc1, gt1, sh2, sc2, gt2 = [mod_l[k][None] for k in range(6)]
    P = functools.partial
    dw2, dgt2 = _mm_tn(s["r"], dx2, "ff_out_dw", gate=gt2, w=s["w2"])
    da, r_w2 = _hosted(P(_mm_nt, dx2, s["w2"], "ff_out_dx", gate=gt2, drelu=s["a"]), ("a2a", _rows_to_owners(dw2)))
    dw1 = _mm_tn(s["h2"], da, "ff_in_dw")
    dh2, r_w1 = _hosted(P(_mm_nt, da, s["w1"], "ff_in_dx"), ("a2a", _cols_to_owners(dw1)))
    dx1, dsc2, dsh2, dnmlp = _modnorm_bwd(dh2, s["x1"], p["norm_mlp"], sc2, sh2, dx2, "norm_mlp_bwd")
    dwout, dgt1 = _mm_tn(s["m"], dx1, "out_proj_dw", gate=gt1, w=s["wout"])
    dm, r_out = _hosted(P(_mm_nt, dx1, s["wout"], "out_proj_dx", gate=gt1), ("a2a", _rows_to_owners(dwout)))
    do, dzg, dob, dza, dzb, dgn = _merge_bwd(dm, s["o"], s["z"], s["ob"], p["gdn_norm"], "merge_bwd")
    dq_att, dk_att, dv_att, dbias = _att_bwd(s["z"], p["bmask"], s["ob"], dob, "att_bwd")
    drb = _bias_fold(dbias, onehot, "rel_bias_fold")[:, :513]
    dqd, dkd, dvn, dw, dattn, dcdt = _gdn_scan_bwd(do, s["w"], s["qd"], s["kd"], s["attn"], s["cdt"],
                                                   s["vn"], s["st"], "gdn_scan_bwd")
    dqn, dkn, dv, dgb = _gdn_intra_bwd(s["qn"], s["kn"], s["v"], s["gb"], s["u"], s["w"], s["tmat"],
                                       dqd, dkd, dvn, dw, dattn, dcdt, "gdn_intra_bwd")
    dzq, dzab, dcw, dvec = _gdn_prep_bwd(s["z"], dqn, dkn, dv, dgb, p["conv_w"], p["alog"], p["dtb"],
                                         "gdn_prep_bwd")
    dz = (dzq, dzg, dq_att, dk_att, dv_att, dza, dzb, dzab)
    dwz = _in_proj_dw(s["h"], dz, "in_proj_dw")
    dw_in = jnp.concatenate([dwz[:, :Z_ATT], dwz[:, Z_AB:Z_AB + 2 * NH], dwz[:, Z_ATT:Z_AB]], axis=1)
    dh, r_in = _hosted(P(_in_proj_dx, dz, p["wz"], "in_proj_dx"), ("a2a", _cols_to_owners(dw_in)))
    dx, dsc1, dsh1, dnmix = _modnorm_bwd(dh, s["x"], p["norm_mix"], sc1, sh1, dx1, "norm_mix_bwd")
    grads = dict(norm_mix=dnmix[0], norm_mlp=dnmlp[0], conv_w=dcw[:4], a_log=dvec[0, :NH], dt_bias=dvec[1, :NH],
                 gdn_norm=dgn[0], rel_bias=drb, mod=jnp.concatenate([dsh1, dsc1, dgt1, dsh2, dsc2, dgt2], axis=1)[0])
    return dx, grads, dict(w_in=r_in, w_out=r_out, w_ff_in=r_w1, w_ff_out=r_w2)


def _bias_onehot():
    return (jnp.asarray(_bias_index())[:, :, None] == jnp.arange(640)[None, None, :]).astype(F32)


def _layer_params(l, conv_full, norm_mix, norm_mlp, a_log, dt_bias, gdn_norm, rel_bias, onehot):
    pad = lambda a: jnp.pad(a, (0, 128 - NH))[None]
    fvec = _bias_vec(jnp.pad(rel_bias[l], ((0, 0), (0, 640 - rel_bias.shape[2]))), onehot, "rel_bias_vec")
    return dict(conv_w=conv_full[l], norm_mix=norm_mix[l][None], norm_mlp=norm_mlp[l][None], alog=pad(a_log[l]),
                dtb=pad(dt_bias[l]), gdn_norm=gdn_norm[l][None], bmask=_bias_mask(fvec, "rel_bias_mask"))


def _local_step(x, target, mod, small, shards, final_norm, onehot):
    L = len(small)
    g_in = _all_gather(shards[0]["w_in"], "gather_w_in")
    saved, params = [], []
    for l in range(L):
        params.append({**small[l], "wz": _z_weights(g_in)})
        x, sv, g_in = _forward_layer(x, mod[l].reshape(6, D), params[l], shards[l],
                                     shards[l + 1]["w_in"] if l + 1 < L else None)
        saved.append(sv)
    loss, dx, dfn = _loss_head(x, target, final_norm[None], "loss_head")
    grads, recv = [None] * L, [None] * L
    for l in reversed(range(L)):
        dx, grads[l], recv[l] = _backward_layer(dx, mod[l].reshape(6, D), params[l], saved[l], onehot)
    return loss, dx, grads, dfn[0], recv


SMALL = ("b_ada", "norm_mix", "norm_mlp", "a_log", "dt_bias", "gdn_norm", "rel_bias", "final_norm")


def kernel(x, c, w_ada, b_ada, norm_mix, norm_mlp, w_in, conv_w, a_log, dt_bias, gdn_norm, rel_bias, w_out, w_ff_in, w_ff_out, final_norm, loss_target, m_w_ada, m_b_ada, m_norm_mix, m_norm_mlp, m_w_in, m_conv_w, m_a_log, m_dt_bias, m_gdn_norm, m_rel_bias, m_w_out, m_w_ff_in, m_w_ff_out, m_final_norm, v_w_ada, v_b_ada, v_norm_mix, v_norm_mlp, v_w_in, v_conv_w, v_a_log, v_dt_bias, v_gdn_norm, v_rel_bias, v_w_out, v_w_ff_in, v_w_ff_out, v_final_norm):
    W = dict(w_ada=w_ada, b_ada=b_ada, norm_mix=norm_mix, norm_mlp=norm_mlp, w_in=w_in, conv_w=conv_w,
             a_log=a_log, dt_bias=dt_bias, gdn_norm=gdn_norm, rel_bias=rel_bias, w_out=w_out,
             w_ff_in=w_ff_in, w_ff_out=w_ff_out, final_norm=final_norm)
    Mo = dict(w_ada=m_w_ada, b_ada=m_b_ada, norm_mix=m_norm_mix, norm_mlp=m_norm_mlp, w_in=m_w_in,
              conv_w=m_conv_w, a_log=m_a_log, dt_bias=m_dt_bias, gdn_norm=m_gdn_norm, rel_bias=m_rel_bias,
              w_out=m_w_out, w_ff_in=m_w_ff_in, w_ff_out=m_w_ff_out, final_norm=m_final_norm)
    Vo = dict(w_ada=v_w_ada, b_ada=v_b_ada, norm_mix=v_norm_mix, norm_mlp=v_norm_mlp, w_in=v_w_in,
              conv_w=v_conv_w, a_log=v_a_log, dt_bias=v_dt_bias, gdn_norm=v_gdn_norm, rel_bias=v_rel_bias,
              w_out=v_w_out, w_ff_in=v_w_ff_in, w_ff_out=v_w_ff_out, final_norm=v_final_norm)
    L = w_in.shape[0]
    me = _flat(_mesh_pos())
    cshard = conv_w.shape[2]

    small_in = _all_gather(_pack_rows([c, conv_w]), "gather_c_conv")
    c_all = small_in[:, 0, :]
    conv_full = small_in.reshape(NDEV, -1)[:, D:D + L * 4 * cshard].reshape(NDEV, L, 4, cshard)
    conv_full = jnp.transpose(conv_full, (1, 2, 0, 3)).reshape(L, 4, NDEV * cshard)

    b_shard = lax.dynamic_slice_in_dim(b_ada, me * ADA_SHARD, ADA_SHARD, axis=1)
    mod_all = _all_gather(_ada_mod(c_all, w_ada, b_shard, "ada_mod"), "gather_mod")
    mod = lax.dynamic_index_in_dim(mod_all, me, axis=2, keepdims=False)
    mod = jnp.transpose(mod, (1, 0, 2)).reshape(L, 6 * D)

    onehot = _bias_onehot()
    small = [_layer_params(l, conv_full, norm_mix, norm_mlp, a_log, dt_bias, gdn_norm, rel_bias, onehot)
             for l in range(L)]
    shards = [{n: W[n][l].astype(MXU) for n in BIG} for l in range(L)]
    loss, dx, grads, dfn, recv = _local_step(x[0], loss_target[0], mod, small, shards, final_norm, onehot)

    def stack(name):
        return jnp.stack([g[name] for g in grads])

    small_names = ("mod", "norm_mix", "norm_mlp", "a_log", "dt_bias", "gdn_norm", "rel_bias")
    small_parts = [stack(n) for n in small_names] + [dfn, stack("conv_w"), loss[0, 0:1]]
    small_shapes = [a.shape for a in small_parts]
    gathered = _all_gather(_pack_rows(small_parts), "gather_small_grads")
    total = _unpack_rows(_sum_parts(gathered, "sum_small_grads"), small_shapes)
    tot = dict(zip(small_names + ("final_norm", "conv_w", "loss"), total))
    tot["b_ada"] = tot.pop("mod")
    tot["conv_w"] = lax.dynamic_slice_in_dim(tot["conv_w"], me * cshard, cshard, axis=2)

    out_g, out_d, out_m, out_v = {}, {}, {}, {}
    names = SMALL + ("conv_w",)
    shapes = [W[n].shape for n in names]
    packed = [_pack_rows([src[n] for n in names])[None] if src is tot else _pack_rows([src[n] for n in names])
              for src in (tot, W, Mo, Vo)]
    res = _adamw_reduce(*packed, "adamw_small", tr=8)
    for dst, arr in zip((out_g, out_d, out_m, out_v), res):
        dst.update(zip(names, _unpack_rows(arr, shapes)))

    dmod_all = gathered.reshape(NDEV, -1)[:, :L * 6 * D].reshape(NDEV, L, 6 * D)
    dmod_mine = jnp.transpose(lax.dynamic_slice_in_dim(dmod_all, me * ADA_SHARD, ADA_SHARD, axis=2), (1, 0, 2))
    res = _wada_adamw(jnp.transpose(c_all), dmod_mine, w_ada, m_w_ada, v_w_ada, "adamw_w_ada")
    for dst, arr in zip((out_g, out_d, out_m, out_v), res):
        dst["w_ada"] = arr

    for name, tr in (("w_in", 256), ("w_out", 128), ("w_ff_in", 256), ("w_ff_out", 256)):
        sh = W[name].shape
        rows = int(np.prod(sh[:-1]))
        flat = lambda a: a.reshape(rows, sh[-1])
        parts = jnp.stack([recv[l][name] for l in range(L)], axis=1).reshape(NDEV, rows, sh[-1])
        res = _adamw_reduce(parts, flat(W[name]), flat(Mo[name]), flat(Vo[name]), "adamw_" + name, tr=tr)
        for dst, arr in zip((out_g, out_d, out_m, out_v), res):
            dst[name] = arr.reshape(sh)

    order = ("w_ada", "b_ada", "norm_mix", "norm_mlp", "w_in", "conv_w", "a_log", "dt_bias", "gdn_norm",
             "rel_bias", "w_out", "w_ff_in", "w_ff_out", "final_norm")
    return (tot["loss"].reshape(()), dx[None], *[out_g[n] for n in order], *[out_d[n] for n in order],
            *[out_m[n] for n in order], *[out_v[n] for n in order])
```

```python
import functools
import math

import numpy as np
import jax
import jax.numpy as jnp
from jax import lax
from jax.experimental import pallas as pl
from jax.experimental.pallas import tpu as pltpu

F32 = jnp.float32
MXU = jnp.bfloat16
HI = lax.Precision.HIGHEST
MESH_ID = pl.DeviceIdType.MESH

D = 1024
NH = 8
HD = 128
CH = 64
PAST = 8
DFF = 4096
EPS = 1e-6
NDEV = 8
DEPTH = 4
IN_W = 9232
ZW = 9728
Z_GATE, Z_ATT, Z_BR, Z_AB = 3072, 4096, 7168, 9216
QB = 256
MASKED = -1e30
KWIN = 768
FW = 1024
ADAM_LR, ADAM_B1, ADAM_B2, ADAM_EPS, ADAM_WD, ADAM_STEP = 0.001, 0.9, 0.999, 1e-08, 0.01, 10


def _dot(a, b, prec=None):
    return jnp.dot(a, b, preferred_element_type=F32, precision=prec)


def _dot_nt(a, b, prec=None):
    return lax.dot_general(a, b, (((1,), (1,)), ((), ())), preferred_element_type=F32, precision=prec)


def _dot_tn(a, b, prec=None):
    return lax.dot_general(a, b, (((0,), (0,)), ((), ())), preferred_element_type=F32, precision=prec)


def _mx(a):
    return a.astype(MXU)


def _sigmoid(x):
    return 1.0 / (1.0 + jnp.exp(-x))


def _softplus(x):
    return jnp.maximum(x, 0.0) + jnp.log(1.0 + jnp.exp(-jnp.abs(x)))


def _rowsum(x):
    return jnp.sum(x, axis=1, keepdims=True)


def _colsum(x):
    return jnp.sum(x, axis=0, keepdims=True)


def _call(body, name, grid, in_specs, out_specs, out_shape, scratch=(), comm=None):
    if comm is None:
        return pl.pallas_call(body, name=name, grid=grid, in_specs=in_specs, out_specs=out_specs,
                              out_shape=out_shape, scratch_shapes=list(scratch))
    kind, x = comm
    single = not isinstance(out_specs, (list, tuple))
    o_specs = [out_specs] if single else list(out_specs)
    o_shape = [out_shape] if single else list(out_shape)
    n_in, n_out, n_scr = len(in_specs), len(o_specs), len(scratch)
    c_shape = (NDEV,) + x.shape if kind == "gather" else x.shape

    def wrapped(*refs):
        ins, x_ref = refs[:n_in], refs[n_in]
        outs, c_ref = refs[n_in + 1:n_in + 1 + n_out], refs[n_in + 1 + n_out]
        scr = refs[n_in + 2 + n_out:n_in + 2 + n_out + n_scr]
        sems = refs[n_in + 2 + n_out + n_scr:]
        first = functools.reduce(jnp.logical_and, [pl.program_id(a) == 0 for a in range(len(grid))])
        last = functools.reduce(jnp.logical_and, [pl.program_id(a) == grid[a] - 1 for a in range(len(grid))])

        @pl.when(first)
        def _():
            _comm_start(*_comm_copies(kind, x_ref, c_ref, *sems))

        body(*ins, *outs, *scr)

        @pl.when(last)
        def _():
            _comm_wait(*_comm_copies(kind, x_ref, c_ref, *sems))

    any_spec = pl.BlockSpec(memory_space=pl.ANY)
    call = pl.pallas_call(
        wrapped, name=name, grid=grid, in_specs=list(in_specs) + [any_spec], out_specs=o_specs + [any_spec],
        out_shape=o_shape + [jax.ShapeDtypeStruct(c_shape, x.dtype)],
        scratch_shapes=list(scratch) + _comm_sems())

    def run(*args):
        res = call(*args, x)
        return (res[0] if single else list(res[:-1])), res[-1]

    return run


def _hosted(fn, comm):
    return (fn(), None) if comm is None else fn(comm=comm)


def _full(shape):
    n = len(shape)
    return pl.BlockSpec(shape, lambda *_: (0,) * n)


def _mesh_pos():
    return lax.axis_index("x"), lax.axis_index("y"), lax.axis_index("c")


def _peer(pos, k):
    x, y, c = pos
    return (x ^ ((k >> 2) & 1), y ^ ((k >> 1) & 1), c ^ (k & 1))


def _flat(pos):
    return 4 * pos[0] + 2 * pos[1] + pos[2]


def _comm_sems():
    return [pltpu.SemaphoreType.DMA((NDEV - 1,)), pltpu.SemaphoreType.DMA((NDEV - 1,)), pltpu.SemaphoreType.DMA]


def _comm_copies(kind, x_ref, out_ref, send_sems, recv_sems, local_sem):
    pos = _mesh_pos()
    me = _flat(pos)
    src = (lambda d: x_ref) if kind == "gather" else (lambda d: x_ref.at[d])
    mine = pltpu.make_async_copy(src(me), out_ref.at[me], local_sem)
    sends, recvs = [], []
    for k in range(1, NDEV):
        peer = _peer(pos, k)
        pid = _flat(peer)
        sems = dict(send_sem=send_sems.at[k - 1], recv_sem=recv_sems.at[k - 1], device_id=peer,
                    device_id_type=MESH_ID)
        sends.append(pltpu.make_async_remote_copy(src_ref=src(pid), dst_ref=out_ref.at[me], **sems))
        recvs.append(pltpu.make_async_remote_copy(src_ref=src(pid), dst_ref=out_ref.at[pid], **sems))
    return mine, sends, recvs


def _comm_start(mine, sends, recvs):
    mine.start()
    for cp in sends:
        cp.start()


def _comm_wait(mine, sends, recvs):
    for cp in recvs:
        cp.wait_recv()
    for cp in sends:
        cp.wait_send()
    mine.wait()


def _collective(kind, x, name):
    def body(x_ref, out_ref, *sems):
        copies = _comm_copies(kind, x_ref, out_ref, *sems)
        _comm_start(*copies)
        _comm_wait(*copies)

    shape = (NDEV,) + x.shape if kind == "gather" else x.shape
    return pl.pallas_call(
        body, name=name, out_shape=jax.ShapeDtypeStruct(shape, x.dtype),
        in_specs=[pl.BlockSpec(memory_space=pl.ANY)], out_specs=pl.BlockSpec(memory_space=pl.ANY),
        scratch_shapes=_comm_sems())(x)


def _all_gather(x, name):
    return _collective("gather", x, name)


def _mm_nn(a, w, name, *, mode="plain", res=None, gate=None, norm=None, tm=1024, tn=1024, tk=1024, comm=None):
    M, K = a.shape
    N = w.shape[1]
    tm, tk, tn = min(tm, M), min(tk, K), min(tn, N)
    nk = K // tk

    def body(*refs):
        refs = list(refs)
        acc = refs.pop() if nk > 1 else None
        if mode == "resid":
            a_ref, w_ref, res_ref, gate_ref, o_ref = refs
        elif mode == "resid_norm":
            a_ref, w_ref, res_ref, gate_ref, g_ref, sc_ref, sh_ref, o_ref, r_ref = refs
        elif mode == "relu2":
            a_ref, w_ref, o_ref, r_ref = refs
        else:
            a_ref, w_ref, o_ref = refs
        k = pl.program_id(2)
        part = _dot(_mx(a_ref[...]), w_ref[...])

        def finish(r):
            if mode == "resid":
                o_ref[...] = res_ref[...] + gate_ref[...] * r
            elif mode == "resid_norm":
                xv = res_ref[...] + gate_ref[...] * r
                o_ref[...] = xv
                rs = lax.rsqrt(jnp.mean(xv * xv, axis=1, keepdims=True) + EPS)
                r_ref[...] = ((xv * rs * g_ref[...]) * (1.0 + sc_ref[...]) + sh_ref[...]).astype(r_ref.dtype)
            elif mode == "relu2":
                o_ref[...] = r
                r_ref[...] = jnp.square(jnp.maximum(r, 0.0)).astype(r_ref.dtype)
            else:
                o_ref[...] = r

        if nk == 1:
            finish(part)
        else:
            @pl.when(k == 0)
            def _():
                acc[...] = part

            @pl.when((k > 0) & (k < nk - 1))
            def _():
                acc[...] += part

            @pl.when(k == nk - 1)
            def _():
                finish(acc[...] + part)

    in_specs = [pl.BlockSpec((tm, tk), lambda i, j, k: (i, k)),
                pl.BlockSpec((tk, tn), lambda i, j, k: (k, j))]
    args = [a, w]
    o_spec = pl.BlockSpec((tm, tn), lambda i, j, k: (i, j))
    out_specs, out_shape = o_spec, jax.ShapeDtypeStruct((M, N), F32)
    vec = pl.BlockSpec((1, tn), lambda i, j, k: (0, j))
    if mode == "resid":
        in_specs += [o_spec, vec]
        args += [res, gate]
    elif mode == "resid_norm":
        assert tn == N
        in_specs += [o_spec, vec, vec, vec, vec]
        args += [res, gate, *norm]
        out_specs = [o_spec, o_spec]
        out_shape = [jax.ShapeDtypeStruct((M, N), F32), jax.ShapeDtypeStruct((M, N), MXU)]
    elif mode == "relu2":
        out_specs = [o_spec, o_spec]
        out_shape = [jax.ShapeDtypeStruct((M, N), F32), jax.ShapeDtypeStruct((M, N), MXU)]
    return _call(body, name, (M // tm, N // tn, nk), in_specs, out_specs, out_shape,
                 [pltpu.VMEM((tm, tn), F32)] if nk > 1 else [], comm=comm)(*args)


def _mm_nt(a, w, name, *, gate=None, drelu=None, tm=1024, tko=1024, tn=1024, comm=None):
    M, N = a.shape
    K = w.shape[0]
    tm, tn, tko = min(tm, M), min(tn, N), min(tko, K)
    nn = N // tn

    def body(*refs):
        refs = list(refs)
        acc = refs.pop() if nn > 1 else None
        a_ref, w_ref = refs[:2]
        rest = refs[2:]
        gate_ref = rest.pop(0) if gate is not None else None
        pre_ref = rest.pop(0) if drelu is not None else None
        (o_ref,) = rest
        n = pl.program_id(2)
        av = a_ref[...]
        if gate_ref is not None:
            av = av * gate_ref[...]
        part = _dot_nt(_mx(av), w_ref[...])

        def finish(r):
            if pre_ref is not None:
                r = r * (2.0 * jnp.maximum(pre_ref[...], 0.0))
            o_ref[...] = r.astype(o_ref.dtype)

        if nn == 1:
            finish(part)
        else:
            @pl.when(n == 0)
            def _():
                acc[...] = part

            @pl.when((n > 0) & (n < nn - 1))
            def _():
                acc[...] += part

            @pl.when(n == nn - 1)
            def _():
                finish(acc[...] + part)

    in_specs = [pl.BlockSpec((tm, tn), lambda i, j, n: (i, n)),
                pl.BlockSpec((tko, tn), lambda i, j, n: (j, n))]
    args = [a, w]
    if gate is not None:
        in_specs.append(pl.BlockSpec((1, tn), lambda i, j, n: (0, n)))
        args.append(gate)
    o_spec = pl.BlockSpec((tm, tko), lambda i, j, n: (i, j))
    if drelu is not None:
        in_specs.append(o_spec)
        args.append(drelu)
    out_dtype = MXU if drelu is not None else F32
    return _call(body, name, (M // tm, K // tko, nn), in_specs, o_spec,
                 jax.ShapeDtypeStruct((M, K), out_dtype), [pltpu.VMEM((tm, tko), F32)] if nn > 1 else [],
                 comm=comm)(*args)


def _mm_tn(a, b, name, *, gate=None, w=None, tk=1024, tn=1024, tm=1024, comm=None):
    M, K = a.shape
    N = b.shape[1]
    tm, tk, tn = min(tm, M), min(tk, K), min(tn, N)
    nm = M // tm
    gated = gate is not None

    def body(*refs):
        if gated:
            a_ref, b_ref, gate_ref, w_ref, o_ref, dg_ref, acc = refs
        else:
            a_ref, b_ref, o_ref, acc = refs
        kk = pl.program_id(1)
        m = pl.program_id(2)
        part = _dot_tn(_mx(a_ref[...]), _mx(b_ref[...]))

        @pl.when((m == 0) & (nm > 1))
        def _():
            acc[...] = part

        @pl.when((m > 0) & (m < nm - 1))
        def _():
            acc[...] += part

        if gated:
            @pl.when((m == 0) & (kk == 0))
            def _():
                dg_ref[...] = jnp.zeros_like(dg_ref)

        @pl.when(m == nm - 1)
        def _():
            r = acc[...] + part if nm > 1 else part
            if gated:
                o_ref[...] = r * gate_ref[...]
                dg_ref[...] += _colsum(r * w_ref[...].astype(F32))
            else:
                o_ref[...] = r

    in_specs = [pl.BlockSpec((tm, tk), lambda j, k, m: (m, k)),
                pl.BlockSpec((tm, tn), lambda j, k, m: (m, j))]
    args = [a, b]
    o_spec = pl.BlockSpec((tk, tn), lambda j, k, m: (k, j))
    out_specs, out_shape = o_spec, jax.ShapeDtypeStruct((K, N), F32)
    if gated:
        in_specs += [pl.BlockSpec((1, tn), lambda j, k, m: (0, j)), o_spec]
        args += [gate, w]
        out_specs = [o_spec, pl.BlockSpec((1, tn), lambda j, k, m: (0, j))]
        out_shape = [out_shape, jax.ShapeDtypeStruct((1, N), F32)]
    return _call(body, name, (N // tn, K // tk, nm), in_specs, out_specs, out_shape,
                 [pltpu.VMEM((tk, tn), F32)], comm=comm)(*args)


SEG_T = 512


def _seg_layout(segs):
    starts, t = [], 0
    for a in segs:
        starts.append(t)
        t += a.shape[1] // SEG_T
    return starts, t


def _seg_spec(tm, lo, hi, row_axis, col_axis):
    def index(*ids):
        col = ids[col_axis]
        act = (col >= lo) & (col < hi)
        return jnp.where(act, ids[row_axis], 0), jnp.where(act, col - lo, 0)

    return pl.BlockSpec((tm, SEG_T), index)


def _in_proj_dw(h, segs, name, tm=1024):
    S = h.shape[0]
    tm = min(tm, S)
    nm = S // tm
    starts, ntile = _seg_layout(segs)
    bounds = [(lo, lo + a.shape[1] // SEG_T) for lo, a in zip(starts, segs)]

    def body(*refs):
        h_ref, seg_refs, o_ref, acc = refs[0], refs[1:1 + len(segs)], refs[-2], refs[-1]
        j = pl.program_id(0)
        m = pl.program_id(1)
        for (lo, hi), b_ref in zip(bounds, seg_refs):
            @pl.when((j >= lo) & (j < hi))
            def _():
                part = _dot_tn(h_ref[...], _mx(b_ref[...]))
                if nm == 1:
                    o_ref[...] = part
                else:
                    @pl.when(m == 0)
                    def _():
                        acc[...] = part

                    @pl.when((m > 0) & (m < nm - 1))
                    def _():
                        acc[...] += part

                    @pl.when(m == nm - 1)
                    def _():
                        o_ref[...] = acc[...] + part

    return _call(
        body, name, (ntile, nm),
        [pl.BlockSpec((tm, D), lambda j, m: (m, 0))] + [_seg_spec(tm, lo, hi, 1, 0) for lo, hi in bounds],
        pl.BlockSpec((D, SEG_T), lambda j, m: (0, j)), jax.ShapeDtypeStruct((D, ntile * SEG_T), F32),
        [pltpu.VMEM((D, SEG_T), F32)])(h, *segs)


def _in_proj_dx(segs, w, name, tm=1024, comm=None):
    S = segs[0].shape[0]
    tm = min(tm, S)
    starts, ntile = _seg_layout(segs)
    bounds = [(lo, lo + a.shape[1] // SEG_T) for lo, a in zip(starts, segs)]

    def body(*refs):
        seg_refs, w_ref, o_ref, acc = refs[:len(segs)], refs[-3], refs[-2], refs[-1]
        n = pl.program_id(1)
        for (lo, hi), a_ref in zip(bounds, seg_refs):
            @pl.when((n >= lo) & (n < hi))
            def _():
                part = _dot_nt(_mx(a_ref[...]), w_ref[...])

                @pl.when(n == 0)
                def _():
                    acc[...] = part

                @pl.when((n > 0) & (n < ntile - 1))
                def _():
                    acc[...] += part

                @pl.when(n == ntile - 1)
                def _():
                    o_ref[...] = acc[...] + part

    return _call(
        body, name, (S // tm, ntile),
        [_seg_spec(tm, lo, hi, 0, 1) for lo, hi in bounds] + [pl.BlockSpec((D, SEG_T), lambda i, n: (0, n))],
        pl.BlockSpec((tm, D), lambda i, n: (i, 0)), jax.ShapeDtypeStruct((S, D), F32),
        [pltpu.VMEM((tm, D), F32)], comm=comm)(*segs, w)


def _modnorm_fwd(x, gain, sc, sh, name, ts=512):
    S = x.shape[0]

    def body(x_ref, g_ref, sc_ref, sh_ref, h_ref):
        xv = x_ref[...]
        r = lax.rsqrt(jnp.mean(xv * xv, axis=1, keepdims=True) + EPS)
        h_ref[...] = ((xv * r * g_ref[...]) * (1.0 + sc_ref[...]) + sh_ref[...]).astype(h_ref.dtype)

    row = pl.BlockSpec((ts, D), lambda i: (i, 0))
    vec = pl.BlockSpec((1, D), lambda i: (0, 0))
    return _call(body, name, (S // ts,), [row, vec, vec, vec], row,
                 jax.ShapeDtypeStruct((S, D), MXU))(x, gain, sc, sh)


def _modnorm_bwd(dh, x, gain, sc, sh, dx_in, name, ts=512):
    S = x.shape[0]

    def body(dh_ref, x_ref, g_ref, sc_ref, sh_ref, dxin_ref, dx_ref, dsc_ref, dsh_ref, dg_ref):
        i = pl.program_id(0)
        xv = x_ref[...]
        dhv = dh_ref[...]
        g = g_ref[...]
        r = lax.rsqrt(jnp.mean(xv * xv, axis=1, keepdims=True) + EPS)
        xr = xv * r
        dn = dhv * (1.0 + sc_ref[...])
        u = dn * g
        dx_ref[...] = dxin_ref[...] + r * (u - xr * jnp.mean(xr * u, axis=1, keepdims=True))

        @pl.when(i == 0)
        def _():
            dsc_ref[...] = jnp.zeros_like(dsc_ref)
            dsh_ref[...] = jnp.zeros_like(dsh_ref)
            dg_ref[...] = jnp.zeros_like(dg_ref)

        dsc_ref[...] += _colsum(dhv * (xr * g))
        dsh_ref[...] += _colsum(dhv)
        dg_ref[...] += _colsum(dn * xr)

    row = pl.BlockSpec((ts, D), lambda i: (i, 0))
    vec = pl.BlockSpec((1, D), lambda i: (0, 0))
    vshape = jax.ShapeDtypeStruct((1, D), F32)
    return _call(body, name, (S // ts,), [row, row, vec, vec, vec, row], [row, vec, vec, vec],
                 [jax.ShapeDtypeStruct((S, D), F32), vshape, vshape, vshape])(dh, x, gain, sc, sh, dx_in)


def _loss_head(x, target, gain, name, ts=512):
    S = x.shape[0]

    def body(x_ref, t_ref, g_ref, loss_ref, dx_ref, dg_ref):
        i = pl.program_id(0)
        xv = x_ref[...]
        g = g_ref[...]
        r = lax.rsqrt(jnp.mean(xv * xv, axis=1, keepdims=True) + EPS)
        xr = xv * r
        e = xr * g - t_ref[...]
        dy = e * (1.0 / D)
        u = dy * g
        dx_ref[...] = r * (u - xr * jnp.mean(xr * u, axis=1, keepdims=True))

        @pl.when(i == 0)
        def _():
            loss_ref[...] = jnp.zeros_like(loss_ref)
            dg_ref[...] = jnp.zeros_like(dg_ref)

        part = 0.5 * jnp.sum(jnp.mean(e * e, axis=1, keepdims=True), axis=0, keepdims=True)
        loss_ref[...] += jnp.broadcast_to(part, loss_ref.shape)
        dg_ref[...] += _colsum(dy * xr)

    row = pl.BlockSpec((ts, D), lambda i: (i, 0))
    vec = pl.BlockSpec((1, D), lambda i: (0, 0))
    return _call(body, name, (S // ts,), [row, row, vec],
                 [pl.BlockSpec((1, 128), lambda i: (0, 0)), row, vec],
                 [jax.ShapeDtypeStruct((1, 128), F32), jax.ShapeDtypeStruct((S, D), F32),
                  jax.ShapeDtypeStruct((1, D), F32)])(x, target, gain)


def _merge_specs(ts):
    o_spec = pl.BlockSpec((NH, ts, HD), lambda i: (0, i, 0))
    zg = pl.BlockSpec((ts, D), lambda i: (i, Z_GATE // D))
    za = pl.BlockSpec((ts, D), lambda i: (i, Z_BR // D))
    zb = pl.BlockSpec((ts, D), lambda i: (i, Z_BR // D + 1))
    row = pl.BlockSpec((ts, D), lambda i: (i, 0))
    gn = pl.BlockSpec((1, HD), lambda i: (0, 0))
    return o_spec, zg, za, zb, row, gn


def _merge_fwd(o, z, ob, gn, name, ts=256):
    S = ob.shape[0]

    def body(o_ref, zg_ref, za_ref, zb_ref, ob_ref, gn_ref, m_ref):
        for h in range(NH):
            sl = slice(h * HD, (h + 1) * HD)
            oh = o_ref[h]
            r = lax.rsqrt(jnp.mean(oh * oh, axis=1, keepdims=True) + EPS)
            gate = zg_ref[:, sl]
            oa = (oh * r * gn_ref[...]) * (gate * _sigmoid(gate))
            m = _sigmoid(za_ref[:, sl]) * oa + _sigmoid(zb_ref[:, sl]) * ob_ref[:, sl]
            m_ref[:, sl] = m.astype(m_ref.dtype)

    o_spec, zg, za, zb, row, gns = _merge_specs(ts)
    return _call(body, name, (S // ts,), [o_spec, zg, za, zb, row, gns], row,
                 jax.ShapeDtypeStruct((S, D), MXU))(o, z, z, z, ob, gn)


def _merge_bwd(dm, o, z, ob, gn, name, ts=256):
    S = ob.shape[0]

    def body(dm_ref, o_ref, zg_ref, za_ref, zb_ref, ob_ref, gn_ref,
             do_ref, dzg_ref, dob_ref, dza_ref, dzb_ref, dgn_ref):
        i = pl.program_id(0)
        gn_v = gn_ref[...]
        dgn = jnp.zeros((1, HD), F32)
        for h in range(NH):
            sl = slice(h * HD, (h + 1) * HD)
            dmh = dm_ref[:, sl]
            oh = o_ref[h]
            r = lax.rsqrt(jnp.mean(oh * oh, axis=1, keepdims=True) + EPS)
            ohr = oh * r
            on = ohr * gn_v
            gate = zg_ref[:, sl]
            sg = _sigmoid(gate)
            silu = gate * sg
            oa = on * silu
            ga = _sigmoid(za_ref[:, sl])
            gb = _sigmoid(zb_ref[:, sl])
            obh = ob_ref[:, sl]
            doa = dmh * ga
            dob_ref[:, sl] = dmh * gb
            dza_ref[:, sl] = (dmh * oa * ga * (1.0 - ga)).astype(dza_ref.dtype)
            dzb_ref[:, sl] = (dmh * obh * gb * (1.0 - gb)).astype(dzb_ref.dtype)
            don = doa * silu
            dzg_ref[:, sl] = (doa * on * (sg * (1.0 + gate * (1.0 - sg)))).astype(dzg_ref.dtype)
            dgn = dgn + _colsum(don * ohr)
            u = don * gn_v
            do_ref[h] = r * (u - ohr * jnp.mean(ohr * u, axis=1, keepdims=True))

        @pl.when(i == 0)
        def _():
            dgn_ref[...] = jnp.zeros_like(dgn_ref)

        dgn_ref[...] += dgn

    o_spec, zg, za, zb, row, gns = _merge_specs(ts)
    return _call(
        body, name, (S // ts,), [row, o_spec, zg, za, zb, row, gns],
        [o_spec, row, row, row, row, gns],
        [jax.ShapeDtypeStruct((NH, S, HD), F32), jax.ShapeDtypeStruct((S, D), MXU),
         jax.ShapeDtypeStruct((S, D), F32), jax.ShapeDtypeStruct((S, D), MXU),
         jax.ShapeDtypeStruct((S, D), MXU), jax.ShapeDtypeStruct((1, HD), F32)],
    )(dm, o, z, z, z, ob, gn)


GROWS = 256
GCH = GROWS // CH


def _gdn_prep_fwd(z, conv_w, alog_row, dtb_row, name, comm=None):
    S = z.shape[0]
    ts = GROWS
    scale = HD ** -0.5

    def body(z_ref, halo_ref, zab_ref, w_ref, al_ref, dt_ref, q_ref, k_ref, v_ref, gb_ref, buf):
        i = pl.program_id(0)
        buf[0:8, :] = jnp.where(i == 0, 0.0, halo_ref[...])
        buf[8:8 + ts, :] = z_ref[...]
        outs = (q_ref, k_ref, v_ref)
        for seg in range(3):
            cs = slice(seg * D, (seg + 1) * D)
            c = jnp.zeros((ts, D), F32)
            for j in range(4):
                c = c + w_ref[j:j + 1, cs] * buf[pl.ds(5 + j, ts), cs]
            s = c * _sigmoid(c)
            if seg == 2:
                outs[seg][...] = s
            else:
                mul = scale if seg == 0 else 1.0
                for h in range(NH):
                    sl = slice(h * HD, (h + 1) * HD)
                    sh = s[:, sl]
                    r = lax.rsqrt(_rowsum(sh * sh) + EPS)
                    outs[seg][:, sl] = sh * (r * mul)
        zab = zab_ref[...]
        lane = lax.broadcasted_iota(jnp.int32, zab.shape, 1)
        g = -jnp.exp(al_ref[...]) * _softplus(zab + dt_ref[...])
        ri = lax.broadcasted_iota(jnp.int32, (CH, CH), 0)
        ci = lax.broadcasted_iota(jnp.int32, (CH, CH), 1)
        incl = (ri >= ci).astype(F32)
        gcum = jnp.concatenate([_dot(incl, g[c * CH:(c + 1) * CH], HI) for c in range(ts // CH)], axis=0)
        gb_ref[...] = jnp.where(lane < NH, gcum, jnp.where(lane < 2 * NH, _sigmoid(zab), 0.0))

    row = pl.BlockSpec((ts, D), lambda i: (i, 0))
    vec = pl.BlockSpec((1, 128), lambda i: (0, 0))
    return _call(
        body, name, (S // ts,),
        [pl.BlockSpec((ts, 3 * D), lambda i: (i, 0)),
         pl.BlockSpec((8, 3 * D), lambda i: (jnp.maximum(i * (ts // 8) - 1, 0), 0)),
         pl.BlockSpec((ts, 128), lambda i: (i, Z_AB // 128)),
         _full((4, 3 * D)), vec, vec],
        [row, row, row, pl.BlockSpec((ts, 128), lambda i: (i, 0))],
        [jax.ShapeDtypeStruct((S, D), F32)] * 3 + [jax.ShapeDtypeStruct((S, 128), F32)],
        [pltpu.VMEM((ts + 8, 3 * D), F32)], comm=comm,
    )(z, z, z, conv_w, alog_row, dtb_row)


def _split(a):
    hi = a.astype(MXU)
    return hi, (a - hi.astype(F32)).astype(MXU)


def _dot3(a, b, dot=_dot):
    ah, al = _split(a)
    bh, bl = _split(b)
    return dot(ah, bh) + (dot(ah, bl) + dot(al, bh))


IROWS = 512
ICH = IROWS // CH


def _chunk_common(gbk, h, k):
    lane = lax.broadcasted_iota(jnp.int32, gbk.shape, 1)
    G = _rowsum(jnp.where(lane == h, gbk, 0.0))
    b_col = _rowsum(jnp.where(lane == h + NH, gbk, 0.0))
    ri = lax.broadcasted_iota(jnp.int32, (CH, CH), 0)
    ci = lax.broadcasted_iota(jnp.int32, (CH, CH), 1)
    incl = ri >= ci
    gc = jnp.broadcast_to(G, (CH, CH))
    decay = jnp.where(incl, jnp.exp(jnp.where(incl, gc - gc.T, 0.0)), 0.0)
    Gl = G[CH - 1:CH, :]
    kb = k * b_col
    return dict(b=b_col, ri=ri, ci=ci, incl=incl, strict=ri > ci, decay=decay, eG=jnp.exp(G),
                e2=jnp.exp(Gl - G), cd=jnp.exp(Gl), kb=kb, kk=_dot_nt(_mx(kb), _mx(k)))


def _gdn_intra_fwd(qn, kn, v, gb, name, comm=None):
    S = qn.shape[0]

    def body(q_ref, k_ref, v_ref, gb_ref, u_ref, w_ref, qd_ref, kd_ref, at_ref, t_ref, cd_ref):
        h = pl.program_id(1)
        rows = [slice(c * CH, (c + 1) * CH) for c in range(ICH)]
        ks = [k_ref[r, :] for r in rows]
        cms = [_chunk_common(gb_ref[r, :], h, k) for r, k in zip(rows, ks)]
        ps = [jnp.where(cm["strict"], cm["kk"] * cm["decay"], 0.0) for cm in cms]
        ts = [(cm["ri"] == cm["ci"]).astype(F32) - p for cm, p in zip(cms, ps)]
        for _ in range(5):
            ps = [_dot3(p, p) for p in ps]
            ts = [t + _dot3(t, p) for t, p in zip(ts, ps)]
        for c, (r, k, cm, t) in enumerate(zip(rows, ks, cms, ts)):
            rhs = jnp.concatenate([v_ref[r, :] * cm["b"], k * (cm["b"] * cm["eG"])], axis=1)
            sol = _dot3(t, rhs)
            u_ref[0, r, :] = sol[:, :HD]
            w_ref[0, r, :] = sol[:, HD:]
            t_ref[0, r, :] = t
        for c, (r, k, cm) in enumerate(zip(rows, ks, cms)):
            q = q_ref[r, :]
            qk = _dot_nt(_mx(q), _mx(k))
            at_ref[0, r, :] = jnp.where(cm["incl"], qk * cm["decay"], 0.0)
            qd_ref[0, r, :] = q * cm["eG"]
            kd_ref[0, r, :] = k * cm["e2"]
            cd_ref[0, c] = jnp.broadcast_to(cm["cd"], (8, 128))

    tok = pl.BlockSpec((IROWS, HD), lambda i, h: (i, h))
    hm = pl.BlockSpec((1, IROWS, HD), lambda i, h: (h, i, 0))
    hm64 = pl.BlockSpec((1, IROWS, CH), lambda i, h: (h, i, 0))
    big = jax.ShapeDtypeStruct((NH, S, HD), F32)
    sm = jax.ShapeDtypeStruct((NH, S, CH), F32)
    return _call(
        body, name, (S // IROWS, NH),
        [tok, tok, tok, pl.BlockSpec((IROWS, 128), lambda i, h: (i, 0))],
        [hm, hm, hm, hm, hm64, hm64, pl.BlockSpec((1, ICH, 8, 128), lambda i, h: (h, i, 0, 0))],
        [big, big, big, big, sm, sm, jax.ShapeDtypeStruct((NH, S // CH, 8, 128), F32)], comm=comm,
    )(qn, kn, v, gb)


def _scale_state(s, cd_tile):
    return (s.reshape(HD // 8, 8, HD) * cd_tile[None]).reshape(HD, HD)


def _gdn_scan_fwd(u, w, qd, kd, attn, cdt, name):
    S = u.shape[1]
    nblk = S // GROWS

    def body(u_ref, w_ref, qd_ref, kd_ref, at_ref, cd_ref, o_ref, vn_ref, st_ref, s_ref):
        i = pl.program_id(0)

        @pl.when(i == 0)
        def _():
            s_ref[...] = jnp.zeros_like(s_ref)

        def chunk(c, carry):
            r0 = pl.multiple_of(c * CH, CH)
            rows = pl.ds(r0, CH)
            for h in range(NH):
                sh = s_ref[h]
                st_ref[h, c] = sh
                sb = _mx(sh)
                vn = u_ref[h, rows, :] - _dot(_mx(w_ref[h, rows, :]), sb)
                vb = _mx(vn)
                vn_ref[h, rows, :] = vn
                o_ref[h, rows, :] = _dot(_mx(qd_ref[h, rows, :]), sb) + _dot(_mx(at_ref[h, rows, :]), vb)
                s_ref[h] = _scale_state(sh, cd_ref[h, c]) + _dot_tn(_mx(kd_ref[h, rows, :]), vb)
            return carry

        lax.fori_loop(0, GCH, chunk, 0)

    hm = pl.BlockSpec((NH, GROWS, HD), lambda i: (0, i, 0))
    hm64 = pl.BlockSpec((NH, GROWS, CH), lambda i: (0, i, 0))
    big = jax.ShapeDtypeStruct((NH, S, HD), F32)
    return _call(
        body, name, (nblk,),
        [hm, hm, hm, hm, hm64, pl.BlockSpec((NH, GCH, 8, 128), lambda i: (0, i, 0, 0))],
        [hm, hm, pl.BlockSpec((NH, GCH, HD, HD), lambda i: (0, i, 0, 0))],
        [big, big, jax.ShapeDtypeStruct((NH, S // CH, HD, HD), F32)],
        [pltpu.VMEM((NH, HD, HD), F32)],
    )(u, w, qd, kd, attn, cdt)


def _gdn_scan_bwd(do, w, qd, kd, attn, cdt, vn, st, name):
    S = do.shape[1]
    nblk = S // GROWS

    def body(do_ref, w_ref, qd_ref, kd_ref, at_ref, cd_ref, vn_ref, st_ref,
             dqd_ref, dkd_ref, dvn_ref, dw_ref, dat_ref, dcd_ref, ds_ref):
        i = pl.program_id(0)

        @pl.when(i == 0)
        def _():
            ds_ref[...] = jnp.zeros_like(ds_ref)

        def chunk(cc, carry):
            c = GCH - 1 - cc
            r0 = pl.multiple_of(c * CH, CH)
            rows = pl.ds(r0, CH)
            for h in range(NH):
                dsp = ds_ref[h]
                sh = st_ref[h, c]
                dsb, sb = _mx(dsp), _mx(sh)
                dob = _mx(do_ref[h, rows, :])
                vb = _mx(vn_ref[h, rows, :])
                dvn = _dot(_mx(kd_ref[h, rows, :]), dsb) + _dot_tn(_mx(at_ref[h, rows, :]), dob)
                dvb = _mx(dvn)
                dvn_ref[h, rows, :] = dvn
                dqd_ref[h, rows, :] = _dot_nt(dob, sb)
                dat_ref[h, rows, :] = _dot_nt(dob, vb)
                dkd_ref[h, rows, :] = _dot_nt(vb, dsb)
                dw_ref[h, rows, :] = -_dot_nt(dvb, sb)
                dcd = jnp.sum(_rowsum(dsp * sh), axis=0, keepdims=True)
                dcd_ref[h, c] = jnp.broadcast_to(dcd, (8, 128))
                ds_ref[h] = (_scale_state(dsp, cd_ref[h, c]) + _dot_tn(_mx(qd_ref[h, rows, :]), dob)
                             - _dot_tn(_mx(w_ref[h, rows, :]), dvb))
            return carry

        lax.fori_loop(0, GCH, chunk, 0)

    hm = pl.BlockSpec((NH, GROWS, HD), lambda i: (0, nblk - 1 - i, 0))
    hm64 = pl.BlockSpec((NH, GROWS, CH), lambda i: (0, nblk - 1 - i, 0))
    tile = pl.BlockSpec((NH, GCH, 8, 128), lambda i: (0, nblk - 1 - i, 0, 0))
    big = jax.ShapeDtypeStruct((NH, S, HD), F32)
    return _call(
        body, name, (nblk,),
        [hm, hm, hm, hm, hm64, tile, hm, pl.BlockSpec((NH, GCH, HD, HD), lambda i: (0, nblk - 1 - i, 0, 0))],
        [hm, hm, hm, hm, hm64, tile],
        [big, big, big, big, jax.ShapeDtypeStruct((NH, S, CH), F32),
         jax.ShapeDtypeStruct((NH, S // CH, 8, 128), F32)],
        [pltpu.VMEM((NH, HD, HD), F32)],
    )(do, w, qd, kd, attn, cdt, vn, st)


def _gdn_intra_bwd(qn, kn, v, gb, u, w, tmat, dqd, dkd, du, dw, dattn, dcdt, name):
    S = qn.shape[0]

    def body(q_ref, k_ref, v_ref, gb_ref, u_ref, w_ref, t_ref, dqd_ref, dkd_ref, du_ref, dw_ref,
             dat_ref, dcd_ref, dq_ref, dk_ref, dv_ref, dgb_ref):
        h = pl.program_id(1)
        rows = [slice(c * CH, (c + 1) * CH) for c in range(ICH)]
        ks = [k_ref[r, :] for r in rows]
        cms = [_chunk_common(gb_ref[r, :], h, k) for r, k in zip(rows, ks)]
        sols = [jnp.concatenate([u_ref[0, r, :], w_ref[0, r, :]], axis=1) for r in rows]
        drhss = [_dot3(t_ref[0, r, :], jnp.concatenate([du_ref[0, r, :], dw_ref[0, r, :]], axis=1), _dot_tn)
                 for r in rows]
        das = [-_dot3(drhs, sol, _dot_nt) for drhs, sol in zip(drhss, sols)]
        for c, (r, k, cm, drhs, da) in enumerate(zip(rows, ks, cms, drhss, das)):
            q, vv = q_ref[r, :], v_ref[r, :]
            decay, eG, e2, b = cm["decay"], cm["eG"], cm["e2"], cm["b"]
            dru, drw = drhs[:, :HD], drhs[:, HD:]
            dv_ref[r, :] = dru * b
            s_w = _rowsum(drw * k)
            dbeta = _rowsum(dru * vv) + s_w * eG
            deg = s_w * b
            dk = drw * (b * eG)
            dkk = jnp.where(cm["strict"], da * decay, 0.0)
            ddec = jnp.where(cm["strict"], da * cm["kk"], 0.0)
            dkkb = _mx(dkk)
            dkb = _dot(dkkb, _mx(k))
            dk = dk + _dot_tn(dkkb, _mx(cm["kb"])) + dkb * b
            dbeta = dbeta + _rowsum(dkb * k)
            dat = jnp.where(cm["incl"], dat_ref[0, r, :], 0.0)
            qk = _dot_nt(_mx(q), _mx(k))
            dqk = _mx(dat * decay)
            ddec = ddec + dat * qk
            dqd = dqd_ref[0, r, :]
            dkd = dkd_ref[0, r, :]
            dq_ref[r, :] = _dot(dqk, _mx(k)) + dqd * eG
            dk_ref[r, :] = dk + _dot_tn(dqk, _mx(q)) + dkd * e2
            deg = deg + _rowsum(dqd * q)
            t2 = _rowsum(dkd * k) * e2
            dgl = jnp.sum(t2, axis=0, keepdims=True) + dcd_ref[0, c][0:1, 0:1] * cm["cd"]
            dd = ddec * decay
            dG = deg * eG - t2 + _rowsum(dd) - _rowsum(dd.T)
            rowi = lax.broadcasted_iota(jnp.int32, (CH, 1), 0)
            dG = dG + jnp.where(rowi == CH - 1, dgl, 0.0)
            lane = lax.broadcasted_iota(jnp.int32, (CH, 128), 1)
            dgb_ref[0, r, :] = jnp.where(lane == h, dG, 0.0) + jnp.where(lane == h + NH, dbeta, 0.0)

    tok = pl.BlockSpec((IROWS, HD), lambda i, h: (i, h))
    hm = pl.BlockSpec((1, IROWS, HD), lambda i, h: (h, i, 0))
    hm64 = pl.BlockSpec((1, IROWS, CH), lambda i, h: (h, i, 0))
    tile = pl.BlockSpec((1, ICH, 8, 128), lambda i, h: (h, i, 0, 0))
    tokout = jax.ShapeDtypeStruct((S, D), F32)
    return _call(
        body, name, (S // IROWS, NH),
        [tok, tok, tok, pl.BlockSpec((IROWS, 128), lambda i, h: (i, 0)), hm, hm, hm64,
         hm, hm, hm, hm, hm64, tile],
        [tok, tok, tok, pl.BlockSpec((1, IROWS, 128), lambda i, h: (h, i, 0))],
        [tokout, tokout, tokout, jax.ShapeDtypeStruct((NH, S, 128), F32)],
    )(qn, kn, v, gb, u, w, tmat, dqd, dkd, du, dw, dattn, dcdt)


def _gdn_prep_bwd(z, dqn, dkn, dv, dgb, conv_w, alog_row, dtb_row, name):
    S = z.shape[0]
    ts = GROWS
    nblk = S // ts
    scale = HD ** -0.5
    tb = ts // 8

    def body(z_ref, hp_ref, hn_ref, zab_ref, dq_ref, dqn_ref, dk_ref, dkn_ref, dv_ref, dvn_ref,
             dgb_ref, w_ref, al_ref, dt_ref, dz_ref, dzab_ref, dcw_ref, dvec_ref, buf, dybuf, dcbuf):
        i = pl.program_id(0)
        last = i == nblk - 1

        @pl.when(i == 0)
        def _():
            dcw_ref[...] = jnp.zeros_like(dcw_ref)
            dvec_ref[...] = jnp.zeros_like(dvec_ref)

        buf[0:8, :] = jnp.where(i == 0, 0.0, hp_ref[...])
        buf[8:8 + ts, :] = z_ref[...]
        buf[8 + ts:16 + ts, :] = hn_ref[...]
        rowi = lax.broadcasted_iota(jnp.int32, (ts + 8, 1), 0)
        live = jnp.logical_or(rowi < ts, jnp.logical_not(last))
        dys = ((dq_ref, dqn_ref), (dk_ref, dkn_ref), (dv_ref, dvn_ref))
        for seg in range(3):
            cs = slice(seg * D, (seg + 1) * D)
            dybuf[0:ts, :] = dys[seg][0][...]
            dybuf[ts:ts + 8, :] = dys[seg][1][...]
            c = jnp.zeros((ts + 8, D), F32)
            for j in range(4):
                c = c + w_ref[j:j + 1, cs] * buf[pl.ds(5 + j, ts + 8), cs]
            sg = _sigmoid(c)
            s = c * sg
            dsilu = sg * (1.0 + c * (1.0 - sg))
            if seg == 2:
                dcbuf[...] = jnp.where(live, dybuf[...] * dsilu, 0.0)
            else:
                mul = scale if seg == 0 else 1.0
                for h in range(NH):
                    sl = slice(h * HD, (h + 1) * HD)
                    sh = s[:, sl]
                    dy = dybuf[:, sl]
                    r = lax.rsqrt(_rowsum(sh * sh) + EPS)
                    shr = sh * r
                    ds = (mul * r) * (dy - shr * _rowsum(shr * dy))
                    dcbuf[:, sl] = jnp.where(live, ds * dsilu[:, sl], 0.0)
            dx = jnp.zeros((ts, D), F32)
            for j in range(4):
                dcw_ref[j:j + 1, cs] += _colsum(dcbuf[0:ts, :] * buf[pl.ds(5 + j, ts), cs])
                dx = dx + w_ref[j:j + 1, cs] * dcbuf[pl.ds(3 - j, ts), :]
            dz_ref[:, cs] = dx.astype(dz_ref.dtype)
        dgbs = dgb_ref[0]
        for h in range(1, NH):
            dgbs = dgbs + dgb_ref[h]
        ri = lax.broadcasted_iota(jnp.int32, (CH, CH), 0)
        ci = lax.broadcasted_iota(jnp.int32, (CH, CH), 1)
        rev = (ci >= ri).astype(F32)
        dgrev = jnp.concatenate([_dot(rev, dgbs[c * CH:(c + 1) * CH], HI) for c in range(ts // CH)], axis=0)
        lane0 = lax.broadcasted_iota(jnp.int32, dgbs.shape, 1)
        dgbs = jnp.where(lane0 < NH, dgrev, dgbs)
        zab = zab_ref[...]
        lane = lax.broadcasted_iota(jnp.int32, zab.shape, 1)
        xx = zab + dt_ref[...]
        ea = jnp.exp(al_ref[...])
        g = -ea * _softplus(xx)
        da = dgbs * (-ea) * _sigmoid(xx)
        beta = _sigmoid(zab)
        db = dgbs * beta * (1.0 - beta)
        is_a = lane < NH
        dzab = jnp.where(is_a, da, jnp.where(lane < 2 * NH, db, 0.0))
        dzab_ref[:, 0:128] = dzab.astype(dzab_ref.dtype)
        dzab_ref[:, 128:512] = jnp.zeros((ts, 384), dzab_ref.dtype)
        dvec_ref[0:1, :] += _colsum(jnp.where(is_a, dgbs * g, 0.0))
        dvec_ref[1:2, :] += _colsum(jnp.where(is_a, da, 0.0))

    z3 = pl.BlockSpec((ts, 3 * D), lambda i: (i, 0))
    row = pl.BlockSpec((ts, D), lambda i: (i, 0))
    nxt = pl.BlockSpec((8, D), lambda i: (jnp.minimum((i + 1) * tb, S // 8 - 1), 0))
    vec = pl.BlockSpec((1, 128), lambda i: (0, 0))
    return _call(
        body, name, (nblk,),
        [z3,
         pl.BlockSpec((8, 3 * D), lambda i: (jnp.maximum(i * tb - 1, 0), 0)),
         pl.BlockSpec((8, 3 * D), lambda i: (jnp.minimum((i + 1) * tb, S // 8 - 1), 0)),
         pl.BlockSpec((ts, 128), lambda i: (i, Z_AB // 128)),
         row, nxt, row, nxt, row, nxt,
         pl.BlockSpec((NH, ts, 128), lambda i: (0, i, 0)),
         _full((4, 3 * D)), vec, vec],
        [z3, pl.BlockSpec((ts, 512), lambda i: (i, 0)), _full((8, 3 * D)), _full((8, 128))],
        [jax.ShapeDtypeStruct((S, 3 * D), MXU), jax.ShapeDtypeStruct((S, 512), MXU),
         jax.ShapeDtypeStruct((8, 3 * D), F32), jax.ShapeDtypeStruct((8, 128), F32)],
        [pltpu.VMEM((ts + 16, 3 * D), F32), pltpu.VMEM((ts + 8, D), F32), pltpu.VMEM((ts + 8, D), F32)],
    )(z, z, z, z, dqn, dqn, dkn, dkn, dv, dv, dgb, conv_w, alog_row, dtb_row)


def _bias_index():
    u = np.arange(FW)[None, :]
    s = np.arange(3)[:, None]
    return np.clip(KWIN - 1 - u - QB * s, -256, 256) + 256


def _bias_vec(rel_bias_pad, onehot, name):
    def body(rb_ref, e_ref, o_ref):
        o_ref[:, 0, :] = _dot_nt(rb_ref[...], e_ref[0], HI)

    return _call(body, name, (3,),
                 [_full((NH, 640)), pl.BlockSpec((1, FW, 640), lambda s: (s, 0, 0))],
                 pl.BlockSpec((NH, 1, FW), lambda s: (s, 0, 0)),
                 jax.ShapeDtypeStruct((3 * NH, 1, FW), F32))(rel_bias_pad, onehot)


def _att_window(i):
    return pl.multiple_of(jnp.maximum(i * QB - PAST * CH, 0), QB)


def _bias_mask(fvec, name):
    def body(f_ref, o_ref):
        i = 2 - pl.program_id(0) // NH
        ws = jnp.maximum(i * QB - PAST * CH, 0)
        fb = jnp.broadcast_to(f_ref[0], (QB, FW))
        bias = pltpu.roll(fb, FW - 255, 1, stride=1, stride_axis=0)[:, :KWIN]
        qc = (i * QB + lax.broadcasted_iota(jnp.int32, (QB, KWIN), 0)) // CH
        kc = (ws + lax.broadcasted_iota(jnp.int32, (QB, KWIN), 1)) // CH
        o_ref[0] = jnp.where((kc <= qc) & (kc >= qc - PAST), bias, MASKED)

    return _call(body, name, (3 * NH,), [pl.BlockSpec((1, 1, FW), lambda j: (j, 0, 0))],
                 pl.BlockSpec((1, QB, KWIN), lambda j: (j, 0, 0)),
                 jax.ShapeDtypeStruct((3 * NH, QB, KWIN), F32))(fvec)


def _att_scores(q_ref, k_ref, bm_ref, i):
    ws = _att_window(i)
    q = _mx(q_ref[...] * (HD ** -0.5))
    kw = _mx(k_ref[pl.ds(ws, KWIN), :])
    return q, kw, ws, _dot_nt(q, kw) + bm_ref[0]


def _att_specs(S):
    c0 = Z_ATT // HD
    q = pl.BlockSpec((QB, HD), lambda h, i: (i, c0 + h))
    k = pl.BlockSpec((S, HD), lambda h, i: (0, c0 + NH + h))
    v = pl.BlockSpec((S, HD), lambda h, i: (0, c0 + 2 * NH + h))
    bm = pl.BlockSpec((1, QB, KWIN), lambda h, i: (jnp.maximum(2 - i, 0) * NH + h, 0, 0))
    tok = pl.BlockSpec((QB, HD), lambda h, i: (i, h))
    return q, k, v, bm, tok


def _att_fwd(z, bmask, name, comm=None):
    S = z.shape[0]

    def body(q_ref, k_ref, v_ref, bm_ref, o_ref, lse_ref):
        _, _, ws, s = _att_scores(q_ref, k_ref, bm_ref, pl.program_id(1))
        m = jnp.max(s, axis=1, keepdims=True)
        p = jnp.exp(s - m)
        l = _rowsum(p)
        o_ref[...] = _dot(_mx(p), _mx(v_ref[pl.ds(ws, KWIN), :])) * (1.0 / l)
        lse_ref[...] = jnp.broadcast_to(m + jnp.log(l), (QB, HD))

    q, k, v, bm, tok = _att_specs(S)
    shp = jax.ShapeDtypeStruct((S, D), F32)
    return _call(body, name, (NH, S // QB), [q, k, v, bm], [tok, tok], [shp, shp], comm=comm)(z, z, z, bmask)


def _att_bwd(z, bmask, ob, lse, dob, name):
    S = z.shape[0]
    nq = S // QB

    def body(q_ref, k_ref, v_ref, bm_ref, o_ref, lse_ref, do_ref, dq_ref, dk_ref, dv_ref, db_ref, dk_acc, dv_acc):
        i = pl.program_id(1)
        q, kw, ws, s = _att_scores(q_ref, k_ref, bm_ref, i)
        p = jnp.exp(s - lse_ref[:, 0:1])
        do = do_ref[...]
        dob16 = _mx(do)
        dp = _dot_nt(dob16, _mx(v_ref[pl.ds(ws, KWIN), :]))
        ds = p * (dp - _rowsum(do * o_ref[...]))
        dsb = _mx(ds)
        dq_ref[...] = (_dot(dsb, kw) * (HD ** -0.5)).astype(dq_ref.dtype)

        @pl.when(i == 0)
        def _():
            dk_acc[...] = jnp.zeros_like(dk_acc)
            dv_acc[...] = jnp.zeros_like(dv_acc)

        dk_acc[pl.ds(ws, KWIN), :] += _dot_tn(dsb, q)
        dv_acc[pl.ds(ws, KWIN), :] += _dot_tn(_mx(p), dob16)

        @pl.when(i == nq - 1)
        def _():
            dk_ref[...] = dk_acc[...].astype(dk_ref.dtype)
            dv_ref[...] = dv_acc[...].astype(dv_ref.dtype)

        @pl.when(i <= 2)
        def _():
            db_ref[0] = ds

        @pl.when(i > 2)
        def _():
            db_ref[0] += ds

    q, k, v, bm, tok = _att_specs(S)
    acc = pl.BlockSpec((S, HD), lambda h, i: (0, h))
    half = jax.ShapeDtypeStruct((S, D), MXU)
    return _call(
        body, name, (NH, nq), [q, k, v, bm, tok, tok, tok], [tok, acc, acc, bm],
        [half, half, half, jax.ShapeDtypeStruct((3 * NH, QB, KWIN), F32)],
        [pltpu.VMEM((S, HD), F32), pltpu.VMEM((S, HD), F32)],
    )(z, z, z, bmask, ob, lse, dob)


def _bias_fold(dbias, onehot, name):
    def body(db_ref, e_ref, o_ref):
        j = pl.program_id(0)
        h = j % NH
        x = jnp.concatenate([db_ref[0], jnp.zeros((QB, FW - KWIN), F32)], axis=1)
        half = QB // 2
        while half >= 8:
            x = x[:half] + pltpu.roll(x[half:2 * half], FW - half, 1)
            half //= 2
        df = jnp.zeros((1, FW), F32)
        for r in range(8):
            df = df + pltpu.roll(x[r:r + 1], 255 - r, 1)
        contrib = _dot(df, e_ref[0], HI)
        rowh = lax.broadcasted_iota(jnp.int32, (NH, 640), 0)

        @pl.when(j == 0)
        def _():
            o_ref[...] = jnp.zeros_like(o_ref)

        o_ref[...] += jnp.where(rowh == h, contrib, 0.0)

    return _call(
        body, name, (3 * NH,),
        [pl.BlockSpec((1, QB, KWIN), lambda j: (j, 0, 0)),
         pl.BlockSpec((1, FW, 640), lambda j: (j // NH, 0, 0))],
        _full((NH, 640)), jax.ShapeDtypeStruct((NH, 640), F32),
    )(dbias, onehot)


ADA_SHARD = 6 * D // NDEV


def _ada_mod(c_all, w_ada, b_shard, name):
    def body(c_ref, w_ref, b_ref, o_ref):
        cv = c_ref[...]
        ca = cv * _sigmoid(cv)
        o_ref[0] = _dot(_mx(ca), _mx(w_ref[0])) + b_ref[0]

    return _call(
        body, name, (DEPTH,),
        [_full((NDEV, D)), pl.BlockSpec((1, D, ADA_SHARD), lambda l: (l, 0, 0)),
         pl.BlockSpec((1, 1, ADA_SHARD), lambda l: (l, 0, 0))],
        pl.BlockSpec((1, NDEV, ADA_SHARD), lambda l: (l, 0, 0)),
        jax.ShapeDtypeStruct((DEPTH, NDEV, ADA_SHARD), F32),
    )(c_all, w_ada, b_shard.reshape(DEPTH, 1, ADA_SHARD))


def _adam(g, w, m, v):
    m = ADAM_B1 * m + (1.0 - ADAM_B1) * g
    v = ADAM_B2 * v + (1.0 - ADAM_B2) * jnp.square(g)
    m_hat = m / (1.0 - ADAM_B1 ** ADAM_STEP)
    v_hat = v / (1.0 - ADAM_B2 ** ADAM_STEP)
    delta = -ADAM_LR * (m_hat / (jnp.sqrt(v_hat) + ADAM_EPS) + ADAM_WD * w)
    return delta, m, v


def _wada_adamw(c_all_t, dmod, w, m, v, name):
    def body(c_ref, d_ref, w_ref, m_ref, v_ref, g_ref, dl_ref, mo_ref, vo_ref):
        cv = c_ref[...]
        ca = cv * _sigmoid(cv)
        g = _dot(ca, d_ref[0], HI)
        g_ref[0] = g
        dl_ref[0], mo_ref[0], vo_ref[0] = _adam(g, w_ref[0], m_ref[0], v_ref[0])

    blk = pl.BlockSpec((1, D, ADA_SHARD), lambda l: (l, 0, 0))
    shp = jax.ShapeDtypeStruct((DEPTH, D, ADA_SHARD), F32)
    return _call(
        body, name, (DEPTH,),
        [_full((D, NDEV)), pl.BlockSpec((1, NDEV, ADA_SHARD), lambda l: (l, 0, 0)), blk, blk, blk],
        [blk] * 4, [shp] * 4,
    )(c_all_t, dmod, w, m, v)


def _adamw_reduce(parts, w, m, v, name, tr):
    P, R, C = parts.shape

    def body(p_ref, w_ref, m_ref, v_ref, g_ref, dl_ref, mo_ref, vo_ref):
        g = p_ref[0].astype(F32)
        for k in range(1, P):
            g = g + p_ref[k].astype(F32)
        g_ref[...] = g
        dl_ref[...], mo_ref[...], vo_ref[...] = _adam(g, w_ref[...], m_ref[...], v_ref[...])

    blk = pl.BlockSpec((tr, C), lambda i: (i, 0))
    shp = jax.ShapeDtypeStruct((R, C), F32)
    return _call(body, name, (R // tr,), [pl.BlockSpec((P, tr, C), lambda i: (0, i, 0)), blk, blk, blk],
                 [blk] * 4, [shp] * 4)(parts, w, m, v)


def _sum_parts(parts, name):
    P, R, C = parts.shape

    def body(p_ref, o_ref):
        g = p_ref[0]
        for k in range(1, P):
            g = g + p_ref[k]
        o_ref[...] = g

    return _call(body, name, (1,), [_full((P, R, C))], _full((R, C)),
                 jax.ShapeDtypeStruct((R, C), F32))(parts)


def _pack_rows(vecs, width=1024):
    flat = jnp.concatenate([a.reshape(-1) for a in vecs])
    n = flat.shape[0]
    rows = -(-n // width)
    rows = -(-rows // 8) * 8
    return jnp.pad(flat, (0, rows * width - n)).reshape(rows, width)


def _unpack_rows(packed, shapes):
    flat = packed.reshape(-1)
    out, off = [], 0
    for s in shapes:
        n = int(np.prod(s)) if len(s) else 1
        out.append(flat[off:off + n].reshape(s))
        off += n
    return out


BIG = ("w_in", "w_out", "w_ff_in", "w_ff_out")


def _z_weights(g_in):
    w = jnp.transpose(g_in, (1, 0, 2)).reshape(D, IN_W)
    return jnp.concatenate([w[:, :Z_ATT], w[:, Z_ATT + 2 * NH:], w[:, Z_ATT:Z_ATT + 2 * NH],
                            jnp.zeros((D, ZW - IN_W), MXU)], axis=-1)


def _cols_to_owners(a):
    return jnp.transpose(a.astype(MXU).reshape(a.shape[0], NDEV, -1), (1, 0, 2))


def _rows_to_owners(a):
    return a.astype(MXU).reshape(NDEV, -1, a.shape[1])


def _forward_layer(x, mod_l, p, shard, next_w_in):
    sh1, sc1, gt1, sh2, sc2, gt2 = [mod_l[k][None] for k in range(6)]
    P = functools.partial
    gather = lambda a: None if a is None else ("gather", a)
    h = _modnorm_fwd(x, p["norm_mix"], sc1, sh1, "norm_mix_fwd")
    z, g_in = _hosted(P(_mm_nn, h, p["wz"], "in_proj", tm=2048, tn=512), gather(next_w_in))
    (qn, kn, v, gb), g_w1 = _hosted(P(_gdn_prep_fwd, z, p["conv_w"], p["alog"], p["dtb"], "gdn_prep_fwd"),
                                    gather(shard["w_ff_in"]))
    (u, w, qd, kd, attn, tmat, cdt), g_w2 = _hosted(P(_gdn_intra_fwd, qn, kn, v, gb, "gdn_intra_fwd"),
                                                    gather(shard["w_ff_out"]))
    o, vn, st = _gdn_scan_fwd(u, w, qd, kd, attn, cdt, "gdn_scan_fwd")
    (ob, lse), g_out = _hosted(P(_att_fwd, z, p["bmask"], "att_fwd"), gather(shard["w_out"]))
    wout = g_out.reshape(D, D)
    w1 = jnp.transpose(g_w1, (1, 0, 2)).reshape(D, DFF)
    w2 = g_w2.reshape(DFF, D)
    m = _merge_fwd(o, z, ob, p["gdn_norm"], "merge_fwd")
    x1, h2 = _mm_nn(m, wout, "out_proj", mode="resid_norm", res=x, gate=gt1, norm=(p["norm_mlp"], sc2, sh2))
    a, r = _mm_nn(h2, w1, "ff_in", mode="relu2")
    x2 = _mm_nn(r, w2, "ff_out", mode="resid", res=x1, gate=gt2)
    saved = dict(x=x, h=h, z=z, qn=qn, kn=kn, v=v, gb=gb, u=u, w=w, qd=qd, kd=kd, attn=attn,
                 tmat=tmat, cdt=cdt, o=o, vn=vn, st=st, ob=ob, lse=lse, m=m, x1=x1, h2=h2, a=a, r=r,
                 wout=wout, w1=w1, w2=w2)
    return x2, saved, g_in


def _backward_layer(dx2, mod_l, p, s, onehot):
    sh1, sc1, gt1, sh2, sc2, gt2 = [mod_l[k][None] for k in range(6)]
    P = functools.partial
    dw2, dgt2 = _mm_tn(s["r"], dx2, "ff_out_dw", gate=gt2, w=s["w2"])
    da, r_w2 = _hosted(P(_mm_nt, dx2, s["w2"], "ff_out_dx", gate=gt2, drelu=s["a"]), ("a2a", _rows_to_owners(dw2)))
    dw1 = _mm_tn(s["h2"], da, "ff_in_dw")
    dh2, r_w1 = _hosted(P(_mm_nt, da, s["w1"], "ff_in_dx"), ("a2a", _cols_to_owners(dw1)))
    dx1, dsc2, dsh2, dnmlp = _modnorm_bwd(dh2, s["x1"], p["norm_mlp"], sc2, sh2, dx2, "norm_mlp_bwd")
    dwout, dgt1 = _mm_tn(s["m"], dx1, "out_proj_dw", gate=gt1, w=s["wout"])
    dm, r_out = _hosted(P(_mm_nt, dx1, s["wout"], "out_proj_dx", gate=gt1), ("a2a", _rows_to_owners(dwout)))
    do, dzg, dob, dza, dzb, dgn = _merge_bwd(dm, s["o"], s["z"], s["ob"], p["gdn_norm"], "merge_bwd")
    dq_att, dk_att, dv_att, dbias = _att_bwd(s["z"], p["bmask"], s["ob"], s["lse"], dob, "att_bwd")
    drb = _bias_fold(dbias, onehot, "rel_bias_fold")[:, :513]
    dqd, dkd, dvn, dw, dattn, dcdt = _gdn_scan_bwd(do, s["w"], s["qd"], s["kd"], s["attn"], s["cdt"],
                                                   s["vn"], s["st"], "gdn_scan_bwd")
    dqn, dkn, dv, dgb = _gdn_intra_bwd(s["qn"], s["kn"], s["v"], s["gb"], s["u"], s["w"], s["tmat"],
                                       dqd, dkd, dvn, dw, dattn, dcdt, "gdn_intra_bwd")
    dzq, dzab, dcw, dvec = _gdn_prep_bwd(s["z"], dqn, dkn, dv, dgb, p["conv_w"], p["alog"], p["dtb"],
                                         "gdn_prep_bwd")
    dz = (dzq, dzg, dq_att, dk_att, dv_att, dza, dzb, dzab)
    dwz = _in_proj_dw(s["h"], dz, "in_proj_dw")
    dw_in = jnp.concatenate([dwz[:, :Z_ATT], dwz[:, Z_AB:Z_AB + 2 * NH], dwz[:, Z_ATT:Z_AB]], axis=1)
    dh, r_in = _hosted(P(_in_proj_dx, dz, p["wz"], "in_proj_dx"), ("a2a", _cols_to_owners(dw_in)))
    dx, dsc1, dsh1, dnmix = _modnorm_bwd(dh, s["x"], p["norm_mix"], sc1, sh1, dx1, "norm_mix_bwd")
    grads = dict(norm_mix=dnmix[0], norm_mlp=dnmlp[0], conv_w=dcw[:4], a_log=dvec[0, :NH], dt_bias=dvec[1, :NH],
                 gdn_norm=dgn[0], rel_bias=drb, mod=jnp.concatenate([dsh1, dsc1, dgt1, dsh2, dsc2, dgt2], axis=1)[0])
    return dx, grads, dict(w_in=r_in, w_out=r_out, w_ff_in=r_w1, w_ff_out=r_w2)


def _bias_onehot():
    return (jnp.asarray(_bias_index())[:, :, None] == jnp.arange(640)[None, None, :]).astype(F32)


def _layer_params(l, conv_full, norm_mix, norm_mlp, a_log, dt_bias, gdn_norm, rel_bias, onehot):
    pad = lambda a: jnp.pad(a, (0, 128 - NH))[None]
    fvec = _bias_vec(jnp.pad(rel_bias[l], ((0, 0), (0, 640 - rel_bias.shape[2]))), onehot, "rel_bias_vec")
    return dict(conv_w=conv_full[l], norm_mix=norm_mix[l][None], norm_mlp=norm_mlp[l][None], alog=pad(a_log[l]),
                dtb=pad(dt_bias[l]), gdn_norm=gdn_norm[l][None], bmask=_bias_mask(fvec, "rel_bias_mask"))


def _local_step(x, target, mod, small, shards, final_norm, onehot):
    L = len(small)
    g_in = _all_gather(shards[0]["w_in"], "gather_w_in")
    saved, params = [], []
    for l in range(L):
        params.append({**small[l], "wz": _z_weights(g_in)})
        x, sv, g_in = _forward_layer(x, mod[l].reshape(6, D), params[l], shards[l],
                                     shards[l + 1]["w_in"] if l + 1 < L else None)
        saved.append(sv)
    loss, dx, dfn = _loss_head(x, target, final_norm[None], "loss_head")
    grads, recv = [None] * L, [None] * L
    for l in reversed(range(L)):
        dx, grads[l], recv[l] = _backward_layer(dx, mod[l].reshape(6, D), params[l], saved[l], onehot)
    return loss, dx, grads, dfn[0], recv


SMALL = ("b_ada", "norm_mix", "norm_mlp", "a_log", "dt_bias", "gdn_norm", "rel_bias", "final_norm")


def kernel(x, c, w_ada, b_ada, norm_mix, norm_mlp, w_in, conv_w, a_log, dt_bias, gdn_norm, rel_bias, w_out, w_ff_in, w_ff_out, final_norm, loss_target, m_w_ada, m_b_ada, m_norm_mix, m_norm_mlp, m_w_in, m_conv_w, m_a_log, m_dt_bias, m_gdn_norm, m_rel_bias, m_w_out, m_w_ff_in, m_w_ff_out, m_final_norm, v_w_ada, v_b_ada, v_norm_mix, v_norm_mlp, v_w_in, v_conv_w, v_a_log, v_dt_bias, v_gdn_norm, v_rel_bias, v_w_out, v_w_ff_in, v_w_ff_out, v_final_norm):
    W = dict(w_ada=w_ada, b_ada=b_ada, norm_mix=norm_mix, norm_mlp=norm_mlp, w_in=w_in, conv_w=conv_w,
             a_log=a_log, dt_bias=dt_bias, gdn_norm=gdn_norm, rel_bias=rel_bias, w_out=w_out,
             w_ff_in=w_ff_in, w_ff_out=w_ff_out, final_norm=final_norm)
    Mo = dict(w_ada=m_w_ada, b_ada=m_b_ada, norm_mix=m_norm_mix, norm_mlp=m_norm_mlp, w_in=m_w_in,
              conv_w=m_conv_w, a_log=m_a_log, dt_bias=m_dt_bias, gdn_norm=m_gdn_norm, rel_bias=m_rel_bias,
              w_out=m_w_out, w_ff_in=m_w_ff_in, w_ff_out=m_w_ff_out, final_norm=m_final_norm)
    Vo = dict(w_ada=v_w_ada, b_ada=v_b_ada, norm_mix=v_norm_mix, norm_mlp=v_norm_mlp, w_in=v_w_in,
              conv_w=v_conv_w, a_log=v_a_log, dt_bias=v_dt_bias, gdn_norm=v_gdn_norm, rel_bias=v_rel_bias,
              w_out=v_w_out, w_ff_in=v_w_ff_in, w_ff_out=v_w_ff_out, final_norm=v_final_norm)
    L = w_in.shape[0]
    me = _flat(_mesh_pos())
    cshard = conv_w.shape[2]

    small_in = _all_gather(_pack_rows([c, conv_w]), "gather_c_conv")
    c_all = small_in[:, 0, :]
    conv_full = small_in.reshape(NDEV, -1)[:, D:D + L * 4 * cshard].reshape(NDEV, L, 4, cshard)
    conv_full = jnp.transpose(conv_full, (1, 2, 0, 3)).reshape(L, 4, NDEV * cshard)

    b_shard = lax.dynamic_slice_in_dim(b_ada, me * ADA_SHARD, ADA_SHARD, axis=1)
    mod_all = _all_gather(_ada_mod(c_all, w_ada, b_shard, "ada_mod"), "gather_mod")
    mod = lax.dynamic_index_in_dim(mod_all, me, axis=2, keepdims=False)
    mod = jnp.transpose(mod, (1, 0, 2)).reshape(L, 6 * D)

    onehot = _bias_onehot()
    small = [_layer_params(l, conv_full, norm_mix, norm_mlp, a_log, dt_bias, gdn_norm, rel_bias, onehot)
             for l in range(L)]
    shards = [{n: W[n][l].astype(MXU) for n in BIG} for l in range(L)]
    loss, dx, grads, dfn, recv = _local_step(x[0], loss_target[0], mod, small, shards, final_norm, onehot)

    def stack(name):
        return jnp.stack([g[name] for g in grads])

    small_names = ("mod", "norm_mix", "norm_mlp", "a_log", "dt_bias", "gdn_norm", "rel_bias")
    small_parts = [stack(n) for n in small_names] + [dfn, stack("conv_w"), loss[0, 0:1]]
    small_shapes = [a.shape for a in small_parts]
    gathered = _all_gather(_pack_rows(small_parts), "gather_small_grads")
    total = _unpack_rows(_sum_parts(gathered, "sum_small_grads"), small_shapes)
    tot = dict(zip(small_names + ("final_norm", "conv_w", "loss"), total))
    tot["b_ada"] = tot.pop("mod")
    tot["conv_w"] = lax.dynamic_slice_in_dim(tot["conv_w"], me * cshard, cshard, axis=2)

    out_g, out_d, out_m, out_v = {}, {}, {}, {}
    names = SMALL + ("conv_w",)
    shapes = [W[n].shape for n in names]
    packed = [_pack_rows([src[n] for n in names])[None] if src is tot else _pack_rows([src[n] for n in names])
              for src in (tot, W, Mo, Vo)]
    res = _adamw_reduce(*packed, "adamw_small", tr=8)
    for dst, arr in zip((out_g, out_d, out_m, out_v), res):
        dst.update(zip(names, _unpack_rows(arr, shapes)))

    dmod_all = gathered.reshape(NDEV, -1)[:, :L * 6 * D].reshape(NDEV, L, 6 * D)
    dmod_mine = jnp.transpose(lax.dynamic_slice_in_dim(dmod_all, me * ADA_SHARD, ADA_SHARD, axis=2), (1, 0, 2))
    res = _wada_adamw(jnp.transpose(c_all), dmod_mine, w_ada, m_w_ada, v_w_ada, "adamw_w_ada")
    for dst, arr in zip((out_g, out_d, out_m, out_v), res):
        dst["w_ada"] = arr

    for name, tr in (("w_in", 256), ("w_out", 128), ("w_ff_in", 256), ("w_ff_out", 256)):
        sh = W[name].shape
        rows = int(np.prod(sh[:-1]))
        flat = lambda a: a.reshape(rows, sh[-1])
        parts = jnp.stack([recv[l][name] for l in range(L)], axis=1).reshape(NDEV, rows, sh[-1])
        res = _adamw_reduce(parts, flat(W[name]), flat(Mo[name]), flat(Vo[name]), "adamw_" + name, tr=tr)
        for dst, arr in zip((out_g, out_d, out_m, out_v), res):
            dst[name] = arr.reshape(sh)

    order = ("w_ada", "b_ada", "norm_mix", "norm_mlp", "w_in", "conv_w", "a_log", "dt_bias", "gdn_norm",
             "rel_bias", "w_out", "w_ff_in", "w_ff_out", "final_norm")
    return (tot["loss"].reshape(()), dx[None], *[out_g[n] for n in order], *[out_d[n] for n in order],
            *[out_m[n] for n in order], *[out_v[n] for n in order])
```

```python
import functools
import math

import numpy as np
import jax
import jax.numpy as jnp
from jax import lax
from jax.experimental import pallas as pl
from jax.experimental.pallas import tpu as pltpu

F32 = jnp.float32
MXU = jnp.bfloat16
HI = lax.Precision.HIGHEST
MESH_ID = pl.DeviceIdType.MESH

D = 1024
NH = 8
HD = 128
CH = 64
PAST = 8
DFF = 4096
EPS = 1e-6
NDEV = 8
DEPTH = 4
IN_W = 9232
ZW = 9728
Z_GATE, Z_ATT, Z_BR, Z_AB = 3072, 4096, 7168, 9216
QB = 256
MASKED = -1e30
KWIN = 768
FW = 1024
ADAM_LR, ADAM_B1, ADAM_B2, ADAM_EPS, ADAM_WD, ADAM_STEP = 0.001, 0.9, 0.999, 1e-08, 0.01, 10


def _dot(a, b, prec=None):
    return jnp.dot(a, b, preferred_element_type=F32, precision=prec)


def _dot_nt(a, b, prec=None):
    return lax.dot_general(a, b, (((1,), (1,)), ((), ())), preferred_element_type=F32, precision=prec)


def _dot_tn(a, b, prec=None):
    return lax.dot_general(a, b, (((0,), (0,)), ((), ())), preferred_element_type=F32, precision=prec)


def _mx(a):
    return a.astype(MXU)


def _sigmoid(x):
    return 1.0 / (1.0 + jnp.exp(-x))


def _softplus(x):
    return jnp.maximum(x, 0.0) + jnp.log(1.0 + jnp.exp(-jnp.abs(x)))


def _rowsum(x):
    return jnp.sum(x, axis=1, keepdims=True)


def _colsum(x):
    return jnp.sum(x, axis=0, keepdims=True)


def _call(body, name, grid, in_specs, out_specs, out_shape, scratch=(), comm=None):
    if comm is None:
        return pl.pallas_call(body, name=name, grid=grid, in_specs=in_specs, out_specs=out_specs,
                              out_shape=out_shape, scratch_shapes=list(scratch))
    kind, x = comm
    single = not isinstance(out_specs, (list, tuple))
    o_specs = [out_specs] if single else list(out_specs)
    o_shape = [out_shape] if single else list(out_shape)
    n_in, n_out, n_scr = len(in_specs), len(o_specs), len(scratch)
    c_shape = (NDEV,) + x.shape if kind == "gather" else x.shape

    def wrapped(*refs):
        ins, x_ref = refs[:n_in], refs[n_in]
        outs, c_ref = refs[n_in + 1:n_in + 1 + n_out], refs[n_in + 1 + n_out]
        scr = refs[n_in + 2 + n_out:n_in + 2 + n_out + n_scr]
        sems = refs[n_in + 2 + n_out + n_scr:]
        first = functools.reduce(jnp.logical_and, [pl.program_id(a) == 0 for a in range(len(grid))])
        last = functools.reduce(jnp.logical_and, [pl.program_id(a) == grid[a] - 1 for a in range(len(grid))])

        @pl.when(first)
        def _():
            _comm_start(*_comm_copies(kind, x_ref, c_ref, *sems))

        body(*ins, *outs, *scr)

        @pl.when(last)
        def _():
            _comm_wait(*_comm_copies(kind, x_ref, c_ref, *sems))

    any_spec = pl.BlockSpec(memory_space=pl.ANY)
    call = pl.pallas_call(
        wrapped, name=name, grid=grid, in_specs=list(in_specs) + [any_spec], out_specs=o_specs + [any_spec],
        out_shape=o_shape + [jax.ShapeDtypeStruct(c_shape, x.dtype)],
        scratch_shapes=list(scratch) + _comm_sems())

    def run(*args):
        res = call(*args, x)
        return (res[0] if single else list(res[:-1])), res[-1]

    return run


def _hosted(fn, comm):
    return (fn(), None) if comm is None else fn(comm=comm)


def _full(shape):
    n = len(shape)
    return pl.BlockSpec(shape, lambda *_: (0,) * n)


def _mesh_pos():
    return lax.axis_index("x"), lax.axis_index("y"), lax.axis_index("c")


def _peer(pos, k):
    x, y, c = pos
    return (x ^ ((k >> 2) & 1), y ^ ((k >> 1) & 1), c ^ (k & 1))


def _flat(pos):
    return 4 * pos[0] + 2 * pos[1] + pos[2]


def _comm_sems():
    return [pltpu.SemaphoreType.DMA((NDEV - 1,)), pltpu.SemaphoreType.DMA((NDEV - 1,)), pltpu.SemaphoreType.DMA]


def _comm_copies(kind, x_ref, out_ref, send_sems, recv_sems, local_sem):
    pos = _mesh_pos()
    me = _flat(pos)
    src = (lambda d: x_ref) if kind == "gather" else (lambda d: x_ref.at[d])
    mine = pltpu.make_async_copy(src(me), out_ref.at[me], local_sem)
    sends, recvs = [], []
    for k in range(1, NDEV):
        peer = _peer(pos, k)
        pid = _flat(peer)
        sems = dict(send_sem=send_sems.at[k - 1], recv_sem=recv_sems.at[k - 1], device_id=peer,
                    device_id_type=MESH_ID)
        sends.append(pltpu.make_async_remote_copy(src_ref=src(pid), dst_ref=out_ref.at[me], **sems))
        recvs.append(pltpu.make_async_remote_copy(src_ref=src(pid), dst_ref=out_ref.at[pid], **sems))
    return mine, sends, recvs


def _comm_start(mine, sends, recvs):
    mine.start()
    for cp in sends:
        cp.start()


def _comm_wait(mine, sends, recvs):
    for cp in recvs:
        cp.wait_recv()
    for cp in sends:
        cp.wait_send()
    mine.wait()


def _collective(kind, x, name):
    def body(x_ref, out_ref, *sems):
        copies = _comm_copies(kind, x_ref, out_ref, *sems)
        _comm_start(*copies)
        _comm_wait(*copies)

    shape = (NDEV,) + x.shape if kind == "gather" else x.shape
    return pl.pallas_call(
        body, name=name, out_shape=jax.ShapeDtypeStruct(shape, x.dtype),
        in_specs=[pl.BlockSpec(memory_space=pl.ANY)], out_specs=pl.BlockSpec(memory_space=pl.ANY),
        scratch_shapes=_comm_sems())(x)


def _all_gather(x, name):
    return _collective("gather", x, name)


def _mm_nn(a, w, name, *, mode="plain", res=None, gate=None, norm=None, tm=1024, tn=1024, tk=1024, comm=None):
    M, K = a.shape
    N = w.shape[1]
    tm, tk, tn = min(tm, M), min(tk, K), min(tn, N)
    nk = K // tk

    def body(*refs):
        refs = list(refs)
        acc = refs.pop() if nk > 1 else None
        if mode == "resid":
            a_ref, w_ref, res_ref, gate_ref, o_ref = refs
        elif mode == "resid_norm":
            a_ref, w_ref, res_ref, gate_ref, g_ref, sc_ref, sh_ref, o_ref, r_ref = refs
        elif mode == "relu2":
            a_ref, w_ref, o_ref, r_ref = refs
        else:
            a_ref, w_ref, o_ref = refs
        k = pl.program_id(2)
        part = _dot(_mx(a_ref[...]), w_ref[...])

        def finish(r):
            if mode == "resid":
                o_ref[...] = res_ref[...] + gate_ref[...] * r
            elif mode == "resid_norm":
                xv = res_ref[...] + gate_ref[...] * r
                o_ref[...] = xv
                rs = lax.rsqrt(jnp.mean(xv * xv, axis=1, keepdims=True) + EPS)
                r_ref[...] = ((xv * rs * g_ref[...]) * (1.0 + sc_ref[...]) + sh_ref[...]).astype(r_ref.dtype)
            elif mode == "relu2":
                o_ref[...] = r
                r_ref[...] = jnp.square(jnp.maximum(r, 0.0)).astype(r_ref.dtype)
            else:
                o_ref[...] = r

        if nk == 1:
            finish(part)
        else:
            @pl.when(k == 0)
            def _():
                acc[...] = part

            @pl.when((k > 0) & (k < nk - 1))
            def _():
                acc[...] += part

            @pl.when(k == nk - 1)
            def _():
                finish(acc[...] + part)

    in_specs = [pl.BlockSpec((tm, tk), lambda i, j, k: (i, k)),
                pl.BlockSpec((tk, tn), lambda i, j, k: (k, j))]
    args = [a, w]
    o_spec = pl.BlockSpec((tm, tn), lambda i, j, k: (i, j))
    out_specs, out_shape = o_spec, jax.ShapeDtypeStruct((M, N), F32)
    vec = pl.BlockSpec((1, tn), lambda i, j, k: (0, j))
    if mode == "resid":
        in_specs += [o_spec, vec]
        args += [res, gate]
    elif mode == "resid_norm":
        assert tn == N
        in_specs += [o_spec, vec, vec, vec, vec]
        args += [res, gate, *norm]
        out_specs = [o_spec, o_spec]
        out_shape = [jax.ShapeDtypeStruct((M, N), F32), jax.ShapeDtypeStruct((M, N), MXU)]
    elif mode == "relu2":
        out_specs = [o_spec, o_spec]
        out_shape = [jax.ShapeDtypeStruct((M, N), F32), jax.ShapeDtypeStruct((M, N), MXU)]
    return _call(body, name, (M // tm, N // tn, nk), in_specs, out_specs, out_shape,
                 [pltpu.VMEM((tm, tn), F32)] if nk > 1 else [], comm=comm)(*args)


def _mm_nt(a, w, name, *, gate=None, drelu=None, tm=1024, tko=1024, tn=1024, comm=None):
    M, N = a.shape
    K = w.shape[0]
    tm, tn, tko = min(tm, M), min(tn, N), min(tko, K)
    nn = N // tn

    def body(*refs):
        refs = list(refs)
        acc = refs.pop() if nn > 1 else None
        a_ref, w_ref = refs[:2]
        rest = refs[2:]
        gate_ref = rest.pop(0) if gate is not None else None
        pre_ref = rest.pop(0) if drelu is not None else None
        (o_ref,) = rest
        n = pl.program_id(2)
        av = a_ref[...]
        if gate_ref is not None:
            av = av * gate_ref[...]
        part = _dot_nt(_mx(av), w_ref[...])

        def finish(r):
            if pre_ref is not None:
                r = r * (2.0 * jnp.maximum(pre_ref[...], 0.0))
            o_ref[...] = r.astype(o_ref.dtype)

        if nn == 1:
            finish(part)
        else:
            @pl.when(n == 0)
            def _():
                acc[...] = part

            @pl.when((n > 0) & (n < nn - 1))
            def _():
                acc[...] += part

            @pl.when(n == nn - 1)
            def _():
                finish(acc[...] + part)

    in_specs = [pl.BlockSpec((tm, tn), lambda i, j, n: (i, n)),
                pl.BlockSpec((tko, tn), lambda i, j, n: (j, n))]
    args = [a, w]
    if gate is not None:
        in_specs.append(pl.BlockSpec((1, tn), lambda i, j, n: (0, n)))
        args.append(gate)
    o_spec = pl.BlockSpec((tm, tko), lambda i, j, n: (i, j))
    if drelu is not None:
        in_specs.append(o_spec)
        args.append(drelu)
    out_dtype = MXU if drelu is not None else F32
    return _call(body, name, (M // tm, K // tko, nn), in_specs, o_spec,
                 jax.ShapeDtypeStruct((M, K), out_dtype), [pltpu.VMEM((tm, tko), F32)] if nn > 1 else [],
                 comm=comm)(*args)


def _mm_tn(a, b, name, *, gate=None, w=None, tk=1024, tn=1024, tm=1024, comm=None):
    M, K = a.shape
    N = b.shape[1]
    tm, tk, tn = min(tm, M), min(tk, K), min(tn, N)
    nm = M // tm
    gated = gate is not None

    def body(*refs):
        if gated:
            a_ref, b_ref, gate_ref, w_ref, o_ref, dg_ref, acc = refs
        else:
            a_ref, b_ref, o_ref, acc = refs
        kk = pl.program_id(1)
        m = pl.program_id(2)
        part = _dot_tn(_mx(a_ref[...]), _mx(b_ref[...]))

        @pl.when((m == 0) & (nm > 1))
        def _():
            acc[...] = part

        @pl.when((m > 0) & (m < nm - 1))
        def _():
            acc[...] += part

        if gated:
            @pl.when((m == 0) & (kk == 0))
            def _():
                dg_ref[...] = jnp.zeros_like(dg_ref)

        @pl.when(m == nm - 1)
        def _():
            r = acc[...] + part if nm > 1 else part
            if gated:
                o_ref[...] = r * gate_ref[...]
                dg_ref[...] += _colsum(r * w_ref[...].astype(F32))
            else:
                o_ref[...] = r

    in_specs = [pl.BlockSpec((tm, tk), lambda j, k, m: (m, k)),
                pl.BlockSpec((tm, tn), lambda j, k, m: (m, j))]
    args = [a, b]
    o_spec = pl.BlockSpec((tk, tn), lambda j, k, m: (k, j))
    out_specs, out_shape = o_spec, jax.ShapeDtypeStruct((K, N), F32)
    if gated:
        in_specs += [pl.BlockSpec((1, tn), lambda j, k, m: (0, j)), o_spec]
        args += [gate, w]
        out_specs = [o_spec, pl.BlockSpec((1, tn), lambda j, k, m: (0, j))]
        out_shape = [out_shape, jax.ShapeDtypeStruct((1, N), F32)]
    return _call(body, name, (N // tn, K // tk, nm), in_specs, out_specs, out_shape,
                 [pltpu.VMEM((tk, tn), F32)], comm=comm)(*args)


SEG_T = 512


def _seg_layout(segs):
    starts, t = [], 0
    for a in segs:
        starts.append(t)
        t += a.shape[1] // SEG_T
    return starts, t


def _seg_spec(tm, lo, hi, row_axis, col_axis):
    def index(*ids):
        col = ids[col_axis]
        act = (col >= lo) & (col < hi)
        return jnp.where(act, ids[row_axis], 0), jnp.where(act, col - lo, 0)

    return pl.BlockSpec((tm, SEG_T), index)


def _in_proj_dw(h, segs, name, tm=1024):
    S = h.shape[0]
    tm = min(tm, S)
    nm = S // tm
    starts, ntile = _seg_layout(segs)
    bounds = [(lo, lo + a.shape[1] // SEG_T) for lo, a in zip(starts, segs)]

    def body(*refs):
        h_ref, seg_refs, o_ref, acc = refs[0], refs[1:1 + len(segs)], refs[-2], refs[-1]
        j = pl.program_id(0)
        m = pl.program_id(1)
        for (lo, hi), b_ref in zip(bounds, seg_refs):
            @pl.when((j >= lo) & (j < hi))
            def _():
                part = _dot_tn(h_ref[...], _mx(b_ref[...]))
                if nm == 1:
                    o_ref[...] = part
                else:
                    @pl.when(m == 0)
                    def _():
                        acc[...] = part

                    @pl.when((m > 0) & (m < nm - 1))
                    def _():
                        acc[...] += part

                    @pl.when(m == nm - 1)
                    def _():
                        o_ref[...] = acc[...] + part

    return _call(
        body, name, (ntile, nm),
        [pl.BlockSpec((tm, D), lambda j, m: (m, 0))] + [_seg_spec(tm, lo, hi, 1, 0) for lo, hi in bounds],
        pl.BlockSpec((D, SEG_T), lambda j, m: (0, j)), jax.ShapeDtypeStruct((D, ntile * SEG_T), F32),
        [pltpu.VMEM((D, SEG_T), F32)])(h, *segs)


def _in_proj_dx(segs, w, name, tm=1024, comm=None):
    S = segs[0].shape[0]
    tm = min(tm, S)
    starts, ntile = _seg_layout(segs)
    bounds = [(lo, lo + a.shape[1] // SEG_T) for lo, a in zip(starts, segs)]

    def body(*refs):
        seg_refs, w_ref, o_ref, acc = refs[:len(segs)], refs[-3], refs[-2], refs[-1]
        n = pl.program_id(1)
        for (lo, hi), a_ref in zip(bounds, seg_refs):
            @pl.when((n >= lo) & (n < hi))
            def _():
                part = _dot_nt(_mx(a_ref[...]), w_ref[...])

                @pl.when(n == 0)
                def _():
                    acc[...] = part

                @pl.when((n > 0) & (n < ntile - 1))
                def _():
                    acc[...] += part

                @pl.when(n == ntile - 1)
                def _():
                    o_ref[...] = acc[...] + part

    return _call(
        body, name, (S // tm, ntile),
        [_seg_spec(tm, lo, hi, 0, 1) for lo, hi in bounds] + [pl.BlockSpec((D, SEG_T), lambda i, n: (0, n))],
        pl.BlockSpec((tm, D), lambda i, n: (i, 0)), jax.ShapeDtypeStruct((S, D), F32),
        [pltpu.VMEM((tm, D), F32)], comm=comm)(*segs, w)


def _modnorm_fwd(x, gain, sc, sh, name, ts=512):
    S = x.shape[0]

    def body(x_ref, g_ref, sc_ref, sh_ref, h_ref):
        xv = x_ref[...]
        r = lax.rsqrt(jnp.mean(xv * xv, axis=1, keepdims=True) + EPS)
        h_ref[...] = ((xv * r * g_ref[...]) * (1.0 + sc_ref[...]) + sh_ref[...]).astype(h_ref.dtype)

    row = pl.BlockSpec((ts, D), lambda i: (i, 0))
    vec = pl.BlockSpec((1, D), lambda i: (0, 0))
    return _call(body, name, (S // ts,), [row, vec, vec, vec], row,
                 jax.ShapeDtypeStruct((S, D), MXU))(x, gain, sc, sh)


def _modnorm_bwd(dh, x, gain, sc, sh, dx_in, name, ts=512):
    S = x.shape[0]

    def body(dh_ref, x_ref, g_ref, sc_ref, sh_ref, dxin_ref, dx_ref, dsc_ref, dsh_ref, dg_ref):
        i = pl.program_id(0)
        xv = x_ref[...]
        dhv = dh_ref[...]
        g = g_ref[...]
        r = lax.rsqrt(jnp.mean(xv * xv, axis=1, keepdims=True) + EPS)
        xr = xv * r
        dn = dhv * (1.0 + sc_ref[...])
        u = dn * g
        dx_ref[...] = dxin_ref[...] + r * (u - xr * jnp.mean(xr * u, axis=1, keepdims=True))

        @pl.when(i == 0)
        def _():
            dsc_ref[...] = jnp.zeros_like(dsc_ref)
            dsh_ref[...] = jnp.zeros_like(dsh_ref)
            dg_ref[...] = jnp.zeros_like(dg_ref)

        dsc_ref[...] += _colsum(dhv * (xr * g))
        dsh_ref[...] += _colsum(dhv)
        dg_ref[...] += _colsum(dn * xr)

    row = pl.BlockSpec((ts, D), lambda i: (i, 0))
    vec = pl.BlockSpec((1, D), lambda i: (0, 0))
    vshape = jax.ShapeDtypeStruct((1, D), F32)
    return _call(body, name, (S // ts,), [row, row, vec, vec, vec, row], [row, vec, vec, vec],
                 [jax.ShapeDtypeStruct((S, D), F32), vshape, vshape, vshape])(dh, x, gain, sc, sh, dx_in)


def _loss_head(x, target, gain, name, ts=512):
    S = x.shape[0]

    def body(x_ref, t_ref, g_ref, loss_ref, dx_ref, dg_ref):
        i = pl.program_id(0)
        xv = x_ref[...]
        g = g_ref[...]
        r = lax.rsqrt(jnp.mean(xv * xv, axis=1, keepdims=True) + EPS)
        xr = xv * r
        e = xr * g - t_ref[...]
        dy = e * (1.0 / D)
        u = dy * g
        dx_ref[...] = r * (u - xr * jnp.mean(xr * u, axis=1, keepdims=True))

        @pl.when(i == 0)
        def _():
            loss_ref[...] = jnp.zeros_like(loss_ref)
            dg_ref[...] = jnp.zeros_like(dg_ref)

        part = 0.5 * jnp.sum(jnp.mean(e * e, axis=1, keepdims=True), axis=0, keepdims=True)
        loss_ref[...] += jnp.broadcast_to(part, loss_ref.shape)
        dg_ref[...] += _colsum(dy * xr)

    row = pl.BlockSpec((ts, D), lambda i: (i, 0))
    vec = pl.BlockSpec((1, D), lambda i: (0, 0))
    return _call(body, name, (S // ts,), [row, row, vec],
                 [pl.BlockSpec((1, 128), lambda i: (0, 0)), row, vec],
                 [jax.ShapeDtypeStruct((1, 128), F32), jax.ShapeDtypeStruct((S, D), F32),
                  jax.ShapeDtypeStruct((1, D), F32)])(x, target, gain)


def _merge_specs(ts):
    o_spec = pl.BlockSpec((NH, ts, HD), lambda i: (0, i, 0))
    zg = pl.BlockSpec((ts, D), lambda i: (i, Z_GATE // D))
    za = pl.BlockSpec((ts, D), lambda i: (i, Z_BR // D))
    zb = pl.BlockSpec((ts, D), lambda i: (i, Z_BR // D + 1))
    row = pl.BlockSpec((ts, D), lambda i: (i, 0))
    gn = pl.BlockSpec((1, HD), lambda i: (0, 0))
    return o_spec, zg, za, zb, row, gn


def _merge_fwd(o, z, ob, gn, name, ts=256):
    S = ob.shape[0]

    def body(o_ref, zg_ref, za_ref, zb_ref, ob_ref, gn_ref, m_ref):
        for h in range(NH):
            sl = slice(h * HD, (h + 1) * HD)
            oh = o_ref[h]
            r = lax.rsqrt(jnp.mean(oh * oh, axis=1, keepdims=True) + EPS)
            gate = zg_ref[:, sl]
            oa = (oh * r * gn_ref[...]) * (gate * _sigmoid(gate))
            m = _sigmoid(za_ref[:, sl]) * oa + _sigmoid(zb_ref[:, sl]) * ob_ref[:, sl]
            m_ref[:, sl] = m.astype(m_ref.dtype)

    o_spec, zg, za, zb, row, gns = _merge_specs(ts)
    return _call(body, name, (S // ts,), [o_spec, zg, za, zb, row, gns], row,
                 jax.ShapeDtypeStruct((S, D), MXU))(o, z, z, z, ob, gn)


def _merge_bwd(dm, o, z, ob, gn, name, ts=256):
    S = ob.shape[0]

    def body(dm_ref, o_ref, zg_ref, za_ref, zb_ref, ob_ref, gn_ref,
             do_ref, dzg_ref, dob_ref, dza_ref, dzb_ref, dgn_ref):
        i = pl.program_id(0)
        gn_v = gn_ref[...]
        dgn = jnp.zeros((1, HD), F32)
        for h in range(NH):
            sl = slice(h * HD, (h + 1) * HD)
            dmh = dm_ref[:, sl]
            oh = o_ref[h]
            r = lax.rsqrt(jnp.mean(oh * oh, axis=1, keepdims=True) + EPS)
            ohr = oh * r
            on = ohr * gn_v
            gate = zg_ref[:, sl]
            sg = _sigmoid(gate)
            silu = gate * sg
            oa = on * silu
            ga = _sigmoid(za_ref[:, sl])
            gb = _sigmoid(zb_ref[:, sl])
            obh = ob_ref[:, sl]
            doa = dmh * ga
            dob_ref[:, sl] = dmh * gb
            dza_ref[:, sl] = (dmh * oa * ga * (1.0 - ga)).astype(dza_ref.dtype)
            dzb_ref[:, sl] = (dmh * obh * gb * (1.0 - gb)).astype(dzb_ref.dtype)
            don = doa * silu
            dzg_ref[:, sl] = (doa * on * (sg * (1.0 + gate * (1.0 - sg)))).astype(dzg_ref.dtype)
            dgn = dgn + _colsum(don * ohr)
            u = don * gn_v
            do_ref[h] = r * (u - ohr * jnp.mean(ohr * u, axis=1, keepdims=True))

        @pl.when(i == 0)
        def _():
            dgn_ref[...] = jnp.zeros_like(dgn_ref)

        dgn_ref[...] += dgn

    o_spec, zg, za, zb, row, gns = _merge_specs(ts)
    return _call(
        body, name, (S // ts,), [row, o_spec, zg, za, zb, row, gns],
        [o_spec, row, row, row, row, gns],
        [jax.ShapeDtypeStruct((NH, S, HD), F32), jax.ShapeDtypeStruct((S, D), MXU),
         jax.ShapeDtypeStruct((S, D), F32), jax.ShapeDtypeStruct((S, D), MXU),
         jax.ShapeDtypeStruct((S, D), MXU), jax.ShapeDtypeStruct((1, HD), F32)],
    )(dm, o, z, z, z, ob, gn)


GROWS = 256
GCH = GROWS // CH


def _gdn_prep_fwd(z, conv_w, alog_row, dtb_row, name, comm=None):
    S = z.shape[0]
    ts = GROWS
    scale = HD ** -0.5

    def body(z_ref, halo_ref, zab_ref, w_ref, al_ref, dt_ref, q_ref, k_ref, v_ref, gb_ref, buf):
        i = pl.program_id(0)
        buf[0:8, :] = jnp.where(i == 0, 0.0, halo_ref[...])
        buf[8:8 + ts, :] = z_ref[...]
        outs = (q_ref, k_ref, v_ref)
        for seg in range(3):
            cs = slice(seg * D, (seg + 1) * D)
            c = jnp.zeros((ts, D), F32)
            for j in range(4):
                c = c + w_ref[j:j + 1, cs] * buf[pl.ds(5 + j, ts), cs]
            s = c * _sigmoid(c)
            if seg == 2:
                outs[seg][...] = s
            else:
                mul = scale if seg == 0 else 1.0
                for h in range(NH):
                    sl = slice(h * HD, (h + 1) * HD)
                    sh = s[:, sl]
                    r = lax.rsqrt(_rowsum(sh * sh) + EPS)
                    outs[seg][:, sl] = sh * (r * mul)
        zab = zab_ref[...]
        lane = lax.broadcasted_iota(jnp.int32, zab.shape, 1)
        g = -jnp.exp(al_ref[...]) * _softplus(zab + dt_ref[...])
        ri = lax.broadcasted_iota(jnp.int32, (CH, CH), 0)
        ci = lax.broadcasted_iota(jnp.int32, (CH, CH), 1)
        incl = (ri >= ci).astype(F32)
        gcum = jnp.concatenate([_dot(incl, g[c * CH:(c + 1) * CH], HI) for c in range(ts // CH)], axis=0)
        gb_ref[...] = jnp.where(lane < NH, gcum, jnp.where(lane < 2 * NH, _sigmoid(zab), 0.0))

    row = pl.BlockSpec((ts, D), lambda i: (i, 0))
    vec = pl.BlockSpec((1, 128), lambda i: (0, 0))
    return _call(
        body, name, (S // ts,),
        [pl.BlockSpec((ts, 3 * D), lambda i: (i, 0)),
         pl.BlockSpec((8, 3 * D), lambda i: (jnp.maximum(i * (ts // 8) - 1, 0), 0)),
         pl.BlockSpec((ts, 128), lambda i: (i, Z_AB // 128)),
         _full((4, 3 * D)), vec, vec],
        [row, row, row, pl.BlockSpec((ts, 128), lambda i: (i, 0))],
        [jax.ShapeDtypeStruct((S, D), F32)] * 3 + [jax.ShapeDtypeStruct((S, 128), F32)],
        [pltpu.VMEM((ts + 8, 3 * D), F32)], comm=comm,
    )(z, z, z, conv_w, alog_row, dtb_row)


def _split(a):
    hi = a.astype(MXU)
    return hi, (a - hi.astype(F32)).astype(MXU)


def _dot3(a, b, dot=_dot):
    ah, al = _split(a)
    bh, bl = _split(b)
    return dot(ah, bh) + (dot(ah, bl) + dot(al, bh))


IROWS = 512
ICH = IROWS // CH


def _chunk_common(gbk, h, k):
    lane = lax.broadcasted_iota(jnp.int32, gbk.shape, 1)
    G = _rowsum(jnp.where(lane == h, gbk, 0.0))
    b_col = _rowsum(jnp.where(lane == h + NH, gbk, 0.0))
    ri = lax.broadcasted_iota(jnp.int32, (CH, CH), 0)
    ci = lax.broadcasted_iota(jnp.int32, (CH, CH), 1)
    incl = ri >= ci
    gc = jnp.broadcast_to(G, (CH, CH))
    decay = jnp.where(incl, jnp.exp(jnp.where(incl, gc - gc.T, 0.0)), 0.0)
    Gl = G[CH - 1:CH, :]
    kb = k * b_col
    return dict(b=b_col, ri=ri, ci=ci, incl=incl, strict=ri > ci, decay=decay, eG=jnp.exp(G),
                e2=jnp.exp(Gl - G), cd=jnp.exp(Gl), kb=kb, kk=_dot_nt(_mx(kb), _mx(k)))


def _gdn_intra_fwd(qn, kn, v, gb, name, comm=None):
    S = qn.shape[0]

    def body(q_ref, k_ref, v_ref, gb_ref, u_ref, w_ref, qd_ref, kd_ref, at_ref, t_ref, cd_ref):
        h = pl.program_id(1)
        rows = [slice(c * CH, (c + 1) * CH) for c in range(ICH)]
        ks = [k_ref[r, :] for r in rows]
        cms = [_chunk_common(gb_ref[r, :], h, k) for r, k in zip(rows, ks)]
        ps = [jnp.where(cm["strict"], cm["kk"] * cm["decay"], 0.0) for cm in cms]
        ts = [(cm["ri"] == cm["ci"]).astype(F32) - p for cm, p in zip(cms, ps)]
        for _ in range(5):
            ps = [_dot3(p, p) for p in ps]
            ts = [t + _dot3(t, p) for t, p in zip(ts, ps)]
        for c, (r, k, cm, t) in enumerate(zip(rows, ks, cms, ts)):
            rhs = jnp.concatenate([v_ref[r, :] * cm["b"], k * (cm["b"] * cm["eG"])], axis=1)
            sol = _dot3(t, rhs)
            u_ref[0, r, :] = sol[:, :HD]
            w_ref[0, r, :] = sol[:, HD:]
            t_ref[0, r, :] = t
        for c, (r, k, cm) in enumerate(zip(rows, ks, cms)):
            q = q_ref[r, :]
            qk = _dot_nt(_mx(q), _mx(k))
            at_ref[0, r, :] = jnp.where(cm["incl"], qk * cm["decay"], 0.0)
            qd_ref[0, r, :] = q * cm["eG"]
            kd_ref[0, r, :] = k * cm["e2"]
            cd_ref[0, c] = jnp.broadcast_to(cm["cd"], (8, 128))

    tok = pl.BlockSpec((IROWS, HD), lambda i, h: (i, h))
    hm = pl.BlockSpec((1, IROWS, HD), lambda i, h: (h, i, 0))
    hm64 = pl.BlockSpec((1, IROWS, CH), lambda i, h: (h, i, 0))
    big = jax.ShapeDtypeStruct((NH, S, HD), F32)
    sm = jax.ShapeDtypeStruct((NH, S, CH), F32)
    return _call(
        body, name, (S // IROWS, NH),
        [tok, tok, tok, pl.BlockSpec((IROWS, 128), lambda i, h: (i, 0))],
        [hm, hm, hm, hm, hm64, hm64, pl.BlockSpec((1, ICH, 8, 128), lambda i, h: (h, i, 0, 0))],
        [big, big, big, big, sm, sm, jax.ShapeDtypeStruct((NH, S // CH, 8, 128), F32)], comm=comm,
    )(qn, kn, v, gb)


def _scale_state(s, cd_tile):
    return (s.reshape(HD // 8, 8, HD) * cd_tile[None]).reshape(HD, HD)


def _gdn_scan_fwd(u, w, qd, kd, attn, cdt, name):
    S = u.shape[1]
    nblk = S // GROWS

    def body(u_ref, w_ref, qd_ref, kd_ref, at_ref, cd_ref, o_ref, vn_ref, st_ref, s_ref):
        i = pl.program_id(0)

        @pl.when(i == 0)
        def _():
            s_ref[...] = jnp.zeros_like(s_ref)

        def chunk(c, carry):
            r0 = pl.multiple_of(c * CH, CH)
            rows = pl.ds(r0, CH)
            H = range(NH)
            shs = [s_ref[h] for h in H]
            sbs = [_mx(sh) for sh in shs]
            ws = [_dot(_mx(w_ref[h, rows, :]), sbs[h]) for h in H]
            qs = [_dot(_mx(qd_ref[h, rows, :]), sbs[h]) for h in H]
            vns = [u_ref[h, rows, :] - ws[h] for h in H]
            vbs = [_mx(vn) for vn in vns]
            avs = [_dot(_mx(at_ref[h, rows, :]), vbs[h]) for h in H]
            kvs = [_dot_tn(_mx(kd_ref[h, rows, :]), vbs[h]) for h in H]
            for h in H:
                st_ref[h, c] = shs[h]
                vn_ref[h, rows, :] = vns[h]
                o_ref[h, rows, :] = qs[h] + avs[h]
                s_ref[h] = _scale_state(shs[h], cd_ref[h, c]) + kvs[h]
            return carry

        lax.fori_loop(0, GCH, chunk, 0)

    hm = pl.BlockSpec((NH, GROWS, HD), lambda i: (0, i, 0))
    hm64 = pl.BlockSpec((NH, GROWS, CH), lambda i: (0, i, 0))
    big = jax.ShapeDtypeStruct((NH, S, HD), F32)
    return _call(
        body, name, (nblk,),
        [hm, hm, hm, hm, hm64, pl.BlockSpec((NH, GCH, 8, 128), lambda i: (0, i, 0, 0))],
        [hm, hm, pl.BlockSpec((NH, GCH, HD, HD), lambda i: (0, i, 0, 0))],
        [big, big, jax.ShapeDtypeStruct((NH, S // CH, HD, HD), F32)],
        [pltpu.VMEM((NH, HD, HD), F32)],
    )(u, w, qd, kd, attn, cdt)


def _gdn_scan_bwd(do, w, qd, kd, attn, cdt, vn, st, name):
    S = do.shape[1]
    nblk = S // GROWS

    def body(do_ref, w_ref, qd_ref, kd_ref, at_ref, cd_ref, vn_ref, st_ref,
             dqd_ref, dkd_ref, dvn_ref, dw_ref, dat_ref, dcd_ref, ds_ref):
        i = pl.program_id(0)

        @pl.when(i == 0)
        def _():
            ds_ref[...] = jnp.zeros_like(ds_ref)

        def chunk(cc, carry):
            c = GCH - 1 - cc
            r0 = pl.multiple_of(c * CH, CH)
            rows = pl.ds(r0, CH)
            H = range(NH)
            dsps = [ds_ref[h] for h in H]
            shs = [st_ref[h, c] for h in H]
            dsbs = [_mx(a) for a in dsps]
            sbs = [_mx(a) for a in shs]
            dobs = [_mx(do_ref[h, rows, :]) for h in H]
            vbs = [_mx(vn_ref[h, rows, :]) for h in H]
            dvns = [_dot(_mx(kd_ref[h, rows, :]), dsbs[h]) + _dot_tn(_mx(at_ref[h, rows, :]), dobs[h]) for h in H]
            dvbs = [_mx(a) for a in dvns]
            dqds = [_dot_nt(dobs[h], sbs[h]) for h in H]
            dats = [_dot_nt(dobs[h], vbs[h]) for h in H]
            dkds = [_dot_nt(vbs[h], dsbs[h]) for h in H]
            dws = [_dot_nt(dvbs[h], sbs[h]) for h in H]
            qdos = [_dot_tn(_mx(qd_ref[h, rows, :]), dobs[h]) for h in H]
            wdvs = [_dot_tn(_mx(w_ref[h, rows, :]), dvbs[h]) for h in H]
            for h in H:
                dvn_ref[h, rows, :] = dvns[h]
                dqd_ref[h, rows, :] = dqds[h]
                dat_ref[h, rows, :] = dats[h]
                dkd_ref[h, rows, :] = dkds[h]
                dw_ref[h, rows, :] = -dws[h]
                dcd = jnp.sum(_rowsum(dsps[h] * shs[h]), axis=0, keepdims=True)
                dcd_ref[h, c] = jnp.broadcast_to(dcd, (8, 128))
                ds_ref[h] = _scale_state(dsps[h], cd_ref[h, c]) + qdos[h] - wdvs[h]
            return carry

        lax.fori_loop(0, GCH, chunk, 0)

    hm = pl.BlockSpec((NH, GROWS, HD), lambda i: (0, nblk - 1 - i, 0))
    hm64 = pl.BlockSpec((NH, GROWS, CH), lambda i: (0, nblk - 1 - i, 0))
    tile = pl.BlockSpec((NH, GCH, 8, 128), lambda i: (0, nblk - 1 - i, 0, 0))
    big = jax.ShapeDtypeStruct((NH, S, HD), F32)
    return _call(
        body, name, (nblk,),
        [hm, hm, hm, hm, hm64, tile, hm, pl.BlockSpec((NH, GCH, HD, HD), lambda i: (0, nblk - 1 - i, 0, 0))],
        [hm, hm, hm, hm, hm64, tile],
        [big, big, big, big, jax.ShapeDtypeStruct((NH, S, CH), F32),
         jax.ShapeDtypeStruct((NH, S // CH, 8, 128), F32)],
        [pltpu.VMEM((NH, HD, HD), F32)],
    )(do, w, qd, kd, attn, cdt, vn, st)


def _gdn_intra_bwd(qn, kn, v, gb, u, w, tmat, dqd, dkd, du, dw, dattn, dcdt, name):
    S = qn.shape[0]

    def body(q_ref, k_ref, v_ref, gb_ref, u_ref, w_ref, t_ref, dqd_ref, dkd_ref, du_ref, dw_ref,
             dat_ref, dcd_ref, dq_ref, dk_ref, dv_ref, dgb_ref):
        h = pl.program_id(1)
        rows = [slice(c * CH, (c + 1) * CH) for c in range(ICH)]
        ks = [k_ref[r, :] for r in rows]
        cms = [_chunk_common(gb_ref[r, :], h, k) for r, k in zip(rows, ks)]
        sols = [jnp.concatenate([u_ref[0, r, :], w_ref[0, r, :]], axis=1) for r in rows]
        drhss = [_dot3(t_ref[0, r, :], jnp.concatenate([du_ref[0, r, :], dw_ref[0, r, :]], axis=1), _dot_tn)
                 for r in rows]
        das = [-_dot3(drhs, sol, _dot_nt) for drhs, sol in zip(drhss, sols)]
        for c, (r, k, cm, drhs, da) in enumerate(zip(rows, ks, cms, drhss, das)):
            q, vv = q_ref[r, :], v_ref[r, :]
            decay, eG, e2, b = cm["decay"], cm["eG"], cm["e2"], cm["b"]
            dru, drw = drhs[:, :HD], drhs[:, HD:]
            dv_ref[r, :] = dru * b
            s_w = _rowsum(drw * k)
            dbeta = _rowsum(dru * vv) + s_w * eG
            deg = s_w * b
            dk = drw * (b * eG)
            dkk = jnp.where(cm["strict"], da * decay, 0.0)
            ddec = jnp.where(cm["strict"], da * cm["kk"], 0.0)
            dkkb = _mx(dkk)
            dkb = _dot(dkkb, _mx(k))
            dk = dk + _dot_tn(dkkb, _mx(cm["kb"])) + dkb * b
            dbeta = dbeta + _rowsum(dkb * k)
            dat = jnp.where(cm["incl"], dat_ref[0, r, :], 0.0)
            qk = _dot_nt(_mx(q), _mx(k))
            dqk = _mx(dat * decay)
            ddec = ddec + dat * qk
            dqd = dqd_ref[0, r, :]
            dkd = dkd_ref[0, r, :]
            dq_ref[r, :] = _dot(dqk, _mx(k)) + dqd * eG
            dk_ref[r, :] = dk + _dot_tn(dqk, _mx(q)) + dkd * e2
            deg = deg + _rowsum(dqd * q)
            t2 = _rowsum(dkd * k) * e2
            dgl = jnp.sum(t2, axis=0, keepdims=True) + dcd_ref[0, c][0:1, 0:1] * cm["cd"]
            dd = ddec * decay
            dG = deg * eG - t2 + _rowsum(dd) - _rowsum(dd.T)
            rowi = lax.broadcasted_iota(jnp.int32, (CH, 1), 0)
            dG = dG + jnp.where(rowi == CH - 1, dgl, 0.0)
            lane = lax.broadcasted_iota(jnp.int32, (CH, 128), 1)
            dgb_ref[0, r, :] = jnp.where(lane == h, dG, 0.0) + jnp.where(lane == h + NH, dbeta, 0.0)

    tok = pl.BlockSpec((IROWS, HD), lambda i, h: (i, h))
    hm = pl.BlockSpec((1, IROWS, HD), lambda i, h: (h, i, 0))
    hm64 = pl.BlockSpec((1, IROWS, CH), lambda i, h: (h, i, 0))
    tile = pl.BlockSpec((1, ICH, 8, 128), lambda i, h: (h, i, 0, 0))
    tokout = jax.ShapeDtypeStruct((S, D), F32)
    return _call(
        body, name, (S // IROWS, NH),
        [tok, tok, tok, pl.BlockSpec((IROWS, 128), lambda i, h: (i, 0)), hm, hm, hm64,
         hm, hm, hm, hm, hm64, tile],
        [tok, tok, tok, pl.BlockSpec((1, IROWS, 128), lambda i, h: (h, i, 0))],
        [tokout, tokout, tokout, jax.ShapeDtypeStruct((NH, S, 128), F32)],
    )(qn, kn, v, gb, u, w, tmat, dqd, dkd, du, dw, dattn, dcdt)


def _gdn_prep_bwd(z, dqn, dkn, dv, dgb, conv_w, alog_row, dtb_row, name):
    S = z.shape[0]
    ts = GROWS
    nblk = S // ts
    scale = HD ** -0.5
    tb = ts // 8

    def body(z_ref, hp_ref, hn_ref, zab_ref, dq_ref, dqn_ref, dk_ref, dkn_ref, dv_ref, dvn_ref,
             dgb_ref, w_ref, al_ref, dt_ref, dz_ref, dzab_ref, dcw_ref, dvec_ref, buf, dybuf, dcbuf):
        i = pl.program_id(0)
        last = i == nblk - 1

        @pl.when(i == 0)
        def _():
            dcw_ref[...] = jnp.zeros_like(dcw_ref)
            dvec_ref[...] = jnp.zeros_like(dvec_ref)

        buf[0:8, :] = jnp.where(i == 0, 0.0, hp_ref[...])
        buf[8:8 + ts, :] = z_ref[...]
        buf[8 + ts:16 + ts, :] = hn_ref[...]
        rowi = lax.broadcasted_iota(jnp.int32, (ts + 8, 1), 0)
        live = jnp.logical_or(rowi < ts, jnp.logical_not(last))
        dys = ((dq_ref, dqn_ref), (dk_ref, dkn_ref), (dv_ref, dvn_ref))
        for seg in range(3):
            cs = slice(seg * D, (seg + 1) * D)
            dybuf[0:ts, :] = dys[seg][0][...]
            dybuf[ts:ts + 8, :] = dys[seg][1][...]
            c = jnp.zeros((ts + 8, D), F32)
            for j in range(4):
                c = c + w_ref[j:j + 1, cs] * buf[pl.ds(5 + j, ts + 8), cs]
            sg = _sigmoid(c)
            s = c * sg
            dsilu = sg * (1.0 + c * (1.0 - sg))
            if seg == 2:
                dcbuf[...] = jnp.where(live, dybuf[...] * dsilu, 0.0)
            else:
                mul = scale if seg == 0 else 1.0
                for h in range(NH):
                    sl = slice(h * HD, (h + 1) * HD)
                    sh = s[:, sl]
                    dy = dybuf[:, sl]
                    r = lax.rsqrt(_rowsum(sh * sh) + EPS)
                    shr = sh * r
                    ds = (mul * r) * (dy - shr * _rowsum(shr * dy))
                    dcbuf[:, sl] = jnp.where(live, ds * dsilu[:, sl], 0.0)
            dx = jnp.zeros((ts, D), F32)
            for j in range(4):
                dcw_ref[j:j + 1, cs] += _colsum(dcbuf[0:ts, :] * buf[pl.ds(5 + j, ts), cs])
                dx = dx + w_ref[j:j + 1, cs] * dcbuf[pl.ds(3 - j, ts), :]
            dz_ref[:, cs] = dx.astype(dz_ref.dtype)
        dgbs = dgb_ref[0]
        for h in range(1, NH):
            dgbs = dgbs + dgb_ref[h]
        ri = lax.broadcasted_iota(jnp.int32, (CH, CH), 0)
        ci = lax.broadcasted_iota(jnp.int32, (CH, CH), 1)
        rev = (ci >= ri).astype(F32)
        dgrev = jnp.concatenate([_dot(rev, dgbs[c * CH:(c + 1) * CH], HI) for c in range(ts // CH)], axis=0)
        lane0 = lax.broadcasted_iota(jnp.int32, dgbs.shape, 1)
        dgbs = jnp.where(lane0 < NH, dgrev, dgbs)
        zab = zab_ref[...]
        lane = lax.broadcasted_iota(jnp.int32, zab.shape, 1)
        xx = zab + dt_ref[...]
        ea = jnp.exp(al_ref[...])
        g = -ea * _softplus(xx)
        da = dgbs * (-ea) * _sigmoid(xx)
        beta = _sigmoid(zab)
        db = dgbs * beta * (1.0 - beta)
        is_a = lane < NH
        dzab = jnp.where(is_a, da, jnp.where(lane < 2 * NH, db, 0.0))
        dzab_ref[:, 0:128] = dzab.astype(dzab_ref.dtype)
        dzab_ref[:, 128:512] = jnp.zeros((ts, 384), dzab_ref.dtype)
        dvec_ref[0:1, :] += _colsum(jnp.where(is_a, dgbs * g, 0.0))
        dvec_ref[1:2, :] += _colsum(jnp.where(is_a, da, 0.0))

    z3 = pl.BlockSpec((ts, 3 * D), lambda i: (i, 0))
    row = pl.BlockSpec((ts, D), lambda i: (i, 0))
    nxt = pl.BlockSpec((8, D), lambda i: (jnp.minimum((i + 1) * tb, S // 8 - 1), 0))
    vec = pl.BlockSpec((1, 128), lambda i: (0, 0))
    return _call(
        body, name, (nblk,),
        [z3,
         pl.BlockSpec((8, 3 * D), lambda i: (jnp.maximum(i * tb - 1, 0), 0)),
         pl.BlockSpec((8, 3 * D), lambda i: (jnp.minimum((i + 1) * tb, S // 8 - 1), 0)),
         pl.BlockSpec((ts, 128), lambda i: (i, Z_AB // 128)),
         row, nxt, row, nxt, row, nxt,
         pl.BlockSpec((NH, ts, 128), lambda i: (0, i, 0)),
         _full((4, 3 * D)), vec, vec],
        [z3, pl.BlockSpec((ts, 512), lambda i: (i, 0)), _full((8, 3 * D)), _full((8, 128))],
        [jax.ShapeDtypeStruct((S, 3 * D), MXU), jax.ShapeDtypeStruct((S, 512), MXU),
         jax.ShapeDtypeStruct((8, 3 * D), F32), jax.ShapeDtypeStruct((8, 128), F32)],
        [pltpu.VMEM((ts + 16, 3 * D), F32), pltpu.VMEM((ts + 8, D), F32), pltpu.VMEM((ts + 8, D), F32)],
    )(z, z, z, z, dqn, dqn, dkn, dkn, dv, dv, dgb, conv_w, alog_row, dtb_row)


def _bias_index():
    u = np.arange(FW)[None, :]
    s = np.arange(3)[:, None]
    return np.clip(KWIN - 1 - u - QB * s, -256, 256) + 256


def _bias_vec(rel_bias_pad, onehot, name):
    def body(rb_ref, e_ref, o_ref):
        o_ref[:, 0, :] = _dot_nt(rb_ref[...], e_ref[0], HI)

    return _call(body, name, (3,),
                 [_full((NH, 640)), pl.BlockSpec((1, FW, 640), lambda s: (s, 0, 0))],
                 pl.BlockSpec((NH, 1, FW), lambda s: (s, 0, 0)),
                 jax.ShapeDtypeStruct((3 * NH, 1, FW), F32))(rel_bias_pad, onehot)


def _att_window(i):
    return pl.multiple_of(jnp.maximum(i * QB - PAST * CH, 0), QB)


def _bias_mask(fvec, name):
    def body(f_ref, o_ref):
        i = 2 - pl.program_id(0) // NH
        ws = jnp.maximum(i * QB - PAST * CH, 0)
        fb = jnp.broadcast_to(f_ref[0], (QB, FW))
        bias = pltpu.roll(fb, FW - 255, 1, stride=1, stride_axis=0)[:, :KWIN]
        qc = (i * QB + lax.broadcasted_iota(jnp.int32, (QB, KWIN), 0)) // CH
        kc = (ws + lax.broadcasted_iota(jnp.int32, (QB, KWIN), 1)) // CH
        o_ref[0] = jnp.where((kc <= qc) & (kc >= qc - PAST), bias, MASKED)

    return _call(body, name, (3 * NH,), [pl.BlockSpec((1, 1, FW), lambda j: (j, 0, 0))],
                 pl.BlockSpec((1, QB, KWIN), lambda j: (j, 0, 0)),
                 jax.ShapeDtypeStruct((3 * NH, QB, KWIN), F32))(fvec)


def _att_scores(q_ref, k_ref, bm_ref, i):
    ws = _att_window(i)
    q = _mx(q_ref[...] * (HD ** -0.5))
    kw = _mx(k_ref[pl.ds(ws, KWIN), :])
    return q, kw, ws, _dot_nt(q, kw) + bm_ref[0]


def _att_specs(S):
    c0 = Z_ATT // HD
    q = pl.BlockSpec((QB, HD), lambda h, i: (i, c0 + h))
    k = pl.BlockSpec((S, HD), lambda h, i: (0, c0 + NH + h))
    v = pl.BlockSpec((S, HD), lambda h, i: (0, c0 + 2 * NH + h))
    bm = pl.BlockSpec((1, QB, KWIN), lambda h, i: (jnp.maximum(2 - i, 0) * NH + h, 0, 0))
    tok = pl.BlockSpec((QB, HD), lambda h, i: (i, h))
    return q, k, v, bm, tok


def _att_fwd(z, bmask, name, comm=None):
    S = z.shape[0]

    def body(q_ref, k_ref, v_ref, bm_ref, o_ref, lse_ref):
        _, _, ws, s = _att_scores(q_ref, k_ref, bm_ref, pl.program_id(1))
        m = jnp.max(s, axis=1, keepdims=True)
        p = jnp.exp(s - m)
        l = _rowsum(p)
        o_ref[...] = _dot(_mx(p), _mx(v_ref[pl.ds(ws, KWIN), :])) * (1.0 / l)
        lse_ref[...] = jnp.broadcast_to(m + jnp.log(l), (QB, HD))

    q, k, v, bm, tok = _att_specs(S)
    shp = jax.ShapeDtypeStruct((S, D), F32)
    return _call(body, name, (NH, S // QB), [q, k, v, bm], [tok, tok], [shp, shp], comm=comm)(z, z, z, bmask)


def _att_bwd(z, bmask, ob, lse, dob, name):
    S = z.shape[0]
    nq = S // QB

    def body(q_ref, k_ref, v_ref, bm_ref, o_ref, lse_ref, do_ref, dq_ref, dk_ref, dv_ref, db_ref, dk_acc, dv_acc):
        i = pl.program_id(1)
        q, kw, ws, s = _att_scores(q_ref, k_ref, bm_ref, i)
        p = jnp.exp(s - lse_ref[:, 0:1])
        do = do_ref[...]
        dob16 = _mx(do)
        dp = _dot_nt(dob16, _mx(v_ref[pl.ds(ws, KWIN), :]))
        ds = p * (dp - _rowsum(do * o_ref[...]))
        dsb = _mx(ds)
        dq_ref[...] = (_dot(dsb, kw) * (HD ** -0.5)).astype(dq_ref.dtype)

        @pl.when(i == 0)
        def _():
            dk_acc[...] = jnp.zeros_like(dk_acc)
            dv_acc[...] = jnp.zeros_like(dv_acc)

        dk_acc[pl.ds(ws, KWIN), :] += _dot_tn(dsb, q)
        dv_acc[pl.ds(ws, KWIN), :] += _dot_tn(_mx(p), dob16)

        @pl.when(i == nq - 1)
        def _():
            dk_ref[...] = dk_acc[...].astype(dk_ref.dtype)
            dv_ref[...] = dv_acc[...].astype(dv_ref.dtype)

        @pl.when(i <= 2)
        def _():
            db_ref[0] = ds

        @pl.when(i > 2)
        def _():
            db_ref[0] += ds

    q, k, v, bm, tok = _att_specs(S)
    acc = pl.BlockSpec((S, HD), lambda h, i: (0, h))
    half = jax.ShapeDtypeStruct((S, D), MXU)
    return _call(
        body, name, (NH, nq), [q, k, v, bm, tok, tok, tok], [tok, acc, acc, bm],
        [half, half, half, jax.ShapeDtypeStruct((3 * NH, QB, KWIN), F32)],
        [pltpu.VMEM((S, HD), F32), pltpu.VMEM((S, HD), F32)],
    )(z, z, z, bmask, ob, lse, dob)


def _bias_fold(dbias, onehot, name):
    def body(db_ref, e_ref, o_ref):
        j = pl.program_id(0)
        h = j % NH
        x = jnp.concatenate([db_ref[0], jnp.zeros((QB, FW - KWIN), F32)], axis=1)
        half = QB // 2
        while half >= 8:
            x = x[:half] + pltpu.roll(x[half:2 * half], FW - half, 1)
            half //= 2
        df = jnp.zeros((1, FW), F32)
        for r in range(8):
            df = df + pltpu.roll(x[r:r + 1], 255 - r, 1)
        contrib = _dot(df, e_ref[0], HI)
        rowh = lax.broadcasted_iota(jnp.int32, (NH, 640), 0)

        @pl.when(j == 0)
        def _():
            o_ref[...] = jnp.zeros_like(o_ref)

        o_ref[...] += jnp.where(rowh == h, contrib, 0.0)

    return _call(
        body, name, (3 * NH,),
        [pl.BlockSpec((1, QB, KWIN), lambda j: (j, 0, 0)),
         pl.BlockSpec((1, FW, 640), lambda j: (j // NH, 0, 0))],
        _full((NH, 640)), jax.ShapeDtypeStruct((NH, 640), F32),
    )(dbias, onehot)


ADA_SHARD = 6 * D // NDEV


def _ada_mod(c_all, w_ada, b_shard, name):
    def body(c_ref, w_ref, b_ref, o_ref):
        cv = c_ref[...]
        ca = cv * _sigmoid(cv)
        o_ref[0] = _dot(_mx(ca), _mx(w_ref[0])) + b_ref[0]

    return _call(
        body, name, (DEPTH,),
        [_full((NDEV, D)), pl.BlockSpec((1, D, ADA_SHARD), lambda l: (l, 0, 0)),
         pl.BlockSpec((1, 1, ADA_SHARD), lambda l: (l, 0, 0))],
        pl.BlockSpec((1, NDEV, ADA_SHARD), lambda l: (l, 0, 0)),
        jax.ShapeDtypeStruct((DEPTH, NDEV, ADA_SHARD), F32),
    )(c_all, w_ada, b_shard.reshape(DEPTH, 1, ADA_SHARD))


def _adam(g, w, m, v):
    m = ADAM_B1 * m + (1.0 - ADAM_B1) * g
    v = ADAM_B2 * v + (1.0 - ADAM_B2) * jnp.square(g)
    m_hat = m / (1.0 - ADAM_B1 ** ADAM_STEP)
    v_hat = v / (1.0 - ADAM_B2 ** ADAM_STEP)
    delta = -ADAM_LR * (m_hat / (jnp.sqrt(v_hat) + ADAM_EPS) + ADAM_WD * w)
    return delta, m, v


def _wada_adamw(c_all_t, dmod, w, m, v, name):
    def body(c_ref, d_ref, w_ref, m_ref, v_ref, g_ref, dl_ref, mo_ref, vo_ref):
        cv = c_ref[...]
        ca = cv * _sigmoid(cv)
        g = _dot(ca, d_ref[0], HI)
        g_ref[0] = g
        dl_ref[0], mo_ref[0], vo_ref[0] = _adam(g, w_ref[0], m_ref[0], v_ref[0])

    blk = pl.BlockSpec((1, D, ADA_SHARD), lambda l: (l, 0, 0))
    shp = jax.ShapeDtypeStruct((DEPTH, D, ADA_SHARD), F32)
    return _call(
        body, name, (DEPTH,),
        [_full((D, NDEV)), pl.BlockSpec((1, NDEV, ADA_SHARD), lambda l: (l, 0, 0)), blk, blk, blk],
        [blk] * 4, [shp] * 4,
    )(c_all_t, dmod, w, m, v)


def _adamw_reduce(parts, w, m, v, name, tr):
    P, R, C = parts.shape

    def body(p_ref, w_ref, m_ref, v_ref, g_ref, dl_ref, mo_ref, vo_ref):
        g = p_ref[0].astype(F32)
        for k in range(1, P):
            g = g + p_ref[k].astype(F32)
        g_ref[...] = g
        dl_ref[...], mo_ref[...], vo_ref[...] = _adam(g, w_ref[...], m_ref[...], v_ref[...])

    blk = pl.BlockSpec((tr, C), lambda i: (i, 0))
    shp = jax.ShapeDtypeStruct((R, C), F32)
    return _call(body, name, (R // tr,), [pl.BlockSpec((P, tr, C), lambda i: (0, i, 0)), blk, blk, blk],
                 [blk] * 4, [shp] * 4)(parts, w, m, v)


def _sum_parts(parts, name):
    P, R, C = parts.shape

    def body(p_ref, o_ref):
        g = p_ref[0]
        for k in range(1, P):
            g = g + p_ref[k]
        o_ref[...] = g

    return _call(body, name, (1,), [_full((P, R, C))], _full((R, C)),
                 jax.ShapeDtypeStruct((R, C), F32))(parts)


def _pack_rows(vecs, width=1024):
    flat = jnp.concatenate([a.reshape(-1) for a in vecs])
    n = flat.shape[0]
    rows = -(-n // width)
    rows = -(-rows // 8) * 8
    return jnp.pad(flat, (0, rows * width - n)).reshape(rows, width)


def _unpack_rows(packed, shapes):
    flat = packed.reshape(-1)
    out, off = [], 0
    for s in shapes:
        n = int(np.prod(s)) if len(s) else 1
        out.append(flat[off:off + n].reshape(s))
        off += n
    return out


BIG = ("w_in", "w_out", "w_ff_in", "w_ff_out")


def _z_weights(g_in):
    w = jnp.transpose(g_in, (1, 0, 2)).reshape(D, IN_W)
    return jnp.concatenate([w[:, :Z_ATT], w[:, Z_ATT + 2 * NH:], w[:, Z_ATT:Z_ATT + 2 * NH],
                            jnp.zeros((D, ZW - IN_W), MXU)], axis=-1)


def _cols_to_owners(a):
    return jnp.transpose(a.astype(MXU).reshape(a.shape[0], NDEV, -1), (1, 0, 2))


def _rows_to_owners(a):
    return a.astype(MXU).reshape(NDEV, -1, a.shape[1])


def _forward_layer(x, mod_l, p, shard, next_w_in):
    sh1, sc1, gt1, sh2, sc2, gt2 = [mod_l[k][None] for k in range(6)]
    P = functools.partial
    gather = lambda a: None if a is None else ("gather", a)
    h = _modnorm_fwd(x, p["norm_mix"], sc1, sh1, "norm_mix_fwd")
    z, g_in = _hosted(P(_mm_nn, h, p["wz"], "in_proj", tm=2048, tn=512), gather(next_w_in))
    (qn, kn, v, gb), g_w1 = _hosted(P(_gdn_prep_fwd, z, p["conv_w"], p["alog"], p["dtb"], "gdn_prep_fwd"),
                                    gather(shard["w_ff_in"]))
    (u, w, qd, kd, attn, tmat, cdt), g_w2 = _hosted(P(_gdn_intra_fwd, qn, kn, v, gb, "gdn_intra_fwd"),
                                                    gather(shard["w_ff_out"]))
    o, vn, st = _gdn_scan_fwd(u, w, qd, kd, attn, cdt, "gdn_scan_fwd")
    (ob, lse), g_out = _hosted(P(_att_fwd, z, p["bmask"], "att_fwd"), gather(shard["w_out"]))
    wout = g_out.reshape(D, D)
    w1 = jnp.transpose(g_w1, (1, 0, 2)).reshape(D, DFF)
    w2 = g_w2.reshape(DFF, D)
    m = _merge_fwd(o, z, ob, p["gdn_norm"], "merge_fwd")
    x1, h2 = _mm_nn(m, wout, "out_proj", mode="resid_norm", res=x, gate=gt1, norm=(p["norm_mlp"], sc2, sh2))
    a, r = _mm_nn(h2, w1, "ff_in", mode="relu2")
    x2 = _mm_nn(r, w2, "ff_out", mode="resid", res=x1, gate=gt2)
    saved = dict(x=x, h=h, z=z, qn=qn, kn=kn, v=v, gb=gb, u=u, w=w, qd=qd, kd=kd, attn=attn,
                 tmat=tmat, cdt=cdt, o=o, vn=vn, st=st, ob=ob, lse=lse, m=m, x1=x1, h2=h2, a=a, r=r,
                 wout=wout, w1=w1, w2=w2)
    return x2, saved, g_in


def _backward_layer(dx2, mod_l, p, s, onehot):
    sh1, sc1, gt1, sh2, sc2, gt2 = [mod_l[k][None] for k in range(6)]
    P = functools.partial
    dw2, dgt2 = _mm_tn(s["r"], dx2, "ff_out_dw", gate=gt2, w=s["w2"])
    da, r_w2 = _hosted(P(_mm_nt, dx2, s["w2"], "ff_out_dx", gate=gt2, drelu=s["a"]), ("a2a", _rows_to_owners(dw2)))
    dw1 = _mm_tn(s["h2"], da, "ff_in_dw")
    dh2, r_w1 = _hosted(P(_mm_nt, da, s["w1"], "ff_in_dx"), ("a2a", _cols_to_owners(dw1)))
    dx1, dsc2, dsh2, dnmlp = _modnorm_bwd(dh2, s["x1"], p["norm_mlp"], sc2, sh2, dx2, "norm_mlp_bwd")
    dwout, dgt1 = _mm_tn(s["m"], dx1, "out_proj_dw", gate=gt1, w=s["wout"])
    dm, r_out = _hosted(P(_mm_nt, dx1, s["wout"], "out_proj_dx", gate=gt1), ("a2a", _rows_to_owners(dwout)))
    do, dzg, dob, dza, dzb, dgn = _merge_bwd(dm, s["o"], s["z"], s["ob"], p["gdn_norm"], "merge_bwd")
    dq_att, dk_att, dv_att, dbias = _att_bwd(s["z"], p["bmask"], s["ob"], s["lse"], dob, "att_bwd")
    drb = _bias_fold(dbias, onehot, "rel_bias_fold")[:, :513]
    dqd, dkd, dvn, dw, dattn, dcdt = _gdn_scan_bwd(do, s["w"], s["qd"], s["kd"], s["attn"], s["cdt"],
                                                   s["vn"], s["st"], "gdn_scan_bwd")
    dqn, dkn, dv, dgb = _gdn_intra_bwd(s["qn"], s["kn"], s["v"], s["gb"], s["u"], s["w"], s["tmat"],
                                       dqd, dkd, dvn, dw, dattn, dcdt, "gdn_intra_bwd")
    dzq, dzab, dcw, dvec = _gdn_prep_bwd(s["z"], dqn, dkn, dv, dgb, p["conv_w"], p["alog"], p["dtb"],
                                         "gdn_prep_bwd")
    dz = (dzq, dzg, dq_att, dk_att, dv_att, dza, dzb, dzab)
    dwz = _in_proj_dw(s["h"], dz, "in_proj_dw")
    dw_in = jnp.concatenate([dwz[:, :Z_ATT], dwz[:, Z_AB:Z_AB + 2 * NH], dwz[:, Z_ATT:Z_AB]], axis=1)
    dh, r_in = _hosted(P(_in_proj_dx, dz, p["wz"], "in_proj_dx"), ("a2a", _cols_to_owners(dw_in)))
    dx, dsc1, dsh1, dnmix = _modnorm_bwd(dh, s["x"], p["norm_mix"], sc1, sh1, dx1, "norm_mix_bwd")
    grads = dict(norm_mix=dnmix[0], norm_mlp=dnmlp[0], conv_w=dcw[:4], a_log=dvec[0, :NH], dt_bias=dvec[1, :NH],
                 gdn_norm=dgn[0], rel_bias=drb, mod=jnp.concatenate([dsh1, dsc1, dgt1, dsh2, dsc2, dgt2], axis=1)[0])
    return dx, grads, dict(w_in=r_in, w_out=r_out, w_ff_in=r_w1, w_ff_out=r_w2)


def _bias_onehot():
    return (jnp.asarray(_bias_index())[:, :, None] == jnp.arange(640)[None, None, :]).astype(F32)


def _layer_params(l, conv_full, norm_mix, norm_mlp, a_log, dt_bias, gdn_norm, rel_bias, onehot):
    pad = lambda a: jnp.pad(a, (0, 128 - NH))[None]
    fvec = _bias_vec(jnp.pad(rel_bias[l], ((0, 0), (0, 640 - rel_bias.shape[2]))), onehot, "rel_bias_vec")
    return dict(conv_w=conv_full[l], norm_mix=norm_mix[l][None], norm_mlp=norm_mlp[l][None], alog=pad(a_log[l]),
                dtb=pad(dt_bias[l]), gdn_norm=gdn_norm[l][None], bmask=_bias_mask(fvec, "rel_bias_mask"))


def _local_step(x, target, mod, small, shards, final_norm, onehot):
    L = len(small)
    g_in = _all_gather(shards[0]["w_in"], "gather_w_in")
    saved, params = [], []
    for l in range(L):
        params.append({**small[l], "wz": _z_weights(g_in)})
        x, sv, g_in = _forward_layer(x, mod[l].reshape(6, D), params[l], shards[l],
                                     shards[l + 1]["w_in"] if l + 1 < L else None)
        saved.append(sv)
    loss, dx, dfn = _loss_head(x, target, final_norm[None], "loss_head")
    grads, recv = [None] * L, [None] * L
    for l in reversed(range(L)):
        dx, grads[l], recv[l] = _backward_layer(dx, mod[l].reshape(6, D), params[l], saved[l], onehot)
    return loss, dx, grads, dfn[0], recv


SMALL = ("b_ada", "norm_mix", "norm_mlp", "a_log", "dt_bias", "gdn_norm", "rel_bias", "final_norm")


def kernel(x, c, w_ada, b_ada, norm_mix, norm_mlp, w_in, conv_w, a_log, dt_bias, gdn_norm, rel_bias, w_out, w_ff_in, w_ff_out, final_norm, loss_target, m_w_ada, m_b_ada, m_norm_mix, m_norm_mlp, m_w_in, m_conv_w, m_a_log, m_dt_bias, m_gdn_norm, m_rel_bias, m_w_out, m_w_ff_in, m_w_ff_out, m_final_norm, v_w_ada, v_b_ada, v_norm_mix, v_norm_mlp, v_w_in, v_conv_w, v_a_log, v_dt_bias, v_gdn_norm, v_rel_bias, v_w_out, v_w_ff_in, v_w_ff_out, v_final_norm):
    W = dict(w_ada=w_ada, b_ada=b_ada, norm_mix=norm_mix, norm_mlp=norm_mlp, w_in=w_in, conv_w=conv_w,
             a_log=a_log, dt_bias=dt_bias, gdn_norm=gdn_norm, rel_bias=rel_bias, w_out=w_out,
             w_ff_in=w_ff_in, w_ff_out=w_ff_out, final_norm=final_norm)
    Mo = dict(w_ada=m_w_ada, b_ada=m_b_ada, norm_mix=m_norm_mix, norm_mlp=m_norm_mlp, w_in=m_w_in,
              conv_w=m_conv_w, a_log=m_a_log, dt_bias=m_dt_bias, gdn_norm=m_gdn_norm, rel_bias=m_rel_bias,
              w_out=m_w_out, w_ff_in=m_w_ff_in, w_ff_out=m_w_ff_out, final_norm=m_final_norm)
    Vo = dict(w_ada=v_w_ada, b_ada=v_b_ada, norm_mix=v_norm_mix, norm_mlp=v_norm_mlp, w_in=v_w_in,
              conv_w=v_conv_w, a_log=v_a_log, dt_bias=v_dt_bias, gdn_norm=v_gdn_norm, rel_bias=v_rel_bias,
              w_out=v_w_out, w_ff_in=v_w_ff_in, w_ff_out=v_w_ff_out, final_norm=v_final_norm)
    L = w_in.shape[0]
    me = _flat(_mesh_pos())
    cshard = conv_w.shape[2]

    small_in = _all_gather(_pack_rows([c, conv_w]), "gather_c_conv")
    c_all = small_in[:, 0, :]
    conv_full = small_in.reshape(NDEV, -1)[:, D:D + L * 4 * cshard].reshape(NDEV, L, 4, cshard)
    conv_full = jnp.transpose(conv_full, (1, 2, 0, 3)).reshape(L, 4, NDEV * cshard)

    b_shard = lax.dynamic_slice_in_dim(b_ada, me * ADA_SHARD, ADA_SHARD, axis=1)
    mod_all = _all_gather(_ada_mod(c_all, w_ada, b_shard, "ada_mod"), "gather_mod")
    mod = lax.dynamic_index_in_dim(mod_all, me, axis=2, keepdims=False)
    mod = jnp.transpose(mod, (1, 0, 2)).reshape(L, 6 * D)

    onehot = _bias_onehot()
    small = [_layer_params(l, conv_full, norm_mix, norm_mlp, a_log, dt_bias, gdn_norm, rel_bias, onehot)
             for l in range(L)]
    shards = [{n: W[n][l].astype(MXU) for n in BIG} for l in range(L)]
    loss, dx, grads, dfn, recv = _local_step(x[0], loss_target[0], mod, small, shards, final_norm, onehot)

    def stack(name):
        return jnp.stack([g[name] for g in grads])

    small_names = ("mod", "norm_mix", "norm_mlp", "a_log", "dt_bias", "gdn_norm", "rel_bias")
    small_parts = [stack(n) for n in small_names] + [dfn, stack("conv_w"), loss[0, 0:1]]
    small_shapes = [a.shape for a in small_parts]
    gathered = _all_gather(_pack_rows(small_parts), "gather_small_grads")
    total = _unpack_rows(_sum_parts(gathered, "sum_small_grads"), small_shapes)
    tot = dict(zip(small_names + ("final_norm", "conv_w", "loss"), total))
    tot["b_ada"] = tot.pop("mod")
    tot["conv_w"] = lax.dynamic_slice_in_dim(tot["conv_w"], me * cshard, cshard, axis=2)

    out_g, out_d, out_m, out_v = {}, {}, {}, {}
    names = SMALL + ("conv_w",)
    shapes = [W[n].shape for n in names]
    packed = [_pack_rows([src[n] for n in names])[None] if src is tot else _pack_rows([src[n] for n in names])
              for src in (tot, W, Mo, Vo)]
    res = _adamw_reduce(*packed, "adamw_small", tr=8)
    for dst, arr in zip((out_g, out_d, out_m, out_v), res):
        dst.update(zip(names, _unpack_rows(arr, shapes)))

    dmod_all = gathered.reshape(NDEV, -1)[:, :L * 6 * D].reshape(NDEV, L, 6 * D)
    dmod_mine = jnp.transpose(lax.dynamic_slice_in_dim(dmod_all, me * ADA_SHARD, ADA_SHARD, axis=2), (1, 0, 2))
    res = _wada_adamw(jnp.transpose(c_all), dmod_mine, w_ada, m_w_ada, v_w_ada, "adamw_w_ada")
    for dst, arr in zip((out_g, out_d, out_m, out_v), res):
        dst["w_ada"] = arr

    for name, tr in (("w_in", 256), ("w_out", 128), ("w_ff_in", 256), ("w_ff_out", 256)):
        sh = W[name].shape
        rows = int(np.prod(sh[:-1]))
        flat = lambda a: a.reshape(rows, sh[-1])
        parts = jnp.stack([recv[l][name] for l in range(L)], axis=1).reshape(NDEV, rows, sh[-1])
        res = _adamw_reduce(parts, flat(W[name]), flat(Mo[name]), flat(Vo[name]), "adamw_" + name, tr=tr)
        for dst, arr in zip((out_g, out_d, out_m, out_v), res):
            dst[name] = arr.reshape(sh)

    order = ("w_ada", "b_ada", "norm_mix", "norm_mlp", "w_in", "conv_w", "a_log", "dt_bias", "gdn_norm",
             "rel_bias", "w_out", "w_ff_in", "w_ff_out", "final_norm")
    return (tot["loss"].reshape(()), dx[None], *[out_g[n] for n in order], *[out_d[n] for n in order],
            *[out_m[n] for n in order], *[out_v[n] for n in order])
```

```python
import functools
import math

import numpy as np
import jax
import jax.numpy as jnp
from jax import lax
from jax.experimental import pallas as pl
from jax.experimental.pallas import tpu as pltpu

F32 = jnp.float32
MXU = jnp.bfloat16
HI = lax.Precision.HIGHEST
MESH_ID = pl.DeviceIdType.MESH

D = 1024
NH = 8
HD = 128
CH = 64
PAST = 8
DFF = 4096
EPS = 1e-6
NDEV = 8
DEPTH = 4
IN_W = 9232
ZW = 9728
Z_GATE, Z_ATT, Z_BR, Z_AB = 3072, 4096, 7168, 9216
QB = 256
MASKED = -1e30
KWIN = 768
FW = 1024
ADAM_LR, ADAM_B1, ADAM_B2, ADAM_EPS, ADAM_WD, ADAM_STEP = 0.001, 0.9, 0.999, 1e-08, 0.01, 10


def _dot(a, b, prec=None):
    return jnp.dot(a, b, preferred_element_type=F32, precision=prec)


def _dot_nt(a, b, prec=None):
    return lax.dot_general(a, b, (((1,), (1,)), ((), ())), preferred_element_type=F32, precision=prec)


def _dot_tn(a, b, prec=None):
    return lax.dot_general(a, b, (((0,), (0,)), ((), ())), preferred_element_type=F32, precision=prec)


def _mx(a):
    return a.astype(MXU)


def _sigmoid(x):
    return 0.5 * jnp.tanh(0.5 * x) + 0.5


def _softplus(x):
    return jnp.maximum(x, 0.0) + jnp.log(1.0 + jnp.exp(-jnp.abs(x)))


def _rowsum(x):
    return jnp.sum(x, axis=1, keepdims=True)


def _colsum(x):
    return jnp.sum(x, axis=0, keepdims=True)


def _call(body, name, grid, in_specs, out_specs, out_shape, scratch=(), comm=None):
    if comm is None:
        return pl.pallas_call(body, name=name, grid=grid, in_specs=in_specs, out_specs=out_specs,
                              out_shape=out_shape, scratch_shapes=list(scratch))
    kind, x = comm
    single = not isinstance(out_specs, (list, tuple))
    o_specs = [out_specs] if single else list(out_specs)
    o_shape = [out_shape] if single else list(out_shape)
    n_in, n_out, n_scr = len(in_specs), len(o_specs), len(scratch)
    c_shape = (NDEV,) + x.shape if kind == "gather" else x.shape

    def wrapped(*refs):
        ins, x_ref = refs[:n_in], refs[n_in]
        outs, c_ref = refs[n_in + 1:n_in + 1 + n_out], refs[n_in + 1 + n_out]
        scr = refs[n_in + 2 + n_out:n_in + 2 + n_out + n_scr]
        sems = refs[n_in + 2 + n_out + n_scr:]
        first = functools.reduce(jnp.logical_and, [pl.program_id(a) == 0 for a in range(len(grid))])
        last = functools.reduce(jnp.logical_and, [pl.program_id(a) == grid[a] - 1 for a in range(len(grid))])

        @pl.when(first)
        def _():
            _comm_start(*_comm_copies(kind, x_ref, c_ref, *sems))

        body(*ins, *outs, *scr)

        @pl.when(last)
        def _():
            _comm_wait(*_comm_copies(kind, x_ref, c_ref, *sems))

    any_spec = pl.BlockSpec(memory_space=pl.ANY)
    call = pl.pallas_call(
        wrapped, name=name, grid=grid, in_specs=list(in_specs) + [any_spec], out_specs=o_specs + [any_spec],
        out_shape=o_shape + [jax.ShapeDtypeStruct(c_shape, x.dtype)],
        scratch_shapes=list(scratch) + _comm_sems())

    def run(*args):
        res = call(*args, x)
        return (res[0] if single else list(res[:-1])), res[-1]

    return run


def _hosted(fn, comm):
    return (fn(), None) if comm is None else fn(comm=comm)


def _full(shape):
    n = len(shape)
    return pl.BlockSpec(shape, lambda *_: (0,) * n)


def _mesh_pos():
    return lax.axis_index("x"), lax.axis_index("y"), lax.axis_index("c")


def _peer(pos, k):
    x, y, c = pos
    return (x ^ ((k >> 2) & 1), y ^ ((k >> 1) & 1), c ^ (k & 1))


def _flat(pos):
    return 4 * pos[0] + 2 * pos[1] + pos[2]


def _comm_sems():
    return [pltpu.SemaphoreType.DMA((NDEV - 1,)), pltpu.SemaphoreType.DMA((NDEV - 1,)), pltpu.SemaphoreType.DMA]


def _comm_copies(kind, x_ref, out_ref, send_sems, recv_sems, local_sem):
    pos = _mesh_pos()
    me = _flat(pos)
    src = (lambda d: x_ref) if kind == "gather" else (lambda d: x_ref.at[d])
    mine = pltpu.make_async_copy(src(me), out_ref.at[me], local_sem)
    sends, recvs = [], []
    for k in range(1, NDEV):
        peer = _peer(pos, k)
        pid = _flat(peer)
        sems = dict(send_sem=send_sems.at[k - 1], recv_sem=recv_sems.at[k - 1], device_id=peer,
                    device_id_type=MESH_ID)
        sends.append(pltpu.make_async_remote_copy(src_ref=src(pid), dst_ref=out_ref.at[me], **sems))
        recvs.append(pltpu.make_async_remote_copy(src_ref=src(pid), dst_ref=out_ref.at[pid], **sems))
    return mine, sends, recvs


def _comm_start(mine, sends, recvs):
    mine.start()
    for cp in sends:
        cp.start()


def _comm_wait(mine, sends, recvs):
    for cp in recvs:
        cp.wait_recv()
    for cp in sends:
        cp.wait_send()
    mine.wait()


def _collective(kind, x, name):
    def body(x_ref, out_ref, *sems):
        copies = _comm_copies(kind, x_ref, out_ref, *sems)
        _comm_start(*copies)
        _comm_wait(*copies)

    shape = (NDEV,) + x.shape if kind == "gather" else x.shape
    return pl.pallas_call(
        body, name=name, out_shape=jax.ShapeDtypeStruct(shape, x.dtype),
        in_specs=[pl.BlockSpec(memory_space=pl.ANY)], out_specs=pl.BlockSpec(memory_space=pl.ANY),
        scratch_shapes=_comm_sems())(x)


def _all_gather(x, name):
    return _collective("gather", x, name)


def _mm_nn(a, w, name, *, mode="plain", res=None, gate=None, norm=None, tm=1024, tn=1024, tk=1024, comm=None):
    M, K = a.shape
    N = w.shape[1]
    tm, tk, tn = min(tm, M), min(tk, K), min(tn, N)
    nk = K // tk

    def body(*refs):
        refs = list(refs)
        acc = refs.pop() if nk > 1 else None
        if mode == "resid":
            a_ref, w_ref, res_ref, gate_ref, o_ref = refs
        elif mode == "resid_norm":
            a_ref, w_ref, res_ref, gate_ref, g_ref, sc_ref, sh_ref, o_ref, r_ref = refs
        elif mode == "relu2":
            a_ref, w_ref, o_ref, r_ref = refs
        else:
            a_ref, w_ref, o_ref = refs
        k = pl.program_id(2)
        part = _dot(_mx(a_ref[...]), w_ref[...])

        def finish(r):
            if mode == "resid":
                o_ref[...] = res_ref[...] + gate_ref[...] * r
            elif mode == "resid_norm":
                xv = res_ref[...] + gate_ref[...] * r
                o_ref[...] = xv
                rs = lax.rsqrt(jnp.mean(xv * xv, axis=1, keepdims=True) + EPS)
                r_ref[...] = ((xv * rs * g_ref[...]) * (1.0 + sc_ref[...]) + sh_ref[...]).astype(r_ref.dtype)
            elif mode == "relu2":
                o_ref[...] = r
                r_ref[...] = jnp.square(jnp.maximum(r, 0.0)).astype(r_ref.dtype)
            else:
                o_ref[...] = r

        if nk == 1:
            finish(part)
        else:
            @pl.when(k == 0)
            def _():
                acc[...] = part

            @pl.when((k > 0) & (k < nk - 1))
            def _():
                acc[...] += part

            @pl.when(k == nk - 1)
            def _():
                finish(acc[...] + part)

    in_specs = [pl.BlockSpec((tm, tk), lambda i, j, k: (i, k)),
                pl.BlockSpec((tk, tn), lambda i, j, k: (k, j))]
    args = [a, w]
    o_spec = pl.BlockSpec((tm, tn), lambda i, j, k: (i, j))
    out_specs, out_shape = o_spec, jax.ShapeDtypeStruct((M, N), F32)
    vec = pl.BlockSpec((1, tn), lambda i, j, k: (0, j))
    if mode == "resid":
        in_specs += [o_spec, vec]
        args += [res, gate]
    elif mode == "resid_norm":
        assert tn == N
        in_specs += [o_spec, vec, vec, vec, vec]
        args += [res, gate, *norm]
        out_specs = [o_spec, o_spec]
        out_shape = [jax.ShapeDtypeStruct((M, N), F32), jax.ShapeDtypeStruct((M, N), MXU)]
    elif mode == "relu2":
        out_specs = [o_spec, o_spec]
        out_shape = [jax.ShapeDtypeStruct((M, N), F32), jax.ShapeDtypeStruct((M, N), MXU)]
    return _call(body, name, (M // tm, N // tn, nk), in_specs, out_specs, out_shape,
                 [pltpu.VMEM((tm, tn), F32)] if nk > 1 else [], comm=comm)(*args)


def _mm_nt(a, w, name, *, gate=None, drelu=None, tm=1024, tko=1024, tn=1024, comm=None):
    M, N = a.shape
    K = w.shape[0]
    tm, tn, tko = min(tm, M), min(tn, N), min(tko, K)
    nn = N // tn

    def body(*refs):
        refs = list(refs)
        acc = refs.pop() if nn > 1 else None
        a_ref, w_ref = refs[:2]
        rest = refs[2:]
        gate_ref = rest.pop(0) if gate is not None else None
        pre_ref = rest.pop(0) if drelu is not None else None
        (o_ref,) = rest
        n = pl.program_id(2)
        av = a_ref[...]
        if gate_ref is not None:
            av = av * gate_ref[...]
        part = _dot_nt(_mx(av), w_ref[...])

        def finish(r):
            if pre_ref is not None:
                r = r * (2.0 * jnp.maximum(pre_ref[...], 0.0))
            o_ref[...] = r.astype(o_ref.dtype)

        if nn == 1:
            finish(part)
        else:
            @pl.when(n == 0)
            def _():
                acc[...] = part

            @pl.when((n > 0) & (n < nn - 1))
            def _():
                acc[...] += part

            @pl.when(n == nn - 1)
            def _():
                finish(acc[...] + part)

    in_specs = [pl.BlockSpec((tm, tn), lambda i, j, n: (i, n)),
                pl.BlockSpec((tko, tn), lambda i, j, n: (j, n))]
    args = [a, w]
    if gate is not None:
        in_specs.append(pl.BlockSpec((1, tn), lambda i, j, n: (0, n)))
        args.append(gate)
    o_spec = pl.BlockSpec((tm, tko), lambda i, j, n: (i, j))
    if drelu is not None:
        in_specs.append(o_spec)
        args.append(drelu)
    out_dtype = MXU if drelu is not None else F32
    return _call(body, name, (M // tm, K // tko, nn), in_specs, o_spec,
                 jax.ShapeDtypeStruct((M, K), out_dtype), [pltpu.VMEM((tm, tko), F32)] if nn > 1 else [],
                 comm=comm)(*args)


def _mm_tn(a, b, name, *, gate=None, w=None, tk=1024, tn=1024, tm=1024, comm=None):
    M, K = a.shape
    N = b.shape[1]
    tm, tk, tn = min(tm, M), min(tk, K), min(tn, N)
    nm = M // tm
    gated = gate is not None

    def body(*refs):
        if gated:
            a_ref, b_ref, gate_ref, w_ref, o_ref, dg_ref, acc = refs
        else:
            a_ref, b_ref, o_ref, acc = refs
        kk = pl.program_id(1)
        m = pl.program_id(2)
        part = _dot_tn(_mx(a_ref[...]), _mx(b_ref[...]))

        @pl.when((m == 0) & (nm > 1))
        def _():
            acc[...] = part

        @pl.when((m > 0) & (m < nm - 1))
        def _():
            acc[...] += part

        if gated:
            @pl.when((m == 0) & (kk == 0))
            def _():
                dg_ref[...] = jnp.zeros_like(dg_ref)

        @pl.when(m == nm - 1)
        def _():
            r = acc[...] + part if nm > 1 else part
            if gated:
                o_ref[...] = r * gate_ref[...]
                dg_ref[...] += _colsum(r * w_ref[...].astype(F32))
            else:
                o_ref[...] = r

    in_specs = [pl.BlockSpec((tm, tk), lambda j, k, m: (m, k)),
                pl.BlockSpec((tm, tn), lambda j, k, m: (m, j))]
    args = [a, b]
    o_spec = pl.BlockSpec((tk, tn), lambda j, k, m: (k, j))
    out_specs, out_shape = o_spec, jax.ShapeDtypeStruct((K, N), F32)
    if gated:
        in_specs += [pl.BlockSpec((1, tn), lambda j, k, m: (0, j)), o_spec]
        args += [gate, w]
        out_specs = [o_spec, pl.BlockSpec((1, tn), lambda j, k, m: (0, j))]
        out_shape = [out_shape, jax.ShapeDtypeStruct((1, N), F32)]
    return _call(body, name, (N // tn, K // tk, nm), in_specs, out_specs, out_shape,
                 [pltpu.VMEM((tk, tn), F32)], comm=comm)(*args)


SEG_T = 512


def _seg_layout(segs):
    starts, t = [], 0
    for a in segs:
        starts.append(t)
        t += a.shape[1] // SEG_T
    return starts, t


def _seg_spec(tm, lo, hi, row_axis, col_axis):
    def index(*ids):
        col = ids[col_axis]
        act = (col >= lo) & (col < hi)
        return jnp.where(act, ids[row_axis], 0), jnp.where(act, col - lo, 0)

    return pl.BlockSpec((tm, SEG_T), index)


def _in_proj_dw(h, segs, name, tm=1024):
    S = h.shape[0]
    tm = min(tm, S)
    nm = S // tm
    starts, ntile = _seg_layout(segs)
    bounds = [(lo, lo + a.shape[1] // SEG_T) for lo, a in zip(starts, segs)]

    def body(*refs):
        h_ref, seg_refs, o_ref, acc = refs[0], refs[1:1 + len(segs)], refs[-2], refs[-1]
        j = pl.program_id(0)
        m = pl.program_id(1)
        for (lo, hi), b_ref in zip(bounds, seg_refs):
            @pl.when((j >= lo) & (j < hi))
            def _():
                part = _dot_tn(h_ref[...], _mx(b_ref[...]))
                if nm == 1:
                    o_ref[...] = part
                else:
                    @pl.when(m == 0)
                    def _():
                        acc[...] = part

                    @pl.when((m > 0) & (m < nm - 1))
                    def _():
                        acc[...] += part

                    @pl.when(m == nm - 1)
                    def _():
                        o_ref[...] = acc[...] + part

    return _call(
        body, name, (ntile, nm),
        [pl.BlockSpec((tm, D), lambda j, m: (m, 0))] + [_seg_spec(tm, lo, hi, 1, 0) for lo, hi in bounds],
        pl.BlockSpec((D, SEG_T), lambda j, m: (0, j)), jax.ShapeDtypeStruct((D, ntile * SEG_T), F32),
        [pltpu.VMEM((D, SEG_T), F32)])(h, *segs)


def _in_proj_dx(segs, w, name, tm=1024, comm=None):
    S = segs[0].shape[0]
    tm = min(tm, S)
    starts, ntile = _seg_layout(segs)
    bounds = [(lo, lo + a.shape[1] // SEG_T) for lo, a in zip(starts, segs)]

    def body(*refs):
        seg_refs, w_ref, o_ref, acc = refs[:len(segs)], refs[-3], refs[-2], refs[-1]
        n = pl.program_id(1)
        for (lo, hi), a_ref in zip(bounds, seg_refs):
            @pl.when((n >= lo) & (n < hi))
            def _():
                part = _dot_nt(_mx(a_ref[...]), w_ref[...])

                @pl.when(n == 0)
                def _():
                    acc[...] = part

                @pl.when((n > 0) & (n < ntile - 1))
                def _():
                    acc[...] += part

                @pl.when(n == ntile - 1)
                def _():
                    o_ref[...] = acc[...] + part

    return _call(
        body, name, (S // tm, ntile),
        [_seg_spec(tm, lo, hi, 0, 1) for lo, hi in bounds] + [pl.BlockSpec((D, SEG_T), lambda i, n: (0, n))],
        pl.BlockSpec((tm, D), lambda i, n: (i, 0)), jax.ShapeDtypeStruct((S, D), F32),
        [pltpu.VMEM((tm, D), F32)], comm=comm)(*segs, w)


def _modnorm_fwd(x, gain, sc, sh, name, ts=512):
    S = x.shape[0]

    def body(x_ref, g_ref, sc_ref, sh_ref, h_ref):
        xv = x_ref[...]
        r = lax.rsqrt(jnp.mean(xv * xv, axis=1, keepdims=True) + EPS)
        h_ref[...] = ((xv * r * g_ref[...]) * (1.0 + sc_ref[...]) + sh_ref[...]).astype(h_ref.dtype)

    row = pl.BlockSpec((ts, D), lambda i: (i, 0))
    vec = pl.BlockSpec((1, D), lambda i: (0, 0))
    return _call(body, name, (S // ts,), [row, vec, vec, vec], row,
                 jax.ShapeDtypeStruct((S, D), MXU))(x, gain, sc, sh)


def _modnorm_bwd(dh, x, gain, sc, sh, dx_in, name, ts=512):
    S = x.shape[0]

    def body(dh_ref, x_ref, g_ref, sc_ref, sh_ref, dxin_ref, dx_ref, dsc_ref, dsh_ref, dg_ref):
        i = pl.program_id(0)
        xv = x_ref[...]
        dhv = dh_ref[...]
        g = g_ref[...]
        r = lax.rsqrt(jnp.mean(xv * xv, axis=1, keepdims=True) + EPS)
        xr = xv * r
        dn = dhv * (1.0 + sc_ref[...])
        u = dn * g
        dx_ref[...] = dxin_ref[...] + r * (u - xr * jnp.mean(xr * u, axis=1, keepdims=True))

        @pl.when(i == 0)
        def _():
            dsc_ref[...] = jnp.zeros_like(dsc_ref)
            dsh_ref[...] = jnp.zeros_like(dsh_ref)
            dg_ref[...] = jnp.zeros_like(dg_ref)

        dsc_ref[...] += _colsum(dhv * (xr * g))
        dsh_ref[...] += _colsum(dhv)
        dg_ref[...] += _colsum(dn * xr)

    row = pl.BlockSpec((ts, D), lambda i: (i, 0))
    vec = pl.BlockSpec((1, D), lambda i: (0, 0))
    vshape = jax.ShapeDtypeStruct((1, D), F32)
    return _call(body, name, (S // ts,), [row, row, vec, vec, vec, row], [row, vec, vec, vec],
                 [jax.ShapeDtypeStruct((S, D), F32), vshape, vshape, vshape])(dh, x, gain, sc, sh, dx_in)


def _loss_head(x, target, gain, name, ts=512):
    S = x.shape[0]

    def body(x_ref, t_ref, g_ref, loss_ref, dx_ref, dg_ref):
        i = pl.program_id(0)
        xv = x_ref[...]
        g = g_ref[...]
        r = lax.rsqrt(jnp.mean(xv * xv, axis=1, keepdims=True) + EPS)
        xr = xv * r
        e = xr * g - t_ref[...]
        dy = e * (1.0 / D)
        u = dy * g
        dx_ref[...] = r * (u - xr * jnp.mean(xr * u, axis=1, keepdims=True))

        @pl.when(i == 0)
        def _():
            loss_ref[...] = jnp.zeros_like(loss_ref)
            dg_ref[...] = jnp.zeros_like(dg_ref)

        part = 0.5 * jnp.sum(jnp.mean(e * e, axis=1, keepdims=True), axis=0, keepdims=True)
        loss_ref[...] += jnp.broadcast_to(part, loss_ref.shape)
        dg_ref[...] += _colsum(dy * xr)

    row = pl.BlockSpec((ts, D), lambda i: (i, 0))
    vec = pl.BlockSpec((1, D), lambda i: (0, 0))
    return _call(body, name, (S // ts,), [row, row, vec],
                 [pl.BlockSpec((1, 128), lambda i: (0, 0)), row, vec],
                 [jax.ShapeDtypeStruct((1, 128), F32), jax.ShapeDtypeStruct((S, D), F32),
                  jax.ShapeDtypeStruct((1, D), F32)])(x, target, gain)


def _merge_specs(ts):
    o_spec = pl.BlockSpec((NH, ts, HD), lambda i: (0, i, 0))
    zg = pl.BlockSpec((ts, D), lambda i: (i, Z_GATE // D))
    za = pl.BlockSpec((ts, D), lambda i: (i, Z_BR // D))
    zb = pl.BlockSpec((ts, D), lambda i: (i, Z_BR // D + 1))
    row = pl.BlockSpec((ts, D), lambda i: (i, 0))
    gn = pl.BlockSpec((1, HD), lambda i: (0, 0))
    return o_spec, zg, za, zb, row, gn


def _merge_fwd(o, z, ob, gn, name, ts=256):
    S = ob.shape[0]

    def body(o_ref, zg_ref, za_ref, zb_ref, ob_ref, gn_ref, m_ref):
        for h in range(NH):
            sl = slice(h * HD, (h + 1) * HD)
            oh = o_ref[h]
            r = lax.rsqrt(jnp.mean(oh * oh, axis=1, keepdims=True) + EPS)
            gate = zg_ref[:, sl]
            oa = (oh * r * gn_ref[...]) * (gate * _sigmoid(gate))
            m = _sigmoid(za_ref[:, sl]) * oa + _sigmoid(zb_ref[:, sl]) * ob_ref[:, sl]
            m_ref[:, sl] = m.astype(m_ref.dtype)

    o_spec, zg, za, zb, row, gns = _merge_specs(ts)
    return _call(body, name, (S // ts,), [o_spec, zg, za, zb, row, gns], row,
                 jax.ShapeDtypeStruct((S, D), MXU))(o, z, z, z, ob, gn)


def _merge_bwd(dm, o, z, ob, gn, name, ts=256):
    S = ob.shape[0]

    def body(dm_ref, o_ref, zg_ref, za_ref, zb_ref, ob_ref, gn_ref,
             do_ref, dzg_ref, dob_ref, dza_ref, dzb_ref, dgn_ref):
        i = pl.program_id(0)
        gn_v = gn_ref[...]
        dgn = jnp.zeros((1, HD), F32)
        for h in range(NH):
            sl = slice(h * HD, (h + 1) * HD)
            dmh = dm_ref[:, sl]
            oh = o_ref[h]
            r = lax.rsqrt(jnp.mean(oh * oh, axis=1, keepdims=True) + EPS)
            ohr = oh * r
            on = ohr * gn_v
            gate = zg_ref[:, sl]
            sg = _sigmoid(gate)
            silu = gate * sg
            oa = on * silu
            ga = _sigmoid(za_ref[:, sl])
            gb = _sigmoid(zb_ref[:, sl])
            obh = ob_ref[:, sl]
            doa = dmh * ga
            dob_ref[:, sl] = dmh * gb
            dza_ref[:, sl] = (dmh * oa * ga * (1.0 - ga)).astype(dza_ref.dtype)
            dzb_ref[:, sl] = (dmh * obh * gb * (1.0 - gb)).astype(dzb_ref.dtype)
            don = doa * silu
            dzg_ref[:, sl] = (doa * on * (sg * (1.0 + gate * (1.0 - sg)))).astype(dzg_ref.dtype)
            dgn = dgn + _colsum(don * ohr)
            u = don * gn_v
            do_ref[h] = r * (u - ohr * jnp.mean(ohr * u, axis=1, keepdims=True))

        @pl.when(i == 0)
        def _():
            dgn_ref[...] = jnp.zeros_like(dgn_ref)

        dgn_ref[...] += dgn

    o_spec, zg, za, zb, row, gns = _merge_specs(ts)
    return _call(
        body, name, (S // ts,), [row, o_spec, zg, za, zb, row, gns],
        [o_spec, row, row, row, row, gns],
        [jax.ShapeDtypeStruct((NH, S, HD), F32), jax.ShapeDtypeStruct((S, D), MXU),
         jax.ShapeDtypeStruct((S, D), F32), jax.ShapeDtypeStruct((S, D), MXU),
         jax.ShapeDtypeStruct((S, D), MXU), jax.ShapeDtypeStruct((1, HD), F32)],
    )(dm, o, z, z, z, ob, gn)


GROWS = 256
GCH = GROWS // CH


def _gdn_prep_fwd(z, conv_w, alog_row, dtb_row, name, comm=None):
    S = z.shape[0]
    ts = GROWS
    scale = HD ** -0.5

    def body(z_ref, halo_ref, zab_ref, w_ref, al_ref, dt_ref, q_ref, k_ref, v_ref, gb_ref, buf):
        i = pl.program_id(0)
        buf[0:8, :] = jnp.where(i == 0, 0.0, halo_ref[...])
        buf[8:8 + ts, :] = z_ref[...]
        outs = (q_ref, k_ref, v_ref)
        for seg in range(3):
            cs = slice(seg * D, (seg + 1) * D)
            c = jnp.zeros((ts, D), F32)
            for j in range(4):
                c = c + w_ref[j:j + 1, cs] * buf[pl.ds(5 + j, ts), cs]
            s = c * _sigmoid(c)
            if seg == 2:
                outs[seg][...] = s
            else:
                mul = scale if seg == 0 else 1.0
                for h in range(NH):
                    sl = slice(h * HD, (h + 1) * HD)
                    sh = s[:, sl]
                    r = lax.rsqrt(_rowsum(sh * sh) + EPS)
                    outs[seg][:, sl] = sh * (r * mul)
        zab = zab_ref[...]
        lane = lax.broadcasted_iota(jnp.int32, zab.shape, 1)
        g = -jnp.exp(al_ref[...]) * _softplus(zab + dt_ref[...])
        ri = lax.broadcasted_iota(jnp.int32, (CH, CH), 0)
        ci = lax.broadcasted_iota(jnp.int32, (CH, CH), 1)
        incl = (ri >= ci).astype(F32)
        gcum = jnp.concatenate([_dot(incl, g[c * CH:(c + 1) * CH], HI) for c in range(ts // CH)], axis=0)
        gb_ref[...] = jnp.where(lane < NH, gcum, jnp.where(lane < 2 * NH, _sigmoid(zab), 0.0))

    row = pl.BlockSpec((ts, D), lambda i: (i, 0))
    vec = pl.BlockSpec((1, 128), lambda i: (0, 0))
    return _call(
        body, name, (S // ts,),
        [pl.BlockSpec((ts, 3 * D), lambda i: (i, 0)),
         pl.BlockSpec((8, 3 * D), lambda i: (jnp.maximum(i * (ts // 8) - 1, 0), 0)),
         pl.BlockSpec((ts, 128), lambda i: (i, Z_AB // 128)),
         _full((4, 3 * D)), vec, vec],
        [row, row, row, pl.BlockSpec((ts, 128), lambda i: (i, 0))],
        [jax.ShapeDtypeStruct((S, D), F32)] * 3 + [jax.ShapeDtypeStruct((S, 128), F32)],
        [pltpu.VMEM((ts + 8, 3 * D), F32)], comm=comm,
    )(z, z, z, conv_w, alog_row, dtb_row)


def _split(a):
    hi = a.astype(MXU)
    return hi, (a - hi.astype(F32)).astype(MXU)


def _dot3(a, b, dot=_dot):
    ah, al = _split(a)
    bh, bl = _split(b)
    return dot(ah, bh) + (dot(ah, bl) + dot(al, bh))


IROWS = 512
ICH = IROWS // CH


def _chunk_common(gbk, h, k):
    lane = lax.broadcasted_iota(jnp.int32, gbk.shape, 1)
    G = _rowsum(jnp.where(lane == h, gbk, 0.0))
    b_col = _rowsum(jnp.where(lane == h + NH, gbk, 0.0))
    ri = lax.broadcasted_iota(jnp.int32, (CH, CH), 0)
    ci = lax.broadcasted_iota(jnp.int32, (CH, CH), 1)
    incl = ri >= ci
    gc = jnp.broadcast_to(G, (CH, CH))
    decay = jnp.where(incl, jnp.exp(jnp.where(incl, gc - gc.T, 0.0)), 0.0)
    Gl = G[CH - 1:CH, :]
    kb = k * b_col
    return dict(b=b_col, ri=ri, ci=ci, incl=incl, strict=ri > ci, decay=decay, eG=jnp.exp(G),
                e2=jnp.exp(Gl - G), cd=jnp.exp(Gl), kb=kb, kk=_dot_nt(_mx(kb), _mx(k)))


def _gdn_intra_fwd(qn, kn, v, gb, name, comm=None):
    S = qn.shape[0]

    def body(q_ref, k_ref, v_ref, gb_ref, u_ref, w_ref, qd_ref, kd_ref, at_ref, t_ref, cd_ref):
        h = pl.program_id(1)
        rows = [slice(c * CH, (c + 1) * CH) for c in range(ICH)]
        ks = [k_ref[r, :] for r in rows]
        cms = [_chunk_common(gb_ref[r, :], h, k) for r, k in zip(rows, ks)]
        ps = [jnp.where(cm["strict"], cm["kk"] * cm["decay"], 0.0) for cm in cms]
        ts = [(cm["ri"] == cm["ci"]).astype(F32) - p for cm, p in zip(cms, ps)]
        for _ in range(5):
            ps = [_dot3(p, p) for p in ps]
            ts = [t + _dot3(t, p) for t, p in zip(ts, ps)]
        for c, (r, k, cm, t) in enumerate(zip(rows, ks, cms, ts)):
            rhs = jnp.concatenate([v_ref[r, :] * cm["b"], k * (cm["b"] * cm["eG"])], axis=1)
            sol = _dot3(t, rhs)
            u_ref[0, r, :] = sol[:, :HD]
            w_ref[0, r, :] = sol[:, HD:]
            t_ref[0, r, :] = t
        for c, (r, k, cm) in enumerate(zip(rows, ks, cms)):
            q = q_ref[r, :]
            qk = _dot_nt(_mx(q), _mx(k))
            at_ref[0, r, :] = jnp.where(cm["incl"], qk * cm["decay"], 0.0)
            qd_ref[0, r, :] = q * cm["eG"]
            kd_ref[0, r, :] = k * cm["e2"]
            cd_ref[0, c] = jnp.broadcast_to(cm["cd"], (8, 128))

    tok = pl.BlockSpec((IROWS, HD), lambda i, h: (i, h))
    hm = pl.BlockSpec((1, IROWS, HD), lambda i, h: (h, i, 0))
    hm64 = pl.BlockSpec((1, IROWS, CH), lambda i, h: (h, i, 0))
    big = jax.ShapeDtypeStruct((NH, S, HD), F32)
    sm = jax.ShapeDtypeStruct((NH, S, CH), F32)
    return _call(
        body, name, (S // IROWS, NH),
        [tok, tok, tok, pl.BlockSpec((IROWS, 128), lambda i, h: (i, 0))],
        [hm, hm, hm, hm, hm64, hm64, pl.BlockSpec((1, ICH, 8, 128), lambda i, h: (h, i, 0, 0))],
        [big, big, big, big, sm, sm, jax.ShapeDtypeStruct((NH, S // CH, 8, 128), F32)], comm=comm,
    )(qn, kn, v, gb)


def _scale_state(s, cd_tile):
    return (s.reshape(HD // 8, 8, HD) * cd_tile[None]).reshape(HD, HD)


def _gdn_scan_fwd(u, w, qd, kd, attn, cdt, name):
    S = u.shape[1]
    nblk = S // GROWS

    def body(u_ref, w_ref, qd_ref, kd_ref, at_ref, cd_ref, o_ref, vn_ref, st_ref, s_ref):
        i = pl.program_id(0)

        @pl.when(i == 0)
        def _():
            s_ref[...] = jnp.zeros_like(s_ref)

        def chunk(c, carry):
            r0 = pl.multiple_of(c * CH, CH)
            rows = pl.ds(r0, CH)
            H = range(NH)
            shs = [s_ref[h] for h in H]
            sbs = [_mx(sh) for sh in shs]
            ws = [_dot(_mx(w_ref[h, rows, :]), sbs[h]) for h in H]
            qs = [_dot(_mx(qd_ref[h, rows, :]), sbs[h]) for h in H]
            vns = [u_ref[h, rows, :] - ws[h] for h in H]
            vbs = [_mx(vn) for vn in vns]
            avs = [_dot(_mx(at_ref[h, rows, :]), vbs[h]) for h in H]
            kvs = [_dot_tn(_mx(kd_ref[h, rows, :]), vbs[h]) for h in H]
            for h in H:
                st_ref[h, c] = shs[h]
                vn_ref[h, rows, :] = vns[h]
                o_ref[h, rows, :] = qs[h] + avs[h]
                s_ref[h] = _scale_state(shs[h], cd_ref[h, c]) + kvs[h]
            return carry

        lax.fori_loop(0, GCH, chunk, 0)

    hm = pl.BlockSpec((NH, GROWS, HD), lambda i: (0, i, 0))
    hm64 = pl.BlockSpec((NH, GROWS, CH), lambda i: (0, i, 0))
    big = jax.ShapeDtypeStruct((NH, S, HD), F32)
    return _call(
        body, name, (nblk,),
        [hm, hm, hm, hm, hm64, pl.BlockSpec((NH, GCH, 8, 128), lambda i: (0, i, 0, 0))],
        [hm, hm, pl.BlockSpec((NH, GCH, HD, HD), lambda i: (0, i, 0, 0))],
        [big, big, jax.ShapeDtypeStruct((NH, S // CH, HD, HD), F32)],
        [pltpu.VMEM((NH, HD, HD), F32)],
    )(u, w, qd, kd, attn, cdt)


def _gdn_scan_bwd(do, w, qd, kd, attn, cdt, vn, st, name):
    S = do.shape[1]
    nblk = S // GROWS

    def body(do_ref, w_ref, qd_ref, kd_ref, at_ref, cd_ref, vn_ref, st_ref,
             dqd_ref, dkd_ref, dvn_ref, dw_ref, dat_ref, dcd_ref, ds_ref):
        i = pl.program_id(0)

        @pl.when(i == 0)
        def _():
            ds_ref[...] = jnp.zeros_like(ds_ref)

        def chunk(cc, carry):
            c = GCH - 1 - cc
            r0 = pl.multiple_of(c * CH, CH)
            rows = pl.ds(r0, CH)
            H = range(NH)
            dsps = [ds_ref[h] for h in H]
            shs = [st_ref[h, c] for h in H]
            dsbs = [_mx(a) for a in dsps]
            sbs = [_mx(a) for a in shs]
            dobs = [_mx(do_ref[h, rows, :]) for h in H]
            vbs = [_mx(vn_ref[h, rows, :]) for h in H]
            dvns = [_dot(_mx(kd_ref[h, rows, :]), dsbs[h]) + _dot_tn(_mx(at_ref[h, rows, :]), dobs[h]) for h in H]
            dvbs = [_mx(a) for a in dvns]
            dqds = [_dot_nt(dobs[h], sbs[h]) for h in H]
            dats = [_dot_nt(dobs[h], vbs[h]) for h in H]
            dkds = [_dot_nt(vbs[h], dsbs[h]) for h in H]
            dws = [_dot_nt(dvbs[h], sbs[h]) for h in H]
            qdos = [_dot_tn(_mx(qd_ref[h, rows, :]), dobs[h]) for h in H]
            wdvs = [_dot_tn(_mx(w_ref[h, rows, :]), dvbs[h]) for h in H]
            for h in H:
                dvn_ref[h, rows, :] = dvns[h]
                dqd_ref[h, rows, :] = dqds[h]
                dat_ref[h, rows, :] = dats[h]
                dkd_ref[h, rows, :] = dkds[h]
                dw_ref[h, rows, :] = -dws[h]
                dcd = jnp.sum(_rowsum(dsps[h] * shs[h]), axis=0, keepdims=True)
                dcd_ref[h, c] = jnp.broadcast_to(dcd, (8, 128))
                ds_ref[h] = _scale_state(dsps[h], cd_ref[h, c]) + qdos[h] - wdvs[h]
            return carry

        lax.fori_loop(0, GCH, chunk, 0)

    hm = pl.BlockSpec((NH, GROWS, HD), lambda i: (0, nblk - 1 - i, 0))
    hm64 = pl.BlockSpec((NH, GROWS, CH), lambda i: (0, nblk - 1 - i, 0))
    tile = pl.BlockSpec((NH, GCH, 8, 128), lambda i: (0, nblk - 1 - i, 0, 0))
    big = jax.ShapeDtypeStruct((NH, S, HD), F32)
    return _call(
        body, name, (nblk,),
        [hm, hm, hm, hm, hm64, tile, hm, pl.BlockSpec((NH, GCH, HD, HD), lambda i: (0, nblk - 1 - i, 0, 0))],
        [hm, hm, hm, hm, hm64, tile],
        [big, big, big, big, jax.ShapeDtypeStruct((NH, S, CH), F32),
         jax.ShapeDtypeStruct((NH, S // CH, 8, 128), F32)],
        [pltpu.VMEM((NH, HD, HD), F32)],
    )(do, w, qd, kd, attn, cdt, vn, st)


def _gdn_intra_bwd(qn, kn, v, gb, u, w, tmat, dqd, dkd, du, dw, dattn, dcdt, name):
    S = qn.shape[0]

    def body(q_ref, k_ref, v_ref, gb_ref, u_ref, w_ref, t_ref, dqd_ref, dkd_ref, du_ref, dw_ref,
             dat_ref, dcd_ref, dq_ref, dk_ref, dv_ref, dgb_ref):
        h = pl.program_id(1)
        rows = [slice(c * CH, (c + 1) * CH) for c in range(ICH)]
        ks = [k_ref[r, :] for r in rows]
        cms = [_chunk_common(gb_ref[r, :], h, k) for r, k in zip(rows, ks)]
        sols = [jnp.concatenate([u_ref[0, r, :], w_ref[0, r, :]], axis=1) for r in rows]
        drhss = [_dot3(t_ref[0, r, :], jnp.concatenate([du_ref[0, r, :], dw_ref[0, r, :]], axis=1), _dot_tn)
                 for r in rows]
        das = [-_dot3(drhs, sol, _dot_nt) for drhs, sol in zip(drhss, sols)]
        for c, (r, k, cm, drhs, da) in enumerate(zip(rows, ks, cms, drhss, das)):
            q, vv = q_ref[r, :], v_ref[r, :]
            decay, eG, e2, b = cm["decay"], cm["eG"], cm["e2"], cm["b"]
            dru, drw = drhs[:, :HD], drhs[:, HD:]
            dv_ref[r, :] = dru * b
            s_w = _rowsum(drw * k)
            dbeta = _rowsum(dru * vv) + s_w * eG
            deg = s_w * b
            dk = drw * (b * eG)
            dkk = jnp.where(cm["strict"], da * decay, 0.0)
            ddec = jnp.where(cm["strict"], da * cm["kk"], 0.0)
            dkkb = _mx(dkk)
            dkb = _dot(dkkb, _mx(k))
            dk = dk + _dot_tn(dkkb, _mx(cm["kb"])) + dkb * b
            dbeta = dbeta + _rowsum(dkb * k)
            dat = jnp.where(cm["incl"], dat_ref[0, r, :], 0.0)
            qk = _dot_nt(_mx(q), _mx(k))
            dqk = _mx(dat * decay)
            ddec = ddec + dat * qk
            dqd = dqd_ref[0, r, :]
            dkd = dkd_ref[0, r, :]
            dq_ref[r, :] = _dot(dqk, _mx(k)) + dqd * eG
            dk_ref[r, :] = dk + _dot_tn(dqk, _mx(q)) + dkd * e2
            deg = deg + _rowsum(dqd * q)
            t2 = _rowsum(dkd * k) * e2
            dgl = jnp.sum(t2, axis=0, keepdims=True) + dcd_ref[0, c][0:1, 0:1] * cm["cd"]
            dd = ddec * decay
            dG = deg * eG - t2 + _rowsum(dd) - _rowsum(dd.T)
            rowi = lax.broadcasted_iota(jnp.int32, (CH, 1), 0)
            dG = dG + jnp.where(rowi == CH - 1, dgl, 0.0)
            lane = lax.broadcasted_iota(jnp.int32, (CH, 128), 1)
            dgb_ref[0, r, :] = jnp.where(lane == h, dG, 0.0) + jnp.where(lane == h + NH, dbeta, 0.0)

    tok = pl.BlockSpec((IROWS, HD), lambda i, h: (i, h))
    hm = pl.BlockSpec((1, IROWS, HD), lambda i, h: (h, i, 0))
    hm64 = pl.BlockSpec((1, IROWS, CH), lambda i, h: (h, i, 0))
    tile = pl.BlockSpec((1, ICH, 8, 128), lambda i, h: (h, i, 0, 0))
    tokout = jax.ShapeDtypeStruct((S, D), F32)
    return _call(
        body, name, (S // IROWS, NH),
        [tok, tok, tok, pl.BlockSpec((IROWS, 128), lambda i, h: (i, 0)), hm, hm, hm64,
         hm, hm, hm, hm, hm64, tile],
        [tok, tok, tok, pl.BlockSpec((1, IROWS, 128), lambda i, h: (h, i, 0))],
        [tokout, tokout, tokout, jax.ShapeDtypeStruct((NH, S, 128), F32)],
    )(qn, kn, v, gb, u, w, tmat, dqd, dkd, du, dw, dattn, dcdt)


def _gdn_prep_bwd(z, dqn, dkn, dv, dgb, conv_w, alog_row, dtb_row, name):
    S = z.shape[0]
    ts = GROWS
    nblk = S // ts
    scale = HD ** -0.5
    tb = ts // 8

    def body(z_ref, hp_ref, hn_ref, zab_ref, dq_ref, dqn_ref, dk_ref, dkn_ref, dv_ref, dvn_ref,
             dgb_ref, w_ref, al_ref, dt_ref, dz_ref, dzab_ref, dcw_ref, dvec_ref, buf, dybuf, dcbuf):
        i = pl.program_id(0)
        last = i == nblk - 1

        @pl.when(i == 0)
        def _():
            dcw_ref[...] = jnp.zeros_like(dcw_ref)
            dvec_ref[...] = jnp.zeros_like(dvec_ref)

        buf[0:8, :] = jnp.where(i == 0, 0.0, hp_ref[...])
        buf[8:8 + ts, :] = z_ref[...]
        buf[8 + ts:16 + ts, :] = hn_ref[...]
        rowi = lax.broadcasted_iota(jnp.int32, (ts + 8, 1), 0)
        live = jnp.logical_or(rowi < ts, jnp.logical_not(last))
        dys = ((dq_ref, dqn_ref), (dk_ref, dkn_ref), (dv_ref, dvn_ref))
        for seg in range(3):
            cs = slice(seg * D, (seg + 1) * D)
            dybuf[0:ts, :] = dys[seg][0][...]
            dybuf[ts:ts + 8, :] = dys[seg][1][...]
            taps = [buf[pl.ds(5 + j, ts + 8), cs] for j in range(4)]
            c = jnp.zeros((ts + 8, D), F32)
            for j in range(4):
                c = c + w_ref[j:j + 1, cs] * taps[j]
            sg = _sigmoid(c)
            s = c * sg
            dsilu = sg * (1.0 + c * (1.0 - sg))
            if seg == 2:
                dcbuf[...] = jnp.where(live, dybuf[...] * dsilu, 0.0)
            else:
                mul = scale if seg == 0 else 1.0
                for h in range(NH):
                    sl = slice(h * HD, (h + 1) * HD)
                    sh = s[:, sl]
                    dy = dybuf[:, sl]
                    r = lax.rsqrt(_rowsum(sh * sh) + EPS)
                    shr = sh * r
                    ds = (mul * r) * (dy - shr * _rowsum(shr * dy))
                    dcbuf[:, sl] = jnp.where(live, ds * dsilu[:, sl], 0.0)
            dx = jnp.zeros((ts, D), F32)
            for j in range(4):
                dcw_ref[j:j + 1, cs] += _colsum(dcbuf[0:ts, :] * taps[j][0:ts])
                dx = dx + w_ref[j:j + 1, cs] * dcbuf[pl.ds(3 - j, ts), :]
            dz_ref[:, cs] = dx.astype(dz_ref.dtype)
        dgbs = dgb_ref[0]
        for h in range(1, NH):
            dgbs = dgbs + dgb_ref[h]
        ri = lax.broadcasted_iota(jnp.int32, (CH, CH), 0)
        ci = lax.broadcasted_iota(jnp.int32, (CH, CH), 1)
        rev = (ci >= ri).astype(F32)
        dgrev = jnp.concatenate([_dot(rev, dgbs[c * CH:(c + 1) * CH], HI) for c in range(ts // CH)], axis=0)
        lane0 = lax.broadcasted_iota(jnp.int32, dgbs.shape, 1)
        dgbs = jnp.where(lane0 < NH, dgrev, dgbs)
        zab = zab_ref[...]
        lane = lax.broadcasted_iota(jnp.int32, zab.shape, 1)
        xx = zab + dt_ref[...]
        ea = jnp.exp(al_ref[...])
        g = -ea * _softplus(xx)
        da = dgbs * (-ea) * _sigmoid(xx)
        beta = _sigmoid(zab)
        db = dgbs * beta * (1.0 - beta)
        is_a = lane < NH
        dzab = jnp.where(is_a, da, jnp.where(lane < 2 * NH, db, 0.0))
        dzab_ref[:, 0:128] = dzab.astype(dzab_ref.dtype)
        dzab_ref[:, 128:512] = jnp.zeros((ts, 384), dzab_ref.dtype)
        dvec_ref[0:1, :] += _colsum(jnp.where(is_a, dgbs * g, 0.0))
        dvec_ref[1:2, :] += _colsum(jnp.where(is_a, da, 0.0))

    z3 = pl.BlockSpec((ts, 3 * D), lambda i: (i, 0))
    row = pl.BlockSpec((ts, D), lambda i: (i, 0))
    nxt = pl.BlockSpec((8, D), lambda i: (jnp.minimum((i + 1) * tb, S // 8 - 1), 0))
    vec = pl.BlockSpec((1, 128), lambda i: (0, 0))
    return _call(
        body, name, (nblk,),
        [z3,
         pl.BlockSpec((8, 3 * D), lambda i: (jnp.maximum(i * tb - 1, 0), 0)),
         pl.BlockSpec((8, 3 * D), lambda i: (jnp.minimum((i + 1) * tb, S // 8 - 1), 0)),
         pl.BlockSpec((ts, 128), lambda i: (i, Z_AB // 128)),
         row, nxt, row, nxt, row, nxt,
         pl.BlockSpec((NH, ts, 128), lambda i: (0, i, 0)),
         _full((4, 3 * D)), vec, vec],
        [z3, pl.BlockSpec((ts, 512), lambda i: (i, 0)), _full((8, 3 * D)), _full((8, 128))],
        [jax.ShapeDtypeStruct((S, 3 * D), MXU), jax.ShapeDtypeStruct((S, 512), MXU),
         jax.ShapeDtypeStruct((8, 3 * D), F32), jax.ShapeDtypeStruct((8, 128), F32)],
        [pltpu.VMEM((ts + 16, 3 * D), F32), pltpu.VMEM((ts + 8, D), F32), pltpu.VMEM((ts + 8, D), F32)],
    )(z, z, z, z, dqn, dqn, dkn, dkn, dv, dv, dgb, conv_w, alog_row, dtb_row)


def _bias_index():
    u = np.arange(FW)[None, :]
    s = np.arange(3)[:, None]
    return np.clip(KWIN - 1 - u - QB * s, -256, 256) + 256


def _bias_vec(rel_bias_pad, onehot, name):
    def body(rb_ref, e_ref, o_ref):
        o_ref[:, 0, :] = _dot_nt(rb_ref[...], e_ref[0], HI)

    return _call(body, name, (3,),
                 [_full((NH, 640)), pl.BlockSpec((1, FW, 640), lambda s: (s, 0, 0))],
                 pl.BlockSpec((NH, 1, FW), lambda s: (s, 0, 0)),
                 jax.ShapeDtypeStruct((3 * NH, 1, FW), F32))(rel_bias_pad, onehot)


def _att_window(i):
    return pl.multiple_of(jnp.maximum(i * QB - PAST * CH, 0), QB)


def _bias_mask(fvec, name):
    def body(f_ref, o_ref):
        i = 2 - pl.program_id(0) // NH
        ws = jnp.maximum(i * QB - PAST * CH, 0)
        fb = jnp.broadcast_to(f_ref[0], (QB, FW))
        bias = pltpu.roll(fb, FW - 255, 1, stride=1, stride_axis=0)[:, :KWIN]
        qc = (i * QB + lax.broadcasted_iota(jnp.int32, (QB, KWIN), 0)) // CH
        kc = (ws + lax.broadcasted_iota(jnp.int32, (QB, KWIN), 1)) // CH
        o_ref[0] = jnp.where((kc <= qc) & (kc >= qc - PAST), bias, MASKED)

    return _call(body, name, (3 * NH,), [pl.BlockSpec((1, 1, FW), lambda j: (j, 0, 0))],
                 pl.BlockSpec((1, QB, KWIN), lambda j: (j, 0, 0)),
                 jax.ShapeDtypeStruct((3 * NH, QB, KWIN), F32))(fvec)


def _att_scores(q_ref, k_ref, bm_ref, i):
    ws = _att_window(i)
    q = _mx(q_ref[...] * (HD ** -0.5))
    kw = _mx(k_ref[pl.ds(ws, KWIN), :])
    return q, kw, ws, _dot_nt(q, kw) + bm_ref[0]


def _att_specs(S):
    c0 = Z_ATT // HD
    q = pl.BlockSpec((QB, HD), lambda h, i: (i, c0 + h))
    k = pl.BlockSpec((S, HD), lambda h, i: (0, c0 + NH + h))
    v = pl.BlockSpec((S, HD), lambda h, i: (0, c0 + 2 * NH + h))
    bm = pl.BlockSpec((1, QB, KWIN), lambda h, i: (jnp.maximum(2 - i, 0) * NH + h, 0, 0))
    tok = pl.BlockSpec((QB, HD), lambda h, i: (i, h))
    return q, k, v, bm, tok


def _att_fwd(z, bmask, name, comm=None):
    S = z.shape[0]

    def body(q_ref, k_ref, v_ref, bm_ref, o_ref, lse_ref):
        _, _, ws, s = _att_scores(q_ref, k_ref, bm_ref, pl.program_id(1))
        m = jnp.max(s, axis=1, keepdims=True)
        p = jnp.exp(s - m)
        l = _rowsum(p)
        o_ref[...] = _dot(_mx(p), _mx(v_ref[pl.ds(ws, KWIN), :])) * (1.0 / l)
        lse_ref[...] = jnp.broadcast_to(m + jnp.log(l), (QB, HD))

    q, k, v, bm, tok = _att_specs(S)
    shp = jax.ShapeDtypeStruct((S, D), F32)
    return _call(body, name, (NH, S // QB), [q, k, v, bm], [tok, tok], [shp, shp], comm=comm)(z, z, z, bmask)


def _att_bwd(z, bmask, ob, lse, dob, name):
    S = z.shape[0]
    nq = S // QB

    def body(q_ref, k_ref, v_ref, bm_ref, o_ref, lse_ref, do_ref, dq_ref, dk_ref, dv_ref, db_ref, dk_acc, dv_acc):
        i = pl.program_id(1)
        q, kw, ws, s = _att_scores(q_ref, k_ref, bm_ref, i)
        p = jnp.exp(s - lse_ref[:, 0:1])
        do = do_ref[...]
        dob16 = _mx(do)
        dp = _dot_nt(dob16, _mx(v_ref[pl.ds(ws, KWIN), :]))
        ds = p * (dp - _rowsum(do * o_ref[...]))
        dsb = _mx(ds)
        dq_ref[...] = (_dot(dsb, kw) * (HD ** -0.5)).astype(dq_ref.dtype)

        @pl.when(i == 0)
        def _():
            dk_acc[...] = jnp.zeros_like(dk_acc)
            dv_acc[...] = jnp.zeros_like(dv_acc)

        dk_acc[pl.ds(ws, KWIN), :] += _dot_tn(dsb, q)
        dv_acc[pl.ds(ws, KWIN), :] += _dot_tn(_mx(p), dob16)

        @pl.when(i == nq - 1)
        def _():
            dk_ref[...] = dk_acc[...].astype(dk_ref.dtype)
            dv_ref[...] = dv_acc[...].astype(dv_ref.dtype)

        @pl.when(i <= 2)
        def _():
            db_ref[0] = ds

        @pl.when(i > 2)
        def _():
            db_ref[0] += ds

    q, k, v, bm, tok = _att_specs(S)
    acc = pl.BlockSpec((S, HD), lambda h, i: (0, h))
    half = jax.ShapeDtypeStruct((S, D), MXU)
    return _call(
        body, name, (NH, nq), [q, k, v, bm, tok, tok, tok], [tok, acc, acc, bm],
        [half, half, half, jax.ShapeDtypeStruct((3 * NH, QB, KWIN), F32)],
        [pltpu.VMEM((S, HD), F32), pltpu.VMEM((S, HD), F32)],
    )(z, z, z, bmask, ob, lse, dob)


def _bias_fold(dbias, onehot, name):
    def body(db_ref, e_ref, o_ref):
        j = pl.program_id(0)
        h = j % NH
        x = jnp.concatenate([db_ref[0], jnp.zeros((QB, FW - KWIN), F32)], axis=1)
        half = QB // 2
        while half >= 8:
            x = x[:half] + pltpu.roll(x[half:2 * half], FW - half, 1)
            half //= 2
        df = jnp.zeros((1, FW), F32)
        for r in range(8):
            df = df + pltpu.roll(x[r:r + 1], 255 - r, 1)
        contrib = _dot(df, e_ref[0], HI)
        rowh = lax.broadcasted_iota(jnp.int32, (NH, 640), 0)

        @pl.when(j == 0)
        def _():
            o_ref[...] = jnp.zeros_like(o_ref)

        o_ref[...] += jnp.where(rowh == h, contrib, 0.0)

    return _call(
        body, name, (3 * NH,),
        [pl.BlockSpec((1, QB, KWIN), lambda j: (j, 0, 0)),
         pl.BlockSpec((1, FW, 640), lambda j: (j // NH, 0, 0))],
        _full((NH, 640)), jax.ShapeDtypeStruct((NH, 640), F32),
    )(dbias, onehot)


ADA_SHARD = 6 * D // NDEV


def _ada_mod(c_all, w_ada, b_shard, name):
    def body(c_ref, w_ref, b_ref, o_ref):
        cv = c_ref[...]
        ca = cv * _sigmoid(cv)
        o_ref[0] = _dot(_mx(ca), _mx(w_ref[0])) + b_ref[0]

    return _call(
        body, name, (DEPTH,),
        [_full((NDEV, D)), pl.BlockSpec((1, D, ADA_SHARD), lambda l: (l, 0, 0)),
         pl.BlockSpec((1, 1, ADA_SHARD), lambda l: (l, 0, 0))],
        pl.BlockSpec((1, NDEV, ADA_SHARD), lambda l: (l, 0, 0)),
        jax.ShapeDtypeStruct((DEPTH, NDEV, ADA_SHARD), F32),
    )(c_all, w_ada, b_shard.reshape(DEPTH, 1, ADA_SHARD))


def _adam(g, w, m, v):
    m = ADAM_B1 * m + (1.0 - ADAM_B1) * g
    v = ADAM_B2 * v + (1.0 - ADAM_B2) * jnp.square(g)
    m_hat = m / (1.0 - ADAM_B1 ** ADAM_STEP)
    v_hat = v / (1.0 - ADAM_B2 ** ADAM_STEP)
    delta = -ADAM_LR * (m_hat / (jnp.sqrt(v_hat) + ADAM_EPS) + ADAM_WD * w)
    return delta, m, v


def _wada_adamw(c_all_t, dmod, w, m, v, name):
    def body(c_ref, d_ref, w_ref, m_ref, v_ref, g_ref, dl_ref, mo_ref, vo_ref):
        cv = c_ref[...]
        ca = cv * _sigmoid(cv)
        g = _dot(ca, d_ref[0], HI)
        g_ref[0] = g
        dl_ref[0], mo_ref[0], vo_ref[0] = _adam(g, w_ref[0], m_ref[0], v_ref[0])

    blk = pl.BlockSpec((1, D, ADA_SHARD), lambda l: (l, 0, 0))
    shp = jax.ShapeDtypeStruct((DEPTH, D, ADA_SHARD), F32)
    return _call(
        body, name, (DEPTH,),
        [_full((D, NDEV)), pl.BlockSpec((1, NDEV, ADA_SHARD), lambda l: (l, 0, 0)), blk, blk, blk],
        [blk] * 4, [shp] * 4,
    )(c_all_t, dmod, w, m, v)


def _adamw_reduce(parts, w, m, v, name, tr):
    P, R, C = parts.shape

    def body(p_ref, w_ref, m_ref, v_ref, g_ref, dl_ref, mo_ref, vo_ref):
        g = p_ref[0].astype(F32)
        for k in range(1, P):
            g = g + p_ref[k].astype(F32)
        g_ref[...] = g
        dl_ref[...], mo_ref[...], vo_ref[...] = _adam(g, w_ref[...], m_ref[...], v_ref[...])

    blk = pl.BlockSpec((tr, C), lambda i: (i, 0))
    shp = jax.ShapeDtypeStruct((R, C), F32)
    return _call(body, name, (R // tr,), [pl.BlockSpec((P, tr, C), lambda i: (0, i, 0)), blk, blk, blk],
                 [blk] * 4, [shp] * 4)(parts, w, m, v)


def _sum_parts(parts, name):
    P, R, C = parts.shape

    def body(p_ref, o_ref):
        g = p_ref[0]
        for k in range(1, P):
            g = g + p_ref[k]
        o_ref[...] = g

    return _call(body, name, (1,), [_full((P, R, C))], _full((R, C)),
                 jax.ShapeDtypeStruct((R, C), F32))(parts)


def _pack_rows(vecs, width=1024):
    flat = jnp.concatenate([a.reshape(-1) for a in vecs])
    n = flat.shape[0]
    rows = -(-n // width)
    rows = -(-rows // 8) * 8
    return jnp.pad(flat, (0, rows * width - n)).reshape(rows, width)


def _unpack_rows(packed, shapes):
    flat = packed.reshape(-1)
    out, off = [], 0
    for s in shapes:
        n = int(np.prod(s)) if len(s) else 1
        out.append(flat[off:off + n].reshape(s))
        off += n
    return out


BIG = ("w_in", "w_out", "w_ff_in", "w_ff_out")


def _z_weights(g_in):
    w = jnp.transpose(g_in, (1, 0, 2)).reshape(D, IN_W)
    return jnp.concatenate([w[:, :Z_ATT], w[:, Z_ATT + 2 * NH:], w[:, Z_ATT:Z_ATT + 2 * NH],
                            jnp.zeros((D, ZW - IN_W), MXU)], axis=-1)


def _cols_to_owners(a):
    return jnp.transpose(a.astype(MXU).reshape(a.shape[0], NDEV, -1), (1, 0, 2))


def _rows_to_owners(a):
    return a.astype(MXU).reshape(NDEV, -1, a.shape[1])


def _forward_layer(x, mod_l, p, shard, next_w_in):
    sh1, sc1, gt1, sh2, sc2, gt2 = [mod_l[k][None] for k in range(6)]
    P = functools.partial
    gather = lambda a: None if a is None else ("gather", a)
    h = _modnorm_fwd(x, p["norm_mix"], sc1, sh1, "norm_mix_fwd")
    z, g_in = _hosted(P(_mm_nn, h, p["wz"], "in_proj", tm=2048, tn=512), gather(next_w_in))
    (qn, kn, v, gb), g_w1 = _hosted(P(_gdn_prep_fwd, z, p["conv_w"], p["alog"], p["dtb"], "gdn_prep_fwd"),
                                    gather(shard["w_ff_in"]))
    (u, w, qd, kd, attn, tmat, cdt), g_w2 = _hosted(P(_gdn_intra_fwd, qn, kn, v, gb, "gdn_intra_fwd"),
                                                    gather(shard["w_ff_out"]))
    o, vn, st = _gdn_scan_fwd(u, w, qd, kd, attn, cdt, "gdn_scan_fwd")
    (ob, lse), g_out = _hosted(P(_att_fwd, z, p["bmask"], "att_fwd"), gather(shard["w_out"]))
    wout = g_out.reshape(D, D)
    w1 = jnp.transpose(g_w1, (1, 0, 2)).reshape(D, DFF)
    w2 = g_w2.reshape(DFF, D)
    m = _merge_fwd(o, z, ob, p["gdn_norm"], "merge_fwd")
    x1, h2 = _mm_nn(m, wout, "out_proj", mode="resid_norm", res=x, gate=gt1, norm=(p["norm_mlp"], sc2, sh2))
    a, r = _mm_nn(h2, w1, "ff_in", mode="relu2")
    x2 = _mm_nn(r, w2, "ff_out", mode="resid", res=x1, gate=gt2)
    saved = dict(x=x, h=h, z=z, qn=qn, kn=kn, v=v, gb=gb, u=u, w=w, qd=qd, kd=kd, attn=attn,
                 tmat=tmat, cdt=cdt, o=o, vn=vn, st=st, ob=ob, lse=lse, m=m, x1=x1, h2=h2, a=a, r=r,
                 wout=wout, w1=w1, w2=w2)
    return x2, saved, g_in


def _backward_layer(dx2, mod_l, p, s, onehot):
    sh1, sc1, gt1, sh2, sc2, gt2 = [mod_l[k][None] for k in range(6)]
    P = functools.partial
    dw2, dgt2 = _mm_tn(s["r"], dx2, "ff_out_dw", gate=gt2, w=s["w2"])
    da, r_w2 = _hosted(P(_mm_nt, dx2, s["w2"], "ff_out_dx", gate=gt2, drelu=s["a"]), ("a2a", _rows_to_owners(dw2)))
    dw1 = _mm_tn(s["h2"], da, "ff_in_dw")
    dh2, r_w1 = _hosted(P(_mm_nt, da, s["w1"], "ff_in_dx"), ("a2a", _cols_to_owners(dw1)))
    dx1, dsc2, dsh2, dnmlp = _modnorm_bwd(dh2, s["x1"], p["norm_mlp"], sc2, sh2, dx2, "norm_mlp_bwd")
    dwout, dgt1 = _mm_tn(s["m"], dx1, "out_proj_dw", gate=gt1, w=s["wout"])
    dm, r_out = _hosted(P(_mm_nt, dx1, s["wout"], "out_proj_dx", gate=gt1), ("a2a", _rows_to_owners(dwout)))
    do, dzg, dob, dza, dzb, dgn = _merge_bwd(dm, s["o"], s["z"], s["ob"], p["gdn_norm"], "merge_bwd")
    dq_att, dk_att, dv_att, dbias = _att_bwd(s["z"], p["bmask"], s["ob"], s["lse"], dob, "att_bwd")
    drb = _bias_fold(dbias, onehot, "rel_bias_fold")[:, :513]
    dqd, dkd, dvn, dw, dattn, dcdt = _gdn_scan_bwd(do, s["w"], s["qd"], s["kd"], s["attn"], s["cdt"],
                                                   s["vn"], s["st"], "gdn_scan_bwd")
    dqn, dkn, dv, dgb = _gdn_intra_bwd(s["qn"], s["kn"], s["v"], s["gb"], s["u"], s["w"], s["tmat"],
                                       dqd, dkd, dvn, dw, dattn, dcdt, "gdn_intra_bwd")
    dzq, dzab, dcw, dvec = _gdn_prep_bwd(s["z"], dqn, dkn, dv, dgb, p["conv_w"], p["alog"], p["dtb"],
                                         "gdn_prep_bwd")
    dz = (dzq, dzg, dq_att, dk_att, dv_att, dza, dzb, dzab)
    dwz = _in_proj_dw(s["h"], dz, "in_proj_dw")
    dw_in = jnp.concatenate([dwz[:, :Z_ATT], dwz[:, Z_AB:Z_AB + 2 * NH], dwz[:, Z_ATT:Z_AB]], axis=1)
    dh, r_in = _hosted(P(_in_proj_dx, dz, p["wz"], "in_proj_dx"), ("a2a", _cols_to_owners(dw_in)))
    dx, dsc1, dsh1, dnmix = _modnorm_bwd(dh, s["x"], p["norm_mix"], sc1, sh1, dx1, "norm_mix_bwd")
    grads = dict(norm_mix=dnmix[0], norm_mlp=dnmlp[0], conv_w=dcw[:4], a_log=dvec[0, :NH], dt_bias=dvec[1, :NH],
                 gdn_norm=dgn[0], rel_bias=drb, mod=jnp.concatenate([dsh1, dsc1, dgt1, dsh2, dsc2, dgt2], axis=1)[0])
    return dx, grads, dict(w_in=r_in, w_out=r_out, w_ff_in=r_w1, w_ff_out=r_w2)


def _bias_onehot():
    return (jnp.asarray(_bias_index())[:, :, None] == jnp.arange(640)[None, None, :]).astype(F32)


def _layer_params(l, conv_full, norm_mix, norm_mlp, a_log, dt_bias, gdn_norm, rel_bias, onehot):
    pad = lambda a: jnp.pad(a, (0, 128 - NH))[None]
    fvec = _bias_vec(jnp.pad(rel_bias[l], ((0, 0), (0, 640 - rel_bias.shape[2]))), onehot, "rel_bias_vec")
    return dict(conv_w=conv_full[l], norm_mix=norm_mix[l][None], norm_mlp=norm_mlp[l][None], alog=pad(a_log[l]),
                dtb=pad(dt_bias[l]), gdn_norm=gdn_norm[l][None], bmask=_bias_mask(fvec, "rel_bias_mask"))


def _local_step(x, target, mod, small, shards, final_norm, onehot):
    L = len(small)
    g_in = _all_gather(shards[0]["w_in"], "gather_w_in")
    saved, params = [], []
    for l in range(L):
        params.append({**small[l], "wz": _z_weights(g_in)})
        x, sv, g_in = _forward_layer(x, mod[l].reshape(6, D), params[l], shards[l],
                                     shards[l + 1]["w_in"] if l + 1 < L else None)
        saved.append(sv)
    loss, dx, dfn = _loss_head(x, target, final_norm[None], "loss_head")
    grads, recv = [None] * L, [None] * L
    for l in reversed(range(L)):
        dx, grads[l], recv[l] = _backward_layer(dx, mod[l].reshape(6, D), params[l], saved[l], onehot)
    return loss, dx, grads, dfn[0], recv


SMALL = ("b_ada", "norm_mix", "norm_mlp", "a_log", "dt_bias", "gdn_norm", "rel_bias", "final_norm")


def kernel(x, c, w_ada, b_ada, norm_mix, norm_mlp, w_in, conv_w, a_log, dt_bias, gdn_norm, rel_bias, w_out, w_ff_in, w_ff_out, final_norm, loss_target, m_w_ada, m_b_ada, m_norm_mix, m_norm_mlp, m_w_in, m_conv_w, m_a_log, m_dt_bias, m_gdn_norm, m_rel_bias, m_w_out, m_w_ff_in, m_w_ff_out, m_final_norm, v_w_ada, v_b_ada, v_norm_mix, v_norm_mlp, v_w_in, v_conv_w, v_a_log, v_dt_bias, v_gdn_norm, v_rel_bias, v_w_out, v_w_ff_in, v_w_ff_out, v_final_norm):
    W = dict(w_ada=w_ada, b_ada=b_ada, norm_mix=norm_mix, norm_mlp=norm_mlp, w_in=w_in, conv_w=conv_w,
             a_log=a_log, dt_bias=dt_bias, gdn_norm=gdn_norm, rel_bias=rel_bias, w_out=w_out,
             w_ff_in=w_ff_in, w_ff_out=w_ff_out, final_norm=final_norm)
    Mo = dict(w_ada=m_w_ada, b_ada=m_b_ada, norm_mix=m_norm_mix, norm_mlp=m_norm_mlp, w_in=m_w_in,
              conv_w=m_conv_w, a_log=m_a_log, dt_bias=m_dt_bias, gdn_norm=m_gdn_norm, rel_bias=m_rel_bias,
              w_out=m_w_out, w_ff_in=m_w_ff_in, w_ff_out=m_w_ff_out, final_norm=m_final_norm)
    Vo = dict(w_ada=v_w_ada, b_ada=v_b_ada, norm_mix=v_norm_mix, norm_mlp=v_norm_mlp, w_in=v_w_in,
              conv_w=v_conv_w, a_log=v_a_log, dt_bias=v_dt_bias, gdn_norm=v_gdn_norm, rel_bias=v_rel_bias,
              w_out=v_w_out, w_ff_in=v_w_ff_in, w_ff_out=v_w_ff_out, final_norm=v_final_norm)
    L = w_in.shape[0]
    me = _flat(_mesh_pos())
    cshard = conv_w.shape[2]

    small_in = _all_gather(_pack_rows([c, conv_w]), "gather_c_conv")
    c_all = small_in[:, 0, :]
    conv_full = small_in.reshape(NDEV, -1)[:, D:D + L * 4 * cshard].reshape(NDEV, L, 4, cshard)
    conv_full = jnp.transpose(conv_full, (1, 2, 0, 3)).reshape(L, 4, NDEV * cshard)

    b_shard = lax.dynamic_slice_in_dim(b_ada, me * ADA_SHARD, ADA_SHARD, axis=1)
    mod_all = _all_gather(_ada_mod(c_all, w_ada, b_shard, "ada_mod"), "gather_mod")
    mod = lax.dynamic_index_in_dim(mod_all, me, axis=2, keepdims=False)
    mod = jnp.transpose(mod, (1, 0, 2)).reshape(L, 6 * D)

    onehot = _bias_onehot()
    small = [_layer_params(l, conv_full, norm_mix, norm_mlp, a_log, dt_bias, gdn_norm, rel_bias, onehot)
             for l in range(L)]
    shards = [{n: W[n][l].astype(MXU) for n in BIG} for l in range(L)]
    loss, dx, grads, dfn, recv = _local_step(x[0], loss_target[0], mod, small, shards, final_norm, onehot)

    def stack(name):
        return jnp.stack([g[name] for g in grads])

    small_names = ("mod", "norm_mix", "norm_mlp", "a_log", "dt_bias", "gdn_norm", "rel_bias")
    small_parts = [stack(n) for n in small_names] + [dfn, stack("conv_w"), loss[0, 0:1]]
    small_shapes = [a.shape for a in small_parts]
    gathered = _all_gather(_pack_rows(small_parts), "gather_small_grads")
    total = _unpack_rows(_sum_parts(gathered, "sum_small_grads"), small_shapes)
    tot = dict(zip(small_names + ("final_norm", "conv_w", "loss"), total))
    tot["b_ada"] = tot.pop("mod")
    tot["conv_w"] = lax.dynamic_slice_in_dim(tot["conv_w"], me * cshard, cshard, axis=2)

    out_g, out_d, out_m, out_v = {}, {}, {}, {}
    names = SMALL + ("conv_w",)
    shapes = [W[n].shape for n in names]
    packed = [_pack_rows([src[n] for n in names])[None] if src is tot else _pack_rows([src[n] for n in names])
              for src in (tot, W, Mo, Vo)]
    res = _adamw_reduce(*packed, "adamw_small", tr=8)
    for dst, arr in zip((out_g, out_d, out_m, out_v), res):
        dst.update(zip(names, _unpack_rows(arr, shapes)))

    dmod_all = gathered.reshape(NDEV, -1)[:, :L * 6 * D].reshape(NDEV, L, 6 * D)
    dmod_mine = jnp.transpose(lax.dynamic_slice_in_dim(dmod_all, me * ADA_SHARD, ADA_SHARD, axis=2), (1, 0, 2))
    res = _wada_adamw(jnp.transpose(c_all), dmod_mine, w_ada, m_w_ada, v_w_ada, "adamw_w_ada")
    for dst, arr in zip((out_g, out_d, out_m, out_v), res):
        dst["w_ada"] = arr

    for name, tr in (("w_in", 256), ("w_out", 128), ("w_ff_in", 256), ("w_ff_out", 256)):
        sh = W[name].shape
        rows = int(np.prod(sh[:-1]))
        flat = lambda a: a.reshape(rows, sh[-1])
        parts = jnp.stack([recv[l][name] for l in range(L)], axis=1).reshape(NDEV, rows, sh[-1])
        res = _adamw_reduce(parts, flat(W[name]), flat(Mo[name]), flat(Vo[name]), "adamw_" + name, tr=tr)
        for dst, arr in zip((out_g, out_d, out_m, out_v), res):
            dst[name] = arr.reshape(sh)

    order = ("w_ada", "b_ada", "norm_mix", "norm_mlp", "w_in", "conv_w", "a_log", "dt_bias", "gdn_norm",
             "rel_bias", "w_out", "w_ff_in", "w_ff_out", "final_norm")
    return (tot["loss"].reshape(()), dx[None], *[out_g[n] for n in order], *[out_d[n] for n in order],
            *[out_m[n] for n in order], *[out_v[n] for n in order])
```

```python
import functools
import math

import numpy as np
import jax
import jax.numpy as jnp
from jax import lax
from jax.experimental import pallas as pl
from jax.experimental.pallas import tpu as pltpu

F32 = jnp.float32
MXU = jnp.bfloat16
HI = lax.Precision.HIGHEST
MESH_ID = pl.DeviceIdType.MESH

D = 1024
NH = 8
HD = 128
CH = 64
PAST = 8
DFF = 4096
EPS = 1e-6
NDEV = 8
DEPTH = 4
IN_W = 9232
ZW = 9728
Z_GATE, Z_ATT, Z_BR, Z_AB = 3072, 4096, 7168, 9216
QB = 256
MASKED = -1e30
KWIN = 768
FW = 1024
ADAM_LR, ADAM_B1, ADAM_B2, ADAM_EPS, ADAM_WD, ADAM_STEP = 0.001, 0.9, 0.999, 1e-08, 0.01, 10


def _dot(a, b, prec=None):
    return jnp.dot(a, b, preferred_element_type=F32, precision=prec)


def _dot_nt(a, b, prec=None):
    return lax.dot_general(a, b, (((1,), (1,)), ((), ())), preferred_element_type=F32, precision=prec)


def _dot_tn(a, b, prec=None):
    return lax.dot_general(a, b, (((0,), (0,)), ((), ())), preferred_element_type=F32, precision=prec)


def _mx(a):
    return a.astype(MXU)


def _sigmoid(x):
    return 0.5 * jnp.tanh(0.5 * x) + 0.5


def _softplus(x):
    return jnp.maximum(x, 0.0) + jnp.log(1.0 + jnp.exp(-jnp.abs(x)))


def _rowsum(x):
    return jnp.sum(x, axis=1, keepdims=True)


def _colsum(x):
    return jnp.sum(x, axis=0, keepdims=True)


def _call(body, name, grid, in_specs, out_specs, out_shape, scratch=(), comm=None):
    if comm is None:
        return pl.pallas_call(body, name=name, grid=grid, in_specs=in_specs, out_specs=out_specs,
                              out_shape=out_shape, scratch_shapes=list(scratch))
    kind, x = comm
    single = not isinstance(out_specs, (list, tuple))
    o_specs = [out_specs] if single else list(out_specs)
    o_shape = [out_shape] if single else list(out_shape)
    n_in, n_out, n_scr = len(in_specs), len(o_specs), len(scratch)
    c_shape = (NDEV,) + x.shape if kind == "gather" else x.shape

    def wrapped(*refs):
        ins, x_ref = refs[:n_in], refs[n_in]
        outs, c_ref = refs[n_in + 1:n_in + 1 + n_out], refs[n_in + 1 + n_out]
        scr = refs[n_in + 2 + n_out:n_in + 2 + n_out + n_scr]
        sems = refs[n_in + 2 + n_out + n_scr:]
        first = functools.reduce(jnp.logical_and, [pl.program_id(a) == 0 for a in range(len(grid))])
        last = functools.reduce(jnp.logical_and, [pl.program_id(a) == grid[a] - 1 for a in range(len(grid))])

        @pl.when(first)
        def _():
            _comm_start(*_comm_copies(kind, x_ref, c_ref, *sems))

        body(*ins, *outs, *scr)

        @pl.when(last)
        def _():
            _comm_wait(*_comm_copies(kind, x_ref, c_ref, *sems))

    any_spec = pl.BlockSpec(memory_space=pl.ANY)
    call = pl.pallas_call(
        wrapped, name=name, grid=grid, in_specs=list(in_specs) + [any_spec], out_specs=o_specs + [any_spec],
        out_shape=o_shape + [jax.ShapeDtypeStruct(c_shape, x.dtype)],
        scratch_shapes=list(scratch) + _comm_sems())

    def run(*args):
        res = call(*args, x)
        return (res[0] if single else list(res[:-1])), res[-1]

    return run


def _hosted(fn, comm):
    return (fn(), None) if comm is None else fn(comm=comm)


def _full(shape):
    n = len(shape)
    return pl.BlockSpec(shape, lambda *_: (0,) * n)


def _mesh_pos():
    return lax.axis_index("x"), lax.axis_index("y"), lax.axis_index("c")


def _peer(pos, k):
    x, y, c = pos
    return (x ^ ((k >> 2) & 1), y ^ ((k >> 1) & 1), c ^ (k & 1))


def _flat(pos):
    return 4 * pos[0] + 2 * pos[1] + pos[2]


def _comm_sems():
    return [pltpu.SemaphoreType.DMA((NDEV - 1,)), pltpu.SemaphoreType.DMA((NDEV - 1,)), pltpu.SemaphoreType.DMA]


def _comm_copies(kind, x_ref, out_ref, send_sems, recv_sems, local_sem):
    pos = _mesh_pos()
    me = _flat(pos)
    src = (lambda d: x_ref) if kind == "gather" else (lambda d: x_ref.at[d])
    mine = pltpu.make_async_copy(src(me), out_ref.at[me], local_sem)
    sends, recvs = [], []
    for k in range(1, NDEV):
        peer = _peer(pos, k)
        pid = _flat(peer)
        sems = dict(send_sem=send_sems.at[k - 1], recv_sem=recv_sems.at[k - 1], device_id=peer,
                    device_id_type=MESH_ID)
        sends.append(pltpu.make_async_remote_copy(src_ref=src(pid), dst_ref=out_ref.at[me], **sems))
        recvs.append(pltpu.make_async_remote_copy(src_ref=src(pid), dst_ref=out_ref.at[pid], **sems))
    return mine, sends, recvs


def _comm_start(mine, sends, recvs):
    mine.start()
    for cp in sends:
        cp.start()


def _comm_wait(mine, sends, recvs):
    for cp in recvs:
        cp.wait_recv()
    for cp in sends:
        cp.wait_send()
    mine.wait()


def _collective(kind, x, name):
    def body(x_ref, out_ref, *sems):
        copies = _comm_copies(kind, x_ref, out_ref, *sems)
        _comm_start(*copies)
        _comm_wait(*copies)

    shape = (NDEV,) + x.shape if kind == "gather" else x.shape
    return pl.pallas_call(
        body, name=name, out_shape=jax.ShapeDtypeStruct(shape, x.dtype),
        in_specs=[pl.BlockSpec(memory_space=pl.ANY)], out_specs=pl.BlockSpec(memory_space=pl.ANY),
        scratch_shapes=_comm_sems())(x)


def _all_gather(x, name):
    return _collective("gather", x, name)


def _mm_nn(a, w, name, *, mode="plain", res=None, gate=None, norm=None, tm=1024, tn=1024, tk=1024, comm=None):
    M, K = a.shape
    N = w.shape[1]
    tm, tk, tn = min(tm, M), min(tk, K), min(tn, N)
    nk = K // tk

    def body(*refs):
        refs = list(refs)
        acc = refs.pop() if nk > 1 else None
        if mode == "resid":
            a_ref, w_ref, res_ref, gate_ref, o_ref = refs
        elif mode == "resid_norm":
            a_ref, w_ref, res_ref, gate_ref, g_ref, sc_ref, sh_ref, o_ref, r_ref = refs
        elif mode == "relu2":
            a_ref, w_ref, o_ref, r_ref = refs
        else:
            a_ref, w_ref, o_ref = refs
        k = pl.program_id(2)
        part = _dot(_mx(a_ref[...]), w_ref[...])

        def finish(r):
            if mode == "resid":
                o_ref[...] = res_ref[...] + gate_ref[...] * r
            elif mode == "resid_norm":
                xv = res_ref[...] + gate_ref[...] * r
                o_ref[...] = xv
                rs = lax.rsqrt(jnp.mean(xv * xv, axis=1, keepdims=True) + EPS)
                r_ref[...] = ((xv * rs * g_ref[...]) * (1.0 + sc_ref[...]) + sh_ref[...]).astype(r_ref.dtype)
            elif mode == "relu2":
                o_ref[...] = r
                r_ref[...] = jnp.square(jnp.maximum(r, 0.0)).astype(r_ref.dtype)
            else:
                o_ref[...] = r

        if nk == 1:
            finish(part)
        else:
            @pl.when(k == 0)
            def _():
                acc[...] = part

            @pl.when((k > 0) & (k < nk - 1))
            def _():
                acc[...] += part

            @pl.when(k == nk - 1)
            def _():
                finish(acc[...] + part)

    in_specs = [pl.BlockSpec((tm, tk), lambda i, j, k: (i, k)),
                pl.BlockSpec((tk, tn), lambda i, j, k: (k, j))]
    args = [a, w]
    o_spec = pl.BlockSpec((tm, tn), lambda i, j, k: (i, j))
    out_specs, out_shape = o_spec, jax.ShapeDtypeStruct((M, N), F32)
    vec = pl.BlockSpec((1, tn), lambda i, j, k: (0, j))
    if mode == "resid":
        in_specs += [o_spec, vec]
        args += [res, gate]
    elif mode == "resid_norm":
        assert tn == N
        in_specs += [o_spec, vec, vec, vec, vec]
        args += [res, gate, *norm]
        out_specs = [o_spec, o_spec]
        out_shape = [jax.ShapeDtypeStruct((M, N), F32), jax.ShapeDtypeStruct((M, N), MXU)]
    elif mode == "relu2":
        out_specs = [o_spec, o_spec]
        out_shape = [jax.ShapeDtypeStruct((M, N), F32), jax.ShapeDtypeStruct((M, N), MXU)]
    return _call(body, name, (M // tm, N // tn, nk), in_specs, out_specs, out_shape,
                 [pltpu.VMEM((tm, tn), F32)] if nk > 1 else [], comm=comm)(*args)


def _mm_nt(a, w, name, *, gate=None, drelu=None, tm=1024, tko=1024, tn=1024, comm=None):
    M, N = a.shape
    K = w.shape[0]
    tm, tn, tko = min(tm, M), min(tn, N), min(tko, K)
    nn = N // tn

    def body(*refs):
        refs = list(refs)
        acc = refs.pop() if nn > 1 else None
        a_ref, w_ref = refs[:2]
        rest = refs[2:]
        gate_ref = rest.pop(0) if gate is not None else None
        pre_ref = rest.pop(0) if drelu is not None else None
        (o_ref,) = rest
        n = pl.program_id(2)
        av = a_ref[...]
        if gate_ref is not None:
            av = av * gate_ref[...]
        part = _dot_nt(_mx(av), w_ref[...])

        def finish(r):
            if pre_ref is not None:
                r = r * (2.0 * jnp.maximum(pre_ref[...], 0.0))
            o_ref[...] = r.astype(o_ref.dtype)

        if nn == 1:
            finish(part)
        else:
            @pl.when(n == 0)
            def _():
                acc[...] = part

            @pl.when((n > 0) & (n < nn - 1))
            def _():
                acc[...] += part

            @pl.when(n == nn - 1)
            def _():
                finish(acc[...] + part)

    in_specs = [pl.BlockSpec((tm, tn), lambda i, j, n: (i, n)),
                pl.BlockSpec((tko, tn), lambda i, j, n: (j, n))]
    args = [a, w]
    if gate is not None:
        in_specs.append(pl.BlockSpec((1, tn), lambda i, j, n: (0, n)))
        args.append(gate)
    o_spec = pl.BlockSpec((tm, tko), lambda i, j, n: (i, j))
    if drelu is not None:
        in_specs.append(o_spec)
        args.append(drelu)
    out_dtype = MXU if drelu is not None else F32
    return _call(body, name, (M // tm, K // tko, nn), in_specs, o_spec,
                 jax.ShapeDtypeStruct((M, K), out_dtype), [pltpu.VMEM((tm, tko), F32)] if nn > 1 else [],
                 comm=comm)(*args)


def _mm_tn(a, b, name, *, gate=None, w=None, tk=1024, tn=1024, tm=1024, out_dtype=F32, comm=None):
    M, K = a.shape
    N = b.shape[1]
    tm, tk, tn = min(tm, M), min(tk, K), min(tn, N)
    nm = M // tm
    gated = gate is not None

    def body(*refs):
        if gated:
            a_ref, b_ref, gate_ref, w_ref, o_ref, dg_ref, acc = refs
        else:
            a_ref, b_ref, o_ref, acc = refs
        kk = pl.program_id(1)
        m = pl.program_id(2)
        part = _dot_tn(_mx(a_ref[...]), _mx(b_ref[...]))

        @pl.when((m == 0) & (nm > 1))
        def _():
            acc[...] = part

        @pl.when((m > 0) & (m < nm - 1))
        def _():
            acc[...] += part

        if gated:
            @pl.when((m == 0) & (kk == 0))
            def _():
                dg_ref[...] = jnp.zeros_like(dg_ref)

        @pl.when(m == nm - 1)
        def _():
            r = acc[...] + part if nm > 1 else part
            if gated:
                o_ref[...] = (r * gate_ref[...]).astype(o_ref.dtype)
                dg_ref[...] += _colsum(r * w_ref[...].astype(F32))
            else:
                o_ref[...] = r.astype(o_ref.dtype)

    in_specs = [pl.BlockSpec((tm, tk), lambda j, k, m: (m, k)),
                pl.BlockSpec((tm, tn), lambda j, k, m: (m, j))]
    args = [a, b]
    o_spec = pl.BlockSpec((tk, tn), lambda j, k, m: (k, j))
    out_specs, out_shape = o_spec, jax.ShapeDtypeStruct((K, N), out_dtype)
    if gated:
        in_specs += [pl.BlockSpec((1, tn), lambda j, k, m: (0, j)), o_spec]
        args += [gate, w]
        out_specs = [o_spec, pl.BlockSpec((1, tn), lambda j, k, m: (0, j))]
        out_shape = [out_shape, jax.ShapeDtypeStruct((1, N), F32)]
    return _call(body, name, (N // tn, K // tk, nm), in_specs, out_specs, out_shape,
                 [pltpu.VMEM((tk, tn), F32)], comm=comm)(*args)


SEG_T = 512


def _seg_layout(segs):
    starts, t = [], 0
    for a in segs:
        starts.append(t)
        t += a.shape[1] // SEG_T
    return starts, t


def _seg_spec(tm, lo, hi, row_axis, col_axis):
    def index(*ids):
        col = ids[col_axis]
        act = (col >= lo) & (col < hi)
        return jnp.where(act, ids[row_axis], 0), jnp.where(act, col - lo, 0)

    return pl.BlockSpec((tm, SEG_T), index)


def _in_proj_dw(h, segs, name, tm=1024):
    S = h.shape[0]
    tm = min(tm, S)
    nm = S // tm
    starts, ntile = _seg_layout(segs)
    bounds = [(lo, lo + a.shape[1] // SEG_T) for lo, a in zip(starts, segs)]

    def body(*refs):
        h_ref, seg_refs, o_ref, acc = refs[0], refs[1:1 + len(segs)], refs[-2], refs[-1]
        j = pl.program_id(0)
        m = pl.program_id(1)
        for (lo, hi), b_ref in zip(bounds, seg_refs):
            @pl.when((j >= lo) & (j < hi))
            def _():
                part = _dot_tn(h_ref[...], _mx(b_ref[...]))
                if nm == 1:
                    o_ref[...] = part.astype(o_ref.dtype)
                else:
                    @pl.when(m == 0)
                    def _():
                        acc[...] = part

                    @pl.when((m > 0) & (m < nm - 1))
                    def _():
                        acc[...] += part

                    @pl.when(m == nm - 1)
                    def _():
                        o_ref[...] = (acc[...] + part).astype(o_ref.dtype)

    return _call(
        body, name, (ntile, nm),
        [pl.BlockSpec((tm, D), lambda j, m: (m, 0))] + [_seg_spec(tm, lo, hi, 1, 0) for lo, hi in bounds],
        pl.BlockSpec((D, SEG_T), lambda j, m: (0, j)), jax.ShapeDtypeStruct((D, ntile * SEG_T), MXU),
        [pltpu.VMEM((D, SEG_T), F32)])(h, *segs)


def _in_proj_dx(segs, w, name, tm=1024, comm=None):
    S = segs[0].shape[0]
    tm = min(tm, S)
    starts, ntile = _seg_layout(segs)
    bounds = [(lo, lo + a.shape[1] // SEG_T) for lo, a in zip(starts, segs)]

    def body(*refs):
        seg_refs, w_ref, o_ref, acc = refs[:len(segs)], refs[-3], refs[-2], refs[-1]
        n = pl.program_id(1)
        for (lo, hi), a_ref in zip(bounds, seg_refs):
            @pl.when((n >= lo) & (n < hi))
            def _():
                part = _dot_nt(_mx(a_ref[...]), w_ref[...])

                @pl.when(n == 0)
                def _():
                    acc[...] = part

                @pl.when((n > 0) & (n < ntile - 1))
                def _():
                    acc[...] += part

                @pl.when(n == ntile - 1)
                def _():
                    o_ref[...] = acc[...] + part

    return _call(
        body, name, (S // tm, ntile),
        [_seg_spec(tm, lo, hi, 0, 1) for lo, hi in bounds] + [pl.BlockSpec((D, SEG_T), lambda i, n: (0, n))],
        pl.BlockSpec((tm, D), lambda i, n: (i, 0)), jax.ShapeDtypeStruct((S, D), F32),
        [pltpu.VMEM((tm, D), F32)], comm=comm)(*segs, w)


def _modnorm_fwd(x, gain, sc, sh, name, ts=512):
    S = x.shape[0]

    def body(x_ref, g_ref, sc_ref, sh_ref, h_ref):
        xv = x_ref[...]
        r = lax.rsqrt(jnp.mean(xv * xv, axis=1, keepdims=True) + EPS)
        h_ref[...] = ((xv * r * g_ref[...]) * (1.0 + sc_ref[...]) + sh_ref[...]).astype(h_ref.dtype)

    row = pl.BlockSpec((ts, D), lambda i: (i, 0))
    vec = pl.BlockSpec((1, D), lambda i: (0, 0))
    return _call(body, name, (S // ts,), [row, vec, vec, vec], row,
                 jax.ShapeDtypeStruct((S, D), MXU))(x, gain, sc, sh)


def _modnorm_bwd(dh, x, gain, sc, sh, dx_in, name, ts=512):
    S = x.shape[0]

    def body(dh_ref, x_ref, g_ref, sc_ref, sh_ref, dxin_ref, dx_ref, dsc_ref, dsh_ref, dg_ref):
        i = pl.program_id(0)
        xv = x_ref[...]
        dhv = dh_ref[...]
        g = g_ref[...]
        r = lax.rsqrt(jnp.mean(xv * xv, axis=1, keepdims=True) + EPS)
        xr = xv * r
        dn = dhv * (1.0 + sc_ref[...])
        u = dn * g
        dx_ref[...] = dxin_ref[...] + r * (u - xr * jnp.mean(xr * u, axis=1, keepdims=True))

        @pl.when(i == 0)
        def _():
            dsc_ref[...] = jnp.zeros_like(dsc_ref)
            dsh_ref[...] = jnp.zeros_like(dsh_ref)
            dg_ref[...] = jnp.zeros_like(dg_ref)

        dsc_ref[...] += _colsum(dhv * (xr * g))
        dsh_ref[...] += _colsum(dhv)
        dg_ref[...] += _colsum(dn * xr)

    row = pl.BlockSpec((ts, D), lambda i: (i, 0))
    vec = pl.BlockSpec((1, D), lambda i: (0, 0))
    vshape = jax.ShapeDtypeStruct((1, D), F32)
    return _call(body, name, (S // ts,), [row, row, vec, vec, vec, row], [row, vec, vec, vec],
                 [jax.ShapeDtypeStruct((S, D), F32), vshape, vshape, vshape])(dh, x, gain, sc, sh, dx_in)


def _loss_head(x, target, gain, name, ts=512):
    S = x.shape[0]

    def body(x_ref, t_ref, g_ref, loss_ref, dx_ref, dg_ref):
        i = pl.program_id(0)
        xv = x_ref[...]
        g = g_ref[...]
        r = lax.rsqrt(jnp.mean(xv * xv, axis=1, keepdims=True) + EPS)
        xr = xv * r
        e = xr * g - t_ref[...]
        dy = e * (1.0 / D)
        u = dy * g
        dx_ref[...] = r * (u - xr * jnp.mean(xr * u, axis=1, keepdims=True))

        @pl.when(i == 0)
        def _():
            loss_ref[...] = jnp.zeros_like(loss_ref)
            dg_ref[...] = jnp.zeros_like(dg_ref)

        part = 0.5 * jnp.sum(jnp.mean(e * e, axis=1, keepdims=True), axis=0, keepdims=True)
        loss_ref[...] += jnp.broadcast_to(part, loss_ref.shape)
        dg_ref[...] += _colsum(dy * xr)

    row = pl.BlockSpec((ts, D), lambda i: (i, 0))
    vec = pl.BlockSpec((1, D), lambda i: (0, 0))
    return _call(body, name, (S // ts,), [row, row, vec],
                 [pl.BlockSpec((1, 128), lambda i: (0, 0)), row, vec],
                 [jax.ShapeDtypeStruct((1, 128), F32), jax.ShapeDtypeStruct((S, D), F32),
                  jax.ShapeDtypeStruct((1, D), F32)])(x, target, gain)


def _merge_specs(ts):
    o_spec = pl.BlockSpec((NH, ts, HD), lambda i: (0, i, 0))
    zg = pl.BlockSpec((ts, D), lambda i: (i, Z_GATE // D))
    za = pl.BlockSpec((ts, D), lambda i: (i, Z_BR // D))
    zb = pl.BlockSpec((ts, D), lambda i: (i, Z_BR // D + 1))
    row = pl.BlockSpec((ts, D), lambda i: (i, 0))
    gn = pl.BlockSpec((1, HD), lambda i: (0, 0))
    return o_spec, zg, za, zb, row, gn


def _merge_fwd(o, z, ob, gn, name, ts=256):
    S = ob.shape[0]

    def body(o_ref, zg_ref, za_ref, zb_ref, ob_ref, gn_ref, m_ref):
        for h in range(NH):
            sl = slice(h * HD, (h + 1) * HD)
            oh = o_ref[h]
            r = lax.rsqrt(jnp.mean(oh * oh, axis=1, keepdims=True) + EPS)
            gate = zg_ref[:, sl]
            oa = (oh * r * gn_ref[...]) * (gate * _sigmoid(gate))
            m = _sigmoid(za_ref[:, sl]) * oa + _sigmoid(zb_ref[:, sl]) * ob_ref[:, sl]
            m_ref[:, sl] = m.astype(m_ref.dtype)

    o_spec, zg, za, zb, row, gns = _merge_specs(ts)
    return _call(body, name, (S // ts,), [o_spec, zg, za, zb, row, gns], row,
                 jax.ShapeDtypeStruct((S, D), MXU))(o, z, z, z, ob, gn)


def _merge_bwd(dm, o, z, ob, gn, name, ts=256):
    S = ob.shape[0]

    def body(dm_ref, o_ref, zg_ref, za_ref, zb_ref, ob_ref, gn_ref,
             do_ref, dzg_ref, dob_ref, dza_ref, dzb_ref, dgn_ref):
        i = pl.program_id(0)
        gn_v = gn_ref[...]
        dgn = jnp.zeros((1, HD), F32)
        for h in range(NH):
            sl = slice(h * HD, (h + 1) * HD)
            dmh = dm_ref[:, sl]
            oh = o_ref[h]
            r = lax.rsqrt(jnp.mean(oh * oh, axis=1, keepdims=True) + EPS)
            ohr = oh * r
            on = ohr * gn_v
            gate = zg_ref[:, sl]
            sg = _sigmoid(gate)
            silu = gate * sg
            oa = on * silu
            ga = _sigmoid(za_ref[:, sl])
            gb = _sigmoid(zb_ref[:, sl])
            obh = ob_ref[:, sl]
            doa = dmh * ga
            dob_ref[:, sl] = dmh * gb
            dza_ref[:, sl] = (dmh * oa * ga * (1.0 - ga)).astype(dza_ref.dtype)
            dzb_ref[:, sl] = (dmh * obh * gb * (1.0 - gb)).astype(dzb_ref.dtype)
            don = doa * silu
            dzg_ref[:, sl] = (doa * on * (sg * (1.0 + gate * (1.0 - sg)))).astype(dzg_ref.dtype)
            dgn = dgn + _colsum(don * ohr)
            u = don * gn_v
            do_ref[h] = r * (u - ohr * jnp.mean(ohr * u, axis=1, keepdims=True))

        @pl.when(i == 0)
        def _():
            dgn_ref[...] = jnp.zeros_like(dgn_ref)

        dgn_ref[...] += dgn

    o_spec, zg, za, zb, row, gns = _merge_specs(ts)
    return _call(
        body, name, (S // ts,), [row, o_spec, zg, za, zb, row, gns],
        [o_spec, row, row, row, row, gns],
        [jax.ShapeDtypeStruct((NH, S, HD), F32), jax.ShapeDtypeStruct((S, D), MXU),
         jax.ShapeDtypeStruct((S, D), F32), jax.ShapeDtypeStruct((S, D), MXU),
         jax.ShapeDtypeStruct((S, D), MXU), jax.ShapeDtypeStruct((1, HD), F32)],
    )(dm, o, z, z, z, ob, gn)


GROWS = 256
GCH = GROWS // CH


def _gdn_prep_fwd(z, conv_w, alog_row, dtb_row, name, comm=None):
    S = z.shape[0]
    ts = GROWS
    scale = HD ** -0.5

    def body(z_ref, halo_ref, zab_ref, w_ref, al_ref, dt_ref, q_ref, k_ref, v_ref, gb_ref, buf):
        i = pl.program_id(0)
        buf[0:8, :] = jnp.where(i == 0, 0.0, halo_ref[...])
        buf[8:8 + ts, :] = z_ref[...]
        outs = (q_ref, k_ref, v_ref)
        for seg in range(3):
            cs = slice(seg * D, (seg + 1) * D)
            c = jnp.zeros((ts, D), F32)
            for j in range(4):
                c = c + w_ref[j:j + 1, cs] * buf[pl.ds(5 + j, ts), cs]
            s = c * _sigmoid(c)
            if seg == 2:
                outs[seg][...] = s
            else:
                mul = scale if seg == 0 else 1.0
                for h in range(NH):
                    sl = slice(h * HD, (h + 1) * HD)
                    sh = s[:, sl]
                    r = lax.rsqrt(_rowsum(sh * sh) + EPS)
                    outs[seg][:, sl] = sh * (r * mul)
        zab = zab_ref[...]
        lane = lax.broadcasted_iota(jnp.int32, zab.shape, 1)
        g = -jnp.exp(al_ref[...]) * _softplus(zab + dt_ref[...])
        ri = lax.broadcasted_iota(jnp.int32, (CH, CH), 0)
        ci = lax.broadcasted_iota(jnp.int32, (CH, CH), 1)
        incl = (ri >= ci).astype(F32)
        gcum = jnp.concatenate([_dot(incl, g[c * CH:(c + 1) * CH], HI) for c in range(ts // CH)], axis=0)
        gb_ref[...] = jnp.where(lane < NH, gcum, jnp.where(lane < 2 * NH, _sigmoid(zab), 0.0))

    row = pl.BlockSpec((ts, D), lambda i: (i, 0))
    vec = pl.BlockSpec((1, 128), lambda i: (0, 0))
    return _call(
        body, name, (S // ts,),
        [pl.BlockSpec((ts, 3 * D), lambda i: (i, 0)),
         pl.BlockSpec((8, 3 * D), lambda i: (jnp.maximum(i * (ts // 8) - 1, 0), 0)),
         pl.BlockSpec((ts, 128), lambda i: (i, Z_AB // 128)),
         _full((4, 3 * D)), vec, vec],
        [row, row, row, pl.BlockSpec((ts, 128), lambda i: (i, 0))],
        [jax.ShapeDtypeStruct((S, D), F32)] * 3 + [jax.ShapeDtypeStruct((S, 128), F32)],
        [pltpu.VMEM((ts + 8, 3 * D), F32)], comm=comm,
    )(z, z, z, conv_w, alog_row, dtb_row)


def _split(a):
    hi = a.astype(MXU)
    return hi, (a - hi.astype(F32)).astype(MXU)


def _dot3(a, b, dot=_dot):
    ah, al = _split(a)
    bh, bl = _split(b)
    return dot(ah, bh) + (dot(ah, bl) + dot(al, bh))


IROWS = 512
ICH = IROWS // CH


def _chunk_common(gbk, h, k):
    lane = lax.broadcasted_iota(jnp.int32, gbk.shape, 1)
    G = _rowsum(jnp.where(lane == h, gbk, 0.0))
    b_col = _rowsum(jnp.where(lane == h + NH, gbk, 0.0))
    ri = lax.broadcasted_iota(jnp.int32, (CH, CH), 0)
    ci = lax.broadcasted_iota(jnp.int32, (CH, CH), 1)
    incl = ri >= ci
    gc = jnp.broadcast_to(G, (CH, CH))
    decay = jnp.where(incl, jnp.exp(jnp.where(incl, gc - gc.T, 0.0)), 0.0)
    Gl = G[CH - 1:CH, :]
    kb = k * b_col
    return dict(b=b_col, ri=ri, ci=ci, incl=incl, strict=ri > ci, decay=decay, eG=jnp.exp(G),
                e2=jnp.exp(Gl - G), cd=jnp.exp(Gl), kb=kb, kk=_dot_nt(_mx(kb), _mx(k)))


def _gdn_intra_fwd(qn, kn, v, gb, name, comm=None):
    S = qn.shape[0]

    def body(q_ref, k_ref, v_ref, gb_ref, u_ref, w_ref, qd_ref, kd_ref, at_ref, t_ref, cd_ref):
        h = pl.program_id(1)
        rows = [slice(c * CH, (c + 1) * CH) for c in range(ICH)]
        ks = [k_ref[r, :] for r in rows]
        cms = [_chunk_common(gb_ref[r, :], h, k) for r, k in zip(rows, ks)]
        ps = [jnp.where(cm["strict"], cm["kk"] * cm["decay"], 0.0) for cm in cms]
        ts = [(cm["ri"] == cm["ci"]).astype(F32) - p for cm, p in zip(cms, ps)]
        for _ in range(5):
            ps = [_dot3(p, p) for p in ps]
            ts = [t + _dot3(t, p) for t, p in zip(ts, ps)]
        for c, (r, k, cm, t) in enumerate(zip(rows, ks, cms, ts)):
            rhs = jnp.concatenate([v_ref[r, :] * cm["b"], k * (cm["b"] * cm["eG"])], axis=1)
            sol = _dot3(t, rhs)
            u_ref[0, r, :] = sol[:, :HD]
            w_ref[0, r, :] = sol[:, HD:]
            t_ref[0, r, :] = t
        for c, (r, k, cm) in enumerate(zip(rows, ks, cms)):
            q = q_ref[r, :]
            qk = _dot_nt(_mx(q), _mx(k))
            at_ref[0, r, :] = jnp.where(cm["incl"], qk * cm["decay"], 0.0)
            qd_ref[0, r, :] = q * cm["eG"]
            kd_ref[0, r, :] = k * cm["e2"]
            cd_ref[0, c] = jnp.broadcast_to(cm["cd"], (8, 128))

    tok = pl.BlockSpec((IROWS, HD), lambda i, h: (i, h))
    hm = pl.BlockSpec((1, IROWS, HD), lambda i, h: (h, i, 0))
    hm64 = pl.BlockSpec((1, IROWS, CH), lambda i, h: (h, i, 0))
    big = jax.ShapeDtypeStruct((NH, S, HD), F32)
    sm = jax.ShapeDtypeStruct((NH, S, CH), F32)
    return _call(
        body, name, (S // IROWS, NH),
        [tok, tok, tok, pl.BlockSpec((IROWS, 128), lambda i, h: (i, 0))],
        [hm, hm, hm, hm, hm64, hm64, pl.BlockSpec((1, ICH, 8, 128), lambda i, h: (h, i, 0, 0))],
        [big, big, big, big, sm, sm, jax.ShapeDtypeStruct((NH, S // CH, 8, 128), F32)], comm=comm,
    )(qn, kn, v, gb)


def _scale_state(s, cd_tile):
    return (s.reshape(HD // 8, 8, HD) * cd_tile[None]).reshape(HD, HD)


def _gdn_scan_fwd(u, w, qd, kd, attn, cdt, name):
    S = u.shape[1]
    nblk = S // GROWS

    def body(u_ref, w_ref, qd_ref, kd_ref, at_ref, cd_ref, o_ref, vn_ref, st_ref, s_ref):
        i = pl.program_id(0)

        @pl.when(i == 0)
        def _():
            s_ref[...] = jnp.zeros_like(s_ref)

        def chunk(c, carry):
            r0 = pl.multiple_of(c * CH, CH)
            rows = pl.ds(r0, CH)
            H = range(NH)
            shs = [s_ref[h] for h in H]
            sbs = [_mx(sh) for sh in shs]
            ws = [_dot(_mx(w_ref[h, rows, :]), sbs[h]) for h in H]
            qs = [_dot(_mx(qd_ref[h, rows, :]), sbs[h]) for h in H]
            vns = [u_ref[h, rows, :] - ws[h] for h in H]
            vbs = [_mx(vn) for vn in vns]
            avs = [_dot(_mx(at_ref[h, rows, :]), vbs[h]) for h in H]
            kvs = [_dot_tn(_mx(kd_ref[h, rows, :]), vbs[h]) for h in H]
            for h in H:
                st_ref[h, c] = shs[h]
                vn_ref[h, rows, :] = vns[h]
                o_ref[h, rows, :] = qs[h] + avs[h]
                s_ref[h] = _scale_state(shs[h], cd_ref[h, c]) + kvs[h]
            return carry

        lax.fori_loop(0, GCH, chunk, 0)

    hm = pl.BlockSpec((NH, GROWS, HD), lambda i: (0, i, 0))
    hm64 = pl.BlockSpec((NH, GROWS, CH), lambda i: (0, i, 0))
    big = jax.ShapeDtypeStruct((NH, S, HD), F32)
    return _call(
        body, name, (nblk,),
        [hm, hm, hm, hm, hm64, pl.BlockSpec((NH, GCH, 8, 128), lambda i: (0, i, 0, 0))],
        [hm, hm, pl.BlockSpec((NH, GCH, HD, HD), lambda i: (0, i, 0, 0))],
        [big, big, jax.ShapeDtypeStruct((NH, S // CH, HD, HD), F32)],
        [pltpu.VMEM((NH, HD, HD), F32)],
    )(u, w, qd, kd, attn, cdt)


def _gdn_scan_bwd(do, w, qd, kd, attn, cdt, vn, st, name):
    S = do.shape[1]
    nblk = S // GROWS

    def body(do_ref, w_ref, qd_ref, kd_ref, at_ref, cd_ref, vn_ref, st_ref,
             dqd_ref, dkd_ref, dvn_ref, dw_ref, dat_ref, dcd_ref, ds_ref):
        i = pl.program_id(0)

        @pl.when(i == 0)
        def _():
            ds_ref[...] = jnp.zeros_like(ds_ref)

        def chunk(cc, carry):
            c = GCH - 1 - cc
            r0 = pl.multiple_of(c * CH, CH)
            rows = pl.ds(r0, CH)
            H = range(NH)
            dsps = [ds_ref[h] for h in H]
            shs = [st_ref[h, c] for h in H]
            dsbs = [_mx(a) for a in dsps]
            sbs = [_mx(a) for a in shs]
            dobs = [_mx(do_ref[h, rows, :]) for h in H]
            vbs = [_mx(vn_ref[h, rows, :]) for h in H]
            dvns = [_dot(_mx(kd_ref[h, rows, :]), dsbs[h]) + _dot_tn(_mx(at_ref[h, rows, :]), dobs[h]) for h in H]
            dvbs = [_mx(a) for a in dvns]
            dqds = [_dot_nt(dobs[h], sbs[h]) for h in H]
            dats = [_dot_nt(dobs[h], vbs[h]) for h in H]
            dkds = [_dot_nt(vbs[h], dsbs[h]) for h in H]
            dws = [_dot_nt(dvbs[h], sbs[h]) for h in H]
            qdos = [_dot_tn(_mx(qd_ref[h, rows, :]), dobs[h]) for h in H]
            wdvs = [_dot_tn(_mx(w_ref[h, rows, :]), dvbs[h]) for h in H]
            for h in H:
                dvn_ref[h, rows, :] = dvns[h]
                dqd_ref[h, rows, :] = dqds[h]
                dat_ref[h, rows, :] = dats[h]
                dkd_ref[h, rows, :] = dkds[h]
                dw_ref[h, rows, :] = -dws[h]
                dcd = jnp.sum(_rowsum(dsps[h] * shs[h]), axis=0, keepdims=True)
                dcd_ref[h, c] = jnp.broadcast_to(dcd, (8, 128))
                ds_ref[h] = _scale_state(dsps[h], cd_ref[h, c]) + qdos[h] - wdvs[h]
            return carry

        lax.fori_loop(0, GCH, chunk, 0)

    hm = pl.BlockSpec((NH, GROWS, HD), lambda i: (0, nblk - 1 - i, 0))
    hm64 = pl.BlockSpec((NH, GROWS, CH), lambda i: (0, nblk - 1 - i, 0))
    tile = pl.BlockSpec((NH, GCH, 8, 128), lambda i: (0, nblk - 1 - i, 0, 0))
    big = jax.ShapeDtypeStruct((NH, S, HD), F32)
    return _call(
        body, name, (nblk,),
        [hm, hm, hm, hm, hm64, tile, hm, pl.BlockSpec((NH, GCH, HD, HD), lambda i: (0, nblk - 1 - i, 0, 0))],
        [hm, hm, hm, hm, hm64, tile],
        [big, big, big, big, jax.ShapeDtypeStruct((NH, S, CH), F32),
         jax.ShapeDtypeStruct((NH, S // CH, 8, 128), F32)],
        [pltpu.VMEM((NH, HD, HD), F32)],
    )(do, w, qd, kd, attn, cdt, vn, st)


def _gdn_intra_bwd(qn, kn, v, gb, u, w, tmat, dqd, dkd, du, dw, dattn, dcdt, name):
    S = qn.shape[0]

    def body(q_ref, k_ref, v_ref, gb_ref, u_ref, w_ref, t_ref, dqd_ref, dkd_ref, du_ref, dw_ref,
             dat_ref, dcd_ref, dq_ref, dk_ref, dv_ref, dgb_ref):
        h = pl.program_id(1)
        rows = [slice(c * CH, (c + 1) * CH) for c in range(ICH)]
        ks = [k_ref[r, :] for r in rows]
        cms = [_chunk_common(gb_ref[r, :], h, k) for r, k in zip(rows, ks)]
        sols = [jnp.concatenate([u_ref[0, r, :], w_ref[0, r, :]], axis=1) for r in rows]
        drhss = [_dot3(t_ref[0, r, :], jnp.concatenate([du_ref[0, r, :], dw_ref[0, r, :]], axis=1), _dot_tn)
                 for r in rows]
        das = [-_dot3(drhs, sol, _dot_nt) for drhs, sol in zip(drhss, sols)]
        for c, (r, k, cm, drhs, da) in enumerate(zip(rows, ks, cms, drhss, das)):
            q, vv = q_ref[r, :], v_ref[r, :]
            decay, eG, e2, b = cm["decay"], cm["eG"], cm["e2"], cm["b"]
            dru, drw = drhs[:, :HD], drhs[:, HD:]
            dv_ref[r, :] = dru * b
            s_w = _rowsum(drw * k)
            dbeta = _rowsum(dru * vv) + s_w * eG
            deg = s_w * b
            dk = drw * (b * eG)
            dkk = jnp.where(cm["strict"], da * decay, 0.0)
            ddec = jnp.where(cm["strict"], da * cm["kk"], 0.0)
            dkkb = _mx(dkk)
            dkb = _dot(dkkb, _mx(k))
            dk = dk + _dot_tn(dkkb, _mx(cm["kb"])) + dkb * b
            dbeta = dbeta + _rowsum(dkb * k)
            dat = jnp.where(cm["incl"], dat_ref[0, r, :], 0.0)
            qk = _dot_nt(_mx(q), _mx(k))
            dqk = _mx(dat * decay)
            ddec = ddec + dat * qk
            dqd = dqd_ref[0, r, :]
            dkd = dkd_ref[0, r, :]
            dq_ref[r, :] = _dot(dqk, _mx(k)) + dqd * eG
            dk_ref[r, :] = dk + _dot_tn(dqk, _mx(q)) + dkd * e2
            deg = deg + _rowsum(dqd * q)
            t2 = _rowsum(dkd * k) * e2
            dgl = jnp.sum(t2, axis=0, keepdims=True) + dcd_ref[0, c][0:1, 0:1] * cm["cd"]
            dd = ddec * decay
            dG = deg * eG - t2 + _rowsum(dd) - _rowsum(dd.T)
            rowi = lax.broadcasted_iota(jnp.int32, (CH, 1), 0)
            dG = dG + jnp.where(rowi == CH - 1, dgl, 0.0)
            lane = lax.broadcasted_iota(jnp.int32, (CH, 128), 1)
            dgb_ref[0, r, :] = jnp.where(lane == h, dG, 0.0) + jnp.where(lane == h + NH, dbeta, 0.0)

    tok = pl.BlockSpec((IROWS, HD), lambda i, h: (i, h))
    hm = pl.BlockSpec((1, IROWS, HD), lambda i, h: (h, i, 0))
    hm64 = pl.BlockSpec((1, IROWS, CH), lambda i, h: (h, i, 0))
    tile = pl.BlockSpec((1, ICH, 8, 128), lambda i, h: (h, i, 0, 0))
    tokout = jax.ShapeDtypeStruct((S, D), F32)
    return _call(
        body, name, (S // IROWS, NH),
        [tok, tok, tok, pl.BlockSpec((IROWS, 128), lambda i, h: (i, 0)), hm, hm, hm64,
         hm, hm, hm, hm, hm64, tile],
        [tok, tok, tok, pl.BlockSpec((1, IROWS, 128), lambda i, h: (h, i, 0))],
        [tokout, tokout, tokout, jax.ShapeDtypeStruct((NH, S, 128), F32)],
    )(qn, kn, v, gb, u, w, tmat, dqd, dkd, du, dw, dattn, dcdt)


def _gdn_prep_bwd(z, dqn, dkn, dv, dgb, conv_w, alog_row, dtb_row, name):
    S = z.shape[0]
    ts = GROWS
    nblk = S // ts
    scale = HD ** -0.5
    tb = ts // 8

    def body(z_ref, hp_ref, hn_ref, zab_ref, dq_ref, dqn_ref, dk_ref, dkn_ref, dv_ref, dvn_ref,
             dgb_ref, w_ref, al_ref, dt_ref, dz_ref, dzab_ref, dcw_ref, dvec_ref, buf, dybuf, dcbuf):
        i = pl.program_id(0)
        last = i == nblk - 1

        @pl.when(i == 0)
        def _():
            dcw_ref[...] = jnp.zeros_like(dcw_ref)
            dvec_ref[...] = jnp.zeros_like(dvec_ref)

        buf[0:8, :] = jnp.where(i == 0, 0.0, hp_ref[...])
        buf[8:8 + ts, :] = z_ref[...]
        buf[8 + ts:16 + ts, :] = hn_ref[...]
        rowi = lax.broadcasted_iota(jnp.int32, (ts + 8, 1), 0)
        live = jnp.logical_or(rowi < ts, jnp.logical_not(last))
        dys = ((dq_ref, dqn_ref), (dk_ref, dkn_ref), (dv_ref, dvn_ref))
        for seg in range(3):
            cs = slice(seg * D, (seg + 1) * D)
            dybuf[0:ts, :] = dys[seg][0][...]
            dybuf[ts:ts + 8, :] = dys[seg][1][...]
            taps = [buf[pl.ds(5 + j, ts + 8), cs] for j in range(4)]
            c = jnp.zeros((ts + 8, D), F32)
            for j in range(4):
                c = c + w_ref[j:j + 1, cs] * taps[j]
            sg = _sigmoid(c)
            s = c * sg
            dsilu = sg * (1.0 + c * (1.0 - sg))
            if seg == 2:
                dcbuf[...] = jnp.where(live, dybuf[...] * dsilu, 0.0)
            else:
                mul = scale if seg == 0 else 1.0
                for h in range(NH):
                    sl = slice(h * HD, (h + 1) * HD)
                    sh = s[:, sl]
                    dy = dybuf[:, sl]
                    r = lax.rsqrt(_rowsum(sh * sh) + EPS)
                    shr = sh * r
                    ds = (mul * r) * (dy - shr * _rowsum(shr * dy))
                    dcbuf[:, sl] = jnp.where(live, ds * dsilu[:, sl], 0.0)
            dx = jnp.zeros((ts, D), F32)
            for j in range(4):
                dcw_ref[j:j + 1, cs] += _colsum(dcbuf[0:ts, :] * taps[j][0:ts])
                dx = dx + w_ref[j:j + 1, cs] * dcbuf[pl.ds(3 - j, ts), :]
            dz_ref[:, cs] = dx.astype(dz_ref.dtype)
        dgbs = dgb_ref[0]
        for h in range(1, NH):
            dgbs = dgbs + dgb_ref[h]
        ri = lax.broadcasted_iota(jnp.int32, (CH, CH), 0)
        ci = lax.broadcasted_iota(jnp.int32, (CH, CH), 1)
        rev = (ci >= ri).astype(F32)
        dgrev = jnp.concatenate([_dot(rev, dgbs[c * CH:(c + 1) * CH], HI) for c in range(ts // CH)], axis=0)
        lane0 = lax.broadcasted_iota(jnp.int32, dgbs.shape, 1)
        dgbs = jnp.where(lane0 < NH, dgrev, dgbs)
        zab = zab_ref[...]
        lane = lax.broadcasted_iota(jnp.int32, zab.shape, 1)
        xx = zab + dt_ref[...]
        ea = jnp.exp(al_ref[...])
        g = -ea * _softplus(xx)
        da = dgbs * (-ea) * _sigmoid(xx)
        beta = _sigmoid(zab)
        db = dgbs * beta * (1.0 - beta)
        is_a = lane < NH
        dzab = jnp.where(is_a, da, jnp.where(lane < 2 * NH, db, 0.0))
        dzab_ref[:, 0:128] = dzab.astype(dzab_ref.dtype)
        dzab_ref[:, 128:512] = jnp.zeros((ts, 384), dzab_ref.dtype)
        dvec_ref[0:1, :] += _colsum(jnp.where(is_a, dgbs * g, 0.0))
        dvec_ref[1:2, :] += _colsum(jnp.where(is_a, da, 0.0))

    z3 = pl.BlockSpec((ts, 3 * D), lambda i: (i, 0))
    row = pl.BlockSpec((ts, D), lambda i: (i, 0))
    nxt = pl.BlockSpec((8, D), lambda i: (jnp.minimum((i + 1) * tb, S // 8 - 1), 0))
    vec = pl.BlockSpec((1, 128), lambda i: (0, 0))
    return _call(
        body, name, (nblk,),
        [z3,
         pl.BlockSpec((8, 3 * D), lambda i: (jnp.maximum(i * tb - 1, 0), 0)),
         pl.BlockSpec((8, 3 * D), lambda i: (jnp.minimum((i + 1) * tb, S // 8 - 1), 0)),
         pl.BlockSpec((ts, 128), lambda i: (i, Z_AB // 128)),
         row, nxt, row, nxt, row, nxt,
         pl.BlockSpec((NH, ts, 128), lambda i: (0, i, 0)),
         _full((4, 3 * D)), vec, vec],
        [z3, pl.BlockSpec((ts, 512), lambda i: (i, 0)), _full((8, 3 * D)), _full((8, 128))],
        [jax.ShapeDtypeStruct((S, 3 * D), MXU), jax.ShapeDtypeStruct((S, 512), MXU),
         jax.ShapeDtypeStruct((8, 3 * D), F32), jax.ShapeDtypeStruct((8, 128), F32)],
        [pltpu.VMEM((ts + 16, 3 * D), F32), pltpu.VMEM((ts + 8, D), F32), pltpu.VMEM((ts + 8, D), F32)],
    )(z, z, z, z, dqn, dqn, dkn, dkn, dv, dv, dgb, conv_w, alog_row, dtb_row)


def _bias_index():
    u = np.arange(FW)[None, :]
    s = np.arange(3)[:, None]
    return np.clip(KWIN - 1 - u - QB * s, -256, 256) + 256


def _bias_vec(rel_bias_pad, onehot, name):
    def body(rb_ref, e_ref, o_ref):
        o_ref[:, 0, :] = _dot_nt(rb_ref[...], e_ref[0], HI)

    return _call(body, name, (3,),
                 [_full((NH, 640)), pl.BlockSpec((1, FW, 640), lambda s: (s, 0, 0))],
                 pl.BlockSpec((NH, 1, FW), lambda s: (s, 0, 0)),
                 jax.ShapeDtypeStruct((3 * NH, 1, FW), F32))(rel_bias_pad, onehot)


def _att_window(i):
    return pl.multiple_of(jnp.maximum(i * QB - PAST * CH, 0), QB)


def _bias_mask(fvec, name):
    def body(f_ref, o_ref):
        i = 2 - pl.program_id(0) // NH
        ws = jnp.maximum(i * QB - PAST * CH, 0)
        fb = jnp.broadcast_to(f_ref[0], (QB, FW))
        bias = pltpu.roll(fb, FW - 255, 1, stride=1, stride_axis=0)[:, :KWIN]
        qc = (i * QB + lax.broadcasted_iota(jnp.int32, (QB, KWIN), 0)) // CH
        kc = (ws + lax.broadcasted_iota(jnp.int32, (QB, KWIN), 1)) // CH
        o_ref[0] = jnp.where((kc <= qc) & (kc >= qc - PAST), bias, MASKED)

    return _call(body, name, (3 * NH,), [pl.BlockSpec((1, 1, FW), lambda j: (j, 0, 0))],
                 pl.BlockSpec((1, QB, KWIN), lambda j: (j, 0, 0)),
                 jax.ShapeDtypeStruct((3 * NH, QB, KWIN), F32))(fvec)


def _att_scores(q_ref, k_ref, bm_ref, i):
    ws = _att_window(i)
    q = _mx(q_ref[...] * (HD ** -0.5))
    kw = _mx(k_ref[pl.ds(ws, KWIN), :])
    return q, kw, ws, _dot_nt(q, kw) + bm_ref[0]


def _att_specs(S):
    c0 = Z_ATT // HD
    q = pl.BlockSpec((QB, HD), lambda h, i: (i, c0 + h))
    k = pl.BlockSpec((S, HD), lambda h, i: (0, c0 + NH + h))
    v = pl.BlockSpec((S, HD), lambda h, i: (0, c0 + 2 * NH + h))
    bm = pl.BlockSpec((1, QB, KWIN), lambda h, i: (jnp.maximum(2 - i, 0) * NH + h, 0, 0))
    tok = pl.BlockSpec((QB, HD), lambda h, i: (i, h))
    return q, k, v, bm, tok


def _att_fwd(z, bmask, name, comm=None):
    S = z.shape[0]

    def body(q_ref, k_ref, v_ref, bm_ref, o_ref, lse_ref):
        _, _, ws, s = _att_scores(q_ref, k_ref, bm_ref, pl.program_id(1))
        m = jnp.max(s, axis=1, keepdims=True)
        p = jnp.exp(s - m)
        l = _rowsum(p)
        o_ref[...] = _dot(_mx(p), _mx(v_ref[pl.ds(ws, KWIN), :])) * (1.0 / l)
        lse_ref[...] = jnp.broadcast_to(m + jnp.log(l), (QB, HD))

    q, k, v, bm, tok = _att_specs(S)
    shp = jax.ShapeDtypeStruct((S, D), F32)
    return _call(body, name, (NH, S // QB), [q, k, v, bm], [tok, tok], [shp, shp], comm=comm)(z, z, z, bmask)


def _att_bwd(z, bmask, ob, lse, dob, name):
    S = z.shape[0]
    nq = S // QB

    def body(q_ref, k_ref, v_ref, bm_ref, o_ref, lse_ref, do_ref, dq_ref, dk_ref, dv_ref, db_ref, dk_acc, dv_acc):
        i = pl.program_id(1)
        q, kw, ws, s = _att_scores(q_ref, k_ref, bm_ref, i)
        p = jnp.exp(s - lse_ref[:, 0:1])
        do = do_ref[...]
        dob16 = _mx(do)
        dp = _dot_nt(dob16, _mx(v_ref[pl.ds(ws, KWIN), :]))
        ds = p * (dp - _rowsum(do * o_ref[...]))
        dsb = _mx(ds)
        dq_ref[...] = (_dot(dsb, kw) * (HD ** -0.5)).astype(dq_ref.dtype)

        @pl.when(i == 0)
        def _():
            dk_acc[...] = jnp.zeros_like(dk_acc)
            dv_acc[...] = jnp.zeros_like(dv_acc)

        dk_acc[pl.ds(ws, KWIN), :] += _dot_tn(dsb, q)
        dv_acc[pl.ds(ws, KWIN), :] += _dot_tn(_mx(p), dob16)

        @pl.when(i == nq - 1)
        def _():
            dk_ref[...] = dk_acc[...].astype(dk_ref.dtype)
            dv_ref[...] = dv_acc[...].astype(dv_ref.dtype)

        @pl.when(i <= 2)
        def _():
            db_ref[0] = ds

        @pl.when(i > 2)
        def _():
            db_ref[0] += ds

    q, k, v, bm, tok = _att_specs(S)
    acc = pl.BlockSpec((S, HD), lambda h, i: (0, h))
    half = jax.ShapeDtypeStruct((S, D), MXU)
    return _call(
        body, name, (NH, nq), [q, k, v, bm, tok, tok, tok], [tok, acc, acc, bm],
        [half, half, half, jax.ShapeDtypeStruct((3 * NH, QB, KWIN), F32)],
        [pltpu.VMEM((S, HD), F32), pltpu.VMEM((S, HD), F32)],
    )(z, z, z, bmask, ob, lse, dob)


def _bias_fold(dbias, onehot, name):
    def body(db_ref, e_ref, o_ref):
        j = pl.program_id(0)
        h = j % NH
        x = jnp.concatenate([db_ref[0], jnp.zeros((QB, FW - KWIN), F32)], axis=1)
        half = QB // 2
        while half >= 8:
            x = x[:half] + pltpu.roll(x[half:2 * half], FW - half, 1)
            half //= 2
        df = jnp.zeros((1, FW), F32)
        for r in range(8):
            df = df + pltpu.roll(x[r:r + 1], 255 - r, 1)
        contrib = _dot(df, e_ref[0], HI)
        rowh = lax.broadcasted_iota(jnp.int32, (NH, 640), 0)

        @pl.when(j == 0)
        def _():
            o_ref[...] = jnp.zeros_like(o_ref)

        o_ref[...] += jnp.where(rowh == h, contrib, 0.0)

    return _call(
        body, name, (3 * NH,),
        [pl.BlockSpec((1, QB, KWIN), lambda j: (j, 0, 0)),
         pl.BlockSpec((1, FW, 640), lambda j: (j // NH, 0, 0))],
        _full((NH, 640)), jax.ShapeDtypeStruct((NH, 640), F32),
    )(dbias, onehot)


ADA_SHARD = 6 * D // NDEV


def _ada_mod(c_all, w_ada, b_shard, name):
    def body(c_ref, w_ref, b_ref, o_ref):
        cv = c_ref[...]
        ca = cv * _sigmoid(cv)
        o_ref[0] = _dot(_mx(ca), _mx(w_ref[0])) + b_ref[0]

    return _call(
        body, name, (DEPTH,),
        [_full((NDEV, D)), pl.BlockSpec((1, D, ADA_SHARD), lambda l: (l, 0, 0)),
         pl.BlockSpec((1, 1, ADA_SHARD), lambda l: (l, 0, 0))],
        pl.BlockSpec((1, NDEV, ADA_SHARD), lambda l: (l, 0, 0)),
        jax.ShapeDtypeStruct((DEPTH, NDEV, ADA_SHARD), F32),
    )(c_all, w_ada, b_shard.reshape(DEPTH, 1, ADA_SHARD))


def _adam(g, w, m, v):
    m = ADAM_B1 * m + (1.0 - ADAM_B1) * g
    v = ADAM_B2 * v + (1.0 - ADAM_B2) * jnp.square(g)
    m_hat = m / (1.0 - ADAM_B1 ** ADAM_STEP)
    v_hat = v / (1.0 - ADAM_B2 ** ADAM_STEP)
    delta = -ADAM_LR * (m_hat / (jnp.sqrt(v_hat) + ADAM_EPS) + ADAM_WD * w)
    return delta, m, v


def _wada_adamw(c_all_t, dmod, w, m, v, name):
    def body(c_ref, d_ref, w_ref, m_ref, v_ref, g_ref, dl_ref, mo_ref, vo_ref):
        cv = c_ref[...]
        ca = cv * _sigmoid(cv)
        g = _dot(ca, d_ref[0], HI)
        g_ref[0] = g
        dl_ref[0], mo_ref[0], vo_ref[0] = _adam(g, w_ref[0], m_ref[0], v_ref[0])

    blk = pl.BlockSpec((1, D, ADA_SHARD), lambda l: (l, 0, 0))
    shp = jax.ShapeDtypeStruct((DEPTH, D, ADA_SHARD), F32)
    return _call(
        body, name, (DEPTH,),
        [_full((D, NDEV)), pl.BlockSpec((1, NDEV, ADA_SHARD), lambda l: (l, 0, 0)), blk, blk, blk],
        [blk] * 4, [shp] * 4,
    )(c_all_t, dmod, w, m, v)


def _adamw_reduce(parts, w, m, v, name, tr):
    P, R, C = parts.shape

    def body(p_ref, w_ref, m_ref, v_ref, g_ref, dl_ref, mo_ref, vo_ref):
        g = p_ref[0].astype(F32)
        for k in range(1, P):
            g = g + p_ref[k].astype(F32)
        g_ref[...] = g
        dl_ref[...], mo_ref[...], vo_ref[...] = _adam(g, w_ref[...], m_ref[...], v_ref[...])

    blk = pl.BlockSpec((tr, C), lambda i: (i, 0))
    shp = jax.ShapeDtypeStruct((R, C), F32)
    return _call(body, name, (R // tr,), [pl.BlockSpec((P, tr, C), lambda i: (0, i, 0)), blk, blk, blk],
                 [blk] * 4, [shp] * 4)(parts, w, m, v)


def _sum_parts(parts, name):
    P, R, C = parts.shape

    def body(p_ref, o_ref):
        g = p_ref[0]
        for k in range(1, P):
            g = g + p_ref[k]
        o_ref[...] = g

    return _call(body, name, (1,), [_full((P, R, C))], _full((R, C)),
                 jax.ShapeDtypeStruct((R, C), F32))(parts)


def _pack_rows(vecs, width=1024):
    flat = jnp.concatenate([a.reshape(-1) for a in vecs])
    n = flat.shape[0]
    rows = -(-n // width)
    rows = -(-rows // 8) * 8
    return jnp.pad(flat, (0, rows * width - n)).reshape(rows, width)


def _unpack_rows(packed, shapes):
    flat = packed.reshape(-1)
    out, off = [], 0
    for s in shapes:
        n = int(np.prod(s)) if len(s) else 1
        out.append(flat[off:off + n].reshape(s))
        off += n
    return out


BIG = ("w_in", "w_out", "w_ff_in", "w_ff_out")


def _z_weights(g_in):
    w = jnp.transpose(g_in, (1, 0, 2)).reshape(D, IN_W)
    return jnp.concatenate([w[:, :Z_ATT], w[:, Z_ATT + 2 * NH:], w[:, Z_ATT:Z_ATT + 2 * NH],
                            jnp.zeros((D, ZW - IN_W), MXU)], axis=-1)


def _cols_to_owners(a):
    return jnp.transpose(a.reshape(a.shape[0], NDEV, -1), (1, 0, 2))


def _rows_to_owners(a):
    return a.reshape(NDEV, -1, a.shape[1])


def _forward_layer(x, mod_l, p, shard, next_w_in):
    sh1, sc1, gt1, sh2, sc2, gt2 = [mod_l[k][None] for k in range(6)]
    P = functools.partial
    gather = lambda a: None if a is None else ("gather", a)
    h = _modnorm_fwd(x, p["norm_mix"], sc1, sh1, "norm_mix_fwd")
    z, g_in = _hosted(P(_mm_nn, h, p["wz"], "in_proj", tm=2048, tn=512), gather(next_w_in))
    (qn, kn, v, gb), g_w1 = _hosted(P(_gdn_prep_fwd, z, p["conv_w"], p["alog"], p["dtb"], "gdn_prep_fwd"),
                                    gather(shard["w_ff_in"]))
    (u, w, qd, kd, attn, tmat, cdt), g_w2 = _hosted(P(_gdn_intra_fwd, qn, kn, v, gb, "gdn_intra_fwd"),
                                                    gather(shard["w_ff_out"]))
    o, vn, st = _gdn_scan_fwd(u, w, qd, kd, attn, cdt, "gdn_scan_fwd")
    (ob, lse), g_out = _hosted(P(_att_fwd, z, p["bmask"], "att_fwd"), gather(shard["w_out"]))
    wout = g_out.reshape(D, D)
    w1 = jnp.transpose(g_w1, (1, 0, 2)).reshape(D, DFF)
    w2 = g_w2.reshape(DFF, D)
    m = _merge_fwd(o, z, ob, p["gdn_norm"], "merge_fwd")
    x1, h2 = _mm_nn(m, wout, "out_proj", mode="resid_norm", res=x, gate=gt1, norm=(p["norm_mlp"], sc2, sh2))
    a, r = _mm_nn(h2, w1, "ff_in", mode="relu2")
    x2 = _mm_nn(r, w2, "ff_out", mode="resid", res=x1, gate=gt2)
    saved = dict(x=x, h=h, z=z, qn=qn, kn=kn, v=v, gb=gb, u=u, w=w, qd=qd, kd=kd, attn=attn,
                 tmat=tmat, cdt=cdt, o=o, vn=vn, st=st, ob=ob, lse=lse, m=m, x1=x1, h2=h2, a=a, r=r,
                 wout=wout, w1=w1, w2=w2)
    return x2, saved, g_in


def _backward_layer(dx2, mod_l, p, s, onehot):
    sh1, sc1, gt1, sh2, sc2, gt2 = [mod_l[k][None] for k in range(6)]
    P = functools.partial
    dw2, dgt2 = _mm_tn(s["r"], dx2, "ff_out_dw", gate=gt2, w=s["w2"], out_dtype=MXU)
    da, r_w2 = _hosted(P(_mm_nt, dx2, s["w2"], "ff_out_dx", gate=gt2, drelu=s["a"]), ("a2a", _rows_to_owners(dw2)))
    dw1 = _mm_tn(s["h2"], da, "ff_in_dw", out_dtype=MXU)
    dh2, r_w1 = _hosted(P(_mm_nt, da, s["w1"], "ff_in_dx"), ("a2a", _cols_to_owners(dw1)))
    dx1, dsc2, dsh2, dnmlp = _modnorm_bwd(dh2, s["x1"], p["norm_mlp"], sc2, sh2, dx2, "norm_mlp_bwd")
    dwout, dgt1 = _mm_tn(s["m"], dx1, "out_proj_dw", gate=gt1, w=s["wout"], out_dtype=MXU)
    dm, r_out = _hosted(P(_mm_nt, dx1, s["wout"], "out_proj_dx", gate=gt1), ("a2a", _rows_to_owners(dwout)))
    do, dzg, dob, dza, dzb, dgn = _merge_bwd(dm, s["o"], s["z"], s["ob"], p["gdn_norm"], "merge_bwd")
    dq_att, dk_att, dv_att, dbias = _att_bwd(s["z"], p["bmask"], s["ob"], s["lse"], dob, "att_bwd")
    drb = _bias_fold(dbias, onehot, "rel_bias_fold")[:, :513]
    dqd, dkd, dvn, dw, dattn, dcdt = _gdn_scan_bwd(do, s["w"], s["qd"], s["kd"], s["attn"], s["cdt"],
                                                   s["vn"], s["st"], "gdn_scan_bwd")
    dqn, dkn, dv, dgb = _gdn_intra_bwd(s["qn"], s["kn"], s["v"], s["gb"], s["u"], s["w"], s["tmat"],
                                       dqd, dkd, dvn, dw, dattn, dcdt, "gdn_intra_bwd")
    dzq, dzab, dcw, dvec = _gdn_prep_bwd(s["z"], dqn, dkn, dv, dgb, p["conv_w"], p["alog"], p["dtb"],
                                         "gdn_prep_bwd")
    dz = (dzq, dzg, dq_att, dk_att, dv_att, dza, dzb, dzab)
    dwz = _in_proj_dw(s["h"], dz, "in_proj_dw")
    dw_in = jnp.concatenate([dwz[:, :Z_ATT], dwz[:, Z_AB:Z_AB + 2 * NH], dwz[:, Z_ATT:Z_AB]], axis=1)
    dh, r_in = _hosted(P(_in_proj_dx, dz, p["wz"], "in_proj_dx"), ("a2a", _cols_to_owners(dw_in)))
    dx, dsc1, dsh1, dnmix = _modnorm_bwd(dh, s["x"], p["norm_mix"], sc1, sh1, dx1, "norm_mix_bwd")
    grads = dict(norm_mix=dnmix[0], norm_mlp=dnmlp[0], conv_w=dcw[:4], a_log=dvec[0, :NH], dt_bias=dvec[1, :NH],
                 gdn_norm=dgn[0], rel_bias=drb, mod=jnp.concatenate([dsh1, dsc1, dgt1, dsh2, dsc2, dgt2], axis=1)[0])
    return dx, grads, dict(w_in=r_in, w_out=r_out, w_ff_in=r_w1, w_ff_out=r_w2)


def _bias_onehot():
    return (jnp.asarray(_bias_index())[:, :, None] == jnp.arange(640)[None, None, :]).astype(F32)


def _layer_params(l, conv_full, norm_mix, norm_mlp, a_log, dt_bias, gdn_norm, rel_bias, onehot):
    pad = lambda a: jnp.pad(a, (0, 128 - NH))[None]
    fvec = _bias_vec(jnp.pad(rel_bias[l], ((0, 0), (0, 640 - rel_bias.shape[2]))), onehot, "rel_bias_vec")
    return dict(conv_w=conv_full[l], norm_mix=norm_mix[l][None], norm_mlp=norm_mlp[l][None], alog=pad(a_log[l]),
                dtb=pad(dt_bias[l]), gdn_norm=gdn_norm[l][None], bmask=_bias_mask(fvec, "rel_bias_mask"))


def _local_step(x, target, mod, small, shards, final_norm, onehot):
    L = len(small)
    g_in = _all_gather(shards[0]["w_in"], "gather_w_in")
    saved, params = [], []
    for l in range(L):
        params.append({**small[l], "wz": _z_weights(g_in)})
        x, sv, g_in = _forward_layer(x, mod[l].reshape(6, D), params[l], shards[l],
                                     shards[l + 1]["w_in"] if l + 1 < L else None)
        saved.append(sv)
    loss, dx, dfn = _loss_head(x, target, final_norm[None], "loss_head")
    grads, recv = [None] * L, [None] * L
    for l in reversed(range(L)):
        dx, grads[l], recv[l] = _backward_layer(dx, mod[l].reshape(6, D), params[l], saved[l], onehot)
    return loss, dx, grads, dfn[0], recv


SMALL = ("b_ada", "norm_mix", "norm_mlp", "a_log", "dt_bias", "gdn_norm", "rel_bias", "final_norm")


def kernel(x, c, w_ada, b_ada, norm_mix, norm_mlp, w_in, conv_w, a_log, dt_bias, gdn_norm, rel_bias, w_out, w_ff_in, w_ff_out, final_norm, loss_target, m_w_ada, m_b_ada, m_norm_mix, m_norm_mlp, m_w_in, m_conv_w, m_a_log, m_dt_bias, m_gdn_norm, m_rel_bias, m_w_out, m_w_ff_in, m_w_ff_out, m_final_norm, v_w_ada, v_b_ada, v_norm_mix, v_norm_mlp, v_w_in, v_conv_w, v_a_log, v_dt_bias, v_gdn_norm, v_rel_bias, v_w_out, v_w_ff_in, v_w_ff_out, v_final_norm):
    W = dict(w_ada=w_ada, b_ada=b_ada, norm_mix=norm_mix, norm_mlp=norm_mlp, w_in=w_in, conv_w=conv_w,
             a_log=a_log, dt_bias=dt_bias, gdn_norm=gdn_norm, rel_bias=rel_bias, w_out=w_out,
             w_ff_in=w_ff_in, w_ff_out=w_ff_out, final_norm=final_norm)
    Mo = dict(w_ada=m_w_ada, b_ada=m_b_ada, norm_mix=m_norm_mix, norm_mlp=m_norm_mlp, w_in=m_w_in,
              conv_w=m_conv_w, a_log=m_a_log, dt_bias=m_dt_bias, gdn_norm=m_gdn_norm, rel_bias=m_rel_bias,
              w_out=m_w_out, w_ff_in=m_w_ff_in, w_ff_out=m_w_ff_out, final_norm=m_final_norm)
    Vo = dict(w_ada=v_w_ada, b_ada=v_b_ada, norm_mix=v_norm_mix, norm_mlp=v_norm_mlp, w_in=v_w_in,
              conv_w=v_conv_w, a_log=v_a_log, dt_bias=v_dt_bias, gdn_norm=v_gdn_norm, rel_bias=v_rel_bias,
              w_out=v_w_out, w_ff_in=v_w_ff_in, w_ff_out=v_w_ff_out, final_norm=v_final_norm)
    L = w_in.shape[0]
    me = _flat(_mesh_pos())
    cshard = conv_w.shape[2]

    small_in = _all_gather(_pack_rows([c, conv_w]), "gather_c_conv")
    c_all = small_in[:, 0, :]
    conv_full = small_in.reshape(NDEV, -1)[:, D:D + L * 4 * cshard].reshape(NDEV, L, 4, cshard)
    conv_full = jnp.transpose(conv_full, (1, 2, 0, 3)).reshape(L, 4, NDEV * cshard)

    b_shard = lax.dynamic_slice_in_dim(b_ada, me * ADA_SHARD, ADA_SHARD, axis=1)
    mod_all = _all_gather(_ada_mod(c_all, w_ada, b_shard, "ada_mod"), "gather_mod")
    mod = lax.dynamic_index_in_dim(mod_all, me, axis=2, keepdims=False)
    mod = jnp.transpose(mod, (1, 0, 2)).reshape(L, 6 * D)

    onehot = _bias_onehot()
    small = [_layer_params(l, conv_full, norm_mix, norm_mlp, a_log, dt_bias, gdn_norm, rel_bias, onehot)
             for l in range(L)]
    shards = [{n: W[n][l].astype(MXU) for n in BIG} for l in range(L)]
    loss, dx, grads, dfn, recv = _local_step(x[0], loss_target[0], mod, small, shards, final_norm, onehot)

    def stack(name):
        return jnp.stack([g[name] for g in grads])

    small_names = ("mod", "norm_mix", "norm_mlp", "a_log", "dt_bias", "gdn_norm", "rel_bias")
    small_parts = [stack(n) for n in small_names] + [dfn, stack("conv_w"), loss[0, 0:1]]
    small_shapes = [a.shape for a in small_parts]
    gathered = _all_gather(_pack_rows(small_parts), "gather_small_grads")
    total = _unpack_rows(_sum_parts(gathered, "sum_small_grads"), small_shapes)
    tot = dict(zip(small_names + ("final_norm", "conv_w", "loss"), total))
    tot["b_ada"] = tot.pop("mod")
    tot["conv_w"] = lax.dynamic_slice_in_dim(tot["conv_w"], me * cshard, cshard, axis=2)

    out_g, out_d, out_m, out_v = {}, {}, {}, {}
    names = SMALL + ("conv_w",)
    shapes = [W[n].shape for n in names]
    packed = [_pack_rows([src[n] for n in names])[None] if src is tot else _pack_rows([src[n] for n in names])
              for src in (tot, W, Mo, Vo)]
    res = _adamw_reduce(*packed, "adamw_small", tr=8)
    for dst, arr in zip((out_g, out_d, out_m, out_v), res):
        dst.update(zip(names, _unpack_rows(arr, shapes)))

    dmod_all = gathered.reshape(NDEV, -1)[:, :L * 6 * D].reshape(NDEV, L, 6 * D)
    dmod_mine = jnp.transpose(lax.dynamic_slice_in_dim(dmod_all, me * ADA_SHARD, ADA_SHARD, axis=2), (1, 0, 2))
    res = _wada_adamw(jnp.transpose(c_all), dmod_mine, w_ada, m_w_ada, v_w_ada, "adamw_w_ada")
    for dst, arr in zip((out_g, out_d, out_m, out_v), res):
        dst["w_ada"] = arr

    for name, tr in (("w_in", 256), ("w_out", 128), ("w_ff_in", 256), ("w_ff_out", 256)):
        sh = W[name].shape
        rows = int(np.prod(sh[:-1]))
        flat = lambda a: a.reshape(rows, sh[-1])
        parts = jnp.stack([recv[l][name] for l in range(L)], axis=1).reshape(NDEV, rows, sh[-1])
        res = _adamw_reduce(parts, flat(W[name]), flat(Mo[name]), flat(Vo[name]), "adamw_" + name, tr=tr)
        for dst, arr in zip((out_g, out_d, out_m, out_v), res):
            dst[name] = arr.reshape(sh)

    order = ("w_ada", "b_ada", "norm_mix", "norm_mlp", "w_in", "conv_w", "a_log", "dt_bias", "gdn_norm",
             "rel_bias", "w_out", "w_ff_in", "w_ff_out", "final_norm")
    return (tot["loss"].reshape(()), dx[None], *[out_g[n] for n in order], *[out_d[n] for n in order],
            *[out_m[n] for n in order], *[out_v[n] for n in order])
```

```python
import functools
import math

import numpy as np
import jax
import jax.numpy as jnp
from jax import lax
from jax.experimental import pallas as pl
from jax.experimental.pallas import tpu as pltpu

F32 = jnp.float32
MXU = jnp.bfloat16
HI = lax.Precision.HIGHEST
MESH_ID = pl.DeviceIdType.MESH

D = 1024
NH = 8
HD = 128
CH = 64
PAST = 8
DFF = 4096
EPS = 1e-6
NDEV = 8
DEPTH = 4
IN_W = 9232
ZW = 9728
Z_GATE, Z_ATT, Z_BR, Z_AB = 3072, 4096, 7168, 9216
QB = 256
MASKED = -1e30
KWIN = 768
FW = 1024
ADAM_LR, ADAM_B1, ADAM_B2, ADAM_EPS, ADAM_WD, ADAM_STEP = 0.001, 0.9, 0.999, 1e-08, 0.01, 10


def _dot(a, b, prec=None):
    return jnp.dot(a, b, preferred_element_type=F32, precision=prec)


def _dot_nt(a, b, prec=None):
    return lax.dot_general(a, b, (((1,), (1,)), ((), ())), preferred_element_type=F32, precision=prec)


def _dot_tn(a, b, prec=None):
    return lax.dot_general(a, b, (((0,), (0,)), ((), ())), preferred_element_type=F32, precision=prec)


def _mx(a):
    return a.astype(MXU)


def _sigmoid(x):
    return 0.5 * jnp.tanh(0.5 * x) + 0.5


def _softplus(x):
    return jnp.maximum(x, 0.0) + jnp.log(1.0 + jnp.exp(-jnp.abs(x)))


def _rowsum(x):
    return jnp.sum(x, axis=1, keepdims=True)


def _colsum(x):
    return jnp.sum(x, axis=0, keepdims=True)


def _call(body, name, grid, in_specs, out_specs, out_shape, scratch=(), comm=None):
    if comm is None:
        return pl.pallas_call(body, name=name, grid=grid, in_specs=in_specs, out_specs=out_specs,
                              out_shape=out_shape, scratch_shapes=list(scratch))
    kind, x = comm
    single = not isinstance(out_specs, (list, tuple))
    o_specs = [out_specs] if single else list(out_specs)
    o_shape = [out_shape] if single else list(out_shape)
    n_in, n_out, n_scr = len(in_specs), len(o_specs), len(scratch)
    c_shape = (NDEV,) + x.shape if kind == "gather" else x.shape

    def wrapped(*refs):
        ins, x_ref = refs[:n_in], refs[n_in]
        outs, c_ref = refs[n_in + 1:n_in + 1 + n_out], refs[n_in + 1 + n_out]
        scr = refs[n_in + 2 + n_out:n_in + 2 + n_out + n_scr]
        sems = refs[n_in + 2 + n_out + n_scr:]
        first = functools.reduce(jnp.logical_and, [pl.program_id(a) == 0 for a in range(len(grid))])
        last = functools.reduce(jnp.logical_and, [pl.program_id(a) == grid[a] - 1 for a in range(len(grid))])

        @pl.when(first)
        def _():
            _comm_start(*_comm_copies(kind, x_ref, c_ref, *sems))

        body(*ins, *outs, *scr)

        @pl.when(last)
        def _():
            _comm_wait(*_comm_copies(kind, x_ref, c_ref, *sems))

    any_spec = pl.BlockSpec(memory_space=pl.ANY)
    call = pl.pallas_call(
        wrapped, name=name, grid=grid, in_specs=list(in_specs) + [any_spec], out_specs=o_specs + [any_spec],
        out_shape=o_shape + [jax.ShapeDtypeStruct(c_shape, x.dtype)],
        scratch_shapes=list(scratch) + _comm_sems())

    def run(*args):
        res = call(*args, x)
        return (res[0] if single else list(res[:-1])), res[-1]

    return run


def _hosted(fn, comm):
    return (fn(), None) if comm is None else fn(comm=comm)


def _full(shape):
    n = len(shape)
    return pl.BlockSpec(shape, lambda *_: (0,) * n)


def _mesh_pos():
    return lax.axis_index("x"), lax.axis_index("y"), lax.axis_index("c")


def _peer(pos, k):
    x, y, c = pos
    return (x ^ ((k >> 2) & 1), y ^ ((k >> 1) & 1), c ^ (k & 1))


def _flat(pos):
    return 4 * pos[0] + 2 * pos[1] + pos[2]


def _comm_sems():
    return [pltpu.SemaphoreType.DMA((NDEV - 1,)), pltpu.SemaphoreType.DMA((NDEV - 1,)), pltpu.SemaphoreType.DMA]


def _comm_copies(kind, x_ref, out_ref, send_sems, recv_sems, local_sem):
    pos = _mesh_pos()
    me = _flat(pos)
    src = (lambda d: x_ref) if kind == "gather" else (lambda d: x_ref.at[d])
    mine = pltpu.make_async_copy(src(me), out_ref.at[me], local_sem)
    sends, recvs = [], []
    for k in range(1, NDEV):
        peer = _peer(pos, k)
        pid = _flat(peer)
        sems = dict(send_sem=send_sems.at[k - 1], recv_sem=recv_sems.at[k - 1], device_id=peer,
                    device_id_type=MESH_ID)
        sends.append(pltpu.make_async_remote_copy(src_ref=src(pid), dst_ref=out_ref.at[me], **sems))
        recvs.append(pltpu.make_async_remote_copy(src_ref=src(pid), dst_ref=out_ref.at[pid], **sems))
    return mine, sends, recvs


def _comm_start(mine, sends, recvs):
    mine.start()
    for cp in sends:
        cp.start()


def _comm_wait(mine, sends, recvs):
    for cp in recvs:
        cp.wait_recv()
    for cp in sends:
        cp.wait_send()
    mine.wait()


def _collective(kind, x, name):
    def body(x_ref, out_ref, *sems):
        copies = _comm_copies(kind, x_ref, out_ref, *sems)
        _comm_start(*copies)
        _comm_wait(*copies)

    shape = (NDEV,) + x.shape if kind == "gather" else x.shape
    return pl.pallas_call(
        body, name=name, out_shape=jax.ShapeDtypeStruct(shape, x.dtype),
        in_specs=[pl.BlockSpec(memory_space=pl.ANY)], out_specs=pl.BlockSpec(memory_space=pl.ANY),
        scratch_shapes=_comm_sems())(x)


def _all_gather(x, name):
    return _collective("gather", x, name)


def _mm_nn(a, w, name, *, mode="plain", res=None, gate=None, norm=None, tm=1024, tn=1024, tk=1024, comm=None):
    M, K = a.shape
    N = w.shape[1]
    tm, tk, tn = min(tm, M), min(tk, K), min(tn, N)
    nk = K // tk

    def body(*refs):
        refs = list(refs)
        acc = refs.pop() if nk > 1 else None
        if mode == "resid":
            a_ref, w_ref, res_ref, gate_ref, o_ref = refs
        elif mode == "resid_norm":
            a_ref, w_ref, res_ref, gate_ref, g_ref, sc_ref, sh_ref, o_ref, r_ref = refs
        elif mode == "relu2":
            a_ref, w_ref, o_ref, r_ref = refs
        else:
            a_ref, w_ref, o_ref = refs
        k = pl.program_id(2)
        part = _dot(_mx(a_ref[...]), w_ref[...])

        def finish(r):
            if mode == "resid":
                o_ref[...] = res_ref[...] + gate_ref[...] * r
            elif mode == "resid_norm":
                xv = res_ref[...] + gate_ref[...] * r
                o_ref[...] = xv
                rs = lax.rsqrt(jnp.mean(xv * xv, axis=1, keepdims=True) + EPS)
                r_ref[...] = ((xv * rs * g_ref[...]) * (1.0 + sc_ref[...]) + sh_ref[...]).astype(r_ref.dtype)
            elif mode == "relu2":
                o_ref[...] = r
                r_ref[...] = jnp.square(jnp.maximum(r, 0.0)).astype(r_ref.dtype)
            else:
                o_ref[...] = r

        if nk == 1:
            finish(part)
        else:
            @pl.when(k == 0)
            def _():
                acc[...] = part

            @pl.when((k > 0) & (k < nk - 1))
            def _():
                acc[...] += part

            @pl.when(k == nk - 1)
            def _():
                finish(acc[...] + part)

    in_specs = [pl.BlockSpec((tm, tk), lambda i, j, k: (i, k)),
                pl.BlockSpec((tk, tn), lambda i, j, k: (k, j))]
    args = [a, w]
    o_spec = pl.BlockSpec((tm, tn), lambda i, j, k: (i, j))
    out_specs, out_shape = o_spec, jax.ShapeDtypeStruct((M, N), F32)
    vec = pl.BlockSpec((1, tn), lambda i, j, k: (0, j))
    if mode == "resid":
        in_specs += [o_spec, vec]
        args += [res, gate]
    elif mode == "resid_norm":
        assert tn == N
        in_specs += [o_spec, vec, vec, vec, vec]
        args += [res, gate, *norm]
        out_specs = [o_spec, o_spec]
        out_shape = [jax.ShapeDtypeStruct((M, N), F32), jax.ShapeDtypeStruct((M, N), MXU)]
    elif mode == "relu2":
        out_specs = [o_spec, o_spec]
        out_shape = [jax.ShapeDtypeStruct((M, N), F32), jax.ShapeDtypeStruct((M, N), MXU)]
    return _call(body, name, (M // tm, N // tn, nk), in_specs, out_specs, out_shape,
                 [pltpu.VMEM((tm, tn), F32)] if nk > 1 else [], comm=comm)(*args)


def _mm_nt(a, w, name, *, gate=None, drelu=None, tm=1024, tko=1024, tn=1024, comm=None):
    M, N = a.shape
    K = w.shape[0]
    tm, tn, tko = min(tm, M), min(tn, N), min(tko, K)
    nn = N // tn

    def body(*refs):
        refs = list(refs)
        acc = refs.pop() if nn > 1 else None
        a_ref, w_ref = refs[:2]
        rest = refs[2:]
        gate_ref = rest.pop(0) if gate is not None else None
        pre_ref = rest.pop(0) if drelu is not None else None
        (o_ref,) = rest
        n = pl.program_id(2)
        av = a_ref[...]
        if gate_ref is not None:
            av = av * gate_ref[...]
        part = _dot_nt(_mx(av), w_ref[...])

        def finish(r):
            if pre_ref is not None:
                r = r * (2.0 * jnp.maximum(pre_ref[...], 0.0))
            o_ref[...] = r.astype(o_ref.dtype)

        if nn == 1:
            finish(part)
        else:
            @pl.when(n == 0)
            def _():
                acc[...] = part

            @pl.when((n > 0) & (n < nn - 1))
            def _():
                acc[...] += part

            @pl.when(n == nn - 1)
            def _():
                finish(acc[...] + part)

    in_specs = [pl.BlockSpec((tm, tn), lambda i, j, n: (i, n)),
                pl.BlockSpec((tko, tn), lambda i, j, n: (j, n))]
    args = [a, w]
    if gate is not None:
        in_specs.append(pl.BlockSpec((1, tn), lambda i, j, n: (0, n)))
        args.append(gate)
    o_spec = pl.BlockSpec((tm, tko), lambda i, j, n: (i, j))
    if drelu is not None:
        in_specs.append(o_spec)
        args.append(drelu)
    out_dtype = MXU if drelu is not None else F32
    return _call(body, name, (M // tm, K // tko, nn), in_specs, o_spec,
                 jax.ShapeDtypeStruct((M, K), out_dtype), [pltpu.VMEM((tm, tko), F32)] if nn > 1 else [],
                 comm=comm)(*args)


def _mm_tn(a, b, name, *, gate=None, w=None, tk=1024, tn=1024, tm=1024, out_dtype=F32, comm=None):
    M, K = a.shape
    N = b.shape[1]
    tm, tk, tn = min(tm, M), min(tk, K), min(tn, N)
    nm = M // tm
    gated = gate is not None

    def body(*refs):
        if gated:
            a_ref, b_ref, gate_ref, w_ref, o_ref, dg_ref, acc = refs
        else:
            a_ref, b_ref, o_ref, acc = refs
        kk = pl.program_id(1)
        m = pl.program_id(2)
        part = _dot_tn(_mx(a_ref[...]), _mx(b_ref[...]))

        @pl.when((m == 0) & (nm > 1))
        def _():
            acc[...] = part

        @pl.when((m > 0) & (m < nm - 1))
        def _():
            acc[...] += part

        if gated:
            @pl.when((m == 0) & (kk == 0))
            def _():
                dg_ref[...] = jnp.zeros_like(dg_ref)

        @pl.when(m == nm - 1)
        def _():
            r = acc[...] + part if nm > 1 else part
            if gated:
                o_ref[...] = (r * gate_ref[...]).astype(o_ref.dtype)
                dg_ref[...] += _colsum(r * w_ref[...].astype(F32))
            else:
                o_ref[...] = r.astype(o_ref.dtype)

    in_specs = [pl.BlockSpec((tm, tk), lambda j, k, m: (m, k)),
                pl.BlockSpec((tm, tn), lambda j, k, m: (m, j))]
    args = [a, b]
    o_spec = pl.BlockSpec((tk, tn), lambda j, k, m: (k, j))
    out_specs, out_shape = o_spec, jax.ShapeDtypeStruct((K, N), out_dtype)
    if gated:
        in_specs += [pl.BlockSpec((1, tn), lambda j, k, m: (0, j)), o_spec]
        args += [gate, w]
        out_specs = [o_spec, pl.BlockSpec((1, tn), lambda j, k, m: (0, j))]
        out_shape = [out_shape, jax.ShapeDtypeStruct((1, N), F32)]
    return _call(body, name, (N // tn, K // tk, nm), in_specs, out_specs, out_shape,
                 [pltpu.VMEM((tk, tn), F32)], comm=comm)(*args)


SEG_T = 512


def _seg_layout(segs):
    starts, t = [], 0
    for a in segs:
        starts.append(t)
        t += a.shape[1] // SEG_T
    return starts, t


def _seg_spec(tm, lo, hi, row_axis, col_axis):
    def index(*ids):
        col = ids[col_axis]
        act = (col >= lo) & (col < hi)
        return jnp.where(act, ids[row_axis], 0), jnp.where(act, col - lo, 0)

    return pl.BlockSpec((tm, SEG_T), index)


def _in_proj_dw(ht, segs, name, tm=1024):
    S = ht.shape[1]
    tm = min(tm, S)
    nm = S // tm
    starts, ntile = _seg_layout(segs)
    bounds = [(lo, lo + a.shape[1] // SEG_T) for lo, a in zip(starts, segs)]

    def body(*refs):
        h_ref, seg_refs, o_ref, acc = refs[0], refs[1:1 + len(segs)], refs[-2], refs[-1]
        j = pl.program_id(0)
        m = pl.program_id(1)
        for (lo, hi), b_ref in zip(bounds, seg_refs):
            @pl.when((j >= lo) & (j < hi))
            def _():
                part = _dot(h_ref[...], _mx(b_ref[...]))
                if nm == 1:
                    o_ref[...] = part.astype(o_ref.dtype)
                else:
                    @pl.when(m == 0)
                    def _():
                        acc[...] = part

                    @pl.when((m > 0) & (m < nm - 1))
                    def _():
                        acc[...] += part

                    @pl.when(m == nm - 1)
                    def _():
                        o_ref[...] = (acc[...] + part).astype(o_ref.dtype)

    return _call(
        body, name, (ntile, nm),
        [pl.BlockSpec((D, tm), lambda j, m: (0, m))] + [_seg_spec(tm, lo, hi, 1, 0) for lo, hi in bounds],
        pl.BlockSpec((D, SEG_T), lambda j, m: (0, j)), jax.ShapeDtypeStruct((D, ntile * SEG_T), MXU),
        [pltpu.VMEM((D, SEG_T), F32)])(ht, *segs)


def _in_proj_dx(segs, w, name, tm=1024, comm=None):
    S = segs[0].shape[0]
    tm = min(tm, S)
    starts, ntile = _seg_layout(segs)
    bounds = [(lo, lo + a.shape[1] // SEG_T) for lo, a in zip(starts, segs)]

    def body(*refs):
        seg_refs, w_ref, o_ref, acc = refs[:len(segs)], refs[-3], refs[-2], refs[-1]
        n = pl.program_id(1)
        for (lo, hi), a_ref in zip(bounds, seg_refs):
            @pl.when((n >= lo) & (n < hi))
            def _():
                part = _dot_nt(_mx(a_ref[...]), w_ref[...])

                @pl.when(n == 0)
                def _():
                    acc[...] = part

                @pl.when((n > 0) & (n < ntile - 1))
                def _():
                    acc[...] += part

                @pl.when(n == ntile - 1)
                def _():
                    o_ref[...] = acc[...] + part

    return _call(
        body, name, (S // tm, ntile),
        [_seg_spec(tm, lo, hi, 0, 1) for lo, hi in bounds] + [pl.BlockSpec((D, SEG_T), lambda i, n: (0, n))],
        pl.BlockSpec((tm, D), lambda i, n: (i, 0)), jax.ShapeDtypeStruct((S, D), F32),
        [pltpu.VMEM((tm, D), F32)], comm=comm)(*segs, w)


def _modnorm_fwd(x, gain, sc, sh, name, ts=512):
    S = x.shape[0]

    def body(x_ref, g_ref, sc_ref, sh_ref, h_ref, ht_ref):
        xv = x_ref[...]
        r = lax.rsqrt(jnp.mean(xv * xv, axis=1, keepdims=True) + EPS)
        h = (xv * r * g_ref[...]) * (1.0 + sc_ref[...]) + sh_ref[...]
        h_ref[...] = h.astype(h_ref.dtype)
        ht_ref[...] = h.T.astype(ht_ref.dtype)

    row = pl.BlockSpec((ts, D), lambda i: (i, 0))
    vec = pl.BlockSpec((1, D), lambda i: (0, 0))
    return _call(body, name, (S // ts,), [row, vec, vec, vec], [row, pl.BlockSpec((D, ts), lambda i: (0, i))],
                 [jax.ShapeDtypeStruct((S, D), MXU), jax.ShapeDtypeStruct((D, S), MXU)])(x, gain, sc, sh)


def _modnorm_bwd(dh, x, gain, sc, sh, dx_in, name, ts=512):
    S = x.shape[0]

    def body(dh_ref, x_ref, g_ref, sc_ref, sh_ref, dxin_ref, dx_ref, dsc_ref, dsh_ref, dg_ref):
        i = pl.program_id(0)
        xv = x_ref[...]
        dhv = dh_ref[...]
        g = g_ref[...]
        r = lax.rsqrt(jnp.mean(xv * xv, axis=1, keepdims=True) + EPS)
        xr = xv * r
        dn = dhv * (1.0 + sc_ref[...])
        u = dn * g
        dx_ref[...] = dxin_ref[...] + r * (u - xr * jnp.mean(xr * u, axis=1, keepdims=True))

        @pl.when(i == 0)
        def _():
            dsc_ref[...] = jnp.zeros_like(dsc_ref)
            dsh_ref[...] = jnp.zeros_like(dsh_ref)
            dg_ref[...] = jnp.zeros_like(dg_ref)

        dsc_ref[...] += _colsum(dhv * (xr * g))
        dsh_ref[...] += _colsum(dhv)
        dg_ref[...] += _colsum(dn * xr)

    row = pl.BlockSpec((ts, D), lambda i: (i, 0))
    vec = pl.BlockSpec((1, D), lambda i: (0, 0))
    vshape = jax.ShapeDtypeStruct((1, D), F32)
    return _call(body, name, (S // ts,), [row, row, vec, vec, vec, row], [row, vec, vec, vec],
                 [jax.ShapeDtypeStruct((S, D), F32), vshape, vshape, vshape])(dh, x, gain, sc, sh, dx_in)


def _loss_head(x, target, gain, name, ts=512):
    S = x.shape[0]

    def body(x_ref, t_ref, g_ref, loss_ref, dx_ref, dg_ref):
        i = pl.program_id(0)
        xv = x_ref[...]
        g = g_ref[...]
        r = lax.rsqrt(jnp.mean(xv * xv, axis=1, keepdims=True) + EPS)
        xr = xv * r
        e = xr * g - t_ref[...]
        dy = e * (1.0 / D)
        u = dy * g
        dx_ref[...] = r * (u - xr * jnp.mean(xr * u, axis=1, keepdims=True))

        @pl.when(i == 0)
        def _():
            loss_ref[...] = jnp.zeros_like(loss_ref)
            dg_ref[...] = jnp.zeros_like(dg_ref)

        part = 0.5 * jnp.sum(jnp.mean(e * e, axis=1, keepdims=True), axis=0, keepdims=True)
        loss_ref[...] += jnp.broadcast_to(part, loss_ref.shape)
        dg_ref[...] += _colsum(dy * xr)

    row = pl.BlockSpec((ts, D), lambda i: (i, 0))
    vec = pl.BlockSpec((1, D), lambda i: (0, 0))
    return _call(body, name, (S // ts,), [row, row, vec],
                 [pl.BlockSpec((1, 128), lambda i: (0, 0)), row, vec],
                 [jax.ShapeDtypeStruct((1, 128), F32), jax.ShapeDtypeStruct((S, D), F32),
                  jax.ShapeDtypeStruct((1, D), F32)])(x, target, gain)


def _merge_specs(ts):
    o_spec = pl.BlockSpec((NH, ts, HD), lambda i: (0, i, 0))
    zg = pl.BlockSpec((ts, D), lambda i: (i, Z_GATE // D))
    za = pl.BlockSpec((ts, D), lambda i: (i, Z_BR // D))
    zb = pl.BlockSpec((ts, D), lambda i: (i, Z_BR // D + 1))
    row = pl.BlockSpec((ts, D), lambda i: (i, 0))
    gn = pl.BlockSpec((1, HD), lambda i: (0, 0))
    return o_spec, zg, za, zb, row, gn


def _merge_fwd(o, z, ob, gn, name, ts=256):
    S = ob.shape[0]

    def body(o_ref, zg_ref, za_ref, zb_ref, ob_ref, gn_ref, m_ref):
        for h in range(NH):
            sl = slice(h * HD, (h + 1) * HD)
            oh = o_ref[h]
            r = lax.rsqrt(jnp.mean(oh * oh, axis=1, keepdims=True) + EPS)
            gate = zg_ref[:, sl]
            oa = (oh * r * gn_ref[...]) * (gate * _sigmoid(gate))
            m = _sigmoid(za_ref[:, sl]) * oa + _sigmoid(zb_ref[:, sl]) * ob_ref[:, sl]
            m_ref[:, sl] = m.astype(m_ref.dtype)

    o_spec, zg, za, zb, row, gns = _merge_specs(ts)
    return _call(body, name, (S // ts,), [o_spec, zg, za, zb, row, gns], row,
                 jax.ShapeDtypeStruct((S, D), MXU))(o, z, z, z, ob, gn)


def _merge_bwd(dm, o, z, ob, gn, name, ts=256):
    S = ob.shape[0]

    def body(dm_ref, o_ref, zg_ref, za_ref, zb_ref, ob_ref, gn_ref,
             do_ref, dzg_ref, dob_ref, dza_ref, dzb_ref, dgn_ref):
        i = pl.program_id(0)
        gn_v = gn_ref[...]
        dgn = jnp.zeros((1, HD), F32)
        for h in range(NH):
            sl = slice(h * HD, (h + 1) * HD)
            dmh = dm_ref[:, sl]
            oh = o_ref[h]
            r = lax.rsqrt(jnp.mean(oh * oh, axis=1, keepdims=True) + EPS)
            ohr = oh * r
            on = ohr * gn_v
            gate = zg_ref[:, sl]
            sg = _sigmoid(gate)
            silu = gate * sg
            oa = on * silu
            ga = _sigmoid(za_ref[:, sl])
            gb = _sigmoid(zb_ref[:, sl])
            obh = ob_ref[:, sl]
            doa = dmh * ga
            dob_ref[:, sl] = dmh * gb
            dza_ref[:, sl] = (dmh * oa * ga * (1.0 - ga)).astype(dza_ref.dtype)
            dzb_ref[:, sl] = (dmh * obh * gb * (1.0 - gb)).astype(dzb_ref.dtype)
            don = doa * silu
            dzg_ref[:, sl] = (doa * on * (sg * (1.0 + gate * (1.0 - sg)))).astype(dzg_ref.dtype)
            dgn = dgn + _colsum(don * ohr)
            u = don * gn_v
            do_ref[h] = r * (u - ohr * jnp.mean(ohr * u, axis=1, keepdims=True))

        @pl.when(i == 0)
        def _():
            dgn_ref[...] = jnp.zeros_like(dgn_ref)

        dgn_ref[...] += dgn

    o_spec, zg, za, zb, row, gns = _merge_specs(ts)
    return _call(
        body, name, (S // ts,), [row, o_spec, zg, za, zb, row, gns],
        [o_spec, row, row, row, row, gns],
        [jax.ShapeDtypeStruct((NH, S, HD), F32), jax.ShapeDtypeStruct((S, D), MXU),
         jax.ShapeDtypeStruct((S, D), F32), jax.ShapeDtypeStruct((S, D), MXU),
         jax.ShapeDtypeStruct((S, D), MXU), jax.ShapeDtypeStruct((1, HD), F32)],
    )(dm, o, z, z, z, ob, gn)


GROWS = 256
GCH = GROWS // CH


def _gdn_prep_fwd(z, conv_w, alog_row, dtb_row, name, comm=None):
    S = z.shape[0]
    ts = GROWS
    scale = HD ** -0.5

    def body(z_ref, halo_ref, zab_ref, w_ref, al_ref, dt_ref, q_ref, k_ref, v_ref, gb_ref, buf):
        i = pl.program_id(0)
        buf[0:8, :] = jnp.where(i == 0, 0.0, halo_ref[...])
        buf[8:8 + ts, :] = z_ref[...]
        outs = (q_ref, k_ref, v_ref)
        for seg in range(3):
            cs = slice(seg * D, (seg + 1) * D)
            c = jnp.zeros((ts, D), F32)
            for j in range(4):
                c = c + w_ref[j:j + 1, cs] * buf[pl.ds(5 + j, ts), cs]
            s = c * _sigmoid(c)
            if seg == 2:
                outs[seg][...] = s
            else:
                mul = scale if seg == 0 else 1.0
                for h in range(NH):
                    sl = slice(h * HD, (h + 1) * HD)
                    sh = s[:, sl]
                    r = lax.rsqrt(_rowsum(sh * sh) + EPS)
                    outs[seg][:, sl] = sh * (r * mul)
        zab = zab_ref[...]
        lane = lax.broadcasted_iota(jnp.int32, zab.shape, 1)
        g = -jnp.exp(al_ref[...]) * _softplus(zab + dt_ref[...])
        ri = lax.broadcasted_iota(jnp.int32, (CH, CH), 0)
        ci = lax.broadcasted_iota(jnp.int32, (CH, CH), 1)
        incl = (ri >= ci).astype(F32)
        gcum = jnp.concatenate([_dot(incl, g[c * CH:(c + 1) * CH], HI) for c in range(ts // CH)], axis=0)
        gb_ref[...] = jnp.where(lane < NH, gcum, jnp.where(lane < 2 * NH, _sigmoid(zab), 0.0))

    row = pl.BlockSpec((ts, D), lambda i: (i, 0))
    vec = pl.BlockSpec((1, 128), lambda i: (0, 0))
    return _call(
        body, name, (S // ts,),
        [pl.BlockSpec((ts, 3 * D), lambda i: (i, 0)),
         pl.BlockSpec((8, 3 * D), lambda i: (jnp.maximum(i * (ts // 8) - 1, 0), 0)),
         pl.BlockSpec((ts, 128), lambda i: (i, Z_AB // 128)),
         _full((4, 3 * D)), vec, vec],
        [row, row, row, pl.BlockSpec((ts, 128), lambda i: (i, 0))],
        [jax.ShapeDtypeStruct((S, D), F32)] * 3 + [jax.ShapeDtypeStruct((S, 128), F32)],
        [pltpu.VMEM((ts + 8, 3 * D), F32)], comm=comm,
    )(z, z, z, conv_w, alog_row, dtb_row)


def _split(a):
    hi = a.astype(MXU)
    return hi, (a - hi.astype(F32)).astype(MXU)


def _dot3(a, b, dot=_dot):
    ah, al = _split(a)
    bh, bl = _split(b)
    return dot(ah, bh) + (dot(ah, bl) + dot(al, bh))


IROWS = 512
ICH = IROWS // CH


def _chunk_common(gbk, h, k):
    lane = lax.broadcasted_iota(jnp.int32, gbk.shape, 1)
    G = _rowsum(jnp.where(lane == h, gbk, 0.0))
    b_col = _rowsum(jnp.where(lane == h + NH, gbk, 0.0))
    ri = lax.broadcasted_iota(jnp.int32, (CH, CH), 0)
    ci = lax.broadcasted_iota(jnp.int32, (CH, CH), 1)
    incl = ri >= ci
    gc = jnp.broadcast_to(G, (CH, CH))
    decay = jnp.where(incl, jnp.exp(jnp.where(incl, gc - gc.T, 0.0)), 0.0)
    Gl = G[CH - 1:CH, :]
    kb = k * b_col
    return dict(b=b_col, ri=ri, ci=ci, incl=incl, strict=ri > ci, decay=decay, eG=jnp.exp(G),
                e2=jnp.exp(Gl - G), cd=jnp.exp(Gl), kb=kb, kk=_dot_nt(_mx(kb), _mx(k)))


def _gdn_intra_fwd(qn, kn, v, gb, name, comm=None):
    S = qn.shape[0]

    def body(q_ref, k_ref, v_ref, gb_ref, u_ref, w_ref, qd_ref, kd_ref, at_ref, t_ref, cd_ref):
        h = pl.program_id(1)
        rows = [slice(c * CH, (c + 1) * CH) for c in range(ICH)]
        ks = [k_ref[r, :] for r in rows]
        cms = [_chunk_common(gb_ref[r, :], h, k) for r, k in zip(rows, ks)]
        ps = [jnp.where(cm["strict"], cm["kk"] * cm["decay"], 0.0) for cm in cms]
        ts = [(cm["ri"] == cm["ci"]).astype(F32) - p for cm, p in zip(cms, ps)]
        for _ in range(5):
            ps = [_dot3(p, p) for p in ps]
            ts = [t + _dot3(t, p) for t, p in zip(ts, ps)]
        for c, (r, k, cm, t) in enumerate(zip(rows, ks, cms, ts)):
            rhs = jnp.concatenate([v_ref[r, :] * cm["b"], k * (cm["b"] * cm["eG"])], axis=1)
            sol = _dot3(t, rhs)
            u_ref[0, r, :] = sol[:, :HD]
            w_ref[0, r, :] = sol[:, HD:]
            t_ref[0, r, :] = t
        for c, (r, k, cm) in enumerate(zip(rows, ks, cms)):
            q = q_ref[r, :]
            qk = _dot_nt(_mx(q), _mx(k))
            at_ref[0, r, :] = jnp.where(cm["incl"], qk * cm["decay"], 0.0)
            qd_ref[0, r, :] = q * cm["eG"]
            kd_ref[0, r, :] = k * cm["e2"]
            cd_ref[0, c] = jnp.broadcast_to(cm["cd"], (8, 128))

    tok = pl.BlockSpec((IROWS, HD), lambda i, h: (i, h))
    hm = pl.BlockSpec((1, IROWS, HD), lambda i, h: (h, i, 0))
    hm64 = pl.BlockSpec((1, IROWS, CH), lambda i, h: (h, i, 0))
    big = jax.ShapeDtypeStruct((NH, S, HD), F32)
    sm = jax.ShapeDtypeStruct((NH, S, CH), F32)
    return _call(
        body, name, (S // IROWS, NH),
        [tok, tok, tok, pl.BlockSpec((IROWS, 128), lambda i, h: (i, 0))],
        [hm, hm, hm, hm, hm64, hm64, pl.BlockSpec((1, ICH, 8, 128), lambda i, h: (h, i, 0, 0))],
        [big, big, big, big, sm, sm, jax.ShapeDtypeStruct((NH, S // CH, 8, 128), F32)], comm=comm,
    )(qn, kn, v, gb)


def _scale_state(s, cd_tile):
    return (s.reshape(HD // 8, 8, HD) * cd_tile[None]).reshape(HD, HD)


def _gdn_scan_fwd(u, w, qd, kd, attn, cdt, name):
    S = u.shape[1]
    nblk = S // GROWS

    def body(u_ref, w_ref, qd_ref, kd_ref, at_ref, cd_ref, o_ref, vn_ref, st_ref, s_ref):
        i = pl.program_id(0)

        @pl.when(i == 0)
        def _():
            s_ref[...] = jnp.zeros_like(s_ref)

        def chunk(c, carry):
            r0 = pl.multiple_of(c * CH, CH)
            rows = pl.ds(r0, CH)
            H = range(NH)
            shs = [s_ref[h] for h in H]
            sbs = [_mx(sh) for sh in shs]
            ws = [_dot(_mx(w_ref[h, rows, :]), sbs[h]) for h in H]
            qs = [_dot(_mx(qd_ref[h, rows, :]), sbs[h]) for h in H]
            vns = [u_ref[h, rows, :] - ws[h] for h in H]
            vbs = [_mx(vn) for vn in vns]
            avs = [_dot(_mx(at_ref[h, rows, :]), vbs[h]) for h in H]
            kvs = [_dot_tn(_mx(kd_ref[h, rows, :]), vbs[h]) for h in H]
            for h in H:
                st_ref[h, c] = shs[h]
                vn_ref[h, rows, :] = vns[h]
                o_ref[h, rows, :] = qs[h] + avs[h]
                s_ref[h] = _scale_state(shs[h], cd_ref[h, c]) + kvs[h]
            return carry

        lax.fori_loop(0, GCH, chunk, 0)

    hm = pl.BlockSpec((NH, GROWS, HD), lambda i: (0, i, 0))
    hm64 = pl.BlockSpec((NH, GROWS, CH), lambda i: (0, i, 0))
    big = jax.ShapeDtypeStruct((NH, S, HD), F32)
    return _call(
        body, name, (nblk,),
        [hm, hm, hm, hm, hm64, pl.BlockSpec((NH, GCH, 8, 128), lambda i: (0, i, 0, 0))],
        [hm, hm, pl.BlockSpec((NH, GCH, HD, HD), lambda i: (0, i, 0, 0))],
        [big, big, jax.ShapeDtypeStruct((NH, S // CH, HD, HD), F32)],
        [pltpu.VMEM((NH, HD, HD), F32)],
    )(u, w, qd, kd, attn, cdt)


def _gdn_scan_bwd(do, w, qd, kd, attn, cdt, vn, st, name):
    S = do.shape[1]
    nblk = S // GROWS

    def body(do_ref, w_ref, qd_ref, kd_ref, at_ref, cd_ref, vn_ref, st_ref,
             dqd_ref, dkd_ref, dvn_ref, dw_ref, dat_ref, dcd_ref, ds_ref):
        i = pl.program_id(0)

        @pl.when(i == 0)
        def _():
            ds_ref[...] = jnp.zeros_like(ds_ref)

        def chunk(cc, carry):
            c = GCH - 1 - cc
            r0 = pl.multiple_of(c * CH, CH)
            rows = pl.ds(r0, CH)
            H = range(NH)
            dsps = [ds_ref[h] for h in H]
            shs = [st_ref[h, c] for h in H]
            dsbs = [_mx(a) for a in dsps]
            sbs = [_mx(a) for a in shs]
            dobs = [_mx(do_ref[h, rows, :]) for h in H]
            vbs = [_mx(vn_ref[h, rows, :]) for h in H]
            dvns = [_dot(_mx(kd_ref[h, rows, :]), dsbs[h]) + _dot_tn(_mx(at_ref[h, rows, :]), dobs[h]) for h in H]
            dvbs = [_mx(a) for a in dvns]
            dqds = [_dot_nt(dobs[h], sbs[h]) for h in H]
            dats = [_dot_nt(dobs[h], vbs[h]) for h in H]
            dkds = [_dot_nt(vbs[h], dsbs[h]) for h in H]
            dws = [_dot_nt(dvbs[h], sbs[h]) for h in H]
            qdos = [_dot_tn(_mx(qd_ref[h, rows, :]), dobs[h]) for h in H]
            wdvs = [_dot_tn(_mx(w_ref[h, rows, :]), dvbs[h]) for h in H]
            for h in H:
                dvn_ref[h, rows, :] = dvns[h]
                dqd_ref[h, rows, :] = dqds[h]
                dat_ref[h, rows, :] = dats[h]
                dkd_ref[h, rows, :] = dkds[h]
                dw_ref[h, rows, :] = -dws[h]
                dcd = jnp.sum(_rowsum(dsps[h] * shs[h]), axis=0, keepdims=True)
                dcd_ref[h, c] = jnp.broadcast_to(dcd, (8, 128))
                ds_ref[h] = _scale_state(dsps[h], cd_ref[h, c]) + qdos[h] - wdvs[h]
            return carry

        lax.fori_loop(0, GCH, chunk, 0)

    hm = pl.BlockSpec((NH, GROWS, HD), lambda i: (0, nblk - 1 - i, 0))
    hm64 = pl.BlockSpec((NH, GROWS, CH), lambda i: (0, nblk - 1 - i, 0))
    tile = pl.BlockSpec((NH, GCH, 8, 128), lambda i: (0, nblk - 1 - i, 0, 0))
    big = jax.ShapeDtypeStruct((NH, S, HD), F32)
    return _call(
        body, name, (nblk,),
        [hm, hm, hm, hm, hm64, tile, hm, pl.BlockSpec((NH, GCH, HD, HD), lambda i: (0, nblk - 1 - i, 0, 0))],
        [hm, hm, hm, hm, hm64, tile],
        [big, big, big, big, jax.ShapeDtypeStruct((NH, S, CH), F32),
         jax.ShapeDtypeStruct((NH, S // CH, 8, 128), F32)],
        [pltpu.VMEM((NH, HD, HD), F32)],
    )(do, w, qd, kd, attn, cdt, vn, st)


def _gdn_intra_bwd(qn, kn, v, gb, u, w, tmat, dqd, dkd, du, dw, dattn, dcdt, name):
    S = qn.shape[0]

    def body(q_ref, k_ref, v_ref, gb_ref, u_ref, w_ref, t_ref, dqd_ref, dkd_ref, du_ref, dw_ref,
             dat_ref, dcd_ref, dq_ref, dk_ref, dv_ref, dgb_ref):
        h = pl.program_id(1)
        rows = [slice(c * CH, (c + 1) * CH) for c in range(ICH)]
        ks = [k_ref[r, :] for r in rows]
        cms = [_chunk_common(gb_ref[r, :], h, k) for r, k in zip(rows, ks)]
        sols = [jnp.concatenate([u_ref[0, r, :], w_ref[0, r, :]], axis=1) for r in rows]
        drhss = [_dot3(t_ref[0, r, :], jnp.concatenate([du_ref[0, r, :], dw_ref[0, r, :]], axis=1), _dot_tn)
                 for r in rows]
        das = [-_dot3(drhs, sol, _dot_nt) for drhs, sol in zip(drhss, sols)]
        for c, (r, k, cm, drhs, da) in enumerate(zip(rows, ks, cms, drhss, das)):
            q, vv = q_ref[r, :], v_ref[r, :]
            decay, eG, e2, b = cm["decay"], cm["eG"], cm["e2"], cm["b"]
            dru, drw = drhs[:, :HD], drhs[:, HD:]
            dv_ref[r, :] = dru * b
            s_w = _rowsum(drw * k)
            dbeta = _rowsum(dru * vv) + s_w * eG
            deg = s_w * b
            dk = drw * (b * eG)
            dkk = jnp.where(cm["strict"], da * decay, 0.0)
            ddec = jnp.where(cm["strict"], da * cm["kk"], 0.0)
            dkkb = _mx(dkk)
            dkb = _dot(dkkb, _mx(k))
            dk = dk + _dot_tn(dkkb, _mx(cm["kb"])) + dkb * b
            dbeta = dbeta + _rowsum(dkb * k)
            dat = jnp.where(cm["incl"], dat_ref[0, r, :], 0.0)
            qk = _dot_nt(_mx(q), _mx(k))
            dqk = _mx(dat * decay)
            ddec = ddec + dat * qk
            dqd = dqd_ref[0, r, :]
            dkd = dkd_ref[0, r, :]
            dq_ref[r, :] = _dot(dqk, _mx(k)) + dqd * eG
            dk_ref[r, :] = dk + _dot_tn(dqk, _mx(q)) + dkd * e2
            deg = deg + _rowsum(dqd * q)
            t2 = _rowsum(dkd * k) * e2
            dgl = jnp.sum(t2, axis=0, keepdims=True) + dcd_ref[0, c][0:1, 0:1] * cm["cd"]
            dd = ddec * decay
            dG = deg * eG - t2 + _rowsum(dd) - _rowsum(dd.T)
            rowi = lax.broadcasted_iota(jnp.int32, (CH, 1), 0)
            dG = dG + jnp.where(rowi == CH - 1, dgl, 0.0)
            lane = lax.broadcasted_iota(jnp.int32, (CH, 128), 1)
            dgb_ref[0, r, :] = jnp.where(lane == h, dG, 0.0) + jnp.where(lane == h + NH, dbeta, 0.0)

    tok = pl.BlockSpec((IROWS, HD), lambda i, h: (i, h))
    hm = pl.BlockSpec((1, IROWS, HD), lambda i, h: (h, i, 0))
    hm64 = pl.BlockSpec((1, IROWS, CH), lambda i, h: (h, i, 0))
    tile = pl.BlockSpec((1, ICH, 8, 128), lambda i, h: (h, i, 0, 0))
    tokout = jax.ShapeDtypeStruct((S, D), F32)
    return _call(
        body, name, (S // IROWS, NH),
        [tok, tok, tok, pl.BlockSpec((IROWS, 128), lambda i, h: (i, 0)), hm, hm, hm64,
         hm, hm, hm, hm, hm64, tile],
        [tok, tok, tok, pl.BlockSpec((1, IROWS, 128), lambda i, h: (h, i, 0))],
        [tokout, tokout, tokout, jax.ShapeDtypeStruct((NH, S, 128), F32)],
    )(qn, kn, v, gb, u, w, tmat, dqd, dkd, du, dw, dattn, dcdt)


def _gdn_prep_bwd(z, dqn, dkn, dv, dgb, conv_w, alog_row, dtb_row, name):
    S = z.shape[0]
    ts = GROWS
    nblk = S // ts
    scale = HD ** -0.5
    tb = ts // 8

    def body(z_ref, hp_ref, hn_ref, zab_ref, dq_ref, dqn_ref, dk_ref, dkn_ref, dv_ref, dvn_ref,
             dgb_ref, w_ref, al_ref, dt_ref, dz_ref, dzab_ref, dcw_ref, dvec_ref, buf, dybuf, dcbuf):
        i = pl.program_id(0)
        last = i == nblk - 1

        @pl.when(i == 0)
        def _():
            dcw_ref[...] = jnp.zeros_like(dcw_ref)
            dvec_ref[...] = jnp.zeros_like(dvec_ref)

        buf[0:8, :] = jnp.where(i == 0, 0.0, hp_ref[...])
        buf[8:8 + ts, :] = z_ref[...]
        buf[8 + ts:16 + ts, :] = hn_ref[...]
        rowi = lax.broadcasted_iota(jnp.int32, (ts + 8, 1), 0)
        live = jnp.logical_or(rowi < ts, jnp.logical_not(last))
        dys = ((dq_ref, dqn_ref), (dk_ref, dkn_ref), (dv_ref, dvn_ref))
        for seg in range(3):
            cs = slice(seg * D, (seg + 1) * D)
            dybuf[0:ts, :] = dys[seg][0][...]
            dybuf[ts:ts + 8, :] = dys[seg][1][...]
            taps = [buf[pl.ds(5 + j, ts + 8), cs] for j in range(4)]
            c = jnp.zeros((ts + 8, D), F32)
            for j in range(4):
                c = c + w_ref[j:j + 1, cs] * taps[j]
            sg = _sigmoid(c)
            s = c * sg
            dsilu = sg * (1.0 + c * (1.0 - sg))
            if seg == 2:
                dcbuf[...] = jnp.where(live, dybuf[...] * dsilu, 0.0)
            else:
                mul = scale if seg == 0 else 1.0
                for h in range(NH):
                    sl = slice(h * HD, (h + 1) * HD)
                    sh = s[:, sl]
                    dy = dybuf[:, sl]
                    r = lax.rsqrt(_rowsum(sh * sh) + EPS)
                    shr = sh * r
                    ds = (mul * r) * (dy - shr * _rowsum(shr * dy))
                    dcbuf[:, sl] = jnp.where(live, ds * dsilu[:, sl], 0.0)
            dx = jnp.zeros((ts, D), F32)
            for j in range(4):
                dcw_ref[j:j + 1, cs] += _colsum(dcbuf[0:ts, :] * taps[j][0:ts])
                dx = dx + w_ref[j:j + 1, cs] * dcbuf[pl.ds(3 - j, ts), :]
            dz_ref[:, cs] = dx.astype(dz_ref.dtype)
        dgbs = dgb_ref[0]
        for h in range(1, NH):
            dgbs = dgbs + dgb_ref[h]
        ri = lax.broadcasted_iota(jnp.int32, (CH, CH), 0)
        ci = lax.broadcasted_iota(jnp.int32, (CH, CH), 1)
        rev = (ci >= ri).astype(F32)
        dgrev = jnp.concatenate([_dot(rev, dgbs[c * CH:(c + 1) * CH], HI) for c in range(ts // CH)], axis=0)
        lane0 = lax.broadcasted_iota(jnp.int32, dgbs.shape, 1)
        dgbs = jnp.where(lane0 < NH, dgrev, dgbs)
        zab = zab_ref[...]
        lane = lax.broadcasted_iota(jnp.int32, zab.shape, 1)
        xx = zab + dt_ref[...]
        ea = jnp.exp(al_ref[...])
        g = -ea * _softplus(xx)
        da = dgbs * (-ea) * _sigmoid(xx)
        beta = _sigmoid(zab)
        db = dgbs * beta * (1.0 - beta)
        is_a = lane < NH
        dzab = jnp.where(is_a, da, jnp.where(lane < 2 * NH, db, 0.0))
        dzab_ref[:, 0:128] = dzab.astype(dzab_ref.dtype)
        dzab_ref[:, 128:512] = jnp.zeros((ts, 384), dzab_ref.dtype)
        dvec_ref[0:1, :] += _colsum(jnp.where(is_a, dgbs * g, 0.0))
        dvec_ref[1:2, :] += _colsum(jnp.where(is_a, da, 0.0))

    z3 = pl.BlockSpec((ts, 3 * D), lambda i: (i, 0))
    row = pl.BlockSpec((ts, D), lambda i: (i, 0))
    nxt = pl.BlockSpec((8, D), lambda i: (jnp.minimum((i + 1) * tb, S // 8 - 1), 0))
    vec = pl.BlockSpec((1, 128), lambda i: (0, 0))
    return _call(
        body, name, (nblk,),
        [z3,
         pl.BlockSpec((8, 3 * D), lambda i: (jnp.maximum(i * tb - 1, 0), 0)),
         pl.BlockSpec((8, 3 * D), lambda i: (jnp.minimum((i + 1) * tb, S // 8 - 1), 0)),
         pl.BlockSpec((ts, 128), lambda i: (i, Z_AB // 128)),
         row, nxt, row, nxt, row, nxt,
         pl.BlockSpec((NH, ts, 128), lambda i: (0, i, 0)),
         _full((4, 3 * D)), vec, vec],
        [z3, pl.BlockSpec((ts, 512), lambda i: (i, 0)), _full((8, 3 * D)), _full((8, 128))],
        [jax.ShapeDtypeStruct((S, 3 * D), MXU), jax.ShapeDtypeStruct((S, 512), MXU),
         jax.ShapeDtypeStruct((8, 3 * D), F32), jax.ShapeDtypeStruct((8, 128), F32)],
        [pltpu.VMEM((ts + 16, 3 * D), F32), pltpu.VMEM((ts + 8, D), F32), pltpu.VMEM((ts + 8, D), F32)],
    )(z, z, z, z, dqn, dqn, dkn, dkn, dv, dv, dgb, conv_w, alog_row, dtb_row)


def _bias_index():
    u = np.arange(FW)[None, :]
    s = np.arange(3)[:, None]
    return np.clip(KWIN - 1 - u - QB * s, -256, 256) + 256


def _bias_vec(rel_bias_pad, onehot, name):
    def body(rb_ref, e_ref, o_ref):
        o_ref[:, 0, :] = _dot_nt(rb_ref[...], e_ref[0], HI)

    return _call(body, name, (3,),
                 [_full((NH, 640)), pl.BlockSpec((1, FW, 640), lambda s: (s, 0, 0))],
                 pl.BlockSpec((NH, 1, FW), lambda s: (s, 0, 0)),
                 jax.ShapeDtypeStruct((3 * NH, 1, FW), F32))(rel_bias_pad, onehot)


def _att_window(i):
    return pl.multiple_of(jnp.maximum(i * QB - PAST * CH, 0), QB)


def _bias_mask(fvec, name):
    def body(f_ref, o_ref):
        i = 2 - pl.program_id(0) // NH
        ws = jnp.maximum(i * QB - PAST * CH, 0)
        fb = jnp.broadcast_to(f_ref[0], (QB, FW))
        bias = pltpu.roll(fb, FW - 255, 1, stride=1, stride_axis=0)[:, :KWIN]
        qc = (i * QB + lax.broadcasted_iota(jnp.int32, (QB, KWIN), 0)) // CH
        kc = (ws + lax.broadcasted_iota(jnp.int32, (QB, KWIN), 1)) // CH
        o_ref[0] = jnp.where((kc <= qc) & (kc >= qc - PAST), bias, MASKED)

    return _call(body, name, (3 * NH,), [pl.BlockSpec((1, 1, FW), lambda j: (j, 0, 0))],
                 pl.BlockSpec((1, QB, KWIN), lambda j: (j, 0, 0)),
                 jax.ShapeDtypeStruct((3 * NH, QB, KWIN), F32))(fvec)


def _att_scores(q_ref, k_ref, bm_ref, i):
    ws = _att_window(i)
    q = _mx(q_ref[...] * (HD ** -0.5))
    kw = _mx(k_ref[pl.ds(ws, KWIN), :])
    return q, kw, ws, _dot_nt(q, kw) + bm_ref[0]


def _att_specs(S):
    c0 = Z_ATT // HD
    q = pl.BlockSpec((QB, HD), lambda h, i: (i, c0 + h))
    k = pl.BlockSpec((S, HD), lambda h, i: (0, c0 + NH + h))
    v = pl.BlockSpec((S, HD), lambda h, i: (0, c0 + 2 * NH + h))
    bm = pl.BlockSpec((1, QB, KWIN), lambda h, i: (jnp.maximum(2 - i, 0) * NH + h, 0, 0))
    tok = pl.BlockSpec((QB, HD), lambda h, i: (i, h))
    return q, k, v, bm, tok


def _att_fwd(z, bmask, name, comm=None):
    S = z.shape[0]

    def body(q_ref, k_ref, v_ref, bm_ref, o_ref, lse_ref):
        _, _, ws, s = _att_scores(q_ref, k_ref, bm_ref, pl.program_id(1))
        m = jnp.max(s, axis=1, keepdims=True)
        p = jnp.exp(s - m)
        l = _rowsum(p)
        o_ref[...] = _dot(_mx(p), _mx(v_ref[pl.ds(ws, KWIN), :])) * (1.0 / l)
        lse_ref[...] = jnp.broadcast_to(m + jnp.log(l), (QB, HD))

    q, k, v, bm, tok = _att_specs(S)
    shp = jax.ShapeDtypeStruct((S, D), F32)
    return _call(body, name, (NH, S // QB), [q, k, v, bm], [tok, tok], [shp, shp], comm=comm)(z, z, z, bmask)


def _att_bwd(z, bmask, ob, lse, dob, name):
    S = z.shape[0]
    nq = S // QB

    def body(q_ref, k_ref, v_ref, bm_ref, o_ref, lse_ref, do_ref, dq_ref, dk_ref, dv_ref, db_ref, dk_acc, dv_acc):
        i = pl.program_id(1)
        q, kw, ws, s = _att_scores(q_ref, k_ref, bm_ref, i)
        p = jnp.exp(s - lse_ref[:, 0:1])
        do = do_ref[...]
        dob16 = _mx(do)
        dp = _dot_nt(dob16, _mx(v_ref[pl.ds(ws, KWIN), :]))
        ds = p * (dp - _rowsum(do * o_ref[...]))
        dsb = _mx(ds)
        dq_ref[...] = (_dot(dsb, kw) * (HD ** -0.5)).astype(dq_ref.dtype)

        @pl.when(i == 0)
        def _():
            dk_acc[...] = jnp.zeros_like(dk_acc)
            dv_acc[...] = jnp.zeros_like(dv_acc)

        dk_acc[pl.ds(ws, KWIN), :] += _dot_tn(dsb, q)
        dv_acc[pl.ds(ws, KWIN), :] += _dot_tn(_mx(p), dob16)

        @pl.when(i == nq - 1)
        def _():
            dk_ref[...] = dk_acc[...].astype(dk_ref.dtype)
            dv_ref[...] = dv_acc[...].astype(dv_ref.dtype)

        @pl.when(i <= 2)
        def _():
            db_ref[0] = ds

        @pl.when(i > 2)
        def _():
            db_ref[0] += ds

    q, k, v, bm, tok = _att_specs(S)
    acc = pl.BlockSpec((S, HD), lambda h, i: (0, h))
    half = jax.ShapeDtypeStruct((S, D), MXU)
    return _call(
        body, name, (NH, nq), [q, k, v, bm, tok, tok, tok], [tok, acc, acc, bm],
        [half, half, half, jax.ShapeDtypeStruct((3 * NH, QB, KWIN), F32)],
        [pltpu.VMEM((S, HD), F32), pltpu.VMEM((S, HD), F32)],
    )(z, z, z, bmask, ob, lse, dob)


def _bias_fold(dbias, onehot, name):
    def body(db_ref, e_ref, o_ref):
        j = pl.program_id(0)
        h = j % NH
        x = jnp.concatenate([db_ref[0], jnp.zeros((QB, FW - KWIN), F32)], axis=1)
        half = QB // 2
        while half >= 8:
            x = x[:half] + pltpu.roll(x[half:2 * half], FW - half, 1)
            half //= 2
        df = jnp.zeros((1, FW), F32)
        for r in range(8):
            df = df + pltpu.roll(x[r:r + 1], 255 - r, 1)
        contrib = _dot(df, e_ref[0], HI)
        rowh = lax.broadcasted_iota(jnp.int32, (NH, 640), 0)

        @pl.when(j == 0)
        def _():
            o_ref[...] = jnp.zeros_like(o_ref)

        o_ref[...] += jnp.where(rowh == h, contrib, 0.0)

    return _call(
        body, name, (3 * NH,),
        [pl.BlockSpec((1, QB, KWIN), lambda j: (j, 0, 0)),
         pl.BlockSpec((1, FW, 640), lambda j: (j // NH, 0, 0))],
        _full((NH, 640)), jax.ShapeDtypeStruct((NH, 640), F32),
    )(dbias, onehot)


ADA_SHARD = 6 * D // NDEV


def _ada_mod(c_all, w_ada, b_shard, name):
    def body(c_ref, w_ref, b_ref, o_ref):
        cv = c_ref[...]
        ca = cv * _sigmoid(cv)
        o_ref[0] = _dot(_mx(ca), _mx(w_ref[0])) + b_ref[0]

    return _call(
        body, name, (DEPTH,),
        [_full((NDEV, D)), pl.BlockSpec((1, D, ADA_SHARD), lambda l: (l, 0, 0)),
         pl.BlockSpec((1, 1, ADA_SHARD), lambda l: (l, 0, 0))],
        pl.BlockSpec((1, NDEV, ADA_SHARD), lambda l: (l, 0, 0)),
        jax.ShapeDtypeStruct((DEPTH, NDEV, ADA_SHARD), F32),
    )(c_all, w_ada, b_shard.reshape(DEPTH, 1, ADA_SHARD))


def _adam(g, w, m, v):
    m = ADAM_B1 * m + (1.0 - ADAM_B1) * g
    v = ADAM_B2 * v + (1.0 - ADAM_B2) * jnp.square(g)
    m_hat = m / (1.0 - ADAM_B1 ** ADAM_STEP)
    v_hat = v / (1.0 - ADAM_B2 ** ADAM_STEP)
    delta = -ADAM_LR * (m_hat / (jnp.sqrt(v_hat) + ADAM_EPS) + ADAM_WD * w)
    return delta, m, v


def _wada_adamw(c_all_t, dmod, w, m, v, name):
    def body(c_ref, d_ref, w_ref, m_ref, v_ref, g_ref, dl_ref, mo_ref, vo_ref):
        cv = c_ref[...]
        ca = cv * _sigmoid(cv)
        g = _dot(ca, d_ref[0], HI)
        g_ref[0] = g
        dl_ref[0], mo_ref[0], vo_ref[0] = _adam(g, w_ref[0], m_ref[0], v_ref[0])

    blk = pl.BlockSpec((1, D, ADA_SHARD), lambda l: (l, 0, 0))
    shp = jax.ShapeDtypeStruct((DEPTH, D, ADA_SHARD), F32)
    return _call(
        body, name, (DEPTH,),
        [_full((D, NDEV)), pl.BlockSpec((1, NDEV, ADA_SHARD), lambda l: (l, 0, 0)), blk, blk, blk],
        [blk] * 4, [shp] * 4,
    )(c_all_t, dmod, w, m, v)


def _adamw_reduce(parts, w, m, v, name, tr):
    P, R, C = parts.shape

    def body(p_ref, w_ref, m_ref, v_ref, g_ref, dl_ref, mo_ref, vo_ref):
        g = p_ref[0].astype(F32)
        for k in range(1, P):
            g = g + p_ref[k].astype(F32)
        g_ref[...] = g
        dl_ref[...], mo_ref[...], vo_ref[...] = _adam(g, w_ref[...], m_ref[...], v_ref[...])

    blk = pl.BlockSpec((tr, C), lambda i: (i, 0))
    shp = jax.ShapeDtypeStruct((R, C), F32)
    return _call(body, name, (R // tr,), [pl.BlockSpec((P, tr, C), lambda i: (0, i, 0)), blk, blk, blk],
                 [blk] * 4, [shp] * 4)(parts, w, m, v)


def _sum_parts(parts, name):
    P, R, C = parts.shape

    def body(p_ref, o_ref):
        g = p_ref[0]
        for k in range(1, P):
            g = g + p_ref[k]
        o_ref[...] = g

    return _call(body, name, (1,), [_full((P, R, C))], _full((R, C)),
                 jax.ShapeDtypeStruct((R, C), F32))(parts)


def _pack_rows(vecs, width=1024):
    flat = jnp.concatenate([a.reshape(-1) for a in vecs])
    n = flat.shape[0]
    rows = -(-n // width)
    rows = -(-rows // 8) * 8
    return jnp.pad(flat, (0, rows * width - n)).reshape(rows, width)


def _unpack_rows(packed, shapes):
    flat = packed.reshape(-1)
    out, off = [], 0
    for s in shapes:
        n = int(np.prod(s)) if len(s) else 1
        out.append(flat[off:off + n].reshape(s))
        off += n
    return out


BIG = ("w_in", "w_out", "w_ff_in", "w_ff_out")


def _z_weights(g_in):
    w = jnp.transpose(g_in, (1, 0, 2)).reshape(D, IN_W)
    return jnp.concatenate([w[:, :Z_ATT], w[:, Z_ATT + 2 * NH:], w[:, Z_ATT:Z_ATT + 2 * NH],
                            jnp.zeros((D, ZW - IN_W), MXU)], axis=-1)


def _cols_to_owners(a):
    return jnp.transpose(a.reshape(a.shape[0], NDEV, -1), (1, 0, 2))


def _rows_to_owners(a):
    return a.reshape(NDEV, -1, a.shape[1])


def _forward_layer(x, mod_l, p, shard, next_w_in):
    sh1, sc1, gt1, sh2, sc2, gt2 = [mod_l[k][None] for k in range(6)]
    P = functools.partial
    gather = lambda a: None if a is None else ("gather", a)
    h, ht = _modnorm_fwd(x, p["norm_mix"], sc1, sh1, "norm_mix_fwd")
    z, g_in = _hosted(P(_mm_nn, h, p["wz"], "in_proj", tm=2048, tn=512), gather(next_w_in))
    (qn, kn, v, gb), g_w1 = _hosted(P(_gdn_prep_fwd, z, p["conv_w"], p["alog"], p["dtb"], "gdn_prep_fwd"),
                                    gather(shard["w_ff_in"]))
    (u, w, qd, kd, attn, tmat, cdt), g_w2 = _hosted(P(_gdn_intra_fwd, qn, kn, v, gb, "gdn_intra_fwd"),
                                                    gather(shard["w_ff_out"]))
    o, vn, st = _gdn_scan_fwd(u, w, qd, kd, attn, cdt, "gdn_scan_fwd")
    (ob, lse), g_out = _hosted(P(_att_fwd, z, p["bmask"], "att_fwd"), gather(shard["w_out"]))
    wout = g_out.reshape(D, D)
    w1 = jnp.transpose(g_w1, (1, 0, 2)).reshape(D, DFF)
    w2 = g_w2.reshape(DFF, D)
    m = _merge_fwd(o, z, ob, p["gdn_norm"], "merge_fwd")
    x1, h2 = _mm_nn(m, wout, "out_proj", mode="resid_norm", res=x, gate=gt1, norm=(p["norm_mlp"], sc2, sh2))
    a, r = _mm_nn(h2, w1, "ff_in", mode="relu2")
    x2 = _mm_nn(r, w2, "ff_out", mode="resid", res=x1, gate=gt2)
    saved = dict(x=x, ht=ht, z=z, qn=qn, kn=kn, v=v, gb=gb, u=u, w=w, qd=qd, kd=kd, attn=attn,
                 tmat=tmat, cdt=cdt, o=o, vn=vn, st=st, ob=ob, lse=lse, m=m, x1=x1, h2=h2, a=a, r=r,
                 wout=wout, w1=w1, w2=w2)
    return x2, saved, g_in


def _backward_layer(dx2, mod_l, p, s, onehot):
    sh1, sc1, gt1, sh2, sc2, gt2 = [mod_l[k][None] for k in range(6)]
    P = functools.partial
    dw2, dgt2 = _mm_tn(s["r"], dx2, "ff_out_dw", gate=gt2, w=s["w2"], out_dtype=MXU)
    da, r_w2 = _hosted(P(_mm_nt, dx2, s["w2"], "ff_out_dx", gate=gt2, drelu=s["a"]), ("a2a", _rows_to_owners(dw2)))
    dw1 = _mm_tn(s["h2"], da, "ff_in_dw", out_dtype=MXU)
    dh2, r_w1 = _hosted(P(_mm_nt, da, s["w1"], "ff_in_dx"), ("a2a", _cols_to_owners(dw1)))
    dx1, dsc2, dsh2, dnmlp = _modnorm_bwd(dh2, s["x1"], p["norm_mlp"], sc2, sh2, dx2, "norm_mlp_bwd")
    dwout, dgt1 = _mm_tn(s["m"], dx1, "out_proj_dw", gate=gt1, w=s["wout"], out_dtype=MXU)
    dm, r_out = _hosted(P(_mm_nt, dx1, s["wout"], "out_proj_dx", gate=gt1), ("a2a", _rows_to_owners(dwout)))
    do, dzg, dob, dza, dzb, dgn = _merge_bwd(dm, s["o"], s["z"], s["ob"], p["gdn_norm"], "merge_bwd")
    dq_att, dk_att, dv_att, dbias = _att_bwd(s["z"], p["bmask"], s["ob"], s["lse"], dob, "att_bwd")
    drb = _bias_fold(dbias, onehot, "rel_bias_fold")[:, :513]
    dqd, dkd, dvn, dw, dattn, dcdt = _gdn_scan_bwd(do, s["w"], s["qd"], s["kd"], s["attn"], s["cdt"],
                                                   s["vn"], s["st"], "gdn_scan_bwd")
    dqn, dkn, dv, dgb = _gdn_intra_bwd(s["qn"], s["kn"], s["v"], s["gb"], s["u"], s["w"], s["tmat"],
                                       dqd, dkd, dvn, dw, dattn, dcdt, "gdn_intra_bwd")
    dzq, dzab, dcw, dvec = _gdn_prep_bwd(s["z"], dqn, dkn, dv, dgb, p["conv_w"], p["alog"], p["dtb"],
                                         "gdn_prep_bwd")
    dz = (dzq, dzg, dq_att, dk_att, dv_att, dza, dzb, dzab)
    dwz = _in_proj_dw(s["ht"], dz, "in_proj_dw")
    dw_in = jnp.concatenate([dwz[:, :Z_ATT], dwz[:, Z_AB:Z_AB + 2 * NH], dwz[:, Z_ATT:Z_AB]], axis=1)
    dh, r_in = _hosted(P(_in_proj_dx, dz, p["wz"], "in_proj_dx"), ("a2a", _cols_to_owners(dw_in)))
    dx, dsc1, dsh1, dnmix = _modnorm_bwd(dh, s["x"], p["norm_mix"], sc1, sh1, dx1, "norm_mix_bwd")
    grads = dict(norm_mix=dnmix[0], norm_mlp=dnmlp[0], conv_w=dcw[:4], a_log=dvec[0, :NH], dt_bias=dvec[1, :NH],
                 gdn_norm=dgn[0], rel_bias=drb, mod=jnp.concatenate([dsh1, dsc1, dgt1, dsh2, dsc2, dgt2], axis=1)[0])
    return dx, grads, dict(w_in=r_in, w_out=r_out, w_ff_in=r_w1, w_ff_out=r_w2)


def _bias_onehot():
    return (jnp.asarray(_bias_index())[:, :, None] == jnp.arange(640)[None, None, :]).astype(F32)


def _layer_params(l, conv_full, norm_mix, norm_mlp, a_log, dt_bias, gdn_norm, rel_bias, onehot):
    pad = lambda a: jnp.pad(a, (0, 128 - NH))[None]
    fvec = _bias_vec(jnp.pad(rel_bias[l], ((0, 0), (0, 640 - rel_bias.shape[2]))), onehot, "rel_bias_vec")
    return dict(conv_w=conv_full[l], norm_mix=norm_mix[l][None], norm_mlp=norm_mlp[l][None], alog=pad(a_log[l]),
                dtb=pad(dt_bias[l]), gdn_norm=gdn_norm[l][None], bmask=_bias_mask(fvec, "rel_bias_mask"))


def _local_step(x, target, mod, small, shards, final_norm, onehot):
    L = len(small)
    g_in = _all_gather(shards[0]["w_in"], "gather_w_in")
    saved, params = [], []
    for l in range(L):
        params.append({**small[l], "wz": _z_weights(g_in)})
        x, sv, g_in = _forward_layer(x, mod[l].reshape(6, D), params[l], shards[l],
                                     shards[l + 1]["w_in"] if l + 1 < L else None)
        saved.append(sv)
    loss, dx, dfn = _loss_head(x, target, final_norm[None], "loss_head")
    grads, recv = [None] * L, [None] * L
    for l in reversed(range(L)):
        dx, grads[l], recv[l] = _backward_layer(dx, mod[l].reshape(6, D), params[l], saved[l], onehot)
    return loss, dx, grads, dfn[0], recv


SMALL = ("b_ada", "norm_mix", "norm_mlp", "a_log", "dt_bias", "gdn_norm", "rel_bias", "final_norm")


def kernel(x, c, w_ada, b_ada, norm_mix, norm_mlp, w_in, conv_w, a_log, dt_bias, gdn_norm, rel_bias, w_out, w_ff_in, w_ff_out, final_norm, loss_target, m_w_ada, m_b_ada, m_norm_mix, m_norm_mlp, m_w_in, m_conv_w, m_a_log, m_dt_bias, m_gdn_norm, m_rel_bias, m_w_out, m_w_ff_in, m_w_ff_out, m_final_norm, v_w_ada, v_b_ada, v_norm_mix, v_norm_mlp, v_w_in, v_conv_w, v_a_log, v_dt_bias, v_gdn_norm, v_rel_bias, v_w_out, v_w_ff_in, v_w_ff_out, v_final_norm):
    W = dict(w_ada=w_ada, b_ada=b_ada, norm_mix=norm_mix, norm_mlp=norm_mlp, w_in=w_in, conv_w=conv_w,
             a_log=a_log, dt_bias=dt_bias, gdn_norm=gdn_norm, rel_bias=rel_bias, w_out=w_out,
             w_ff_in=w_ff_in, w_ff_out=w_ff_out, final_norm=final_norm)
    Mo = dict(w_ada=m_w_ada, b_ada=m_b_ada, norm_mix=m_norm_mix, norm_mlp=m_norm_mlp, w_in=m_w_in,
              conv_w=m_conv_w, a_log=m_a_log, dt_bias=m_dt_bias, gdn_norm=m_gdn_norm, rel_bias=m_rel_bias,
              w_out=m_w_out, w_ff_in=m_w_ff_in, w_ff_out=m_w_ff_out, final_norm=m_final_norm)
    Vo = dict(w_ada=v_w_ada, b_ada=v_b_ada, norm_mix=v_norm_mix, norm_mlp=v_norm_mlp, w_in=v_w_in,
              conv_w=v_conv_w, a_log=v_a_log, dt_bias=v_dt_bias, gdn_norm=v_gdn_norm, rel_bias=v_rel_bias,
              w_out=v_w_out, w_ff_in=v_w_ff_in, w_ff_out=v_w_ff_out, final_norm=v_final_norm)
    L = w_in.shape[0]
    me = _flat(_mesh_pos())
    cshard = conv_w.shape[2]

    small_in = _all_gather(_pack_rows([c, conv_w]), "gather_c_conv")
    c_all = small_in[:, 0, :]
    conv_full = small_in.reshape(NDEV, -1)[:, D:D + L * 4 * cshard].reshape(NDEV, L, 4, cshard)
    conv_full = jnp.transpose(conv_full, (1, 2, 0, 3)).reshape(L, 4, NDEV * cshard)

    b_shard = lax.dynamic_slice_in_dim(b_ada, me * ADA_SHARD, ADA_SHARD, axis=1)
    mod_all = _all_gather(_ada_mod(c_all, w_ada, b_shard, "ada_mod"), "gather_mod")
    mod = lax.dynamic_index_in_dim(mod_all, me, axis=2, keepdims=False)
    mod = jnp.transpose(mod, (1, 0, 2)).reshape(L, 6 * D)

    onehot = _bias_onehot()
    small = [_layer_params(l, conv_full, norm_mix, norm_mlp, a_log, dt_bias, gdn_norm, rel_bias, onehot)
             for l in range(L)]
    shards = [{n: W[n][l].astype(MXU) for n in BIG} for l in range(L)]
    loss, dx, grads, dfn, recv = _local_step(x[0], loss_target[0], mod, small, shards, final_norm, onehot)

    def stack(name):
        return jnp.stack([g[name] for g in grads])

    small_names = ("mod", "norm_mix", "norm_mlp", "a_log", "dt_bias", "gdn_norm", "rel_bias")
    small_parts = [stack(n) for n in small_names] + [dfn, stack("conv_w"), loss[0, 0:1]]
    small_shapes = [a.shape for a in small_parts]
    gathered = _all_gather(_pack_rows(small_parts), "gather_small_grads")
    total = _unpack_rows(_sum_parts(gathered, "sum_small_grads"), small_shapes)
    tot = dict(zip(small_names + ("final_norm", "conv_w", "loss"), total))
    tot["b_ada"] = tot.pop("mod")
    tot["conv_w"] = lax.dynamic_slice_in_dim(tot["conv_w"], me * cshard, cshard, axis=2)

    out_g, out_d, out_m, out_v = {}, {}, {}, {}
    names = SMALL + ("conv_w",)
    shapes = [W[n].shape for n in names]
    packed = [_pack_rows([src[n] for n in names])[None] if src is tot else _pack_rows([src[n] for n in names])
              for src in (tot, W, Mo, Vo)]
    res = _adamw_reduce(*packed, "adamw_small", tr=8)
    for dst, arr in zip((out_g, out_d, out_m, out_v), res):
        dst.update(zip(names, _unpack_rows(arr, shapes)))

    dmod_all = gathered.reshape(NDEV, -1)[:, :L * 6 * D].reshape(NDEV, L, 6 * D)
    dmod_mine = jnp.transpose(lax.dynamic_slice_in_dim(dmod_all, me * ADA_SHARD, ADA_SHARD, axis=2), (1, 0, 2))
    res = _wada_adamw(jnp.transpose(c_all), dmod_mine, w_ada, m_w_ada, v_w_ada, "adamw_w_ada")
    for dst, arr in zip((out_g, out_d, out_m, out_v), res):
        dst["w_ada"] = arr

    for name, tr in (("w_in", 256), ("w_out", 128), ("w_ff_in", 256), ("w_ff_out", 256)):
        sh = W[name].shape
        rows = int(np.prod(sh[:-1]))
        flat = lambda a: a.reshape(rows, sh[-1])
        parts = jnp.stack([recv[l][name] for l in range(L)], axis=1).reshape(NDEV, rows, sh[-1])
        res = _adamw_reduce(parts, flat(W[name]), flat(Mo[name]), flat(Vo[name]), "adamw_" + name, tr=tr)
        for dst, arr in zip((out_g, out_d, out_m, out_v), res):
            dst[name] = arr.reshape(sh)

    order = ("w_ada", "b_ada", "norm_mix", "norm_mlp", "w_in", "conv_w", "a_log", "dt_bias", "gdn_norm",
             "rel_bias", "w_out", "w_ff_in", "w_ff_out", "final_norm")
    return (tot["loss"].reshape(()), dx[None], *[out_g[n] for n in order], *[out_d[n] for n in order],
            *[out_m[n] for n in order], *[out_v[n] for n in order])
```

```python
import functools
import math

import numpy as np
import jax
import jax.numpy as jnp
from jax import lax
from jax.experimental import pallas as pl
from jax.experimental.pallas import tpu as pltpu

F32 = jnp.float32
MXU = jnp.bfloat16
HI = lax.Precision.HIGHEST
MESH_ID = pl.DeviceIdType.MESH

D = 1024
NH = 8
HD = 128
CH = 64
PAST = 8
DFF = 4096
EPS = 1e-6
NDEV = 8
DEPTH = 4
IN_W = 9232
ZW = 9728
Z_GATE, Z_ATT, Z_BR, Z_AB = 3072, 4096, 7168, 9216
QB = 256
MASKED = -1e30
KWIN = 768
FW = 1024
ADAM_LR, ADAM_B1, ADAM_B2, ADAM_EPS, ADAM_WD, ADAM_STEP = 0.001, 0.9, 0.999, 1e-08, 0.01, 10


def _dot(a, b, prec=None):
    return jnp.dot(a, b, preferred_element_type=F32, precision=prec)


def _dot_nt(a, b, prec=None):
    return lax.dot_general(a, b, (((1,), (1,)), ((), ())), preferred_element_type=F32, precision=prec)


def _dot_tn(a, b, prec=None):
    return lax.dot_general(a, b, (((0,), (0,)), ((), ())), preferred_element_type=F32, precision=prec)


def _mx(a):
    return a.astype(MXU)


def _sigmoid(x):
    return 0.5 * jnp.tanh(0.5 * x) + 0.5


def _softplus(x):
    return jnp.maximum(x, 0.0) + jnp.log(1.0 + jnp.exp(-jnp.abs(x)))


def _rowsum(x):
    return jnp.sum(x, axis=1, keepdims=True)


def _colsum(x):
    return jnp.sum(x, axis=0, keepdims=True)


def _call(body, name, grid, in_specs, out_specs, out_shape, scratch=(), comm=None):
    if comm is None:
        return pl.pallas_call(body, name=name, grid=grid, in_specs=in_specs, out_specs=out_specs,
                              out_shape=out_shape, scratch_shapes=list(scratch))
    kind, x = comm
    single = not isinstance(out_specs, (list, tuple))
    o_specs = [out_specs] if single else list(out_specs)
    o_shape = [out_shape] if single else list(out_shape)
    n_in, n_out, n_scr = len(in_specs), len(o_specs), len(scratch)
    c_shape = (NDEV,) + x.shape if kind == "gather" else x.shape

    def wrapped(*refs):
        ins, x_ref = refs[:n_in], refs[n_in]
        outs, c_ref = refs[n_in + 1:n_in + 1 + n_out], refs[n_in + 1 + n_out]
        scr = refs[n_in + 2 + n_out:n_in + 2 + n_out + n_scr]
        sems = refs[n_in + 2 + n_out + n_scr:]
        first = functools.reduce(jnp.logical_and, [pl.program_id(a) == 0 for a in range(len(grid))])
        last = functools.reduce(jnp.logical_and, [pl.program_id(a) == grid[a] - 1 for a in range(len(grid))])

        @pl.when(first)
        def _():
            _comm_start(*_comm_copies(kind, x_ref, c_ref, *sems))

        body(*ins, *outs, *scr)

        @pl.when(last)
        def _():
            _comm_wait(*_comm_copies(kind, x_ref, c_ref, *sems))

    any_spec = pl.BlockSpec(memory_space=pl.ANY)
    call = pl.pallas_call(
        wrapped, name=name, grid=grid, in_specs=list(in_specs) + [any_spec], out_specs=o_specs + [any_spec],
        out_shape=o_shape + [jax.ShapeDtypeStruct(c_shape, x.dtype)],
        scratch_shapes=list(scratch) + _comm_sems())

    def run(*args):
        res = call(*args, x)
        return (res[0] if single else list(res[:-1])), res[-1]

    return run


def _hosted(fn, comm):
    return (fn(), None) if comm is None else fn(comm=comm)


def _full(shape):
    n = len(shape)
    return pl.BlockSpec(shape, lambda *_: (0,) * n)


def _mesh_pos():
    return lax.axis_index("x"), lax.axis_index("y"), lax.axis_index("c")


def _peer(pos, k):
    x, y, c = pos
    return (x ^ ((k >> 2) & 1), y ^ ((k >> 1) & 1), c ^ (k & 1))


def _flat(pos):
    return 4 * pos[0] + 2 * pos[1] + pos[2]


def _comm_sems():
    return [pltpu.SemaphoreType.DMA((NDEV - 1,)), pltpu.SemaphoreType.DMA((NDEV - 1,)), pltpu.SemaphoreType.DMA]


def _comm_copies(kind, x_ref, out_ref, send_sems, recv_sems, local_sem):
    pos = _mesh_pos()
    me = _flat(pos)
    src = (lambda d: x_ref) if kind == "gather" else (lambda d: x_ref.at[d])
    mine = pltpu.make_async_copy(src(me), out_ref.at[me], local_sem)
    sends, recvs = [], []
    for k in range(1, NDEV):
        peer = _peer(pos, k)
        pid = _flat(peer)
        sems = dict(send_sem=send_sems.at[k - 1], recv_sem=recv_sems.at[k - 1], device_id=peer,
                    device_id_type=MESH_ID)
        sends.append(pltpu.make_async_remote_copy(src_ref=src(pid), dst_ref=out_ref.at[me], **sems))
        recvs.append(pltpu.make_async_remote_copy(src_ref=src(pid), dst_ref=out_ref.at[pid], **sems))
    return mine, sends, recvs


def _comm_start(mine, sends, recvs):
    mine.start()
    for cp in sends:
        cp.start()


def _comm_wait(mine, sends, recvs):
    for cp in recvs:
        cp.wait_recv()
    for cp in sends:
        cp.wait_send()
    mine.wait()


def _collective(kind, x, name):
    def body(x_ref, out_ref, *sems):
        copies = _comm_copies(kind, x_ref, out_ref, *sems)
        _comm_start(*copies)
        _comm_wait(*copies)

    shape = (NDEV,) + x.shape if kind == "gather" else x.shape
    return pl.pallas_call(
        body, name=name, out_shape=jax.ShapeDtypeStruct(shape, x.dtype),
        in_specs=[pl.BlockSpec(memory_space=pl.ANY)], out_specs=pl.BlockSpec(memory_space=pl.ANY),
        scratch_shapes=_comm_sems())(x)


def _all_gather(x, name):
    return _collective("gather", x, name)


def _mm_nn(a, w, name, *, mode="plain", res=None, gate=None, norm=None, tm=1024, tn=1024, tk=1024, comm=None):
    M, K = a.shape
    N = w.shape[1]
    tm, tk, tn = min(tm, M), min(tk, K), min(tn, N)
    nk = K // tk

    def body(*refs):
        refs = list(refs)
        acc = refs.pop() if nk > 1 else None
        if mode == "resid":
            a_ref, w_ref, res_ref, gate_ref, o_ref = refs
        elif mode == "resid_norm":
            a_ref, w_ref, res_ref, gate_ref, g_ref, sc_ref, sh_ref, o_ref, r_ref = refs
        elif mode == "relu2":
            a_ref, w_ref, o_ref, r_ref = refs
        else:
            a_ref, w_ref, o_ref = refs
        k = pl.program_id(2)
        part = _dot(_mx(a_ref[...]), w_ref[...])

        def finish(r):
            if mode == "resid":
                o_ref[...] = res_ref[...] + gate_ref[...] * r
            elif mode == "resid_norm":
                xv = res_ref[...] + gate_ref[...] * r
                o_ref[...] = xv
                rs = lax.rsqrt(jnp.mean(xv * xv, axis=1, keepdims=True) + EPS)
                r_ref[...] = ((xv * rs * g_ref[...]) * (1.0 + sc_ref[...]) + sh_ref[...]).astype(r_ref.dtype)
            elif mode == "relu2":
                o_ref[...] = r
                r_ref[...] = jnp.square(jnp.maximum(r, 0.0)).astype(r_ref.dtype)
            else:
                o_ref[...] = r

        if nk == 1:
            finish(part)
        else:
            @pl.when(k == 0)
            def _():
                acc[...] = part

            @pl.when((k > 0) & (k < nk - 1))
            def _():
                acc[...] += part

            @pl.when(k == nk - 1)
            def _():
                finish(acc[...] + part)

    in_specs = [pl.BlockSpec((tm, tk), lambda i, j, k: (i, k)),
                pl.BlockSpec((tk, tn), lambda i, j, k: (k, j))]
    args = [a, w]
    o_spec = pl.BlockSpec((tm, tn), lambda i, j, k: (i, j))
    out_specs, out_shape = o_spec, jax.ShapeDtypeStruct((M, N), F32)
    vec = pl.BlockSpec((1, tn), lambda i, j, k: (0, j))
    if mode == "resid":
        in_specs += [o_spec, vec]
        args += [res, gate]
    elif mode == "resid_norm":
        assert tn == N
        in_specs += [o_spec, vec, vec, vec, vec]
        args += [res, gate, *norm]
        out_specs = [o_spec, o_spec]
        out_shape = [jax.ShapeDtypeStruct((M, N), F32), jax.ShapeDtypeStruct((M, N), MXU)]
    elif mode == "relu2":
        out_specs = [o_spec, o_spec]
        out_shape = [jax.ShapeDtypeStruct((M, N), F32), jax.ShapeDtypeStruct((M, N), MXU)]
    return _call(body, name, (M // tm, N // tn, nk), in_specs, out_specs, out_shape,
                 [pltpu.VMEM((tm, tn), F32)] if nk > 1 else [], comm=comm)(*args)


def _mm_nt(a, w, name, *, gate=None, drelu=None, tm=1024, tko=1024, tn=1024, comm=None):
    M, N = a.shape
    K = w.shape[0]
    tm, tn, tko = min(tm, M), min(tn, N), min(tko, K)
    nn = N // tn

    def body(*refs):
        refs = list(refs)
        acc = refs.pop() if nn > 1 else None
        a_ref, w_ref = refs[:2]
        rest = refs[2:]
        gate_ref = rest.pop(0) if gate is not None else None
        pre_ref = rest.pop(0) if drelu is not None else None
        (o_ref,) = rest
        n = pl.program_id(2)
        av = a_ref[...]
        if gate_ref is not None:
            av = av * gate_ref[...]
        part = _dot_nt(_mx(av), w_ref[...])

        def finish(r):
            if pre_ref is not None:
                r = r * (2.0 * jnp.maximum(pre_ref[...], 0.0))
            o_ref[...] = r.astype(o_ref.dtype)

        if nn == 1:
            finish(part)
        else:
            @pl.when(n == 0)
            def _():
                acc[...] = part

            @pl.when((n > 0) & (n < nn - 1))
            def _():
                acc[...] += part

            @pl.when(n == nn - 1)
            def _():
                finish(acc[...] + part)

    in_specs = [pl.BlockSpec((tm, tn), lambda i, j, n: (i, n)),
                pl.BlockSpec((tko, tn), lambda i, j, n: (j, n))]
    args = [a, w]
    if gate is not None:
        in_specs.append(pl.BlockSpec((1, tn), lambda i, j, n: (0, n)))
        args.append(gate)
    o_spec = pl.BlockSpec((tm, tko), lambda i, j, n: (i, j))
    if drelu is not None:
        in_specs.append(o_spec)
        args.append(drelu)
    out_dtype = MXU if drelu is not None else F32
    return _call(body, name, (M // tm, K // tko, nn), in_specs, o_spec,
                 jax.ShapeDtypeStruct((M, K), out_dtype), [pltpu.VMEM((tm, tko), F32)] if nn > 1 else [],
                 comm=comm)(*args)


def _mm_tn(a, b, name, *, gate=None, w=None, tk=1024, tn=1024, tm=1024, out_dtype=F32, comm=None):
    M, K = a.shape
    N = b.shape[1]
    tm, tk, tn = min(tm, M), min(tk, K), min(tn, N)
    nm = M // tm
    gated = gate is not None

    def body(*refs):
        if gated:
            a_ref, b_ref, gate_ref, w_ref, o_ref, dg_ref, acc = refs
        else:
            a_ref, b_ref, o_ref, acc = refs
        kk = pl.program_id(1)
        m = pl.program_id(2)
        part = _dot_tn(_mx(a_ref[...]), _mx(b_ref[...]))

        @pl.when((m == 0) & (nm > 1))
        def _():
            acc[...] = part

        @pl.when((m > 0) & (m < nm - 1))
        def _():
            acc[...] += part

        if gated:
            @pl.when((m == 0) & (kk == 0))
            def _():
                dg_ref[...] = jnp.zeros_like(dg_ref)

        @pl.when(m == nm - 1)
        def _():
            r = acc[...] + part if nm > 1 else part
            if gated:
                o_ref[...] = (r * gate_ref[...]).astype(o_ref.dtype)
                dg_ref[...] += _colsum(r * w_ref[...].astype(F32))
            else:
                o_ref[...] = r.astype(o_ref.dtype)

    in_specs = [pl.BlockSpec((tm, tk), lambda j, k, m: (m, k)),
                pl.BlockSpec((tm, tn), lambda j, k, m: (m, j))]
    args = [a, b]
    o_spec = pl.BlockSpec((tk, tn), lambda j, k, m: (k, j))
    out_specs, out_shape = o_spec, jax.ShapeDtypeStruct((K, N), out_dtype)
    if gated:
        in_specs += [pl.BlockSpec((1, tn), lambda j, k, m: (0, j)), o_spec]
        args += [gate, w]
        out_specs = [o_spec, pl.BlockSpec((1, tn), lambda j, k, m: (0, j))]
        out_shape = [out_shape, jax.ShapeDtypeStruct((1, N), F32)]
    return _call(body, name, (N // tn, K // tk, nm), in_specs, out_specs, out_shape,
                 [pltpu.VMEM((tk, tn), F32)], comm=comm)(*args)


SEG_T = 512


def _seg_layout(segs):
    starts, t = [], 0
    for a in segs:
        starts.append(t)
        t += a.shape[1] // SEG_T
    return starts, t


def _seg_spec(tm, lo, hi, row_axis, col_axis):
    def index(*ids):
        col = ids[col_axis]
        act = (col >= lo) & (col < hi)
        return jnp.where(act, ids[row_axis], 0), jnp.where(act, col - lo, 0)

    return pl.BlockSpec((tm, SEG_T), index)


def _in_proj_dw(ht, segs, name, tm=1024):
    S = ht.shape[1]
    tm = min(tm, S)
    nm = S // tm
    starts, ntile = _seg_layout(segs)
    bounds = [(lo, lo + a.shape[1] // SEG_T) for lo, a in zip(starts, segs)]

    def body(*refs):
        h_ref, seg_refs, o_ref, acc = refs[0], refs[1:1 + len(segs)], refs[-2], refs[-1]
        j = pl.program_id(0)
        m = pl.program_id(1)
        for (lo, hi), b_ref in zip(bounds, seg_refs):
            @pl.when((j >= lo) & (j < hi))
            def _():
                part = _dot(h_ref[...], _mx(b_ref[...]))
                if nm == 1:
                    o_ref[...] = part.astype(o_ref.dtype)
                else:
                    @pl.when(m == 0)
                    def _():
                        acc[...] = part

                    @pl.when((m > 0) & (m < nm - 1))
                    def _():
                        acc[...] += part

                    @pl.when(m == nm - 1)
                    def _():
                        o_ref[...] = (acc[...] + part).astype(o_ref.dtype)

    return _call(
        body, name, (ntile, nm),
        [pl.BlockSpec((D, tm), lambda j, m: (0, m))] + [_seg_spec(tm, lo, hi, 1, 0) for lo, hi in bounds],
        pl.BlockSpec((D, SEG_T), lambda j, m: (0, j)), jax.ShapeDtypeStruct((D, ntile * SEG_T), MXU),
        [pltpu.VMEM((D, SEG_T), F32)])(ht, *segs)


def _in_proj_dx(segs, w, name, tm=1024, comm=None):
    S = segs[0].shape[0]
    tm = min(tm, S)
    starts, ntile = _seg_layout(segs)
    bounds = [(lo, lo + a.shape[1] // SEG_T) for lo, a in zip(starts, segs)]

    def body(*refs):
        seg_refs, w_ref, o_ref, acc = refs[:len(segs)], refs[-3], refs[-2], refs[-1]
        n = pl.program_id(1)
        for (lo, hi), a_ref in zip(bounds, seg_refs):
            @pl.when((n >= lo) & (n < hi))
            def _():
                part = _dot_nt(_mx(a_ref[...]), w_ref[...])

                @pl.when(n == 0)
                def _():
                    acc[...] = part

                @pl.when((n > 0) & (n < ntile - 1))
                def _():
                    acc[...] += part

                @pl.when(n == ntile - 1)
                def _():
                    o_ref[...] = acc[...] + part

    return _call(
        body, name, (S // tm, ntile),
        [_seg_spec(tm, lo, hi, 0, 1) for lo, hi in bounds] + [pl.BlockSpec((D, SEG_T), lambda i, n: (0, n))],
        pl.BlockSpec((tm, D), lambda i, n: (i, 0)), jax.ShapeDtypeStruct((S, D), F32),
        [pltpu.VMEM((tm, D), F32)], comm=comm)(*segs, w)


def _modnorm_fwd(x, gain, sc, sh, name, ts=512):
    S = x.shape[0]

    def body(x_ref, g_ref, sc_ref, sh_ref, h_ref, ht_ref):
        xv = x_ref[...]
        r = lax.rsqrt(jnp.mean(xv * xv, axis=1, keepdims=True) + EPS)
        h = (xv * r * g_ref[...]) * (1.0 + sc_ref[...]) + sh_ref[...]
        h_ref[...] = h.astype(h_ref.dtype)
        ht_ref[...] = h.T.astype(ht_ref.dtype)

    row = pl.BlockSpec((ts, D), lambda i: (i, 0))
    vec = pl.BlockSpec((1, D), lambda i: (0, 0))
    return _call(body, name, (S // ts,), [row, vec, vec, vec], [row, pl.BlockSpec((D, ts), lambda i: (0, i))],
                 [jax.ShapeDtypeStruct((S, D), MXU), jax.ShapeDtypeStruct((D, S), MXU)])(x, gain, sc, sh)


def _modnorm_bwd(dh, x, gain, sc, sh, dx_in, name, ts=512):
    S = x.shape[0]

    def body(dh_ref, x_ref, g_ref, sc_ref, sh_ref, dxin_ref, dx_ref, dsc_ref, dsh_ref, dg_ref):
        i = pl.program_id(0)
        xv = x_ref[...]
        dhv = dh_ref[...]
        g = g_ref[...]
        r = lax.rsqrt(jnp.mean(xv * xv, axis=1, keepdims=True) + EPS)
        xr = xv * r
        dn = dhv * (1.0 + sc_ref[...])
        u = dn * g
        dx_ref[...] = dxin_ref[...] + r * (u - xr * jnp.mean(xr * u, axis=1, keepdims=True))

        @pl.when(i == 0)
        def _():
            dsc_ref[...] = jnp.zeros_like(dsc_ref)
            dsh_ref[...] = jnp.zeros_like(dsh_ref)
            dg_ref[...] = jnp.zeros_like(dg_ref)

        dsc_ref[...] += _colsum(dhv * (xr * g))
        dsh_ref[...] += _colsum(dhv)
        dg_ref[...] += _colsum(dn * xr)

    row = pl.BlockSpec((ts, D), lambda i: (i, 0))
    vec = pl.BlockSpec((1, D), lambda i: (0, 0))
    vshape = jax.ShapeDtypeStruct((1, D), F32)
    return _call(body, name, (S // ts,), [row, row, vec, vec, vec, row], [row, vec, vec, vec],
                 [jax.ShapeDtypeStruct((S, D), F32), vshape, vshape, vshape])(dh, x, gain, sc, sh, dx_in)


def _loss_head(x, target, gain, name, ts=512):
    S = x.shape[0]

    def body(x_ref, t_ref, g_ref, loss_ref, dx_ref, dg_ref):
        i = pl.program_id(0)
        xv = x_ref[...]
        g = g_ref[...]
        r = lax.rsqrt(jnp.mean(xv * xv, axis=1, keepdims=True) + EPS)
        xr = xv * r
        e = xr * g - t_ref[...]
        dy = e * (1.0 / D)
        u = dy * g
        dx_ref[...] = r * (u - xr * jnp.mean(xr * u, axis=1, keepdims=True))

        @pl.when(i == 0)
        def _():
            loss_ref[...] = jnp.zeros_like(loss_ref)
            dg_ref[...] = jnp.zeros_like(dg_ref)

        part = 0.5 * jnp.sum(jnp.mean(e * e, axis=1, keepdims=True), axis=0, keepdims=True)
        loss_ref[...] += jnp.broadcast_to(part, loss_ref.shape)
        dg_ref[...] += _colsum(dy * xr)

    row = pl.BlockSpec((ts, D), lambda i: (i, 0))
    vec = pl.BlockSpec((1, D), lambda i: (0, 0))
    return _call(body, name, (S // ts,), [row, row, vec],
                 [pl.BlockSpec((1, 128), lambda i: (0, 0)), row, vec],
                 [jax.ShapeDtypeStruct((1, 128), F32), jax.ShapeDtypeStruct((S, D), F32),
                  jax.ShapeDtypeStruct((1, D), F32)])(x, target, gain)


def _merge_specs(ts):
    o_spec = pl.BlockSpec((NH, ts, HD), lambda i: (0, i, 0))
    zg = pl.BlockSpec((ts, D), lambda i: (i, Z_GATE // D))
    za = pl.BlockSpec((ts, D), lambda i: (i, Z_BR // D))
    zb = pl.BlockSpec((ts, D), lambda i: (i, Z_BR // D + 1))
    row = pl.BlockSpec((ts, D), lambda i: (i, 0))
    gn = pl.BlockSpec((1, HD), lambda i: (0, 0))
    return o_spec, zg, za, zb, row, gn


def _merge_fwd(o, z, ob, gn, name, ts=256):
    S = ob.shape[0]

    def body(o_ref, zg_ref, za_ref, zb_ref, ob_ref, gn_ref, m_ref):
        for h in range(NH):
            sl = slice(h * HD, (h + 1) * HD)
            oh = o_ref[h]
            r = lax.rsqrt(jnp.mean(oh * oh, axis=1, keepdims=True) + EPS)
            gate = zg_ref[:, sl]
            oa = (oh * r * gn_ref[...]) * (gate * _sigmoid(gate))
            m = _sigmoid(za_ref[:, sl]) * oa + _sigmoid(zb_ref[:, sl]) * ob_ref[:, sl]
            m_ref[:, sl] = m.astype(m_ref.dtype)

    o_spec, zg, za, zb, row, gns = _merge_specs(ts)
    return _call(body, name, (S // ts,), [o_spec, zg, za, zb, row, gns], row,
                 jax.ShapeDtypeStruct((S, D), MXU))(o, z, z, z, ob, gn)


def _merge_bwd(dm, o, z, ob, gn, name, ts=256):
    S = ob.shape[0]

    def body(dm_ref, o_ref, zg_ref, za_ref, zb_ref, ob_ref, gn_ref,
             do_ref, dzg_ref, dob_ref, dza_ref, dzb_ref, dgn_ref):
        i = pl.program_id(0)
        gn_v = gn_ref[...]
        dgn = jnp.zeros((1, HD), F32)
        for h in range(NH):
            sl = slice(h * HD, (h + 1) * HD)
            dmh = dm_ref[:, sl]
            oh = o_ref[h]
            r = lax.rsqrt(jnp.mean(oh * oh, axis=1, keepdims=True) + EPS)
            ohr = oh * r
            on = ohr * gn_v
            gate = zg_ref[:, sl]
            sg = _sigmoid(gate)
            silu = gate * sg
            oa = on * silu
            ga = _sigmoid(za_ref[:, sl])
            gb = _sigmoid(zb_ref[:, sl])
            obh = ob_ref[:, sl]
            doa = dmh * ga
            dob_ref[:, sl] = dmh * gb
            dza_ref[:, sl] = (dmh * oa * ga * (1.0 - ga)).astype(dza_ref.dtype)
            dzb_ref[:, sl] = (dmh * obh * gb * (1.0 - gb)).astype(dzb_ref.dtype)
            don = doa * silu
            dzg_ref[:, sl] = (doa * on * (sg * (1.0 + gate * (1.0 - sg)))).astype(dzg_ref.dtype)
            dgn = dgn + _colsum(don * ohr)
            u = don * gn_v
            do_ref[h] = r * (u - ohr * jnp.mean(ohr * u, axis=1, keepdims=True))

        @pl.when(i == 0)
        def _():
            dgn_ref[...] = jnp.zeros_like(dgn_ref)

        dgn_ref[...] += dgn

    o_spec, zg, za, zb, row, gns = _merge_specs(ts)
    return _call(
        body, name, (S // ts,), [row, o_spec, zg, za, zb, row, gns],
        [o_spec, row, row, row, row, gns],
        [jax.ShapeDtypeStruct((NH, S, HD), F32), jax.ShapeDtypeStruct((S, D), MXU),
         jax.ShapeDtypeStruct((S, D), F32), jax.ShapeDtypeStruct((S, D), MXU),
         jax.ShapeDtypeStruct((S, D), MXU), jax.ShapeDtypeStruct((1, HD), F32)],
    )(dm, o, z, z, z, ob, gn)


GROWS = 256
GCH = GROWS // CH


def _gdn_prep_fwd(z, conv_w, alog_row, dtb_row, name, comm=None):
    S = z.shape[0]
    ts = GROWS
    scale = HD ** -0.5

    def body(z_ref, halo_ref, zab_ref, w_ref, al_ref, dt_ref, q_ref, k_ref, v_ref, gb_ref, buf):
        i = pl.program_id(0)
        buf[0:8, :] = jnp.where(i == 0, 0.0, halo_ref[...])
        buf[8:8 + ts, :] = z_ref[...]
        outs = (q_ref, k_ref, v_ref)
        for seg in range(3):
            cs = slice(seg * D, (seg + 1) * D)
            c = jnp.zeros((ts, D), F32)
            for j in range(4):
                c = c + w_ref[j:j + 1, cs] * buf[pl.ds(5 + j, ts), cs]
            s = c * _sigmoid(c)
            if seg == 2:
                outs[seg][...] = s
            else:
                mul = scale if seg == 0 else 1.0
                for h in range(NH):
                    sl = slice(h * HD, (h + 1) * HD)
                    sh = s[:, sl]
                    r = lax.rsqrt(_rowsum(sh * sh) + EPS)
                    outs[seg][:, sl] = sh * (r * mul)
        zab = zab_ref[...]
        lane = lax.broadcasted_iota(jnp.int32, zab.shape, 1)
        g = -jnp.exp(al_ref[...]) * _softplus(zab + dt_ref[...])
        ri = lax.broadcasted_iota(jnp.int32, (CH, CH), 0)
        ci = lax.broadcasted_iota(jnp.int32, (CH, CH), 1)
        incl = (ri >= ci).astype(F32)
        gcum = jnp.concatenate([_dot(incl, g[c * CH:(c + 1) * CH], HI) for c in range(ts // CH)], axis=0)
        gb_ref[...] = jnp.where(lane < NH, gcum, jnp.where(lane < 2 * NH, _sigmoid(zab), 0.0))

    row = pl.BlockSpec((ts, D), lambda i: (i, 0))
    vec = pl.BlockSpec((1, 128), lambda i: (0, 0))
    return _call(
        body, name, (S // ts,),
        [pl.BlockSpec((ts, 3 * D), lambda i: (i, 0)),
         pl.BlockSpec((8, 3 * D), lambda i: (jnp.maximum(i * (ts // 8) - 1, 0), 0)),
         pl.BlockSpec((ts, 128), lambda i: (i, Z_AB // 128)),
         _full((4, 3 * D)), vec, vec],
        [row, row, row, pl.BlockSpec((ts, 128), lambda i: (i, 0))],
        [jax.ShapeDtypeStruct((S, D), F32)] * 3 + [jax.ShapeDtypeStruct((S, 128), F32)],
        [pltpu.VMEM((ts + 8, 3 * D), F32)], comm=comm,
    )(z, z, z, conv_w, alog_row, dtb_row)


def _split(a):
    hi = a.astype(MXU)
    return hi, (a - hi.astype(F32)).astype(MXU)


def _dot3(a, b, dot=_dot):
    ah, al = _split(a)
    bh, bl = _split(b)
    return dot(ah, bh) + (dot(ah, bl) + dot(al, bh))


IROWS = 512
ICH = IROWS // CH


def _chunk_common(gbk, h, k):
    lane = lax.broadcasted_iota(jnp.int32, gbk.shape, 1)
    G = _rowsum(jnp.where(lane == h, gbk, 0.0))
    b_col = _rowsum(jnp.where(lane == h + NH, gbk, 0.0))
    ri = lax.broadcasted_iota(jnp.int32, (CH, CH), 0)
    ci = lax.broadcasted_iota(jnp.int32, (CH, CH), 1)
    incl = ri >= ci
    gc = jnp.broadcast_to(G, (CH, CH))
    decay = jnp.where(incl, jnp.exp(jnp.where(incl, gc - gc.T, 0.0)), 0.0)
    Gl = G[CH - 1:CH, :]
    kb = k * b_col
    return dict(b=b_col, ri=ri, ci=ci, incl=incl, strict=ri > ci, decay=decay, eG=jnp.exp(G),
                e2=jnp.exp(Gl - G), cd=jnp.exp(Gl), kb=kb, kk=_dot_nt(_mx(kb), _mx(k)))


def _gdn_intra_fwd(qn, kn, v, gb, name, comm=None):
    S = qn.shape[0]

    def body(q_ref, k_ref, v_ref, gb_ref, u_ref, w_ref, qd_ref, kd_ref, at_ref, t_ref, cd_ref):
        h = pl.program_id(1)
        rows = [slice(c * CH, (c + 1) * CH) for c in range(ICH)]
        ks = [k_ref[r, :] for r in rows]
        cms = [_chunk_common(gb_ref[r, :], h, k) for r, k in zip(rows, ks)]
        ps = [jnp.where(cm["strict"], cm["kk"] * cm["decay"], 0.0) for cm in cms]
        ts = [(cm["ri"] == cm["ci"]).astype(F32) - p for cm, p in zip(cms, ps)]
        for _ in range(5):
            ps = [_dot3(p, p) for p in ps]
            ts = [t + _dot3(t, p) for t, p in zip(ts, ps)]
        for c, (r, k, cm, t) in enumerate(zip(rows, ks, cms, ts)):
            rhs = jnp.concatenate([v_ref[r, :] * cm["b"], k * (cm["b"] * cm["eG"])], axis=1)
            sol = _dot3(t, rhs)
            u_ref[0, r, :] = sol[:, :HD]
            w_ref[0, r, :] = sol[:, HD:]
            t_ref[0, r, :] = t
        for c, (r, k, cm) in enumerate(zip(rows, ks, cms)):
            q = q_ref[r, :]
            qk = _dot_nt(_mx(q), _mx(k))
            at_ref[0, r, :] = jnp.where(cm["incl"], qk * cm["decay"], 0.0)
            qd_ref[0, r, :] = q * cm["eG"]
            kd_ref[0, r, :] = k * cm["e2"]
            cd_ref[0, c] = jnp.broadcast_to(cm["cd"], (8, 128))

    tok = pl.BlockSpec((IROWS, HD), lambda i, h: (i, h))
    hm = pl.BlockSpec((1, IROWS, HD), lambda i, h: (h, i, 0))
    hm64 = pl.BlockSpec((1, IROWS, CH), lambda i, h: (h, i, 0))
    big = jax.ShapeDtypeStruct((NH, S, HD), F32)
    sm = jax.ShapeDtypeStruct((NH, S, CH), F32)
    return _call(
        body, name, (S // IROWS, NH),
        [tok, tok, tok, pl.BlockSpec((IROWS, 128), lambda i, h: (i, 0))],
        [hm, hm, hm, hm, hm64, hm64, pl.BlockSpec((1, ICH, 8, 128), lambda i, h: (h, i, 0, 0))],
        [big, big, big, big, sm, sm, jax.ShapeDtypeStruct((NH, S // CH, 8, 128), F32)], comm=comm,
    )(qn, kn, v, gb)


def _scale_state(s, cd_tile):
    return (s.reshape(HD // 8, 8, HD) * cd_tile[None]).reshape(HD, HD)


def _gdn_scan_fwd(u, w, qd, kd, attn, cdt, name):
    S = u.shape[1]
    nblk = S // GROWS

    def body(u_ref, w_ref, qd_ref, kd_ref, at_ref, cd_ref, o_ref, vn_ref, st_ref, s_ref):
        i = pl.program_id(0)

        @pl.when(i == 0)
        def _():
            s_ref[...] = jnp.zeros_like(s_ref)

        def chunk(c, carry):
            r0 = pl.multiple_of(c * CH, CH)
            rows = pl.ds(r0, CH)
            H = range(NH)
            shs = [s_ref[h] for h in H]
            sbs = [_mx(sh) for sh in shs]
            ws = [_dot(_mx(w_ref[h, rows, :]), sbs[h]) for h in H]
            qs = [_dot(_mx(qd_ref[h, rows, :]), sbs[h]) for h in H]
            vns = [u_ref[h, rows, :] - ws[h] for h in H]
            vbs = [_mx(vn) for vn in vns]
            avs = [_dot(_mx(at_ref[h, rows, :]), vbs[h]) for h in H]
            kvs = [_dot_tn(_mx(kd_ref[h, rows, :]), vbs[h]) for h in H]
            for h in H:
                st_ref[h, c] = shs[h]
                vn_ref[h, rows, :] = vns[h]
                o_ref[h, rows, :] = qs[h] + avs[h]
                s_ref[h] = _scale_state(shs[h], cd_ref[h, c]) + kvs[h]
            return carry

        lax.fori_loop(0, GCH, chunk, 0)

    hm = pl.BlockSpec((NH, GROWS, HD), lambda i: (0, i, 0))
    hm64 = pl.BlockSpec((NH, GROWS, CH), lambda i: (0, i, 0))
    big = jax.ShapeDtypeStruct((NH, S, HD), F32)
    return _call(
        body, name, (nblk,),
        [hm, hm, hm, hm, hm64, pl.BlockSpec((NH, GCH, 8, 128), lambda i: (0, i, 0, 0))],
        [hm, hm, pl.BlockSpec((NH, GCH, HD, HD), lambda i: (0, i, 0, 0))],
        [big, big, jax.ShapeDtypeStruct((NH, S // CH, HD, HD), F32)],
        [pltpu.VMEM((NH, HD, HD), F32)],
    )(u, w, qd, kd, attn, cdt)


def _gdn_scan_bwd(do, w, qd, kd, attn, cdt, vn, st, name):
    S = do.shape[1]
    nblk = S // GROWS

    def body(do_ref, w_ref, qd_ref, kd_ref, at_ref, cd_ref, vn_ref, st_ref,
             dqd_ref, dkd_ref, dvn_ref, dw_ref, dat_ref, dcd_ref, ds_ref):
        i = pl.program_id(0)

        @pl.when(i == 0)
        def _():
            ds_ref[...] = jnp.zeros_like(ds_ref)

        def chunk(cc, carry):
            c = GCH - 1 - cc
            r0 = pl.multiple_of(c * CH, CH)
            rows = pl.ds(r0, CH)
            H = range(NH)
            dsps = [ds_ref[h] for h in H]
            shs = [st_ref[h, c] for h in H]
            dsbs = [_mx(a) for a in dsps]
            sbs = [_mx(a) for a in shs]
            dobs = [_mx(do_ref[h, rows, :]) for h in H]
            vbs = [_mx(vn_ref[h, rows, :]) for h in H]
            dvns = [_dot(_mx(kd_ref[h, rows, :]), dsbs[h]) + _dot_tn(_mx(at_ref[h, rows, :]), dobs[h]) for h in H]
            dvbs = [_mx(a) for a in dvns]
            dqds = [_dot_nt(dobs[h], sbs[h]) for h in H]
            dats = [_dot_nt(dobs[h], vbs[h]) for h in H]
            dkds = [_dot_nt(vbs[h], dsbs[h]) for h in H]
            dws = [_dot_nt(dvbs[h], sbs[h]) for h in H]
            qdos = [_dot_tn(_mx(qd_ref[h, rows, :]), dobs[h]) for h in H]
            wdvs = [_dot_tn(_mx(w_ref[h, rows, :]), dvbs[h]) for h in H]
            for h in H:
                dvn_ref[h, rows, :] = dvns[h]
                dqd_ref[h, rows, :] = dqds[h]
                dat_ref[h, rows, :] = dats[h]
                dkd_ref[h, rows, :] = dkds[h]
                dw_ref[h, rows, :] = -dws[h]
                dcd = jnp.sum(_rowsum(dsps[h] * shs[h]), axis=0, keepdims=True)
                dcd_ref[h, c] = jnp.broadcast_to(dcd, (8, 128))
                ds_ref[h] = _scale_state(dsps[h], cd_ref[h, c]) + qdos[h] - wdvs[h]
            return carry

        lax.fori_loop(0, GCH, chunk, 0)

    hm = pl.BlockSpec((NH, GROWS, HD), lambda i: (0, nblk - 1 - i, 0))
    hm64 = pl.BlockSpec((NH, GROWS, CH), lambda i: (0, nblk - 1 - i, 0))
    tile = pl.BlockSpec((NH, GCH, 8, 128), lambda i: (0, nblk - 1 - i, 0, 0))
    big = jax.ShapeDtypeStruct((NH, S, HD), F32)
    return _call(
        body, name, (nblk,),
        [hm, hm, hm, hm, hm64, tile, hm, pl.BlockSpec((NH, GCH, HD, HD), lambda i: (0, nblk - 1 - i, 0, 0))],
        [hm, hm, hm, hm, hm64, tile],
        [big, big, big, big, jax.ShapeDtypeStruct((NH, S, CH), F32),
         jax.ShapeDtypeStruct((NH, S // CH, 8, 128), F32)],
        [pltpu.VMEM((NH, HD, HD), F32)],
    )(do, w, qd, kd, attn, cdt, vn, st)


def _gdn_intra_bwd(qn, kn, v, gb, u, w, tmat, dqd, dkd, du, dw, dattn, dcdt, name):
    S = qn.shape[0]

    def body(q_ref, k_ref, v_ref, gb_ref, u_ref, w_ref, t_ref, dqd_ref, dkd_ref, du_ref, dw_ref,
             dat_ref, dcd_ref, dq_ref, dk_ref, dv_ref, dgb_ref):
        h = pl.program_id(1)
        rows = [slice(c * CH, (c + 1) * CH) for c in range(ICH)]
        ks = [k_ref[r, :] for r in rows]
        cms = [_chunk_common(gb_ref[r, :], h, k) for r, k in zip(rows, ks)]
        sols = [jnp.concatenate([u_ref[0, r, :], w_ref[0, r, :]], axis=1) for r in rows]
        drhss = [_dot3(t_ref[0, r, :], jnp.concatenate([du_ref[0, r, :], dw_ref[0, r, :]], axis=1), _dot_tn)
                 for r in rows]
        das = [-_dot3(drhs, sol, _dot_nt) for drhs, sol in zip(drhss, sols)]
        for c, (r, k, cm, drhs, da) in enumerate(zip(rows, ks, cms, drhss, das)):
            q, vv = q_ref[r, :], v_ref[r, :]
            decay, eG, e2, b = cm["decay"], cm["eG"], cm["e2"], cm["b"]
            dru, drw = drhs[:, :HD], drhs[:, HD:]
            dv_ref[r, :] = dru * b
            s_w = _rowsum(drw * k)
            dbeta = _rowsum(dru * vv) + s_w * eG
            deg = s_w * b
            dk = drw * (b * eG)
            dkk = jnp.where(cm["strict"], da * decay, 0.0)
            ddec = jnp.where(cm["strict"], da * cm["kk"], 0.0)
            dkkb = _mx(dkk)
            dkb = _dot(dkkb, _mx(k))
            dk = dk + _dot_tn(dkkb, _mx(cm["kb"])) + dkb * b
            dbeta = dbeta + _rowsum(dkb * k)
            dat = jnp.where(cm["incl"], dat_ref[0, r, :], 0.0)
            qk = _dot_nt(_mx(q), _mx(k))
            dqk = _mx(dat * decay)
            ddec = ddec + dat * qk
            dqd = dqd_ref[0, r, :]
            dkd = dkd_ref[0, r, :]
            dq_ref[r, :] = _dot(dqk, _mx(k)) + dqd * eG
            dk_ref[r, :] = dk + _dot_tn(dqk, _mx(q)) + dkd * e2
            deg = deg + _rowsum(dqd * q)
            t2 = _rowsum(dkd * k) * e2
            dgl = jnp.sum(t2, axis=0, keepdims=True) + dcd_ref[0, c][0:1, 0:1] * cm["cd"]
            dd = ddec * decay
            dG = deg * eG - t2 + _rowsum(dd) - _rowsum(dd.T)
            rowi = lax.broadcasted_iota(jnp.int32, (CH, 1), 0)
            dG = dG + jnp.where(rowi == CH - 1, dgl, 0.0)
            lane = lax.broadcasted_iota(jnp.int32, (CH, 128), 1)
            dgb_ref[0, r, :] = jnp.where(lane == h, dG, 0.0) + jnp.where(lane == h + NH, dbeta, 0.0)

    tok = pl.BlockSpec((IROWS, HD), lambda i, h: (i, h))
    hm = pl.BlockSpec((1, IROWS, HD), lambda i, h: (h, i, 0))
    hm64 = pl.BlockSpec((1, IROWS, CH), lambda i, h: (h, i, 0))
    tile = pl.BlockSpec((1, ICH, 8, 128), lambda i, h: (h, i, 0, 0))
    tokout = jax.ShapeDtypeStruct((S, D), F32)
    return _call(
        body, name, (S // IROWS, NH),
        [tok, tok, tok, pl.BlockSpec((IROWS, 128), lambda i, h: (i, 0)), hm, hm, hm64,
         hm, hm, hm, hm, hm64, tile],
        [tok, tok, tok, pl.BlockSpec((1, IROWS, 128), lambda i, h: (h, i, 0))],
        [tokout, tokout, tokout, jax.ShapeDtypeStruct((NH, S, 128), F32)],
    )(qn, kn, v, gb, u, w, tmat, dqd, dkd, du, dw, dattn, dcdt)


def _gdn_prep_bwd(z, dqn, dkn, dv, dgb, conv_w, alog_row, dtb_row, name):
    S = z.shape[0]
    ts = GROWS
    nblk = S // ts
    scale = HD ** -0.5
    tb = ts // 8

    def body(z_ref, hp_ref, hn_ref, zab_ref, dq_ref, dqn_ref, dk_ref, dkn_ref, dv_ref, dvn_ref,
             dgb_ref, w_ref, al_ref, dt_ref, dz_ref, dzab_ref, dcw_ref, dvec_ref, buf, dybuf, dcbuf):
        i = pl.program_id(0)
        last = i == nblk - 1

        @pl.when(i == 0)
        def _():
            dcw_ref[...] = jnp.zeros_like(dcw_ref)
            dvec_ref[...] = jnp.zeros_like(dvec_ref)

        buf[0:8, :] = jnp.where(i == 0, 0.0, hp_ref[...])
        buf[8:8 + ts, :] = z_ref[...]
        buf[8 + ts:16 + ts, :] = hn_ref[...]
        rowi = lax.broadcasted_iota(jnp.int32, (ts + 8, 1), 0)
        live = jnp.logical_or(rowi < ts, jnp.logical_not(last))
        dys = ((dq_ref, dqn_ref), (dk_ref, dkn_ref), (dv_ref, dvn_ref))
        for seg in range(3):
            cs = slice(seg * D, (seg + 1) * D)
            dybuf[0:ts, :] = dys[seg][0][...]
            dybuf[ts:ts + 8, :] = dys[seg][1][...]
            taps = [buf[pl.ds(5 + j, ts + 8), cs] for j in range(4)]
            c = jnp.zeros((ts + 8, D), F32)
            for j in range(4):
                c = c + w_ref[j:j + 1, cs] * taps[j]
            sg = _sigmoid(c)
            s = c * sg
            dsilu = sg * (1.0 + c * (1.0 - sg))
            if seg == 2:
                dcbuf[...] = jnp.where(live, dybuf[...] * dsilu, 0.0)
            else:
                mul = scale if seg == 0 else 1.0
                for h in range(NH):
                    sl = slice(h * HD, (h + 1) * HD)
                    sh = s[:, sl]
                    dy = dybuf[:, sl]
                    r = lax.rsqrt(_rowsum(sh * sh) + EPS)
                    shr = sh * r
                    ds = (mul * r) * (dy - shr * _rowsum(shr * dy))
                    dcbuf[:, sl] = jnp.where(live, ds * dsilu[:, sl], 0.0)
            dx = jnp.zeros((ts, D), F32)
            for j in range(4):
                dcw_ref[j:j + 1, cs] += _colsum(dcbuf[0:ts, :] * taps[j][0:ts])
                dx = dx + w_ref[j:j + 1, cs] * dcbuf[pl.ds(3 - j, ts), :]
            dz_ref[:, cs] = dx.astype(dz_ref.dtype)
        dgbs = dgb_ref[0]
        for h in range(1, NH):
            dgbs = dgbs + dgb_ref[h]
        ri = lax.broadcasted_iota(jnp.int32, (CH, CH), 0)
        ci = lax.broadcasted_iota(jnp.int32, (CH, CH), 1)
        rev = (ci >= ri).astype(F32)
        dgrev = jnp.concatenate([_dot(rev, dgbs[c * CH:(c + 1) * CH], HI) for c in range(ts // CH)], axis=0)
        lane0 = lax.broadcasted_iota(jnp.int32, dgbs.shape, 1)
        dgbs = jnp.where(lane0 < NH, dgrev, dgbs)
        zab = zab_ref[...]
        lane = lax.broadcasted_iota(jnp.int32, zab.shape, 1)
        xx = zab + dt_ref[...]
        ea = jnp.exp(al_ref[...])
        g = -ea * _softplus(xx)
        da = dgbs * (-ea) * _sigmoid(xx)
        beta = _sigmoid(zab)
        db = dgbs * beta * (1.0 - beta)
        is_a = lane < NH
        dzab = jnp.where(is_a, da, jnp.where(lane < 2 * NH, db, 0.0))
        dzab_ref[:, 0:128] = dzab.astype(dzab_ref.dtype)
        dzab_ref[:, 128:512] = jnp.zeros((ts, 384), dzab_ref.dtype)
        dvec_ref[0:1, :] += _colsum(jnp.where(is_a, dgbs * g, 0.0))
        dvec_ref[1:2, :] += _colsum(jnp.where(is_a, da, 0.0))

    z3 = pl.BlockSpec((ts, 3 * D), lambda i: (i, 0))
    row = pl.BlockSpec((ts, D), lambda i: (i, 0))
    nxt = pl.BlockSpec((8, D), lambda i: (jnp.minimum((i + 1) * tb, S // 8 - 1), 0))
    vec = pl.BlockSpec((1, 128), lambda i: (0, 0))
    return _call(
        body, name, (nblk,),
        [z3,
         pl.BlockSpec((8, 3 * D), lambda i: (jnp.maximum(i * tb - 1, 0), 0)),
         pl.BlockSpec((8, 3 * D), lambda i: (jnp.minimum((i + 1) * tb, S // 8 - 1), 0)),
         pl.BlockSpec((ts, 128), lambda i: (i, Z_AB // 128)),
         row, nxt, row, nxt, row, nxt,
         pl.BlockSpec((NH, ts, 128), lambda i: (0, i, 0)),
         _full((4, 3 * D)), vec, vec],
        [z3, pl.BlockSpec((ts, 512), lambda i: (i, 0)), _full((8, 3 * D)), _full((8, 128))],
        [jax.ShapeDtypeStruct((S, 3 * D), MXU), jax.ShapeDtypeStruct((S, 512), MXU),
         jax.ShapeDtypeStruct((8, 3 * D), F32), jax.ShapeDtypeStruct((8, 128), F32)],
        [pltpu.VMEM((ts + 16, 3 * D), F32), pltpu.VMEM((ts + 8, D), F32), pltpu.VMEM((ts + 8, D), F32)],
    )(z, z, z, z, dqn, dqn, dkn, dkn, dv, dv, dgb, conv_w, alog_row, dtb_row)


def _bias_index():
    u = np.arange(FW)[None, :]
    s = np.arange(3)[:, None]
    return np.clip(KWIN - 1 - u - QB * s, -256, 256) + 256


def _bias_vec(rel_bias_pad, onehot, name):
    def body(rb_ref, e_ref, o_ref):
        o_ref[:, 0, :] = _dot_nt(rb_ref[...], e_ref[0], HI)

    return _call(body, name, (3,),
                 [_full((NH, 640)), pl.BlockSpec((1, FW, 640), lambda s: (s, 0, 0))],
                 pl.BlockSpec((NH, 1, FW), lambda s: (s, 0, 0)),
                 jax.ShapeDtypeStruct((3 * NH, 1, FW), F32))(rel_bias_pad, onehot)


def _att_window(i):
    return pl.multiple_of(jnp.maximum(i * QB - PAST * CH, 0), QB)


def _bias_mask(fvec, name):
    def body(f_ref, o_ref):
        i = 2 - pl.program_id(0) // NH
        ws = jnp.maximum(i * QB - PAST * CH, 0)
        fb = jnp.broadcast_to(f_ref[0], (QB, FW))
        bias = pltpu.roll(fb, FW - 255, 1, stride=1, stride_axis=0)[:, :KWIN]
        qc = (i * QB + lax.broadcasted_iota(jnp.int32, (QB, KWIN), 0)) // CH
        kc = (ws + lax.broadcasted_iota(jnp.int32, (QB, KWIN), 1)) // CH
        o_ref[0] = jnp.where((kc <= qc) & (kc >= qc - PAST), bias, MASKED)

    return _call(body, name, (3 * NH,), [pl.BlockSpec((1, 1, FW), lambda j: (j, 0, 0))],
                 pl.BlockSpec((1, QB, KWIN), lambda j: (j, 0, 0)),
                 jax.ShapeDtypeStruct((3 * NH, QB, KWIN), F32))(fvec)


def _att_scores(q_ref, k_ref, bm_ref, i):
    ws = _att_window(i)
    q = _mx(q_ref[...] * (HD ** -0.5))
    kw = _mx(k_ref[pl.ds(ws, KWIN), :])
    return q, kw, ws, _dot_nt(q, kw) + bm_ref[0]


def _att_specs(S):
    c0 = Z_ATT // HD
    q = pl.BlockSpec((QB, HD), lambda h, i: (i, c0 + h))
    k = pl.BlockSpec((S, HD), lambda h, i: (0, c0 + NH + h))
    v = pl.BlockSpec((S, HD), lambda h, i: (0, c0 + 2 * NH + h))
    bm = pl.BlockSpec((1, QB, KWIN), lambda h, i: (jnp.maximum(2 - i, 0) * NH + h, 0, 0))
    tok = pl.BlockSpec((QB, HD), lambda h, i: (i, h))
    return q, k, v, bm, tok


def _att_fwd(z, bmask, name, comm=None):
    S = z.shape[0]

    def body(q_ref, k_ref, v_ref, bm_ref, o_ref, lse_ref):
        _, _, ws, s = _att_scores(q_ref, k_ref, bm_ref, pl.program_id(1))
        m = jnp.max(s, axis=1, keepdims=True)
        p = jnp.exp(s - m)
        l = _rowsum(p)
        o_ref[...] = _dot(_mx(p), _mx(v_ref[pl.ds(ws, KWIN), :])) * (1.0 / l)
        lse_ref[...] = jnp.broadcast_to(m + jnp.log(l), (QB, HD))

    q, k, v, bm, tok = _att_specs(S)
    shp = jax.ShapeDtypeStruct((S, D), F32)
    return _call(body, name, (NH, S // QB), [q, k, v, bm], [tok, tok], [shp, shp], comm=comm)(z, z, z, bmask)


def _att_bwd(z, bmask, ob, lse, dob, name):
    S = z.shape[0]
    nq = S // QB

    def body(q_ref, k_ref, v_ref, bm_ref, o_ref, lse_ref, do_ref, dq_ref, dk_ref, dv_ref, db_ref, dk_acc, dv_acc):
        i = pl.program_id(1)
        q, kw, ws, s = _att_scores(q_ref, k_ref, bm_ref, i)
        p = jnp.exp(s - lse_ref[:, 0:1])
        do = do_ref[...]
        dob16 = _mx(do)
        dp = _dot_nt(dob16, _mx(v_ref[pl.ds(ws, KWIN), :]))
        ds = p * (dp - _rowsum(do * o_ref[...]))
        dsb = _mx(ds)
        dq_ref[...] = (_dot(dsb, kw) * (HD ** -0.5)).astype(dq_ref.dtype)

        @pl.when(i == 0)
        def _():
            dk_acc[...] = jnp.zeros_like(dk_acc)
            dv_acc[...] = jnp.zeros_like(dv_acc)

        dk_acc[pl.ds(ws, KWIN), :] += _dot_tn(dsb, q)
        dv_acc[pl.ds(ws, KWIN), :] += _dot_tn(_mx(p), dob16)

        @pl.when(i == nq - 1)
        def _():
            dk_ref[...] = dk_acc[...].astype(dk_ref.dtype)
            dv_ref[...] = dv_acc[...].astype(dv_ref.dtype)

        @pl.when(i <= 2)
        def _():
            db_ref[0] = ds

        @pl.when(i > 2)
        def _():
            db_ref[0] += ds

    q, k, v, bm, tok = _att_specs(S)
    acc = pl.BlockSpec((S, HD), lambda h, i: (0, h))
    half = jax.ShapeDtypeStruct((S, D), MXU)
    return _call(
        body, name, (NH, nq), [q, k, v, bm, tok, tok, tok], [tok, acc, acc, bm],
        [half, half, half, jax.ShapeDtypeStruct((3 * NH, QB, KWIN), F32)],
        [pltpu.VMEM((S, HD), F32), pltpu.VMEM((S, HD), F32)],
    )(z, z, z, bmask, ob, lse, dob)


def _bias_fold(dbias, onehot, name):
    def body(db_ref, e_ref, o_ref):
        j = pl.program_id(0)
        h = j % NH
        x = jnp.concatenate([db_ref[0], jnp.zeros((QB, FW - KWIN), F32)], axis=1)
        half = QB // 2
        while half >= 8:
            x = x[:half] + pltpu.roll(x[half:2 * half], FW - half, 1)
            half //= 2
        df = jnp.zeros((1, FW), F32)
        for r in range(8):
            df = df + pltpu.roll(x[r:r + 1], 255 - r, 1)
        contrib = _dot(df, e_ref[0], HI)
        rowh = lax.broadcasted_iota(jnp.int32, (NH, 640), 0)

        @pl.when(j == 0)
        def _():
            o_ref[...] = jnp.zeros_like(o_ref)

        o_ref[...] += jnp.where(rowh == h, contrib, 0.0)

    return _call(
        body, name, (3 * NH,),
        [pl.BlockSpec((1, QB, KWIN), lambda j: (j, 0, 0)),
         pl.BlockSpec((1, FW, 640), lambda j: (j // NH, 0, 0))],
        _full((NH, 640)), jax.ShapeDtypeStruct((NH, 640), F32),
    )(dbias, onehot)


ADA_SHARD = 6 * D // NDEV


def _ada_mod(c_all, w_ada, b_shard, name):
    def body(c_ref, w_ref, b_ref, o_ref):
        cv = c_ref[...]
        ca = cv * _sigmoid(cv)
        o_ref[0] = _dot(_mx(ca), _mx(w_ref[0])) + b_ref[0]

    return _call(
        body, name, (DEPTH,),
        [_full((NDEV, D)), pl.BlockSpec((1, D, ADA_SHARD), lambda l: (l, 0, 0)),
         pl.BlockSpec((1, 1, ADA_SHARD), lambda l: (l, 0, 0))],
        pl.BlockSpec((1, NDEV, ADA_SHARD), lambda l: (l, 0, 0)),
        jax.ShapeDtypeStruct((DEPTH, NDEV, ADA_SHARD), F32),
    )(c_all, w_ada, b_shard.reshape(DEPTH, 1, ADA_SHARD))


def _adam(g, w, m, v):
    m = ADAM_B1 * m + (1.0 - ADAM_B1) * g
    v = ADAM_B2 * v + (1.0 - ADAM_B2) * jnp.square(g)
    m_hat = m / (1.0 - ADAM_B1 ** ADAM_STEP)
    v_hat = v / (1.0 - ADAM_B2 ** ADAM_STEP)
    delta = -ADAM_LR * (m_hat / (jnp.sqrt(v_hat) + ADAM_EPS) + ADAM_WD * w)
    return delta, m, v


def _wada_adamw(c_all_t, dmod, w, m, v, name):
    def body(c_ref, d_ref, w_ref, m_ref, v_ref, g_ref, dl_ref, mo_ref, vo_ref):
        cv = c_ref[...]
        ca = cv * _sigmoid(cv)
        g = _dot(ca, d_ref[0], HI)
        g_ref[0] = g
        dl_ref[0], mo_ref[0], vo_ref[0] = _adam(g, w_ref[0], m_ref[0], v_ref[0])

    blk = pl.BlockSpec((1, D, ADA_SHARD), lambda l: (l, 0, 0))
    shp = jax.ShapeDtypeStruct((DEPTH, D, ADA_SHARD), F32)
    return _call(
        body, name, (DEPTH,),
        [_full((D, NDEV)), pl.BlockSpec((1, NDEV, ADA_SHARD), lambda l: (l, 0, 0)), blk, blk, blk],
        [blk] * 4, [shp] * 4,
    )(c_all_t, dmod, w, m, v)


def _adamw_reduce(parts, w, m, v, name, tr):
    L = len(parts)
    P, RL, C = parts[0].shape
    nb = RL // tr

    def body(*refs):
        p_refs = refs[:L]
        w_ref, m_ref, v_ref, g_ref, dl_ref, mo_ref, vo_ref = refs[L:]
        layer = pl.program_id(0)
        for l in range(L):
            @pl.when(layer == l)
            def _():
                g = p_refs[l][0].astype(F32)
                for k in range(1, P):
                    g = g + p_refs[l][k].astype(F32)
                g_ref[...] = g
                dl_ref[...], mo_ref[...], vo_ref[...] = _adam(g, w_ref[...], m_ref[...], v_ref[...])

    def part_spec(l):
        return pl.BlockSpec((P, tr, C), lambda ll, i: (0, jnp.where(ll == l, i, 0), 0))

    blk = pl.BlockSpec((tr, C), lambda ll, i: (ll * nb + i, 0))
    shp = jax.ShapeDtypeStruct((L * RL, C), F32)
    return _call(body, name, (L, nb), [part_spec(l) for l in range(L)] + [blk, blk, blk],
                 [blk] * 4, [shp] * 4)(*parts, w, m, v)


def _sum_parts(parts, name):
    P, R, C = parts.shape

    def body(p_ref, o_ref):
        g = p_ref[0]
        for k in range(1, P):
            g = g + p_ref[k]
        o_ref[...] = g

    return _call(body, name, (1,), [_full((P, R, C))], _full((R, C)),
                 jax.ShapeDtypeStruct((R, C), F32))(parts)


def _pack_rows(vecs, width=1024):
    flat = jnp.concatenate([a.reshape(-1) for a in vecs])
    n = flat.shape[0]
    rows = -(-n // width)
    rows = -(-rows // 8) * 8
    return jnp.pad(flat, (0, rows * width - n)).reshape(rows, width)


def _unpack_rows(packed, shapes):
    flat = packed.reshape(-1)
    out, off = [], 0
    for s in shapes:
        n = int(np.prod(s)) if len(s) else 1
        out.append(flat[off:off + n].reshape(s))
        off += n
    return out


BIG = ("w_in", "w_out", "w_ff_in", "w_ff_out")


def _z_weights(g_in):
    w = jnp.transpose(g_in, (1, 0, 2)).reshape(D, IN_W)
    return jnp.concatenate([w[:, :Z_ATT], w[:, Z_ATT + 2 * NH:], w[:, Z_ATT:Z_ATT + 2 * NH],
                            jnp.zeros((D, ZW - IN_W), MXU)], axis=-1)


def _cols_to_owners(a):
    return jnp.transpose(a.reshape(a.shape[0], NDEV, -1), (1, 0, 2))


def _rows_to_owners(a):
    return a.reshape(NDEV, -1, a.shape[1])


def _forward_layer(x, mod_l, p, shard, next_w_in):
    sh1, sc1, gt1, sh2, sc2, gt2 = [mod_l[k][None] for k in range(6)]
    P = functools.partial
    gather = lambda a: None if a is None else ("gather", a)
    h, ht = _modnorm_fwd(x, p["norm_mix"], sc1, sh1, "norm_mix_fwd")
    z, g_in = _hosted(P(_mm_nn, h, p["wz"], "in_proj", tm=2048, tn=512), gather(next_w_in))
    (qn, kn, v, gb), g_w1 = _hosted(P(_gdn_prep_fwd, z, p["conv_w"], p["alog"], p["dtb"], "gdn_prep_fwd"),
                                    gather(shard["w_ff_in"]))
    (u, w, qd, kd, attn, tmat, cdt), g_w2 = _hosted(P(_gdn_intra_fwd, qn, kn, v, gb, "gdn_intra_fwd"),
                                                    gather(shard["w_ff_out"]))
    o, vn, st = _gdn_scan_fwd(u, w, qd, kd, attn, cdt, "gdn_scan_fwd")
    (ob, lse), g_out = _hosted(P(_att_fwd, z, p["bmask"], "att_fwd"), gather(shard["w_out"]))
    wout = g_out.reshape(D, D)
    w1 = jnp.transpose(g_w1, (1, 0, 2)).reshape(D, DFF)
    w2 = g_w2.reshape(DFF, D)
    m = _merge_fwd(o, z, ob, p["gdn_norm"], "merge_fwd")
    x1, h2 = _mm_nn(m, wout, "out_proj", mode="resid_norm", res=x, gate=gt1, norm=(p["norm_mlp"], sc2, sh2))
    a, r = _mm_nn(h2, w1, "ff_in", mode="relu2")
    x2 = _mm_nn(r, w2, "ff_out", mode="resid", res=x1, gate=gt2)
    saved = dict(x=x, ht=ht, z=z, qn=qn, kn=kn, v=v, gb=gb, u=u, w=w, qd=qd, kd=kd, attn=attn,
                 tmat=tmat, cdt=cdt, o=o, vn=vn, st=st, ob=ob, lse=lse, m=m, x1=x1, h2=h2, a=a, r=r,
                 wout=wout, w1=w1, w2=w2)
    return x2, saved, g_in


def _backward_layer(dx2, mod_l, p, s, onehot):
    sh1, sc1, gt1, sh2, sc2, gt2 = [mod_l[k][None] for k in range(6)]
    P = functools.partial
    dw2, dgt2 = _mm_tn(s["r"], dx2, "ff_out_dw", gate=gt2, w=s["w2"], out_dtype=MXU)
    da, r_w2 = _hosted(P(_mm_nt, dx2, s["w2"], "ff_out_dx", gate=gt2, drelu=s["a"]), ("a2a", _rows_to_owners(dw2)))
    dw1 = _mm_tn(s["h2"], da, "ff_in_dw", out_dtype=MXU)
    dh2, r_w1 = _hosted(P(_mm_nt, da, s["w1"], "ff_in_dx"), ("a2a", _cols_to_owners(dw1)))
    dx1, dsc2, dsh2, dnmlp = _modnorm_bwd(dh2, s["x1"], p["norm_mlp"], sc2, sh2, dx2, "norm_mlp_bwd")
    dwout, dgt1 = _mm_tn(s["m"], dx1, "out_proj_dw", gate=gt1, w=s["wout"], out_dtype=MXU)
    dm, r_out = _hosted(P(_mm_nt, dx1, s["wout"], "out_proj_dx", gate=gt1), ("a2a", _rows_to_owners(dwout)))
    do, dzg, dob, dza, dzb, dgn = _merge_bwd(dm, s["o"], s["z"], s["ob"], p["gdn_norm"], "merge_bwd")
    dq_att, dk_att, dv_att, dbias = _att_bwd(s["z"], p["bmask"], s["ob"], s["lse"], dob, "att_bwd")
    drb = _bias_fold(dbias, onehot, "rel_bias_fold")[:, :513]
    dqd, dkd, dvn, dw, dattn, dcdt = _gdn_scan_bwd(do, s["w"], s["qd"], s["kd"], s["attn"], s["cdt"],
                                                   s["vn"], s["st"], "gdn_scan_bwd")
    dqn, dkn, dv, dgb = _gdn_intra_bwd(s["qn"], s["kn"], s["v"], s["gb"], s["u"], s["w"], s["tmat"],
                                       dqd, dkd, dvn, dw, dattn, dcdt, "gdn_intra_bwd")
    dzq, dzab, dcw, dvec = _gdn_prep_bwd(s["z"], dqn, dkn, dv, dgb, p["conv_w"], p["alog"], p["dtb"],
                                         "gdn_prep_bwd")
    dz = (dzq, dzg, dq_att, dk_att, dv_att, dza, dzb, dzab)
    dwz = _in_proj_dw(s["ht"], dz, "in_proj_dw")
    dw_in = jnp.concatenate([dwz[:, :Z_ATT], dwz[:, Z_AB:Z_AB + 2 * NH], dwz[:, Z_ATT:Z_AB]], axis=1)
    dh, r_in = _hosted(P(_in_proj_dx, dz, p["wz"], "in_proj_dx"), ("a2a", _cols_to_owners(dw_in)))
    dx, dsc1, dsh1, dnmix = _modnorm_bwd(dh, s["x"], p["norm_mix"], sc1, sh1, dx1, "norm_mix_bwd")
    grads = dict(norm_mix=dnmix[0], norm_mlp=dnmlp[0], conv_w=dcw[:4], a_log=dvec[0, :NH], dt_bias=dvec[1, :NH],
                 gdn_norm=dgn[0], rel_bias=drb, mod=jnp.concatenate([dsh1, dsc1, dgt1, dsh2, dsc2, dgt2], axis=1)[0])
    return dx, grads, dict(w_in=r_in, w_out=r_out, w_ff_in=r_w1, w_ff_out=r_w2)


def _bias_onehot():
    return (jnp.asarray(_bias_index())[:, :, None] == jnp.arange(640)[None, None, :]).astype(F32)


def _layer_params(l, conv_full, norm_mix, norm_mlp, a_log, dt_bias, gdn_norm, rel_bias, onehot):
    pad = lambda a: jnp.pad(a, (0, 128 - NH))[None]
    fvec = _bias_vec(jnp.pad(rel_bias[l], ((0, 0), (0, 640 - rel_bias.shape[2]))), onehot, "rel_bias_vec")
    return dict(conv_w=conv_full[l], norm_mix=norm_mix[l][None], norm_mlp=norm_mlp[l][None], alog=pad(a_log[l]),
                dtb=pad(dt_bias[l]), gdn_norm=gdn_norm[l][None], bmask=_bias_mask(fvec, "rel_bias_mask"))


def _local_step(x, target, mod, small, shards, final_norm, onehot):
    L = len(small)
    g_in = _all_gather(shards[0]["w_in"], "gather_w_in")
    saved, params = [], []
    for l in range(L):
        params.append({**small[l], "wz": _z_weights(g_in)})
        x, sv, g_in = _forward_layer(x, mod[l].reshape(6, D), params[l], shards[l],
                                     shards[l + 1]["w_in"] if l + 1 < L else None)
        saved.append(sv)
    loss, dx, dfn = _loss_head(x, target, final_norm[None], "loss_head")
    grads, recv = [None] * L, [None] * L
    for l in reversed(range(L)):
        dx, grads[l], recv[l] = _backward_layer(dx, mod[l].reshape(6, D), params[l], saved[l], onehot)
    return loss, dx, grads, dfn[0], recv


SMALL = ("b_ada", "norm_mix", "norm_mlp", "a_log", "dt_bias", "gdn_norm", "rel_bias", "final_norm")


def kernel(x, c, w_ada, b_ada, norm_mix, norm_mlp, w_in, conv_w, a_log, dt_bias, gdn_norm, rel_bias, w_out, w_ff_in, w_ff_out, final_norm, loss_target, m_w_ada, m_b_ada, m_norm_mix, m_norm_mlp, m_w_in, m_conv_w, m_a_log, m_dt_bias, m_gdn_norm, m_rel_bias, m_w_out, m_w_ff_in, m_w_ff_out, m_final_norm, v_w_ada, v_b_ada, v_norm_mix, v_norm_mlp, v_w_in, v_conv_w, v_a_log, v_dt_bias, v_gdn_norm, v_rel_bias, v_w_out, v_w_ff_in, v_w_ff_out, v_final_norm):
    W = dict(w_ada=w_ada, b_ada=b_ada, norm_mix=norm_mix, norm_mlp=norm_mlp, w_in=w_in, conv_w=conv_w,
             a_log=a_log, dt_bias=dt_bias, gdn_norm=gdn_norm, rel_bias=rel_bias, w_out=w_out,
             w_ff_in=w_ff_in, w_ff_out=w_ff_out, final_norm=final_norm)
    Mo = dict(w_ada=m_w_ada, b_ada=m_b_ada, norm_mix=m_norm_mix, norm_mlp=m_norm_mlp, w_in=m_w_in,
              conv_w=m_conv_w, a_log=m_a_log, dt_bias=m_dt_bias, gdn_norm=m_gdn_norm, rel_bias=m_rel_bias,
              w_out=m_w_out, w_ff_in=m_w_ff_in, w_ff_out=m_w_ff_out, final_norm=m_final_norm)
    Vo = dict(w_ada=v_w_ada, b_ada=v_b_ada, norm_mix=v_norm_mix, norm_mlp=v_norm_mlp, w_in=v_w_in,
              conv_w=v_conv_w, a_log=v_a_log, dt_bias=v_dt_bias, gdn_norm=v_gdn_norm, rel_bias=v_rel_bias,
              w_out=v_w_out, w_ff_in=v_w_ff_in, w_ff_out=v_w_ff_out, final_norm=v_final_norm)
    L = w_in.shape[0]
    me = _flat(_mesh_pos())
    cshard = conv_w.shape[2]

    small_in = _all_gather(_pack_rows([c, conv_w]), "gather_c_conv")
    c_all = small_in[:, 0, :]
    conv_full = small_in.reshape(NDEV, -1)[:, D:D + L * 4 * cshard].reshape(NDEV, L, 4, cshard)
    conv_full = jnp.transpose(conv_full, (1, 2, 0, 3)).reshape(L, 4, NDEV * cshard)

    b_shard = lax.dynamic_slice_in_dim(b_ada, me * ADA_SHARD, ADA_SHARD, axis=1)
    mod_all = _all_gather(_ada_mod(c_all, w_ada, b_shard, "ada_mod"), "gather_mod")
    mod = lax.dynamic_index_in_dim(mod_all, me, axis=2, keepdims=False)
    mod = jnp.transpose(mod, (1, 0, 2)).reshape(L, 6 * D)

    onehot = _bias_onehot()
    small = [_layer_params(l, conv_full, norm_mix, norm_mlp, a_log, dt_bias, gdn_norm, rel_bias, onehot)
             for l in range(L)]
    shards = [{n: W[n][l].astype(MXU) for n in BIG} for l in range(L)]
    loss, dx, grads, dfn, recv = _local_step(x[0], loss_target[0], mod, small, shards, final_norm, onehot)

    def stack(name):
        return jnp.stack([g[name] for g in grads])

    small_names = ("mod", "norm_mix", "norm_mlp", "a_log", "dt_bias", "gdn_norm", "rel_bias")
    small_parts = [stack(n) for n in small_names] + [dfn, stack("conv_w"), loss[0, 0:1]]
    small_shapes = [a.shape for a in small_parts]
    gathered = _all_gather(_pack_rows(small_parts), "gather_small_grads")
    total = _unpack_rows(_sum_parts(gathered, "sum_small_grads"), small_shapes)
    tot = dict(zip(small_names + ("final_norm", "conv_w", "loss"), total))
    tot["b_ada"] = tot.pop("mod")
    tot["conv_w"] = lax.dynamic_slice_in_dim(tot["conv_w"], me * cshard, cshard, axis=2)

    out_g, out_d, out_m, out_v = {}, {}, {}, {}
    names = SMALL + ("conv_w",)
    shapes = [W[n].shape for n in names]
    packed = [_pack_rows([src[n] for n in names]) for src in (tot, W, Mo, Vo)]
    res = _adamw_reduce([packed[0][None]], *packed[1:], "adamw_small", tr=8)
    for dst, arr in zip((out_g, out_d, out_m, out_v), res):
        dst.update(zip(names, _unpack_rows(arr, shapes)))

    dmod_all = gathered.reshape(NDEV, -1)[:, :L * 6 * D].reshape(NDEV, L, 6 * D)
    dmod_mine = jnp.transpose(lax.dynamic_slice_in_dim(dmod_all, me * ADA_SHARD, ADA_SHARD, axis=2), (1, 0, 2))
    res = _wada_adamw(jnp.transpose(c_all), dmod_mine, w_ada, m_w_ada, v_w_ada, "adamw_w_ada")
    for dst, arr in zip((out_g, out_d, out_m, out_v), res):
        dst["w_ada"] = arr

    for name, tr in (("w_in", 128), ("w_out", 128), ("w_ff_in", 256), ("w_ff_out", 128)):
        sh = W[name].shape
        rows = int(np.prod(sh[:-1]))
        flat = lambda a: a.reshape(rows, sh[-1])
        parts = [recv[l][name] for l in range(L)]
        res = _adamw_reduce(parts, flat(W[name]), flat(Mo[name]), flat(Vo[name]), "adamw_" + name, tr=tr)
        for dst, arr in zip((out_g, out_d, out_m, out_v), res):
            dst[name] = arr.reshape(sh)

    order = ("w_ada", "b_ada", "norm_mix", "norm_mlp", "w_in", "conv_w", "a_log", "dt_bias", "gdn_norm",
             "rel_bias", "w_out", "w_ff_in", "w_ff_out", "final_norm")
    return (tot["loss"].reshape(()), dx[None], *[out_g[n] for n in order], *[out_d[n] for n in order],
            *[out_m[n] for n in order], *[out_v[n] for n in order])
```

```python
import functools
import math

import numpy as np
import jax
import jax.numpy as jnp
from jax import lax
from jax.experimental import pallas as pl
from jax.experimental.pallas import tpu as pltpu

F32 = jnp.float32
MXU = jnp.bfloat16
HI = lax.Precision.HIGHEST
MESH_ID = pl.DeviceIdType.MESH

D = 1024
NH = 8
HD = 128
CH = 64
PAST = 8
DFF = 4096
EPS = 1e-6
NDEV = 8
DEPTH = 4
IN_W = 9232
ZW = 9728
Z_GATE, Z_ATT, Z_BR, Z_AB = 3072, 4096, 7168, 9216
QB = 256
MASKED = -1e30
KWIN = 768
FW = 1024
ADAM_LR, ADAM_B1, ADAM_B2, ADAM_EPS, ADAM_WD, ADAM_STEP = 0.001, 0.9, 0.999, 1e-08, 0.01, 10


def _dot(a, b, prec=None):
    return jnp.dot(a, b, preferred_element_type=F32, precision=prec)


def _dot_nt(a, b, prec=None):
    return lax.dot_general(a, b, (((1,), (1,)), ((), ())), preferred_element_type=F32, precision=prec)


def _dot_tn(a, b, prec=None):
    return lax.dot_general(a, b, (((0,), (0,)), ((), ())), preferred_element_type=F32, precision=prec)


def _mx(a):
    return a.astype(MXU)


def _sigmoid(x):
    return 0.5 * jnp.tanh(0.5 * x) + 0.5


def _softplus(x):
    return jnp.maximum(x, 0.0) + jnp.log(1.0 + jnp.exp(-jnp.abs(x)))


def _rowsum(x):
    return jnp.sum(x, axis=1, keepdims=True)


def _colsum(x):
    return jnp.sum(x, axis=0, keepdims=True)


def _call(body, name, grid, in_specs, out_specs, out_shape, scratch=(), comm=None):
    if comm is None:
        return pl.pallas_call(body, name=name, grid=grid, in_specs=in_specs, out_specs=out_specs,
                              out_shape=out_shape, scratch_shapes=list(scratch))
    kind, x = comm
    single = not isinstance(out_specs, (list, tuple))
    o_specs = [out_specs] if single else list(out_specs)
    o_shape = [out_shape] if single else list(out_shape)
    n_in, n_out, n_scr = len(in_specs), len(o_specs), len(scratch)
    c_shape = (NDEV,) + x.shape if kind == "gather" else x.shape

    def wrapped(*refs):
        ins, x_ref = refs[:n_in], refs[n_in]
        outs, c_ref = refs[n_in + 1:n_in + 1 + n_out], refs[n_in + 1 + n_out]
        scr = refs[n_in + 2 + n_out:n_in + 2 + n_out + n_scr]
        sems = refs[n_in + 2 + n_out + n_scr:]
        first = functools.reduce(jnp.logical_and, [pl.program_id(a) == 0 for a in range(len(grid))])
        last = functools.reduce(jnp.logical_and, [pl.program_id(a) == grid[a] - 1 for a in range(len(grid))])

        @pl.when(first)
        def _():
            _comm_start(*_comm_copies(kind, x_ref, c_ref, *sems))

        body(*ins, *outs, *scr)

        @pl.when(last)
        def _():
            _comm_wait(*_comm_copies(kind, x_ref, c_ref, *sems))

    any_spec = pl.BlockSpec(memory_space=pl.ANY)
    call = pl.pallas_call(
        wrapped, name=name, grid=grid, in_specs=list(in_specs) + [any_spec], out_specs=o_specs + [any_spec],
        out_shape=o_shape + [jax.ShapeDtypeStruct(c_shape, x.dtype)],
        scratch_shapes=list(scratch) + _comm_sems())

    def run(*args):
        res = call(*args, x)
        return (res[0] if single else list(res[:-1])), res[-1]

    return run


def _hosted(fn, comm):
    return (fn(), None) if comm is None else fn(comm=comm)


def _full(shape):
    n = len(shape)
    return pl.BlockSpec(shape, lambda *_: (0,) * n)


def _mesh_pos():
    return lax.axis_index("x"), lax.axis_index("y"), lax.axis_index("c")


def _peer(pos, k):
    x, y, c = pos
    return (x ^ ((k >> 2) & 1), y ^ ((k >> 1) & 1), c ^ (k & 1))


def _flat(pos):
    return 4 * pos[0] + 2 * pos[1] + pos[2]


def _comm_sems():
    return [pltpu.SemaphoreType.DMA((NDEV - 1,)), pltpu.SemaphoreType.DMA((NDEV - 1,)), pltpu.SemaphoreType.DMA]


def _comm_copies(kind, x_ref, out_ref, send_sems, recv_sems, local_sem):
    pos = _mesh_pos()
    me = _flat(pos)
    src = (lambda d: x_ref) if kind == "gather" else (lambda d: x_ref.at[d])
    mine = pltpu.make_async_copy(src(me), out_ref.at[me], local_sem)
    sends, recvs = [], []
    for k in range(1, NDEV):
        peer = _peer(pos, k)
        pid = _flat(peer)
        sems = dict(send_sem=send_sems.at[k - 1], recv_sem=recv_sems.at[k - 1], device_id=peer,
                    device_id_type=MESH_ID)
        sends.append(pltpu.make_async_remote_copy(src_ref=src(pid), dst_ref=out_ref.at[me], **sems))
        recvs.append(pltpu.make_async_remote_copy(src_ref=src(pid), dst_ref=out_ref.at[pid], **sems))
    return mine, sends, recvs


def _comm_start(mine, sends, recvs):
    mine.start()
    for cp in sends:
        cp.start()


def _comm_wait(mine, sends, recvs):
    for cp in recvs:
        cp.wait_recv()
    for cp in sends:
        cp.wait_send()
    mine.wait()


def _collective(kind, x, name):
    def body(x_ref, out_ref, *sems):
        copies = _comm_copies(kind, x_ref, out_ref, *sems)
        _comm_start(*copies)
        _comm_wait(*copies)

    shape = (NDEV,) + x.shape if kind == "gather" else x.shape
    return pl.pallas_call(
        body, name=name, out_shape=jax.ShapeDtypeStruct(shape, x.dtype),
        in_specs=[pl.BlockSpec(memory_space=pl.ANY)], out_specs=pl.BlockSpec(memory_space=pl.ANY),
        scratch_shapes=_comm_sems())(x)


def _all_gather(x, name):
    return _collective("gather", x, name)


def _mm_nn(a, w, name, *, mode="plain", res=None, gate=None, norm=None, tm=1024, tn=1024, tk=1024, comm=None):
    M, K = a.shape
    N = w.shape[1]
    tm, tk, tn = min(tm, M), min(tk, K), min(tn, N)
    nk = K // tk

    def body(*refs):
        refs = list(refs)
        acc = refs.pop() if nk > 1 else None
        if mode == "resid":
            a_ref, w_ref, res_ref, gate_ref, o_ref = refs
        elif mode == "resid_norm":
            a_ref, w_ref, res_ref, gate_ref, g_ref, sc_ref, sh_ref, o_ref, r_ref = refs
        elif mode == "relu2":
            a_ref, w_ref, o_ref, r_ref = refs
        else:
            a_ref, w_ref, o_ref = refs
        k = pl.program_id(2)
        part = _dot(_mx(a_ref[...]), w_ref[...])

        def finish(r):
            if mode == "resid":
                o_ref[...] = res_ref[...] + gate_ref[...] * r
            elif mode == "resid_norm":
                xv = res_ref[...] + gate_ref[...] * r
                o_ref[...] = xv
                rs = lax.rsqrt(jnp.mean(xv * xv, axis=1, keepdims=True) + EPS)
                r_ref[...] = ((xv * rs * g_ref[...]) * (1.0 + sc_ref[...]) + sh_ref[...]).astype(r_ref.dtype)
            elif mode == "relu2":
                o_ref[...] = r
                r_ref[...] = jnp.square(jnp.maximum(r, 0.0)).astype(r_ref.dtype)
            else:
                o_ref[...] = r

        if nk == 1:
            finish(part)
        else:
            @pl.when(k == 0)
            def _():
                acc[...] = part

            @pl.when((k > 0) & (k < nk - 1))
            def _():
                acc[...] += part

            @pl.when(k == nk - 1)
            def _():
                finish(acc[...] + part)

    in_specs = [pl.BlockSpec((tm, tk), lambda i, j, k: (i, k)),
                pl.BlockSpec((tk, tn), lambda i, j, k: (k, j))]
    args = [a, w]
    o_spec = pl.BlockSpec((tm, tn), lambda i, j, k: (i, j))
    out_specs, out_shape = o_spec, jax.ShapeDtypeStruct((M, N), F32)
    vec = pl.BlockSpec((1, tn), lambda i, j, k: (0, j))
    if mode == "resid":
        in_specs += [o_spec, vec]
        args += [res, gate]
    elif mode == "resid_norm":
        assert tn == N
        in_specs += [o_spec, vec, vec, vec, vec]
        args += [res, gate, *norm]
        out_specs = [o_spec, o_spec]
        out_shape = [jax.ShapeDtypeStruct((M, N), F32), jax.ShapeDtypeStruct((M, N), MXU)]
    elif mode == "relu2":
        out_specs = [o_spec, o_spec]
        out_shape = [jax.ShapeDtypeStruct((M, N), F32), jax.ShapeDtypeStruct((M, N), MXU)]
    return _call(body, name, (M // tm, N // tn, nk), in_specs, out_specs, out_shape,
                 [pltpu.VMEM((tm, tn), F32)] if nk > 1 else [], comm=comm)(*args)


def _mm_nt(a, w, name, *, gate=None, drelu=None, tm=1024, tko=1024, tn=1024, comm=None):
    M, N = a.shape
    K = w.shape[0]
    tm, tn, tko = min(tm, M), min(tn, N), min(tko, K)
    nn = N // tn

    def body(*refs):
        refs = list(refs)
        acc = refs.pop() if nn > 1 else None
        a_ref, w_ref = refs[:2]
        rest = refs[2:]
        gate_ref = rest.pop(0) if gate is not None else None
        pre_ref = rest.pop(0) if drelu is not None else None
        (o_ref,) = rest
        n = pl.program_id(2)
        av = a_ref[...]
        if gate_ref is not None:
            av = av * gate_ref[...]
        part = _dot_nt(_mx(av), w_ref[...])

        def finish(r):
            if pre_ref is not None:
                r = r * (2.0 * jnp.maximum(pre_ref[...], 0.0))
            o_ref[...] = r.astype(o_ref.dtype)

        if nn == 1:
            finish(part)
        else:
            @pl.when(n == 0)
            def _():
                acc[...] = part

            @pl.when((n > 0) & (n < nn - 1))
            def _():
                acc[...] += part

            @pl.when(n == nn - 1)
            def _():
                finish(acc[...] + part)

    in_specs = [pl.BlockSpec((tm, tn), lambda i, j, n: (i, n)),
                pl.BlockSpec((tko, tn), lambda i, j, n: (j, n))]
    args = [a, w]
    if gate is not None:
        in_specs.append(pl.BlockSpec((1, tn), lambda i, j, n: (0, n)))
        args.append(gate)
    o_spec = pl.BlockSpec((tm, tko), lambda i, j, n: (i, j))
    if drelu is not None:
        in_specs.append(o_spec)
        args.append(drelu)
    out_dtype = MXU if drelu is not None else F32
    return _call(body, name, (M // tm, K // tko, nn), in_specs, o_spec,
                 jax.ShapeDtypeStruct((M, K), out_dtype), [pltpu.VMEM((tm, tko), F32)] if nn > 1 else [],
                 comm=comm)(*args)


def _mm_tn(a, b, name, *, gate=None, w=None, tk=1024, tn=1024, tm=1024, out_dtype=F32, comm=None):
    M, K = a.shape
    N = b.shape[1]
    tm, tk, tn = min(tm, M), min(tk, K), min(tn, N)
    nm = M // tm
    gated = gate is not None

    def body(*refs):
        if gated:
            a_ref, b_ref, gate_ref, w_ref, o_ref, dg_ref, acc = refs
        else:
            a_ref, b_ref, o_ref, acc = refs
        kk = pl.program_id(1)
        m = pl.program_id(2)
        part = _dot_tn(_mx(a_ref[...]), _mx(b_ref[...]))

        @pl.when((m == 0) & (nm > 1))
        def _():
            acc[...] = part

        @pl.when((m > 0) & (m < nm - 1))
        def _():
            acc[...] += part

        if gated:
            @pl.when((m == 0) & (kk == 0))
            def _():
                dg_ref[...] = jnp.zeros_like(dg_ref)

        @pl.when(m == nm - 1)
        def _():
            r = acc[...] + part if nm > 1 else part
            if gated:
                o_ref[...] = (r * gate_ref[...]).astype(o_ref.dtype)
                dg_ref[...] += _colsum(r * w_ref[...].astype(F32))
            else:
                o_ref[...] = r.astype(o_ref.dtype)

    in_specs = [pl.BlockSpec((tm, tk), lambda j, k, m: (m, k)),
                pl.BlockSpec((tm, tn), lambda j, k, m: (m, j))]
    args = [a, b]
    o_spec = pl.BlockSpec((tk, tn), lambda j, k, m: (k, j))
    out_specs, out_shape = o_spec, jax.ShapeDtypeStruct((K, N), out_dtype)
    if gated:
        in_specs += [pl.BlockSpec((1, tn), lambda j, k, m: (0, j)), o_spec]
        args += [gate, w]
        out_specs = [o_spec, pl.BlockSpec((1, tn), lambda j, k, m: (0, j))]
        out_shape = [out_shape, jax.ShapeDtypeStruct((1, N), F32)]
    return _call(body, name, (N // tn, K // tk, nm), in_specs, out_specs, out_shape,
                 [pltpu.VMEM((tk, tn), F32)], comm=comm)(*args)


SEG_T = 512


def _seg_layout(segs):
    starts, t = [], 0
    for a in segs:
        starts.append(t)
        t += a.shape[1] // SEG_T
    return starts, t


def _seg_spec(tm, lo, hi, row_axis, col_axis):
    def index(*ids):
        col = ids[col_axis]
        act = (col >= lo) & (col < hi)
        return jnp.where(act, ids[row_axis], 0), jnp.where(act, col - lo, 0)

    return pl.BlockSpec((tm, SEG_T), index)


def _in_proj_dw(ht, segs, name, tm=1024):
    S = ht.shape[1]
    tm = min(tm, S)
    nm = S // tm
    starts, ntile = _seg_layout(segs)
    bounds = [(lo, lo + a.shape[1] // SEG_T) for lo, a in zip(starts, segs)]

    def body(*refs):
        h_ref, seg_refs, o_ref, acc = refs[0], refs[1:1 + len(segs)], refs[-2], refs[-1]
        j = pl.program_id(0)
        m = pl.program_id(1)
        for (lo, hi), b_ref in zip(bounds, seg_refs):
            @pl.when((j >= lo) & (j < hi))
            def _():
                part = _dot(h_ref[...], _mx(b_ref[...]))
                if nm == 1:
                    o_ref[...] = part.astype(o_ref.dtype)
                else:
                    @pl.when(m == 0)
                    def _():
                        acc[...] = part

                    @pl.when((m > 0) & (m < nm - 1))
                    def _():
                        acc[...] += part

                    @pl.when(m == nm - 1)
                    def _():
                        o_ref[...] = (acc[...] + part).astype(o_ref.dtype)

    return _call(
        body, name, (ntile, nm),
        [pl.BlockSpec((D, tm), lambda j, m: (0, m))] + [_seg_spec(tm, lo, hi, 1, 0) for lo, hi in bounds],
        pl.BlockSpec((D, SEG_T), lambda j, m: (0, j)), jax.ShapeDtypeStruct((D, ntile * SEG_T), MXU),
        [pltpu.VMEM((D, SEG_T), F32)])(ht, *segs)


def _in_proj_dx(segs, w, name, tm=1024, comm=None):
    S = segs[0].shape[0]
    tm = min(tm, S)
    starts, ntile = _seg_layout(segs)
    bounds = [(lo, lo + a.shape[1] // SEG_T) for lo, a in zip(starts, segs)]

    def body(*refs):
        seg_refs, w_ref, o_ref, acc = refs[:len(segs)], refs[-3], refs[-2], refs[-1]
        n = pl.program_id(1)
        for (lo, hi), a_ref in zip(bounds, seg_refs):
            @pl.when((n >= lo) & (n < hi))
            def _():
                part = _dot_nt(_mx(a_ref[...]), w_ref[...])

                @pl.when(n == 0)
                def _():
                    acc[...] = part

                @pl.when((n > 0) & (n < ntile - 1))
                def _():
                    acc[...] += part

                @pl.when(n == ntile - 1)
                def _():
                    o_ref[...] = acc[...] + part

    return _call(
        body, name, (S // tm, ntile),
        [_seg_spec(tm, lo, hi, 0, 1) for lo, hi in bounds] + [pl.BlockSpec((D, SEG_T), lambda i, n: (0, n))],
        pl.BlockSpec((tm, D), lambda i, n: (i, 0)), jax.ShapeDtypeStruct((S, D), F32),
        [pltpu.VMEM((tm, D), F32)], comm=comm)(*segs, w)


def _modnorm_fwd(x, gain, sc, sh, name, ts=512):
    S = x.shape[0]

    def body(x_ref, g_ref, sc_ref, sh_ref, h_ref, ht_ref):
        xv = x_ref[...]
        r = lax.rsqrt(jnp.mean(xv * xv, axis=1, keepdims=True) + EPS)
        h = (xv * r * g_ref[...]) * (1.0 + sc_ref[...]) + sh_ref[...]
        h_ref[...] = h.astype(h_ref.dtype)
        ht_ref[...] = h.T.astype(ht_ref.dtype)

    row = pl.BlockSpec((ts, D), lambda i: (i, 0))
    vec = pl.BlockSpec((1, D), lambda i: (0, 0))
    return _call(body, name, (S // ts,), [row, vec, vec, vec], [row, pl.BlockSpec((D, ts), lambda i: (0, i))],
                 [jax.ShapeDtypeStruct((S, D), MXU), jax.ShapeDtypeStruct((D, S), MXU)])(x, gain, sc, sh)


def _modnorm_bwd(dh, x, gain, sc, sh, dx_in, name, ts=512):
    S = x.shape[0]

    def body(dh_ref, x_ref, g_ref, sc_ref, sh_ref, dxin_ref, dx_ref, dsc_ref, dsh_ref, dg_ref):
        i = pl.program_id(0)
        xv = x_ref[...]
        dhv = dh_ref[...]
        g = g_ref[...]
        r = lax.rsqrt(jnp.mean(xv * xv, axis=1, keepdims=True) + EPS)
        xr = xv * r
        dn = dhv * (1.0 + sc_ref[...])
        u = dn * g
        dx_ref[...] = dxin_ref[...] + r * (u - xr * jnp.mean(xr * u, axis=1, keepdims=True))

        @pl.when(i == 0)
        def _():
            dsc_ref[...] = jnp.zeros_like(dsc_ref)
            dsh_ref[...] = jnp.zeros_like(dsh_ref)
            dg_ref[...] = jnp.zeros_like(dg_ref)

        dsc_ref[...] += _colsum(dhv * (xr * g))
        dsh_ref[...] += _colsum(dhv)
        dg_ref[...] += _colsum(dn * xr)

    row = pl.BlockSpec((ts, D), lambda i: (i, 0))
    vec = pl.BlockSpec((1, D), lambda i: (0, 0))
    vshape = jax.ShapeDtypeStruct((1, D), F32)
    return _call(body, name, (S // ts,), [row, row, vec, vec, vec, row], [row, vec, vec, vec],
                 [jax.ShapeDtypeStruct((S, D), F32), vshape, vshape, vshape])(dh, x, gain, sc, sh, dx_in)


def _loss_head(x, target, gain, name, ts=512):
    S = x.shape[0]

    def body(x_ref, t_ref, g_ref, loss_ref, dx_ref, dg_ref):
        i = pl.program_id(0)
        xv = x_ref[...]
        g = g_ref[...]
        r = lax.rsqrt(jnp.mean(xv * xv, axis=1, keepdims=True) + EPS)
        xr = xv * r
        e = xr * g - t_ref[...]
        dy = e * (1.0 / D)
        u = dy * g
        dx_ref[...] = r * (u - xr * jnp.mean(xr * u, axis=1, keepdims=True))

        @pl.when(i == 0)
        def _():
            loss_ref[...] = jnp.zeros_like(loss_ref)
            dg_ref[...] = jnp.zeros_like(dg_ref)

        part = 0.5 * jnp.sum(jnp.mean(e * e, axis=1, keepdims=True), axis=0, keepdims=True)
        loss_ref[...] += jnp.broadcast_to(part, loss_ref.shape)
        dg_ref[...] += _colsum(dy * xr)

    row = pl.BlockSpec((ts, D), lambda i: (i, 0))
    vec = pl.BlockSpec((1, D), lambda i: (0, 0))
    return _call(body, name, (S // ts,), [row, row, vec],
                 [pl.BlockSpec((1, 128), lambda i: (0, 0)), row, vec],
                 [jax.ShapeDtypeStruct((1, 128), F32), jax.ShapeDtypeStruct((S, D), F32),
                  jax.ShapeDtypeStruct((1, D), F32)])(x, target, gain)


def _merge_specs(ts):
    o_spec = pl.BlockSpec((NH, ts, HD), lambda i: (0, i, 0))
    zg = pl.BlockSpec((ts, D), lambda i: (i, Z_GATE // D))
    za = pl.BlockSpec((ts, D), lambda i: (i, Z_BR // D))
    zb = pl.BlockSpec((ts, D), lambda i: (i, Z_BR // D + 1))
    row = pl.BlockSpec((ts, D), lambda i: (i, 0))
    gn = pl.BlockSpec((1, HD), lambda i: (0, 0))
    return o_spec, zg, za, zb, row, gn


def _merge_fwd(o, z, ob, gn, name, ts=256):
    S = ob.shape[0]

    def body(o_ref, zg_ref, za_ref, zb_ref, ob_ref, gn_ref, m_ref):
        for h in range(NH):
            sl = slice(h * HD, (h + 1) * HD)
            oh = o_ref[h]
            r = lax.rsqrt(jnp.mean(oh * oh, axis=1, keepdims=True) + EPS)
            gate = zg_ref[:, sl]
            oa = (oh * r * gn_ref[...]) * (gate * _sigmoid(gate))
            m = _sigmoid(za_ref[:, sl]) * oa + _sigmoid(zb_ref[:, sl]) * ob_ref[:, sl]
            m_ref[:, sl] = m.astype(m_ref.dtype)

    o_spec, zg, za, zb, row, gns = _merge_specs(ts)
    return _call(body, name, (S // ts,), [o_spec, zg, za, zb, row, gns], row,
                 jax.ShapeDtypeStruct((S, D), MXU))(o, z, z, z, ob, gn)


def _merge_bwd(dm, o, z, ob, gn, name, ts=256):
    S = ob.shape[0]

    def body(dm_ref, o_ref, zg_ref, za_ref, zb_ref, ob_ref, gn_ref,
             do_ref, dzg_ref, dob_ref, dza_ref, dzb_ref, dgn_ref):
        i = pl.program_id(0)
        gn_v = gn_ref[...]
        dgn = jnp.zeros((1, HD), F32)
        for h in range(NH):
            sl = slice(h * HD, (h + 1) * HD)
            dmh = dm_ref[:, sl]
            oh = o_ref[h]
            r = lax.rsqrt(jnp.mean(oh * oh, axis=1, keepdims=True) + EPS)
            ohr = oh * r
            on = ohr * gn_v
            gate = zg_ref[:, sl]
            sg = _sigmoid(gate)
            silu = gate * sg
            oa = on * silu
            ga = _sigmoid(za_ref[:, sl])
            gb = _sigmoid(zb_ref[:, sl])
            obh = ob_ref[:, sl]
            doa = dmh * ga
            dob_ref[:, sl] = dmh * gb
            dza_ref[:, sl] = (dmh * oa * ga * (1.0 - ga)).astype(dza_ref.dtype)
            dzb_ref[:, sl] = (dmh * obh * gb * (1.0 - gb)).astype(dzb_ref.dtype)
            don = doa * silu
            dzg_ref[:, sl] = (doa * on * (sg * (1.0 + gate * (1.0 - sg)))).astype(dzg_ref.dtype)
            dgn = dgn + _colsum(don * ohr)
            u = don * gn_v
            do_ref[h] = r * (u - ohr * jnp.mean(ohr * u, axis=1, keepdims=True))

        @pl.when(i == 0)
        def _():
            dgn_ref[...] = jnp.zeros_like(dgn_ref)

        dgn_ref[...] += dgn

    o_spec, zg, za, zb, row, gns = _merge_specs(ts)
    return _call(
        body, name, (S // ts,), [row, o_spec, zg, za, zb, row, gns],
        [o_spec, row, row, row, row, gns],
        [jax.ShapeDtypeStruct((NH, S, HD), F32), jax.ShapeDtypeStruct((S, D), MXU),
         jax.ShapeDtypeStruct((S, D), F32), jax.ShapeDtypeStruct((S, D), MXU),
         jax.ShapeDtypeStruct((S, D), MXU), jax.ShapeDtypeStruct((1, HD), F32)],
    )(dm, o, z, z, z, ob, gn)


GROWS = 256
GCH = GROWS // CH


def _gdn_prep_fwd(z, conv_w, alog_row, dtb_row, name, comm=None):
    S = z.shape[0]
    ts = GROWS
    scale = HD ** -0.5

    def body(z_ref, halo_ref, zab_ref, w_ref, al_ref, dt_ref, q_ref, k_ref, v_ref, gb_ref, buf):
        i = pl.program_id(0)
        buf[0:8, :] = jnp.where(i == 0, 0.0, halo_ref[...])
        buf[8:8 + ts, :] = z_ref[...]
        outs = (q_ref, k_ref, v_ref)
        for seg in range(3):
            cs = slice(seg * D, (seg + 1) * D)
            c = jnp.zeros((ts, D), F32)
            for j in range(4):
                c = c + w_ref[j:j + 1, cs] * buf[pl.ds(5 + j, ts), cs]
            s = c * _sigmoid(c)
            if seg == 2:
                outs[seg][...] = s
            else:
                mul = scale if seg == 0 else 1.0
                for h in range(NH):
                    sl = slice(h * HD, (h + 1) * HD)
                    sh = s[:, sl]
                    r = lax.rsqrt(_rowsum(sh * sh) + EPS)
                    outs[seg][:, sl] = sh * (r * mul)
        zab = zab_ref[...]
        lane = lax.broadcasted_iota(jnp.int32, zab.shape, 1)
        g = -jnp.exp(al_ref[...]) * _softplus(zab + dt_ref[...])
        ri = lax.broadcasted_iota(jnp.int32, (CH, CH), 0)
        ci = lax.broadcasted_iota(jnp.int32, (CH, CH), 1)
        incl = (ri >= ci).astype(F32)
        gcum = jnp.concatenate([_dot(incl, g[c * CH:(c + 1) * CH], HI) for c in range(ts // CH)], axis=0)
        gb_ref[...] = jnp.where(lane < NH, gcum, jnp.where(lane < 2 * NH, _sigmoid(zab), 0.0))

    row = pl.BlockSpec((ts, D), lambda i: (i, 0))
    vec = pl.BlockSpec((1, 128), lambda i: (0, 0))
    return _call(
        body, name, (S // ts,),
        [pl.BlockSpec((ts, 3 * D), lambda i: (i, 0)),
         pl.BlockSpec((8, 3 * D), lambda i: (jnp.maximum(i * (ts // 8) - 1, 0), 0)),
         pl.BlockSpec((ts, 128), lambda i: (i, Z_AB // 128)),
         _full((4, 3 * D)), vec, vec],
        [row, row, row, pl.BlockSpec((ts, 128), lambda i: (i, 0))],
        [jax.ShapeDtypeStruct((S, D), F32)] * 3 + [jax.ShapeDtypeStruct((S, 128), F32)],
        [pltpu.VMEM((ts + 8, 3 * D), F32)], comm=comm,
    )(z, z, z, conv_w, alog_row, dtb_row)


def _split(a):
    hi = a.astype(MXU)
    return hi, (a - hi.astype(F32)).astype(MXU)


def _dot3(a, b, dot=_dot):
    ah, al = _split(a)
    bh, bl = _split(b)
    return dot(ah, bh) + (dot(ah, bl) + dot(al, bh))


IROWS = 1024
ICH = IROWS // CH


def _chunk_common(gbk, h, k):
    lane = lax.broadcasted_iota(jnp.int32, gbk.shape, 1)
    G = _rowsum(jnp.where(lane == h, gbk, 0.0))
    b_col = _rowsum(jnp.where(lane == h + NH, gbk, 0.0))
    ri = lax.broadcasted_iota(jnp.int32, (CH, CH), 0)
    ci = lax.broadcasted_iota(jnp.int32, (CH, CH), 1)
    incl = ri >= ci
    gc = jnp.broadcast_to(G, (CH, CH))
    decay = jnp.where(incl, jnp.exp(jnp.where(incl, gc - gc.T, 0.0)), 0.0)
    Gl = G[CH - 1:CH, :]
    kb = k * b_col
    return dict(b=b_col, ri=ri, ci=ci, incl=incl, strict=ri > ci, decay=decay, eG=jnp.exp(G),
                e2=jnp.exp(Gl - G), cd=jnp.exp(Gl), kb=kb, kk=_dot_nt(_mx(kb), _mx(k)))


def _gdn_intra_fwd(qn, kn, v, gb, name, comm=None):
    S = qn.shape[0]

    def body(q_ref, k_ref, v_ref, gb_ref, u_ref, w_ref, qd_ref, kd_ref, at_ref, t_ref, cd_ref):
        h = pl.program_id(1)
        rows = [slice(c * CH, (c + 1) * CH) for c in range(ICH)]
        ks = [k_ref[r, :] for r in rows]
        cms = [_chunk_common(gb_ref[r, :], h, k) for r, k in zip(rows, ks)]
        ps = [jnp.where(cm["strict"], cm["kk"] * cm["decay"], 0.0) for cm in cms]
        ts = [(cm["ri"] == cm["ci"]).astype(F32) - p for cm, p in zip(cms, ps)]
        for _ in range(5):
            ps = [_dot3(p, p) for p in ps]
            ts = [t + _dot3(t, p) for t, p in zip(ts, ps)]
        for c, (r, k, cm, t) in enumerate(zip(rows, ks, cms, ts)):
            rhs = jnp.concatenate([v_ref[r, :] * cm["b"], k * (cm["b"] * cm["eG"])], axis=1)
            sol = _dot3(t, rhs)
            u_ref[0, r, :] = sol[:, :HD]
            w_ref[0, r, :] = sol[:, HD:]
            t_ref[0, r, :] = t
        for c, (r, k, cm) in enumerate(zip(rows, ks, cms)):
            q = q_ref[r, :]
            qk = _dot_nt(_mx(q), _mx(k))
            at_ref[0, r, :] = jnp.where(cm["incl"], qk * cm["decay"], 0.0)
            qd_ref[0, r, :] = q * cm["eG"]
            kd_ref[0, r, :] = k * cm["e2"]
            cd_ref[0, c] = jnp.broadcast_to(cm["cd"], (8, 128))

    tok = pl.BlockSpec((IROWS, HD), lambda i, h: (i, h))
    hm = pl.BlockSpec((1, IROWS, HD), lambda i, h: (h, i, 0))
    hm64 = pl.BlockSpec((1, IROWS, CH), lambda i, h: (h, i, 0))
    big = jax.ShapeDtypeStruct((NH, S, HD), F32)
    sm = jax.ShapeDtypeStruct((NH, S, CH), F32)
    return _call(
        body, name, (S // IROWS, NH),
        [tok, tok, tok, pl.BlockSpec((IROWS, 128), lambda i, h: (i, 0))],
        [hm, hm, hm, hm, hm64, hm64, pl.BlockSpec((1, ICH, 8, 128), lambda i, h: (h, i, 0, 0))],
        [big, big, big, big, sm, sm, jax.ShapeDtypeStruct((NH, S // CH, 8, 128), F32)], comm=comm,
    )(qn, kn, v, gb)


def _scale_state(s, cd_tile):
    return (s.reshape(HD // 8, 8, HD) * cd_tile[None]).reshape(HD, HD)


def _gdn_scan_fwd(u, w, qd, kd, attn, cdt, name):
    S = u.shape[1]
    nblk = S // GROWS

    def body(u_ref, w_ref, qd_ref, kd_ref, at_ref, cd_ref, o_ref, vn_ref, st_ref, s_ref):
        i = pl.program_id(0)

        @pl.when(i == 0)
        def _():
            s_ref[...] = jnp.zeros_like(s_ref)

        def chunk(c, carry):
            r0 = pl.multiple_of(c * CH, CH)
            rows = pl.ds(r0, CH)
            H = range(NH)
            shs = [s_ref[h] for h in H]
            sbs = [_mx(sh) for sh in shs]
            ws = [_dot(_mx(w_ref[h, rows, :]), sbs[h]) for h in H]
            qs = [_dot(_mx(qd_ref[h, rows, :]), sbs[h]) for h in H]
            vns = [u_ref[h, rows, :] - ws[h] for h in H]
            vbs = [_mx(vn) for vn in vns]
            avs = [_dot(_mx(at_ref[h, rows, :]), vbs[h]) for h in H]
            kvs = [_dot_tn(_mx(kd_ref[h, rows, :]), vbs[h]) for h in H]
            for h in H:
                st_ref[h, c] = shs[h]
                vn_ref[h, rows, :] = vns[h]
                o_ref[h, rows, :] = qs[h] + avs[h]
                s_ref[h] = _scale_state(shs[h], cd_ref[h, c]) + kvs[h]
            return carry

        lax.fori_loop(0, GCH, chunk, 0)

    hm = pl.BlockSpec((NH, GROWS, HD), lambda i: (0, i, 0))
    hm64 = pl.BlockSpec((NH, GROWS, CH), lambda i: (0, i, 0))
    big = jax.ShapeDtypeStruct((NH, S, HD), F32)
    return _call(
        body, name, (nblk,),
        [hm, hm, hm, hm, hm64, pl.BlockSpec((NH, GCH, 8, 128), lambda i: (0, i, 0, 0))],
        [hm, hm, pl.BlockSpec((NH, GCH, HD, HD), lambda i: (0, i, 0, 0))],
        [big, big, jax.ShapeDtypeStruct((NH, S // CH, HD, HD), F32)],
        [pltpu.VMEM((NH, HD, HD), F32)],
    )(u, w, qd, kd, attn, cdt)


def _gdn_scan_bwd(do, w, qd, kd, attn, cdt, vn, st, name):
    S = do.shape[1]
    nblk = S // GROWS

    def body(do_ref, w_ref, qd_ref, kd_ref, at_ref, cd_ref, vn_ref, st_ref,
             dqd_ref, dkd_ref, dvn_ref, dw_ref, dat_ref, dcd_ref, ds_ref):
        i = pl.program_id(0)

        @pl.when(i == 0)
        def _():
            ds_ref[...] = jnp.zeros_like(ds_ref)

        def chunk(cc, carry):
            c = GCH - 1 - cc
            r0 = pl.multiple_of(c * CH, CH)
            rows = pl.ds(r0, CH)
            H = range(NH)
            dsps = [ds_ref[h] for h in H]
            shs = [st_ref[h, c] for h in H]
            dsbs = [_mx(a) for a in dsps]
            sbs = [_mx(a) for a in shs]
            dobs = [_mx(do_ref[h, rows, :]) for h in H]
            vbs = [_mx(vn_ref[h, rows, :]) for h in H]
            dvns = [_dot(_mx(kd_ref[h, rows, :]), dsbs[h]) + _dot_tn(_mx(at_ref[h, rows, :]), dobs[h]) for h in H]
            dvbs = [_mx(a) for a in dvns]
            dqds = [_dot_nt(dobs[h], sbs[h]) for h in H]
            dats = [_dot_nt(dobs[h], vbs[h]) for h in H]
            dkds = [_dot_nt(vbs[h], dsbs[h]) for h in H]
            dws = [_dot_nt(dvbs[h], sbs[h]) for h in H]
            qdos = [_dot_tn(_mx(qd_ref[h, rows, :]), dobs[h]) for h in H]
            wdvs = [_dot_tn(_mx(w_ref[h, rows, :]), dvbs[h]) for h in H]
            for h in H:
                dvn_ref[h, rows, :] = dvns[h]
                dqd_ref[h, rows, :] = dqds[h]
                dat_ref[h, rows, :] = dats[h]
                dkd_ref[h, rows, :] = dkds[h]
                dw_ref[h, rows, :] = -dws[h]
                dcd = jnp.sum(_rowsum(dsps[h] * shs[h]), axis=0, keepdims=True)
                dcd_ref[h, c] = jnp.broadcast_to(dcd, (8, 128))
                ds_ref[h] = _scale_state(dsps[h], cd_ref[h, c]) + qdos[h] - wdvs[h]
            return carry

        lax.fori_loop(0, GCH, chunk, 0)

    hm = pl.BlockSpec((NH, GROWS, HD), lambda i: (0, nblk - 1 - i, 0))
    hm64 = pl.BlockSpec((NH, GROWS, CH), lambda i: (0, nblk - 1 - i, 0))
    tile = pl.BlockSpec((NH, GCH, 8, 128), lambda i: (0, nblk - 1 - i, 0, 0))
    big = jax.ShapeDtypeStruct((NH, S, HD), F32)
    return _call(
        body, name, (nblk,),
        [hm, hm, hm, hm, hm64, tile, hm, pl.BlockSpec((NH, GCH, HD, HD), lambda i: (0, nblk - 1 - i, 0, 0))],
        [hm, hm, hm, hm, hm64, tile],
        [big, big, big, big, jax.ShapeDtypeStruct((NH, S, CH), F32),
         jax.ShapeDtypeStruct((NH, S // CH, 8, 128), F32)],
        [pltpu.VMEM((NH, HD, HD), F32)],
    )(do, w, qd, kd, attn, cdt, vn, st)


def _gdn_intra_bwd(qn, kn, v, gb, u, w, tmat, dqd, dkd, du, dw, dattn, dcdt, name):
    S = qn.shape[0]

    def body(q_ref, k_ref, v_ref, gb_ref, u_ref, w_ref, t_ref, dqd_ref, dkd_ref, du_ref, dw_ref,
             dat_ref, dcd_ref, dq_ref, dk_ref, dv_ref, dgb_ref):
        h = pl.program_id(1)
        rows = [slice(c * CH, (c + 1) * CH) for c in range(ICH)]
        ks = [k_ref[r, :] for r in rows]
        cms = [_chunk_common(gb_ref[r, :], h, k) for r, k in zip(rows, ks)]
        sols = [jnp.concatenate([u_ref[0, r, :], w_ref[0, r, :]], axis=1) for r in rows]
        drhss = [_dot3(t_ref[0, r, :], jnp.concatenate([du_ref[0, r, :], dw_ref[0, r, :]], axis=1), _dot_tn)
                 for r in rows]
        das = [-_dot3(drhs, sol, _dot_nt) for drhs, sol in zip(drhss, sols)]
        for c, (r, k, cm, drhs, da) in enumerate(zip(rows, ks, cms, drhss, das)):
            q, vv = q_ref[r, :], v_ref[r, :]
            decay, eG, e2, b = cm["decay"], cm["eG"], cm["e2"], cm["b"]
            dru, drw = drhs[:, :HD], drhs[:, HD:]
            dv_ref[r, :] = dru * b
            s_w = _rowsum(drw * k)
            dbeta = _rowsum(dru * vv) + s_w * eG
            deg = s_w * b
            dk = drw * (b * eG)
            dkk = jnp.where(cm["strict"], da * decay, 0.0)
            ddec = jnp.where(cm["strict"], da * cm["kk"], 0.0)
            dkkb = _mx(dkk)
            dkb = _dot(dkkb, _mx(k))
            dk = dk + _dot_tn(dkkb, _mx(cm["kb"])) + dkb * b
            dbeta = dbeta + _rowsum(dkb * k)
            dat = jnp.where(cm["incl"], dat_ref[0, r, :], 0.0)
            qk = _dot_nt(_mx(q), _mx(k))
            dqk = _mx(dat * decay)
            ddec = ddec + dat * qk
            dqd = dqd_ref[0, r, :]
            dkd = dkd_ref[0, r, :]
            dq_ref[r, :] = _dot(dqk, _mx(k)) + dqd * eG
            dk_ref[r, :] = dk + _dot_tn(dqk, _mx(q)) + dkd * e2
            deg = deg + _rowsum(dqd * q)
            t2 = _rowsum(dkd * k) * e2
            dgl = jnp.sum(t2, axis=0, keepdims=True) + dcd_ref[0, c][0:1, 0:1] * cm["cd"]
            dd = ddec * decay
            dG = deg * eG - t2 + _rowsum(dd) - _rowsum(dd.T)
            rowi = lax.broadcasted_iota(jnp.int32, (CH, 1), 0)
            dG = dG + jnp.where(rowi == CH - 1, dgl, 0.0)
            lane = lax.broadcasted_iota(jnp.int32, (CH, 128), 1)
            dgb_ref[0, r, :] = jnp.where(lane == h, dG, 0.0) + jnp.where(lane == h + NH, dbeta, 0.0)

    tok = pl.BlockSpec((IROWS, HD), lambda i, h: (i, h))
    hm = pl.BlockSpec((1, IROWS, HD), lambda i, h: (h, i, 0))
    hm64 = pl.BlockSpec((1, IROWS, CH), lambda i, h: (h, i, 0))
    tile = pl.BlockSpec((1, ICH, 8, 128), lambda i, h: (h, i, 0, 0))
    tokout = jax.ShapeDtypeStruct((S, D), F32)
    return _call(
        body, name, (S // IROWS, NH),
        [tok, tok, tok, pl.BlockSpec((IROWS, 128), lambda i, h: (i, 0)), hm, hm, hm64,
         hm, hm, hm, hm, hm64, tile],
        [tok, tok, tok, pl.BlockSpec((1, IROWS, 128), lambda i, h: (h, i, 0))],
        [tokout, tokout, tokout, jax.ShapeDtypeStruct((NH, S, 128), F32)],
    )(qn, kn, v, gb, u, w, tmat, dqd, dkd, du, dw, dattn, dcdt)


def _gdn_prep_bwd(z, dqn, dkn, dv, dgb, conv_w, alog_row, dtb_row, name):
    S = z.shape[0]
    ts = GROWS
    nblk = S // ts
    scale = HD ** -0.5
    tb = ts // 8

    def body(z_ref, hp_ref, hn_ref, zab_ref, dq_ref, dqn_ref, dk_ref, dkn_ref, dv_ref, dvn_ref,
             dgb_ref, w_ref, al_ref, dt_ref, dz_ref, dzab_ref, dcw_ref, dvec_ref, buf, dybuf, dcbuf):
        i = pl.program_id(0)
        last = i == nblk - 1

        @pl.when(i == 0)
        def _():
            dcw_ref[...] = jnp.zeros_like(dcw_ref)
            dvec_ref[...] = jnp.zeros_like(dvec_ref)

        buf[0:8, :] = jnp.where(i == 0, 0.0, hp_ref[...])
        buf[8:8 + ts, :] = z_ref[...]
        buf[8 + ts:16 + ts, :] = hn_ref[...]
        rowi = lax.broadcasted_iota(jnp.int32, (ts + 8, 1), 0)
        live = jnp.logical_or(rowi < ts, jnp.logical_not(last))
        dys = ((dq_ref, dqn_ref), (dk_ref, dkn_ref), (dv_ref, dvn_ref))
        for seg in range(3):
            cs = slice(seg * D, (seg + 1) * D)
            dybuf[0:ts, :] = dys[seg][0][...]
            dybuf[ts:ts + 8, :] = dys[seg][1][...]
            taps = [buf[pl.ds(5 + j, ts + 8), cs] for j in range(4)]
            c = jnp.zeros((ts + 8, D), F32)
            for j in range(4):
                c = c + w_ref[j:j + 1, cs] * taps[j]
            sg = _sigmoid(c)
            s = c * sg
            dsilu = sg * (1.0 + c * (1.0 - sg))
            if seg == 2:
                dcbuf[...] = jnp.where(live, dybuf[...] * dsilu, 0.0)
            else:
                mul = scale if seg == 0 else 1.0
                for h in range(NH):
                    sl = slice(h * HD, (h + 1) * HD)
                    sh = s[:, sl]
                    dy = dybuf[:, sl]
                    r = lax.rsqrt(_rowsum(sh * sh) + EPS)
                    shr = sh * r
                    ds = (mul * r) * (dy - shr * _rowsum(shr * dy))
                    dcbuf[:, sl] = jnp.where(live, ds * dsilu[:, sl], 0.0)
            dx = jnp.zeros((ts, D), F32)
            for j in range(4):
                dcw_ref[j:j + 1, cs] += _colsum(dcbuf[0:ts, :] * taps[j][0:ts])
                dx = dx + w_ref[j:j + 1, cs] * dcbuf[pl.ds(3 - j, ts), :]
            dz_ref[:, cs] = dx.astype(dz_ref.dtype)
        dgbs = dgb_ref[0]
        for h in range(1, NH):
            dgbs = dgbs + dgb_ref[h]
        ri = lax.broadcasted_iota(jnp.int32, (CH, CH), 0)
        ci = lax.broadcasted_iota(jnp.int32, (CH, CH), 1)
        rev = (ci >= ri).astype(F32)
        dgrev = jnp.concatenate([_dot(rev, dgbs[c * CH:(c + 1) * CH], HI) for c in range(ts // CH)], axis=0)
        lane0 = lax.broadcasted_iota(jnp.int32, dgbs.shape, 1)
        dgbs = jnp.where(lane0 < NH, dgrev, dgbs)
        zab = zab_ref[...]
        lane = lax.broadcasted_iota(jnp.int32, zab.shape, 1)
        xx = zab + dt_ref[...]
        ea = jnp.exp(al_ref[...])
        g = -ea * _softplus(xx)
        da = dgbs * (-ea) * _sigmoid(xx)
        beta = _sigmoid(zab)
        db = dgbs * beta * (1.0 - beta)
        is_a = lane < NH
        dzab = jnp.where(is_a, da, jnp.where(lane < 2 * NH, db, 0.0))
        dzab_ref[:, 0:128] = dzab.astype(dzab_ref.dtype)
        dzab_ref[:, 128:512] = jnp.zeros((ts, 384), dzab_ref.dtype)
        dvec_ref[0:1, :] += _colsum(jnp.where(is_a, dgbs * g, 0.0))
        dvec_ref[1:2, :] += _colsum(jnp.where(is_a, da, 0.0))

    z3 = pl.BlockSpec((ts, 3 * D), lambda i: (i, 0))
    row = pl.BlockSpec((ts, D), lambda i: (i, 0))
    nxt = pl.BlockSpec((8, D), lambda i: (jnp.minimum((i + 1) * tb, S // 8 - 1), 0))
    vec = pl.BlockSpec((1, 128), lambda i: (0, 0))
    return _call(
        body, name, (nblk,),
        [z3,
         pl.BlockSpec((8, 3 * D), lambda i: (jnp.maximum(i * tb - 1, 0), 0)),
         pl.BlockSpec((8, 3 * D), lambda i: (jnp.minimum((i + 1) * tb, S // 8 - 1), 0)),
         pl.BlockSpec((ts, 128), lambda i: (i, Z_AB // 128)),
         row, nxt, row, nxt, row, nxt,
         pl.BlockSpec((NH, ts, 128), lambda i: (0, i, 0)),
         _full((4, 3 * D)), vec, vec],
        [z3, pl.BlockSpec((ts, 512), lambda i: (i, 0)), _full((8, 3 * D)), _full((8, 128))],
        [jax.ShapeDtypeStruct((S, 3 * D), MXU), jax.ShapeDtypeStruct((S, 512), MXU),
         jax.ShapeDtypeStruct((8, 3 * D), F32), jax.ShapeDtypeStruct((8, 128), F32)],
        [pltpu.VMEM((ts + 16, 3 * D), F32), pltpu.VMEM((ts + 8, D), F32), pltpu.VMEM((ts + 8, D), F32)],
    )(z, z, z, z, dqn, dqn, dkn, dkn, dv, dv, dgb, conv_w, alog_row, dtb_row)


def _bias_index():
    u = np.arange(FW)[None, :]
    s = np.arange(3)[:, None]
    return np.clip(KWIN - 1 - u - QB * s, -256, 256) + 256


def _bias_vec(rel_bias_pad, onehot, name):
    def body(rb_ref, e_ref, o_ref):
        o_ref[:, 0, :] = _dot_nt(rb_ref[...], e_ref[0], HI)

    return _call(body, name, (3,),
                 [_full((NH, 640)), pl.BlockSpec((1, FW, 640), lambda s: (s, 0, 0))],
                 pl.BlockSpec((NH, 1, FW), lambda s: (s, 0, 0)),
                 jax.ShapeDtypeStruct((3 * NH, 1, FW), F32))(rel_bias_pad, onehot)


def _att_window(i):
    return pl.multiple_of(jnp.maximum(i * QB - PAST * CH, 0), QB)


def _bias_mask(fvec, name):
    def body(f_ref, o_ref):
        i = 2 - pl.program_id(0) // NH
        ws = jnp.maximum(i * QB - PAST * CH, 0)
        fb = jnp.broadcast_to(f_ref[0], (QB, FW))
        bias = pltpu.roll(fb, FW - 255, 1, stride=1, stride_axis=0)[:, :KWIN]
        qc = (i * QB + lax.broadcasted_iota(jnp.int32, (QB, KWIN), 0)) // CH
        kc = (ws + lax.broadcasted_iota(jnp.int32, (QB, KWIN), 1)) // CH
        o_ref[0] = jnp.where((kc <= qc) & (kc >= qc - PAST), bias, MASKED)

    return _call(body, name, (3 * NH,), [pl.BlockSpec((1, 1, FW), lambda j: (j, 0, 0))],
                 pl.BlockSpec((1, QB, KWIN), lambda j: (j, 0, 0)),
                 jax.ShapeDtypeStruct((3 * NH, QB, KWIN), F32))(fvec)


def _att_scores(q_ref, k_ref, bm_ref, i):
    ws = _att_window(i)
    q = _mx(q_ref[...] * (HD ** -0.5))
    kw = _mx(k_ref[pl.ds(ws, KWIN), :])
    return q, kw, ws, _dot_nt(q, kw) + bm_ref[0]


def _att_specs(S):
    c0 = Z_ATT // HD
    q = pl.BlockSpec((QB, HD), lambda h, i: (i, c0 + h))
    k = pl.BlockSpec((S, HD), lambda h, i: (0, c0 + NH + h))
    v = pl.BlockSpec((S, HD), lambda h, i: (0, c0 + 2 * NH + h))
    bm = pl.BlockSpec((1, QB, KWIN), lambda h, i: (jnp.maximum(2 - i, 0) * NH + h, 0, 0))
    tok = pl.BlockSpec((QB, HD), lambda h, i: (i, h))
    return q, k, v, bm, tok


def _att_fwd(z, bmask, name, comm=None):
    S = z.shape[0]

    def body(q_ref, k_ref, v_ref, bm_ref, o_ref, lse_ref):
        _, _, ws, s = _att_scores(q_ref, k_ref, bm_ref, pl.program_id(1))
        m = jnp.max(s, axis=1, keepdims=True)
        p = jnp.exp(s - m)
        l = _rowsum(p)
        o_ref[...] = _dot(_mx(p), _mx(v_ref[pl.ds(ws, KWIN), :])) * (1.0 / l)
        lse_ref[...] = jnp.broadcast_to(m + jnp.log(l), (QB, HD))

    q, k, v, bm, tok = _att_specs(S)
    shp = jax.ShapeDtypeStruct((S, D), F32)
    return _call(body, name, (NH, S // QB), [q, k, v, bm], [tok, tok], [shp, shp], comm=comm)(z, z, z, bmask)


def _att_bwd(z, bmask, ob, lse, dob, name):
    S = z.shape[0]
    nq = S // QB

    def body(q_ref, k_ref, v_ref, bm_ref, o_ref, lse_ref, do_ref, dq_ref, dk_ref, dv_ref, db_ref, dk_acc, dv_acc):
        i = pl.program_id(1)
        q, kw, ws, s = _att_scores(q_ref, k_ref, bm_ref, i)
        p = jnp.exp(s - lse_ref[:, 0:1])
        do = do_ref[...]
        dob16 = _mx(do)
        dp = _dot_nt(dob16, _mx(v_ref[pl.ds(ws, KWIN), :]))
        ds = p * (dp - _rowsum(do * o_ref[...]))
        dsb = _mx(ds)
        dq_ref[...] = (_dot(dsb, kw) * (HD ** -0.5)).astype(dq_ref.dtype)

        @pl.when(i == 0)
        def _():
            dk_acc[...] = jnp.zeros_like(dk_acc)
            dv_acc[...] = jnp.zeros_like(dv_acc)

        dk_acc[pl.ds(ws, KWIN), :] += _dot_tn(dsb, q)
        dv_acc[pl.ds(ws, KWIN), :] += _dot_tn(_mx(p), dob16)

        @pl.when(i == nq - 1)
        def _():
            dk_ref[...] = dk_acc[...].astype(dk_ref.dtype)
            dv_ref[...] = dv_acc[...].astype(dv_ref.dtype)

        @pl.when(i <= 2)
        def _():
            db_ref[0] = ds

        @pl.when(i > 2)
        def _():
            db_ref[0] += ds

    q, k, v, bm, tok = _att_specs(S)
    acc = pl.BlockSpec((S, HD), lambda h, i: (0, h))
    half = jax.ShapeDtypeStruct((S, D), MXU)
    return _call(
        body, name, (NH, nq), [q, k, v, bm, tok, tok, tok], [tok, acc, acc, bm],
        [half, half, half, jax.ShapeDtypeStruct((3 * NH, QB, KWIN), F32)],
        [pltpu.VMEM((S, HD), F32), pltpu.VMEM((S, HD), F32)],
    )(z, z, z, bmask, ob, lse, dob)


def _bias_fold(dbias, onehot, name):
    def body(db_ref, e_ref, o_ref):
        j = pl.program_id(0)
        h = j % NH
        x = jnp.concatenate([db_ref[0], jnp.zeros((QB, FW - KWIN), F32)], axis=1)
        half = QB // 2
        while half >= 8:
            x = x[:half] + pltpu.roll(x[half:2 * half], FW - half, 1)
            half //= 2
        df = jnp.zeros((1, FW), F32)
        for r in range(8):
            df = df + pltpu.roll(x[r:r + 1], 255 - r, 1)
        contrib = _dot(df, e_ref[0], HI)
        rowh = lax.broadcasted_iota(jnp.int32, (NH, 640), 0)

        @pl.when(j == 0)
        def _():
            o_ref[...] = jnp.zeros_like(o_ref)

        o_ref[...] += jnp.where(rowh == h, contrib, 0.0)

    return _call(
        body, name, (3 * NH,),
        [pl.BlockSpec((1, QB, KWIN), lambda j: (j, 0, 0)),
         pl.BlockSpec((1, FW, 640), lambda j: (j // NH, 0, 0))],
        _full((NH, 640)), jax.ShapeDtypeStruct((NH, 640), F32),
    )(dbias, onehot)


ADA_SHARD = 6 * D // NDEV


def _ada_mod(c_all, w_ada, b_shard, name):
    def body(c_ref, w_ref, b_ref, o_ref):
        cv = c_ref[...]
        ca = cv * _sigmoid(cv)
        o_ref[0] = _dot(_mx(ca), _mx(w_ref[0])) + b_ref[0]

    return _call(
        body, name, (DEPTH,),
        [_full((NDEV, D)), pl.BlockSpec((1, D, ADA_SHARD), lambda l: (l, 0, 0)),
         pl.BlockSpec((1, 1, ADA_SHARD), lambda l: (l, 0, 0))],
        pl.BlockSpec((1, NDEV, ADA_SHARD), lambda l: (l, 0, 0)),
        jax.ShapeDtypeStruct((DEPTH, NDEV, ADA_SHARD), F32),
    )(c_all, w_ada, b_shard.reshape(DEPTH, 1, ADA_SHARD))


def _adam(g, w, m, v):
    m = ADAM_B1 * m + (1.0 - ADAM_B1) * g
    v = ADAM_B2 * v + (1.0 - ADAM_B2) * jnp.square(g)
    m_hat = m / (1.0 - ADAM_B1 ** ADAM_STEP)
    v_hat = v / (1.0 - ADAM_B2 ** ADAM_STEP)
    delta = -ADAM_LR * (m_hat / (jnp.sqrt(v_hat) + ADAM_EPS) + ADAM_WD * w)
    return delta, m, v


def _wada_adamw(c_all_t, dmod, w, m, v, name):
    def body(c_ref, d_ref, w_ref, m_ref, v_ref, g_ref, dl_ref, mo_ref, vo_ref):
        cv = c_ref[...]
        ca = cv * _sigmoid(cv)
        g = _dot(ca, d_ref[0], HI)
        g_ref[0] = g
        dl_ref[0], mo_ref[0], vo_ref[0] = _adam(g, w_ref[0], m_ref[0], v_ref[0])

    blk = pl.BlockSpec((1, D, ADA_SHARD), lambda l: (l, 0, 0))
    shp = jax.ShapeDtypeStruct((DEPTH, D, ADA_SHARD), F32)
    return _call(
        body, name, (DEPTH,),
        [_full((D, NDEV)), pl.BlockSpec((1, NDEV, ADA_SHARD), lambda l: (l, 0, 0)), blk, blk, blk],
        [blk] * 4, [shp] * 4,
    )(c_all_t, dmod, w, m, v)


def _adamw_reduce(parts, w, m, v, name, tr):
    L = len(parts)
    P, RL, C = parts[0].shape
    nb = RL // tr

    def body(*refs):
        p_refs = refs[:L]
        w_ref, m_ref, v_ref, g_ref, dl_ref, mo_ref, vo_ref = refs[L:]
        layer = pl.program_id(0)
        for l in range(L):
            @pl.when(layer == l)
            def _():
                g = p_refs[l][0].astype(F32)
                for k in range(1, P):
                    g = g + p_refs[l][k].astype(F32)
                g_ref[...] = g
                dl_ref[...], mo_ref[...], vo_ref[...] = _adam(g, w_ref[...], m_ref[...], v_ref[...])

    def part_spec(l):
        return pl.BlockSpec((P, tr, C), lambda ll, i: (0, jnp.where(ll == l, i, 0), 0))

    blk = pl.BlockSpec((tr, C), lambda ll, i: (ll * nb + i, 0))
    shp = jax.ShapeDtypeStruct((L * RL, C), F32)
    return _call(body, name, (L, nb), [part_spec(l) for l in range(L)] + [blk, blk, blk],
                 [blk] * 4, [shp] * 4)(*parts, w, m, v)


def _sum_parts(parts, name):
    P, R, C = parts.shape

    def body(p_ref, o_ref):
        g = p_ref[0]
        for k in range(1, P):
            g = g + p_ref[k]
        o_ref[...] = g

    return _call(body, name, (1,), [_full((P, R, C))], _full((R, C)),
                 jax.ShapeDtypeStruct((R, C), F32))(parts)


def _pack_rows(vecs, width=1024):
    flat = jnp.concatenate([a.reshape(-1) for a in vecs])
    n = flat.shape[0]
    rows = -(-n // width)
    rows = -(-rows // 8) * 8
    return jnp.pad(flat, (0, rows * width - n)).reshape(rows, width)


def _unpack_rows(packed, shapes):
    flat = packed.reshape(-1)
    out, off = [], 0
    for s in shapes:
        n = int(np.prod(s)) if len(s) else 1
        out.append(flat[off:off + n].reshape(s))
        off += n
    return out


BIG = ("w_in", "w_out", "w_ff_in", "w_ff_out")


def _z_weights(g_in):
    w = jnp.transpose(g_in, (1, 0, 2)).reshape(D, IN_W)
    return jnp.concatenate([w[:, :Z_ATT], w[:, Z_ATT + 2 * NH:], w[:, Z_ATT:Z_ATT + 2 * NH],
                            jnp.zeros((D, ZW - IN_W), MXU)], axis=-1)


def _cols_to_owners(a):
    return jnp.transpose(a.reshape(a.shape[0], NDEV, -1), (1, 0, 2))


def _rows_to_owners(a):
    return a.reshape(NDEV, -1, a.shape[1])


def _forward_layer(x, mod_l, p, shard, next_w_in):
    sh1, sc1, gt1, sh2, sc2, gt2 = [mod_l[k][None] for k in range(6)]
    P = functools.partial
    gather = lambda a: None if a is None else ("gather", a)
    h, ht = _modnorm_fwd(x, p["norm_mix"], sc1, sh1, "norm_mix_fwd")
    z, g_in = _hosted(P(_mm_nn, h, p["wz"], "in_proj", tm=2048, tn=512), gather(next_w_in))
    (qn, kn, v, gb), g_w1 = _hosted(P(_gdn_prep_fwd, z, p["conv_w"], p["alog"], p["dtb"], "gdn_prep_fwd"),
                                    gather(shard["w_ff_in"]))
    (u, w, qd, kd, attn, tmat, cdt), g_w2 = _hosted(P(_gdn_intra_fwd, qn, kn, v, gb, "gdn_intra_fwd"),
                                                    gather(shard["w_ff_out"]))
    o, vn, st = _gdn_scan_fwd(u, w, qd, kd, attn, cdt, "gdn_scan_fwd")
    (ob, lse), g_out = _hosted(P(_att_fwd, z, p["bmask"], "att_fwd"), gather(shard["w_out"]))
    wout = g_out.reshape(D, D)
    w1 = jnp.transpose(g_w1, (1, 0, 2)).reshape(D, DFF)
    w2 = g_w2.reshape(DFF, D)
    m = _merge_fwd(o, z, ob, p["gdn_norm"], "merge_fwd")
    x1, h2 = _mm_nn(m, wout, "out_proj", mode="resid_norm", res=x, gate=gt1, norm=(p["norm_mlp"], sc2, sh2))
    a, r = _mm_nn(h2, w1, "ff_in", mode="relu2")
    x2 = _mm_nn(r, w2, "ff_out", mode="resid", res=x1, gate=gt2)
    saved = dict(x=x, ht=ht, z=z, qn=qn, kn=kn, v=v, gb=gb, u=u, w=w, qd=qd, kd=kd, attn=attn,
                 tmat=tmat, cdt=cdt, o=o, vn=vn, st=st, ob=ob, lse=lse, m=m, x1=x1, h2=h2, a=a, r=r,
                 wout=wout, w1=w1, w2=w2)
    return x2, saved, g_in


def _backward_layer(dx2, mod_l, p, s, onehot):
    sh1, sc1, gt1, sh2, sc2, gt2 = [mod_l[k][None] for k in range(6)]
    P = functools.partial
    dw2, dgt2 = _mm_tn(s["r"], dx2, "ff_out_dw", gate=gt2, w=s["w2"], out_dtype=MXU)
    da, r_w2 = _hosted(P(_mm_nt, dx2, s["w2"], "ff_out_dx", gate=gt2, drelu=s["a"]), ("a2a", _rows_to_owners(dw2)))
    dw1 = _mm_tn(s["h2"], da, "ff_in_dw", out_dtype=MXU)
    dh2, r_w1 = _hosted(P(_mm_nt, da, s["w1"], "ff_in_dx"), ("a2a", _cols_to_owners(dw1)))
    dx1, dsc2, dsh2, dnmlp = _modnorm_bwd(dh2, s["x1"], p["norm_mlp"], sc2, sh2, dx2, "norm_mlp_bwd")
    dwout, dgt1 = _mm_tn(s["m"], dx1, "out_proj_dw", gate=gt1, w=s["wout"], out_dtype=MXU)
    dm, r_out = _hosted(P(_mm_nt, dx1, s["wout"], "out_proj_dx", gate=gt1), ("a2a", _rows_to_owners(dwout)))
    do, dzg, dob, dza, dzb, dgn = _merge_bwd(dm, s["o"], s["z"], s["ob"], p["gdn_norm"], "merge_bwd")
    dq_att, dk_att, dv_att, dbias = _att_bwd(s["z"], p["bmask"], s["ob"], s["lse"], dob, "att_bwd")
    drb = _bias_fold(dbias, onehot, "rel_bias_fold")[:, :513]
    dqd, dkd, dvn, dw, dattn, dcdt = _gdn_scan_bwd(do, s["w"], s["qd"], s["kd"], s["attn"], s["cdt"],
                                                   s["vn"], s["st"], "gdn_scan_bwd")
    dqn, dkn, dv, dgb = _gdn_intra_bwd(s["qn"], s["kn"], s["v"], s["gb"], s["u"], s["w"], s["tmat"],
                                       dqd, dkd, dvn, dw, dattn, dcdt, "gdn_intra_bwd")
    dzq, dzab, dcw, dvec = _gdn_prep_bwd(s["z"], dqn, dkn, dv, dgb, p["conv_w"], p["alog"], p["dtb"],
                                         "gdn_prep_bwd")
    dz = (dzq, dzg, dq_att, dk_att, dv_att, dza, dzb, dzab)
    dwz = _in_proj_dw(s["ht"], dz, "in_proj_dw")
    dw_in = jnp.concatenate([dwz[:, :Z_ATT], dwz[:, Z_AB:Z_AB + 2 * NH], dwz[:, Z_ATT:Z_AB]], axis=1)
    dh, r_in = _hosted(P(_in_proj_dx, dz, p["wz"], "in_proj_dx"), ("a2a", _cols_to_owners(dw_in)))
    dx, dsc1, dsh1, dnmix = _modnorm_bwd(dh, s["x"], p["norm_mix"], sc1, sh1, dx1, "norm_mix_bwd")
    grads = dict(norm_mix=dnmix[0], norm_mlp=dnmlp[0], conv_w=dcw[:4], a_log=dvec[0, :NH], dt_bias=dvec[1, :NH],
                 gdn_norm=dgn[0], rel_bias=drb, mod=jnp.concatenate([dsh1, dsc1, dgt1, dsh2, dsc2, dgt2], axis=1)[0])
    return dx, grads, dict(w_in=r_in, w_out=r_out, w_ff_in=r_w1, w_ff_out=r_w2)


def _bias_onehot():
    return (jnp.asarray(_bias_index())[:, :, None] == jnp.arange(640)[None, None, :]).astype(F32)


def _layer_params(l, conv_full, norm_mix, norm_mlp, a_log, dt_bias, gdn_norm, rel_bias, onehot):
    pad = lambda a: jnp.pad(a, (0, 128 - NH))[None]
    fvec = _bias_vec(jnp.pad(rel_bias[l], ((0, 0), (0, 640 - rel_bias.shape[2]))), onehot, "rel_bias_vec")
    return dict(conv_w=conv_full[l], norm_mix=norm_mix[l][None], norm_mlp=norm_mlp[l][None], alog=pad(a_log[l]),
                dtb=pad(dt_bias[l]), gdn_norm=gdn_norm[l][None], bmask=_bias_mask(fvec, "rel_bias_mask"))


def _local_step(x, target, mod, small, shards, final_norm, onehot):
    L = len(small)
    g_in = _all_gather(shards[0]["w_in"], "gather_w_in")
    saved, params = [], []
    for l in range(L):
        params.append({**small[l], "wz": _z_weights(g_in)})
        x, sv, g_in = _forward_layer(x, mod[l].reshape(6, D), params[l], shards[l],
                                     shards[l + 1]["w_in"] if l + 1 < L else None)
        saved.append(sv)
    loss, dx, dfn = _loss_head(x, target, final_norm[None], "loss_head")
    grads, recv = [None] * L, [None] * L
    for l in reversed(range(L)):
        dx, grads[l], recv[l] = _backward_layer(dx, mod[l].reshape(6, D), params[l], saved[l], onehot)
    return loss, dx, grads, dfn[0], recv


SMALL = ("b_ada", "norm_mix", "norm_mlp", "a_log", "dt_bias", "gdn_norm", "rel_bias", "final_norm")


def kernel(x, c, w_ada, b_ada, norm_mix, norm_mlp, w_in, conv_w, a_log, dt_bias, gdn_norm, rel_bias, w_out, w_ff_in, w_ff_out, final_norm, loss_target, m_w_ada, m_b_ada, m_norm_mix, m_norm_mlp, m_w_in, m_conv_w, m_a_log, m_dt_bias, m_gdn_norm, m_rel_bias, m_w_out, m_w_ff_in, m_w_ff_out, m_final_norm, v_w_ada, v_b_ada, v_norm_mix, v_norm_mlp, v_w_in, v_conv_w, v_a_log, v_dt_bias, v_gdn_norm, v_rel_bias, v_w_out, v_w_ff_in, v_w_ff_out, v_final_norm):
    W = dict(w_ada=w_ada, b_ada=b_ada, norm_mix=norm_mix, norm_mlp=norm_mlp, w_in=w_in, conv_w=conv_w,
             a_log=a_log, dt_bias=dt_bias, gdn_norm=gdn_norm, rel_bias=rel_bias, w_out=w_out,
             w_ff_in=w_ff_in, w_ff_out=w_ff_out, final_norm=final_norm)
    Mo = dict(w_ada=m_w_ada, b_ada=m_b_ada, norm_mix=m_norm_mix, norm_mlp=m_norm_mlp, w_in=m_w_in,
              conv_w=m_conv_w, a_log=m_a_log, dt_bias=m_dt_bias, gdn_norm=m_gdn_norm, rel_bias=m_rel_bias,
              w_out=m_w_out, w_ff_in=m_w_ff_in, w_ff_out=m_w_ff_out, final_norm=m_final_norm)
    Vo = dict(w_ada=v_w_ada, b_ada=v_b_ada, norm_mix=v_norm_mix, norm_mlp=v_norm_mlp, w_in=v_w_in,
              conv_w=v_conv_w, a_log=v_a_log, dt_bias=v_dt_bias, gdn_norm=v_gdn_norm, rel_bias=v_rel_bias,
              w_out=v_w_out, w_ff_in=v_w_ff_in, w_ff_out=v_w_ff_out, final_norm=v_final_norm)
    L = w_in.shape[0]
    me = _flat(_mesh_pos())
    cshard = conv_w.shape[2]

    small_in = _all_gather(_pack_rows([c, conv_w]), "gather_c_conv")
    c_all = small_in[:, 0, :]
    conv_full = small_in.reshape(NDEV, -1)[:, D:D + L * 4 * cshard].reshape(NDEV, L, 4, cshard)
    conv_full = jnp.transpose(conv_full, (1, 2, 0, 3)).reshape(L, 4, NDEV * cshard)

    b_shard = lax.dynamic_slice_in_dim(b_ada, me * ADA_SHARD, ADA_SHARD, axis=1)
    mod_all = _all_gather(_ada_mod(c_all, w_ada, b_shard, "ada_mod"), "gather_mod")
    mod = lax.dynamic_index_in_dim(mod_all, me, axis=2, keepdims=False)
    mod = jnp.transpose(mod, (1, 0, 2)).reshape(L, 6 * D)

    onehot = _bias_onehot()
    small = [_layer_params(l, conv_full, norm_mix, norm_mlp, a_log, dt_bias, gdn_norm, rel_bias, onehot)
             for l in range(L)]
    shards = [{n: W[n][l].astype(MXU) for n in BIG} for l in range(L)]
    loss, dx, grads, dfn, recv = _local_step(x[0], loss_target[0], mod, small, shards, final_norm, onehot)

    def stack(name):
        return jnp.stack([g[name] for g in grads])

    small_names = ("mod", "norm_mix", "norm_mlp", "a_log", "dt_bias", "gdn_norm", "rel_bias")
    small_parts = [stack(n) for n in small_names] + [dfn, stack("conv_w"), loss[0, 0:1]]
    small_shapes = [a.shape for a in small_parts]
    gathered = _all_gather(_pack_rows(small_parts), "gather_small_grads")
    total = _unpack_rows(_sum_parts(gathered, "sum_small_grads"), small_shapes)
    tot = dict(zip(small_names + ("final_norm", "conv_w", "loss"), total))
    tot["b_ada"] = tot.pop("mod")
    tot["conv_w"] = lax.dynamic_slice_in_dim(tot["conv_w"], me * cshard, cshard, axis=2)

    out_g, out_d, out_m, out_v = {}, {}, {}, {}
    names = SMALL + ("conv_w",)
    shapes = [W[n].shape for n in names]
    packed = [_pack_rows([src[n] for n in names]) for src in (tot, W, Mo, Vo)]
    res = _adamw_reduce([packed[0][None]], *packed[1:], "adamw_small", tr=8)
    for dst, arr in zip((out_g, out_d, out_m, out_v), res):
        dst.update(zip(names, _unpack_rows(arr, shapes)))

    dmod_all = gathered.reshape(NDEV, -1)[:, :L * 6 * D].reshape(NDEV, L, 6 * D)
    dmod_mine = jnp.transpose(lax.dynamic_slice_in_dim(dmod_all, me * ADA_SHARD, ADA_SHARD, axis=2), (1, 0, 2))
    res = _wada_adamw(jnp.transpose(c_all), dmod_mine, w_ada, m_w_ada, v_w_ada, "adamw_w_ada")
    for dst, arr in zip((out_g, out_d, out_m, out_v), res):
        dst["w_ada"] = arr

    for name, tr in (("w_in", 128), ("w_out", 128), ("w_ff_in", 256), ("w_ff_out", 128)):
        sh = W[name].shape
        rows = int(np.prod(sh[:-1]))
        flat = lambda a: a.reshape(rows, sh[-1])
        parts = [recv[l][name] for l in range(L)]
        res = _adamw_reduce(parts, flat(W[name]), flat(Mo[name]), flat(Vo[name]), "adamw_" + name, tr=tr)
        for dst, arr in zip((out_g, out_d, out_m, out_v), res):
            dst[name] = arr.reshape(sh)

    order = ("w_ada", "b_ada", "norm_mix", "norm_mlp", "w_in", "conv_w", "a_log", "dt_bias", "gdn_norm",
             "rel_bias", "w_out", "w_ff_in", "w_ff_out", "final_norm")
    return (tot["loss"].reshape(()), dx[None], *[out_g[n] for n in order], *[out_d[n] for n in order],
            *[out_m[n] for n in order], *[out_v[n] for n in order])
```

```python
import functools
import math

import numpy as np
import jax
import jax.numpy as jnp
from jax import lax
from jax.experimental import pallas as pl
from jax.experimental.pallas import tpu as pltpu

F32 = jnp.float32
MXU = jnp.bfloat16
HI = lax.Precision.HIGHEST
MESH_ID = pl.DeviceIdType.MESH

D = 1024
NH = 8
HD = 128
CH = 64
PAST = 8
DFF = 4096
EPS = 1e-6
NDEV = 8
DEPTH = 4
IN_W = 9232
ZW = 9728
Z_GATE, Z_ATT, Z_BR, Z_AB = 3072, 4096, 7168, 9216
QB = 256
MASKED = -1e30
KWIN = 768
FW = 1024
ADAM_LR, ADAM_B1, ADAM_B2, ADAM_EPS, ADAM_WD, ADAM_STEP = 0.001, 0.9, 0.999, 1e-08, 0.01, 10


def _dot(a, b, prec=None):
    return jnp.dot(a, b, preferred_element_type=F32, precision=prec)


def _dot_nt(a, b, prec=None):
    return lax.dot_general(a, b, (((1,), (1,)), ((), ())), preferred_element_type=F32, precision=prec)


def _dot_tn(a, b, prec=None):
    return lax.dot_general(a, b, (((0,), (0,)), ((), ())), preferred_element_type=F32, precision=prec)


def _mx(a):
    return a.astype(MXU)


def _sigmoid(x):
    return 0.5 * jnp.tanh(0.5 * x) + 0.5


def _softplus(x):
    return jnp.maximum(x, 0.0) + jnp.log(1.0 + jnp.exp(-jnp.abs(x)))


def _rowsum(x):
    return jnp.sum(x, axis=1, keepdims=True)


def _colsum(x):
    return jnp.sum(x, axis=0, keepdims=True)


def _call(body, name, grid, in_specs, out_specs, out_shape, scratch=(), comm=None):
    if comm is None:
        return pl.pallas_call(body, name=name, grid=grid, in_specs=in_specs, out_specs=out_specs,
                              out_shape=out_shape, scratch_shapes=list(scratch))
    kind, x = comm
    single = not isinstance(out_specs, (list, tuple))
    o_specs = [out_specs] if single else list(out_specs)
    o_shape = [out_shape] if single else list(out_shape)
    n_in, n_out, n_scr = len(in_specs), len(o_specs), len(scratch)
    c_shape = (NDEV,) + x.shape if kind == "gather" else x.shape

    def wrapped(*refs):
        ins, x_ref = refs[:n_in], refs[n_in]
        outs, c_ref = refs[n_in + 1:n_in + 1 + n_out], refs[n_in + 1 + n_out]
        scr = refs[n_in + 2 + n_out:n_in + 2 + n_out + n_scr]
        sems = refs[n_in + 2 + n_out + n_scr:]
        first = functools.reduce(jnp.logical_and, [pl.program_id(a) == 0 for a in range(len(grid))])
        last = functools.reduce(jnp.logical_and, [pl.program_id(a) == grid[a] - 1 for a in range(len(grid))])

        @pl.when(first)
        def _():
            _comm_start(*_comm_copies(kind, x_ref, c_ref, *sems))

        body(*ins, *outs, *scr)

        @pl.when(last)
        def _():
            _comm_wait(*_comm_copies(kind, x_ref, c_ref, *sems))

    any_spec = pl.BlockSpec(memory_space=pl.ANY)
    call = pl.pallas_call(
        wrapped, name=name, grid=grid, in_specs=list(in_specs) + [any_spec], out_specs=o_specs + [any_spec],
        out_shape=o_shape + [jax.ShapeDtypeStruct(c_shape, x.dtype)],
        scratch_shapes=list(scratch) + _comm_sems())

    def run(*args):
        res = call(*args, x)
        return (res[0] if single else list(res[:-1])), res[-1]

    return run


def _hosted(fn, comm):
    return (fn(), None) if comm is None else fn(comm=comm)


def _full(shape):
    n = len(shape)
    return pl.BlockSpec(shape, lambda *_: (0,) * n)


def _mesh_pos():
    return lax.axis_index("x"), lax.axis_index("y"), lax.axis_index("c")


def _peer(pos, k):
    x, y, c = pos
    return (x ^ ((k >> 2) & 1), y ^ ((k >> 1) & 1), c ^ (k & 1))


def _flat(pos):
    return 4 * pos[0] + 2 * pos[1] + pos[2]


def _comm_sems():
    return [pltpu.SemaphoreType.DMA((NDEV - 1,)), pltpu.SemaphoreType.DMA((NDEV - 1,)), pltpu.SemaphoreType.DMA]


def _comm_copies(kind, x_ref, out_ref, send_sems, recv_sems, local_sem):
    pos = _mesh_pos()
    me = _flat(pos)
    src = (lambda d: x_ref) if kind == "gather" else (lambda d: x_ref.at[d])
    mine = pltpu.make_async_copy(src(me), out_ref.at[me], local_sem)
    sends, recvs = [], []
    for k in range(1, NDEV):
        peer = _peer(pos, k)
        pid = _flat(peer)
        sems = dict(send_sem=send_sems.at[k - 1], recv_sem=recv_sems.at[k - 1], device_id=peer,
                    device_id_type=MESH_ID)
        sends.append(pltpu.make_async_remote_copy(src_ref=src(pid), dst_ref=out_ref.at[me], **sems))
        recvs.append(pltpu.make_async_remote_copy(src_ref=src(pid), dst_ref=out_ref.at[pid], **sems))
    return mine, sends, recvs


def _comm_start(mine, sends, recvs):
    mine.start()
    for cp in sends:
        cp.start()


def _comm_wait(mine, sends, recvs):
    for cp in recvs:
        cp.wait_recv()
    for cp in sends:
        cp.wait_send()
    mine.wait()


def _collective(kind, x, name):
    def body(x_ref, out_ref, *sems):
        copies = _comm_copies(kind, x_ref, out_ref, *sems)
        _comm_start(*copies)
        _comm_wait(*copies)

    shape = (NDEV,) + x.shape if kind == "gather" else x.shape
    return pl.pallas_call(
        body, name=name, out_shape=jax.ShapeDtypeStruct(shape, x.dtype),
        in_specs=[pl.BlockSpec(memory_space=pl.ANY)], out_specs=pl.BlockSpec(memory_space=pl.ANY),
        scratch_shapes=_comm_sems())(x)


def _all_gather(x, name):
    return _collective("gather", x, name)


def _mm_nn(a, w, name, *, mode="plain", res=None, gate=None, norm=None, tm=1024, tn=1024, tk=1024, comm=None):
    M, K = a.shape
    N = w.shape[1]
    tm, tk, tn = min(tm, M), min(tk, K), min(tn, N)
    nk = K // tk

    def body(*refs):
        refs = list(refs)
        acc = refs.pop() if nk > 1 else None
        if mode == "resid":
            a_ref, w_ref, res_ref, gate_ref, o_ref = refs
        elif mode == "resid_norm":
            a_ref, w_ref, res_ref, gate_ref, g_ref, sc_ref, sh_ref, o_ref, r_ref = refs
        elif mode == "relu2":
            a_ref, w_ref, o_ref, r_ref = refs
        else:
            a_ref, w_ref, o_ref = refs
        k = pl.program_id(2)
        part = _dot(_mx(a_ref[...]), w_ref[...])

        def finish(r):
            if mode == "resid":
                o_ref[...] = res_ref[...] + gate_ref[...] * r
            elif mode == "resid_norm":
                xv = res_ref[...] + gate_ref[...] * r
                o_ref[...] = xv
                rs = lax.rsqrt(jnp.mean(xv * xv, axis=1, keepdims=True) + EPS)
                r_ref[...] = ((xv * rs * g_ref[...]) * (1.0 + sc_ref[...]) + sh_ref[...]).astype(r_ref.dtype)
            elif mode == "relu2":
                o_ref[...] = r
                r_ref[...] = jnp.square(jnp.maximum(r, 0.0)).astype(r_ref.dtype)
            else:
                o_ref[...] = r

        if nk == 1:
            finish(part)
        else:
            @pl.when(k == 0)
            def _():
                acc[...] = part

            @pl.when((k > 0) & (k < nk - 1))
            def _():
                acc[...] += part

            @pl.when(k == nk - 1)
            def _():
                finish(acc[...] + part)

    in_specs = [pl.BlockSpec((tm, tk), lambda i, j, k: (i, k)),
                pl.BlockSpec((tk, tn), lambda i, j, k: (k, j))]
    args = [a, w]
    o_spec = pl.BlockSpec((tm, tn), lambda i, j, k: (i, j))
    out_specs, out_shape = o_spec, jax.ShapeDtypeStruct((M, N), F32)
    vec = pl.BlockSpec((1, tn), lambda i, j, k: (0, j))
    if mode == "resid":
        in_specs += [o_spec, vec]
        args += [res, gate]
    elif mode == "resid_norm":
        assert tn == N
        in_specs += [o_spec, vec, vec, vec, vec]
        args += [res, gate, *norm]
        out_specs = [o_spec, o_spec]
        out_shape = [jax.ShapeDtypeStruct((M, N), F32), jax.ShapeDtypeStruct((M, N), MXU)]
    elif mode == "relu2":
        out_specs = [o_spec, o_spec]
        out_shape = [jax.ShapeDtypeStruct((M, N), F32), jax.ShapeDtypeStruct((M, N), MXU)]
    return _call(body, name, (M // tm, N // tn, nk), in_specs, out_specs, out_shape,
                 [pltpu.VMEM((tm, tn), F32)] if nk > 1 else [], comm=comm)(*args)


def _mm_nt(a, w, name, *, gate=None, drelu=None, tm=1024, tko=1024, tn=1024, comm=None):
    M, N = a.shape
    K = w.shape[0]
    tm, tn, tko = min(tm, M), min(tn, N), min(tko, K)
    nn = N // tn

    def body(*refs):
        refs = list(refs)
        acc = refs.pop() if nn > 1 else None
        a_ref, w_ref = refs[:2]
        rest = refs[2:]
        gate_ref = rest.pop(0) if gate is not None else None
        pre_ref = rest.pop(0) if drelu is not None else None
        (o_ref,) = rest
        n = pl.program_id(2)
        av = a_ref[...]
        if gate_ref is not None:
            av = av * gate_ref[...]
        part = _dot_nt(_mx(av), w_ref[...])

        def finish(r):
            if pre_ref is not None:
                r = r * (2.0 * jnp.maximum(pre_ref[...], 0.0))
            o_ref[...] = r.astype(o_ref.dtype)

        if nn == 1:
            finish(part)
        else:
            @pl.when(n == 0)
            def _():
                acc[...] = part

            @pl.when((n > 0) & (n < nn - 1))
            def _():
                acc[...] += part

            @pl.when(n == nn - 1)
            def _():
                finish(acc[...] + part)

    in_specs = [pl.BlockSpec((tm, tn), lambda i, j, n: (i, n)),
                pl.BlockSpec((tko, tn), lambda i, j, n: (j, n))]
    args = [a, w]
    if gate is not None:
        in_specs.append(pl.BlockSpec((1, tn), lambda i, j, n: (0, n)))
        args.append(gate)
    o_spec = pl.BlockSpec((tm, tko), lambda i, j, n: (i, j))
    if drelu is not None:
        in_specs.append(o_spec)
        args.append(drelu)
    out_dtype = MXU if drelu is not None else F32
    return _call(body, name, (M // tm, K // tko, nn), in_specs, o_spec,
                 jax.ShapeDtypeStruct((M, K), out_dtype), [pltpu.VMEM((tm, tko), F32)] if nn > 1 else [],
                 comm=comm)(*args)


def _mm_tn(a, b, name, *, gate=None, w=None, tk=1024, tn=1024, tm=1024, out_dtype=F32, comm=None):
    M, K = a.shape
    N = b.shape[1]
    tm, tk, tn = min(tm, M), min(tk, K), min(tn, N)
    nm = M // tm
    gated = gate is not None

    def body(*refs):
        if gated:
            a_ref, b_ref, gate_ref, w_ref, o_ref, dg_ref, acc = refs
        else:
            a_ref, b_ref, o_ref, acc = refs
        kk = pl.program_id(1)
        m = pl.program_id(2)
        part = _dot_tn(_mx(a_ref[...]), _mx(b_ref[...]))

        @pl.when((m == 0) & (nm > 1))
        def _():
            acc[...] = part

        @pl.when((m > 0) & (m < nm - 1))
        def _():
            acc[...] += part

        if gated:
            @pl.when((m == 0) & (kk == 0))
            def _():
                dg_ref[...] = jnp.zeros_like(dg_ref)

        @pl.when(m == nm - 1)
        def _():
            r = acc[...] + part if nm > 1 else part
            if gated:
                o_ref[...] = (r * gate_ref[...]).astype(o_ref.dtype)
                dg_ref[...] += _colsum(r * w_ref[...].astype(F32))
            else:
                o_ref[...] = r.astype(o_ref.dtype)

    in_specs = [pl.BlockSpec((tm, tk), lambda j, k, m: (m, k)),
                pl.BlockSpec((tm, tn), lambda j, k, m: (m, j))]
    args = [a, b]
    o_spec = pl.BlockSpec((tk, tn), lambda j, k, m: (k, j))
    out_specs, out_shape = o_spec, jax.ShapeDtypeStruct((K, N), out_dtype)
    if gated:
        in_specs += [pl.BlockSpec((1, tn), lambda j, k, m: (0, j)), o_spec]
        args += [gate, w]
        out_specs = [o_spec, pl.BlockSpec((1, tn), lambda j, k, m: (0, j))]
        out_shape = [out_shape, jax.ShapeDtypeStruct((1, N), F32)]
    return _call(body, name, (N // tn, K // tk, nm), in_specs, out_specs, out_shape,
                 [pltpu.VMEM((tk, tn), F32)], comm=comm)(*args)


SEG_T = 512


def _seg_layout(segs):
    starts, t = [], 0
    for a in segs:
        starts.append(t)
        t += a.shape[1] // SEG_T
    return starts, t


def _seg_spec(tm, lo, hi, row_axis, col_axis):
    def index(*ids):
        col = ids[col_axis]
        act = (col >= lo) & (col < hi)
        return jnp.where(act, ids[row_axis], 0), jnp.where(act, col - lo, 0)

    return pl.BlockSpec((tm, SEG_T), index)


def _in_proj_dw(ht, segs, name, tm=1024):
    S = ht.shape[1]
    tm = min(tm, S)
    nm = S // tm
    starts, ntile = _seg_layout(segs)
    bounds = [(lo, lo + a.shape[1] // SEG_T) for lo, a in zip(starts, segs)]

    def body(*refs):
        h_ref, seg_refs, o_ref, acc = refs[0], refs[1:1 + len(segs)], refs[-2], refs[-1]
        j = pl.program_id(0)
        m = pl.program_id(1)
        for (lo, hi), b_ref in zip(bounds, seg_refs):
            @pl.when((j >= lo) & (j < hi))
            def _():
                part = _dot(h_ref[...], _mx(b_ref[...]))
                if nm == 1:
                    o_ref[...] = part.astype(o_ref.dtype)
                else:
                    @pl.when(m == 0)
                    def _():
                        acc[...] = part

                    @pl.when((m > 0) & (m < nm - 1))
                    def _():
                        acc[...] += part

                    @pl.when(m == nm - 1)
                    def _():
                        o_ref[...] = (acc[...] + part).astype(o_ref.dtype)

    return _call(
        body, name, (ntile, nm),
        [pl.BlockSpec((D, tm), lambda j, m: (0, m))] + [_seg_spec(tm, lo, hi, 1, 0) for lo, hi in bounds],
        pl.BlockSpec((D, SEG_T), lambda j, m: (0, j)), jax.ShapeDtypeStruct((D, ntile * SEG_T), MXU),
        [pltpu.VMEM((D, SEG_T), F32)])(ht, *segs)


def _in_proj_dx(segs, w, name, tm=1024, comm=None):
    S = segs[0].shape[0]
    tm = min(tm, S)
    starts, ntile = _seg_layout(segs)
    bounds = [(lo, lo + a.shape[1] // SEG_T) for lo, a in zip(starts, segs)]

    def body(*refs):
        seg_refs, w_ref, o_ref, acc = refs[:len(segs)], refs[-3], refs[-2], refs[-1]
        n = pl.program_id(1)
        for (lo, hi), a_ref in zip(bounds, seg_refs):
            @pl.when((n >= lo) & (n < hi))
            def _():
                part = _dot_nt(_mx(a_ref[...]), w_ref[...])

                @pl.when(n == 0)
                def _():
                    acc[...] = part

                @pl.when((n > 0) & (n < ntile - 1))
                def _():
                    acc[...] += part

                @pl.when(n == ntile - 1)
                def _():
                    o_ref[...] = acc[...] + part

    return _call(
        body, name, (S // tm, ntile),
        [_seg_spec(tm, lo, hi, 0, 1) for lo, hi in bounds] + [pl.BlockSpec((D, SEG_T), lambda i, n: (0, n))],
        pl.BlockSpec((tm, D), lambda i, n: (i, 0)), jax.ShapeDtypeStruct((S, D), F32),
        [pltpu.VMEM((tm, D), F32)], comm=comm)(*segs, w)


def _modnorm_fwd(x, gain, sc, sh, name, ts=512):
    S = x.shape[0]

    def body(x_ref, g_ref, sc_ref, sh_ref, h_ref, ht_ref):
        xv = x_ref[...]
        r = lax.rsqrt(jnp.mean(xv * xv, axis=1, keepdims=True) + EPS)
        h = (xv * r * g_ref[...]) * (1.0 + sc_ref[...]) + sh_ref[...]
        h_ref[...] = h.astype(h_ref.dtype)
        ht_ref[...] = h.T.astype(ht_ref.dtype)

    row = pl.BlockSpec((ts, D), lambda i: (i, 0))
    vec = pl.BlockSpec((1, D), lambda i: (0, 0))
    return _call(body, name, (S // ts,), [row, vec, vec, vec], [row, pl.BlockSpec((D, ts), lambda i: (0, i))],
                 [jax.ShapeDtypeStruct((S, D), MXU), jax.ShapeDtypeStruct((D, S), MXU)])(x, gain, sc, sh)


def _modnorm_bwd(dh, x, gain, sc, sh, dx_in, name, ts=512):
    S = x.shape[0]

    def body(dh_ref, x_ref, g_ref, sc_ref, sh_ref, dxin_ref, dx_ref, dsc_ref, dsh_ref, dg_ref):
        i = pl.program_id(0)
        xv = x_ref[...]
        dhv = dh_ref[...]
        g = g_ref[...]
        r = lax.rsqrt(jnp.mean(xv * xv, axis=1, keepdims=True) + EPS)
        xr = xv * r
        dn = dhv * (1.0 + sc_ref[...])
        u = dn * g
        dx_ref[...] = dxin_ref[...] + r * (u - xr * jnp.mean(xr * u, axis=1, keepdims=True))

        @pl.when(i == 0)
        def _():
            dsc_ref[...] = jnp.zeros_like(dsc_ref)
            dsh_ref[...] = jnp.zeros_like(dsh_ref)
            dg_ref[...] = jnp.zeros_like(dg_ref)

        dsc_ref[...] += _colsum(dhv * (xr * g))
        dsh_ref[...] += _colsum(dhv)
        dg_ref[...] += _colsum(dn * xr)

    row = pl.BlockSpec((ts, D), lambda i: (i, 0))
    vec = pl.BlockSpec((1, D), lambda i: (0, 0))
    vshape = jax.ShapeDtypeStruct((1, D), F32)
    return _call(body, name, (S // ts,), [row, row, vec, vec, vec, row], [row, vec, vec, vec],
                 [jax.ShapeDtypeStruct((S, D), F32), vshape, vshape, vshape])(dh, x, gain, sc, sh, dx_in)


def _loss_head(x, target, gain, name, ts=512):
    S = x.shape[0]

    def body(x_ref, t_ref, g_ref, loss_ref, dx_ref, dg_ref):
        i = pl.program_id(0)
        xv = x_ref[...]
        g = g_ref[...]
        r = lax.rsqrt(jnp.mean(xv * xv, axis=1, keepdims=True) + EPS)
        xr = xv * r
        e = xr * g - t_ref[...]
        dy = e * (1.0 / D)
        u = dy * g
        dx_ref[...] = r * (u - xr * jnp.mean(xr * u, axis=1, keepdims=True))

        @pl.when(i == 0)
        def _():
            loss_ref[...] = jnp.zeros_like(loss_ref)
            dg_ref[...] = jnp.zeros_like(dg_ref)

        part = 0.5 * jnp.sum(jnp.mean(e * e, axis=1, keepdims=True), axis=0, keepdims=True)
        loss_ref[...] += jnp.broadcast_to(part, loss_ref.shape)
        dg_ref[...] += _colsum(dy * xr)

    row = pl.BlockSpec((ts, D), lambda i: (i, 0))
    vec = pl.BlockSpec((1, D), lambda i: (0, 0))
    return _call(body, name, (S // ts,), [row, row, vec],
                 [pl.BlockSpec((1, 128), lambda i: (0, 0)), row, vec],
                 [jax.ShapeDtypeStruct((1, 128), F32), jax.ShapeDtypeStruct((S, D), F32),
                  jax.ShapeDtypeStruct((1, D), F32)])(x, target, gain)


def _merge_specs(ts):
    o_spec = pl.BlockSpec((NH, ts, HD), lambda i: (0, i, 0))
    zg = pl.BlockSpec((ts, D), lambda i: (i, Z_GATE // D))
    za = pl.BlockSpec((ts, D), lambda i: (i, Z_BR // D))
    zb = pl.BlockSpec((ts, D), lambda i: (i, Z_BR // D + 1))
    row = pl.BlockSpec((ts, D), lambda i: (i, 0))
    gn = pl.BlockSpec((1, HD), lambda i: (0, 0))
    return o_spec, zg, za, zb, row, gn


def _merge_fwd(o, z, ob, gn, name, ts=256):
    S = ob.shape[0]

    def body(o_ref, zg_ref, za_ref, zb_ref, ob_ref, gn_ref, m_ref):
        for h in range(NH):
            sl = slice(h * HD, (h + 1) * HD)
            oh = o_ref[h]
            r = lax.rsqrt(jnp.mean(oh * oh, axis=1, keepdims=True) + EPS)
            gate = zg_ref[:, sl]
            oa = (oh * r * gn_ref[...]) * (gate * _sigmoid(gate))
            m = _sigmoid(za_ref[:, sl]) * oa + _sigmoid(zb_ref[:, sl]) * ob_ref[:, sl]
            m_ref[:, sl] = m.astype(m_ref.dtype)

    o_spec, zg, za, zb, row, gns = _merge_specs(ts)
    return _call(body, name, (S // ts,), [o_spec, zg, za, zb, row, gns], row,
                 jax.ShapeDtypeStruct((S, D), MXU))(o, z, z, z, ob, gn)


def _merge_bwd(dm, o, z, ob, gn, name, ts=256):
    S = ob.shape[0]

    def body(dm_ref, o_ref, zg_ref, za_ref, zb_ref, ob_ref, gn_ref,
             do_ref, dzg_ref, dob_ref, dza_ref, dzb_ref, dgn_ref):
        i = pl.program_id(0)
        gn_v = gn_ref[...]
        dgn = jnp.zeros((1, HD), F32)
        for h in range(NH):
            sl = slice(h * HD, (h + 1) * HD)
            dmh = dm_ref[:, sl]
            oh = o_ref[h]
            r = lax.rsqrt(jnp.mean(oh * oh, axis=1, keepdims=True) + EPS)
            ohr = oh * r
            on = ohr * gn_v
            gate = zg_ref[:, sl]
            sg = _sigmoid(gate)
            silu = gate * sg
            oa = on * silu
            ga = _sigmoid(za_ref[:, sl])
            gb = _sigmoid(zb_ref[:, sl])
            obh = ob_ref[:, sl]
            doa = dmh * ga
            dob_ref[:, sl] = dmh * gb
            dza_ref[:, sl] = (dmh * oa * ga * (1.0 - ga)).astype(dza_ref.dtype)
            dzb_ref[:, sl] = (dmh * obh * gb * (1.0 - gb)).astype(dzb_ref.dtype)
            don = doa * silu
            dzg_ref[:, sl] = (doa * on * (sg * (1.0 + gate * (1.0 - sg)))).astype(dzg_ref.dtype)
            dgn = dgn + _colsum(don * ohr)
            u = don * gn_v
            do_ref[h] = r * (u - ohr * jnp.mean(ohr * u, axis=1, keepdims=True))

        @pl.when(i == 0)
        def _():
            dgn_ref[...] = jnp.zeros_like(dgn_ref)

        dgn_ref[...] += dgn

    o_spec, zg, za, zb, row, gns = _merge_specs(ts)
    return _call(
        body, name, (S // ts,), [row, o_spec, zg, za, zb, row, gns],
        [o_spec, row, row, row, row, gns],
        [jax.ShapeDtypeStruct((NH, S, HD), F32), jax.ShapeDtypeStruct((S, D), MXU),
         jax.ShapeDtypeStruct((S, D), F32), jax.ShapeDtypeStruct((S, D), MXU),
         jax.ShapeDtypeStruct((S, D), MXU), jax.ShapeDtypeStruct((1, HD), F32)],
    )(dm, o, z, z, z, ob, gn)


GROWS = 256
GCH = GROWS // CH


def _gdn_prep_fwd(z, conv_w, alog_row, dtb_row, name, comm=None):
    S = z.shape[0]
    ts = GROWS
    scale = HD ** -0.5

    def body(z_ref, halo_ref, zab_ref, w_ref, al_ref, dt_ref, q_ref, k_ref, v_ref, gb_ref, buf):
        i = pl.program_id(0)
        buf[0:8, :] = jnp.where(i == 0, 0.0, halo_ref[...])
        buf[8:8 + ts, :] = z_ref[...]
        outs = (q_ref, k_ref, v_ref)
        for seg in range(3):
            cs = slice(seg * D, (seg + 1) * D)
            c = jnp.zeros((ts, D), F32)
            for j in range(4):
                c = c + w_ref[j:j + 1, cs] * buf[pl.ds(5 + j, ts), cs]
            s = c * _sigmoid(c)
            if seg == 2:
                outs[seg][...] = s
            else:
                mul = scale if seg == 0 else 1.0
                for h in range(NH):
                    sl = slice(h * HD, (h + 1) * HD)
                    sh = s[:, sl]
                    r = lax.rsqrt(_rowsum(sh * sh) + EPS)
                    outs[seg][:, sl] = sh * (r * mul)
        zab = zab_ref[...]
        lane = lax.broadcasted_iota(jnp.int32, zab.shape, 1)
        g = -jnp.exp(al_ref[...]) * _softplus(zab + dt_ref[...])
        ri = lax.broadcasted_iota(jnp.int32, (CH, CH), 0)
        ci = lax.broadcasted_iota(jnp.int32, (CH, CH), 1)
        incl = (ri >= ci).astype(F32)
        gcum = jnp.concatenate([_dot(incl, g[c * CH:(c + 1) * CH], HI) for c in range(ts // CH)], axis=0)
        gb_ref[...] = jnp.where(lane < NH, gcum, jnp.where(lane < 2 * NH, _sigmoid(zab), 0.0))

    row = pl.BlockSpec((ts, D), lambda i: (i, 0))
    vec = pl.BlockSpec((1, 128), lambda i: (0, 0))
    return _call(
        body, name, (S // ts,),
        [pl.BlockSpec((ts, 3 * D), lambda i: (i, 0)),
         pl.BlockSpec((8, 3 * D), lambda i: (jnp.maximum(i * (ts // 8) - 1, 0), 0)),
         pl.BlockSpec((ts, 128), lambda i: (i, Z_AB // 128)),
         _full((4, 3 * D)), vec, vec],
        [row, row, row, pl.BlockSpec((ts, 128), lambda i: (i, 0))],
        [jax.ShapeDtypeStruct((S, D), F32)] * 3 + [jax.ShapeDtypeStruct((S, 128), F32)],
        [pltpu.VMEM((ts + 8, 3 * D), F32)], comm=comm,
    )(z, z, z, conv_w, alog_row, dtb_row)


def _split(a):
    hi = a.astype(MXU)
    return hi, (a - hi.astype(F32)).astype(MXU)


def _dot3(a, b, dot=_dot):
    ah, al = _split(a)
    bh, bl = _split(b)
    return dot(ah, bh) + (dot(ah, bl) + dot(al, bh))


IROWS = 2048
ICH = IROWS // CH


def _chunk_common(gbk, h, k):
    lane = lax.broadcasted_iota(jnp.int32, gbk.shape, 1)
    G = _rowsum(jnp.where(lane == h, gbk, 0.0))
    b_col = _rowsum(jnp.where(lane == h + NH, gbk, 0.0))
    ri = lax.broadcasted_iota(jnp.int32, (CH, CH), 0)
    ci = lax.broadcasted_iota(jnp.int32, (CH, CH), 1)
    incl = ri >= ci
    gc = jnp.broadcast_to(G, (CH, CH))
    decay = jnp.where(incl, jnp.exp(jnp.where(incl, gc - gc.T, 0.0)), 0.0)
    Gl = G[CH - 1:CH, :]
    kb = k * b_col
    return dict(b=b_col, ri=ri, ci=ci, incl=incl, strict=ri > ci, decay=decay, eG=jnp.exp(G),
                e2=jnp.exp(Gl - G), cd=jnp.exp(Gl), kb=kb, kk=_dot_nt(_mx(kb), _mx(k)))


def _gdn_intra_fwd(qn, kn, v, gb, name, comm=None):
    S = qn.shape[0]

    def body(q_ref, k_ref, v_ref, gb_ref, u_ref, w_ref, qd_ref, kd_ref, at_ref, t_ref, cd_ref):
        h = pl.program_id(1)
        rows = [slice(c * CH, (c + 1) * CH) for c in range(ICH)]
        ks = [k_ref[r, :] for r in rows]
        cms = [_chunk_common(gb_ref[r, :], h, k) for r, k in zip(rows, ks)]
        ps = [jnp.where(cm["strict"], cm["kk"] * cm["decay"], 0.0) for cm in cms]
        ts = [(cm["ri"] == cm["ci"]).astype(F32) - p for cm, p in zip(cms, ps)]
        for _ in range(5):
            ps = [_dot3(p, p) for p in ps]
            ts = [t + _dot3(t, p) for t, p in zip(ts, ps)]
        for c, (r, k, cm, t) in enumerate(zip(rows, ks, cms, ts)):
            rhs = jnp.concatenate([v_ref[r, :] * cm["b"], k * (cm["b"] * cm["eG"])], axis=1)
            sol = _dot3(t, rhs)
            u_ref[0, r, :] = sol[:, :HD]
            w_ref[0, r, :] = sol[:, HD:]
            t_ref[0, r, :] = t
        for c, (r, k, cm) in enumerate(zip(rows, ks, cms)):
            q = q_ref[r, :]
            qk = _dot_nt(_mx(q), _mx(k))
            at_ref[0, r, :] = jnp.where(cm["incl"], qk * cm["decay"], 0.0)
            qd_ref[0, r, :] = q * cm["eG"]
            kd_ref[0, r, :] = k * cm["e2"]
            cd_ref[0, c] = jnp.broadcast_to(cm["cd"], (8, 128))

    tok = pl.BlockSpec((IROWS, HD), lambda i, h: (i, h))
    hm = pl.BlockSpec((1, IROWS, HD), lambda i, h: (h, i, 0))
    hm64 = pl.BlockSpec((1, IROWS, CH), lambda i, h: (h, i, 0))
    big = jax.ShapeDtypeStruct((NH, S, HD), F32)
    sm = jax.ShapeDtypeStruct((NH, S, CH), F32)
    return _call(
        body, name, (S // IROWS, NH),
        [tok, tok, tok, pl.BlockSpec((IROWS, 128), lambda i, h: (i, 0))],
        [hm, hm, hm, hm, hm64, hm64, pl.BlockSpec((1, ICH, 8, 128), lambda i, h: (h, i, 0, 0))],
        [big, big, big, big, sm, sm, jax.ShapeDtypeStruct((NH, S // CH, 8, 128), F32)], comm=comm,
    )(qn, kn, v, gb)


def _scale_state(s, cd_tile):
    return (s.reshape(HD // 8, 8, HD) * cd_tile[None]).reshape(HD, HD)


def _gdn_scan_fwd(u, w, qd, kd, attn, cdt, name):
    S = u.shape[1]
    nblk = S // GROWS

    def body(u_ref, w_ref, qd_ref, kd_ref, at_ref, cd_ref, o_ref, vn_ref, st_ref, s_ref):
        i = pl.program_id(0)

        @pl.when(i == 0)
        def _():
            s_ref[...] = jnp.zeros_like(s_ref)

        def chunk(c, carry):
            r0 = pl.multiple_of(c * CH, CH)
            rows = pl.ds(r0, CH)
            H = range(NH)
            shs = [s_ref[h] for h in H]
            sbs = [_mx(sh) for sh in shs]
            ws = [_dot(_mx(w_ref[h, rows, :]), sbs[h]) for h in H]
            qs = [_dot(_mx(qd_ref[h, rows, :]), sbs[h]) for h in H]
            vns = [u_ref[h, rows, :] - ws[h] for h in H]
            vbs = [_mx(vn) for vn in vns]
            avs = [_dot(_mx(at_ref[h, rows, :]), vbs[h]) for h in H]
            kvs = [_dot_tn(_mx(kd_ref[h, rows, :]), vbs[h]) for h in H]
            for h in H:
                st_ref[h, c] = shs[h]
                vn_ref[h, rows, :] = vns[h]
                o_ref[h, rows, :] = qs[h] + avs[h]
                s_ref[h] = _scale_state(shs[h], cd_ref[h, c]) + kvs[h]
            return carry

        lax.fori_loop(0, GCH, chunk, 0)

    hm = pl.BlockSpec((NH, GROWS, HD), lambda i: (0, i, 0))
    hm64 = pl.BlockSpec((NH, GROWS, CH), lambda i: (0, i, 0))
    big = jax.ShapeDtypeStruct((NH, S, HD), F32)
    return _call(
        body, name, (nblk,),
        [hm, hm, hm, hm, hm64, pl.BlockSpec((NH, GCH, 8, 128), lambda i: (0, i, 0, 0))],
        [hm, hm, pl.BlockSpec((NH, GCH, HD, HD), lambda i: (0, i, 0, 0))],
        [big, big, jax.ShapeDtypeStruct((NH, S // CH, HD, HD), F32)],
        [pltpu.VMEM((NH, HD, HD), F32)],
    )(u, w, qd, kd, attn, cdt)


def _gdn_scan_bwd(do, w, qd, kd, attn, cdt, vn, st, name):
    S = do.shape[1]
    nblk = S // GROWS

    def body(do_ref, w_ref, qd_ref, kd_ref, at_ref, cd_ref, vn_ref, st_ref,
             dqd_ref, dkd_ref, dvn_ref, dw_ref, dat_ref, dcd_ref, ds_ref):
        i = pl.program_id(0)

        @pl.when(i == 0)
        def _():
            ds_ref[...] = jnp.zeros_like(ds_ref)

        def chunk(cc, carry):
            c = GCH - 1 - cc
            r0 = pl.multiple_of(c * CH, CH)
            rows = pl.ds(r0, CH)
            H = range(NH)
            dsps = [ds_ref[h] for h in H]
            shs = [st_ref[h, c] for h in H]
            dsbs = [_mx(a) for a in dsps]
            sbs = [_mx(a) for a in shs]
            dobs = [_mx(do_ref[h, rows, :]) for h in H]
            vbs = [_mx(vn_ref[h, rows, :]) for h in H]
            dvns = [_dot(_mx(kd_ref[h, rows, :]), dsbs[h]) + _dot_tn(_mx(at_ref[h, rows, :]), dobs[h]) for h in H]
            dvbs = [_mx(a) for a in dvns]
            dqds = [_dot_nt(dobs[h], sbs[h]) for h in H]
            dats = [_dot_nt(dobs[h], vbs[h]) for h in H]
            dkds = [_dot_nt(vbs[h], dsbs[h]) for h in H]
            dws = [_dot_nt(dvbs[h], sbs[h]) for h in H]
            qdos = [_dot_tn(_mx(qd_ref[h, rows, :]), dobs[h]) for h in H]
            wdvs = [_dot_tn(_mx(w_ref[h, rows, :]), dvbs[h]) for h in H]
            for h in H:
                dvn_ref[h, rows, :] = dvns[h]
                dqd_ref[h, rows, :] = dqds[h]
                dat_ref[h, rows, :] = dats[h]
                dkd_ref[h, rows, :] = dkds[h]
                dw_ref[h, rows, :] = -dws[h]
                dcd = jnp.sum(_rowsum(dsps[h] * shs[h]), axis=0, keepdims=True)
                dcd_ref[h, c] = jnp.broadcast_to(dcd, (8, 128))
                ds_ref[h] = _scale_state(dsps[h], cd_ref[h, c]) + qdos[h] - wdvs[h]
            return carry

        lax.fori_loop(0, GCH, chunk, 0)

    hm = pl.BlockSpec((NH, GROWS, HD), lambda i: (0, nblk - 1 - i, 0))
    hm64 = pl.BlockSpec((NH, GROWS, CH), lambda i: (0, nblk - 1 - i, 0))
    tile = pl.BlockSpec((NH, GCH, 8, 128), lambda i: (0, nblk - 1 - i, 0, 0))
    big = jax.ShapeDtypeStruct((NH, S, HD), F32)
    return _call(
        body, name, (nblk,),
        [hm, hm, hm, hm, hm64, tile, hm, pl.BlockSpec((NH, GCH, HD, HD), lambda i: (0, nblk - 1 - i, 0, 0))],
        [hm, hm, hm, hm, hm64, tile],
        [big, big, big, big, jax.ShapeDtypeStruct((NH, S, CH), F32),
         jax.ShapeDtypeStruct((NH, S // CH, 8, 128), F32)],
        [pltpu.VMEM((NH, HD, HD), F32)],
    )(do, w, qd, kd, attn, cdt, vn, st)


def _gdn_intra_bwd(qn, kn, v, gb, u, w, tmat, dqd, dkd, du, dw, dattn, dcdt, name):
    S = qn.shape[0]

    def body(q_ref, k_ref, v_ref, gb_ref, u_ref, w_ref, t_ref, dqd_ref, dkd_ref, du_ref, dw_ref,
             dat_ref, dcd_ref, dq_ref, dk_ref, dv_ref, dgb_ref):
        h = pl.program_id(1)
        rows = [slice(c * CH, (c + 1) * CH) for c in range(ICH)]
        ks = [k_ref[r, :] for r in rows]
        cms = [_chunk_common(gb_ref[r, :], h, k) for r, k in zip(rows, ks)]
        sols = [jnp.concatenate([u_ref[0, r, :], w_ref[0, r, :]], axis=1) for r in rows]
        drhss = [_dot3(t_ref[0, r, :], jnp.concatenate([du_ref[0, r, :], dw_ref[0, r, :]], axis=1), _dot_tn)
                 for r in rows]
        das = [-_dot3(drhs, sol, _dot_nt) for drhs, sol in zip(drhss, sols)]
        for c, (r, k, cm, drhs, da) in enumerate(zip(rows, ks, cms, drhss, das)):
            q, vv = q_ref[r, :], v_ref[r, :]
            decay, eG, e2, b = cm["decay"], cm["eG"], cm["e2"], cm["b"]
            dru, drw = drhs[:, :HD], drhs[:, HD:]
            dv_ref[r, :] = dru * b
            s_w = _rowsum(drw * k)
            dbeta = _rowsum(dru * vv) + s_w * eG
            deg = s_w * b
            dk = drw * (b * eG)
            dkk = jnp.where(cm["strict"], da * decay, 0.0)
            ddec = jnp.where(cm["strict"], da * cm["kk"], 0.0)
            dkkb = _mx(dkk)
            dkb = _dot(dkkb, _mx(k))
            dk = dk + _dot_tn(dkkb, _mx(cm["kb"])) + dkb * b
            dbeta = dbeta + _rowsum(dkb * k)
            dat = jnp.where(cm["incl"], dat_ref[0, r, :], 0.0)
            qk = _dot_nt(_mx(q), _mx(k))
            dqk = _mx(dat * decay)
            ddec = ddec + dat * qk
            dqd = dqd_ref[0, r, :]
            dkd = dkd_ref[0, r, :]
            dq_ref[r, :] = _dot(dqk, _mx(k)) + dqd * eG
            dk_ref[r, :] = dk + _dot_tn(dqk, _mx(q)) + dkd * e2
            deg = deg + _rowsum(dqd * q)
            t2 = _rowsum(dkd * k) * e2
            dgl = jnp.sum(t2, axis=0, keepdims=True) + dcd_ref[0, c][0:1, 0:1] * cm["cd"]
            dd = ddec * decay
            dG = deg * eG - t2 + _rowsum(dd) - _rowsum(dd.T)
            rowi = lax.broadcasted_iota(jnp.int32, (CH, 1), 0)
            dG = dG + jnp.where(rowi == CH - 1, dgl, 0.0)
            lane = lax.broadcasted_iota(jnp.int32, (CH, 128), 1)
            dgb_ref[0, r, :] = jnp.where(lane == h, dG, 0.0) + jnp.where(lane == h + NH, dbeta, 0.0)

    tok = pl.BlockSpec((IROWS, HD), lambda i, h: (i, h))
    hm = pl.BlockSpec((1, IROWS, HD), lambda i, h: (h, i, 0))
    hm64 = pl.BlockSpec((1, IROWS, CH), lambda i, h: (h, i, 0))
    tile = pl.BlockSpec((1, ICH, 8, 128), lambda i, h: (h, i, 0, 0))
    tokout = jax.ShapeDtypeStruct((S, D), F32)
    return _call(
        body, name, (S // IROWS, NH),
        [tok, tok, tok, pl.BlockSpec((IROWS, 128), lambda i, h: (i, 0)), hm, hm, hm64,
         hm, hm, hm, hm, hm64, tile],
        [tok, tok, tok, pl.BlockSpec((1, IROWS, 128), lambda i, h: (h, i, 0))],
        [tokout, tokout, tokout, jax.ShapeDtypeStruct((NH, S, 128), F32)],
    )(qn, kn, v, gb, u, w, tmat, dqd, dkd, du, dw, dattn, dcdt)


def _gdn_prep_bwd(z, dqn, dkn, dv, dgb, conv_w, alog_row, dtb_row, name):
    S = z.shape[0]
    ts = GROWS
    nblk = S // ts
    scale = HD ** -0.5
    tb = ts // 8

    def body(z_ref, hp_ref, hn_ref, zab_ref, dq_ref, dqn_ref, dk_ref, dkn_ref, dv_ref, dvn_ref,
             dgb_ref, w_ref, al_ref, dt_ref, dz_ref, dzab_ref, dcw_ref, dvec_ref, buf, dybuf, dcbuf):
        i = pl.program_id(0)
        last = i == nblk - 1

        @pl.when(i == 0)
        def _():
            dcw_ref[...] = jnp.zeros_like(dcw_ref)
            dvec_ref[...] = jnp.zeros_like(dvec_ref)

        buf[0:8, :] = jnp.where(i == 0, 0.0, hp_ref[...])
        buf[8:8 + ts, :] = z_ref[...]
        buf[8 + ts:16 + ts, :] = hn_ref[...]
        rowi = lax.broadcasted_iota(jnp.int32, (ts + 8, 1), 0)
        live = jnp.logical_or(rowi < ts, jnp.logical_not(last))
        dys = ((dq_ref, dqn_ref), (dk_ref, dkn_ref), (dv_ref, dvn_ref))
        for seg in range(3):
            cs = slice(seg * D, (seg + 1) * D)
            dybuf[0:ts, :] = dys[seg][0][...]
            dybuf[ts:ts + 8, :] = dys[seg][1][...]
            taps = [buf[pl.ds(5 + j, ts + 8), cs] for j in range(4)]
            c = jnp.zeros((ts + 8, D), F32)
            for j in range(4):
                c = c + w_ref[j:j + 1, cs] * taps[j]
            sg = _sigmoid(c)
            s = c * sg
            dsilu = sg * (1.0 + c * (1.0 - sg))
            if seg == 2:
                dcbuf[...] = jnp.where(live, dybuf[...] * dsilu, 0.0)
            else:
                mul = scale if seg == 0 else 1.0
                for h in range(NH):
                    sl = slice(h * HD, (h + 1) * HD)
                    sh = s[:, sl]
                    dy = dybuf[:, sl]
                    r = lax.rsqrt(_rowsum(sh * sh) + EPS)
                    shr = sh * r
                    ds = (mul * r) * (dy - shr * _rowsum(shr * dy))
                    dcbuf[:, sl] = jnp.where(live, ds * dsilu[:, sl], 0.0)
            dx = jnp.zeros((ts, D), F32)
            for j in range(4):
                dcw_ref[j:j + 1, cs] += _colsum(dcbuf[0:ts, :] * taps[j][0:ts])
                dx = dx + w_ref[j:j + 1, cs] * dcbuf[pl.ds(3 - j, ts), :]
            dz_ref[:, cs] = dx.astype(dz_ref.dtype)
        dgbs = dgb_ref[0]
        for h in range(1, NH):
            dgbs = dgbs + dgb_ref[h]
        ri = lax.broadcasted_iota(jnp.int32, (CH, CH), 0)
        ci = lax.broadcasted_iota(jnp.int32, (CH, CH), 1)
        rev = (ci >= ri).astype(F32)
        dgrev = jnp.concatenate([_dot(rev, dgbs[c * CH:(c + 1) * CH], HI) for c in range(ts // CH)], axis=0)
        lane0 = lax.broadcasted_iota(jnp.int32, dgbs.shape, 1)
        dgbs = jnp.where(lane0 < NH, dgrev, dgbs)
        zab = zab_ref[...]
        lane = lax.broadcasted_iota(jnp.int32, zab.shape, 1)
        xx = zab + dt_ref[...]
        ea = jnp.exp(al_ref[...])
        g = -ea * _softplus(xx)
        da = dgbs * (-ea) * _sigmoid(xx)
        beta = _sigmoid(zab)
        db = dgbs * beta * (1.0 - beta)
        is_a = lane < NH
        dzab = jnp.where(is_a, da, jnp.where(lane < 2 * NH, db, 0.0))
        dzab_ref[:, 0:128] = dzab.astype(dzab_ref.dtype)
        dzab_ref[:, 128:512] = jnp.zeros((ts, 384), dzab_ref.dtype)
        dvec_ref[0:1, :] += _colsum(jnp.where(is_a, dgbs * g, 0.0))
        dvec_ref[1:2, :] += _colsum(jnp.where(is_a, da, 0.0))

    z3 = pl.BlockSpec((ts, 3 * D), lambda i: (i, 0))
    row = pl.BlockSpec((ts, D), lambda i: (i, 0))
    nxt = pl.BlockSpec((8, D), lambda i: (jnp.minimum((i + 1) * tb, S // 8 - 1), 0))
    vec = pl.BlockSpec((1, 128), lambda i: (0, 0))
    return _call(
        body, name, (nblk,),
        [z3,
         pl.BlockSpec((8, 3 * D), lambda i: (jnp.maximum(i * tb - 1, 0), 0)),
         pl.BlockSpec((8, 3 * D), lambda i: (jnp.minimum((i + 1) * tb, S // 8 - 1), 0)),
         pl.BlockSpec((ts, 128), lambda i: (i, Z_AB // 128)),
         row, nxt, row, nxt, row, nxt,
         pl.BlockSpec((NH, ts, 128), lambda i: (0, i, 0)),
         _full((4, 3 * D)), vec, vec],
        [z3, pl.BlockSpec((ts, 512), lambda i: (i, 0)), _full((8, 3 * D)), _full((8, 128))],
        [jax.ShapeDtypeStruct((S, 3 * D), MXU), jax.ShapeDtypeStruct((S, 512), MXU),
         jax.ShapeDtypeStruct((8, 3 * D), F32), jax.ShapeDtypeStruct((8, 128), F32)],
        [pltpu.VMEM((ts + 16, 3 * D), F32), pltpu.VMEM((ts + 8, D), F32), pltpu.VMEM((ts + 8, D), F32)],
    )(z, z, z, z, dqn, dqn, dkn, dkn, dv, dv, dgb, conv_w, alog_row, dtb_row)


def _bias_index():
    u = np.arange(FW)[None, :]
    s = np.arange(3)[:, None]
    return np.clip(KWIN - 1 - u - QB * s, -256, 256) + 256


def _bias_vec(rel_bias_pad, onehot, name):
    def body(rb_ref, e_ref, o_ref):
        o_ref[:, 0, :] = _dot_nt(rb_ref[...], e_ref[0], HI)

    return _call(body, name, (3,),
                 [_full((NH, 640)), pl.BlockSpec((1, FW, 640), lambda s: (s, 0, 0))],
                 pl.BlockSpec((NH, 1, FW), lambda s: (s, 0, 0)),
                 jax.ShapeDtypeStruct((3 * NH, 1, FW), F32))(rel_bias_pad, onehot)


def _att_window(i):
    return pl.multiple_of(jnp.maximum(i * QB - PAST * CH, 0), QB)


def _bias_mask(fvec, name):
    def body(f_ref, o_ref):
        i = 2 - pl.program_id(0) // NH
        ws = jnp.maximum(i * QB - PAST * CH, 0)
        fb = jnp.broadcast_to(f_ref[0], (QB, FW))
        bias = pltpu.roll(fb, FW - 255, 1, stride=1, stride_axis=0)[:, :KWIN]
        qc = (i * QB + lax.broadcasted_iota(jnp.int32, (QB, KWIN), 0)) // CH
        kc = (ws + lax.broadcasted_iota(jnp.int32, (QB, KWIN), 1)) // CH
        o_ref[0] = jnp.where((kc <= qc) & (kc >= qc - PAST), bias, MASKED)

    return _call(body, name, (3 * NH,), [pl.BlockSpec((1, 1, FW), lambda j: (j, 0, 0))],
                 pl.BlockSpec((1, QB, KWIN), lambda j: (j, 0, 0)),
                 jax.ShapeDtypeStruct((3 * NH, QB, KWIN), F32))(fvec)


def _att_scores(q_ref, k_ref, bm_ref, i):
    ws = _att_window(i)
    q = _mx(q_ref[...] * (HD ** -0.5))
    kw = _mx(k_ref[pl.ds(ws, KWIN), :])
    return q, kw, ws, _dot_nt(q, kw) + bm_ref[0]


def _att_specs(S):
    c0 = Z_ATT // HD
    q = pl.BlockSpec((QB, HD), lambda h, i: (i, c0 + h))
    k = pl.BlockSpec((S, HD), lambda h, i: (0, c0 + NH + h))
    v = pl.BlockSpec((S, HD), lambda h, i: (0, c0 + 2 * NH + h))
    bm = pl.BlockSpec((1, QB, KWIN), lambda h, i: (jnp.maximum(2 - i, 0) * NH + h, 0, 0))
    tok = pl.BlockSpec((QB, HD), lambda h, i: (i, h))
    return q, k, v, bm, tok


def _att_fwd(z, bmask, name, comm=None):
    S = z.shape[0]

    def body(q_ref, k_ref, v_ref, bm_ref, o_ref, lse_ref):
        _, _, ws, s = _att_scores(q_ref, k_ref, bm_ref, pl.program_id(1))
        m = jnp.max(s, axis=1, keepdims=True)
        p = jnp.exp(s - m)
        l = _rowsum(p)
        o_ref[...] = _dot(_mx(p), _mx(v_ref[pl.ds(ws, KWIN), :])) * (1.0 / l)
        lse_ref[...] = jnp.broadcast_to(m + jnp.log(l), (QB, HD))

    q, k, v, bm, tok = _att_specs(S)
    shp = jax.ShapeDtypeStruct((S, D), F32)
    return _call(body, name, (NH, S // QB), [q, k, v, bm], [tok, tok], [shp, shp], comm=comm)(z, z, z, bmask)


def _att_bwd(z, bmask, ob, lse, dob, name):
    S = z.shape[0]
    nq = S // QB

    def body(q_ref, k_ref, v_ref, bm_ref, o_ref, lse_ref, do_ref, dq_ref, dk_ref, dv_ref, db_ref, dk_acc, dv_acc):
        i = pl.program_id(1)
        q, kw, ws, s = _att_scores(q_ref, k_ref, bm_ref, i)
        p = jnp.exp(s - lse_ref[:, 0:1])
        do = do_ref[...]
        dob16 = _mx(do)
        dp = _dot_nt(dob16, _mx(v_ref[pl.ds(ws, KWIN), :]))
        ds = p * (dp - _rowsum(do * o_ref[...]))
        dsb = _mx(ds)
        dq_ref[...] = (_dot(dsb, kw) * (HD ** -0.5)).astype(dq_ref.dtype)

        @pl.when(i == 0)
        def _():
            dk_acc[...] = jnp.zeros_like(dk_acc)
            dv_acc[...] = jnp.zeros_like(dv_acc)

        dk_acc[pl.ds(ws, KWIN), :] += _dot_tn(dsb, q)
        dv_acc[pl.ds(ws, KWIN), :] += _dot_tn(_mx(p), dob16)

        @pl.when(i == nq - 1)
        def _():
            dk_ref[...] = dk_acc[...].astype(dk_ref.dtype)
            dv_ref[...] = dv_acc[...].astype(dv_ref.dtype)

        @pl.when(i <= 2)
        def _():
            db_ref[0] = ds

        @pl.when(i > 2)
        def _():
            db_ref[0] += ds

    q, k, v, bm, tok = _att_specs(S)
    acc = pl.BlockSpec((S, HD), lambda h, i: (0, h))
    half = jax.ShapeDtypeStruct((S, D), MXU)
    return _call(
        body, name, (NH, nq), [q, k, v, bm, tok, tok, tok], [tok, acc, acc, bm],
        [half, half, half, jax.ShapeDtypeStruct((3 * NH, QB, KWIN), F32)],
        [pltpu.VMEM((S, HD), F32), pltpu.VMEM((S, HD), F32)],
    )(z, z, z, bmask, ob, lse, dob)


def _bias_fold(dbias, onehot, name):
    def body(db_ref, e_ref, o_ref):
        j = pl.program_id(0)
        h = j % NH
        x = jnp.concatenate([db_ref[0], jnp.zeros((QB, FW - KWIN), F32)], axis=1)
        half = QB // 2
        while half >= 8:
            x = x[:half] + pltpu.roll(x[half:2 * half], FW - half, 1)
            half //= 2
        df = jnp.zeros((1, FW), F32)
        for r in range(8):
            df = df + pltpu.roll(x[r:r + 1], 255 - r, 1)
        contrib = _dot(df, e_ref[0], HI)
        rowh = lax.broadcasted_iota(jnp.int32, (NH, 640), 0)

        @pl.when(j == 0)
        def _():
            o_ref[...] = jnp.zeros_like(o_ref)

        o_ref[...] += jnp.where(rowh == h, contrib, 0.0)

    return _call(
        body, name, (3 * NH,),
        [pl.BlockSpec((1, QB, KWIN), lambda j: (j, 0, 0)),
         pl.BlockSpec((1, FW, 640), lambda j: (j // NH, 0, 0))],
        _full((NH, 640)), jax.ShapeDtypeStruct((NH, 640), F32),
    )(dbias, onehot)


ADA_SHARD = 6 * D // NDEV


def _ada_mod(c_all, w_ada, b_shard, name):
    def body(c_ref, w_ref, b_ref, o_ref):
        cv = c_ref[...]
        ca = cv * _sigmoid(cv)
        o_ref[0] = _dot(_mx(ca), _mx(w_ref[0])) + b_ref[0]

    return _call(
        body, name, (DEPTH,),
        [_full((NDEV, D)), pl.BlockSpec((1, D, ADA_SHARD), lambda l: (l, 0, 0)),
         pl.BlockSpec((1, 1, ADA_SHARD), lambda l: (l, 0, 0))],
        pl.BlockSpec((1, NDEV, ADA_SHARD), lambda l: (l, 0, 0)),
        jax.ShapeDtypeStruct((DEPTH, NDEV, ADA_SHARD), F32),
    )(c_all, w_ada, b_shard.reshape(DEPTH, 1, ADA_SHARD))


def _adam(g, w, m, v):
    m = ADAM_B1 * m + (1.0 - ADAM_B1) * g
    v = ADAM_B2 * v + (1.0 - ADAM_B2) * jnp.square(g)
    m_hat = m / (1.0 - ADAM_B1 ** ADAM_STEP)
    v_hat = v / (1.0 - ADAM_B2 ** ADAM_STEP)
    delta = -ADAM_LR * (m_hat / (jnp.sqrt(v_hat) + ADAM_EPS) + ADAM_WD * w)
    return delta, m, v


def _wada_adamw(c_all_t, dmod, w, m, v, name):
    def body(c_ref, d_ref, w_ref, m_ref, v_ref, g_ref, dl_ref, mo_ref, vo_ref):
        cv = c_ref[...]
        ca = cv * _sigmoid(cv)
        g = _dot(ca, d_ref[0], HI)
        g_ref[0] = g
        dl_ref[0], mo_ref[0], vo_ref[0] = _adam(g, w_ref[0], m_ref[0], v_ref[0])

    blk = pl.BlockSpec((1, D, ADA_SHARD), lambda l: (l, 0, 0))
    shp = jax.ShapeDtypeStruct((DEPTH, D, ADA_SHARD), F32)
    return _call(
        body, name, (DEPTH,),
        [_full((D, NDEV)), pl.BlockSpec((1, NDEV, ADA_SHARD), lambda l: (l, 0, 0)), blk, blk, blk],
        [blk] * 4, [shp] * 4,
    )(c_all_t, dmod, w, m, v)


def _adamw_reduce(parts, w, m, v, name, tr):
    L = len(parts)
    P, RL, C = parts[0].shape
    nb = RL // tr

    def body(*refs):
        p_refs = refs[:L]
        w_ref, m_ref, v_ref, g_ref, dl_ref, mo_ref, vo_ref = refs[L:]
        layer = pl.program_id(0)
        for l in range(L):
            @pl.when(layer == l)
            def _():
                g = p_refs[l][0].astype(F32)
                for k in range(1, P):
                    g = g + p_refs[l][k].astype(F32)
                g_ref[...] = g
                dl_ref[...], mo_ref[...], vo_ref[...] = _adam(g, w_ref[...], m_ref[...], v_ref[...])

    def part_spec(l):
        return pl.BlockSpec((P, tr, C), lambda ll, i: (0, jnp.where(ll == l, i, 0), 0))

    blk = pl.BlockSpec((tr, C), lambda ll, i: (ll * nb + i, 0))
    shp = jax.ShapeDtypeStruct((L * RL, C), F32)
    return _call(body, name, (L, nb), [part_spec(l) for l in range(L)] + [blk, blk, blk],
                 [blk] * 4, [shp] * 4)(*parts, w, m, v)


def _sum_parts(parts, name):
    P, R, C = parts.shape

    def body(p_ref, o_ref):
        g = p_ref[0]
        for k in range(1, P):
            g = g + p_ref[k]
        o_ref[...] = g

    return _call(body, name, (1,), [_full((P, R, C))], _full((R, C)),
                 jax.ShapeDtypeStruct((R, C), F32))(parts)


def _pack_rows(vecs, width=1024):
    flat = jnp.concatenate([a.reshape(-1) for a in vecs])
    n = flat.shape[0]
    rows = -(-n // width)
    rows = -(-rows // 8) * 8
    return jnp.pad(flat, (0, rows * width - n)).reshape(rows, width)


def _unpack_rows(packed, shapes):
    flat = packed.reshape(-1)
    out, off = [], 0
    for s in shapes:
        n = int(np.prod(s)) if len(s) else 1
        out.append(flat[off:off + n].reshape(s))
        off += n
    return out


BIG = ("w_in", "w_out", "w_ff_in", "w_ff_out")


def _z_weights(g_in):
    w = jnp.transpose(g_in, (1, 0, 2)).reshape(D, IN_W)
    return jnp.concatenate([w[:, :Z_ATT], w[:, Z_ATT + 2 * NH:], w[:, Z_ATT:Z_ATT + 2 * NH],
                            jnp.zeros((D, ZW - IN_W), MXU)], axis=-1)


def _cols_to_owners(a):
    return jnp.transpose(a.reshape(a.shape[0], NDEV, -1), (1, 0, 2))


def _rows_to_owners(a):
    return a.reshape(NDEV, -1, a.shape[1])


def _forward_layer(x, mod_l, p, shard, next_w_in):
    sh1, sc1, gt1, sh2, sc2, gt2 = [mod_l[k][None] for k in range(6)]
    P = functools.partial
    gather = lambda a: None if a is None else ("gather", a)
    h, ht = _modnorm_fwd(x, p["norm_mix"], sc1, sh1, "norm_mix_fwd")
    z, g_in = _hosted(P(_mm_nn, h, p["wz"], "in_proj", tm=2048, tn=512), gather(next_w_in))
    (qn, kn, v, gb), g_w1 = _hosted(P(_gdn_prep_fwd, z, p["conv_w"], p["alog"], p["dtb"], "gdn_prep_fwd"),
                                    gather(shard["w_ff_in"]))
    (u, w, qd, kd, attn, tmat, cdt), g_w2 = _hosted(P(_gdn_intra_fwd, qn, kn, v, gb, "gdn_intra_fwd"),
                                                    gather(shard["w_ff_out"]))
    o, vn, st = _gdn_scan_fwd(u, w, qd, kd, attn, cdt, "gdn_scan_fwd")
    (ob, lse), g_out = _hosted(P(_att_fwd, z, p["bmask"], "att_fwd"), gather(shard["w_out"]))
    wout = g_out.reshape(D, D)
    w1 = jnp.transpose(g_w1, (1, 0, 2)).reshape(D, DFF)
    w2 = g_w2.reshape(DFF, D)
    m = _merge_fwd(o, z, ob, p["gdn_norm"], "merge_fwd")
    x1, h2 = _mm_nn(m, wout, "out_proj", mode="resid_norm", res=x, gate=gt1, norm=(p["norm_mlp"], sc2, sh2))
    a, r = _mm_nn(h2, w1, "ff_in", mode="relu2")
    x2 = _mm_nn(r, w2, "ff_out", mode="resid", res=x1, gate=gt2)
    saved = dict(x=x, ht=ht, z=z, qn=qn, kn=kn, v=v, gb=gb, u=u, w=w, qd=qd, kd=kd, attn=attn,
                 tmat=tmat, cdt=cdt, o=o, vn=vn, st=st, ob=ob, lse=lse, m=m, x1=x1, h2=h2, a=a, r=r,
                 wout=wout, w1=w1, w2=w2)
    return x2, saved, g_in


def _backward_layer(dx2, mod_l, p, s, onehot):
    sh1, sc1, gt1, sh2, sc2, gt2 = [mod_l[k][None] for k in range(6)]
    P = functools.partial
    dw2, dgt2 = _mm_tn(s["r"], dx2, "ff_out_dw", gate=gt2, w=s["w2"], out_dtype=MXU)
    da, r_w2 = _hosted(P(_mm_nt, dx2, s["w2"], "ff_out_dx", gate=gt2, drelu=s["a"]), ("a2a", _rows_to_owners(dw2)))
    dw1 = _mm_tn(s["h2"], da, "ff_in_dw", out_dtype=MXU)
    dh2, r_w1 = _hosted(P(_mm_nt, da, s["w1"], "ff_in_dx"), ("a2a", _cols_to_owners(dw1)))
    dx1, dsc2, dsh2, dnmlp = _modnorm_bwd(dh2, s["x1"], p["norm_mlp"], sc2, sh2, dx2, "norm_mlp_bwd")
    dwout, dgt1 = _mm_tn(s["m"], dx1, "out_proj_dw", gate=gt1, w=s["wout"], out_dtype=MXU)
    dm, r_out = _hosted(P(_mm_nt, dx1, s["wout"], "out_proj_dx", gate=gt1), ("a2a", _rows_to_owners(dwout)))
    do, dzg, dob, dza, dzb, dgn = _merge_bwd(dm, s["o"], s["z"], s["ob"], p["gdn_norm"], "merge_bwd")
    dq_att, dk_att, dv_att, dbias = _att_bwd(s["z"], p["bmask"], s["ob"], s["lse"], dob, "att_bwd")
    drb = _bias_fold(dbias, onehot, "rel_bias_fold")[:, :513]
    dqd, dkd, dvn, dw, dattn, dcdt = _gdn_scan_bwd(do, s["w"], s["qd"], s["kd"], s["attn"], s["cdt"],
                                                   s["vn"], s["st"], "gdn_scan_bwd")
    dqn, dkn, dv, dgb = _gdn_intra_bwd(s["qn"], s["kn"], s["v"], s["gb"], s["u"], s["w"], s["tmat"],
                                       dqd, dkd, dvn, dw, dattn, dcdt, "gdn_intra_bwd")
    dzq, dzab, dcw, dvec = _gdn_prep_bwd(s["z"], dqn, dkn, dv, dgb, p["conv_w"], p["alog"], p["dtb"],
                                         "gdn_prep_bwd")
    dz = (dzq, dzg, dq_att, dk_att, dv_att, dza, dzb, dzab)
    dwz = _in_proj_dw(s["ht"], dz, "in_proj_dw")
    dw_in = jnp.concatenate([dwz[:, :Z_ATT], dwz[:, Z_AB:Z_AB + 2 * NH], dwz[:, Z_ATT:Z_AB]], axis=1)
    dh, r_in = _hosted(P(_in_proj_dx, dz, p["wz"], "in_proj_dx"), ("a2a", _cols_to_owners(dw_in)))
    dx, dsc1, dsh1, dnmix = _modnorm_bwd(dh, s["x"], p["norm_mix"], sc1, sh1, dx1, "norm_mix_bwd")
    grads = dict(norm_mix=dnmix[0], norm_mlp=dnmlp[0], conv_w=dcw[:4], a_log=dvec[0, :NH], dt_bias=dvec[1, :NH],
                 gdn_norm=dgn[0], rel_bias=drb, mod=jnp.concatenate([dsh1, dsc1, dgt1, dsh2, dsc2, dgt2], axis=1)[0])
    return dx, grads, dict(w_in=r_in, w_out=r_out, w_ff_in=r_w1, w_ff_out=r_w2)


def _bias_onehot():
    return (jnp.asarray(_bias_index())[:, :, None] == jnp.arange(640)[None, None, :]).astype(F32)


def _layer_params(l, conv_full, norm_mix, norm_mlp, a_log, dt_bias, gdn_norm, rel_bias, onehot):
    pad = lambda a: jnp.pad(a, (0, 128 - NH))[None]
    fvec = _bias_vec(jnp.pad(rel_bias[l], ((0, 0), (0, 640 - rel_bias.shape[2]))), onehot, "rel_bias_vec")
    return dict(conv_w=conv_full[l], norm_mix=norm_mix[l][None], norm_mlp=norm_mlp[l][None], alog=pad(a_log[l]),
                dtb=pad(dt_bias[l]), gdn_norm=gdn_norm[l][None], bmask=_bias_mask(fvec, "rel_bias_mask"))


def _local_step(x, target, mod, small, shards, final_norm, onehot):
    L = len(small)
    g_in = _all_gather(shards[0]["w_in"], "gather_w_in")
    saved, params = [], []
    for l in range(L):
        params.append({**small[l], "wz": _z_weights(g_in)})
        x, sv, g_in = _forward_layer(x, mod[l].reshape(6, D), params[l], shards[l],
                                     shards[l + 1]["w_in"] if l + 1 < L else None)
        saved.append(sv)
    loss, dx, dfn = _loss_head(x, target, final_norm[None], "loss_head")
    grads, recv = [None] * L, [None] * L
    for l in reversed(range(L)):
        dx, grads[l], recv[l] = _backward_layer(dx, mod[l].reshape(6, D), params[l], saved[l], onehot)
    return loss, dx, grads, dfn[0], recv


SMALL = ("b_ada", "norm_mix", "norm_mlp", "a_log", "dt_bias", "gdn_norm", "rel_bias", "final_norm")


def kernel(x, c, w_ada, b_ada, norm_mix, norm_mlp, w_in, conv_w, a_log, dt_bias, gdn_norm, rel_bias, w_out, w_ff_in, w_ff_out, final_norm, loss_target, m_w_ada, m_b_ada, m_norm_mix, m_norm_mlp, m_w_in, m_conv_w, m_a_log, m_dt_bias, m_gdn_norm, m_rel_bias, m_w_out, m_w_ff_in, m_w_ff_out, m_final_norm, v_w_ada, v_b_ada, v_norm_mix, v_norm_mlp, v_w_in, v_conv_w, v_a_log, v_dt_bias, v_gdn_norm, v_rel_bias, v_w_out, v_w_ff_in, v_w_ff_out, v_final_norm):
    W = dict(w_ada=w_ada, b_ada=b_ada, norm_mix=norm_mix, norm_mlp=norm_mlp, w_in=w_in, conv_w=conv_w,
             a_log=a_log, dt_bias=dt_bias, gdn_norm=gdn_norm, rel_bias=rel_bias, w_out=w_out,
             w_ff_in=w_ff_in, w_ff_out=w_ff_out, final_norm=final_norm)
    Mo = dict(w_ada=m_w_ada, b_ada=m_b_ada, norm_mix=m_norm_mix, norm_mlp=m_norm_mlp, w_in=m_w_in,
              conv_w=m_conv_w, a_log=m_a_log, dt_bias=m_dt_bias, gdn_norm=m_gdn_norm, rel_bias=m_rel_bias,
              w_out=m_w_out, w_ff_in=m_w_ff_in, w_ff_out=m_w_ff_out, final_norm=m_final_norm)
    Vo = dict(w_ada=v_w_ada, b_ada=v_b_ada, norm_mix=v_norm_mix, norm_mlp=v_norm_mlp, w_in=v_w_in,
              conv_w=v_conv_w, a_log=v_a_log, dt_bias=v_dt_bias, gdn_norm=v_gdn_norm, rel_bias=v_rel_bias,
              w_out=v_w_out, w_ff_in=v_w_ff_in, w_ff_out=v_w_ff_out, final_norm=v_final_norm)
    L = w_in.shape[0]
    me = _flat(_mesh_pos())
    cshard = conv_w.shape[2]

    small_in = _all_gather(_pack_rows([c, conv_w]), "gather_c_conv")
    c_all = small_in[:, 0, :]
    conv_full = small_in.reshape(NDEV, -1)[:, D:D + L * 4 * cshard].reshape(NDEV, L, 4, cshard)
    conv_full = jnp.transpose(conv_full, (1, 2, 0, 3)).reshape(L, 4, NDEV * cshard)

    b_shard = lax.dynamic_slice_in_dim(b_ada, me * ADA_SHARD, ADA_SHARD, axis=1)
    mod_all = _all_gather(_ada_mod(c_all, w_ada, b_shard, "ada_mod"), "gather_mod")
    mod = lax.dynamic_index_in_dim(mod_all, me, axis=2, keepdims=False)
    mod = jnp.transpose(mod, (1, 0, 2)).reshape(L, 6 * D)

    onehot = _bias_onehot()
    small = [_layer_params(l, conv_full, norm_mix, norm_mlp, a_log, dt_bias, gdn_norm, rel_bias, onehot)
             for l in range(L)]
    shards = [{n: W[n][l].astype(MXU) for n in BIG} for l in range(L)]
    loss, dx, grads, dfn, recv = _local_step(x[0], loss_target[0], mod, small, shards, final_norm, onehot)

    def stack(name):
        return jnp.stack([g[name] for g in grads])

    small_names = ("mod", "norm_mix", "norm_mlp", "a_log", "dt_bias", "gdn_norm", "rel_bias")
    small_parts = [stack(n) for n in small_names] + [dfn, stack("conv_w"), loss[0, 0:1]]
    small_shapes = [a.shape for a in small_parts]
    gathered = _all_gather(_pack_rows(small_parts), "gather_small_grads")
    total = _unpack_rows(_sum_parts(gathered, "sum_small_grads"), small_shapes)
    tot = dict(zip(small_names + ("final_norm", "conv_w", "loss"), total))
    tot["b_ada"] = tot.pop("mod")
    tot["conv_w"] = lax.dynamic_slice_in_dim(tot["conv_w"], me * cshard, cshard, axis=2)

    out_g, out_d, out_m, out_v = {}, {}, {}, {}
    names = SMALL + ("conv_w",)
    shapes = [W[n].shape for n in names]
    packed = [_pack_rows([src[n] for n in names]) for src in (tot, W, Mo, Vo)]
    res = _adamw_reduce([packed[0][None]], *packed[1:], "adamw_small", tr=8)
    for dst, arr in zip((out_g, out_d, out_m, out_v), res):
        dst.update(zip(names, _unpack_rows(arr, shapes)))

    dmod_all = gathered.reshape(NDEV, -1)[:, :L * 6 * D].reshape(NDEV, L, 6 * D)
    dmod_mine = jnp.transpose(lax.dynamic_slice_in_dim(dmod_all, me * ADA_SHARD, ADA_SHARD, axis=2), (1, 0, 2))
    res = _wada_adamw(jnp.transpose(c_all), dmod_mine, w_ada, m_w_ada, v_w_ada, "adamw_w_ada")
    for dst, arr in zip((out_g, out_d, out_m, out_v), res):
        dst["w_ada"] = arr

    for name, tr in (("w_in", 128), ("w_out", 128), ("w_ff_in", 256), ("w_ff_out", 128)):
        sh = W[name].shape
        rows = int(np.prod(sh[:-1]))
        flat = lambda a: a.reshape(rows, sh[-1])
        parts = [recv[l][name] for l in range(L)]
        res = _adamw_reduce(parts, flat(W[name]), flat(Mo[name]), flat(Vo[name]), "adamw_" + name, tr=tr)
        for dst, arr in zip((out_g, out_d, out_m, out_v), res):
            dst[name] = arr.reshape(sh)

    order = ("w_ada", "b_ada", "norm_mix", "norm_mlp", "w_in", "conv_w", "a_log", "dt_bias", "gdn_norm",
             "rel_bias", "w_out", "w_ff_in", "w_ff_out", "final_norm")
    return (tot["loss"].reshape(()), dx[None], *[out_g[n] for n in order], *[out_d[n] for n in order],
            *[out_m[n] for n in order], *[out_v[n] for n in order])
```
